```python
import math
import jax
import jax.numpy as jnp
from jax import lax
import numpy as np

D_MODEL = 2048
BATCH = 8
SEQ = 8192
DEPTH = 1

SSD_EXPAND = 2
SSD_D_INNER = SSD_EXPAND * D_MODEL
SSD_HEAD_DIM = 64
SSD_N_HEADS = SSD_D_INNER // SSD_HEAD_DIM
SSD_N_GROUPS = 8
SSD_D_STATE = 128
SSD_CONV_WIDTH = 4
SSD_CHUNK = 128
SSD_XBC = SSD_D_INNER + 2 * SSD_N_GROUPS * SSD_D_STATE

ATTN_HEAD_DIM = 64
ATTN_N_Q_HEADS = D_MODEL // ATTN_HEAD_DIM
ATTN_N_KV_HEADS = ATTN_N_Q_HEADS // 8
ATTN_Q_WIDTH = ATTN_N_Q_HEADS * ATTN_HEAD_DIM
ATTN_KV_WIDTH = ATTN_N_KV_HEADS * ATTN_HEAD_DIM
ATTN_WINDOW = 128
ROPE_THETA = 10000.0

FFN_HIDDEN = 5632

DEEPNORM_ALPHA = (2 * DEPTH) ** 0.25
DEEPNORM_BETA = (8 * DEPTH) ** -0.25
LN_EPS = 1e-5
RMS_EPS = 1e-5

PROJ_SIZES = (SSD_D_INNER, SSD_XBC, SSD_N_HEADS, ATTN_Q_WIDTH, ATTN_KV_WIDTH, ATTN_KV_WIDTH, D_MODEL, D_MODEL)
PROJ_SPLITS = tuple(int(s) for s in np.cumsum(PROJ_SIZES)[:-1])
PROJ_WIDTH = int(sum(PROJ_SIZES))

kernel_name = 'hybrid_ssd_swa_sink_macaron_deepnorm'


def layer_norm(x, g, b):
    xf = x.astype(jnp.float32)
    mu = jnp.mean(xf, axis=-1, keepdims=True)
    xc = xf - mu
    var = jnp.mean(xc * xc, axis=-1, keepdims=True)
    return (xc * lax.rsqrt(var + LN_EPS) * g + b).astype(x.dtype)


def swiglu(x, w_gate, w_up, w_down):
    return (jax.nn.silu(x @ w_gate) * (x @ w_up)) @ w_down


def causal_depthwise_conv(u, w, b):
    k_width, ch = w.shape
    y = lax.conv_general_dilated(
        u, w[:, None, :], window_strides=(1,), padding=[(k_width - 1, 0)],
        dimension_numbers=('NWC', 'WIO', 'NWC'), feature_group_count=ch)
    return y + b


def ssd_chunked(x, dt, a, bmat, cmat):
    bsz, t_len, n_heads, p_dim = x.shape
    n_groups, n_state = bmat.shape[2], bmat.shape[3]
    r = n_heads // n_groups
    q_len = SSD_CHUNK
    n_chunks = t_len // q_len
    xd = (x * dt[..., None]).reshape(bsz, n_chunks, q_len, n_groups, r, p_dim)
    adt = (dt * a).reshape(bsz, n_chunks, q_len, n_groups, r).transpose(0, 1, 3, 4, 2)
    bc = bmat.reshape(bsz, n_chunks, q_len, n_groups, n_state)
    cc = cmat.reshape(bsz, n_chunks, q_len, n_groups, n_state)
    a_cs = jnp.cumsum(adt, axis=-1)
    causal = jnp.tril(jnp.ones((q_len, q_len), dtype=bool))
    seg = jnp.exp(jnp.where(causal, a_cs[..., :, None] - a_cs[..., None, :], -jnp.inf))
    cb = jnp.einsum('bclgn,bcsgn->bcgls', cc, bc)
    y_diag = jnp.einsum('bcgrls,bcsgrp->bclgrp', cb[:, :, :, None] * seg, xd)
    decay_to_end = jnp.exp(a_cs[..., -1:] - a_cs).transpose(0, 1, 4, 2, 3)
    states = jnp.einsum('bclgn,bclgrp->bcgrpn', bc, xd * decay_to_end[..., None])
    chunk_decay = jnp.exp(a_cs[..., -1])

    def step(h, inp):
        s_c, d_c = inp
        return h * d_c[..., None, None] + s_c, h

    h0 = jnp.zeros((bsz, n_groups, r, p_dim, n_state), dtype=states.dtype)
    _, prev = lax.scan(step, h0, (jnp.moveaxis(states, 1, 0), jnp.moveaxis(chunk_decay, 1, 0)))
    prev = jnp.moveaxis(prev, 0, 1)
    decay_in = jnp.exp(a_cs).transpose(0, 1, 4, 2, 3)
    y_off = jnp.einsum('bclgn,bcgrpn->bclgrp', cc, prev) * decay_in[..., None]
    return (y_diag + y_off).reshape(bsz, t_len, n_heads, p_dim)


def ssd_branch(z, xbc, dt_raw, conv_w, conv_b, dt_bias, a_log, d_skip, norm_g, w_o):
    bsz, t_len, _ = z.shape
    xbc = jax.nn.silu(causal_depthwise_conv(xbc, conv_w, conv_b))
    gn = SSD_N_GROUPS * SSD_D_STATE
    xs = xbc[..., :SSD_D_INNER].reshape(bsz, t_len, SSD_N_HEADS, SSD_HEAD_DIM).astype(jnp.float32)
    bm = xbc[..., SSD_D_INNER:SSD_D_INNER + gn].reshape(bsz, t_len, SSD_N_GROUPS, SSD_D_STATE).astype(jnp.float32)
    cm = xbc[..., SSD_D_INNER + gn:].reshape(bsz, t_len, SSD_N_GROUPS, SSD_D_STATE).astype(jnp.float32)
    dt = jax.nn.softplus(dt_raw.astype(jnp.float32) + dt_bias.astype(jnp.float32))
    a = -jnp.exp(a_log.astype(jnp.float32))
    y = ssd_chunked(xs, dt, a, bm, cm) + d_skip.astype(jnp.float32)[:, None] * xs
    y = y.reshape(bsz, t_len, SSD_D_INNER) * jax.nn.silu(z.astype(jnp.float32))
    yg = y.reshape(bsz, t_len, SSD_N_GROUPS, SSD_D_INNER // SSD_N_GROUPS)
    yg = yg * lax.rsqrt(jnp.mean(yg * yg, axis=-1, keepdims=True) + RMS_EPS)
    y = yg.reshape(bsz, t_len, SSD_D_INNER) * norm_g.astype(jnp.float32)
    return y.astype(z.dtype) @ w_o


def rope(u, positions):
    dh = u.shape[-1]
    half = dh // 2
    inv_freq = ROPE_THETA ** (-jnp.arange(half, dtype=jnp.float32) * 2.0 / dh)
    ang = positions.astype(jnp.float32)[..., None] * inv_freq
    cos = jnp.cos(ang)[:, :, None, :]
    sin = jnp.sin(ang)[:, :, None, :]
    uf = u.astype(jnp.float32)
    u1, u2 = uf[..., :half], uf[..., half:]
    return jnp.concatenate([u1 * cos - u2 * sin, u2 * cos + u1 * sin], axis=-1).astype(u.dtype)


def sliding_window_attention(q, k, v, sinks):
    bsz, t_len, hq, dh = q.shape
    hkv = k.shape[2]
    g = hq // hkv
    w = ATTN_WINDOW
    nb = t_len // w
    qb = q.reshape(bsz, nb, w, hkv, g, dh).astype(jnp.float32)
    kb = k.reshape(bsz, nb, w, hkv, dh).astype(jnp.float32)
    vb = v.reshape(bsz, nb, w, hkv, dh).astype(jnp.float32)

    def with_prev(u):
        prev = jnp.pad(u[:, :-1], ((0, 0), (1, 0), (0, 0), (0, 0), (0, 0)))
        return jnp.concatenate([prev, u], axis=2)

    kk = with_prev(kb)
    vv = with_prev(vb)
    s = jnp.einsum('bnqhgd,bnkhd->bnhgqk', qb, kk) * (dh ** -0.5)
    qi = jnp.arange(w)[:, None]
    kj = jnp.arange(2 * w)[None, :]
    band = (kj > qi) & (kj <= qi + w)
    valid_prev = (jnp.arange(nb)[:, None, None] > 0) | (kj >= w)[None]
    mask = band[None] & valid_prev
    s = jnp.where(mask[None, :, None, None], s, -jnp.inf)
    sink = sinks.astype(jnp.float32).reshape(hkv, g)[None, None, :, :, None, None]
    m = jnp.maximum(jnp.max(s, axis=-1, keepdims=True), sink)
    p = jnp.exp(s - m)
    denom = jnp.sum(p, axis=-1, keepdims=True) + jnp.exp(sink - m)
    o = jnp.einsum('bnhgqk,bnkhd->bnqhgd', p / denom, vv)
    return o.reshape(bsz, t_len, hq * dh).astype(q.dtype)


def hybrid_mixer(h, positions, w_in, conv_w, conv_b, dt_bias, a_log, d_skip, ssd_norm_g,
                 w_ssd_o, attn_sinks, w_attn_o, w_out):
    bsz, t_len, _ = h.shape
    proj = h @ w_in
    z, xbc, dt_raw, q, k, v, gate_s, gate_a = jnp.split(proj, PROJ_SPLITS, axis=-1)
    y_s = ssd_branch(z, xbc, dt_raw, conv_w, conv_b, dt_bias, a_log, d_skip, ssd_norm_g, w_ssd_o)
    q = rope(q.reshape(bsz, t_len, ATTN_N_Q_HEADS, ATTN_HEAD_DIM), positions)
    k = rope(k.reshape(bsz, t_len, ATTN_N_KV_HEADS, ATTN_HEAD_DIM), positions)
    v = v.reshape(bsz, t_len, ATTN_N_KV_HEADS, ATTN_HEAD_DIM)
    y_a = sliding_window_attention(q, k, v, attn_sinks) @ w_attn_o
    merged = jax.nn.sigmoid(gate_s) * y_s + jax.nn.sigmoid(gate_a) * y_a
    return merged @ w_out


def _fwd_setup_inputs(seed: int = 0) -> dict:
    key = jax.random.key(seed)
    ks = jax.random.split(key, 24)
    f32 = jnp.float32
    nl = DEPTH

    def nrm(k, shape, scale):
        return jax.random.normal(k, shape, f32) * scale

    x = nrm(ks[0], (BATCH, SEQ, D_MODEL), 1.0)
    positions = jnp.broadcast_to(jnp.arange(SEQ, dtype=jnp.int32), (BATCH, SEQ))
    ffn1_w_gate = nrm(ks[1], (nl, D_MODEL, FFN_HIDDEN), D_MODEL ** -0.5)
    ffn1_w_up = nrm(ks[2], (nl, D_MODEL, FFN_HIDDEN), D_MODEL ** -0.5)
    ffn1_w_down = nrm(ks[3], (nl, FFN_HIDDEN, D_MODEL), DEEPNORM_BETA * FFN_HIDDEN ** -0.5)
    ln1_g = 1.0 + nrm(ks[4], (nl, D_MODEL), 0.02)
    ln1_b = nrm(ks[5], (nl, D_MODEL), 0.02)
    w_in = nrm(ks[6], (nl, D_MODEL, PROJ_WIDTH), D_MODEL ** -0.5)
    conv_w = nrm(ks[7], (nl, SSD_CONV_WIDTH, SSD_XBC), SSD_CONV_WIDTH ** -0.5)
    conv_b = nrm(ks[8], (nl, SSD_XBC), 0.02)
    dt0 = jnp.exp(jax.random.uniform(ks[9], (nl, SSD_N_HEADS), f32, math.log(1e-3), math.log(1e-1)))
    dt_bias = dt0 + jnp.log(-jnp.expm1(-dt0))
    a_log = jnp.log(jax.random.uniform(ks[10], (nl, SSD_N_HEADS), f32, 1.0, 16.0))
    d_skip = 1.0 + nrm(ks[11], (nl, SSD_N_HEADS), 0.1)
    ssd_norm_g = 1.0 + nrm(ks[12], (nl, SSD_D_INNER), 0.02)
    w_ssd_o = nrm(ks[13], (nl, SSD_D_INNER, D_MODEL), SSD_D_INNER ** -0.5)
    attn_sinks = nrm(ks[14], (nl, ATTN_N_Q_HEADS), 0.5)
    w_attn_o = nrm(ks[15], (nl, ATTN_Q_WIDTH, D_MODEL), ATTN_Q_WIDTH ** -0.5)
    w_out = nrm(ks[16], (nl, D_MODEL, D_MODEL), DEEPNORM_BETA * D_MODEL ** -0.5)
    ln2_g = 1.0 + nrm(ks[17], (nl, D_MODEL), 0.02)
    ln2_b = nrm(ks[18], (nl, D_MODEL), 0.02)
    ffn2_w_gate = nrm(ks[19], (nl, D_MODEL, FFN_HIDDEN), D_MODEL ** -0.5)
    ffn2_w_up = nrm(ks[20], (nl, D_MODEL, FFN_HIDDEN), D_MODEL ** -0.5)
    ffn2_w_down = nrm(ks[21], (nl, FFN_HIDDEN, D_MODEL), DEEPNORM_BETA * FFN_HIDDEN ** -0.5)
    ln3_g = 1.0 + nrm(ks[22], (nl, D_MODEL), 0.02)
    ln3_b = nrm(ks[23], (nl, D_MODEL), 0.02)
    return {'x': x, 'positions': positions,
            'ffn1_w_gate': ffn1_w_gate, 'ffn1_w_up': ffn1_w_up, 'ffn1_w_down': ffn1_w_down,
            'ln1_g': ln1_g, 'ln1_b': ln1_b,
            'w_in': w_in, 'conv_w': conv_w, 'conv_b': conv_b, 'dt_bias': dt_bias, 'a_log': a_log,
            'd_skip': d_skip, 'ssd_norm_g': ssd_norm_g, 'w_ssd_o': w_ssd_o,
            'attn_sinks': attn_sinks, 'w_attn_o': w_attn_o, 'w_out': w_out,
            'ln2_g': ln2_g, 'ln2_b': ln2_b,
            'ffn2_w_gate': ffn2_w_gate, 'ffn2_w_up': ffn2_w_up, 'ffn2_w_down': ffn2_w_down,
            'ln3_g': ln3_g, 'ln3_b': ln3_b}


def _fwd_reference(x, positions, ffn1_w_gate, ffn1_w_up, ffn1_w_down, ln1_g, ln1_b,
              w_in, conv_w, conv_b, dt_bias, a_log, d_skip, ssd_norm_g, w_ssd_o,
              attn_sinks, w_attn_o, w_out, ln2_g, ln2_b,
              ffn2_w_gate, ffn2_w_up, ffn2_w_down, ln3_g, ln3_b):
    h = x
    for l in range(DEPTH):
        h = layer_norm(DEEPNORM_ALPHA * h + 0.5 * swiglu(h, ffn1_w_gate[l], ffn1_w_up[l], ffn1_w_down[l]),
                       ln1_g[l], ln1_b[l])
        mix = hybrid_mixer(h, positions, w_in[l], conv_w[l], conv_b[l], dt_bias[l], a_log[l], d_skip[l],
                           ssd_norm_g[l], w_ssd_o[l], attn_sinks[l], w_attn_o[l], w_out[l])
        h = layer_norm(DEEPNORM_ALPHA * h + mix, ln2_g[l], ln2_b[l])
        h = layer_norm(DEEPNORM_ALPHA * h + 0.5 * swiglu(h, ffn2_w_gate[l], ffn2_w_up[l], ffn2_w_down[l]),
                       ln3_g[l], ln3_b[l])
    return h


import jax as _jax
import jax.numpy as _jnp

TWIN_FORMAT = 'train_step'
FWD_PARAMS = ['x', 'positions', 'ffn1_w_gate', 'ffn1_w_up', 'ffn1_w_down', 'ln1_g', 'ln1_b', 'w_in', 'conv_w', 'conv_b', 'dt_bias', 'a_log', 'd_skip', 'ssd_norm_g', 'w_ssd_o', 'attn_sinks', 'w_attn_o', 'w_out', 'ln2_g', 'ln2_b', 'ffn2_w_gate', 'ffn2_w_up', 'ffn2_w_down', 'ln3_g', 'ln3_b']
TWIN_WEIGHTS = ['ffn1_w_gate', 'ffn1_w_up', 'ffn1_w_down', 'ln1_g', 'ln1_b', 'w_in', 'conv_w', 'conv_b', 'dt_bias', 'a_log', 'd_skip', 'ssd_norm_g', 'w_ssd_o', 'attn_sinks', 'w_attn_o', 'w_out', 'ln2_g', 'ln2_b', 'ffn2_w_gate', 'ffn2_w_up', 'ffn2_w_down', 'ln3_g', 'ln3_b']
TWIN_DIFF_INPUT = 'x'
TWIN_INPUTS = ['x', 'positions', 'ffn1_w_gate', 'ffn1_w_up', 'ffn1_w_down', 'ln1_g', 'ln1_b', 'w_in', 'conv_w', 'conv_b', 'dt_bias', 'a_log', 'd_skip', 'ssd_norm_g', 'w_ssd_o', 'attn_sinks', 'w_attn_o', 'w_out', 'ln2_g', 'ln2_b', 'ffn2_w_gate', 'ffn2_w_up', 'ffn2_w_down', 'ln3_g', 'ln3_b', 'loss_target', 'm_ffn1_w_gate', 'm_ffn1_w_up', 'm_ffn1_w_down', 'm_ln1_g', 'm_ln1_b', 'm_w_in', 'm_conv_w', 'm_conv_b', 'm_dt_bias', 'm_a_log', 'm_d_skip', 'm_ssd_norm_g', 'm_w_ssd_o', 'm_attn_sinks', 'm_w_attn_o', 'm_w_out', 'm_ln2_g', 'm_ln2_b', 'm_ffn2_w_gate', 'm_ffn2_w_up', 'm_ffn2_w_down', 'm_ln3_g', 'm_ln3_b', 'v_ffn1_w_gate', 'v_ffn1_w_up', 'v_ffn1_w_down', 'v_ln1_g', 'v_ln1_b', 'v_w_in', 'v_conv_w', 'v_conv_b', 'v_dt_bias', 'v_a_log', 'v_d_skip', 'v_ssd_norm_g', 'v_w_ssd_o', 'v_attn_sinks', 'v_w_attn_o', 'v_w_out', 'v_ln2_g', 'v_ln2_b', 'v_ffn2_w_gate', 'v_ffn2_w_up', 'v_ffn2_w_down', 'v_ln3_g', 'v_ln3_b']
TWIN_OUTPUTS = ['loss', 'grad_x', 'grad_ffn1_w_gate', 'grad_ffn1_w_up', 'grad_ffn1_w_down', 'grad_ln1_g', 'grad_ln1_b', 'grad_w_in', 'grad_conv_w', 'grad_conv_b', 'grad_dt_bias', 'grad_a_log', 'grad_d_skip', 'grad_ssd_norm_g', 'grad_w_ssd_o', 'grad_attn_sinks', 'grad_w_attn_o', 'grad_w_out', 'grad_ln2_g', 'grad_ln2_b', 'grad_ffn2_w_gate', 'grad_ffn2_w_up', 'grad_ffn2_w_down', 'grad_ln3_g', 'grad_ln3_b', 'delta_ffn1_w_gate', 'delta_ffn1_w_up', 'delta_ffn1_w_down', 'delta_ln1_g', 'delta_ln1_b', 'delta_w_in', 'delta_conv_w', 'delta_conv_b', 'delta_dt_bias', 'delta_a_log', 'delta_d_skip', 'delta_ssd_norm_g', 'delta_w_ssd_o', 'delta_attn_sinks', 'delta_w_attn_o', 'delta_w_out', 'delta_ln2_g', 'delta_ln2_b', 'delta_ffn2_w_gate', 'delta_ffn2_w_up', 'delta_ffn2_w_down', 'delta_ln3_g', 'delta_ln3_b', 'new_m_ffn1_w_gate', 'new_m_ffn1_w_up', 'new_m_ffn1_w_down', 'new_m_ln1_g', 'new_m_ln1_b', 'new_m_w_in', 'new_m_conv_w', 'new_m_conv_b', 'new_m_dt_bias', 'new_m_a_log', 'new_m_d_skip', 'new_m_ssd_norm_g', 'new_m_w_ssd_o', 'new_m_attn_sinks', 'new_m_w_attn_o', 'new_m_w_out', 'new_m_ln2_g', 'new_m_ln2_b', 'new_m_ffn2_w_gate', 'new_m_ffn2_w_up', 'new_m_ffn2_w_down', 'new_m_ln3_g', 'new_m_ln3_b', 'new_v_ffn1_w_gate', 'new_v_ffn1_w_up', 'new_v_ffn1_w_down', 'new_v_ln1_g', 'new_v_ln1_b', 'new_v_w_in', 'new_v_conv_w', 'new_v_conv_b', 'new_v_dt_bias', 'new_v_a_log', 'new_v_d_skip', 'new_v_ssd_norm_g', 'new_v_w_ssd_o', 'new_v_attn_sinks', 'new_v_w_attn_o', 'new_v_w_out', 'new_v_ln2_g', 'new_v_ln2_b', 'new_v_ffn2_w_gate', 'new_v_ffn2_w_up', 'new_v_ffn2_w_down', 'new_v_ln3_g', 'new_v_ln3_b']
TWIN_LEAF_KINDS = {'loss': 'loss', 'grad_x': 'grad_x', 'grad_ffn1_w_gate': 'grad_w', 'grad_ffn1_w_up': 'grad_w', 'grad_ffn1_w_down': 'grad_w', 'grad_ln1_g': 'grad_w', 'grad_ln1_b': 'grad_w', 'grad_w_in': 'grad_w', 'grad_conv_w': 'grad_w', 'grad_conv_b': 'grad_w', 'grad_dt_bias': 'grad_w', 'grad_a_log': 'grad_w', 'grad_d_skip': 'grad_w', 'grad_ssd_norm_g': 'grad_w', 'grad_w_ssd_o': 'grad_w', 'grad_attn_sinks': 'grad_w', 'grad_w_attn_o': 'grad_w', 'grad_w_out': 'grad_w', 'grad_ln2_g': 'grad_w', 'grad_ln2_b': 'grad_w', 'grad_ffn2_w_gate': 'grad_w', 'grad_ffn2_w_up': 'grad_w', 'grad_ffn2_w_down': 'grad_w', 'grad_ln3_g': 'grad_w', 'grad_ln3_b': 'grad_w', 'delta_ffn1_w_gate': 'delta_w', 'delta_ffn1_w_up': 'delta_w', 'delta_ffn1_w_down': 'delta_w', 'delta_ln1_g': 'delta_w', 'delta_ln1_b': 'delta_w', 'delta_w_in': 'delta_w', 'delta_conv_w': 'delta_w', 'delta_conv_b': 'delta_w', 'delta_dt_bias': 'delta_w', 'delta_a_log': 'delta_w', 'delta_d_skip': 'delta_w', 'delta_ssd_norm_g': 'delta_w', 'delta_w_ssd_o': 'delta_w', 'delta_attn_sinks': 'delta_w', 'delta_w_attn_o': 'delta_w', 'delta_w_out': 'delta_w', 'delta_ln2_g': 'delta_w', 'delta_ln2_b': 'delta_w', 'delta_ffn2_w_gate': 'delta_w', 'delta_ffn2_w_up': 'delta_w', 'delta_ffn2_w_down': 'delta_w', 'delta_ln3_g': 'delta_w', 'delta_ln3_b': 'delta_w', 'new_m_ffn1_w_gate': 'new_m', 'new_m_ffn1_w_up': 'new_m', 'new_m_ffn1_w_down': 'new_m', 'new_m_ln1_g': 'new_m', 'new_m_ln1_b': 'new_m', 'new_m_w_in': 'new_m', 'new_m_conv_w': 'new_m', 'new_m_conv_b': 'new_m', 'new_m_dt_bias': 'new_m', 'new_m_a_log': 'new_m', 'new_m_d_skip': 'new_m', 'new_m_ssd_norm_g': 'new_m', 'new_m_w_ssd_o': 'new_m', 'new_m_attn_sinks': 'new_m', 'new_m_w_attn_o': 'new_m', 'new_m_w_out': 'new_m', 'new_m_ln2_g': 'new_m', 'new_m_ln2_b': 'new_m', 'new_m_ffn2_w_gate': 'new_m', 'new_m_ffn2_w_up': 'new_m', 'new_m_ffn2_w_down': 'new_m', 'new_m_ln3_g': 'new_m', 'new_m_ln3_b': 'new_m', 'new_v_ffn1_w_gate': 'new_v', 'new_v_ffn1_w_up': 'new_v', 'new_v_ffn1_w_down': 'new_v', 'new_v_ln1_g': 'new_v', 'new_v_ln1_b': 'new_v', 'new_v_w_in': 'new_v', 'new_v_conv_w': 'new_v', 'new_v_conv_b': 'new_v', 'new_v_dt_bias': 'new_v', 'new_v_a_log': 'new_v', 'new_v_d_skip': 'new_v', 'new_v_ssd_norm_g': 'new_v', 'new_v_w_ssd_o': 'new_v', 'new_v_attn_sinks': 'new_v', 'new_v_w_attn_o': 'new_v', 'new_v_w_out': 'new_v', 'new_v_ln2_g': 'new_v', 'new_v_ln2_b': 'new_v', 'new_v_ffn2_w_gate': 'new_v', 'new_v_ffn2_w_up': 'new_v', 'new_v_ffn2_w_down': 'new_v', 'new_v_ln3_g': 'new_v', 'new_v_ln3_b': 'new_v'}


def _forward(args):
    return _fwd_reference(*[args[k] for k in FWD_PARAMS])


def _output_shape():
    def fwd():
        inp = _fwd_setup_inputs(0)
        return _fwd_reference(*[inp[k] for k in FWD_PARAMS])
    out = _jax.eval_shape(fwd)
    return out.shape, out.dtype

N_MICROBATCH = 1
ADAM_LR = 0.001
ADAM_B1 = 0.9
ADAM_B2 = 0.999
ADAM_EPS = 1e-08
ADAM_WD = 0.01
ADAM_STEP = 10
PER_EXAMPLE_BATCH_AXIS = {'x': 0, 'positions': 0, 'loss_target': 0}
SHARED_INPUTS = []
_WEIGHT_DTYPES = {'ffn1_w_gate': _jnp.float32, 'ffn1_w_up': _jnp.float32, 'ffn1_w_down': _jnp.float32, 'ln1_g': _jnp.float32, 'ln1_b': _jnp.float32, 'w_in': _jnp.float32, 'conv_w': _jnp.float32, 'conv_b': _jnp.float32, 'dt_bias': _jnp.float32, 'a_log': _jnp.float32, 'd_skip': _jnp.float32, 'ssd_norm_g': _jnp.float32, 'w_ssd_o': _jnp.float32, 'attn_sinks': _jnp.float32, 'w_attn_o': _jnp.float32, 'w_out': _jnp.float32, 'ln2_g': _jnp.float32, 'ln2_b': _jnp.float32, 'ffn2_w_gate': _jnp.float32, 'ffn2_w_up': _jnp.float32, 'ffn2_w_down': _jnp.float32, 'ln3_g': _jnp.float32, 'ln3_b': _jnp.float32}
MOMENT_SCALE = {'ffn1_w_gate': 1.219400e-02, 'ffn1_w_up': 1.180334e-02, 'ffn1_w_down': 3.290512e-02, 'ln1_g': 9.928168e-01, 'ln1_b': 4.764463e-01, 'w_in': 1.817030e-02, 'conv_w': 2.025891e-02, 'conv_b': 3.538094e-02, 'dt_bias': 4.803175e-02, 'a_log': 1.382201e-01, 'd_skip': 1.272818e-01, 'ssd_norm_g': 2.389794e-02, 'w_ssd_o': 3.652610e-02, 'attn_sinks': 4.693188e-03, 'w_attn_o': 6.878344e-03, 'w_out': 6.158321e-02, 'ln2_g': 1.113743e+00, 'ln2_b': 4.917300e-01, 'ffn2_w_gate': 1.151724e-02, 'ffn2_w_up': 1.115552e-02, 'ffn2_w_down': 3.112475e-02, 'ln3_g': 3.199622e+01, 'ln3_b': 1.679474e+00}


def _to_microbatches(a, axis):
    t = _jnp.moveaxis(a, axis, 0)
    t = t.reshape((N_MICROBATCH, t.shape[0] // N_MICROBATCH) + t.shape[1:])
    return _jnp.moveaxis(t, 1, axis + 1)


def setup_inputs(seed: int = 0) -> dict:
    inp = _fwd_setup_inputs(seed)
    key = _jax.random.fold_in(_jax.random.key(seed), 7919)
    shape, _ = _output_shape()
    out = dict(inp)
    out["loss_target"] = _jax.random.normal(_jax.random.fold_in(key, 0), shape, _jnp.float32)
    for i, name in enumerate(TWIN_WEIGHTS):
        w = inp[name].astype(_jnp.float32)
        if MOMENT_SCALE is None:
            s = _jnp.sqrt(_jnp.mean(_jnp.square(w)) + 1e-30)
        else:
            s = MOMENT_SCALE[name]
        km, kv = _jax.random.split(_jax.random.fold_in(key, i + 1))
        out[name] = w
        out["m_" + name] = s * _jax.random.normal(km, w.shape, _jnp.float32)
        out["v_" + name] = (s * s) * _jax.random.uniform(kv, w.shape, _jnp.float32, 0.5, 1.5)
    if N_MICROBATCH > 1:
        for name, axis in PER_EXAMPLE_BATCH_AXIS.items():
            out[name] = _to_microbatches(out[name], axis)
    return {'x': out['x'], 'positions': out['positions'], 'ffn1_w_gate': out['ffn1_w_gate'], 'ffn1_w_up': out['ffn1_w_up'], 'ffn1_w_down': out['ffn1_w_down'], 'ln1_g': out['ln1_g'], 'ln1_b': out['ln1_b'], 'w_in': out['w_in'], 'conv_w': out['conv_w'], 'conv_b': out['conv_b'], 'dt_bias': out['dt_bias'], 'a_log': out['a_log'], 'd_skip': out['d_skip'], 'ssd_norm_g': out['ssd_norm_g'], 'w_ssd_o': out['w_ssd_o'], 'attn_sinks': out['attn_sinks'], 'w_attn_o': out['w_attn_o'], 'w_out': out['w_out'], 'ln2_g': out['ln2_g'], 'ln2_b': out['ln2_b'], 'ffn2_w_gate': out['ffn2_w_gate'], 'ffn2_w_up': out['ffn2_w_up'], 'ffn2_w_down': out['ffn2_w_down'], 'ln3_g': out['ln3_g'], 'ln3_b': out['ln3_b'], 'loss_target': out['loss_target'], 'm_ffn1_w_gate': out['m_ffn1_w_gate'], 'm_ffn1_w_up': out['m_ffn1_w_up'], 'm_ffn1_w_down': out['m_ffn1_w_down'], 'm_ln1_g': out['m_ln1_g'], 'm_ln1_b': out['m_ln1_b'], 'm_w_in': out['m_w_in'], 'm_conv_w': out['m_conv_w'], 'm_conv_b': out['m_conv_b'], 'm_dt_bias': out['m_dt_bias'], 'm_a_log': out['m_a_log'], 'm_d_skip': out['m_d_skip'], 'm_ssd_norm_g': out['m_ssd_norm_g'], 'm_w_ssd_o': out['m_w_ssd_o'], 'm_attn_sinks': out['m_attn_sinks'], 'm_w_attn_o': out['m_w_attn_o'], 'm_w_out': out['m_w_out'], 'm_ln2_g': out['m_ln2_g'], 'm_ln2_b': out['m_ln2_b'], 'm_ffn2_w_gate': out['m_ffn2_w_gate'], 'm_ffn2_w_up': out['m_ffn2_w_up'], 'm_ffn2_w_down': out['m_ffn2_w_down'], 'm_ln3_g': out['m_ln3_g'], 'm_ln3_b': out['m_ln3_b'], 'v_ffn1_w_gate': out['v_ffn1_w_gate'], 'v_ffn1_w_up': out['v_ffn1_w_up'], 'v_ffn1_w_down': out['v_ffn1_w_down'], 'v_ln1_g': out['v_ln1_g'], 'v_ln1_b': out['v_ln1_b'], 'v_w_in': out['v_w_in'], 'v_conv_w': out['v_conv_w'], 'v_conv_b': out['v_conv_b'], 'v_dt_bias': out['v_dt_bias'], 'v_a_log': out['v_a_log'], 'v_d_skip': out['v_d_skip'], 'v_ssd_norm_g': out['v_ssd_norm_g'], 'v_w_ssd_o': out['v_w_ssd_o'], 'v_attn_sinks': out['v_attn_sinks'], 'v_w_attn_o': out['v_w_attn_o'], 'v_w_out': out['v_w_out'], 'v_ln2_g': out['v_ln2_g'], 'v_ln2_b': out['v_ln2_b'], 'v_ffn2_w_gate': out['v_ffn2_w_gate'], 'v_ffn2_w_up': out['v_ffn2_w_up'], 'v_ffn2_w_down': out['v_ffn2_w_down'], 'v_ln3_g': out['v_ln3_g'], 'v_ln3_b': out['v_ln3_b']}


def _loss(weights, diff, rest, loss_target):
    with _jax.named_scope("forward"):
        args = {**rest, TWIN_DIFF_INPUT: diff, **{k: w.astype(_WEIGHT_DTYPES[k]) for k, w in weights.items()}}
        y = _forward(args)
    with _jax.named_scope("loss_head"):
        err = _jnp.square(y.astype(_jnp.float32) - loss_target)
        return 0.5 * _jnp.sum(_jnp.mean(err, axis=-1)) if err.ndim else 0.5 * err


def _adamw(w, g, m, v):
    m = ADAM_B1 * m + (1.0 - ADAM_B1) * g
    v = ADAM_B2 * v + (1.0 - ADAM_B2) * _jnp.square(g)
    m_hat = m / (1.0 - ADAM_B1 ** ADAM_STEP)
    v_hat = v / (1.0 - ADAM_B2 ** ADAM_STEP)
    delta = -ADAM_LR * (m_hat / (_jnp.sqrt(v_hat) + ADAM_EPS) + ADAM_WD * w)
    return delta, m, v


def reference(x, positions, ffn1_w_gate, ffn1_w_up, ffn1_w_down, ln1_g, ln1_b, w_in, conv_w, conv_b, dt_bias, a_log, d_skip, ssd_norm_g, w_ssd_o, attn_sinks, w_attn_o, w_out, ln2_g, ln2_b, ffn2_w_gate, ffn2_w_up, ffn2_w_down, ln3_g, ln3_b, loss_target, m_ffn1_w_gate, m_ffn1_w_up, m_ffn1_w_down, m_ln1_g, m_ln1_b, m_w_in, m_conv_w, m_conv_b, m_dt_bias, m_a_log, m_d_skip, m_ssd_norm_g, m_w_ssd_o, m_attn_sinks, m_w_attn_o, m_w_out, m_ln2_g, m_ln2_b, m_ffn2_w_gate, m_ffn2_w_up, m_ffn2_w_down, m_ln3_g, m_ln3_b, v_ffn1_w_gate, v_ffn1_w_up, v_ffn1_w_down, v_ln1_g, v_ln1_b, v_w_in, v_conv_w, v_conv_b, v_dt_bias, v_a_log, v_d_skip, v_ssd_norm_g, v_w_ssd_o, v_attn_sinks, v_w_attn_o, v_w_out, v_ln2_g, v_ln2_b, v_ffn2_w_gate, v_ffn2_w_up, v_ffn2_w_down, v_ln3_g, v_ln3_b):
    given = dict(x=x, positions=positions, ffn1_w_gate=ffn1_w_gate, ffn1_w_up=ffn1_w_up, ffn1_w_down=ffn1_w_down, ln1_g=ln1_g, ln1_b=ln1_b, w_in=w_in, conv_w=conv_w, conv_b=conv_b, dt_bias=dt_bias, a_log=a_log, d_skip=d_skip, ssd_norm_g=ssd_norm_g, w_ssd_o=w_ssd_o, attn_sinks=attn_sinks, w_attn_o=w_attn_o, w_out=w_out, ln2_g=ln2_g, ln2_b=ln2_b, ffn2_w_gate=ffn2_w_gate, ffn2_w_up=ffn2_w_up, ffn2_w_down=ffn2_w_down, ln3_g=ln3_g, ln3_b=ln3_b, loss_target=loss_target, m_ffn1_w_gate=m_ffn1_w_gate, m_ffn1_w_up=m_ffn1_w_up, m_ffn1_w_down=m_ffn1_w_down, m_ln1_g=m_ln1_g, m_ln1_b=m_ln1_b, m_w_in=m_w_in, m_conv_w=m_conv_w, m_conv_b=m_conv_b, m_dt_bias=m_dt_bias, m_a_log=m_a_log, m_d_skip=m_d_skip, m_ssd_norm_g=m_ssd_norm_g, m_w_ssd_o=m_w_ssd_o, m_attn_sinks=m_attn_sinks, m_w_attn_o=m_w_attn_o, m_w_out=m_w_out, m_ln2_g=m_ln2_g, m_ln2_b=m_ln2_b, m_ffn2_w_gate=m_ffn2_w_gate, m_ffn2_w_up=m_ffn2_w_up, m_ffn2_w_down=m_ffn2_w_down, m_ln3_g=m_ln3_g, m_ln3_b=m_ln3_b, v_ffn1_w_gate=v_ffn1_w_gate, v_ffn1_w_up=v_ffn1_w_up, v_ffn1_w_down=v_ffn1_w_down, v_ln1_g=v_ln1_g, v_ln1_b=v_ln1_b, v_w_in=v_w_in, v_conv_w=v_conv_w, v_conv_b=v_conv_b, v_dt_bias=v_dt_bias, v_a_log=v_a_log, v_d_skip=v_d_skip, v_ssd_norm_g=v_ssd_norm_g, v_w_ssd_o=v_w_ssd_o, v_attn_sinks=v_attn_sinks, v_w_attn_o=v_w_attn_o, v_w_out=v_w_out, v_ln2_g=v_ln2_g, v_ln2_b=v_ln2_b, v_ffn2_w_gate=v_ffn2_w_gate, v_ffn2_w_up=v_ffn2_w_up, v_ffn2_w_down=v_ffn2_w_down, v_ln3_g=v_ln3_g, v_ln3_b=v_ln3_b)
    weights = {n: given[n] for n in TWIN_WEIGHTS}
    shared = {n: given[n] for n in SHARED_INPUTS}
    per_example = {n: given[n] for n in ['x', 'positions']}
    grad_fn = _jax.value_and_grad(_loss, argnums=(0, 1))

    def one_microbatch(ex, loss_target):
        ex = dict(ex)
        diff = ex.pop(TWIN_DIFF_INPUT)
        return grad_fn(weights, diff, {**shared, **ex}, loss_target)

    if N_MICROBATCH == 1:
        loss, (grad_w, grad_x) = one_microbatch(per_example, given["loss_target"])
    else:
        def body(carry, xs):
            loss_sum, grad_sum = carry
            l_k, (gw_k, gx_k) = one_microbatch(xs[0], xs[1])
            with _jax.named_scope("update"):
                return (loss_sum + l_k, _jax.tree.map(_jnp.add, grad_sum, gw_k)), gx_k

        init = (_jnp.zeros((), _jnp.float32), _jax.tree.map(_jnp.zeros_like, weights))
        (loss, grad_w), grad_x = _jax.lax.scan(body, init, (per_example, given["loss_target"]))
    with _jax.named_scope("update"):
        delta_w, new_m, new_v = {}, {}, {}
        for n in TWIN_WEIGHTS:
            delta_w[n], new_m[n], new_v[n] = _adamw(weights[n], grad_w[n], given["m_" + n], given["v_" + n])
    return (loss, grad_x, *[grad_w[n] for n in TWIN_WEIGHTS], *[delta_w[n] for n in TWIN_WEIGHTS],
            *[new_m[n] for n in TWIN_WEIGHTS], *[new_v[n] for n in TWIN_WEIGHTS])
```

```python
import functools
import math

import numpy as np
import jax
import jax.numpy as jnp
from jax import lax
from jax.experimental import pallas as pl
from jax.experimental.pallas import tpu as pltpu

F32 = jnp.float32
BF16 = jnp.bfloat16
HI = lax.Precision.HIGHEST

D = 2048
FFN_H = 5632
SSD_INNER = 4096
SSD_HEADS = 64
SSD_P = 64
SSD_G = 8
SSD_R = 8
SSD_N = 128
CHUNK = 128
XBC = 6144
NQ = 32
NKV = 4
HD = 64
QW = 2048
KVW = 256
WINDOW = 128
ROPE_THETA = 10000.0
ALPHA = 2.0 ** 0.25
LN_EPS = 1e-5
RMS_EPS = 1e-5
PROJ_W = 16960
N_CHIPS = 4
SHARD_IN = PROJ_W // N_CHIPS
SHARD_H = FFN_H // N_CHIPS

SEGS = {
    "z": (0, 4096, 0),
    "xbc": (4096, 6144, 10240),
    "dt": (10240, 64, 16896),
    "q": (10304, 2048, 4096),
    "k": (12352, 256, 16384),
    "v": (12608, 256, 16640),
    "gs": (12864, 2048, 6144),
    "ga": (14912, 2048, 8192),
}
PROJ_PAD = 17024

ADAM_LR = 0.001
ADAM_B1 = 0.9
ADAM_B2 = 0.999
ADAM_EPS = 1e-08
ADAM_WD = 0.01
ADAM_STEP = 10

VMEM_CAP = 60 * 1024 * 1024


def _params(sem, vmem_bytes):
    return pltpu.CompilerParams(dimension_semantics=sem, vmem_limit_bytes=int(min(VMEM_CAP, vmem_bytes)))


def _divtile(n, cap, q=128):
    best = None
    for d in range(q, min(n, cap) + 1, q):
        if n % d == 0:
            best = d
    return n if best is None else best


def _sigmoid(x):
    return 1.0 / (1.0 + jnp.exp(-x))


def _mm(a, b, mode, out_dtype, name, add=None, add_scale=1.0, caps=(1024, 1024, 2048), n_slabs=1):
    if mode == "nn":
        (m, k), (k2, n) = a.shape, b.shape
    elif mode == "nt":
        (m, k), (n, k2) = a.shape, b.shape
    else:
        (k, m), (k2, n) = a.shape, b.shape
    assert k == k2, (a.shape, b.shape, mode)
    tm, tn, tk = _divtile(m, caps[0]), _divtile(n // n_slabs, caps[1]), _divtile(k, caps[2])
    nk = k // tk
    per_slab = n // n_slabs // tn
    dims = {"nn": ((1,), (0,)), "nt": ((1,), (1,)), "tn": ((0,), (0,))}[mode]
    has_add = add is not None

    def body(*refs):
        if has_add:
            a_ref, b_ref, add_ref, o_ref = refs[:4]
            scr = refs[4:]
        else:
            a_ref, b_ref, o_ref = refs[:3]
            add_ref = None
            scr = refs[3:]
        part = lax.dot_general(a_ref[...].astype(BF16), b_ref[...].astype(BF16), (dims, ((), ())),
                               preferred_element_type=F32)

        def finish(acc):
            if has_add:
                acc = acc + add_scale * add_ref[...].astype(F32)
            o_ref[...] = acc.astype(o_ref.dtype)

        if nk == 1:
            finish(part)
        else:
            acc_ref = scr[0]
            kk = pl.program_id(2)

            @pl.when(kk == 0)
            def _():
                acc_ref[...] = part

            @pl.when(kk > 0)
            def _():
                acc_ref[...] += part

            @pl.when(kk == nk - 1)
            def _():
                finish(acc_ref[...])

    if mode == "nn":
        a_spec = pl.BlockSpec((tm, tk), lambda i, j, kk: (i, kk))
        b_spec = pl.BlockSpec((tk, tn), lambda i, j, kk: (kk, j))
    elif mode == "nt":
        a_spec = pl.BlockSpec((tm, tk), lambda i, j, kk: (i, kk))
        b_spec = pl.BlockSpec((tn, tk), lambda i, j, kk: (j, kk))
    else:
        a_spec = pl.BlockSpec((tk, tm), lambda i, j, kk: (kk, i))
        b_spec = pl.BlockSpec((tk, tn), lambda i, j, kk: (kk, j))
    o_spec = pl.BlockSpec((tm, tn), lambda i, j, kk: (i, j))
    out_shape = jax.ShapeDtypeStruct((m, n), out_dtype)
    if n_slabs > 1:
        assert not has_add
        o_spec = pl.BlockSpec((None, tm, tn), lambda i, j, kk: (j // per_slab, i, j % per_slab))
        out_shape = jax.ShapeDtypeStruct((n_slabs, m, n // n_slabs), out_dtype)
    in_specs = [a_spec, b_spec] + ([o_spec] if has_add else [])
    args = (a, b) + ((add,) if has_add else ())
    osz = jnp.dtype(out_dtype).itemsize
    vmem = (2 * (tm * tk * a.dtype.itemsize + tk * tn * b.dtype.itemsize) + 2 * tm * tn * osz
            + (2 * tm * tn * add.dtype.itemsize if has_add else 0) + 2 * tm * tn * 4
            + 2 * (tm * tk + tk * tn) + (8 << 20))
    return pl.pallas_call(
        body, name=name, grid=(m // tm, n // tn, nk),
        in_specs=in_specs, out_specs=o_spec, out_shape=out_shape,
        scratch_shapes=[pltpu.VMEM((tm, tn), F32)] if nk > 1 else [],
        compiler_params=_params(("parallel", "parallel", "arbitrary"), vmem),
    )(*args)


def _swiglu_fwd(gu, name):
    t = gu.shape[0]
    tt = _divtile(t, 512)
    w = SHARD_H

    def body(gu_ref, a_ref):
        g = gu_ref[:, :w]
        u = gu_ref[:, w:]
        a_ref[...] = (g * _sigmoid(g) * u).astype(BF16)

    return pl.pallas_call(
        body, name=name, grid=(t // tt, N_CHIPS),
        in_specs=[pl.BlockSpec((tt, 2 * w), lambda i, j: (i, j))],
        out_specs=pl.BlockSpec((tt, w), lambda i, j: (i, j)),
        out_shape=jax.ShapeDtypeStruct((t, FFN_H), BF16),
        compiler_params=_params(("parallel", "parallel"), 40 << 20),
    )(gu)


def _swiglu_bwd(gu, da, name):
    t = gu.shape[0]
    tt = _divtile(t, 512)
    w = SHARD_H

    def body(gu_ref, da_ref, o_ref):
        g = gu_ref[:, :w]
        u = gu_ref[:, w:]
        d = da_ref[...].astype(F32)
        s = _sigmoid(g)
        o_ref[:, :w] = (d * u * (s * (1.0 + g * (1.0 - s)))).astype(BF16)
        o_ref[:, w:] = (d * (g * s)).astype(BF16)

    return pl.pallas_call(
        body, name=name, grid=(t // tt, N_CHIPS),
        in_specs=[pl.BlockSpec((tt, 2 * w), lambda i, j: (i, j)), pl.BlockSpec((tt, w), lambda i, j: (i, j))],
        out_specs=pl.BlockSpec((tt, 2 * w), lambda i, j: (i, j)),
        out_shape=jax.ShapeDtypeStruct((t, 2 * FFN_H), BF16),
        compiler_params=_params(("parallel", "parallel"), 40 << 20),
    )(gu, da)


def _ln_fwd(base, f, g, b, c, name, target=None):
    t = base.shape[0]
    tt = _divtile(t, 256)
    with_loss = target is not None

    def body(*refs):
        if with_loss:
            base_ref, f_ref, g_ref, b_ref, tg_ref, h_ref, hb_ref, xh_ref, rs_ref, dh_ref, loss_ref = refs
        else:
            base_ref, f_ref, g_ref, b_ref, h_ref, hb_ref, xh_ref, rs_ref = refs
        r = ALPHA * base_ref[...] + c * f_ref[...]
        mu = jnp.mean(r, axis=-1, keepdims=True)
        xc = r - mu
        var = jnp.mean(xc * xc, axis=-1, keepdims=True)
        rstd = lax.rsqrt(var + LN_EPS)
        xh = xc * rstd
        h = xh * g_ref[...] + b_ref[...]
        h_ref[...] = h
        hb_ref[...] = h.astype(BF16)
        xh_ref[...] = xh
        rs_ref[...] = rstd
        if with_loss:
            e = h - tg_ref[...]
            dh_ref[...] = e * (1.0 / D)
            part = 0.5 * jnp.sum(jnp.sum(e * e, axis=-1, keepdims=True) * (1.0 / D), axis=0, keepdims=True)

            @pl.when(pl.program_id(0) == 0)
            def _():
                loss_ref[...] = jnp.zeros_like(loss_ref)

            loss_ref[...] += part

    row = pl.BlockSpec((tt, D), lambda i: (i, 0))
    vec = pl.BlockSpec((1, D), lambda i: (0, 0))
    col = pl.BlockSpec((tt, 1), lambda i: (i, 0))
    in_specs = [row, row, vec, vec] + ([row] if with_loss else [])
    out_specs = [row, row, row, col] + ([row, pl.BlockSpec((1, 1), lambda i: (0, 0))] if with_loss else [])
    out_shape = [jax.ShapeDtypeStruct((t, D), F32), jax.ShapeDtypeStruct((t, D), BF16),
                 jax.ShapeDtypeStruct((t, D), F32), jax.ShapeDtypeStruct((t, 1), F32)]
    if with_loss:
        out_shape += [jax.ShapeDtypeStruct((t, D), F32), jax.ShapeDtypeStruct((1, 1), F32)]
    args = (base, f, g, b) + ((target,) if with_loss else ())
    return pl.pallas_call(
        body, name=name, grid=(t // tt,), in_specs=in_specs, out_specs=out_specs, out_shape=out_shape,
        compiler_params=_params(("arbitrary",) if with_loss else ("parallel",), 48 << 20),
    )(*args)


def _ln_bwd(dy, xh, rstd, g, c, name):
    t = dy.shape[0]
    tt = _divtile(t, 256)

    def body(dy_ref, xh_ref, rs_ref, g_ref, dr_ref, drb_ref, dg_ref, db_ref):
        dyv = dy_ref[...]
        xhv = xh_ref[...]
        dxh = dyv * g_ref[...]
        m1 = jnp.mean(dxh, axis=-1, keepdims=True)
        m2 = jnp.mean(dxh * xhv, axis=-1, keepdims=True)
        dr = rs_ref[...] * (dxh - m1 - xhv * m2)
        dr_ref[...] = dr
        drb_ref[...] = (c * dr).astype(BF16)

        @pl.when(pl.program_id(0) == 0)
        def _():
            dg_ref[...] = jnp.zeros_like(dg_ref)
            db_ref[...] = jnp.zeros_like(db_ref)

        dg_ref[...] += jnp.sum(dyv * xhv, axis=0, keepdims=True)
        db_ref[...] += jnp.sum(dyv, axis=0, keepdims=True)

    row = pl.BlockSpec((tt, D), lambda i: (i, 0))
    vec = pl.BlockSpec((1, D), lambda i: (0, 0))
    col = pl.BlockSpec((tt, 1), lambda i: (i, 0))
    return pl.pallas_call(
        body, name=name, grid=(t // tt,), in_specs=[row, row, col, vec], out_specs=[row, row, vec, vec],
        out_shape=[jax.ShapeDtypeStruct((t, D), F32), jax.ShapeDtypeStruct((t, D), BF16),
                   jax.ShapeDtypeStruct((1, D), F32), jax.ShapeDtypeStruct((1, D), F32)],
        compiler_params=_params(("arbitrary",), 40 << 20),
    )(dy, xh, rstd, g)


DT_BLK = SEGS["dt"][2] // 128


def _dt_prep(proj, bias128, alog128):
    t = proj.shape[0]
    tt = _divtile(t, 1024)

    def body(p_ref, bias_ref, alog_ref, dt_ref, adt_ref):
        dtv = jax.nn.softplus(p_ref[...] + bias_ref[...])
        dt_ref[...] = dtv
        adt_ref[...] = dtv * (-jnp.exp(alog_ref[...]))

    blk = pl.BlockSpec((tt, 128), lambda i: (i, 0))
    vec = pl.BlockSpec((1, 128), lambda i: (0, 0))
    return pl.pallas_call(
        body, name="dt_prep", grid=(t // tt,),
        in_specs=[pl.BlockSpec((tt, 128), lambda i: (i, DT_BLK)), vec, vec], out_specs=[blk, blk],
        out_shape=[jax.ShapeDtypeStruct((t, 128), F32)] * 2,
        compiler_params=_params(("parallel",), 16 << 20),
    )(proj, bias128, alog128)


def _dt_bwd(dadt, dxdx, proj, bias128, alog128):
    t = proj.shape[0]
    tt = _divtile(t, 1024)

    def body(dadt_ref, dxdx_ref, p_ref, bias_ref, alog_ref, o_ref, dbias_ref, dalog_ref):
        pre = p_ref[...] + bias_ref[...]
        dtv = jax.nn.softplus(pre)
        a = -jnp.exp(alog_ref[...])
        ddt = a * dadt_ref[...] + dxdx_ref[...]
        draw = ddt * _sigmoid(pre)
        o_ref[...] = draw.astype(BF16)

        @pl.when(pl.program_id(0) == 0)
        def _():
            dbias_ref[...] = jnp.zeros_like(dbias_ref)
            dalog_ref[...] = jnp.zeros_like(dalog_ref)

        dbias_ref[...] += jnp.sum(draw, axis=0, keepdims=True)
        dalog_ref[...] += jnp.sum(dadt_ref[...] * dtv * a, axis=0, keepdims=True)

    blk = pl.BlockSpec((tt, 128), lambda i: (i, 0))
    vec = pl.BlockSpec((1, 128), lambda i: (0, 0))
    return pl.pallas_call(
        body, name="dt_bwd", grid=(t // tt,),
        in_specs=[blk, blk, pl.BlockSpec((tt, 128), lambda i: (i, DT_BLK)), vec, vec],
        out_specs=[blk, vec, vec],
        out_shape=[jax.ShapeDtypeStruct((t, 128), BF16), jax.ShapeDtypeStruct((1, 128), F32),
                   jax.ShapeDtypeStruct((1, 128), F32)],
        compiler_params=_params(("arbitrary",), 16 << 20),
    )(dadt, dxdx, proj, bias128, alog128)


CONV_CB = 512
CONV_TT = 512


def _shift_down(cur, prev8, s):
    if s == 0:
        return cur
    rolled = pltpu.roll(cur, s, 0)
    head = pltpu.roll(prev8, s, 0)
    r8 = lax.broadcasted_iota(jnp.int32, (8, 1), 0)
    top = jnp.where(r8 < s, head, rolled[:8])
    return jnp.concatenate([top, rolled[8:]], axis=0)


def _shift_up(cur, next8, s):
    if s == 0:
        return cur
    n = cur.shape[0]
    rolled = pltpu.roll(cur, n - s, 0)
    tail = pltpu.roll(next8, 8 - s, 0)
    r8 = lax.broadcasted_iota(jnp.int32, (8, 1), 0)
    bot = jnp.where(r8 >= 8 - s, tail, rolled[n - 8:])
    return jnp.concatenate([rolled[:n - 8], bot], axis=0)


def _conv_fwd(proj, conv_w, conv_b):
    t = proj.shape[0]
    tt = _divtile(t, CONV_TT)
    base = SEGS["xbc"][2] // CONV_CB
    r8 = tt // 8

    def body(u_ref, up_ref, w_ref, b_ref, o_ref):
        cur = u_ref[...]
        prev8 = jnp.where(pl.program_id(1) > 0, up_ref[...], 0.0)
        acc = b_ref[...] + w_ref[3:4, :] * cur
        for k in range(3):
            acc = acc + w_ref[k:k + 1, :] * _shift_down(cur, prev8, 3 - k)
        o_ref[...] = acc * _sigmoid(acc)

    return pl.pallas_call(
        body, name="conv_fwd", grid=(XBC // CONV_CB, t // tt),
        in_specs=[pl.BlockSpec((tt, CONV_CB), lambda c, i: (i, base + c)),
                  pl.BlockSpec((8, CONV_CB), lambda c, i: (jnp.maximum(i * r8 - 1, 0), base + c)),
                  pl.BlockSpec((4, CONV_CB), lambda c, i: (0, c)),
                  pl.BlockSpec((1, CONV_CB), lambda c, i: (0, c))],
        out_specs=pl.BlockSpec((tt, CONV_CB), lambda c, i: (i, c)),
        out_shape=jax.ShapeDtypeStruct((t, XBC), F32),
        compiler_params=_params(("parallel", "parallel"), 24 << 20),
    )(proj, proj, conv_w, conv_b)


def _conv_bwd(proj, dout, conv_w, conv_b, col0, width, name, skip=None):
    t = proj.shape[0]
    tt = _divtile(t, CONV_TT)
    nt = t // tt
    base = SEGS["xbc"][2] // CONV_CB + col0 // CONV_CB
    wb = col0 // CONV_CB
    r8 = tt // 8
    has_skip = skip is not None

    def body(*refs):
        if has_skip:
            u_ref, up_ref, d_ref, w_ref, b_ref, sk_ref, skw_ref, du_ref, dw_ref, db_ref, nx_ref = refs
        else:
            u_ref, up_ref, d_ref, w_ref, b_ref, du_ref, dw_ref, db_ref, nx_ref = refs
        i = pl.program_id(1)
        cur = u_ref[...]
        prev8 = jnp.where(i < nt - 1, up_ref[...], 0.0)
        sh = [_shift_down(cur, prev8, 3 - k) for k in range(3)] + [cur]
        pre = b_ref[...]
        for k in range(4):
            pre = pre + w_ref[k:k + 1, :] * sh[k]
        sg = _sigmoid(pre)
        dout_v = d_ref[...]
        if has_skip:
            dout_v = dout_v + sk_ref[...] * skw_ref[...]
        dpre = dout_v * (sg * (1.0 + pre * (1.0 - sg)))

        @pl.when(i == 0)
        def _():
            nx_ref[...] = jnp.zeros_like(nx_ref)
            dw_ref[...] = jnp.zeros_like(dw_ref)
            db_ref[...] = jnp.zeros_like(db_ref)

        next8 = nx_ref[...]
        du = w_ref[3:4, :] * dpre
        for s in range(1, 4):
            du = du + w_ref[3 - s:4 - s, :] * _shift_up(dpre, next8, s)
        du_ref[...] = du.astype(BF16)
        nx_ref[...] = dpre[:8]
        rows = [jnp.sum(dpre * sh[k], axis=0, keepdims=True) for k in range(4)]
        dw_ref[...] += jnp.concatenate(rows + [jnp.zeros((4, CONV_CB), F32)], axis=0)
        db_ref[...] += jnp.sum(dpre, axis=0, keepdims=True)

    rev = lambda c, i: (nt - 1 - i, c)
    in_specs = [pl.BlockSpec((tt, CONV_CB), lambda c, i: (nt - 1 - i, base + c)),
                pl.BlockSpec((8, CONV_CB), lambda c, i: (jnp.maximum((nt - 1 - i) * r8 - 1, 0), base + c)),
                pl.BlockSpec((tt, CONV_CB), rev),
                pl.BlockSpec((4, CONV_CB), lambda c, i: (0, wb + c)),
                pl.BlockSpec((1, CONV_CB), lambda c, i: (0, wb + c))]
    args = [proj, proj, dout, conv_w, conv_b]
    if has_skip:
        in_specs += [pl.BlockSpec((tt, CONV_CB), rev), pl.BlockSpec((1, CONV_CB), lambda c, i: (0, c))]
        args += [skip[0], skip[1]]
    return pl.pallas_call(
        body, name=name, grid=(width // CONV_CB, nt),
        in_specs=in_specs,
        out_specs=[pl.BlockSpec((tt, CONV_CB), rev), pl.BlockSpec((8, CONV_CB), lambda c, i: (0, c)),
                   pl.BlockSpec((1, CONV_CB), lambda c, i: (0, c))],
        out_shape=[jax.ShapeDtypeStruct((t, width), BF16), jax.ShapeDtypeStruct((8, width), F32),
                   jax.ShapeDtypeStruct((1, width), F32)],
        scratch_shapes=[pltpu.VMEM((8, CONV_CB), F32)],
        compiler_params=_params(("parallel", "arbitrary"), 32 << 20),
    )(*args)


GW = SSD_R * SSD_P


def _expand8(v):
    r = v.shape[0]
    return jnp.concatenate([jnp.broadcast_to(v[:, h:h + 1], (r, SSD_P)) for h in range(SSD_R)], axis=1)


def _sel(rows, group):
    ri = lax.broadcasted_iota(jnp.int32, (rows, rows // group), 0)
    ci = lax.broadcasted_iota(jnp.int32, (rows, rows // group), 1)
    lo = ci * group
    return jnp.where((ri >= lo) & (ri < lo + group), 1.0, 0.0).astype(F32)


def _ssd_chunk_terms(adt):
    li = lax.broadcasted_iota(jnp.int32, (CHUNK, CHUNK), 0)
    si = lax.broadcasted_iota(jnp.int32, (CHUNK, CHUNK), 1)
    causal = li >= si
    tril = jnp.where(causal, 1.0, 0.0).astype(F32)
    a_cs = jnp.dot(tril, adt, preferred_element_type=F32, precision=HI)
    a_cs_t = lax.dot_general(adt, jnp.where(li <= si, 1.0, 0.0).astype(F32), (((0,), (0,)), ((), ())),
                             preferred_element_type=F32, precision=HI)
    return a_cs, a_cs_t, causal


def _ssd_fwd(xc, dt3, adt3):
    t = xc.shape[0]
    nc = t // CHUNK

    def body(xs_ref, b_ref, c_ref, dt_ref, adt_ref, y_ref, hp_ref, h_ref):
        @pl.when(pl.program_id(1) == 0)
        def _():
            h_ref[...] = jnp.zeros_like(h_ref)

        a_cs, a_cs_t, causal = _ssd_chunk_terms(adt_ref[0])
        a_last = a_cs[CHUNK - 1:CHUNK, :]
        h = h_ref[...]
        hp_ref[0, 0] = h
        xd = xs_ref[...] * _expand8(dt_ref[0])
        bb = b_ref[...].astype(BF16)
        cbf = c_ref[...].astype(BF16)
        cb = lax.dot_general(cbf, bb, (((1,), (1,)), ((), ())), preferred_element_type=F32)
        yoff = jnp.dot(cbf, h.astype(BF16), preferred_element_type=F32) * _expand8(jnp.exp(a_cs))
        for r in range(SSD_R):
            seg = jnp.exp(jnp.where(causal, a_cs[:, r:r + 1] - a_cs_t[r:r + 1, :], -jnp.inf))
            lmat = (cb * seg).astype(BF16)
            sl = slice(SSD_P * r, SSD_P * (r + 1))
            y_ref[:, sl] = jnp.dot(lmat, xd[:, sl].astype(BF16), preferred_element_type=F32) + yoff[:, sl]
        xdd = (xd * _expand8(jnp.exp(a_last - a_cs))).astype(BF16)
        h_ref[...] = _expand8(jnp.exp(a_last)) * h + lax.dot_general(
            bb, xdd, (((0,), (0,)), ((), ())), preferred_element_type=F32)

    nb = SSD_INNER // SSD_N
    return pl.pallas_call(
        body, name="ssd_fwd", grid=(SSD_G, nc),
        in_specs=[pl.BlockSpec((CHUNK, GW), lambda g, c: (c, g)),
                  pl.BlockSpec((CHUNK, SSD_N), lambda g, c: (c, nb + g)),
                  pl.BlockSpec((CHUNK, SSD_N), lambda g, c: (c, nb + SSD_G + g)),
                  pl.BlockSpec((1, CHUNK, SSD_R), lambda g, c: (g, c, 0)),
                  pl.BlockSpec((1, CHUNK, SSD_R), lambda g, c: (g, c, 0))],
        out_specs=[pl.BlockSpec((CHUNK, GW), lambda g, c: (c, g)),
                   pl.BlockSpec((1, 1, SSD_N, GW), lambda g, c: (g, c, 0, 0))],
        out_shape=[jax.ShapeDtypeStruct((t, SSD_INNER), F32), jax.ShapeDtypeStruct((SSD_G, nc, SSD_N, GW), F32)],
        scratch_shapes=[pltpu.VMEM((SSD_N, GW), F32)],
        compiler_params=_params(("parallel", "arbitrary"), 32 << 20),
    )(xc, xc, xc, dt3, adt3)


def _ssd_bwd(xc, dt3, adt3, hprev, dy):
    t = xc.shape[0]
    nc = t // CHUNK

    def body(xs_ref, b_ref, c_ref, dt_ref, adt_ref, hp_ref, dy_ref,
             dx_ref, db_ref, dc_ref, dadt_ref, dxdx_ref, dh_ref):
        @pl.when(pl.program_id(1) == 0)
        def _():
            dh_ref[...] = jnp.zeros_like(dh_ref)

        a_cs, a_cs_t, causal = _ssd_chunk_terms(adt_ref[0])
        a_last = a_cs[CHUNK - 1:CHUNK, :]
        e_last = jnp.exp(a_last)
        ex = _expand8(jnp.exp(a_cs))
        dtex = _expand8(jnp.exp(a_last - a_cs))
        dtx = _expand8(dt_ref[0])
        xs = xs_ref[...]
        dyv = dy_ref[...]
        hp = hp_ref[0, 0]
        dh = dh_ref[...]
        sel = _sel(GW, SSD_P)
        seg8 = lambda v: jnp.dot(v, sel, preferred_element_type=F32, precision=HI)

        xd = xs * dtx
        xdd = xd * dtex
        bb = b_ref[...].astype(BF16)
        cbf = c_ref[...].astype(BF16)
        hpb = hp.astype(BF16)
        dhb = dh.astype(BF16)
        xdb = xd.astype(BF16)
        dyb = dyv.astype(BF16)
        cb = lax.dot_general(cbf, bb, (((1,), (1,)), ((), ())), preferred_element_type=F32)
        dye = (dyv * ex).astype(BF16)
        yoff = jnp.dot(cbf, hpb, preferred_element_type=F32) * ex
        dc = lax.dot_general(dye, hpb, (((1,), (1,)), ((), ())), preferred_element_type=F32)
        d_a = seg8(dyv * yoff)
        bdh = jnp.dot(bb, dhb, preferred_element_type=F32)
        db = lax.dot_general(xdd.astype(BF16), dhb, (((1,), (1,)), ((), ())), preferred_element_type=F32)
        dxd_state = bdh * dtex
        q = seg8(xdd * bdh)
        d_a = d_a - q
        d_a_last = (jnp.sum(q, axis=0, keepdims=True)
                    + e_last * seg8(jnp.sum(hp * dh, axis=0, keepdims=True)))
        dh_ref[...] = (lax.dot_general(cbf, dye, (((0,), (0,)), ((), ())), preferred_element_type=F32)
                       + _expand8(e_last) * dh)
        dcb = jnp.zeros((CHUNK, CHUNK), F32)
        w_all = []
        dxd_parts = []
        for r in range(SSD_R):
            seg = jnp.exp(jnp.where(causal, a_cs[:, r:r + 1] - a_cs_t[r:r + 1, :], -jnp.inf))
            lmat = cb * seg
            sl = slice(SSD_P * r, SSD_P * (r + 1))
            dm = lax.dot_general(dyb[:, sl], xdb[:, sl], (((1,), (1,)), ((), ())), preferred_element_type=F32)
            dxd_parts.append(lax.dot_general(lmat.astype(BF16), dyb[:, sl], (((0,), (0,)), ((), ())),
                                             preferred_element_type=F32))
            dcb = dcb + dm * seg
            w_all.append(dm * lmat)
        row_sums = jnp.dot(jnp.concatenate(w_all, axis=1), _sel(SSD_R * CHUNK, CHUNK),
                           preferred_element_type=F32, precision=HI)
        col_sums = lax.dot_general(jnp.concatenate(w_all, axis=0), _sel(SSD_R * CHUNK, CHUNK),
                                   (((0,), (0,)), ((), ())), preferred_element_type=F32, precision=HI)
        d_a = d_a + row_sums - col_sums
        li = lax.broadcasted_iota(jnp.int32, (CHUNK, SSD_R), 0)
        d_a = d_a + jnp.where(li == CHUNK - 1, d_a_last, 0.0)
        l2 = lax.broadcasted_iota(jnp.int32, (CHUNK, CHUNK), 0)
        s2 = lax.broadcasted_iota(jnp.int32, (CHUNK, CHUNK), 1)
        dadt_ref[0] = jnp.dot(jnp.where(s2 >= l2, 1.0, 0.0).astype(F32), d_a,
                              preferred_element_type=F32, precision=HI)
        dxd = dxd_state + jnp.concatenate(dxd_parts, axis=1)
        dxdx_ref[0] = seg8(dxd * xs)
        dx_ref[...] = dxd * dtx
        dcbb = dcb.astype(BF16)
        db_ref[...] = db + lax.dot_general(dcbb, cbf, (((0,), (0,)), ((), ())), preferred_element_type=F32)
        dc_ref[...] = dc + jnp.dot(dcbb, bb, preferred_element_type=F32)

    nb = SSD_INNER // SSD_N
    rc = lambda g, c: (nc - 1 - c, g)
    r3 = lambda g, c: (g, nc - 1 - c, 0)
    return pl.pallas_call(
        body, name="ssd_bwd", grid=(SSD_G, nc),
        in_specs=[pl.BlockSpec((CHUNK, GW), rc),
                  pl.BlockSpec((CHUNK, SSD_N), lambda g, c: (nc - 1 - c, nb + g)),
                  pl.BlockSpec((CHUNK, SSD_N), lambda g, c: (nc - 1 - c, nb + SSD_G + g)),
                  pl.BlockSpec((1, CHUNK, SSD_R), r3),
                  pl.BlockSpec((1, CHUNK, SSD_R), r3),
                  pl.BlockSpec((1, 1, SSD_N, GW), lambda g, c: (g, nc - 1 - c, 0, 0)),
                  pl.BlockSpec((CHUNK, GW), rc)],
        out_specs=[pl.BlockSpec((CHUNK, GW), rc),
                   pl.BlockSpec((CHUNK, SSD_N), rc),
                   pl.BlockSpec((CHUNK, SSD_N), rc),
                   pl.BlockSpec((1, CHUNK, SSD_R), r3),
                   pl.BlockSpec((1, CHUNK, SSD_R), r3)],
        out_shape=[jax.ShapeDtypeStruct((t, SSD_INNER), F32),
                   jax.ShapeDtypeStruct((t, SSD_G * SSD_N), F32),
                   jax.ShapeDtypeStruct((t, SSD_G * SSD_N), F32),
                   jax.ShapeDtypeStruct((SSD_G, t, SSD_R), F32),
                   jax.ShapeDtypeStruct((SSD_G, t, SSD_R), F32)],
        scratch_shapes=[pltpu.VMEM((SSD_N, GW), F32)],
        compiler_params=_params(("parallel", "arbitrary"), 40 << 20),
    )(xc, xc, xc, dt3, adt3, hprev, dy)


def _gated_norm_fwd(y, xc, proj, dexp, ng):
    t = y.shape[0]
    tt = _divtile(t, 256)

    def body(y_ref, x_ref, z_ref, d_ref, g_ref, o_ref):
        z = z_ref[...]
        y2 = (y_ref[...] + d_ref[...] * x_ref[...]) * (z * _sigmoid(z))
        for gi in range(SSD_G):
            sl = slice(GW * gi, GW * (gi + 1))
            seg = y2[:, sl]
            rinv = lax.rsqrt(jnp.mean(seg * seg, axis=-1, keepdims=True) + RMS_EPS)
            o_ref[:, sl] = (seg * rinv * g_ref[:, sl]).astype(BF16)

    row = pl.BlockSpec((tt, SSD_INNER), lambda i: (i, 0))
    vec = pl.BlockSpec((1, SSD_INNER), lambda i: (0, 0))
    return pl.pallas_call(
        body, name="gated_norm_fwd", grid=(t // tt,), in_specs=[row, row, row, vec, vec], out_specs=row,
        out_shape=jax.ShapeDtypeStruct((t, SSD_INNER), BF16),
        compiler_params=_params(("parallel",), 48 << 20),
    )(y, xc, proj, dexp, ng)


def _gated_norm_bwd(dout, y, xc, proj, dexp, ng):
    t = y.shape[0]
    tt = _divtile(t, 128)

    def body(do_ref, y_ref, x_ref, z_ref, d_ref, g_ref, dz_ref, dy_ref, dg_ref, dd_ref):
        z = z_ref[...]
        sg = _sigmoid(z)
        sz = z * sg
        xs = x_ref[...]
        y1 = y_ref[...] + d_ref[...] * xs
        y2 = y1 * sz
        dov = do_ref[...]

        @pl.when(pl.program_id(0) == 0)
        def _():
            dg_ref[...] = jnp.zeros_like(dg_ref)
            dd_ref[...] = jnp.zeros_like(dd_ref)

        for gi in range(SSD_G):
            sl = slice(GW * gi, GW * (gi + 1))
            seg = y2[:, sl]
            rinv = lax.rsqrt(jnp.mean(seg * seg, axis=-1, keepdims=True) + RMS_EPS)
            yn = seg * rinv
            dsl = dov[:, sl]
            dg_ref[:, sl] += jnp.sum(dsl * yn, axis=0, keepdims=True)
            dyn = dsl * g_ref[:, sl]
            dy2 = rinv * (dyn - yn * jnp.mean(dyn * yn, axis=-1, keepdims=True))
            dz_ref[:, sl] = (dy2 * y1[:, sl] * (sg[:, sl] * (1.0 + z[:, sl] * (1.0 - sg[:, sl])))).astype(BF16)
            dy1 = dy2 * sz[:, sl]
            dy_ref[:, sl] = dy1
            dd_ref[:, sl] += jnp.sum(dy1 * xs[:, sl], axis=0, keepdims=True)

    row = pl.BlockSpec((tt, SSD_INNER), lambda i: (i, 0))
    vec = pl.BlockSpec((1, SSD_INNER), lambda i: (0, 0))
    return pl.pallas_call(
        body, name="gated_norm_bwd", grid=(t // tt,), in_specs=[row, row, row, row, vec, vec],
        out_specs=[row, row, vec, vec],
        out_shape=[jax.ShapeDtypeStruct((t, SSD_INNER), BF16), jax.ShapeDtypeStruct((t, SSD_INNER), F32),
                   jax.ShapeDtypeStruct((1, SSD_INNER), F32), jax.ShapeDtypeStruct((1, SSD_INNER), F32)],
        compiler_params=_params(("arbitrary",), 48 << 20),
    )(dout, y, xc, proj, dexp, ng)


def _fold_heads(v, name):
    def body(v_ref, o_ref):
        ri = lax.broadcasted_iota(jnp.int32, (SSD_INNER, 128), 0)
        ci = lax.broadcasted_iota(jnp.int32, (SSD_INNER, 128), 1)
        fold = jnp.where((ri >= ci * SSD_P) & (ri < (ci + 1) * SSD_P), 1.0, 0.0).astype(F32)
        o_ref[...] = jnp.dot(v_ref[...], fold, preferred_element_type=F32, precision=HI)

    return pl.pallas_call(body, name=name, out_shape=jax.ShapeDtypeStruct((1, 128), F32))(v)


Q_BLK = SEGS["q"][2] // QW
K_BLK = SEGS["k"][2] // KVW
V_BLK = SEGS["v"][2] // KVW


def _rope_tables(pos_ref, invf_ref, width):
    ang = pos_ref[...] * invf_ref[...]
    lane = lax.broadcasted_iota(jnp.int32, (1, 128), 1)
    sign = jnp.where((lane % HD) < (HD // 2), -1.0, 1.0)
    cos = jnp.tile(jnp.cos(ang), (1, width // 128))
    sin = jnp.tile(sign * jnp.sin(ang), (1, width // 128))
    first = (lax.broadcasted_iota(jnp.int32, (1, width), 1) % HD) < (HD // 2)
    return cos, sin, first


def _rot_half(u, first):
    w = u.shape[1]
    return jnp.where(first, pltpu.roll(u, w - HD // 2, 1), pltpu.roll(u, HD // 2, 1))


def _rope_fwd(proj, pos, invf):
    t = proj.shape[0]
    tt = _divtile(t, 512)

    def body(q_ref, k_ref, pos_ref, invf_ref, qo_ref, ko_ref):
        cos, sin, first = _rope_tables(pos_ref, invf_ref, QW)
        q = q_ref[...]
        qo_ref[...] = (q * cos + _rot_half(q, first) * sin).astype(BF16)
        k = k_ref[...]
        ko_ref[...] = (k * cos[:, :KVW] + _rot_half(k, first[:, :KVW]) * sin[:, :KVW]).astype(BF16)

    return pl.pallas_call(
        body, name="rope_fwd", grid=(t // tt,),
        in_specs=[pl.BlockSpec((tt, QW), lambda i: (i, Q_BLK)), pl.BlockSpec((tt, KVW), lambda i: (i, K_BLK)),
                  pl.BlockSpec((tt, 1), lambda i: (i, 0)), pl.BlockSpec((1, 128), lambda i: (0, 0))],
        out_specs=[pl.BlockSpec((tt, QW), lambda i: (i, 0)), pl.BlockSpec((tt, KVW), lambda i: (i, 0))],
        out_shape=[jax.ShapeDtypeStruct((t, QW), BF16), jax.ShapeDtypeStruct((t, KVW), BF16)],
        compiler_params=_params(("parallel",), 40 << 20),
    )(proj, proj, pos, invf)


def _rope_bwd(dq, dk, pos, invf):
    t = dq.shape[0]
    tt = _divtile(t, 512)

    def body(dq_ref, dk_ref, pos_ref, invf_ref, qo_ref, ko_ref):
        cos, sin, first = _rope_tables(pos_ref, invf_ref, QW)
        q = dq_ref[...]
        qo_ref[...] = (q * cos + _rot_half(q * sin, first)).astype(BF16)
        k = dk_ref[...]
        ko_ref[...] = (k * cos[:, :KVW] + _rot_half(k * sin[:, :KVW], first[:, :KVW])).astype(BF16)

    return pl.pallas_call(
        body, name="rope_bwd", grid=(t // tt,),
        in_specs=[pl.BlockSpec((tt, QW), lambda i: (i, 0)), pl.BlockSpec((tt, KVW), lambda i: (i, 0)),
                  pl.BlockSpec((tt, 1), lambda i: (i, 0)), pl.BlockSpec((1, 128), lambda i: (0, 0))],
        out_specs=[pl.BlockSpec((tt, QW), lambda i: (i, 0)), pl.BlockSpec((tt, KVW), lambda i: (i, 0))],
        out_shape=[jax.ShapeDtypeStruct((t, QW), BF16), jax.ShapeDtypeStruct((t, KVW), BF16)],
        compiler_params=_params(("parallel",), 40 << 20),
    )(dq, dk, pos, invf)


def _attn_probs(q, kp, kc, sink, mask_p, mask_c):
    nt = (((1,), (1,)), ((), ()))
    sp = jnp.where(mask_p, lax.dot_general(q, kp, nt, preferred_element_type=F32) * (HD ** -0.5), -jnp.inf)
    sc = jnp.where(mask_c, lax.dot_general(q, kc, nt, preferred_element_type=F32) * (HD ** -0.5), -jnp.inf)
    m = jnp.maximum(jnp.maximum(jnp.max(sp, axis=-1, keepdims=True), jnp.max(sc, axis=-1, keepdims=True)), sink)
    pp = jnp.exp(sp - m)
    pc = jnp.exp(sc - m)
    ps = jnp.exp(sink - m)
    inv = 1.0 / (jnp.sum(pp, axis=-1, keepdims=True) + jnp.sum(pc, axis=-1, keepdims=True) + ps)
    return pp * inv, pc * inv, ps * inv


def _attn_masks(n):
    qi = lax.broadcasted_iota(jnp.int32, (WINDOW, WINDOW), 0)
    kj = lax.broadcasted_iota(jnp.int32, (WINDOW, WINDOW), 1)
    return (kj > qi) & (n > 0), kj <= qi


def _attn_fwd(qr, kr, proj, sinks):
    t = qr.shape[0]
    nb = t // WINDOW

    def body(q_ref, kc_ref, kp_ref, vc_ref, vp_ref, s_ref, o_ref):
        mask_p, mask_c = _attn_masks(pl.program_id(0))
        vc = vc_ref[...].astype(BF16)
        vp = vp_ref[...].astype(BF16)
        for h in range(NQ):
            j = h // (NQ // NKV)
            ks = slice(HD * j, HD * (j + 1))
            pp, pc, _ = _attn_probs(q_ref[:, HD * h:HD * (h + 1)], kp_ref[:, ks], kc_ref[:, ks],
                                    s_ref[:, h:h + 1], mask_p, mask_c)
            o = (jnp.dot(pp.astype(BF16), vp[:, ks], preferred_element_type=F32)
                 + jnp.dot(pc.astype(BF16), vc[:, ks], preferred_element_type=F32))
            o_ref[:, HD * h:HD * (h + 1)] = o.astype(BF16)

    prev = lambda n: (jnp.maximum(n - 1, 0), 0)
    return pl.pallas_call(
        body, name="attn_fwd", grid=(nb,),
        in_specs=[pl.BlockSpec((WINDOW, QW), lambda n: (n, 0)),
                  pl.BlockSpec((WINDOW, KVW), lambda n: (n, 0)), pl.BlockSpec((WINDOW, KVW), prev),
                  pl.BlockSpec((WINDOW, KVW), lambda n: (n, V_BLK)),
                  pl.BlockSpec((WINDOW, KVW), lambda n: (jnp.maximum(n - 1, 0), V_BLK)),
                  pl.BlockSpec((1, 128), lambda n: (0, 0))],
        out_specs=pl.BlockSpec((WINDOW, QW), lambda n: (n, 0)),
        out_shape=jax.ShapeDtypeStruct((t, QW), BF16),
        compiler_params=_params(("parallel",), 24 << 20),
    )(qr, kr, kr, proj, proj, sinks)


def _attn_bwd(qr, kr, proj, sinks, do):
    t = qr.shape[0]
    nb = t // WINDOW

    def body(q_ref, kc_ref, kp_ref, vc_ref, vp_ref, s_ref, do_ref,
             dq_ref, dk_ref, dv_ref, ds_ref, dkc_ref, dvc_ref):
        i = pl.program_id(0)
        n = nb - 1 - i
        mask_p, mask_c = _attn_masks(n)

        @pl.when(i == 0)
        def _():
            dkc_ref[...] = jnp.zeros_like(dkc_ref)
            dvc_ref[...] = jnp.zeros_like(dvc_ref)
            ds_ref[...] = jnp.zeros_like(ds_ref)

        vc = vc_ref[...].astype(BF16)
        vp = vp_ref[...].astype(BF16)
        nt = (((1,), (1,)), ((), ()))
        tn = (((0,), (0,)), ((), ()))
        lane = lax.broadcasted_iota(jnp.int32, (1, 128), 1)
        ds_acc = jnp.zeros((1, 128), F32)
        scale = HD ** -0.5
        for j in range(NKV):
            ks = slice(HD * j, HD * (j + 1))
            kp = kp_ref[:, ks]
            kc = kc_ref[:, ks]
            dkp = jnp.zeros((WINDOW, HD), F32)
            dkc = jnp.zeros((WINDOW, HD), F32)
            dvp = jnp.zeros((WINDOW, HD), F32)
            dvc = jnp.zeros((WINDOW, HD), F32)
            for gq in range(NQ // NKV):
                h = j * (NQ // NKV) + gq
                hs = slice(HD * h, HD * (h + 1))
                q = q_ref[:, hs]
                pp, pc, ps = _attn_probs(q, kp, kc, s_ref[:, h:h + 1], mask_p, mask_c)
                dob = do_ref[:, hs].astype(BF16)
                dpp = lax.dot_general(dob, vp[:, ks], nt, preferred_element_type=F32)
                dpc = lax.dot_general(dob, vc[:, ks], nt, preferred_element_type=F32)
                delta = jnp.sum(dpp * pp, axis=-1, keepdims=True) + jnp.sum(dpc * pc, axis=-1, keepdims=True)
                dsp = (pp * (dpp - delta) * scale).astype(BF16)
                dsc = (pc * (dpc - delta) * scale).astype(BF16)
                ds_acc = ds_acc + jnp.where(lane == h, -jnp.sum(ps * delta, axis=0, keepdims=True), 0.0)
                dq_ref[:, hs] = (jnp.dot(dsp, kp, preferred_element_type=F32)
                                 + jnp.dot(dsc, kc, preferred_element_type=F32))
                dkp = dkp + lax.dot_general(dsp, q, tn, preferred_element_type=F32)
                dkc = dkc + lax.dot_general(dsc, q, tn, preferred_element_type=F32)
                dvp = dvp + lax.dot_general(pp.astype(BF16), dob, tn, preferred_element_type=F32)
                dvc = dvc + lax.dot_general(pc.astype(BF16), dob, tn, preferred_element_type=F32)
            dk_ref[:, ks] = dkc + dkc_ref[:, ks]
            dv_ref[:, ks] = (dvc + dvc_ref[:, ks]).astype(BF16)
            dkc_ref[:, ks] = dkp
            dvc_ref[:, ks] = dvp
        ds_ref[...] += ds_acc

    cur = lambda i: (nb - 1 - i, 0)
    prev = lambda i: (jnp.maximum(nb - 2 - i, 0), 0)
    return pl.pallas_call(
        body, name="attn_bwd", grid=(nb,),
        in_specs=[pl.BlockSpec((WINDOW, QW), cur),
                  pl.BlockSpec((WINDOW, KVW), cur), pl.BlockSpec((WINDOW, KVW), prev),
                  pl.BlockSpec((WINDOW, KVW), lambda i: (nb - 1 - i, V_BLK)),
                  pl.BlockSpec((WINDOW, KVW), lambda i: (jnp.maximum(nb - 2 - i, 0), V_BLK)),
                  pl.BlockSpec((1, 128), lambda i: (0, 0)),
                  pl.BlockSpec((WINDOW, QW), cur)],
        out_specs=[pl.BlockSpec((WINDOW, QW), cur), pl.BlockSpec((WINDOW, KVW), cur),
                   pl.BlockSpec((WINDOW, KVW), cur), pl.BlockSpec((1, 128), lambda i: (0, 0))],
        out_shape=[jax.ShapeDtypeStruct((t, QW), F32), jax.ShapeDtypeStruct((t, KVW), F32),
                   jax.ShapeDtypeStruct((t, KVW), BF16), jax.ShapeDtypeStruct((1, 128), F32)],
        scratch_shapes=[pltpu.VMEM((WINDOW, KVW), F32), pltpu.VMEM((WINDOW, KVW), F32)],
        compiler_params=_params(("arbitrary",), 32 << 20),
    )(qr, kr, kr, proj, proj, sinks, do)


GS_BLK = SEGS["gs"][2] // D
GA_BLK = SEGS["ga"][2] // D


def _merge_fwd(ys, ya, proj):
    t = ys.shape[0]
    tt = _divtile(t, 256)

    def body(ys_ref, ya_ref, gs_ref, ga_ref, o_ref):
        o_ref[...] = (_sigmoid(gs_ref[...]) * ys_ref[...] + _sigmoid(ga_ref[...]) * ya_ref[...]).astype(BF16)

    row = pl.BlockSpec((tt, D), lambda i: (i, 0))
    return pl.pallas_call(
        body, name="merge_fwd", grid=(t // tt,),
        in_specs=[row, row, pl.BlockSpec((tt, D), lambda i: (i, GS_BLK)), pl.BlockSpec((tt, D), lambda i: (i, GA_BLK))],
        out_specs=row, out_shape=jax.ShapeDtypeStruct((t, D), BF16),
        compiler_params=_params(("parallel",), 32 << 20),
    )(ys, ya, proj, proj)


def _merge_bwd(dm, ys, ya, proj):
    t = ys.shape[0]
    tt = _divtile(t, 256)

    def body(dm_ref, ys_ref, ya_ref, gs_ref, ga_ref, dys_ref, dya_ref, dgs_ref, dga_ref):
        d = dm_ref[...]
        s = _sigmoid(gs_ref[...])
        a = _sigmoid(ga_ref[...])
        dys_ref[...] = (d * s).astype(BF16)
        dya_ref[...] = (d * a).astype(BF16)
        dgs_ref[...] = (d * ys_ref[...] * (s * (1.0 - s))).astype(BF16)
        dga_ref[...] = (d * ya_ref[...] * (a * (1.0 - a))).astype(BF16)

    row = pl.BlockSpec((tt, D), lambda i: (i, 0))
    return pl.pallas_call(
        body, name="merge_bwd", grid=(t // tt,),
        in_specs=[row, row, row, pl.BlockSpec((tt, D), lambda i: (i, GS_BLK)),
                  pl.BlockSpec((tt, D), lambda i: (i, GA_BLK))],
        out_specs=[row, row, row, row], out_shape=[jax.ShapeDtypeStruct((t, D), BF16)] * 4,
        compiler_params=_params(("parallel",), 40 << 20),
    )(dm, ys, ya, proj, proj)


def _pad128(v):
    return jnp.pad(v, ((0, 0), (0, 128 - v.shape[1])))


def _group_major(v):
    t = v.shape[0]
    return jnp.transpose(v[:, :SSD_HEADS].reshape(t, SSD_G, SSD_R), (1, 0, 2))


def _token_major(v3):
    t = v3.shape[1]
    return _pad128(jnp.transpose(v3, (1, 0, 2)).reshape(t, SSD_HEADS))


def _local_step(x, pos, target, w, small):
    xb = x.astype(BF16)
    gu1 = _mm(xb, w["gu1"], "nn", F32, "ffn1_gu")
    a1 = _swiglu_fwd(gu1, "ffn1_act")
    f1 = _mm(a1, w["d1"], "nn", F32, "ffn1_down", caps=(1024, 1024, 1408))
    h1, h1b, xh1, rs1 = _ln_fwd(x, f1, small["ln1_g"], small["ln1_b"], 0.5, "ln1_fwd")
    proj = _mm(h1b, w["win"], "nn", F32, "proj", caps=(1024, 896, 2048))
    bias128 = _pad128(small["dt_bias"])
    alog128 = _pad128(small["a_log"])
    dt, adt = _dt_prep(proj, bias128, alog128)
    dt3, adt3 = _group_major(dt), _group_major(adt)
    xc = _conv_fwd(proj, small["conv_w"], small["conv_b"])
    y_ssd, hprev = _ssd_fwd(xc, dt3, adt3)
    dexp = jnp.repeat(small["d_skip"], SSD_P, axis=1)
    ysn = _gated_norm_fwd(y_ssd, xc, proj, dexp, small["ssd_norm_g"])
    ys = _mm(ysn, w["so"], "nn", F32, "ssd_out")
    invf = jnp.tile(ROPE_THETA ** (-jnp.arange(HD // 2, dtype=F32) * 2.0 / HD), 4)[None, :]
    qr, kr = _rope_fwd(proj, pos, invf)
    sinks128 = _pad128(small["attn_sinks"])
    o = _attn_fwd(qr, kr, proj, sinks128)
    ya = _mm(o, w["ao"], "nn", F32, "attn_out")
    mg = _merge_fwd(ys, ya, proj)
    mix = _mm(mg, w["out"], "nn", F32, "mix_out")
    h2, h2b, xh2, rs2 = _ln_fwd(h1, mix, small["ln2_g"], small["ln2_b"], 1.0, "ln2_fwd")
    gu2 = _mm(h2b, w["gu2"], "nn", F32, "ffn2_gu")
    a2 = _swiglu_fwd(gu2, "ffn2_act")
    f2 = _mm(a2, w["d2"], "nn", F32, "ffn2_down", caps=(1024, 1024, 1408))
    _, _, xh3, rs3, dh3, loss = _ln_fwd(h2, f2, small["ln3_g"], small["ln3_b"], 0.5, "ln3_fwd", target=target)

    gw, gs = {}, {}
    dr3, dr3h, gs["ln3_g"], gs["ln3_b"] = _ln_bwd(dh3, xh3, rs3, small["ln3_g"], 0.5, "ln3_bwd")
    gw["d2"] = _mm(a2, dr3h, "tn", F32, "ffn2_down_dw")
    da2 = _mm(dr3h, w["d2"], "nt", BF16, "ffn2_down_dx", caps=(1024, 1408, 2048))
    dgu2 = _swiglu_bwd(gu2, da2, "ffn2_act_bwd")
    gw["gu2"] = _mm(h2b, dgu2, "tn", F32, "ffn2_gu_dw", caps=(1024, 1408, 2048), n_slabs=N_CHIPS)
    dh2 = _mm(dgu2, w["gu2"], "nt", F32, "ffn2_gu_dx", add=dr3, add_scale=ALPHA, caps=(1024, 1024, 1408))
    dr2, dr2b, gs["ln2_g"], gs["ln2_b"] = _ln_bwd(dh2, xh2, rs2, small["ln2_g"], 1.0, "ln2_bwd")
    gw["out"] = _mm(mg, dr2b, "tn", F32, "mix_out_dw")
    dmg = _mm(dr2b, w["out"], "nt", F32, "mix_out_dx")
    dys, dya, dgs, dga = _merge_bwd(dmg, ys, ya, proj)
    gw["ao"] = _mm(o, dya, "tn", F32, "attn_out_dw")
    do = _mm(dya, w["ao"], "nt", BF16, "attn_out_dx")
    dqr, dkr, dv, gs["attn_sinks"] = _attn_bwd(qr, kr, proj, sinks128, do)
    dq, dk = _rope_bwd(dqr, dkr, pos, invf)
    gw["so"] = _mm(ysn, dys, "tn", F32, "ssd_out_dw")
    dysn = _mm(dys, w["so"], "nt", F32, "ssd_out_dx")
    dz, dy1, gs["ssd_norm_g"], dd_ch = _gated_norm_bwd(dysn, y_ssd, xc, proj, dexp, small["ssd_norm_g"])
    gs["d_skip"] = _fold_heads(dd_ch, "d_skip_fold")
    dxs, db, dc, dadt3, dxdx3 = _ssd_bwd(xc, dt3, adt3, hprev, dy1)
    ddt, gs["dt_bias"], gs["a_log"] = _dt_bwd(_token_major(dadt3), _token_major(dxdx3), proj, bias128, alog128)
    cw, cbias = small["conv_w"], small["conv_b"]
    dux, dwx, dbx = _conv_bwd(proj, dxs, cw, cbias, 0, SSD_INNER, "conv_bwd_x", skip=(dy1, dexp))
    dub, dwb, dbb = _conv_bwd(proj, db, cw, cbias, SSD_INNER, SSD_G * SSD_N, "conv_bwd_b")
    duc, dwc, dbc = _conv_bwd(proj, dc, cw, cbias, SSD_INNER + SSD_G * SSD_N, SSD_G * SSD_N, "conv_bwd_c")
    gs["conv_w"] = jnp.concatenate([dwx[:4], dwb[:4], dwc[:4]], axis=1)
    gs["conv_b"] = jnp.concatenate([dbx, dbb, dbc], axis=1)
    dproj = jnp.concatenate([dz, dq, dgs, dga, dux, dub, duc, dk, dv, ddt], axis=1)
    gw["win"] = _mm(h1b, dproj, "tn", F32, "proj_dw", caps=(1024, 896, 2048))
    dh1 = _mm(dproj, w["win"], "nt", F32, "proj_dx", add=dr2, add_scale=ALPHA, caps=(1024, 1024, 896))
    dr1, dr1h, gs["ln1_g"], gs["ln1_b"] = _ln_bwd(dh1, xh1, rs1, small["ln1_g"], 0.5, "ln1_bwd")
    gw["d1"] = _mm(a1, dr1h, "tn", F32, "ffn1_down_dw")
    da1 = _mm(dr1h, w["d1"], "nt", BF16, "ffn1_down_dx", caps=(1024, 1408, 2048))
    dgu1 = _swiglu_bwd(gu1, da1, "ffn1_act_bwd")
    gw["gu1"] = _mm(xb, dgu1, "tn", F32, "ffn1_gu_dw", caps=(1024, 1408, 2048), n_slabs=N_CHIPS)
    grad_x = _mm(dgu1, w["gu1"], "nt", F32, "ffn1_gu_dx", add=dr1, add_scale=ALPHA, caps=(1024, 1024, 1408))
    return loss, grad_x, gw, gs


MESH = pl.DeviceIdType.MESH
ANY = pl.BlockSpec(memory_space=pl.ANY)


def _place():
    x, y, c = lax.axis_index("x"), lax.axis_index("y"), lax.axis_index("c")
    peers = [(1 - x, y), (x, 1 - y), (1 - x, 1 - y)]
    return x, y, c, peers


BIG = [
    ("ffn1_w_gate", D, SHARD_H, "gu1", "col", 0),
    ("ffn1_w_up", D, SHARD_H, "gu1", "col", SHARD_H),
    ("ffn1_w_down", SHARD_H, D, "d1", "row", 0),
    ("w_in", D, SHARD_IN, "win4", "lead", 0),
    ("w_ssd_o", SSD_INNER // N_CHIPS, D, "so", "row", 0),
    ("w_attn_o", D // N_CHIPS, D, "ao", "row", 0),
    ("w_out", D // N_CHIPS, D, "out", "row", 0),
    ("ffn2_w_gate", D, SHARD_H, "gu2", "col", 0),
    ("ffn2_w_up", D, SHARD_H, "gu2", "col", SHARD_H),
    ("ffn2_w_down", SHARD_H, D, "d2", "row", 0),
]
GATHERED = {"gu1": (D, 2 * FFN_H), "d1": (FFN_H, D), "win4": (N_CHIPS, D, SHARD_IN), "so": (SSD_INNER, D),
            "ao": (D, D), "out": (D, D), "gu2": (D, 2 * FFN_H), "d2": (FFN_H, D)}


def _gather_weights(shards):
    n = len(BIG)
    out_names = list(GATHERED)

    def body(*refs):
        srcs = refs[:n]
        outs = dict(zip(out_names, refs[n:n + len(out_names)]))
        send, recv, fsend, frecv, lsem = refs[n + len(out_names):]
        x, y, c, peers = _place()
        me = 2 * x + y

        def slot(i, j, half):
            _, rows, cols, oname, kind, off = BIG[i]
            o = outs[oname]
            hr = rows // 2
            if kind == "col":
                cs = pl.ds(pl.multiple_of(j * (2 * SHARD_H) + off, 128), cols)
                return o.at[:, cs] if half is None else o.at[pl.ds(pl.multiple_of(half * hr, 16), hr), cs]
            if kind == "row":
                if half is None:
                    return o.at[pl.ds(pl.multiple_of(j * rows, 16), rows), :]
                return o.at[pl.ds(pl.multiple_of(j * rows + half * hr, 16), hr), :]
            return o.at[j] if half is None else o.at[j, pl.ds(pl.multiple_of(half * hr, 16), hr), :]

        def src_half(i, half):
            hr = BIG[i][1] // 2
            return srcs[i].at[pl.ds(pl.multiple_of(half * hr, 16), hr), :]

        local = [pltpu.make_async_copy(srcs[i], slot(i, me, None), lsem.at[i]) for i in range(n)]
        for cp in local:
            cp.start()
        sends = []
        for i in range(n):
            for k, (px, py) in enumerate(peers):
                cp = pltpu.make_async_remote_copy(src_ref=src_half(i, c), dst_ref=slot(i, me, c),
                                                  send_sem=send.at[3 * i + k], recv_sem=recv.at[3 * i + k],
                                                  device_id=(px, py, c), device_id_type=MESH)
                cp.start()
                sends.append(cp)
        fwds = []
        for i in range(n):
            for k, (px, py) in enumerate(peers):
                pj = 2 * px + py
                pltpu.make_async_remote_copy(src_ref=src_half(i, c), dst_ref=slot(i, pj, c),
                                             send_sem=send.at[3 * i + k], recv_sem=recv.at[3 * i + k],
                                             device_id=(px, py, c), device_id_type=MESH).wait_recv()
                cp = pltpu.make_async_remote_copy(src_ref=slot(i, pj, c), dst_ref=slot(i, pj, c),
                                                  send_sem=fsend.at[3 * i + k], recv_sem=frecv.at[3 * i + k],
                                                  device_id=(x, y, 1 - c), device_id_type=MESH)
                cp.start()
                fwds.append(cp)
        for i in range(n):
            for k, (px, py) in enumerate(peers):
                pj = 2 * px + py
                pltpu.make_async_remote_copy(src_ref=slot(i, pj, 1 - c), dst_ref=slot(i, pj, 1 - c),
                                             send_sem=fsend.at[3 * i + k], recv_sem=frecv.at[3 * i + k],
                                             device_id=(x, y, 1 - c), device_id_type=MESH).wait_recv()
        for cp in sends + fwds:
            cp.wait_send()
        for cp in local:
            cp.wait()

    outs = pl.pallas_call(
        body, name="gather_weights",
        in_specs=[ANY] * n, out_specs=[ANY] * len(out_names),
        out_shape=[jax.ShapeDtypeStruct(GATHERED[k], BF16) for k in out_names],
        scratch_shapes=[pltpu.SemaphoreType.DMA((3 * n,))] * 4 + [pltpu.SemaphoreType.DMA((n,))],
    )(*shards)
    return dict(zip(out_names, outs))


def _win_pieces():
    pieces = []
    for g0, wd, i0 in SEGS.values():
        for j in range(N_CHIPS):
            lo, hi = max(g0, j * SHARD_IN), min(g0 + wd, (j + 1) * SHARD_IN)
            if lo < hi:
                pieces.append((j, lo - j * SHARD_IN, hi - j * SHARD_IN, i0 + lo - g0))
    return pieces


def _win_to_internal(win4):
    tr = 128

    def body(i_ref, o_ref):
        for j, s0, s1, d0 in _win_pieces():
            o_ref[:, d0:d0 + s1 - s0] = i_ref[j, :, s0:s1]
        o_ref[:, PROJ_W:] = jnp.zeros((tr, PROJ_PAD - PROJ_W), o_ref.dtype)

    return pl.pallas_call(
        body, name="win_to_internal", grid=(D // tr,),
        in_specs=[pl.BlockSpec((N_CHIPS, tr, SHARD_IN), lambda i: (0, i, 0))],
        out_specs=pl.BlockSpec((tr, PROJ_PAD), lambda i: (i, 0)),
        out_shape=jax.ShapeDtypeStruct((D, PROJ_PAD), win4.dtype),
        compiler_params=_params(("parallel",), 40 << 20),
    )(win4)


def _win_from_internal(g):
    tr = 64

    def body(i_ref, o_ref):
        for j, s0, s1, d0 in _win_pieces():
            o_ref[j, :, s0:s1] = i_ref[:, d0:d0 + s1 - s0]

    return pl.pallas_call(
        body, name="win_from_internal", grid=(D // tr,),
        in_specs=[pl.BlockSpec((tr, PROJ_PAD), lambda i: (i, 0))],
        out_specs=pl.BlockSpec((N_CHIPS, tr, SHARD_IN), lambda i: (0, i, 0)),
        out_shape=jax.ShapeDtypeStruct((N_CHIPS, D, SHARD_IN), g.dtype),
        compiler_params=_params(("parallel",), 40 << 20),
    )(g)


def _rs_pair_exchange(grads):
    n = len(grads)

    def body(*refs):
        srcs, dsts = refs[:n], refs[n:2 * n]
        send, recv = refs[2 * n:]
        x, y, c, _ = _place()
        cps = []
        for i in range(n):
            hr = srcs[i].shape[1] // 2
            cp = pltpu.make_async_remote_copy(
                src_ref=srcs[i].at[:, pl.ds(pl.multiple_of((1 - c) * hr, 16), hr), :], dst_ref=dsts[i],
                send_sem=send.at[i], recv_sem=recv.at[i], device_id=(x, y, 1 - c), device_id_type=MESH)
            cp.start()
            cps.append(cp)
        for cp in cps:
            cp.wait()

    return pl.pallas_call(
        body, name="rs_pair_exchange", in_specs=[ANY] * n, out_specs=[ANY] * n,
        out_shape=[jax.ShapeDtypeStruct((g.shape[0], g.shape[1] // 2, g.shape[2]), F32) for g in grads],
        scratch_shapes=[pltpu.SemaphoreType.DMA((n,))] * 2,
    )(*grads)


def _half_tile(hr):
    return _divtile(hr, 256, 16) if hr % 256 == 0 else _divtile(hr, 512, 16)


def _rs_pair_sum(g, r, c_idx, name):
    ns, rows, cols = g.shape
    hr = rows // 2
    tr = _half_tile(hr)
    nt = hr // tr

    def body(c_ref, g_ref, r_ref, ob_ref, of_ref):
        s = g_ref[...] + r_ref[...]
        ob_ref[...] = s.astype(BF16)
        of_ref[...] = s

    blk = pl.BlockSpec((None, tr, cols), lambda j, t, c_ref: (j, t, 0))
    return pl.pallas_call(
        body, name=name,
        grid_spec=pltpu.PrefetchScalarGridSpec(
            num_scalar_prefetch=1, grid=(ns, nt),
            in_specs=[pl.BlockSpec((None, tr, cols), lambda j, t, c_ref: (j, c_ref[0] * nt + t, 0)), blk],
            out_specs=[blk, blk]),
        out_shape=[jax.ShapeDtypeStruct((ns, hr, cols), BF16), jax.ShapeDtypeStruct((ns, hr, cols), F32)],
        compiler_params=_params(("parallel", "parallel"), 48 << 20),
    )(c_idx, g, r)


def _rs_chip_exchange(parts):
    n = len(parts)

    def body(*refs):
        srcs, dsts = refs[:n], refs[n:2 * n]
        send, recv = refs[2 * n:]
        x, y, c, peers = _place()
        cps = []
        for i in range(n):
            for k, (px, py) in enumerate(peers):
                cp = pltpu.make_async_remote_copy(
                    src_ref=srcs[i].at[2 * px + py], dst_ref=dsts[i].at[k],
                    send_sem=send.at[3 * i + k], recv_sem=recv.at[3 * i + k],
                    device_id=(px, py, c), device_id_type=MESH)
                cp.start()
                cps.append(cp)
        for cp in cps:
            cp.wait()

    return pl.pallas_call(
        body, name="rs_chip_exchange", in_specs=[ANY] * n, out_specs=[ANY] * n,
        out_shape=[jax.ShapeDtypeStruct((3,) + p.shape[1:], BF16) for p in parts],
        scratch_shapes=[pltpu.SemaphoreType.DMA((3 * n,))] * 2,
    )(*parts)


def _rs_final_sum(own, got, me_idx, name):
    ns, hr, cols = own.shape
    tr = _half_tile(hr)

    def body(me_ref, o_ref, g_ref, out_ref):
        s = o_ref[...]
        for k in range(3):
            s = s + g_ref[k].astype(F32)
        out_ref[...] = s

    return pl.pallas_call(
        body, name=name,
        grid_spec=pltpu.PrefetchScalarGridSpec(
            num_scalar_prefetch=1, grid=(hr // tr,),
            in_specs=[pl.BlockSpec((None, tr, cols), lambda t, me_ref: (me_ref[0], t, 0)),
                      pl.BlockSpec((3, tr, cols), lambda t, me_ref: (0, t, 0))],
            out_specs=pl.BlockSpec((tr, cols), lambda t, me_ref: (t, 0))),
        out_shape=jax.ShapeDtypeStruct((hr, cols), F32),
        compiler_params=_params(("parallel",), 48 << 20),
    )(me_idx, own, got)


def _rs_share_halves(halves):
    n = len(halves)

    def body(*refs):
        srcs, dsts = refs[:n], refs[n:2 * n]
        send, recv, lsem = refs[2 * n:]
        x, y, c, _ = _place()
        cps, local = [], []
        for i in range(n):
            hr = srcs[i].shape[0]
            rows = dsts[i].at[pl.ds(pl.multiple_of(c * hr, 8), hr), :]
            lc = pltpu.make_async_copy(srcs[i], rows, lsem.at[i])
            lc.start()
            local.append(lc)
            cp = pltpu.make_async_remote_copy(src_ref=srcs[i], dst_ref=rows, send_sem=send.at[i], recv_sem=recv.at[i],
                                              device_id=(x, y, 1 - c), device_id_type=MESH)
            cp.start()
            cps.append(cp)
        for i in range(n):
            hr = srcs[i].shape[0]
            other = dsts[i].at[pl.ds(pl.multiple_of((1 - c) * hr, 8), hr), :]
            pltpu.make_async_remote_copy(src_ref=srcs[i], dst_ref=other, send_sem=send.at[i], recv_sem=recv.at[i],
                                         device_id=(x, y, 1 - c), device_id_type=MESH).wait_recv()
        for cp in cps:
            cp.wait_send()
        for lc in local:
            lc.wait()

    return pl.pallas_call(
        body, name="rs_share_halves", in_specs=[ANY] * n, out_specs=[ANY] * n,
        out_shape=[jax.ShapeDtypeStruct((2 * h.shape[0], h.shape[1]), F32) for h in halves],
        scratch_shapes=[pltpu.SemaphoreType.DMA((n,))] * 3,
    )(*halves)


def _all_reduce_small(v):
    rows = v.shape[0]

    def body(v_ref, o_ref, buf, send, recv):
        x, y, c, _ = _place()
        me = 4 * x + 2 * y + c
        buf[me] = v_ref[...]
        cps = []
        for d in range(1, 8):
            px, py, pc = x ^ (d >> 2), y ^ ((d >> 1) & 1), c ^ (d & 1)
            cp = pltpu.make_async_remote_copy(src_ref=v_ref, dst_ref=buf.at[me], send_sem=send.at[d - 1],
                                              recv_sem=recv.at[d - 1], device_id=(px, py, pc), device_id_type=MESH)
            cp.start()
            cps.append(cp)
        for d in range(1, 8):
            px, py, pc = x ^ (d >> 2), y ^ ((d >> 1) & 1), c ^ (d & 1)
            pltpu.make_async_remote_copy(src_ref=v_ref, dst_ref=buf.at[4 * px + 2 * py + pc], send_sem=send.at[d - 1],
                                         recv_sem=recv.at[d - 1], device_id=(px, py, pc),
                                         device_id_type=MESH).wait_recv()
        for cp in cps:
            cp.wait_send()
        acc = buf[0]
        for d in range(1, 8):
            acc = acc + buf[d]
        o_ref[...] = acc

    vm = pl.BlockSpec(memory_space=pltpu.VMEM)
    return pl.pallas_call(
        body, name="all_reduce_small", in_specs=[vm], out_specs=vm,
        out_shape=jax.ShapeDtypeStruct((rows, 128), F32),
        scratch_shapes=[pltpu.VMEM((8, rows, 128), F32), pltpu.SemaphoreType.DMA((7,)), pltpu.SemaphoreType.DMA((7,))],
    )(v)


def _adamw(w, g, m, v, name, g_col_blk=0):
    rows, cols = w.shape
    tr = _divtile(rows, max(8, (2 << 20) // (4 * cols) // 8 * 8), 8)

    def body(w_ref, g_ref, m_ref, v_ref, go_ref, d_ref, mo_ref, vo_ref):
        gv = g_ref[...]
        mn = ADAM_B1 * m_ref[...] + (1.0 - ADAM_B1) * gv
        vn = ADAM_B2 * v_ref[...] + (1.0 - ADAM_B2) * (gv * gv)
        m_hat = mn / (1.0 - ADAM_B1 ** ADAM_STEP)
        v_hat = vn / (1.0 - ADAM_B2 ** ADAM_STEP)
        go_ref[...] = gv
        d_ref[...] = -ADAM_LR * (m_hat / (jnp.sqrt(v_hat) + ADAM_EPS) + ADAM_WD * w_ref[...])
        mo_ref[...] = mn
        vo_ref[...] = vn

    blk = pl.BlockSpec((tr, cols), lambda i: (i, 0))
    return pl.pallas_call(
        body, name=name, grid=(rows // tr,),
        in_specs=[blk, pl.BlockSpec((tr, cols), lambda i: (i, g_col_blk)), blk, blk],
        out_specs=[blk] * 4, out_shape=[jax.ShapeDtypeStruct((rows, cols), F32)] * 4,
        compiler_params=_params(("parallel",), 48 << 20),
    )(w, g, m, v)


SMALL = ["ln1_g", "ln1_b", "conv_w", "conv_b", "dt_bias", "a_log", "d_skip", "ssd_norm_g", "attn_sinks",
         "ln2_g", "ln2_b", "ln3_g", "ln3_b"]


def _pack_rows(vs):
    parts = []
    for v in vs:
        v = v.reshape(-1)
        parts.append(jnp.pad(v, (0, (-v.shape[0]) % 128)))
    flat = jnp.concatenate(parts)
    flat = jnp.pad(flat, (0, (-flat.shape[0]) % 1024))
    return flat.reshape(-1, 128)


def _unpack_rows(packed, shapes):
    flat = packed.reshape(-1)
    out, at = [], 0
    for s in shapes:
        nel = int(np.prod(s))
        out.append(flat[at:at + nel].reshape(s))
        at += nel + (-nel) % 128
    return out


def kernel(x, positions, ffn1_w_gate, ffn1_w_up, ffn1_w_down, ln1_g, ln1_b, w_in, conv_w, conv_b, dt_bias, a_log, d_skip, ssd_norm_g, w_ssd_o, attn_sinks, w_attn_o, w_out, ln2_g, ln2_b, ffn2_w_gate, ffn2_w_up, ffn2_w_down, ln3_g, ln3_b, loss_target, m_ffn1_w_gate, m_ffn1_w_up, m_ffn1_w_down, m_ln1_g, m_ln1_b, m_w_in, m_conv_w, m_conv_b, m_dt_bias, m_a_log, m_d_skip, m_ssd_norm_g, m_w_ssd_o, m_attn_sinks, m_w_attn_o, m_w_out, m_ln2_g, m_ln2_b, m_ffn2_w_gate, m_ffn2_w_up, m_ffn2_w_down, m_ln3_g, m_ln3_b, v_ffn1_w_gate, v_ffn1_w_up, v_ffn1_w_down, v_ln1_g, v_ln1_b, v_w_in, v_conv_w, v_conv_b, v_dt_bias, v_a_log, v_d_skip, v_ssd_norm_g, v_w_ssd_o, v_attn_sinks, v_w_attn_o, v_w_out, v_ln2_g, v_ln2_b, v_ffn2_w_gate, v_ffn2_w_up, v_ffn2_w_down, v_ln3_g, v_ln3_b):
    args = dict(locals())
    wts = {n: args[n][0] for n in [b[0] for b in BIG] + SMALL}
    mom_m = {n: args["m_" + n][0] for n in wts}
    mom_v = {n: args["v_" + n][0] for n in wts}
    t = x.shape[1]
    xi, yi, ci = lax.axis_index("x"), lax.axis_index("y"), lax.axis_index("c")
    chip = 2 * xi + yi

    gathered = _gather_weights([wts[b[0]].astype(BF16) for b in BIG])
    w = {k: gathered[k] for k in ("gu1", "d1", "so", "ao", "out", "gu2", "d2")}
    w["win"] = _win_to_internal(gathered["win4"])
    cw_rows = _pack_rows([lax.dynamic_update_slice(jnp.zeros((4, XBC), F32), wts["conv_w"], (0, chip * (XBC // N_CHIPS)))])
    cw_rows = jnp.where(ci == 0, cw_rows, 0.0)
    conv_w_full = _all_reduce_small(cw_rows)[:4 * XBC // 128].reshape(4, XBC)

    small = {n: (wts[n][None, :] if wts[n].ndim == 1 else wts[n]) for n in SMALL}
    small["conv_w"] = conv_w_full
    loss, grad_x, gw, gs = _local_step(x[0], positions[0].astype(F32)[:, None], loss_target[0], w, small)

    slabs = [gw["gu1"], gw["d1"].reshape(N_CHIPS, SHARD_H, D), _win_from_internal(gw["win"]),
             gw["so"].reshape(N_CHIPS, SSD_INNER // N_CHIPS, D), gw["ao"].reshape(N_CHIPS, D // N_CHIPS, D),
             gw["out"].reshape(N_CHIPS, D // N_CHIPS, D), gw["gu2"], gw["d2"].reshape(N_CHIPS, SHARD_H, D)]
    names = ["gu1", "d1", "win", "so", "ao", "out", "gu2", "d2"]
    from_sib = _rs_pair_exchange(slabs)
    c_idx = ci.astype(jnp.int32).reshape(1)
    chip_idx = chip.astype(jnp.int32).reshape(1)
    pair = [_rs_pair_sum(g, r, c_idx, "rs_pair_sum_" + nm) for g, r, nm in zip(slabs, from_sib, names)]
    got = _rs_chip_exchange([p[0] for p in pair])
    halves = [_rs_final_sum(p[1], gt, chip_idx, "rs_final_sum_" + nm) for p, gt, nm in zip(pair, got, names)]
    full = dict(zip(names, _rs_share_halves(halves)))

    outs = {}
    big_src = {"ffn1_w_gate": ("gu1", 0), "ffn1_w_up": ("gu1", 1), "ffn1_w_down": ("d1", 0), "w_in": ("win", 0),
               "w_ssd_o": ("so", 0), "w_attn_o": ("ao", 0), "w_out": ("out", 0),
               "ffn2_w_gate": ("gu2", 0), "ffn2_w_up": ("gu2", 1), "ffn2_w_down": ("d2", 0)}
    for nm, (src, blk) in big_src.items():
        outs[nm] = _adamw(wts[nm], full[src], mom_m[nm], mom_v[nm], "adamw_" + nm, g_col_blk=blk)

    gvec = {n: gs[n] for n in SMALL}
    gvec["dt_bias"], gvec["a_log"], gvec["d_skip"] = gs["dt_bias"][:, :64], gs["a_log"][:, :64], gs["d_skip"][:, :64]
    gvec["attn_sinks"] = gs["attn_sinks"][:, :NQ]
    red = _all_reduce_small(_pack_rows([gvec[n] for n in SMALL] + [loss]))
    shapes = [(4, XBC) if n == "conv_w" else wts[n].shape for n in SMALL] + [(1,)]
    red_list = _unpack_rows(red, shapes)
    loss_out = red_list[-1].reshape(())
    gsm = dict(zip(SMALL, red_list[:-1]))
    gsm["conv_w"] = lax.dynamic_slice_in_dim(gsm["conv_w"], chip * (XBC // N_CHIPS), XBC // N_CHIPS, axis=1)
    sm_shapes = [wts[n].shape for n in SMALL]
    res = _adamw(_pack_rows([wts[n] for n in SMALL]), _pack_rows([gsm[n] for n in SMALL]),
                 _pack_rows([mom_m[n] for n in SMALL]), _pack_rows([mom_v[n] for n in SMALL]), "adamw_small")
    res = [_unpack_rows(r, sm_shapes) for r in res]
    for i, nm in enumerate(SMALL):
        outs[nm] = tuple(r[i] for r in res)

    order = ["ffn1_w_gate", "ffn1_w_up", "ffn1_w_down", "ln1_g", "ln1_b", "w_in", "conv_w", "conv_b", "dt_bias", "a_log",
             "d_skip", "ssd_norm_g", "w_ssd_o", "attn_sinks", "w_attn_o", "w_out", "ln2_g", "ln2_b",
             "ffn2_w_gate", "ffn2_w_up", "ffn2_w_down", "ln3_g", "ln3_b"]
    result = [loss_out, grad_x[None]]
    for kind in range(4):
        result += [outs[nm][kind][None] for nm in order]
    return tuple(result)
```

```python
import functools
import math

import numpy as np
import jax
import jax.numpy as jnp
from jax import lax
from jax.experimental import pallas as pl
from jax.experimental.pallas import tpu as pltpu

F32 = jnp.float32
BF16 = jnp.bfloat16
HI = lax.Precision.HIGHEST

D = 2048
FFN_H = 5632
SSD_INNER = 4096
SSD_HEADS = 64
SSD_P = 64
SSD_G = 8
SSD_R = 8
SSD_N = 128
CHUNK = 128
XBC = 6144
NQ = 32
NKV = 4
HD = 64
QW = 2048
KVW = 256
WINDOW = 128
ROPE_THETA = 10000.0
ALPHA = 2.0 ** 0.25
LN_EPS = 1e-5
RMS_EPS = 1e-5
PROJ_W = 16960
N_CHIPS = 4
SHARD_IN = PROJ_W // N_CHIPS
SHARD_H = FFN_H // N_CHIPS

SEGS = {
    "z": (0, 4096, 0),
    "xbc": (4096, 6144, 10240),
    "dt": (10240, 64, 16896),
    "q": (10304, 2048, 4096),
    "k": (12352, 256, 16384),
    "v": (12608, 256, 16640),
    "gs": (12864, 2048, 6144),
    "ga": (14912, 2048, 8192),
}
PROJ_PAD = 17024

ADAM_LR = 0.001
ADAM_B1 = 0.9
ADAM_B2 = 0.999
ADAM_EPS = 1e-08
ADAM_WD = 0.01
ADAM_STEP = 10

VMEM_CAP = 60 * 1024 * 1024


def _params(sem, vmem_bytes):
    return pltpu.CompilerParams(dimension_semantics=sem, vmem_limit_bytes=int(min(VMEM_CAP, vmem_bytes)))


def _divtile(n, cap, q=128):
    best = None
    for d in range(q, min(n, cap) + 1, q):
        if n % d == 0:
            best = d
    return n if best is None else best


def _sigmoid(x):
    return 1.0 / (1.0 + jnp.exp(-x))


def _mm(a, b, mode, out_dtype, name, add=None, add_scale=1.0, caps=(1024, 1024, 2048), n_slabs=1):
    if mode == "nn":
        (m, k), (k2, n) = a.shape, b.shape
    elif mode == "nt":
        (m, k), (n, k2) = a.shape, b.shape
    else:
        (k, m), (k2, n) = a.shape, b.shape
    assert k == k2, (a.shape, b.shape, mode)
    tm, tn, tk = _divtile(m, caps[0]), _divtile(n // n_slabs, caps[1]), _divtile(k, caps[2])
    nk = k // tk
    per_slab = n // n_slabs // tn
    dims = {"nn": ((1,), (0,)), "nt": ((1,), (1,)), "tn": ((0,), (0,))}[mode]
    has_add = add is not None

    def body(*refs):
        if has_add:
            a_ref, b_ref, add_ref, o_ref = refs[:4]
            scr = refs[4:]
        else:
            a_ref, b_ref, o_ref = refs[:3]
            add_ref = None
            scr = refs[3:]
        part = lax.dot_general(a_ref[...].astype(BF16), b_ref[...].astype(BF16), (dims, ((), ())),
                               preferred_element_type=F32)

        def finish(acc):
            if has_add:
                acc = acc + add_scale * add_ref[...].astype(F32)
            o_ref[...] = acc.astype(o_ref.dtype)

        if nk == 1:
            finish(part)
        else:
            acc_ref = scr[0]
            kk = pl.program_id(2)

            @pl.when(kk == 0)
            def _():
                acc_ref[...] = part

            @pl.when(kk > 0)
            def _():
                acc_ref[...] += part

            @pl.when(kk == nk - 1)
            def _():
                finish(acc_ref[...])

    if mode == "nn":
        a_spec = pl.BlockSpec((tm, tk), lambda i, j, kk: (i, kk))
        b_spec = pl.BlockSpec((tk, tn), lambda i, j, kk: (kk, j))
    elif mode == "nt":
        a_spec = pl.BlockSpec((tm, tk), lambda i, j, kk: (i, kk))
        b_spec = pl.BlockSpec((tn, tk), lambda i, j, kk: (j, kk))
    else:
        a_spec = pl.BlockSpec((tk, tm), lambda i, j, kk: (kk, i))
        b_spec = pl.BlockSpec((tk, tn), lambda i, j, kk: (kk, j))
    o_spec = pl.BlockSpec((tm, tn), lambda i, j, kk: (i, j))
    out_shape = jax.ShapeDtypeStruct((m, n), out_dtype)
    if n_slabs > 1:
        assert not has_add
        o_spec = pl.BlockSpec((None, tm, tn), lambda i, j, kk: (j // per_slab, i, j % per_slab))
        out_shape = jax.ShapeDtypeStruct((n_slabs, m, n // n_slabs), out_dtype)
    in_specs = [a_spec, b_spec] + ([o_spec] if has_add else [])
    args = (a, b) + ((add,) if has_add else ())
    osz = jnp.dtype(out_dtype).itemsize
    vmem = (2 * (tm * tk * a.dtype.itemsize + tk * tn * b.dtype.itemsize) + 2 * tm * tn * osz
            + (2 * tm * tn * add.dtype.itemsize if has_add else 0) + 2 * tm * tn * 4
            + 2 * (tm * tk + tk * tn) + (8 << 20))
    return pl.pallas_call(
        body, name=name, grid=(m // tm, n // tn, nk),
        in_specs=in_specs, out_specs=o_spec, out_shape=out_shape,
        scratch_shapes=[pltpu.VMEM((tm, tn), F32)] if nk > 1 else [],
        compiler_params=_params(("parallel", "parallel", "arbitrary"), vmem),
    )(*args)


def _swiglu_fwd(gu, name):
    t = gu.shape[0]
    tt = _divtile(t, 512)
    w = SHARD_H

    def body(gu_ref, a_ref):
        g = gu_ref[:, :w]
        u = gu_ref[:, w:]
        a_ref[...] = (g * _sigmoid(g) * u).astype(BF16)

    return pl.pallas_call(
        body, name=name, grid=(t // tt, N_CHIPS),
        in_specs=[pl.BlockSpec((tt, 2 * w), lambda i, j: (i, j))],
        out_specs=pl.BlockSpec((tt, w), lambda i, j: (i, j)),
        out_shape=jax.ShapeDtypeStruct((t, FFN_H), BF16),
        compiler_params=_params(("parallel", "parallel"), 40 << 20),
    )(gu)


def _swiglu_bwd(gu, da, name):
    t = gu.shape[0]
    tt = _divtile(t, 512)
    w = SHARD_H

    def body(gu_ref, da_ref, o_ref):
        g = gu_ref[:, :w]
        u = gu_ref[:, w:]
        d = da_ref[...].astype(F32)
        s = _sigmoid(g)
        o_ref[:, :w] = (d * u * (s * (1.0 + g * (1.0 - s)))).astype(BF16)
        o_ref[:, w:] = (d * (g * s)).astype(BF16)

    return pl.pallas_call(
        body, name=name, grid=(t // tt, N_CHIPS),
        in_specs=[pl.BlockSpec((tt, 2 * w), lambda i, j: (i, j)), pl.BlockSpec((tt, w), lambda i, j: (i, j))],
        out_specs=pl.BlockSpec((tt, 2 * w), lambda i, j: (i, j)),
        out_shape=jax.ShapeDtypeStruct((t, 2 * FFN_H), BF16),
        compiler_params=_params(("parallel", "parallel"), 40 << 20),
    )(gu, da)


def _ln_fwd(base, f, g, b, c, name, target=None):
    t = base.shape[0]
    tt = _divtile(t, 256)
    with_loss = target is not None

    def body(*refs):
        if with_loss:
            base_ref, f_ref, g_ref, b_ref, tg_ref, h_ref, hb_ref, xh_ref, rs_ref, dh_ref, loss_ref = refs
        else:
            base_ref, f_ref, g_ref, b_ref, h_ref, hb_ref, xh_ref, rs_ref = refs
        r = ALPHA * base_ref[...] + c * f_ref[...]
        mu = jnp.mean(r, axis=-1, keepdims=True)
        xc = r - mu
        var = jnp.mean(xc * xc, axis=-1, keepdims=True)
        rstd = lax.rsqrt(var + LN_EPS)
        xh = xc * rstd
        h = xh * g_ref[...] + b_ref[...]
        h_ref[...] = h
        hb_ref[...] = h.astype(BF16)
        xh_ref[...] = xh
        rs_ref[...] = rstd
        if with_loss:
            e = h - tg_ref[...]
            dh_ref[...] = e * (1.0 / D)
            part = 0.5 * jnp.sum(jnp.sum(e * e, axis=-1, keepdims=True) * (1.0 / D), axis=0, keepdims=True)

            @pl.when(pl.program_id(0) == 0)
            def _():
                loss_ref[...] = jnp.zeros_like(loss_ref)

            loss_ref[...] += part

    row = pl.BlockSpec((tt, D), lambda i: (i, 0))
    vec = pl.BlockSpec((1, D), lambda i: (0, 0))
    col = pl.BlockSpec((tt, 1), lambda i: (i, 0))
    in_specs = [row, row, vec, vec] + ([row] if with_loss else [])
    out_specs = [row, row, row, col] + ([row, pl.BlockSpec((1, 1), lambda i: (0, 0))] if with_loss else [])
    out_shape = [jax.ShapeDtypeStruct((t, D), F32), jax.ShapeDtypeStruct((t, D), BF16),
                 jax.ShapeDtypeStruct((t, D), F32), jax.ShapeDtypeStruct((t, 1), F32)]
    if with_loss:
        out_shape += [jax.ShapeDtypeStruct((t, D), F32), jax.ShapeDtypeStruct((1, 1), F32)]
    args = (base, f, g, b) + ((target,) if with_loss else ())
    return pl.pallas_call(
        body, name=name, grid=(t // tt,), in_specs=in_specs, out_specs=out_specs, out_shape=out_shape,
        compiler_params=_params(("arbitrary",) if with_loss else ("parallel",), 48 << 20),
    )(*args)


def _ln_bwd(dy, xh, rstd, g, c, name):
    t = dy.shape[0]
    tt = _divtile(t, 256)

    def body(dy_ref, xh_ref, rs_ref, g_ref, dr_ref, drb_ref, dg_ref, db_ref):
        dyv = dy_ref[...]
        xhv = xh_ref[...]
        dxh = dyv * g_ref[...]
        m1 = jnp.mean(dxh, axis=-1, keepdims=True)
        m2 = jnp.mean(dxh * xhv, axis=-1, keepdims=True)
        dr = rs_ref[...] * (dxh - m1 - xhv * m2)
        dr_ref[...] = dr
        drb_ref[...] = (c * dr).astype(BF16)

        @pl.when(pl.program_id(0) == 0)
        def _():
            dg_ref[...] = jnp.zeros_like(dg_ref)
            db_ref[...] = jnp.zeros_like(db_ref)

        dg_ref[...] += jnp.sum(dyv * xhv, axis=0, keepdims=True)
        db_ref[...] += jnp.sum(dyv, axis=0, keepdims=True)

    row = pl.BlockSpec((tt, D), lambda i: (i, 0))
    vec = pl.BlockSpec((1, D), lambda i: (0, 0))
    col = pl.BlockSpec((tt, 1), lambda i: (i, 0))
    return pl.pallas_call(
        body, name=name, grid=(t // tt,), in_specs=[row, row, col, vec], out_specs=[row, row, vec, vec],
        out_shape=[jax.ShapeDtypeStruct((t, D), F32), jax.ShapeDtypeStruct((t, D), BF16),
                   jax.ShapeDtypeStruct((1, D), F32), jax.ShapeDtypeStruct((1, D), F32)],
        compiler_params=_params(("arbitrary",), 40 << 20),
    )(dy, xh, rstd, g)


DT_BLK = SEGS["dt"][2] // 128


def _dt_prep(proj, bias128, alog128):
    t = proj.shape[0]
    tt = _divtile(t, 1024)

    def body(p_ref, bias_ref, alog_ref, dt_ref, adt_ref):
        dtv = jax.nn.softplus(p_ref[...] + bias_ref[...])
        dt_ref[...] = dtv
        adt_ref[...] = dtv * (-jnp.exp(alog_ref[...]))

    blk = pl.BlockSpec((tt, 128), lambda i: (i, 0))
    vec = pl.BlockSpec((1, 128), lambda i: (0, 0))
    return pl.pallas_call(
        body, name="dt_prep", grid=(t // tt,),
        in_specs=[pl.BlockSpec((tt, 128), lambda i: (i, DT_BLK)), vec, vec], out_specs=[blk, blk],
        out_shape=[jax.ShapeDtypeStruct((t, 128), F32)] * 2,
        compiler_params=_params(("parallel",), 16 << 20),
    )(proj, bias128, alog128)


def _dt_bwd(dadt, dxdx, proj, bias128, alog128):
    t = proj.shape[0]
    tt = _divtile(t, 1024)

    def body(dadt_ref, dxdx_ref, p_ref, bias_ref, alog_ref, o_ref, dbias_ref, dalog_ref):
        pre = p_ref[...] + bias_ref[...]
        dtv = jax.nn.softplus(pre)
        a = -jnp.exp(alog_ref[...])
        ddt = a * dadt_ref[...] + dxdx_ref[...]
        draw = ddt * _sigmoid(pre)
        o_ref[...] = draw.astype(BF16)

        @pl.when(pl.program_id(0) == 0)
        def _():
            dbias_ref[...] = jnp.zeros_like(dbias_ref)
            dalog_ref[...] = jnp.zeros_like(dalog_ref)

        dbias_ref[...] += jnp.sum(draw, axis=0, keepdims=True)
        dalog_ref[...] += jnp.sum(dadt_ref[...] * dtv * a, axis=0, keepdims=True)

    blk = pl.BlockSpec((tt, 128), lambda i: (i, 0))
    vec = pl.BlockSpec((1, 128), lambda i: (0, 0))
    return pl.pallas_call(
        body, name="dt_bwd", grid=(t // tt,),
        in_specs=[blk, blk, pl.BlockSpec((tt, 128), lambda i: (i, DT_BLK)), vec, vec],
        out_specs=[blk, vec, vec],
        out_shape=[jax.ShapeDtypeStruct((t, 128), BF16), jax.ShapeDtypeStruct((1, 128), F32),
                   jax.ShapeDtypeStruct((1, 128), F32)],
        compiler_params=_params(("arbitrary",), 16 << 20),
    )(dadt, dxdx, proj, bias128, alog128)


CONV_CB = 512
CONV_TT = 512


def _shift_down(cur, prev8, s):
    if s == 0:
        return cur
    rolled = pltpu.roll(cur, s, 0)
    head = pltpu.roll(prev8, s, 0)
    r8 = lax.broadcasted_iota(jnp.int32, (8, 1), 0)
    top = jnp.where(r8 < s, head, rolled[:8])
    return jnp.concatenate([top, rolled[8:]], axis=0)


def _shift_up(cur, next8, s):
    if s == 0:
        return cur
    n = cur.shape[0]
    rolled = pltpu.roll(cur, n - s, 0)
    tail = pltpu.roll(next8, 8 - s, 0)
    r8 = lax.broadcasted_iota(jnp.int32, (8, 1), 0)
    bot = jnp.where(r8 >= 8 - s, tail, rolled[n - 8:])
    return jnp.concatenate([rolled[:n - 8], bot], axis=0)


def _conv_fwd(proj, conv_w, conv_b):
    t = proj.shape[0]
    tt = _divtile(t, CONV_TT)
    base = SEGS["xbc"][2] // CONV_CB
    r8 = tt // 8

    def body(u_ref, up_ref, w_ref, b_ref, o_ref):
        cur = u_ref[...]
        prev8 = jnp.where(pl.program_id(1) > 0, up_ref[...], 0.0)
        acc = b_ref[...] + w_ref[3:4, :] * cur
        for k in range(3):
            acc = acc + w_ref[k:k + 1, :] * _shift_down(cur, prev8, 3 - k)
        o_ref[...] = acc * _sigmoid(acc)

    return pl.pallas_call(
        body, name="conv_fwd", grid=(XBC // CONV_CB, t // tt),
        in_specs=[pl.BlockSpec((tt, CONV_CB), lambda c, i: (i, base + c)),
                  pl.BlockSpec((8, CONV_CB), lambda c, i: (jnp.maximum(i * r8 - 1, 0), base + c)),
                  pl.BlockSpec((4, CONV_CB), lambda c, i: (0, c)),
                  pl.BlockSpec((1, CONV_CB), lambda c, i: (0, c))],
        out_specs=pl.BlockSpec((tt, CONV_CB), lambda c, i: (i, c)),
        out_shape=jax.ShapeDtypeStruct((t, XBC), F32),
        compiler_params=_params(("parallel", "parallel"), 24 << 20),
    )(proj, proj, conv_w, conv_b)


def _conv_bwd(proj, dout, conv_w, conv_b, col0, width, name, skip=None):
    t = proj.shape[0]
    tt = _divtile(t, CONV_TT)
    nt = t // tt
    base = SEGS["xbc"][2] // CONV_CB + col0 // CONV_CB
    wb = col0 // CONV_CB
    r8 = tt // 8
    has_skip = skip is not None

    def body(*refs):
        if has_skip:
            u_ref, up_ref, d_ref, w_ref, b_ref, sk_ref, skw_ref, du_ref, dw_ref, db_ref, nx_ref = refs
        else:
            u_ref, up_ref, d_ref, w_ref, b_ref, du_ref, dw_ref, db_ref, nx_ref = refs
        i = pl.program_id(1)
        cur = u_ref[...]
        prev8 = jnp.where(i < nt - 1, up_ref[...], 0.0)
        sh = [_shift_down(cur, prev8, 3 - k) for k in range(3)] + [cur]
        pre = b_ref[...]
        for k in range(4):
            pre = pre + w_ref[k:k + 1, :] * sh[k]
        sg = _sigmoid(pre)
        dout_v = d_ref[...]
        if has_skip:
            dout_v = dout_v + sk_ref[...] * skw_ref[...]
        dpre = dout_v * (sg * (1.0 + pre * (1.0 - sg)))

        @pl.when(i == 0)
        def _():
            nx_ref[...] = jnp.zeros_like(nx_ref)
            dw_ref[...] = jnp.zeros_like(dw_ref)
            db_ref[...] = jnp.zeros_like(db_ref)

        next8 = nx_ref[...]
        du = w_ref[3:4, :] * dpre
        for s in range(1, 4):
            du = du + w_ref[3 - s:4 - s, :] * _shift_up(dpre, next8, s)
        du_ref[...] = du.astype(BF16)
        nx_ref[...] = dpre[:8]
        rows = [jnp.sum(dpre * sh[k], axis=0, keepdims=True) for k in range(4)]
        dw_ref[...] += jnp.concatenate(rows + [jnp.zeros((4, CONV_CB), F32)], axis=0)
        db_ref[...] += jnp.sum(dpre, axis=0, keepdims=True)

    rev = lambda c, i: (nt - 1 - i, c)
    in_specs = [pl.BlockSpec((tt, CONV_CB), lambda c, i: (nt - 1 - i, base + c)),
                pl.BlockSpec((8, CONV_CB), lambda c, i: (jnp.maximum((nt - 1 - i) * r8 - 1, 0), base + c)),
                pl.BlockSpec((tt, CONV_CB), rev),
                pl.BlockSpec((4, CONV_CB), lambda c, i: (0, wb + c)),
                pl.BlockSpec((1, CONV_CB), lambda c, i: (0, wb + c))]
    args = [proj, proj, dout, conv_w, conv_b]
    if has_skip:
        in_specs += [pl.BlockSpec((tt, CONV_CB), rev), pl.BlockSpec((1, CONV_CB), lambda c, i: (0, c))]
        args += [skip[0], skip[1]]
    return pl.pallas_call(
        body, name=name, grid=(width // CONV_CB, nt),
        in_specs=in_specs,
        out_specs=[pl.BlockSpec((tt, CONV_CB), rev), pl.BlockSpec((8, CONV_CB), lambda c, i: (0, c)),
                   pl.BlockSpec((1, CONV_CB), lambda c, i: (0, c))],
        out_shape=[jax.ShapeDtypeStruct((t, width), BF16), jax.ShapeDtypeStruct((8, width), F32),
                   jax.ShapeDtypeStruct((1, width), F32)],
        scratch_shapes=[pltpu.VMEM((8, CONV_CB), F32)],
        compiler_params=_params(("parallel", "arbitrary"), 32 << 20),
    )(*args)


GW = SSD_R * SSD_P


def _expand8(v):
    r = v.shape[0]
    return jnp.concatenate([jnp.broadcast_to(v[:, h:h + 1], (r, SSD_P)) for h in range(SSD_R)], axis=1)


def _sel(rows, group):
    ri = lax.broadcasted_iota(jnp.int32, (rows, rows // group), 0)
    ci = lax.broadcasted_iota(jnp.int32, (rows, rows // group), 1)
    lo = ci * group
    return jnp.where((ri >= lo) & (ri < lo + group), 1.0, 0.0).astype(F32)


def _dot01(lhs, rhs, passes, split_lhs=True, dims=((1,), (0,))):
    val, m01 = (lhs, rhs) if split_lhs else (rhs, lhs)
    m01 = m01.astype(BF16)
    out = None
    for p in range(passes):
        piece = val.astype(BF16)
        ops = (piece, m01) if split_lhs else (m01, piece)
        d = lax.dot_general(ops[0], ops[1], (dims, ((), ())), preferred_element_type=F32)
        out = d if out is None else out + d
        if p + 1 < passes:
            val = val - piece.astype(F32)
    return out


def _ssd_chunk_terms(adt):
    li = lax.broadcasted_iota(jnp.int32, (CHUNK, CHUNK), 0)
    si = lax.broadcasted_iota(jnp.int32, (CHUNK, CHUNK), 1)
    causal = li >= si
    a_cs = _dot01(jnp.where(causal, 1.0, 0.0), adt, 3, split_lhs=False)
    a_cs_t = _dot01(adt, jnp.where(li <= si, 1.0, 0.0), 3, dims=((0,), (0,)))
    return a_cs, a_cs_t, causal


def _ssd_fwd(xc, dt3, adt3):
    t = xc.shape[0]
    nc = t // CHUNK

    def body(xs_ref, b_ref, c_ref, dt_ref, adt_ref, y_ref, hp_ref, h_ref):
        @pl.when(pl.program_id(1) == 0)
        def _():
            h_ref[...] = jnp.zeros_like(h_ref)

        a_cs, a_cs_t, causal = _ssd_chunk_terms(adt_ref[0])
        a_last = a_cs[CHUNK - 1:CHUNK, :]
        h = h_ref[...]
        hp_ref[0, 0] = h
        xd = xs_ref[...] * _expand8(dt_ref[0])
        bb = b_ref[...].astype(BF16)
        cbf = c_ref[...].astype(BF16)
        cb = lax.dot_general(cbf, bb, (((1,), (1,)), ((), ())), preferred_element_type=F32)
        yoff = jnp.dot(cbf, h.astype(BF16), preferred_element_type=F32) * _expand8(jnp.exp(a_cs))
        for r in range(SSD_R):
            seg = jnp.exp(jnp.where(causal, a_cs[:, r:r + 1] - a_cs_t[r:r + 1, :], -jnp.inf))
            lmat = (cb * seg).astype(BF16)
            sl = slice(SSD_P * r, SSD_P * (r + 1))
            y_ref[:, sl] = jnp.dot(lmat, xd[:, sl].astype(BF16), preferred_element_type=F32) + yoff[:, sl]
        xdd = (xd * _expand8(jnp.exp(a_last - a_cs))).astype(BF16)
        h_ref[...] = _expand8(jnp.exp(a_last)) * h + lax.dot_general(
            bb, xdd, (((0,), (0,)), ((), ())), preferred_element_type=F32)

    nb = SSD_INNER // SSD_N
    return pl.pallas_call(
        body, name="ssd_fwd", grid=(SSD_G, nc),
        in_specs=[pl.BlockSpec((CHUNK, GW), lambda g, c: (c, g)),
                  pl.BlockSpec((CHUNK, SSD_N), lambda g, c: (c, nb + g)),
                  pl.BlockSpec((CHUNK, SSD_N), lambda g, c: (c, nb + SSD_G + g)),
                  pl.BlockSpec((1, CHUNK, SSD_R), lambda g, c: (g, c, 0)),
                  pl.BlockSpec((1, CHUNK, SSD_R), lambda g, c: (g, c, 0))],
        out_specs=[pl.BlockSpec((CHUNK, GW), lambda g, c: (c, g)),
                   pl.BlockSpec((1, 1, SSD_N, GW), lambda g, c: (g, c, 0, 0))],
        out_shape=[jax.ShapeDtypeStruct((t, SSD_INNER), F32), jax.ShapeDtypeStruct((SSD_G, nc, SSD_N, GW), F32)],
        scratch_shapes=[pltpu.VMEM((SSD_N, GW), F32)],
        compiler_params=_params(("parallel", "arbitrary"), 32 << 20),
    )(xc, xc, xc, dt3, adt3)


def _ssd_bwd(xc, dt3, adt3, hprev, dy):
    t = xc.shape[0]
    nc = t // CHUNK

    def body(xs_ref, b_ref, c_ref, dt_ref, adt_ref, hp_ref, dy_ref,
             dx_ref, db_ref, dc_ref, dadt_ref, dxdx_ref, dh_ref):
        @pl.when(pl.program_id(1) == 0)
        def _():
            dh_ref[...] = jnp.zeros_like(dh_ref)

        a_cs, a_cs_t, causal = _ssd_chunk_terms(adt_ref[0])
        a_last = a_cs[CHUNK - 1:CHUNK, :]
        e_last = jnp.exp(a_last)
        ex = _expand8(jnp.exp(a_cs))
        dtex = _expand8(jnp.exp(a_last - a_cs))
        dtx = _expand8(dt_ref[0])
        xs = xs_ref[...]
        dyv = dy_ref[...]
        hp = hp_ref[0, 0]
        dh = dh_ref[...]
        sel = _sel(GW, SSD_P)
        seg8 = lambda v: _dot01(v, sel, 2)

        xd = xs * dtx
        xdd = xd * dtex
        bb = b_ref[...].astype(BF16)
        cbf = c_ref[...].astype(BF16)
        hpb = hp.astype(BF16)
        dhb = dh.astype(BF16)
        xdb = xd.astype(BF16)
        dyb = dyv.astype(BF16)
        cb = lax.dot_general(cbf, bb, (((1,), (1,)), ((), ())), preferred_element_type=F32)
        dye = (dyv * ex).astype(BF16)
        yoff = jnp.dot(cbf, hpb, preferred_element_type=F32) * ex
        dc = lax.dot_general(dye, hpb, (((1,), (1,)), ((), ())), preferred_element_type=F32)
        d_a = seg8(dyv * yoff)
        bdh = jnp.dot(bb, dhb, preferred_element_type=F32)
        db = lax.dot_general(xdd.astype(BF16), dhb, (((1,), (1,)), ((), ())), preferred_element_type=F32)
        dxd_state = bdh * dtex
        q = seg8(xdd * bdh)
        d_a = d_a - q
        d_a_last = (jnp.sum(q, axis=0, keepdims=True)
                    + e_last * seg8(jnp.sum(hp * dh, axis=0, keepdims=True)))
        dh_ref[...] = (lax.dot_general(cbf, dye, (((0,), (0,)), ((), ())), preferred_element_type=F32)
                       + _expand8(e_last) * dh)
        dcb = jnp.zeros((CHUNK, CHUNK), F32)
        w_all = []
        dxd_parts = []
        for r in range(SSD_R):
            seg = jnp.exp(jnp.where(causal, a_cs[:, r:r + 1] - a_cs_t[r:r + 1, :], -jnp.inf))
            lmat = cb * seg
            sl = slice(SSD_P * r, SSD_P * (r + 1))
            dm = lax.dot_general(dyb[:, sl], xdb[:, sl], (((1,), (1,)), ((), ())), preferred_element_type=F32)
            dxd_parts.append(lax.dot_general(lmat.astype(BF16), dyb[:, sl], (((0,), (0,)), ((), ())),
                                             preferred_element_type=F32))
            dcb = dcb + dm * seg
            w_all.append(dm * lmat)
        row_sums = _dot01(jnp.concatenate(w_all, axis=1), _sel(SSD_R * CHUNK, CHUNK), 2)
        col_sums = _dot01(jnp.concatenate(w_all, axis=0), _sel(SSD_R * CHUNK, CHUNK), 2, dims=((0,), (0,)))
        d_a = d_a + row_sums - col_sums
        li = lax.broadcasted_iota(jnp.int32, (CHUNK, SSD_R), 0)
        d_a = d_a + jnp.where(li == CHUNK - 1, d_a_last, 0.0)
        l2 = lax.broadcasted_iota(jnp.int32, (CHUNK, CHUNK), 0)
        s2 = lax.broadcasted_iota(jnp.int32, (CHUNK, CHUNK), 1)
        dadt_ref[0] = _dot01(jnp.where(s2 >= l2, 1.0, 0.0), d_a, 3, split_lhs=False)
        dxd = dxd_state + jnp.concatenate(dxd_parts, axis=1)
        dxdx_ref[0] = seg8(dxd * xs)
        dx_ref[...] = dxd * dtx
        dcbb = dcb.astype(BF16)
        db_ref[...] = db + lax.dot_general(dcbb, cbf, (((0,), (0,)), ((), ())), preferred_element_type=F32)
        dc_ref[...] = dc + jnp.dot(dcbb, bb, preferred_element_type=F32)

    nb = SSD_INNER // SSD_N
    rc = lambda g, c: (nc - 1 - c, g)
    r3 = lambda g, c: (g, nc - 1 - c, 0)
    return pl.pallas_call(
        body, name="ssd_bwd", grid=(SSD_G, nc),
        in_specs=[pl.BlockSpec((CHUNK, GW), rc),
                  pl.BlockSpec((CHUNK, SSD_N), lambda g, c: (nc - 1 - c, nb + g)),
                  pl.BlockSpec((CHUNK, SSD_N), lambda g, c: (nc - 1 - c, nb + SSD_G + g)),
                  pl.BlockSpec((1, CHUNK, SSD_R), r3),
                  pl.BlockSpec((1, CHUNK, SSD_R), r3),
                  pl.BlockSpec((1, 1, SSD_N, GW), lambda g, c: (g, nc - 1 - c, 0, 0)),
                  pl.BlockSpec((CHUNK, GW), rc)],
        out_specs=[pl.BlockSpec((CHUNK, GW), rc),
                   pl.BlockSpec((CHUNK, SSD_N), rc),
                   pl.BlockSpec((CHUNK, SSD_N), rc),
                   pl.BlockSpec((1, CHUNK, SSD_R), r3),
                   pl.BlockSpec((1, CHUNK, SSD_R), r3)],
        out_shape=[jax.ShapeDtypeStruct((t, SSD_INNER), F32),
                   jax.ShapeDtypeStruct((t, SSD_G * SSD_N), F32),
                   jax.ShapeDtypeStruct((t, SSD_G * SSD_N), F32),
                   jax.ShapeDtypeStruct((SSD_G, t, SSD_R), F32),
                   jax.ShapeDtypeStruct((SSD_G, t, SSD_R), F32)],
        scratch_shapes=[pltpu.VMEM((SSD_N, GW), F32)],
        compiler_params=_params(("parallel", "arbitrary"), 40 << 20),
    )(xc, xc, xc, dt3, adt3, hprev, dy)


def _gated_norm_fwd(y, xc, proj, dexp, ng):
    t = y.shape[0]
    tt = _divtile(t, 256)

    def body(y_ref, x_ref, z_ref, d_ref, g_ref, o_ref):
        z = z_ref[...]
        y2 = (y_ref[...] + d_ref[...] * x_ref[...]) * (z * _sigmoid(z))
        for gi in range(SSD_G):
            sl = slice(GW * gi, GW * (gi + 1))
            seg = y2[:, sl]
            rinv = lax.rsqrt(jnp.mean(seg * seg, axis=-1, keepdims=True) + RMS_EPS)
            o_ref[:, sl] = (seg * rinv * g_ref[:, sl]).astype(BF16)

    row = pl.BlockSpec((tt, SSD_INNER), lambda i: (i, 0))
    vec = pl.BlockSpec((1, SSD_INNER), lambda i: (0, 0))
    return pl.pallas_call(
        body, name="gated_norm_fwd", grid=(t // tt,), in_specs=[row, row, row, vec, vec], out_specs=row,
        out_shape=jax.ShapeDtypeStruct((t, SSD_INNER), BF16),
        compiler_params=_params(("parallel",), 48 << 20),
    )(y, xc, proj, dexp, ng)


def _gated_norm_bwd(dout, y, xc, proj, dexp, ng):
    t = y.shape[0]
    tt = _divtile(t, 128)

    def body(do_ref, y_ref, x_ref, z_ref, d_ref, g_ref, dz_ref, dy_ref, dg_ref, dd_ref):
        z = z_ref[...]
        sg = _sigmoid(z)
        sz = z * sg
        xs = x_ref[...]
        y1 = y_ref[...] + d_ref[...] * xs
        y2 = y1 * sz
        dov = do_ref[...]

        @pl.when(pl.program_id(0) == 0)
        def _():
            dg_ref[...] = jnp.zeros_like(dg_ref)
            dd_ref[...] = jnp.zeros_like(dd_ref)

        for gi in range(SSD_G):
            sl = slice(GW * gi, GW * (gi + 1))
            seg = y2[:, sl]
            rinv = lax.rsqrt(jnp.mean(seg * seg, axis=-1, keepdims=True) + RMS_EPS)
            yn = seg * rinv
            dsl = dov[:, sl]
            dg_ref[:, sl] += jnp.sum(dsl * yn, axis=0, keepdims=True)
            dyn = dsl * g_ref[:, sl]
            dy2 = rinv * (dyn - yn * jnp.mean(dyn * yn, axis=-1, keepdims=True))
            dz_ref[:, sl] = (dy2 * y1[:, sl] * (sg[:, sl] * (1.0 + z[:, sl] * (1.0 - sg[:, sl])))).astype(BF16)
            dy1 = dy2 * sz[:, sl]
            dy_ref[:, sl] = dy1
            dd_ref[:, sl] += jnp.sum(dy1 * xs[:, sl], axis=0, keepdims=True)

    row = pl.BlockSpec((tt, SSD_INNER), lambda i: (i, 0))
    vec = pl.BlockSpec((1, SSD_INNER), lambda i: (0, 0))
    return pl.pallas_call(
        body, name="gated_norm_bwd", grid=(t // tt,), in_specs=[row, row, row, row, vec, vec],
        out_specs=[row, row, vec, vec],
        out_shape=[jax.ShapeDtypeStruct((t, SSD_INNER), BF16), jax.ShapeDtypeStruct((t, SSD_INNER), F32),
                   jax.ShapeDtypeStruct((1, SSD_INNER), F32), jax.ShapeDtypeStruct((1, SSD_INNER), F32)],
        compiler_params=_params(("arbitrary",), 48 << 20),
    )(dout, y, xc, proj, dexp, ng)


def _fold_heads(v, name):
    def body(v_ref, o_ref):
        ri = lax.broadcasted_iota(jnp.int32, (SSD_INNER, 128), 0)
        ci = lax.broadcasted_iota(jnp.int32, (SSD_INNER, 128), 1)
        fold = jnp.where((ri >= ci * SSD_P) & (ri < (ci + 1) * SSD_P), 1.0, 0.0).astype(F32)
        o_ref[...] = jnp.dot(v_ref[...], fold, preferred_element_type=F32, precision=HI)

    return pl.pallas_call(body, name=name, out_shape=jax.ShapeDtypeStruct((1, 128), F32))(v)


Q_BLK = SEGS["q"][2] // QW
K_BLK = SEGS["k"][2] // KVW
V_BLK = SEGS["v"][2] // KVW


def _rope_tables(pos_ref, invf_ref, width):
    ang = pos_ref[...] * invf_ref[...]
    lane = lax.broadcasted_iota(jnp.int32, (1, 128), 1)
    sign = jnp.where((lane % HD) < (HD // 2), -1.0, 1.0)
    cos = jnp.tile(jnp.cos(ang), (1, width // 128))
    sin = jnp.tile(sign * jnp.sin(ang), (1, width // 128))
    first = (lax.broadcasted_iota(jnp.int32, (1, width), 1) % HD) < (HD // 2)
    return cos, sin, first


def _rot_half(u, first):
    w = u.shape[1]
    return jnp.where(first, pltpu.roll(u, w - HD // 2, 1), pltpu.roll(u, HD // 2, 1))


def _rope_fwd(proj, pos, invf):
    t = proj.shape[0]
    tt = _divtile(t, 512)

    def body(q_ref, k_ref, pos_ref, invf_ref, qo_ref, ko_ref):
        cos, sin, first = _rope_tables(pos_ref, invf_ref, QW)
        q = q_ref[...]
        qo_ref[...] = (q * cos + _rot_half(q, first) * sin).astype(BF16)
        k = k_ref[...]
        ko_ref[...] = (k * cos[:, :KVW] + _rot_half(k, first[:, :KVW]) * sin[:, :KVW]).astype(BF16)

    return pl.pallas_call(
        body, name="rope_fwd", grid=(t // tt,),
        in_specs=[pl.BlockSpec((tt, QW), lambda i: (i, Q_BLK)), pl.BlockSpec((tt, KVW), lambda i: (i, K_BLK)),
                  pl.BlockSpec((tt, 1), lambda i: (i, 0)), pl.BlockSpec((1, 128), lambda i: (0, 0))],
        out_specs=[pl.BlockSpec((tt, QW), lambda i: (i, 0)), pl.BlockSpec((tt, KVW), lambda i: (i, 0))],
        out_shape=[jax.ShapeDtypeStruct((t, QW), BF16), jax.ShapeDtypeStruct((t, KVW), BF16)],
        compiler_params=_params(("parallel",), 40 << 20),
    )(proj, proj, pos, invf)


def _rope_bwd(dq, dk, pos, invf):
    t = dq.shape[0]
    tt = _divtile(t, 512)

    def body(dq_ref, dk_ref, pos_ref, invf_ref, qo_ref, ko_ref):
        cos, sin, first = _rope_tables(pos_ref, invf_ref, QW)
        q = dq_ref[...]
        qo_ref[...] = (q * cos + _rot_half(q * sin, first)).astype(BF16)
        k = dk_ref[...]
        ko_ref[...] = (k * cos[:, :KVW] + _rot_half(k * sin[:, :KVW], first[:, :KVW])).astype(BF16)

    return pl.pallas_call(
        body, name="rope_bwd", grid=(t // tt,),
        in_specs=[pl.BlockSpec((tt, QW), lambda i: (i, 0)), pl.BlockSpec((tt, KVW), lambda i: (i, 0)),
                  pl.BlockSpec((tt, 1), lambda i: (i, 0)), pl.BlockSpec((1, 128), lambda i: (0, 0))],
        out_specs=[pl.BlockSpec((tt, QW), lambda i: (i, 0)), pl.BlockSpec((tt, KVW), lambda i: (i, 0))],
        out_shape=[jax.ShapeDtypeStruct((t, QW), BF16), jax.ShapeDtypeStruct((t, KVW), BF16)],
        compiler_params=_params(("parallel",), 40 << 20),
    )(dq, dk, pos, invf)


GQ = NQ // NKV
NT_DIMS = (((1,), (1,)), ((), ()))
TN_DIMS = (((0,), (0,)), ((), ()))


def _attn_stack(ref, j, dtype=None):
    parts = [ref[:, HD * h:HD * (h + 1)] for h in range(j * GQ, (j + 1) * GQ)]
    out = jnp.concatenate(parts, axis=0)
    return out if dtype is None else out.astype(dtype)


def _attn_sink_col(s_ref, j):
    return jnp.concatenate([jnp.broadcast_to(s_ref[:, h:h + 1], (WINDOW, 1)) for h in range(j * GQ, (j + 1) * GQ)],
                           axis=0)


def _attn_mask(n):
    qi = lax.broadcasted_iota(jnp.int32, (GQ * WINDOW, 2 * WINDOW), 0) % WINDOW
    kj = lax.broadcasted_iota(jnp.int32, (GQ * WINDOW, 2 * WINDOW), 1)
    return (kj > qi) & (kj <= qi + WINDOW) & ((n > 0) | (kj >= WINDOW))


def _attn_exp(qg, kk, sink, mask):
    s = jnp.where(mask, lax.dot_general(qg, kk, NT_DIMS, preferred_element_type=F32) * (HD ** -0.5), -jnp.inf)
    m = jnp.maximum(jnp.max(s, axis=-1, keepdims=True), sink)
    return jnp.exp(s - m), jnp.exp(sink - m)


def _attn_fwd(qr, kr, proj, sinks):
    t = qr.shape[0]
    nb = t // WINDOW

    def body(q_ref, kc_ref, kp_ref, vc_ref, vp_ref, s_ref, o_ref):
        mask = _attn_mask(pl.program_id(0))
        ones = jnp.ones((2 * WINDOW, HD), BF16)
        for j in range(NKV):
            ks = slice(HD * j, HD * (j + 1))
            kk = jnp.concatenate([kp_ref[:, ks], kc_ref[:, ks]], axis=0)
            vv = jnp.concatenate([vp_ref[:, ks], vc_ref[:, ks]], axis=0).astype(BF16)
            p, ps = _attn_exp(_attn_stack(q_ref, j), kk, _attn_sink_col(s_ref, j), mask)
            oa = jnp.dot(p.astype(BF16), jnp.concatenate([vv, ones], axis=1), preferred_element_type=F32)
            o = (oa[:, :HD] * (1.0 / (oa[:, HD:HD + 1] + ps))).astype(BF16)
            for g in range(GQ):
                h = j * GQ + g
                o_ref[:, HD * h:HD * (h + 1)] = o[WINDOW * g:WINDOW * (g + 1)]

    prev = lambda n: (jnp.maximum(n - 1, 0), 0)
    return pl.pallas_call(
        body, name="attn_fwd", grid=(nb,),
        in_specs=[pl.BlockSpec((WINDOW, QW), lambda n: (n, 0)),
                  pl.BlockSpec((WINDOW, KVW), lambda n: (n, 0)), pl.BlockSpec((WINDOW, KVW), prev),
                  pl.BlockSpec((WINDOW, KVW), lambda n: (n, V_BLK)),
                  pl.BlockSpec((WINDOW, KVW), lambda n: (jnp.maximum(n - 1, 0), V_BLK)),
                  pl.BlockSpec((1, 128), lambda n: (0, 0))],
        out_specs=pl.BlockSpec((WINDOW, QW), lambda n: (n, 0)),
        out_shape=jax.ShapeDtypeStruct((t, QW), BF16),
        compiler_params=_params(("parallel",), 24 << 20),
    )(qr, kr, kr, proj, proj, sinks)


def _attn_bwd(qr, kr, proj, sinks, do):
    t = qr.shape[0]
    nb = t // WINDOW

    def body(q_ref, kc_ref, kp_ref, vc_ref, vp_ref, s_ref, do_ref,
             dq_ref, dk_ref, dv_ref, ds_ref, dkc_ref, dvc_ref):
        i = pl.program_id(0)
        mask = _attn_mask(nb - 1 - i)

        @pl.when(i == 0)
        def _():
            dkc_ref[...] = jnp.zeros_like(dkc_ref)
            dvc_ref[...] = jnp.zeros_like(dvc_ref)
            ds_ref[...] = jnp.zeros_like(ds_ref)

        lane = lax.broadcasted_iota(jnp.int32, (1, 128), 1)
        ds_acc = jnp.zeros((1, 128), F32)
        for j in range(NKV):
            ks = slice(HD * j, HD * (j + 1))
            kk = jnp.concatenate([kp_ref[:, ks], kc_ref[:, ks]], axis=0)
            vv = jnp.concatenate([vp_ref[:, ks], vc_ref[:, ks]], axis=0).astype(BF16)
            qg = _attn_stack(q_ref, j)
            p, ps = _attn_exp(qg, kk, _attn_sink_col(s_ref, j), mask)
            inv = 1.0 / (jnp.sum(p, axis=-1, keepdims=True) + ps)
            pn = p * inv
            dog = _attn_stack(do_ref, j, BF16)
            dp = lax.dot_general(dog, vv, NT_DIMS, preferred_element_type=F32)
            delta = jnp.sum(dp * pn, axis=-1, keepdims=True)
            dsb = (pn * (dp - delta) * (HD ** -0.5)).astype(BF16)
            dsink = -(ps * inv) * delta
            dq = jnp.dot(dsb, kk, preferred_element_type=F32)
            for g in range(GQ):
                h = j * GQ + g
                rows = slice(WINDOW * g, WINDOW * (g + 1))
                dq_ref[:, HD * h:HD * (h + 1)] = dq[rows]
                ds_acc = ds_acc + jnp.where(lane == h, jnp.sum(dsink[rows], axis=0, keepdims=True), 0.0)
            dkk = lax.dot_general(dsb, qg, TN_DIMS, preferred_element_type=F32)
            dvv = lax.dot_general(pn.astype(BF16), dog, TN_DIMS, preferred_element_type=F32)
            dk_ref[:, ks] = dkk[WINDOW:] + dkc_ref[:, ks]
            dv_ref[:, ks] = (dvv[WINDOW:] + dvc_ref[:, ks]).astype(BF16)
            dkc_ref[:, ks] = dkk[:WINDOW]
            dvc_ref[:, ks] = dvv[:WINDOW]
        ds_ref[...] += ds_acc

    cur = lambda i: (nb - 1 - i, 0)
    prev = lambda i: (jnp.maximum(nb - 2 - i, 0), 0)
    return pl.pallas_call(
        body, name="attn_bwd", grid=(nb,),
        in_specs=[pl.BlockSpec((WINDOW, QW), cur),
                  pl.BlockSpec((WINDOW, KVW), cur), pl.BlockSpec((WINDOW, KVW), prev),
                  pl.BlockSpec((WINDOW, KVW), lambda i: (nb - 1 - i, V_BLK)),
                  pl.BlockSpec((WINDOW, KVW), lambda i: (jnp.maximum(nb - 2 - i, 0), V_BLK)),
                  pl.BlockSpec((1, 128), lambda i: (0, 0)),
                  pl.BlockSpec((WINDOW, QW), cur)],
        out_specs=[pl.BlockSpec((WINDOW, QW), cur), pl.BlockSpec((WINDOW, KVW), cur),
                   pl.BlockSpec((WINDOW, KVW), cur), pl.BlockSpec((1, 128), lambda i: (0, 0))],
        out_shape=[jax.ShapeDtypeStruct((t, QW), F32), jax.ShapeDtypeStruct((t, KVW), F32),
                   jax.ShapeDtypeStruct((t, KVW), BF16), jax.ShapeDtypeStruct((1, 128), F32)],
        scratch_shapes=[pltpu.VMEM((WINDOW, KVW), F32), pltpu.VMEM((WINDOW, KVW), F32)],
        compiler_params=_params(("arbitrary",), 32 << 20),
    )(qr, kr, kr, proj, proj, sinks, do)


GS_BLK = SEGS["gs"][2] // D
GA_BLK = SEGS["ga"][2] // D


def _merge_fwd(ys, ya, proj):
    t = ys.shape[0]
    tt = _divtile(t, 256)

    def body(ys_ref, ya_ref, gs_ref, ga_ref, o_ref):
        o_ref[...] = (_sigmoid(gs_ref[...]) * ys_ref[...] + _sigmoid(ga_ref[...]) * ya_ref[...]).astype(BF16)

    row = pl.BlockSpec((tt, D), lambda i: (i, 0))
    return pl.pallas_call(
        body, name="merge_fwd", grid=(t // tt,),
        in_specs=[row, row, pl.BlockSpec((tt, D), lambda i: (i, GS_BLK)), pl.BlockSpec((tt, D), lambda i: (i, GA_BLK))],
        out_specs=row, out_shape=jax.ShapeDtypeStruct((t, D), BF16),
        compiler_params=_params(("parallel",), 32 << 20),
    )(ys, ya, proj, proj)


def _merge_bwd(dm, ys, ya, proj):
    t = ys.shape[0]
    tt = _divtile(t, 256)

    def body(dm_ref, ys_ref, ya_ref, gs_ref, ga_ref, dys_ref, dya_ref, dgs_ref, dga_ref):
        d = dm_ref[...]
        s = _sigmoid(gs_ref[...])
        a = _sigmoid(ga_ref[...])
        dys_ref[...] = (d * s).astype(BF16)
        dya_ref[...] = (d * a).astype(BF16)
        dgs_ref[...] = (d * ys_ref[...] * (s * (1.0 - s))).astype(BF16)
        dga_ref[...] = (d * ya_ref[...] * (a * (1.0 - a))).astype(BF16)

    row = pl.BlockSpec((tt, D), lambda i: (i, 0))
    return pl.pallas_call(
        body, name="merge_bwd", grid=(t // tt,),
        in_specs=[row, row, row, pl.BlockSpec((tt, D), lambda i: (i, GS_BLK)),
                  pl.BlockSpec((tt, D), lambda i: (i, GA_BLK))],
        out_specs=[row, row, row, row], out_shape=[jax.ShapeDtypeStruct((t, D), BF16)] * 4,
        compiler_params=_params(("parallel",), 40 << 20),
    )(dm, ys, ya, proj, proj)


def _pad128(v):
    return jnp.pad(v, ((0, 0), (0, 128 - v.shape[1])))


def _group_major(v):
    t = v.shape[0]
    return jnp.transpose(v[:, :SSD_HEADS].reshape(t, SSD_G, SSD_R), (1, 0, 2))


def _token_major(v3):
    t = v3.shape[1]
    return _pad128(jnp.transpose(v3, (1, 0, 2)).reshape(t, SSD_HEADS))


def _local_step(x, pos, target, w, small):
    xb = x.astype(BF16)
    gu1 = _mm(xb, w["gu1"], "nn", F32, "ffn1_gu")
    a1 = _swiglu_fwd(gu1, "ffn1_act")
    f1 = _mm(a1, w["d1"], "nn", F32, "ffn1_down", caps=(1024, 1024, 1408))
    h1, h1b, xh1, rs1 = _ln_fwd(x, f1, small["ln1_g"], small["ln1_b"], 0.5, "ln1_fwd")
    proj = _mm(h1b, w["win"], "nn", F32, "proj", caps=(1024, 896, 2048))
    bias128 = _pad128(small["dt_bias"])
    alog128 = _pad128(small["a_log"])
    dt, adt = _dt_prep(proj, bias128, alog128)
    dt3, adt3 = _group_major(dt), _group_major(adt)
    xc = _conv_fwd(proj, small["conv_w"], small["conv_b"])
    y_ssd, hprev = _ssd_fwd(xc, dt3, adt3)
    dexp = jnp.repeat(small["d_skip"], SSD_P, axis=1)
    ysn = _gated_norm_fwd(y_ssd, xc, proj, dexp, small["ssd_norm_g"])
    ys = _mm(ysn, w["so"], "nn", F32, "ssd_out")
    invf = jnp.tile(ROPE_THETA ** (-jnp.arange(HD // 2, dtype=F32) * 2.0 / HD), 4)[None, :]
    qr, kr = _rope_fwd(proj, pos, invf)
    sinks128 = _pad128(small["attn_sinks"])
    o = _attn_fwd(qr, kr, proj, sinks128)
    ya = _mm(o, w["ao"], "nn", F32, "attn_out")
    mg = _merge_fwd(ys, ya, proj)
    mix = _mm(mg, w["out"], "nn", F32, "mix_out")
    h2, h2b, xh2, rs2 = _ln_fwd(h1, mix, small["ln2_g"], small["ln2_b"], 1.0, "ln2_fwd")
    gu2 = _mm(h2b, w["gu2"], "nn", F32, "ffn2_gu")
    a2 = _swiglu_fwd(gu2, "ffn2_act")
    f2 = _mm(a2, w["d2"], "nn", F32, "ffn2_down", caps=(1024, 1024, 1408))
    _, _, xh3, rs3, dh3, loss = _ln_fwd(h2, f2, small["ln3_g"], small["ln3_b"], 0.5, "ln3_fwd", target=target)

    gw, gs = {}, {}
    dr3, dr3h, gs["ln3_g"], gs["ln3_b"] = _ln_bwd(dh3, xh3, rs3, small["ln3_g"], 0.5, "ln3_bwd")
    gw["d2"] = _mm(a2, dr3h, "tn", F32, "ffn2_down_dw")
    da2 = _mm(dr3h, w["d2"], "nt", BF16, "ffn2_down_dx", caps=(1024, 1408, 2048))
    dgu2 = _swiglu_bwd(gu2, da2, "ffn2_act_bwd")
    gw["gu2"] = _mm(h2b, dgu2, "tn", F32, "ffn2_gu_dw", caps=(1024, 1408, 2048), n_slabs=N_CHIPS)
    dh2 = _mm(dgu2, w["gu2"], "nt", F32, "ffn2_gu_dx", add=dr3, add_scale=ALPHA, caps=(1024, 1024, 2816))
    dr2, dr2b, gs["ln2_g"], gs["ln2_b"] = _ln_bwd(dh2, xh2, rs2, small["ln2_g"], 1.0, "ln2_bwd")
    gw["out"] = _mm(mg, dr2b, "tn", F32, "mix_out_dw")
    dmg = _mm(dr2b, w["out"], "nt", F32, "mix_out_dx")
    dys, dya, dgs, dga = _merge_bwd(dmg, ys, ya, proj)
    gw["ao"] = _mm(o, dya, "tn", F32, "attn_out_dw")
    do = _mm(dya, w["ao"], "nt", BF16, "attn_out_dx")
    dqr, dkr, dv, gs["attn_sinks"] = _attn_bwd(qr, kr, proj, sinks128, do)
    dq, dk = _rope_bwd(dqr, dkr, pos, invf)
    gw["so"] = _mm(ysn, dys, "tn", F32, "ssd_out_dw")
    dysn = _mm(dys, w["so"], "nt", F32, "ssd_out_dx")
    dz, dy1, gs["ssd_norm_g"], dd_ch = _gated_norm_bwd(dysn, y_ssd, xc, proj, dexp, small["ssd_norm_g"])
    gs["d_skip"] = _fold_heads(dd_ch, "d_skip_fold")
    dxs, db, dc, dadt3, dxdx3 = _ssd_bwd(xc, dt3, adt3, hprev, dy1)
    ddt, gs["dt_bias"], gs["a_log"] = _dt_bwd(_token_major(dadt3), _token_major(dxdx3), proj, bias128, alog128)
    cw, cbias = small["conv_w"], small["conv_b"]
    dux, dwx, dbx = _conv_bwd(proj, dxs, cw, cbias, 0, SSD_INNER, "conv_bwd_x", skip=(dy1, dexp))
    dub, dwb, dbb = _conv_bwd(proj, db, cw, cbias, SSD_INNER, SSD_G * SSD_N, "conv_bwd_b")
    duc, dwc, dbc = _conv_bwd(proj, dc, cw, cbias, SSD_INNER + SSD_G * SSD_N, SSD_G * SSD_N, "conv_bwd_c")
    gs["conv_w"] = jnp.concatenate([dwx[:4], dwb[:4], dwc[:4]], axis=1)
    gs["conv_b"] = jnp.concatenate([dbx, dbb, dbc], axis=1)
    dproj = jnp.concatenate([dz, dq, dgs, dga, dux, dub, duc, dk, dv, ddt], axis=1)
    gw["win"] = _mm(h1b, dproj, "tn", F32, "proj_dw", caps=(1024, 896, 2048))
    dh1 = _mm(dproj, w["win"], "nt", F32, "proj_dx", add=dr2, add_scale=ALPHA, caps=(1024, 1024, 2432))
    dr1, dr1h, gs["ln1_g"], gs["ln1_b"] = _ln_bwd(dh1, xh1, rs1, small["ln1_g"], 0.5, "ln1_bwd")
    gw["d1"] = _mm(a1, dr1h, "tn", F32, "ffn1_down_dw")
    da1 = _mm(dr1h, w["d1"], "nt", BF16, "ffn1_down_dx", caps=(1024, 1408, 2048))
    dgu1 = _swiglu_bwd(gu1, da1, "ffn1_act_bwd")
    gw["gu1"] = _mm(xb, dgu1, "tn", F32, "ffn1_gu_dw", caps=(1024, 1408, 2048), n_slabs=N_CHIPS)
    grad_x = _mm(dgu1, w["gu1"], "nt", F32, "ffn1_gu_dx", add=dr1, add_scale=ALPHA, caps=(1024, 1024, 2816))
    return loss, grad_x, gw, gs


MESH = pl.DeviceIdType.MESH
ANY = pl.BlockSpec(memory_space=pl.ANY)


def _place():
    x, y, c = lax.axis_index("x"), lax.axis_index("y"), lax.axis_index("c")
    peers = [(1 - x, y), (x, 1 - y), (1 - x, 1 - y)]
    return x, y, c, peers


BIG = [
    ("ffn1_w_gate", D, SHARD_H, "gu1", "col", 0),
    ("ffn1_w_up", D, SHARD_H, "gu1", "col", SHARD_H),
    ("ffn1_w_down", SHARD_H, D, "d1", "row", 0),
    ("w_in", D, SHARD_IN, "win4", "lead", 0),
    ("w_ssd_o", SSD_INNER // N_CHIPS, D, "so", "row", 0),
    ("w_attn_o", D // N_CHIPS, D, "ao", "row", 0),
    ("w_out", D // N_CHIPS, D, "out", "row", 0),
    ("ffn2_w_gate", D, SHARD_H, "gu2", "col", 0),
    ("ffn2_w_up", D, SHARD_H, "gu2", "col", SHARD_H),
    ("ffn2_w_down", SHARD_H, D, "d2", "row", 0),
]
GATHERED = {"gu1": (D, 2 * FFN_H), "d1": (FFN_H, D), "win4": (N_CHIPS, D, SHARD_IN), "so": (SSD_INNER, D),
            "ao": (D, D), "out": (D, D), "gu2": (D, 2 * FFN_H), "d2": (FFN_H, D)}


def _cast_place(srcs, oname, chip_idx):
    rows, cols = srcs[0].shape
    tr = _divtile(rows, 256, 16)
    kind = [b[4] for b in BIG if b[3] == oname][0]

    def body(chip_ref, *refs):
        o_ref = refs[-1]
        for k, s_ref in enumerate(refs[:-1]):
            o_ref[:, k * cols:(k + 1) * cols] = s_ref[...].astype(BF16)

    nt = rows // tr
    if kind == "col":
        o_spec = pl.BlockSpec((tr, len(srcs) * cols), lambda i, chip_ref: (i, chip_ref[0]))
    elif kind == "row":
        o_spec = pl.BlockSpec((tr, cols), lambda i, chip_ref: (chip_ref[0] * nt + i, 0))
    else:
        o_spec = pl.BlockSpec((None, tr, cols), lambda i, chip_ref: (chip_ref[0], i, 0))
    return pl.pallas_call(
        body, name="cast_place_" + oname,
        grid_spec=pltpu.PrefetchScalarGridSpec(
            num_scalar_prefetch=1, grid=(nt,),
            in_specs=[pl.BlockSpec((tr, cols), lambda i, chip_ref: (i, 0))] * len(srcs), out_specs=o_spec),
        out_shape=jax.ShapeDtypeStruct(GATHERED[oname], BF16),
        compiler_params=_params(("parallel",), 32 << 20),
    )(chip_idx, *srcs)


def _gather_weights(placed):
    n = len(BIG)
    out_names = list(GATHERED)

    def body(*refs):
        outs = dict(zip(out_names, refs[len(out_names):2 * len(out_names)]))
        send, recv, fsend, frecv = refs[2 * len(out_names):]
        x, y, c, peers = _place()
        me = 2 * x + y

        def slot(i, j, half):
            _, rows, cols, oname, kind, off = BIG[i]
            o = outs[oname]
            hr = rows // 2
            if kind == "col":
                cs = pl.ds(pl.multiple_of(j * (2 * SHARD_H) + off, 128), cols)
                return o.at[pl.ds(pl.multiple_of(half * hr, 16), hr), cs]
            if kind == "row":
                return o.at[pl.ds(pl.multiple_of(j * rows + half * hr, 16), hr), :]
            return o.at[j, pl.ds(pl.multiple_of(half * hr, 16), hr), :]

        sends = []
        for i in range(n):
            for k, (px, py) in enumerate(peers):
                cp = pltpu.make_async_remote_copy(src_ref=slot(i, me, c), dst_ref=slot(i, me, c),
                                                  send_sem=send.at[3 * i + k], recv_sem=recv.at[3 * i + k],
                                                  device_id=(px, py, c), device_id_type=MESH)
                cp.start()
                sends.append(cp)
        fwds = []
        for i in range(n):
            for k, (px, py) in enumerate(peers):
                pj = 2 * px + py
                pltpu.make_async_remote_copy(src_ref=slot(i, pj, c), dst_ref=slot(i, pj, c),
                                             send_sem=send.at[3 * i + k], recv_sem=recv.at[3 * i + k],
                                             device_id=(px, py, c), device_id_type=MESH).wait_recv()
                cp = pltpu.make_async_remote_copy(src_ref=slot(i, pj, c), dst_ref=slot(i, pj, c),
                                                  send_sem=fsend.at[3 * i + k], recv_sem=frecv.at[3 * i + k],
                                                  device_id=(x, y, 1 - c), device_id_type=MESH)
                cp.start()
                fwds.append(cp)
        for i in range(n):
            for k, (px, py) in enumerate(peers):
                pj = 2 * px + py
                pltpu.make_async_remote_copy(src_ref=slot(i, pj, 1 - c), dst_ref=slot(i, pj, 1 - c),
                                             send_sem=fsend.at[3 * i + k], recv_sem=frecv.at[3 * i + k],
                                             device_id=(x, y, 1 - c), device_id_type=MESH).wait_recv()
        for cp in sends + fwds:
            cp.wait_send()

    outs = pl.pallas_call(
        body, name="gather_weights",
        in_specs=[ANY] * len(out_names), out_specs=[ANY] * len(out_names),
        out_shape=[jax.ShapeDtypeStruct(GATHERED[k], BF16) for k in out_names],
        input_output_aliases={i: i for i in range(len(out_names))},
        scratch_shapes=[pltpu.SemaphoreType.DMA((3 * n,))] * 4,
    )(*[placed[k] for k in out_names])
    return dict(zip(out_names, outs))


def _win_pieces():
    pieces = []
    for g0, wd, i0 in SEGS.values():
        for j in range(N_CHIPS):
            lo, hi = max(g0, j * SHARD_IN), min(g0 + wd, (j + 1) * SHARD_IN)
            if lo < hi:
                pieces.append((j, lo - j * SHARD_IN, hi - j * SHARD_IN, i0 + lo - g0))
    return pieces


def _win_to_internal(win4):
    tr = 128

    def body(i_ref, o_ref):
        for j, s0, s1, d0 in _win_pieces():
            o_ref[:, d0:d0 + s1 - s0] = i_ref[j, :, s0:s1]
        o_ref[:, PROJ_W:] = jnp.zeros((tr, PROJ_PAD - PROJ_W), o_ref.dtype)

    return pl.pallas_call(
        body, name="win_to_internal", grid=(D // tr,),
        in_specs=[pl.BlockSpec((N_CHIPS, tr, SHARD_IN), lambda i: (0, i, 0))],
        out_specs=pl.BlockSpec((tr, PROJ_PAD), lambda i: (i, 0)),
        out_shape=jax.ShapeDtypeStruct((D, PROJ_PAD), win4.dtype),
        compiler_params=_params(("parallel",), 40 << 20),
    )(win4)


def _win_from_internal(g):
    tr = 64

    def body(i_ref, o_ref):
        for j, s0, s1, d0 in _win_pieces():
            o_ref[j, :, s0:s1] = i_ref[:, d0:d0 + s1 - s0]

    return pl.pallas_call(
        body, name="win_from_internal", grid=(D // tr,),
        in_specs=[pl.BlockSpec((tr, PROJ_PAD), lambda i: (i, 0))],
        out_specs=pl.BlockSpec((N_CHIPS, tr, SHARD_IN), lambda i: (0, i, 0)),
        out_shape=jax.ShapeDtypeStruct((N_CHIPS, D, SHARD_IN), g.dtype),
        compiler_params=_params(("parallel",), 40 << 20),
    )(g)


def _rs_pair_exchange(grads):
    n = len(grads)

    def body(*refs):
        srcs, dsts = refs[:n], refs[n:2 * n]
        send, recv = refs[2 * n:]
        x, y, c, _ = _place()
        cps = []
        for i in range(n):
            hr = srcs[i].shape[1] // 2
            cp = pltpu.make_async_remote_copy(
                src_ref=srcs[i].at[:, pl.ds(pl.multiple_of((1 - c) * hr, 16), hr), :], dst_ref=dsts[i],
                send_sem=send.at[i], recv_sem=recv.at[i], device_id=(x, y, 1 - c), device_id_type=MESH)
            cp.start()
            cps.append(cp)
        for cp in cps:
            cp.wait()

    return pl.pallas_call(
        body, name="rs_pair_exchange", in_specs=[ANY] * n, out_specs=[ANY] * n,
        out_shape=[jax.ShapeDtypeStruct((g.shape[0], g.shape[1] // 2, g.shape[2]), F32) for g in grads],
        scratch_shapes=[pltpu.SemaphoreType.DMA((n,))] * 2,
    )(*grads)


def _half_tile(hr):
    return _divtile(hr, 256, 16) if hr % 256 == 0 else _divtile(hr, 512, 16)


def _rs_pair_sum(g, r, c_idx, name):
    ns, rows, cols = g.shape
    hr = rows // 2
    tr = _half_tile(hr)
    nt = hr // tr

    def body(c_ref, g_ref, r_ref, ob_ref, of_ref):
        s = g_ref[...] + r_ref[...]
        ob_ref[...] = s.astype(BF16)
        of_ref[...] = s

    blk = pl.BlockSpec((None, tr, cols), lambda j, t, c_ref: (j, t, 0))
    return pl.pallas_call(
        body, name=name,
        grid_spec=pltpu.PrefetchScalarGridSpec(
            num_scalar_prefetch=1, grid=(ns, nt),
            in_specs=[pl.BlockSpec((None, tr, cols), lambda j, t, c_ref: (j, c_ref[0] * nt + t, 0)), blk],
            out_specs=[blk, blk]),
        out_shape=[jax.ShapeDtypeStruct((ns, hr, cols), BF16), jax.ShapeDtypeStruct((ns, hr, cols), F32)],
        compiler_params=_params(("parallel", "parallel"), 48 << 20),
    )(c_idx, g, r)


def _rs_chip_exchange(parts):
    n = len(parts)

    def body(*refs):
        srcs, dsts = refs[:n], refs[n:2 * n]
        send, recv = refs[2 * n:]
        x, y, c, peers = _place()
        cps = []
        for i in range(n):
            for k, (px, py) in enumerate(peers):
                cp = pltpu.make_async_remote_copy(
                    src_ref=srcs[i].at[2 * px + py], dst_ref=dsts[i].at[k],
                    send_sem=send.at[3 * i + k], recv_sem=recv.at[3 * i + k],
                    device_id=(px, py, c), device_id_type=MESH)
                cp.start()
                cps.append(cp)
        for cp in cps:
            cp.wait()

    return pl.pallas_call(
        body, name="rs_chip_exchange", in_specs=[ANY] * n, out_specs=[ANY] * n,
        out_shape=[jax.ShapeDtypeStruct((3,) + p.shape[1:], BF16) for p in parts],
        scratch_shapes=[pltpu.SemaphoreType.DMA((3 * n,))] * 2,
    )(*parts)


def _rs_final_sum(own, got, place_idx, name):
    ns, hr, cols = own.shape
    tr = _half_tile(hr)
    nt = hr // tr

    def body(p_ref, o_ref, g_ref, out_ref):
        s = o_ref[...]
        for k in range(3):
            s = s + g_ref[k].astype(F32)
        out_ref[...] = s

    return pl.pallas_call(
        body, name=name,
        grid_spec=pltpu.PrefetchScalarGridSpec(
            num_scalar_prefetch=1, grid=(nt,),
            in_specs=[pl.BlockSpec((None, tr, cols), lambda t, p_ref: (p_ref[0], t, 0)),
                      pl.BlockSpec((3, tr, cols), lambda t, p_ref: (0, t, 0))],
            out_specs=pl.BlockSpec((tr, cols), lambda t, p_ref: (p_ref[1] * nt + t, 0))),
        out_shape=jax.ShapeDtypeStruct((2 * hr, cols), F32),
        compiler_params=_params(("parallel",), 48 << 20),
    )(place_idx, own, got)


def _rs_share_halves(fulls):
    n = len(fulls)

    def body(*refs):
        dsts = refs[n:2 * n]
        send, recv = refs[2 * n:]
        x, y, c, _ = _place()
        cps = []
        for i in range(n):
            hr = dsts[i].shape[0] // 2
            rows = dsts[i].at[pl.ds(pl.multiple_of(c * hr, 8), hr), :]
            cp = pltpu.make_async_remote_copy(src_ref=rows, dst_ref=rows, send_sem=send.at[i], recv_sem=recv.at[i],
                                              device_id=(x, y, 1 - c), device_id_type=MESH)
            cp.start()
            cps.append(cp)
        for i in range(n):
            hr = dsts[i].shape[0] // 2
            other = dsts[i].at[pl.ds(pl.multiple_of((1 - c) * hr, 8), hr), :]
            pltpu.make_async_remote_copy(src_ref=other, dst_ref=other, send_sem=send.at[i], recv_sem=recv.at[i],
                                         device_id=(x, y, 1 - c), device_id_type=MESH).wait_recv()
        for cp in cps:
            cp.wait_send()

    return pl.pallas_call(
        body, name="rs_share_halves", in_specs=[ANY] * n, out_specs=[ANY] * n,
        out_shape=[jax.ShapeDtypeStruct(f.shape, F32) for f in fulls],
        input_output_aliases={i: i for i in range(n)},
        scratch_shapes=[pltpu.SemaphoreType.DMA((n,))] * 2,
    )(*fulls)


def _all_reduce_small(v):
    rows = v.shape[0]

    def body(v_ref, o_ref, buf, send, recv):
        x, y, c, _ = _place()
        me = 4 * x + 2 * y + c
        buf[me] = v_ref[...]
        cps = []
        for d in range(1, 8):
            px, py, pc = x ^ (d >> 2), y ^ ((d >> 1) & 1), c ^ (d & 1)
            cp = pltpu.make_async_remote_copy(src_ref=v_ref, dst_ref=buf.at[me], send_sem=send.at[d - 1],
                                              recv_sem=recv.at[d - 1], device_id=(px, py, pc), device_id_type=MESH)
            cp.start()
            cps.append(cp)
        for d in range(1, 8):
            px, py, pc = x ^ (d >> 2), y ^ ((d >> 1) & 1), c ^ (d & 1)
            pltpu.make_async_remote_copy(src_ref=v_ref, dst_ref=buf.at[4 * px + 2 * py + pc], send_sem=send.at[d - 1],
                                         recv_sem=recv.at[d - 1], device_id=(px, py, pc),
                                         device_id_type=MESH).wait_recv()
        for cp in cps:
            cp.wait_send()
        acc = buf[0]
        for d in range(1, 8):
            acc = acc + buf[d]
        o_ref[...] = acc

    vm = pl.BlockSpec(memory_space=pltpu.VMEM)
    return pl.pallas_call(
        body, name="all_reduce_small", in_specs=[vm], out_specs=vm,
        out_shape=jax.ShapeDtypeStruct((rows, 128), F32),
        scratch_shapes=[pltpu.VMEM((8, rows, 128), F32), pltpu.SemaphoreType.DMA((7,)), pltpu.SemaphoreType.DMA((7,))],
    )(v)


def _adamw(w, g, m, v, name, g_col_blk=0):
    rows, cols = w.shape
    tr = _divtile(rows, max(8, (2 << 20) // (4 * cols) // 8 * 8), 8)

    def body(w_ref, g_ref, m_ref, v_ref, go_ref, d_ref, mo_ref, vo_ref):
        gv = g_ref[...]
        mn = ADAM_B1 * m_ref[...] + (1.0 - ADAM_B1) * gv
        vn = ADAM_B2 * v_ref[...] + (1.0 - ADAM_B2) * (gv * gv)
        m_hat = mn / (1.0 - ADAM_B1 ** ADAM_STEP)
        v_hat = vn / (1.0 - ADAM_B2 ** ADAM_STEP)
        go_ref[...] = gv
        d_ref[...] = -ADAM_LR * (m_hat / (jnp.sqrt(v_hat) + ADAM_EPS) + ADAM_WD * w_ref[...])
        mo_ref[...] = mn
        vo_ref[...] = vn

    blk = pl.BlockSpec((tr, cols), lambda i: (i, 0))
    return pl.pallas_call(
        body, name=name, grid=(rows // tr,),
        in_specs=[blk, pl.BlockSpec((tr, cols), lambda i: (i, g_col_blk)), blk, blk],
        out_specs=[blk] * 4, out_shape=[jax.ShapeDtypeStruct((rows, cols), F32)] * 4,
        compiler_params=_params(("parallel",), 48 << 20),
    )(w, g, m, v)


SMALL = ["ln1_g", "ln1_b", "conv_w", "conv_b", "dt_bias", "a_log", "d_skip", "ssd_norm_g", "attn_sinks",
         "ln2_g", "ln2_b", "ln3_g", "ln3_b"]


def _pack_rows(vs):
    parts = []
    for v in vs:
        v = v.reshape(-1)
        parts.append(jnp.pad(v, (0, (-v.shape[0]) % 128)))
    flat = jnp.concatenate(parts)
    flat = jnp.pad(flat, (0, (-flat.shape[0]) % 1024))
    return flat.reshape(-1, 128)


def _unpack_rows(packed, shapes):
    flat = packed.reshape(-1)
    out, at = [], 0
    for s in shapes:
        nel = int(np.prod(s))
        out.append(flat[at:at + nel].reshape(s))
        at += nel + (-nel) % 128
    return out


def kernel(x, positions, ffn1_w_gate, ffn1_w_up, ffn1_w_down, ln1_g, ln1_b, w_in, conv_w, conv_b, dt_bias, a_log, d_skip, ssd_norm_g, w_ssd_o, attn_sinks, w_attn_o, w_out, ln2_g, ln2_b, ffn2_w_gate, ffn2_w_up, ffn2_w_down, ln3_g, ln3_b, loss_target, m_ffn1_w_gate, m_ffn1_w_up, m_ffn1_w_down, m_ln1_g, m_ln1_b, m_w_in, m_conv_w, m_conv_b, m_dt_bias, m_a_log, m_d_skip, m_ssd_norm_g, m_w_ssd_o, m_attn_sinks, m_w_attn_o, m_w_out, m_ln2_g, m_ln2_b, m_ffn2_w_gate, m_ffn2_w_up, m_ffn2_w_down, m_ln3_g, m_ln3_b, v_ffn1_w_gate, v_ffn1_w_up, v_ffn1_w_down, v_ln1_g, v_ln1_b, v_w_in, v_conv_w, v_conv_b, v_dt_bias, v_a_log, v_d_skip, v_ssd_norm_g, v_w_ssd_o, v_attn_sinks, v_w_attn_o, v_w_out, v_ln2_g, v_ln2_b, v_ffn2_w_gate, v_ffn2_w_up, v_ffn2_w_down, v_ln3_g, v_ln3_b):
    args = dict(locals())
    wts = {n: args[n][0] for n in [b[0] for b in BIG] + SMALL}
    mom_m = {n: args["m_" + n][0] for n in wts}
    mom_v = {n: args["v_" + n][0] for n in wts}
    t = x.shape[1]
    xi, yi, ci = lax.axis_index("x"), lax.axis_index("y"), lax.axis_index("c")
    chip = 2 * xi + yi

    c_idx = ci.astype(jnp.int32).reshape(1)
    chip_idx = chip.astype(jnp.int32).reshape(1)
    place_idx = jnp.stack([chip, ci]).astype(jnp.int32)
    placed = {o: _cast_place([wts[b[0]] for b in BIG if b[3] == o], o, chip_idx) for o in GATHERED}
    gathered = _gather_weights(placed)
    w = {k: gathered[k] for k in ("gu1", "d1", "so", "ao", "out", "gu2", "d2")}
    w["win"] = _win_to_internal(gathered["win4"])
    cw_rows = _pack_rows([lax.dynamic_update_slice(jnp.zeros((4, XBC), F32), wts["conv_w"], (0, chip * (XBC // N_CHIPS)))])
    cw_rows = jnp.where(ci == 0, cw_rows, 0.0)
    conv_w_full = _all_reduce_small(cw_rows)[:4 * XBC // 128].reshape(4, XBC)

    small = {n: (wts[n][None, :] if wts[n].ndim == 1 else wts[n]) for n in SMALL}
    small["conv_w"] = conv_w_full
    loss, grad_x, gw, gs = _local_step(x[0], positions[0].astype(F32)[:, None], loss_target[0], w, small)

    slabs = [gw["gu1"], gw["d1"].reshape(N_CHIPS, SHARD_H, D), _win_from_internal(gw["win"]),
             gw["so"].reshape(N_CHIPS, SSD_INNER // N_CHIPS, D), gw["ao"].reshape(N_CHIPS, D // N_CHIPS, D),
             gw["out"].reshape(N_CHIPS, D // N_CHIPS, D), gw["gu2"], gw["d2"].reshape(N_CHIPS, SHARD_H, D)]
    names = ["gu1", "d1", "win", "so", "ao", "out", "gu2", "d2"]
    from_sib = _rs_pair_exchange(slabs)
    pair = [_rs_pair_sum(g, r, c_idx, "rs_pair_sum_" + nm) for g, r, nm in zip(slabs, from_sib, names)]
    got = _rs_chip_exchange([p[0] for p in pair])
    halves = [_rs_final_sum(p[1], gt, place_idx, "rs_final_sum_" + nm) for p, gt, nm in zip(pair, got, names)]
    full = dict(zip(names, _rs_share_halves(halves)))

    outs = {}
    big_src = {"ffn1_w_gate": ("gu1", 0), "ffn1_w_up": ("gu1", 1), "ffn1_w_down": ("d1", 0), "w_in": ("win", 0),
               "w_ssd_o": ("so", 0), "w_attn_o": ("ao", 0), "w_out": ("out", 0),
               "ffn2_w_gate": ("gu2", 0), "ffn2_w_up": ("gu2", 1), "ffn2_w_down": ("d2", 0)}
    for nm, (src, blk) in big_src.items():
        outs[nm] = _adamw(wts[nm], full[src], mom_m[nm], mom_v[nm], "adamw_" + nm, g_col_blk=blk)

    gvec = {n: gs[n] for n in SMALL}
    gvec["dt_bias"], gvec["a_log"], gvec["d_skip"] = gs["dt_bias"][:, :64], gs["a_log"][:, :64], gs["d_skip"][:, :64]
    gvec["attn_sinks"] = gs["attn_sinks"][:, :NQ]
    red = _all_reduce_small(_pack_rows([gvec[n] for n in SMALL] + [loss]))
    shapes = [(4, XBC) if n == "conv_w" else wts[n].shape for n in SMALL] + [(1,)]
    red_list = _unpack_rows(red, shapes)
    loss_out = red_list[-1].reshape(())
    gsm = dict(zip(SMALL, red_list[:-1]))
    gsm["conv_w"] = lax.dynamic_slice_in_dim(gsm["conv_w"], chip * (XBC // N_CHIPS), XBC // N_CHIPS, axis=1)
    sm_shapes = [wts[n].shape for n in SMALL]
    res = _adamw(_pack_rows([wts[n] for n in SMALL]), _pack_rows([gsm[n] for n in SMALL]),
                 _pack_rows([mom_m[n] for n in SMALL]), _pack_rows([mom_v[n] for n in SMALL]), "adamw_small")
    res = [_unpack_rows(r, sm_shapes) for r in res]
    for i, nm in enumerate(SMALL):
        outs[nm] = tuple(r[i] for r in res)

    order = ["ffn1_w_gate", "ffn1_w_up", "ffn1_w_down", "ln1_g", "ln1_b", "w_in", "conv_w", "conv_b", "dt_bias", "a_log",
             "d_skip", "ssd_norm_g", "w_ssd_o", "attn_sinks", "w_attn_o", "w_out", "ln2_g", "ln2_b",
             "ffn2_w_gate", "ffn2_w_up", "ffn2_w_down", "ln3_g", "ln3_b"]
    result = [loss_out, grad_x[None]]
    for kind in range(4):
        result += [outs[nm][kind][None] for nm in order]
    return tuple(result)
```

```python
import functools
import math

import numpy as np
import jax
import jax.numpy as jnp
from jax import lax
from jax.experimental import pallas as pl
from jax.experimental.pallas import tpu as pltpu

F32 = jnp.float32
BF16 = jnp.bfloat16
HI = lax.Precision.HIGHEST

D = 2048
FFN_H = 5632
SSD_INNER = 4096
SSD_HEADS = 64
SSD_P = 64
SSD_G = 8
SSD_R = 8
SSD_N = 128
CHUNK = 128
XBC = 6144
NQ = 32
NKV = 4
HD = 64
QW = 2048
KVW = 256
WINDOW = 128
ROPE_THETA = 10000.0
ALPHA = 2.0 ** 0.25
LN_EPS = 1e-5
RMS_EPS = 1e-5
PROJ_W = 16960
N_CHIPS = 4
SHARD_IN = PROJ_W // N_CHIPS
SHARD_H = FFN_H // N_CHIPS

SEGS = {
    "z": (0, 4096, 0),
    "xbc": (4096, 6144, 10240),
    "dt": (10240, 64, 16896),
    "q": (10304, 2048, 4096),
    "k": (12352, 256, 16384),
    "v": (12608, 256, 16640),
    "gs": (12864, 2048, 6144),
    "ga": (14912, 2048, 8192),
}
PROJ_PAD = 17024

ADAM_LR = 0.001
ADAM_B1 = 0.9
ADAM_B2 = 0.999
ADAM_EPS = 1e-08
ADAM_WD = 0.01
ADAM_STEP = 10

VMEM_CAP = 60 * 1024 * 1024


def _params(sem, vmem_bytes):
    return pltpu.CompilerParams(dimension_semantics=sem, vmem_limit_bytes=int(min(VMEM_CAP, vmem_bytes)))


def _divtile(n, cap, q=128):
    best = None
    for d in range(q, min(n, cap) + 1, q):
        if n % d == 0:
            best = d
    return n if best is None else best


def _sigmoid(x):
    return 1.0 / (1.0 + jnp.exp(-x))


def _mm(a, b, mode, out_dtype, name, add=None, add_scale=1.0, caps=(1024, 1024, 2048), n_slabs=1):
    if mode == "nn":
        (m, k), (k2, n) = a.shape, b.shape
    elif mode == "nt":
        (m, k), (n, k2) = a.shape, b.shape
    else:
        (k, m), (k2, n) = a.shape, b.shape
    assert k == k2, (a.shape, b.shape, mode)
    tm, tn, tk = _divtile(m, caps[0]), _divtile(n // n_slabs, caps[1]), _divtile(k, caps[2])
    nk = k // tk
    per_slab = n // n_slabs // tn
    dims = {"nn": ((1,), (0,)), "nt": ((1,), (1,)), "tn": ((0,), (0,))}[mode]
    has_add = add is not None

    def body(*refs):
        if has_add:
            a_ref, b_ref, add_ref, o_ref = refs[:4]
            scr = refs[4:]
        else:
            a_ref, b_ref, o_ref = refs[:3]
            add_ref = None
            scr = refs[3:]
        part = lax.dot_general(a_ref[...].astype(BF16), b_ref[...].astype(BF16), (dims, ((), ())),
                               preferred_element_type=F32)

        def finish(acc):
            if has_add:
                acc = acc + add_scale * add_ref[...].astype(F32)
            o_ref[...] = acc.astype(o_ref.dtype)

        if nk == 1:
            finish(part)
        else:
            acc_ref = scr[0]
            kk = pl.program_id(2)

            @pl.when(kk == 0)
            def _():
                acc_ref[...] = part

            @pl.when(kk > 0)
            def _():
                acc_ref[...] += part

            @pl.when(kk == nk - 1)
            def _():
                finish(acc_ref[...])

    if mode == "nn":
        a_spec = pl.BlockSpec((tm, tk), lambda i, j, kk: (i, kk))
        b_spec = pl.BlockSpec((tk, tn), lambda i, j, kk: (kk, j))
    elif mode == "nt":
        a_spec = pl.BlockSpec((tm, tk), lambda i, j, kk: (i, kk))
        b_spec = pl.BlockSpec((tn, tk), lambda i, j, kk: (j, kk))
    else:
        a_spec = pl.BlockSpec((tk, tm), lambda i, j, kk: (kk, i))
        b_spec = pl.BlockSpec((tk, tn), lambda i, j, kk: (kk, j))
    o_spec = pl.BlockSpec((tm, tn), lambda i, j, kk: (i, j))
    out_shape = jax.ShapeDtypeStruct((m, n), out_dtype)
    if n_slabs > 1:
        assert not has_add
        o_spec = pl.BlockSpec((None, tm, tn), lambda i, j, kk: (j // per_slab, i, j % per_slab))
        out_shape = jax.ShapeDtypeStruct((n_slabs, m, n // n_slabs), out_dtype)
    in_specs = [a_spec, b_spec] + ([o_spec] if has_add else [])
    args = (a, b) + ((add,) if has_add else ())
    osz = jnp.dtype(out_dtype).itemsize
    vmem = (2 * (tm * tk * a.dtype.itemsize + tk * tn * b.dtype.itemsize) + 2 * tm * tn * osz
            + (2 * tm * tn * add.dtype.itemsize if has_add else 0) + 2 * tm * tn * 4
            + 2 * (tm * tk + tk * tn) + (8 << 20))
    return pl.pallas_call(
        body, name=name, grid=(m // tm, n // tn, nk),
        in_specs=in_specs, out_specs=o_spec, out_shape=out_shape,
        scratch_shapes=[pltpu.VMEM((tm, tn), F32)] if nk > 1 else [],
        compiler_params=_params(("parallel", "parallel", "arbitrary"), vmem),
    )(*args)


def _swiglu_fwd(gu, name):
    t = gu.shape[0]
    tt = _divtile(t, 512)
    w = SHARD_H

    def body(gu_ref, a_ref):
        g = gu_ref[:, :w]
        u = gu_ref[:, w:]
        a_ref[...] = (g * _sigmoid(g) * u).astype(BF16)

    return pl.pallas_call(
        body, name=name, grid=(t // tt, N_CHIPS),
        in_specs=[pl.BlockSpec((tt, 2 * w), lambda i, j: (i, j))],
        out_specs=pl.BlockSpec((tt, w), lambda i, j: (i, j)),
        out_shape=jax.ShapeDtypeStruct((t, FFN_H), BF16),
        compiler_params=_params(("parallel", "parallel"), 40 << 20),
    )(gu)


def _swiglu_bwd(gu, da, name):
    t = gu.shape[0]
    tt = _divtile(t, 512)
    w = SHARD_H

    def body(gu_ref, da_ref, o_ref):
        g = gu_ref[:, :w]
        u = gu_ref[:, w:]
        d = da_ref[...].astype(F32)
        s = _sigmoid(g)
        o_ref[:, :w] = (d * u * (s * (1.0 + g * (1.0 - s)))).astype(BF16)
        o_ref[:, w:] = (d * (g * s)).astype(BF16)

    return pl.pallas_call(
        body, name=name, grid=(t // tt, N_CHIPS),
        in_specs=[pl.BlockSpec((tt, 2 * w), lambda i, j: (i, j)), pl.BlockSpec((tt, w), lambda i, j: (i, j))],
        out_specs=pl.BlockSpec((tt, 2 * w), lambda i, j: (i, j)),
        out_shape=jax.ShapeDtypeStruct((t, 2 * FFN_H), BF16),
        compiler_params=_params(("parallel", "parallel"), 40 << 20),
    )(gu, da)


def _ln_fwd(base, f, g, b, c, name, target=None):
    t = base.shape[0]
    tt = _divtile(t, 256)
    with_loss = target is not None

    def body(*refs):
        if with_loss:
            base_ref, f_ref, g_ref, b_ref, tg_ref, h_ref, hb_ref, xh_ref, rs_ref, dh_ref, loss_ref = refs
        else:
            base_ref, f_ref, g_ref, b_ref, h_ref, hb_ref, xh_ref, rs_ref = refs
        r = ALPHA * base_ref[...] + c * f_ref[...]
        mu = jnp.mean(r, axis=-1, keepdims=True)
        xc = r - mu
        var = jnp.mean(xc * xc, axis=-1, keepdims=True)
        rstd = lax.rsqrt(var + LN_EPS)
        xh = xc * rstd
        h = xh * g_ref[...] + b_ref[...]
        h_ref[...] = h
        hb_ref[...] = h.astype(BF16)
        xh_ref[...] = xh
        rs_ref[...] = rstd
        if with_loss:
            e = h - tg_ref[...]
            dh_ref[...] = e * (1.0 / D)
            part = 0.5 * jnp.sum(jnp.sum(e * e, axis=-1, keepdims=True) * (1.0 / D), axis=0, keepdims=True)

            @pl.when(pl.program_id(0) == 0)
            def _():
                loss_ref[...] = jnp.zeros_like(loss_ref)

            loss_ref[...] += part

    row = pl.BlockSpec((tt, D), lambda i: (i, 0))
    vec = pl.BlockSpec((1, D), lambda i: (0, 0))
    col = pl.BlockSpec((tt, 1), lambda i: (i, 0))
    in_specs = [row, row, vec, vec] + ([row] if with_loss else [])
    out_specs = [row, row, row, col] + ([row, pl.BlockSpec((1, 1), lambda i: (0, 0))] if with_loss else [])
    out_shape = [jax.ShapeDtypeStruct((t, D), F32), jax.ShapeDtypeStruct((t, D), BF16),
                 jax.ShapeDtypeStruct((t, D), F32), jax.ShapeDtypeStruct((t, 1), F32)]
    if with_loss:
        out_shape += [jax.ShapeDtypeStruct((t, D), F32), jax.ShapeDtypeStruct((1, 1), F32)]
    args = (base, f, g, b) + ((target,) if with_loss else ())
    return pl.pallas_call(
        body, name=name, grid=(t // tt,), in_specs=in_specs, out_specs=out_specs, out_shape=out_shape,
        compiler_params=_params(("arbitrary",) if with_loss else ("parallel",), 48 << 20),
    )(*args)


def _ln_bwd(dy, xh, rstd, g, c, name):
    t = dy.shape[0]
    tt = _divtile(t, 256)

    def body(dy_ref, xh_ref, rs_ref, g_ref, dr_ref, drb_ref, dg_ref, db_ref):
        dyv = dy_ref[...]
        xhv = xh_ref[...]
        dxh = dyv * g_ref[...]
        m1 = jnp.mean(dxh, axis=-1, keepdims=True)
        m2 = jnp.mean(dxh * xhv, axis=-1, keepdims=True)
        dr = rs_ref[...] * (dxh - m1 - xhv * m2)
        dr_ref[...] = dr
        drb_ref[...] = (c * dr).astype(BF16)

        @pl.when(pl.program_id(0) == 0)
        def _():
            dg_ref[...] = jnp.zeros_like(dg_ref)
            db_ref[...] = jnp.zeros_like(db_ref)

        dg_ref[...] += jnp.sum(dyv * xhv, axis=0, keepdims=True)
        db_ref[...] += jnp.sum(dyv, axis=0, keepdims=True)

    row = pl.BlockSpec((tt, D), lambda i: (i, 0))
    vec = pl.BlockSpec((1, D), lambda i: (0, 0))
    col = pl.BlockSpec((tt, 1), lambda i: (i, 0))
    return pl.pallas_call(
        body, name=name, grid=(t // tt,), in_specs=[row, row, col, vec], out_specs=[row, row, vec, vec],
        out_shape=[jax.ShapeDtypeStruct((t, D), F32), jax.ShapeDtypeStruct((t, D), BF16),
                   jax.ShapeDtypeStruct((1, D), F32), jax.ShapeDtypeStruct((1, D), F32)],
        compiler_params=_params(("arbitrary",), 40 << 20),
    )(dy, xh, rstd, g)


DT_BLK = SEGS["dt"][2] // 128


def _dt_prep(proj, bias128, alog128):
    t = proj.shape[0]
    tt = _divtile(t, 1024)

    def body(p_ref, bias_ref, alog_ref, dt_ref, adt_ref):
        dtv = jax.nn.softplus(p_ref[...] + bias_ref[...])
        dt_ref[...] = dtv
        adt_ref[...] = dtv * (-jnp.exp(alog_ref[...]))

    blk = pl.BlockSpec((tt, 128), lambda i: (i, 0))
    vec = pl.BlockSpec((1, 128), lambda i: (0, 0))
    return pl.pallas_call(
        body, name="dt_prep", grid=(t // tt,),
        in_specs=[pl.BlockSpec((tt, 128), lambda i: (i, DT_BLK)), vec, vec], out_specs=[blk, blk],
        out_shape=[jax.ShapeDtypeStruct((t, 128), F32)] * 2,
        compiler_params=_params(("parallel",), 16 << 20),
    )(proj, bias128, alog128)


def _dt_bwd(dadt, dxdx, proj, bias128, alog128):
    t = proj.shape[0]
    tt = _divtile(t, 1024)

    def body(dadt_ref, dxdx_ref, p_ref, bias_ref, alog_ref, o_ref, dbias_ref, dalog_ref):
        pre = p_ref[...] + bias_ref[...]
        dtv = jax.nn.softplus(pre)
        a = -jnp.exp(alog_ref[...])
        ddt = a * dadt_ref[...] + dxdx_ref[...]
        draw = ddt * _sigmoid(pre)
        o_ref[...] = draw.astype(BF16)

        @pl.when(pl.program_id(0) == 0)
        def _():
            dbias_ref[...] = jnp.zeros_like(dbias_ref)
            dalog_ref[...] = jnp.zeros_like(dalog_ref)

        dbias_ref[...] += jnp.sum(draw, axis=0, keepdims=True)
        dalog_ref[...] += jnp.sum(dadt_ref[...] * dtv * a, axis=0, keepdims=True)

    blk = pl.BlockSpec((tt, 128), lambda i: (i, 0))
    vec = pl.BlockSpec((1, 128), lambda i: (0, 0))
    return pl.pallas_call(
        body, name="dt_bwd", grid=(t // tt,),
        in_specs=[blk, blk, pl.BlockSpec((tt, 128), lambda i: (i, DT_BLK)), vec, vec],
        out_specs=[blk, vec, vec],
        out_shape=[jax.ShapeDtypeStruct((t, 128), BF16), jax.ShapeDtypeStruct((1, 128), F32),
                   jax.ShapeDtypeStruct((1, 128), F32)],
        compiler_params=_params(("arbitrary",), 16 << 20),
    )(dadt, dxdx, proj, bias128, alog128)


CONV_CB = 512
CONV_TT = 512


def _shift_down(cur, prev8, s):
    if s == 0:
        return cur
    rolled = pltpu.roll(cur, s, 0)
    head = pltpu.roll(prev8, s, 0)
    r8 = lax.broadcasted_iota(jnp.int32, (8, 1), 0)
    top = jnp.where(r8 < s, head, rolled[:8])
    return jnp.concatenate([top, rolled[8:]], axis=0)


def _shift_up(cur, next8, s):
    if s == 0:
        return cur
    n = cur.shape[0]
    rolled = pltpu.roll(cur, n - s, 0)
    tail = pltpu.roll(next8, 8 - s, 0)
    r8 = lax.broadcasted_iota(jnp.int32, (8, 1), 0)
    bot = jnp.where(r8 >= 8 - s, tail, rolled[n - 8:])
    return jnp.concatenate([rolled[:n - 8], bot], axis=0)


def _conv_fwd(proj, conv_w, conv_b):
    t = proj.shape[0]
    tt = _divtile(t, CONV_TT)
    base = SEGS["xbc"][2] // CONV_CB
    r8 = tt // 8

    def body(u_ref, up_ref, w_ref, b_ref, o_ref):
        cur = u_ref[...]
        prev8 = jnp.where(pl.program_id(1) > 0, up_ref[...], 0.0)
        acc = b_ref[...] + w_ref[3:4, :] * cur
        for k in range(3):
            acc = acc + w_ref[k:k + 1, :] * _shift_down(cur, prev8, 3 - k)
        o_ref[...] = acc * _sigmoid(acc)

    return pl.pallas_call(
        body, name="conv_fwd", grid=(XBC // CONV_CB, t // tt),
        in_specs=[pl.BlockSpec((tt, CONV_CB), lambda c, i: (i, base + c)),
                  pl.BlockSpec((8, CONV_CB), lambda c, i: (jnp.maximum(i * r8 - 1, 0), base + c)),
                  pl.BlockSpec((4, CONV_CB), lambda c, i: (0, c)),
                  pl.BlockSpec((1, CONV_CB), lambda c, i: (0, c))],
        out_specs=pl.BlockSpec((tt, CONV_CB), lambda c, i: (i, c)),
        out_shape=jax.ShapeDtypeStruct((t, XBC), F32),
        compiler_params=_params(("parallel", "parallel"), 24 << 20),
    )(proj, proj, conv_w, conv_b)


def _conv_bwd(proj, dout, conv_w, conv_b, col0, width, name, skip=None):
    t = proj.shape[0]
    tt = _divtile(t, CONV_TT)
    nt = t // tt
    base = SEGS["xbc"][2] // CONV_CB + col0 // CONV_CB
    wb = col0 // CONV_CB
    r8 = tt // 8
    has_skip = skip is not None

    def body(*refs):
        if has_skip:
            u_ref, up_ref, d_ref, w_ref, b_ref, sk_ref, skw_ref, du_ref, dw_ref, db_ref, nx_ref = refs
        else:
            u_ref, up_ref, d_ref, w_ref, b_ref, du_ref, dw_ref, db_ref, nx_ref = refs
        i = pl.program_id(1)
        cur = u_ref[...]
        prev8 = jnp.where(i < nt - 1, up_ref[...], 0.0)
        sh = [_shift_down(cur, prev8, 3 - k) for k in range(3)] + [cur]
        pre = b_ref[...]
        for k in range(4):
            pre = pre + w_ref[k:k + 1, :] * sh[k]
        sg = _sigmoid(pre)
        dout_v = d_ref[...]
        if has_skip:
            dout_v = dout_v + sk_ref[...] * skw_ref[...]
        dpre = dout_v * (sg * (1.0 + pre * (1.0 - sg)))

        @pl.when(i == 0)
        def _():
            nx_ref[...] = jnp.zeros_like(nx_ref)
            dw_ref[...] = jnp.zeros_like(dw_ref)
            db_ref[...] = jnp.zeros_like(db_ref)

        next8 = nx_ref[...]
        du = w_ref[3:4, :] * dpre
        for s in range(1, 4):
            du = du + w_ref[3 - s:4 - s, :] * _shift_up(dpre, next8, s)
        du_ref[...] = du.astype(BF16)
        nx_ref[...] = dpre[:8]
        rows = [jnp.sum(dpre * sh[k], axis=0, keepdims=True) for k in range(4)]
        dw_ref[...] += jnp.concatenate(rows + [jnp.zeros((4, CONV_CB), F32)], axis=0)
        db_ref[...] += jnp.sum(dpre, axis=0, keepdims=True)

    rev = lambda c, i: (nt - 1 - i, c)
    in_specs = [pl.BlockSpec((tt, CONV_CB), lambda c, i: (nt - 1 - i, base + c)),
                pl.BlockSpec((8, CONV_CB), lambda c, i: (jnp.maximum((nt - 1 - i) * r8 - 1, 0), base + c)),
                pl.BlockSpec((tt, CONV_CB), rev),
                pl.BlockSpec((4, CONV_CB), lambda c, i: (0, wb + c)),
                pl.BlockSpec((1, CONV_CB), lambda c, i: (0, wb + c))]
    args = [proj, proj, dout, conv_w, conv_b]
    if has_skip:
        in_specs += [pl.BlockSpec((tt, CONV_CB), rev), pl.BlockSpec((1, CONV_CB), lambda c, i: (0, c))]
        args += [skip[0], skip[1]]
    return pl.pallas_call(
        body, name=name, grid=(width // CONV_CB, nt),
        in_specs=in_specs,
        out_specs=[pl.BlockSpec((tt, CONV_CB), rev), pl.BlockSpec((8, CONV_CB), lambda c, i: (0, c)),
                   pl.BlockSpec((1, CONV_CB), lambda c, i: (0, c))],
        out_shape=[jax.ShapeDtypeStruct((t, width), BF16), jax.ShapeDtypeStruct((8, width), F32),
                   jax.ShapeDtypeStruct((1, width), F32)],
        scratch_shapes=[pltpu.VMEM((8, CONV_CB), F32)],
        compiler_params=_params(("parallel", "arbitrary"), 32 << 20),
    )(*args)


GW = SSD_R * SSD_P


def _expand8(v):
    r = v.shape[0]
    return jnp.concatenate([jnp.broadcast_to(v[:, h:h + 1], (r, SSD_P)) for h in range(SSD_R)], axis=1)


def _sel(rows, group):
    ri = lax.broadcasted_iota(jnp.int32, (rows, rows // group), 0)
    ci = lax.broadcasted_iota(jnp.int32, (rows, rows // group), 1)
    lo = ci * group
    return jnp.where((ri >= lo) & (ri < lo + group), 1.0, 0.0).astype(F32)


def _dot01(lhs, rhs, passes, split_lhs=True, dims=((1,), (0,))):
    val, m01 = (lhs, rhs) if split_lhs else (rhs, lhs)
    m01 = m01.astype(BF16)
    out = None
    for p in range(passes):
        piece = val.astype(BF16)
        ops = (piece, m01) if split_lhs else (m01, piece)
        d = lax.dot_general(ops[0], ops[1], (dims, ((), ())), preferred_element_type=F32)
        out = d if out is None else out + d
        if p + 1 < passes:
            val = val - piece.astype(F32)
    return out


def _ssd_chunk_terms(adt):
    li = lax.broadcasted_iota(jnp.int32, (CHUNK, CHUNK), 0)
    si = lax.broadcasted_iota(jnp.int32, (CHUNK, CHUNK), 1)
    causal = li >= si
    a_cs = _dot01(jnp.where(causal, 1.0, 0.0), adt, 3, split_lhs=False)
    a_cs_t = _dot01(adt, jnp.where(li <= si, 1.0, 0.0), 3, dims=((0,), (0,)))
    return a_cs, a_cs_t, causal


def _ssd_fwd(xc, dt3, adt3):
    t = xc.shape[0]
    nc = t // CHUNK

    def body(xs_ref, b_ref, c_ref, dt_ref, adt_ref, y_ref, hp_ref, h_ref):
        @pl.when(pl.program_id(1) == 0)
        def _():
            h_ref[...] = jnp.zeros_like(h_ref)

        a_cs, a_cs_t, causal = _ssd_chunk_terms(adt_ref[0])
        a_last = a_cs[CHUNK - 1:CHUNK, :]
        h = h_ref[...]
        hp_ref[0, 0] = h
        xd = xs_ref[...] * _expand8(dt_ref[0])
        bb = b_ref[...].astype(BF16)
        cbf = c_ref[...].astype(BF16)
        cb = lax.dot_general(cbf, bb, (((1,), (1,)), ((), ())), preferred_element_type=F32)
        yoff = jnp.dot(cbf, h.astype(BF16), preferred_element_type=F32) * _expand8(jnp.exp(a_cs))
        for r in range(SSD_R):
            seg = jnp.exp(jnp.where(causal, a_cs[:, r:r + 1] - a_cs_t[r:r + 1, :], -jnp.inf))
            lmat = (cb * seg).astype(BF16)
            sl = slice(SSD_P * r, SSD_P * (r + 1))
            y_ref[:, sl] = jnp.dot(lmat, xd[:, sl].astype(BF16), preferred_element_type=F32) + yoff[:, sl]
        xdd = (xd * _expand8(jnp.exp(a_last - a_cs))).astype(BF16)
        h_ref[...] = _expand8(jnp.exp(a_last)) * h + lax.dot_general(
            bb, xdd, (((0,), (0,)), ((), ())), preferred_element_type=F32)

    nb = SSD_INNER // SSD_N
    return pl.pallas_call(
        body, name="ssd_fwd", grid=(SSD_G, nc),
        in_specs=[pl.BlockSpec((CHUNK, GW), lambda g, c: (c, g)),
                  pl.BlockSpec((CHUNK, SSD_N), lambda g, c: (c, nb + g)),
                  pl.BlockSpec((CHUNK, SSD_N), lambda g, c: (c, nb + SSD_G + g)),
                  pl.BlockSpec((1, CHUNK, SSD_R), lambda g, c: (g, c, 0)),
                  pl.BlockSpec((1, CHUNK, SSD_R), lambda g, c: (g, c, 0))],
        out_specs=[pl.BlockSpec((CHUNK, GW), lambda g, c: (c, g)),
                   pl.BlockSpec((1, 1, SSD_N, GW), lambda g, c: (g, c, 0, 0))],
        out_shape=[jax.ShapeDtypeStruct((t, SSD_INNER), F32), jax.ShapeDtypeStruct((SSD_G, nc, SSD_N, GW), F32)],
        scratch_shapes=[pltpu.VMEM((SSD_N, GW), F32)],
        compiler_params=_params(("parallel", "arbitrary"), 32 << 20),
    )(xc, xc, xc, dt3, adt3)


def _ssd_bwd(xc, dt3, adt3, hprev, dy):
    t = xc.shape[0]
    nc = t // CHUNK

    def body(xs_ref, b_ref, c_ref, dt_ref, adt_ref, hp_ref, dy_ref,
             dx_ref, db_ref, dc_ref, dadt_ref, dxdx_ref, dh_ref):
        @pl.when(pl.program_id(1) == 0)
        def _():
            dh_ref[...] = jnp.zeros_like(dh_ref)

        a_cs, a_cs_t, causal = _ssd_chunk_terms(adt_ref[0])
        a_last = a_cs[CHUNK - 1:CHUNK, :]
        e_last = jnp.exp(a_last)
        ex = _expand8(jnp.exp(a_cs))
        dtex = _expand8(jnp.exp(a_last - a_cs))
        dtx = _expand8(dt_ref[0])
        xs = xs_ref[...]
        dyv = dy_ref[...]
        hp = hp_ref[0, 0]
        dh = dh_ref[...]
        sel = _sel(GW, SSD_P)
        seg8 = lambda v: _dot01(v, sel, 2)

        xd = xs * dtx
        xdd = xd * dtex
        bb = b_ref[...].astype(BF16)
        cbf = c_ref[...].astype(BF16)
        hpb = hp.astype(BF16)
        dhb = dh.astype(BF16)
        xdb = xd.astype(BF16)
        dyb = dyv.astype(BF16)
        cb = lax.dot_general(cbf, bb, (((1,), (1,)), ((), ())), preferred_element_type=F32)
        dye = (dyv * ex).astype(BF16)
        yoff = jnp.dot(cbf, hpb, preferred_element_type=F32) * ex
        dc = lax.dot_general(dye, hpb, (((1,), (1,)), ((), ())), preferred_element_type=F32)
        d_a = seg8(dyv * yoff)
        bdh = jnp.dot(bb, dhb, preferred_element_type=F32)
        db = lax.dot_general(xdd.astype(BF16), dhb, (((1,), (1,)), ((), ())), preferred_element_type=F32)
        dxd_state = bdh * dtex
        q = seg8(xdd * bdh)
        d_a = d_a - q
        d_a_last = (jnp.sum(q, axis=0, keepdims=True)
                    + e_last * seg8(jnp.sum(hp * dh, axis=0, keepdims=True)))
        dh_ref[...] = (lax.dot_general(cbf, dye, (((0,), (0,)), ((), ())), preferred_element_type=F32)
                       + _expand8(e_last) * dh)
        dcb = jnp.zeros((CHUNK, CHUNK), F32)
        w_all = []
        dxd_parts = []
        for r in range(SSD_R):
            seg = jnp.exp(jnp.where(causal, a_cs[:, r:r + 1] - a_cs_t[r:r + 1, :], -jnp.inf))
            lmat = cb * seg
            sl = slice(SSD_P * r, SSD_P * (r + 1))
            dm = lax.dot_general(dyb[:, sl], xdb[:, sl], (((1,), (1,)), ((), ())), preferred_element_type=F32)
            dxd_parts.append(lax.dot_general(lmat.astype(BF16), dyb[:, sl], (((0,), (0,)), ((), ())),
                                             preferred_element_type=F32))
            dcb = dcb + dm * seg
            w_all.append(dm * lmat)
        row_sums = _dot01(jnp.concatenate(w_all, axis=1), _sel(SSD_R * CHUNK, CHUNK), 2)
        col_sums = _dot01(jnp.concatenate(w_all, axis=0), _sel(SSD_R * CHUNK, CHUNK), 2, dims=((0,), (0,)))
        d_a = d_a + row_sums - col_sums
        li = lax.broadcasted_iota(jnp.int32, (CHUNK, SSD_R), 0)
        d_a = d_a + jnp.where(li == CHUNK - 1, d_a_last, 0.0)
        l2 = lax.broadcasted_iota(jnp.int32, (CHUNK, CHUNK), 0)
        s2 = lax.broadcasted_iota(jnp.int32, (CHUNK, CHUNK), 1)
        dadt_ref[0] = _dot01(jnp.where(s2 >= l2, 1.0, 0.0), d_a, 3, split_lhs=False)
        dxd = dxd_state + jnp.concatenate(dxd_parts, axis=1)
        dxdx_ref[0] = seg8(dxd * xs)
        dx_ref[...] = dxd * dtx
        dcbb = dcb.astype(BF16)
        db_ref[...] = db + lax.dot_general(dcbb, cbf, (((0,), (0,)), ((), ())), preferred_element_type=F32)
        dc_ref[...] = dc + jnp.dot(dcbb, bb, preferred_element_type=F32)

    nb = SSD_INNER // SSD_N
    rc = lambda g, c: (nc - 1 - c, g)
    r3 = lambda g, c: (g, nc - 1 - c, 0)
    return pl.pallas_call(
        body, name="ssd_bwd", grid=(SSD_G, nc),
        in_specs=[pl.BlockSpec((CHUNK, GW), rc),
                  pl.BlockSpec((CHUNK, SSD_N), lambda g, c: (nc - 1 - c, nb + g)),
                  pl.BlockSpec((CHUNK, SSD_N), lambda g, c: (nc - 1 - c, nb + SSD_G + g)),
                  pl.BlockSpec((1, CHUNK, SSD_R), r3),
                  pl.BlockSpec((1, CHUNK, SSD_R), r3),
                  pl.BlockSpec((1, 1, SSD_N, GW), lambda g, c: (g, nc - 1 - c, 0, 0)),
                  pl.BlockSpec((CHUNK, GW), rc)],
        out_specs=[pl.BlockSpec((CHUNK, GW), rc),
                   pl.BlockSpec((CHUNK, SSD_N), rc),
                   pl.BlockSpec((CHUNK, SSD_N), rc),
                   pl.BlockSpec((1, CHUNK, SSD_R), r3),
                   pl.BlockSpec((1, CHUNK, SSD_R), r3)],
        out_shape=[jax.ShapeDtypeStruct((t, SSD_INNER), F32),
                   jax.ShapeDtypeStruct((t, SSD_G * SSD_N), F32),
                   jax.ShapeDtypeStruct((t, SSD_G * SSD_N), F32),
                   jax.ShapeDtypeStruct((SSD_G, t, SSD_R), F32),
                   jax.ShapeDtypeStruct((SSD_G, t, SSD_R), F32)],
        scratch_shapes=[pltpu.VMEM((SSD_N, GW), F32)],
        compiler_params=_params(("parallel", "arbitrary"), 40 << 20),
    )(xc, xc, xc, dt3, adt3, hprev, dy)


def _gated_norm_fwd(y, xc, proj, dexp, ng):
    t = y.shape[0]
    tt = _divtile(t, 256)

    def body(y_ref, x_ref, z_ref, d_ref, g_ref, o_ref):
        z = z_ref[...]
        y2 = (y_ref[...] + d_ref[...] * x_ref[...]) * (z * _sigmoid(z))
        for gi in range(SSD_G):
            sl = slice(GW * gi, GW * (gi + 1))
            seg = y2[:, sl]
            rinv = lax.rsqrt(jnp.mean(seg * seg, axis=-1, keepdims=True) + RMS_EPS)
            o_ref[:, sl] = (seg * rinv * g_ref[:, sl]).astype(BF16)

    row = pl.BlockSpec((tt, SSD_INNER), lambda i: (i, 0))
    vec = pl.BlockSpec((1, SSD_INNER), lambda i: (0, 0))
    return pl.pallas_call(
        body, name="gated_norm_fwd", grid=(t // tt,), in_specs=[row, row, row, vec, vec], out_specs=row,
        out_shape=jax.ShapeDtypeStruct((t, SSD_INNER), BF16),
        compiler_params=_params(("parallel",), 48 << 20),
    )(y, xc, proj, dexp, ng)


def _gated_norm_bwd(dout, y, xc, proj, dexp, ng):
    t = y.shape[0]
    tt = _divtile(t, 128)

    def body(do_ref, y_ref, x_ref, z_ref, d_ref, g_ref, dz_ref, dy_ref, dg_ref, dd_ref):
        z = z_ref[...]
        sg = _sigmoid(z)
        sz = z * sg
        xs = x_ref[...]
        y1 = y_ref[...] + d_ref[...] * xs
        y2 = y1 * sz
        dov = do_ref[...]

        @pl.when(pl.program_id(0) == 0)
        def _():
            dg_ref[...] = jnp.zeros_like(dg_ref)
            dd_ref[...] = jnp.zeros_like(dd_ref)

        for gi in range(SSD_G):
            sl = slice(GW * gi, GW * (gi + 1))
            seg = y2[:, sl]
            rinv = lax.rsqrt(jnp.mean(seg * seg, axis=-1, keepdims=True) + RMS_EPS)
            yn = seg * rinv
            dsl = dov[:, sl]
            dg_ref[:, sl] += jnp.sum(dsl * yn, axis=0, keepdims=True)
            dyn = dsl * g_ref[:, sl]
            dy2 = rinv * (dyn - yn * jnp.mean(dyn * yn, axis=-1, keepdims=True))
            dz_ref[:, sl] = (dy2 * y1[:, sl] * (sg[:, sl] * (1.0 + z[:, sl] * (1.0 - sg[:, sl])))).astype(BF16)
            dy1 = dy2 * sz[:, sl]
            dy_ref[:, sl] = dy1
            dd_ref[:, sl] += jnp.sum(dy1 * xs[:, sl], axis=0, keepdims=True)

    row = pl.BlockSpec((tt, SSD_INNER), lambda i: (i, 0))
    vec = pl.BlockSpec((1, SSD_INNER), lambda i: (0, 0))
    return pl.pallas_call(
        body, name="gated_norm_bwd", grid=(t // tt,), in_specs=[row, row, row, row, vec, vec],
        out_specs=[row, row, vec, vec],
        out_shape=[jax.ShapeDtypeStruct((t, SSD_INNER), BF16), jax.ShapeDtypeStruct((t, SSD_INNER), F32),
                   jax.ShapeDtypeStruct((1, SSD_INNER), F32), jax.ShapeDtypeStruct((1, SSD_INNER), F32)],
        compiler_params=_params(("arbitrary",), 48 << 20),
    )(dout, y, xc, proj, dexp, ng)


def _fold_heads(v, name):
    def body(v_ref, o_ref):
        ri = lax.broadcasted_iota(jnp.int32, (SSD_INNER, 128), 0)
        ci = lax.broadcasted_iota(jnp.int32, (SSD_INNER, 128), 1)
        fold = jnp.where((ri >= ci * SSD_P) & (ri < (ci + 1) * SSD_P), 1.0, 0.0).astype(F32)
        o_ref[...] = jnp.dot(v_ref[...], fold, preferred_element_type=F32, precision=HI)

    return pl.pallas_call(body, name=name, out_shape=jax.ShapeDtypeStruct((1, 128), F32))(v)


Q_BLK = SEGS["q"][2] // QW
K_BLK = SEGS["k"][2] // KVW
V_BLK = SEGS["v"][2] // KVW


def _rope_tables(pos_ref, invf_ref, width):
    ang = pos_ref[...] * invf_ref[...]
    lane = lax.broadcasted_iota(jnp.int32, (1, 128), 1)
    sign = jnp.where((lane % HD) < (HD // 2), -1.0, 1.0)
    cos = jnp.tile(jnp.cos(ang), (1, width // 128))
    sin = jnp.tile(sign * jnp.sin(ang), (1, width // 128))
    first = (lax.broadcasted_iota(jnp.int32, (1, width), 1) % HD) < (HD // 2)
    return cos, sin, first


def _rot_half(u, first):
    w = u.shape[1]
    return jnp.where(first, pltpu.roll(u, w - HD // 2, 1), pltpu.roll(u, HD // 2, 1))


def _rope_fwd(proj, pos, invf):
    t = proj.shape[0]
    tt = _divtile(t, 512)

    def body(q_ref, k_ref, pos_ref, invf_ref, qo_ref, ko_ref):
        cos, sin, first = _rope_tables(pos_ref, invf_ref, QW)
        q = q_ref[...]
        qo_ref[...] = (q * cos + _rot_half(q, first) * sin).astype(BF16)
        k = k_ref[...]
        ko_ref[...] = (k * cos[:, :KVW] + _rot_half(k, first[:, :KVW]) * sin[:, :KVW]).astype(BF16)

    return pl.pallas_call(
        body, name="rope_fwd", grid=(t // tt,),
        in_specs=[pl.BlockSpec((tt, QW), lambda i: (i, Q_BLK)), pl.BlockSpec((tt, KVW), lambda i: (i, K_BLK)),
                  pl.BlockSpec((tt, 1), lambda i: (i, 0)), pl.BlockSpec((1, 128), lambda i: (0, 0))],
        out_specs=[pl.BlockSpec((tt, QW), lambda i: (i, 0)), pl.BlockSpec((tt, KVW), lambda i: (i, 0))],
        out_shape=[jax.ShapeDtypeStruct((t, QW), BF16), jax.ShapeDtypeStruct((t, KVW), BF16)],
        compiler_params=_params(("parallel",), 40 << 20),
    )(proj, proj, pos, invf)


def _rope_bwd(dq, dk, pos, invf):
    t = dq.shape[0]
    tt = _divtile(t, 512)

    def body(dq_ref, dk_ref, pos_ref, invf_ref, qo_ref, ko_ref):
        cos, sin, first = _rope_tables(pos_ref, invf_ref, QW)
        q = dq_ref[...]
        qo_ref[...] = (q * cos + _rot_half(q * sin, first)).astype(BF16)
        k = dk_ref[...]
        ko_ref[...] = (k * cos[:, :KVW] + _rot_half(k * sin[:, :KVW], first[:, :KVW])).astype(BF16)

    return pl.pallas_call(
        body, name="rope_bwd", grid=(t // tt,),
        in_specs=[pl.BlockSpec((tt, QW), lambda i: (i, 0)), pl.BlockSpec((tt, KVW), lambda i: (i, 0)),
                  pl.BlockSpec((tt, 1), lambda i: (i, 0)), pl.BlockSpec((1, 128), lambda i: (0, 0))],
        out_specs=[pl.BlockSpec((tt, QW), lambda i: (i, 0)), pl.BlockSpec((tt, KVW), lambda i: (i, 0))],
        out_shape=[jax.ShapeDtypeStruct((t, QW), BF16), jax.ShapeDtypeStruct((t, KVW), BF16)],
        compiler_params=_params(("parallel",), 40 << 20),
    )(dq, dk, pos, invf)


GQ = NQ // NKV
NT_DIMS = (((1,), (1,)), ((), ()))
TN_DIMS = (((0,), (0,)), ((), ()))


def _attn_stack(ref, j, dtype=None):
    parts = [ref[:, HD * h:HD * (h + 1)] for h in range(j * GQ, (j + 1) * GQ)]
    out = jnp.concatenate(parts, axis=0)
    return out if dtype is None else out.astype(dtype)


def _attn_sink_col(s_ref, j):
    return jnp.concatenate([jnp.broadcast_to(s_ref[:, h:h + 1], (WINDOW, 1)) for h in range(j * GQ, (j + 1) * GQ)],
                           axis=0)


def _attn_mask(n):
    qi = lax.broadcasted_iota(jnp.int32, (GQ * WINDOW, 2 * WINDOW), 0) % WINDOW
    kj = lax.broadcasted_iota(jnp.int32, (GQ * WINDOW, 2 * WINDOW), 1)
    return (kj > qi) & (kj <= qi + WINDOW) & ((n > 0) | (kj >= WINDOW))


def _attn_exp(qg, kk, sink, mask):
    s = jnp.where(mask, lax.dot_general(qg, kk, NT_DIMS, preferred_element_type=F32) * (HD ** -0.5), -jnp.inf)
    m = jnp.maximum(jnp.max(s, axis=-1, keepdims=True), sink)
    return jnp.exp(s - m), jnp.exp(sink - m)


def _attn_fwd(qr, kr, proj, sinks):
    t = qr.shape[0]
    nb = t // WINDOW

    def body(q_ref, kc_ref, kp_ref, vc_ref, vp_ref, s_ref, o_ref):
        mask = _attn_mask(pl.program_id(0))
        ones = jnp.ones((2 * WINDOW, HD), BF16)
        for j in range(NKV):
            ks = slice(HD * j, HD * (j + 1))
            kk = jnp.concatenate([kp_ref[:, ks], kc_ref[:, ks]], axis=0)
            vv = jnp.concatenate([vp_ref[:, ks], vc_ref[:, ks]], axis=0).astype(BF16)
            p, ps = _attn_exp(_attn_stack(q_ref, j), kk, _attn_sink_col(s_ref, j), mask)
            oa = jnp.dot(p.astype(BF16), jnp.concatenate([vv, ones], axis=1), preferred_element_type=F32)
            o = (oa[:, :HD] * (1.0 / (oa[:, HD:HD + 1] + ps))).astype(BF16)
            for g in range(GQ):
                h = j * GQ + g
                o_ref[:, HD * h:HD * (h + 1)] = o[WINDOW * g:WINDOW * (g + 1)]

    prev = lambda n: (jnp.maximum(n - 1, 0), 0)
    return pl.pallas_call(
        body, name="attn_fwd", grid=(nb,),
        in_specs=[pl.BlockSpec((WINDOW, QW), lambda n: (n, 0)),
                  pl.BlockSpec((WINDOW, KVW), lambda n: (n, 0)), pl.BlockSpec((WINDOW, KVW), prev),
                  pl.BlockSpec((WINDOW, KVW), lambda n: (n, V_BLK)),
                  pl.BlockSpec((WINDOW, KVW), lambda n: (jnp.maximum(n - 1, 0), V_BLK)),
                  pl.BlockSpec((1, 128), lambda n: (0, 0))],
        out_specs=pl.BlockSpec((WINDOW, QW), lambda n: (n, 0)),
        out_shape=jax.ShapeDtypeStruct((t, QW), BF16),
        compiler_params=_params(("parallel",), 24 << 20),
    )(qr, kr, kr, proj, proj, sinks)


def _attn_bwd(qr, kr, proj, sinks, do):
    t = qr.shape[0]
    nb = t // WINDOW

    def body(q_ref, kc_ref, kp_ref, vc_ref, vp_ref, s_ref, do_ref,
             dq_ref, dk_ref, dv_ref, ds_ref, dkc_ref, dvc_ref):
        i = pl.program_id(0)
        mask = _attn_mask(nb - 1 - i)

        @pl.when(i == 0)
        def _():
            dkc_ref[...] = jnp.zeros_like(dkc_ref)
            dvc_ref[...] = jnp.zeros_like(dvc_ref)
            ds_ref[...] = jnp.zeros_like(ds_ref)

        lane = lax.broadcasted_iota(jnp.int32, (1, 128), 1)
        ds_acc = jnp.zeros((1, 128), F32)
        for j in range(NKV):
            ks = slice(HD * j, HD * (j + 1))
            kk = jnp.concatenate([kp_ref[:, ks], kc_ref[:, ks]], axis=0)
            vv = jnp.concatenate([vp_ref[:, ks], vc_ref[:, ks]], axis=0).astype(BF16)
            qg = _attn_stack(q_ref, j)
            p, ps = _attn_exp(qg, kk, _attn_sink_col(s_ref, j), mask)
            inv = 1.0 / (jnp.sum(p, axis=-1, keepdims=True) + ps)
            pn = p * inv
            dog = _attn_stack(do_ref, j, BF16)
            dp = lax.dot_general(dog, vv, NT_DIMS, preferred_element_type=F32)
            delta = jnp.sum(dp * pn, axis=-1, keepdims=True)
            dsb = (pn * (dp - delta) * (HD ** -0.5)).astype(BF16)
            dsink = -(ps * inv) * delta
            dq = jnp.dot(dsb, kk, preferred_element_type=F32)
            for g in range(GQ):
                h = j * GQ + g
                rows = slice(WINDOW * g, WINDOW * (g + 1))
                dq_ref[:, HD * h:HD * (h + 1)] = dq[rows]
                ds_acc = ds_acc + jnp.where(lane == h, jnp.sum(dsink[rows], axis=0, keepdims=True), 0.0)
            dkk = lax.dot_general(dsb, qg, TN_DIMS, preferred_element_type=F32)
            dvv = lax.dot_general(pn.astype(BF16), dog, TN_DIMS, preferred_element_type=F32)
            dk_ref[:, ks] = dkk[WINDOW:] + dkc_ref[:, ks]
            dv_ref[:, ks] = (dvv[WINDOW:] + dvc_ref[:, ks]).astype(BF16)
            dkc_ref[:, ks] = dkk[:WINDOW]
            dvc_ref[:, ks] = dvv[:WINDOW]
        ds_ref[...] += ds_acc

    cur = lambda i: (nb - 1 - i, 0)
    prev = lambda i: (jnp.maximum(nb - 2 - i, 0), 0)
    return pl.pallas_call(
        body, name="attn_bwd", grid=(nb,),
        in_specs=[pl.BlockSpec((WINDOW, QW), cur),
                  pl.BlockSpec((WINDOW, KVW), cur), pl.BlockSpec((WINDOW, KVW), prev),
                  pl.BlockSpec((WINDOW, KVW), lambda i: (nb - 1 - i, V_BLK)),
                  pl.BlockSpec((WINDOW, KVW), lambda i: (jnp.maximum(nb - 2 - i, 0), V_BLK)),
                  pl.BlockSpec((1, 128), lambda i: (0, 0)),
                  pl.BlockSpec((WINDOW, QW), cur)],
        out_specs=[pl.BlockSpec((WINDOW, QW), cur), pl.BlockSpec((WINDOW, KVW), cur),
                   pl.BlockSpec((WINDOW, KVW), cur), pl.BlockSpec((1, 128), lambda i: (0, 0))],
        out_shape=[jax.ShapeDtypeStruct((t, QW), F32), jax.ShapeDtypeStruct((t, KVW), F32),
                   jax.ShapeDtypeStruct((t, KVW), BF16), jax.ShapeDtypeStruct((1, 128), F32)],
        scratch_shapes=[pltpu.VMEM((WINDOW, KVW), F32), pltpu.VMEM((WINDOW, KVW), F32)],
        compiler_params=_params(("arbitrary",), 32 << 20),
    )(qr, kr, kr, proj, proj, sinks, do)


GS_BLK = SEGS["gs"][2] // D
GA_BLK = SEGS["ga"][2] // D


def _merge_fwd(ys, ya, proj):
    t = ys.shape[0]
    tt = _divtile(t, 256)

    def body(ys_ref, ya_ref, gs_ref, ga_ref, o_ref):
        o_ref[...] = (_sigmoid(gs_ref[...]) * ys_ref[...] + _sigmoid(ga_ref[...]) * ya_ref[...]).astype(BF16)

    row = pl.BlockSpec((tt, D), lambda i: (i, 0))
    return pl.pallas_call(
        body, name="merge_fwd", grid=(t // tt,),
        in_specs=[row, row, pl.BlockSpec((tt, D), lambda i: (i, GS_BLK)), pl.BlockSpec((tt, D), lambda i: (i, GA_BLK))],
        out_specs=row, out_shape=jax.ShapeDtypeStruct((t, D), BF16),
        compiler_params=_params(("parallel",), 32 << 20),
    )(ys, ya, proj, proj)


def _merge_bwd(dm, ys, ya, proj):
    t = ys.shape[0]
    tt = _divtile(t, 256)

    def body(dm_ref, ys_ref, ya_ref, gs_ref, ga_ref, dys_ref, dya_ref, dgs_ref, dga_ref):
        d = dm_ref[...]
        s = _sigmoid(gs_ref[...])
        a = _sigmoid(ga_ref[...])
        dys_ref[...] = (d * s).astype(BF16)
        dya_ref[...] = (d * a).astype(BF16)
        dgs_ref[...] = (d * ys_ref[...] * (s * (1.0 - s))).astype(BF16)
        dga_ref[...] = (d * ya_ref[...] * (a * (1.0 - a))).astype(BF16)

    row = pl.BlockSpec((tt, D), lambda i: (i, 0))
    return pl.pallas_call(
        body, name="merge_bwd", grid=(t // tt,),
        in_specs=[row, row, row, pl.BlockSpec((tt, D), lambda i: (i, GS_BLK)),
                  pl.BlockSpec((tt, D), lambda i: (i, GA_BLK))],
        out_specs=[row, row, row, row], out_shape=[jax.ShapeDtypeStruct((t, D), BF16)] * 4,
        compiler_params=_params(("parallel",), 40 << 20),
    )(dm, ys, ya, proj, proj)


def _pad128(v):
    return jnp.pad(v, ((0, 0), (0, 128 - v.shape[1])))


def _group_major(v):
    t = v.shape[0]
    return jnp.transpose(v[:, :SSD_HEADS].reshape(t, SSD_G, SSD_R), (1, 0, 2))


def _token_major(v3):
    t = v3.shape[1]
    return _pad128(jnp.transpose(v3, (1, 0, 2)).reshape(t, SSD_HEADS))


def _local_step(x, pos, target, w, small, start_token=None, later_weights=None, early_grads=None):
    xb = x.astype(BF16) if start_token is None else (x + start_token[0:1, 0:1]).astype(BF16)
    gu1 = _mm(xb, w["gu1"], "nn", F32, "ffn1_gu")
    a1 = _swiglu_fwd(gu1, "ffn1_act")
    f1 = _mm(a1, w["d1"], "nn", F32, "ffn1_down", caps=(1024, 1024, 1408))
    h1, h1b, xh1, rs1 = _ln_fwd(x, f1, small["ln1_g"], small["ln1_b"], 0.5, "ln1_fwd")
    if later_weights is not None:
        w = {**w, **later_weights(h1b)}
    proj = _mm(h1b, w["win"], "nn", F32, "proj", caps=(1024, 896, 2048))
    bias128 = _pad128(small["dt_bias"])
    alog128 = _pad128(small["a_log"])
    dt, adt = _dt_prep(proj, bias128, alog128)
    dt3, adt3 = _group_major(dt), _group_major(adt)
    xc = _conv_fwd(proj, small["conv_w"], small["conv_b"])
    y_ssd, hprev = _ssd_fwd(xc, dt3, adt3)
    dexp = jnp.repeat(small["d_skip"], SSD_P, axis=1)
    ysn = _gated_norm_fwd(y_ssd, xc, proj, dexp, small["ssd_norm_g"])
    ys = _mm(ysn, w["so"], "nn", F32, "ssd_out")
    invf = jnp.tile(ROPE_THETA ** (-jnp.arange(HD // 2, dtype=F32) * 2.0 / HD), 4)[None, :]
    qr, kr = _rope_fwd(proj, pos, invf)
    sinks128 = _pad128(small["attn_sinks"])
    o = _attn_fwd(qr, kr, proj, sinks128)
    ya = _mm(o, w["ao"], "nn", F32, "attn_out")
    mg = _merge_fwd(ys, ya, proj)
    mix = _mm(mg, w["out"], "nn", F32, "mix_out")
    h2, h2b, xh2, rs2 = _ln_fwd(h1, mix, small["ln2_g"], small["ln2_b"], 1.0, "ln2_fwd")
    gu2 = _mm(h2b, w["gu2"], "nn", F32, "ffn2_gu")
    a2 = _swiglu_fwd(gu2, "ffn2_act")
    f2 = _mm(a2, w["d2"], "nn", F32, "ffn2_down", caps=(1024, 1024, 1408))
    _, _, xh3, rs3, dh3, loss = _ln_fwd(h2, f2, small["ln3_g"], small["ln3_b"], 0.5, "ln3_fwd", target=target)

    gw, gs = {}, {}
    dr3, dr3h, gs["ln3_g"], gs["ln3_b"] = _ln_bwd(dh3, xh3, rs3, small["ln3_g"], 0.5, "ln3_bwd")
    gw["d2"] = _mm(a2, dr3h, "tn", F32, "ffn2_down_dw")
    da2 = _mm(dr3h, w["d2"], "nt", BF16, "ffn2_down_dx", caps=(1024, 1408, 2048))
    dgu2 = _swiglu_bwd(gu2, da2, "ffn2_act_bwd")
    gw["gu2"] = _mm(h2b, dgu2, "tn", F32, "ffn2_gu_dw", caps=(1024, 1408, 2048), n_slabs=N_CHIPS)
    dh2 = _mm(dgu2, w["gu2"], "nt", F32, "ffn2_gu_dx", add=dr3, add_scale=ALPHA, caps=(1024, 1024, 2816))
    dr2, dr2b, gs["ln2_g"], gs["ln2_b"] = _ln_bwd(dh2, xh2, rs2, small["ln2_g"], 1.0, "ln2_bwd")
    gw["out"] = _mm(mg, dr2b, "tn", F32, "mix_out_dw")
    dmg = _mm(dr2b, w["out"], "nt", F32, "mix_out_dx")
    dys, dya, dgs, dga = _merge_bwd(dmg, ys, ya, proj)
    gw["ao"] = _mm(o, dya, "tn", F32, "attn_out_dw")
    do = _mm(dya, w["ao"], "nt", BF16, "attn_out_dx")
    dqr, dkr, dv, gs["attn_sinks"] = _attn_bwd(qr, kr, proj, sinks128, do)
    dq, dk = _rope_bwd(dqr, dkr, pos, invf)
    gw["so"] = _mm(ysn, dys, "tn", F32, "ssd_out_dw")
    dysn = _mm(dys, w["so"], "nt", F32, "ssd_out_dx")
    dz, dy1, gs["ssd_norm_g"], dd_ch = _gated_norm_bwd(dysn, y_ssd, xc, proj, dexp, small["ssd_norm_g"])
    gs["d_skip"] = _fold_heads(dd_ch, "d_skip_fold")
    dxs, db, dc, dadt3, dxdx3 = _ssd_bwd(xc, dt3, adt3, hprev, dy1)
    ddt, gs["dt_bias"], gs["a_log"] = _dt_bwd(_token_major(dadt3), _token_major(dxdx3), proj, bias128, alog128)
    cw, cbias = small["conv_w"], small["conv_b"]
    dux, dwx, dbx = _conv_bwd(proj, dxs, cw, cbias, 0, SSD_INNER, "conv_bwd_x", skip=(dy1, dexp))
    dub, dwb, dbb = _conv_bwd(proj, db, cw, cbias, SSD_INNER, SSD_G * SSD_N, "conv_bwd_b")
    duc, dwc, dbc = _conv_bwd(proj, dc, cw, cbias, SSD_INNER + SSD_G * SSD_N, SSD_G * SSD_N, "conv_bwd_c")
    gs["conv_w"] = jnp.concatenate([dwx[:4], dwb[:4], dwc[:4]], axis=1)
    gs["conv_b"] = jnp.concatenate([dbx, dbb, dbc], axis=1)
    dproj = jnp.concatenate([dz, dq, dgs, dga, dux, dub, duc, dk, dv, ddt], axis=1)
    gw["win"] = _mm(h1b, dproj, "tn", F32, "proj_dw", caps=(1024, 896, 2048))
    dh1 = _mm(dproj, w["win"], "nt", F32, "proj_dx", add=dr2, add_scale=ALPHA, caps=(1024, 1024, 2432))
    ln1_g = small["ln1_g"]
    if early_grads is not None:
        ln1_g = ln1_g + early_grads(gw)[0:1, 0:1]
    dr1, dr1h, gs["ln1_g"], gs["ln1_b"] = _ln_bwd(dh1, xh1, rs1, ln1_g, 0.5, "ln1_bwd")
    gw["d1"] = _mm(a1, dr1h, "tn", F32, "ffn1_down_dw")
    da1 = _mm(dr1h, w["d1"], "nt", BF16, "ffn1_down_dx", caps=(1024, 1408, 2048))
    dgu1 = _swiglu_bwd(gu1, da1, "ffn1_act_bwd")
    gw["gu1"] = _mm(xb, dgu1, "tn", F32, "ffn1_gu_dw", caps=(1024, 1408, 2048), n_slabs=N_CHIPS)
    grad_x = _mm(dgu1, w["gu1"], "nt", F32, "ffn1_gu_dx", add=dr1, add_scale=ALPHA, caps=(1024, 1024, 2816))
    return loss, grad_x, gw, gs


MESH = pl.DeviceIdType.MESH
ANY = pl.BlockSpec(memory_space=pl.ANY)


def _place():
    x, y, c = lax.axis_index("x"), lax.axis_index("y"), lax.axis_index("c")
    peers = [(1 - x, y), (x, 1 - y), (1 - x, 1 - y)]
    return x, y, c, peers


BIG = [
    ("ffn1_w_gate", D, SHARD_H, "gu1", "col", 0),
    ("ffn1_w_up", D, SHARD_H, "gu1", "col", SHARD_H),
    ("ffn1_w_down", SHARD_H, D, "d1", "row", 0),
    ("w_in", D, SHARD_IN, "win4", "lead", 0),
    ("w_ssd_o", SSD_INNER // N_CHIPS, D, "so", "row", 0),
    ("w_attn_o", D // N_CHIPS, D, "ao", "row", 0),
    ("w_out", D // N_CHIPS, D, "out", "row", 0),
    ("ffn2_w_gate", D, SHARD_H, "gu2", "col", 0),
    ("ffn2_w_up", D, SHARD_H, "gu2", "col", SHARD_H),
    ("ffn2_w_down", SHARD_H, D, "d2", "row", 0),
]
GATHERED = {"gu1": (D, 2 * FFN_H), "d1": (FFN_H, D), "win4": (N_CHIPS, D, SHARD_IN), "so": (SSD_INNER, D),
            "ao": (D, D), "out": (D, D), "gu2": (D, 2 * FFN_H), "d2": (FFN_H, D)}


def _cast_place(srcs, oname, chip_idx):
    rows, cols = srcs[0].shape
    tr = _divtile(rows, 256, 16)
    kind = [b[4] for b in BIG if b[3] == oname][0]

    def body(chip_ref, *refs):
        o_ref = refs[-1]
        for k, s_ref in enumerate(refs[:-1]):
            o_ref[:, k * cols:(k + 1) * cols] = s_ref[...].astype(BF16)

    nt = rows // tr
    if kind == "col":
        o_spec = pl.BlockSpec((tr, len(srcs) * cols), lambda i, chip_ref: (i, chip_ref[0]))
    elif kind == "row":
        o_spec = pl.BlockSpec((tr, cols), lambda i, chip_ref: (chip_ref[0] * nt + i, 0))
    else:
        o_spec = pl.BlockSpec((None, tr, cols), lambda i, chip_ref: (chip_ref[0], i, 0))
    return pl.pallas_call(
        body, name="cast_place_" + oname,
        grid_spec=pltpu.PrefetchScalarGridSpec(
            num_scalar_prefetch=1, grid=(nt,),
            in_specs=[pl.BlockSpec((tr, cols), lambda i, chip_ref: (i, 0))] * len(srcs), out_specs=o_spec),
        out_shape=jax.ShapeDtypeStruct(GATHERED[oname], BF16),
        compiler_params=_params(("parallel",), 32 << 20),
    )(chip_idx, *srcs)


def _slot(outs, entry, j, half):
    _, rows, cols, oname, kind, off = entry
    o = outs[oname]
    hr = rows // 2
    if kind == "col":
        cs = pl.ds(pl.multiple_of(j * (2 * SHARD_H) + off, 128), cols)
        return o.at[pl.ds(pl.multiple_of(half * hr, 16), hr), cs]
    if kind == "row":
        return o.at[pl.ds(pl.multiple_of(j * rows + half * hr, 16), hr), :]
    return o.at[j, pl.ds(pl.multiple_of(half * hr, 16), hr), :]


HBM = pl.BlockSpec(memory_space=pltpu.HBM)
SEM = pl.BlockSpec(memory_space=pltpu.SEMAPHORE)


def _ici_copy(outs, entry, j, c, to, send, recv, k):
    ref = _slot(outs, entry, j, c)
    return pltpu.make_async_remote_copy(src_ref=ref, dst_ref=ref, send_sem=send.at[k], recv_sem=recv.at[k],
                                        device_id=to, device_id_type=MESH)


def _gather_ici_start(placed, names):
    big = [b for b in BIG if b[3] in names]
    n = len(big)

    def body(*refs):
        outs = dict(zip(names, refs[len(names) + 2:2 * len(names) + 2]))
        send, recv = refs[len(names)], refs[len(names) + 1]
        token = refs[-1]
        x, y, c, peers = _place()
        for i, entry in enumerate(big):
            for k, (px, py) in enumerate(peers):
                _ici_copy(outs, entry, 2 * x + y, c, (px, py, c), send, recv, 3 * i + k).start()
        token[...] = jnp.zeros_like(token)

    res = pl.pallas_call(
        body, name="gather_ici_start",
        in_specs=[HBM] * len(names),
        out_specs=[SEM, SEM] + [HBM] * len(names) + [pl.BlockSpec(memory_space=pltpu.VMEM)],
        out_shape=[pltpu.SemaphoreType.DMA((3 * n,)), pltpu.SemaphoreType.DMA((3 * n,))]
        + [pltpu.HBM(GATHERED[k], BF16) for k in names] + [jax.ShapeDtypeStruct((8, 128), F32)],
        input_output_aliases={i: i + 2 for i in range(len(names))},
        compiler_params=pltpu.CompilerParams(has_side_effects=pltpu.SideEffectType.DATAFLOW_SIDE_EFFECTING),
    )(*[pltpu.with_memory_space_constraint(placed[k], pltpu.HBM) for k in names])
    return res[0], res[1], dict(zip(names, res[2:2 + len(names)])), res[-1]


def _gather_ici_wait(send, recv, arrays, names, after):
    big = [b for b in BIG if b[3] in names]

    def body(*refs):
        outs = dict(zip(names, refs[:len(names)]))
        send_ref, recv_ref = refs[len(names)], refs[len(names) + 1]
        x, y, c, peers = _place()
        for i, entry in enumerate(big):
            for k, (px, py) in enumerate(peers):
                mine = _ici_copy(outs, entry, 2 * x + y, c, (px, py, c), send_ref, recv_ref, 3 * i + k)
                mine.wait_send()
                theirs = _ici_copy(outs, entry, 2 * px + py, c, (px, py, c), send_ref, recv_ref, 3 * i + k)
                theirs.wait_recv()

    res = pl.pallas_call(
        body, name="gather_ici_wait",
        in_specs=[HBM] * len(names) + [SEM, SEM, pl.BlockSpec(memory_space=pl.ANY)],
        out_specs=[HBM] * len(names),
        out_shape=[pltpu.HBM(GATHERED[k], BF16) for k in names],
        input_output_aliases={i: i for i in range(len(names))},
        compiler_params=pltpu.CompilerParams(has_side_effects=pltpu.SideEffectType.DATAFLOW_SIDE_EFFECTING),
    )(*[arrays[k] for k in names], send, recv, after)
    return dict(zip(names, res))


def _gather_d2d(arrays, names):
    big = [b for b in BIG if b[3] in names]
    n = len(big)

    def body(*refs):
        outs = dict(zip(names, refs[len(names):2 * len(names)]))
        fsend, frecv = refs[2 * len(names):]
        x, y, c, peers = _place()
        cps = []
        for i, entry in enumerate(big):
            for k, (px, py) in enumerate(peers):
                cp = _ici_copy(outs, entry, 2 * px + py, c, (x, y, 1 - c), fsend, frecv, 3 * i + k)
                cp.start()
                cps.append(cp)
        for i, entry in enumerate(big):
            for k, (px, py) in enumerate(peers):
                _ici_copy(outs, entry, 2 * px + py, 1 - c, (x, y, 1 - c), fsend, frecv, 3 * i + k).wait_recv()
        for cp in cps:
            cp.wait_send()

    res = pl.pallas_call(
        body, name="gather_d2d",
        in_specs=[ANY] * len(names), out_specs=[ANY] * len(names),
        out_shape=[jax.ShapeDtypeStruct(GATHERED[k], BF16) for k in names],
        input_output_aliases={i: i for i in range(len(names))},
        scratch_shapes=[pltpu.SemaphoreType.DMA((3 * n,))] * 2,
    )(*[arrays[k] for k in names])
    return dict(zip(names, res))


def _gather_weights(placed, out_names):
    big = [b for b in BIG if b[3] in out_names]
    n = len(big)

    def body(*refs):
        outs = dict(zip(out_names, refs[len(out_names):2 * len(out_names)]))
        send, recv, fsend, frecv = refs[2 * len(out_names):]
        x, y, c, peers = _place()
        me = 2 * x + y

        def slot(i, j, half):
            return _slot(outs, big[i], j, half)

        sends = []
        for i in range(n):
            for k, (px, py) in enumerate(peers):
                cp = pltpu.make_async_remote_copy(src_ref=slot(i, me, c), dst_ref=slot(i, me, c),
                                                  send_sem=send.at[3 * i + k], recv_sem=recv.at[3 * i + k],
                                                  device_id=(px, py, c), device_id_type=MESH)
                cp.start()
                sends.append(cp)
        fwds = []
        for i in range(n):
            for k, (px, py) in enumerate(peers):
                pj = 2 * px + py
                pltpu.make_async_remote_copy(src_ref=slot(i, pj, c), dst_ref=slot(i, pj, c),
                                             send_sem=send.at[3 * i + k], recv_sem=recv.at[3 * i + k],
                                             device_id=(px, py, c), device_id_type=MESH).wait_recv()
                cp = pltpu.make_async_remote_copy(src_ref=slot(i, pj, c), dst_ref=slot(i, pj, c),
                                                  send_sem=fsend.at[3 * i + k], recv_sem=frecv.at[3 * i + k],
                                                  device_id=(x, y, 1 - c), device_id_type=MESH)
                cp.start()
                fwds.append(cp)
        for i in range(n):
            for k, (px, py) in enumerate(peers):
                pj = 2 * px + py
                pltpu.make_async_remote_copy(src_ref=slot(i, pj, 1 - c), dst_ref=slot(i, pj, 1 - c),
                                             send_sem=fsend.at[3 * i + k], recv_sem=frecv.at[3 * i + k],
                                             device_id=(x, y, 1 - c), device_id_type=MESH).wait_recv()
        for cp in sends + fwds:
            cp.wait_send()

    outs = pl.pallas_call(
        body, name="gather_weights",
        in_specs=[ANY] * len(out_names), out_specs=[ANY] * len(out_names),
        out_shape=[jax.ShapeDtypeStruct(GATHERED[k], BF16) for k in out_names],
        input_output_aliases={i: i for i in range(len(out_names))},
        scratch_shapes=[pltpu.SemaphoreType.DMA((3 * n,))] * 4,
    )(*[placed[k] for k in out_names])
    return dict(zip(out_names, outs))


FIRST_USED = ["gu1", "d1"]
LATER_USED = ["win4", "so", "ao", "out", "gu2", "d2"]


def _win_pieces():
    pieces = []
    for g0, wd, i0 in SEGS.values():
        for j in range(N_CHIPS):
            lo, hi = max(g0, j * SHARD_IN), min(g0 + wd, (j + 1) * SHARD_IN)
            if lo < hi:
                pieces.append((j, lo - j * SHARD_IN, hi - j * SHARD_IN, i0 + lo - g0))
    return pieces


def _win_to_internal(win4):
    tr = 128

    def body(i_ref, o_ref):
        for j, s0, s1, d0 in _win_pieces():
            o_ref[:, d0:d0 + s1 - s0] = i_ref[j, :, s0:s1]
        o_ref[:, PROJ_W:] = jnp.zeros((tr, PROJ_PAD - PROJ_W), o_ref.dtype)

    return pl.pallas_call(
        body, name="win_to_internal", grid=(D // tr,),
        in_specs=[pl.BlockSpec((N_CHIPS, tr, SHARD_IN), lambda i: (0, i, 0))],
        out_specs=pl.BlockSpec((tr, PROJ_PAD), lambda i: (i, 0)),
        out_shape=jax.ShapeDtypeStruct((D, PROJ_PAD), win4.dtype),
        compiler_params=_params(("parallel",), 40 << 20),
    )(win4)


def _win_from_internal(g):
    tr = 64

    def body(i_ref, o_ref):
        for j, s0, s1, d0 in _win_pieces():
            o_ref[j, :, s0:s1] = i_ref[:, d0:d0 + s1 - s0]

    return pl.pallas_call(
        body, name="win_from_internal", grid=(D // tr,),
        in_specs=[pl.BlockSpec((tr, PROJ_PAD), lambda i: (i, 0))],
        out_specs=pl.BlockSpec((N_CHIPS, tr, SHARD_IN), lambda i: (0, i, 0)),
        out_shape=jax.ShapeDtypeStruct((N_CHIPS, D, SHARD_IN), g.dtype),
        compiler_params=_params(("parallel",), 40 << 20),
    )(g)


def _rs_pair_exchange(grads, tag):
    n = len(grads)

    def body(*refs):
        srcs, dsts = refs[:n], refs[n:2 * n]
        send, recv = refs[2 * n:]
        x, y, c, _ = _place()
        cps = []
        for i in range(n):
            hr = srcs[i].shape[1] // 2
            cp = pltpu.make_async_remote_copy(
                src_ref=srcs[i].at[:, pl.ds(pl.multiple_of((1 - c) * hr, 16), hr), :], dst_ref=dsts[i],
                send_sem=send.at[i], recv_sem=recv.at[i], device_id=(x, y, 1 - c), device_id_type=MESH)
            cp.start()
            cps.append(cp)
        for cp in cps:
            cp.wait()

    return pl.pallas_call(
        body, name="rs_pair_exchange_" + tag, in_specs=[ANY] * n, out_specs=[ANY] * n,
        out_shape=[jax.ShapeDtypeStruct((g.shape[0], g.shape[1] // 2, g.shape[2]), F32) for g in grads],
        scratch_shapes=[pltpu.SemaphoreType.DMA((n,))] * 2,
    )(*grads)


def _half_tile(hr):
    return _divtile(hr, 256, 16) if hr % 256 == 0 else _divtile(hr, 512, 16)


def _rs_pair_sum(g, r, c_idx, name):
    ns, rows, cols = g.shape
    hr = rows // 2
    tr = _half_tile(hr)
    nt = hr // tr

    def body(c_ref, g_ref, r_ref, ob_ref, of_ref):
        s = g_ref[...] + r_ref[...]
        ob_ref[...] = s.astype(BF16)
        of_ref[...] = s

    blk = pl.BlockSpec((None, tr, cols), lambda j, t, c_ref: (j, t, 0))
    return pl.pallas_call(
        body, name=name,
        grid_spec=pltpu.PrefetchScalarGridSpec(
            num_scalar_prefetch=1, grid=(ns, nt),
            in_specs=[pl.BlockSpec((None, tr, cols), lambda j, t, c_ref: (j, c_ref[0] * nt + t, 0)), blk],
            out_specs=[blk, blk]),
        out_shape=[jax.ShapeDtypeStruct((ns, hr, cols), BF16), jax.ShapeDtypeStruct((ns, hr, cols), F32)],
        compiler_params=_params(("parallel", "parallel"), 48 << 20),
    )(c_idx, g, r)


def _rs_chip_exchange(parts, tag):
    n = len(parts)

    def body(*refs):
        srcs, dsts = refs[:n], refs[n:2 * n]
        send, recv = refs[2 * n:]
        x, y, c, peers = _place()
        cps = []
        for i in range(n):
            for k, (px, py) in enumerate(peers):
                cp = pltpu.make_async_remote_copy(
                    src_ref=srcs[i].at[2 * px + py], dst_ref=dsts[i].at[k],
                    send_sem=send.at[3 * i + k], recv_sem=recv.at[3 * i + k],
                    device_id=(px, py, c), device_id_type=MESH)
                cp.start()
                cps.append(cp)
        for cp in cps:
            cp.wait()

    return pl.pallas_call(
        body, name="rs_chip_exchange_" + tag, in_specs=[ANY] * n, out_specs=[ANY] * n,
        out_shape=[jax.ShapeDtypeStruct((3,) + p.shape[1:], BF16) for p in parts],
        scratch_shapes=[pltpu.SemaphoreType.DMA((3 * n,))] * 2,
    )(*parts)


def _rs_chip_start(parts):
    n = len(parts)

    def body(*refs):
        send, recv = refs[2 * n], refs[2 * n + 1]
        srcs, dsts = refs[2 * n + 2:3 * n + 2], refs[3 * n + 2:4 * n + 2]
        token = refs[-1]
        x, y, c, peers = _place()
        for i in range(n):
            for k, (px, py) in enumerate(peers):
                pltpu.make_async_remote_copy(
                    src_ref=srcs[i].at[2 * px + py], dst_ref=dsts[i].at[k],
                    send_sem=send.at[3 * i + k], recv_sem=recv.at[3 * i + k],
                    device_id=(px, py, c), device_id_type=MESH).start()
        token[...] = jnp.zeros_like(token)

    lands = [lax.empty((3,) + p.shape[1:], BF16) for p in parts]
    res = pl.pallas_call(
        body, name="rs_chip_start",
        in_specs=[HBM] * (2 * n),
        out_specs=[SEM, SEM] + [HBM] * (2 * n) + [pl.BlockSpec(memory_space=pltpu.VMEM)],
        out_shape=[pltpu.SemaphoreType.DMA((3 * n,)), pltpu.SemaphoreType.DMA((3 * n,))]
        + [pltpu.HBM(p.shape, BF16) for p in parts] + [pltpu.HBM(l.shape, BF16) for l in lands]
        + [jax.ShapeDtypeStruct((8, 128), F32)],
        input_output_aliases={i: i + 2 for i in range(2 * n)},
        compiler_params=pltpu.CompilerParams(has_side_effects=pltpu.SideEffectType.DATAFLOW_SIDE_EFFECTING),
    )(*[pltpu.with_memory_space_constraint(a, pltpu.HBM) for a in list(parts) + lands])
    return res[0], res[1], list(res[2:2 + n]), list(res[2 + n:2 + 2 * n]), res[-1]


def _rs_chip_wait(send, recv, parts, lands, after):
    n = len(parts)

    def body(*refs):
        srcs, dsts = refs[:n], refs[n:2 * n]
        send_ref, recv_ref = refs[2 * n], refs[2 * n + 1]
        x, y, c, peers = _place()
        for i in range(n):
            for k, (px, py) in enumerate(peers):
                cp = pltpu.make_async_remote_copy(
                    src_ref=srcs[i].at[2 * px + py], dst_ref=dsts[i].at[k],
                    send_sem=send_ref.at[3 * i + k], recv_sem=recv_ref.at[3 * i + k],
                    device_id=(px, py, c), device_id_type=MESH)
                cp.wait_send()
                cp.wait_recv()

    res = pl.pallas_call(
        body, name="rs_chip_wait",
        in_specs=[HBM] * (2 * n) + [SEM, SEM, pl.BlockSpec(memory_space=pl.ANY)],
        out_specs=[HBM] * (2 * n),
        out_shape=[pltpu.HBM(p.shape, BF16) for p in parts] + [pltpu.HBM(l.shape, BF16) for l in lands],
        input_output_aliases={i: i for i in range(2 * n)},
        compiler_params=pltpu.CompilerParams(has_side_effects=pltpu.SideEffectType.DATAFLOW_SIDE_EFFECTING),
    )(*parts, *lands, send, recv, after)
    return list(res[n:])


def _rs_final_sum(own, got, chip_idx, c_idx, name):
    ns, hr, cols = own.shape
    tr = _half_tile(hr)
    nt = hr // tr

    def body(chip_ref, c_ref, o_ref, g_ref, out_ref):
        s = o_ref[...]
        for k in range(3):
            s = s + g_ref[k].astype(F32)
        out_ref[...] = s

    return pl.pallas_call(
        body, name=name,
        grid_spec=pltpu.PrefetchScalarGridSpec(
            num_scalar_prefetch=2, grid=(nt,),
            in_specs=[pl.BlockSpec((None, tr, cols), lambda t, chip_ref, c_ref: (chip_ref[0], t, 0)),
                      pl.BlockSpec((3, tr, cols), lambda t, chip_ref, c_ref: (0, t, 0))],
            out_specs=pl.BlockSpec((tr, cols), lambda t, chip_ref, c_ref: (c_ref[0] * nt + t, 0))),
        out_shape=jax.ShapeDtypeStruct((2 * hr, cols), F32),
        compiler_params=_params(("parallel",), 48 << 20),
    )(chip_idx, c_idx, own, got)


def _rs_share_halves(fulls):
    n = len(fulls)

    def body(*refs):
        dsts = refs[n:2 * n]
        send, recv = refs[2 * n:]
        x, y, c, _ = _place()
        cps = []
        for i in range(n):
            hr = dsts[i].shape[0] // 2
            rows = dsts[i].at[pl.ds(pl.multiple_of(c * hr, 8), hr), :]
            cp = pltpu.make_async_remote_copy(src_ref=rows, dst_ref=rows, send_sem=send.at[i], recv_sem=recv.at[i],
                                              device_id=(x, y, 1 - c), device_id_type=MESH)
            cp.start()
            cps.append(cp)
        for i in range(n):
            hr = dsts[i].shape[0] // 2
            other = dsts[i].at[pl.ds(pl.multiple_of((1 - c) * hr, 8), hr), :]
            pltpu.make_async_remote_copy(src_ref=other, dst_ref=other, send_sem=send.at[i], recv_sem=recv.at[i],
                                         device_id=(x, y, 1 - c), device_id_type=MESH).wait_recv()
        for cp in cps:
            cp.wait_send()

    return pl.pallas_call(
        body, name="rs_share_halves", in_specs=[ANY] * n, out_specs=[ANY] * n,
        out_shape=[jax.ShapeDtypeStruct(f.shape, F32) for f in fulls],
        input_output_aliases={i: i for i in range(n)},
        scratch_shapes=[pltpu.SemaphoreType.DMA((n,))] * 2,
    )(*fulls)


def _all_reduce_small(v):
    rows = v.shape[0]

    def body(v_ref, o_ref, buf, send, recv):
        x, y, c, _ = _place()
        me = 4 * x + 2 * y + c
        buf[me] = v_ref[...]
        cps = []
        for d in range(1, 8):
            px, py, pc = x ^ (d >> 2), y ^ ((d >> 1) & 1), c ^ (d & 1)
            cp = pltpu.make_async_remote_copy(src_ref=v_ref, dst_ref=buf.at[me], send_sem=send.at[d - 1],
                                              recv_sem=recv.at[d - 1], device_id=(px, py, pc), device_id_type=MESH)
            cp.start()
            cps.append(cp)
        for d in range(1, 8):
            px, py, pc = x ^ (d >> 2), y ^ ((d >> 1) & 1), c ^ (d & 1)
            pltpu.make_async_remote_copy(src_ref=v_ref, dst_ref=buf.at[4 * px + 2 * py + pc], send_sem=send.at[d - 1],
                                         recv_sem=recv.at[d - 1], device_id=(px, py, pc),
                                         device_id_type=MESH).wait_recv()
        for cp in cps:
            cp.wait_send()
        acc = buf[0]
        for d in range(1, 8):
            acc = acc + buf[d]
        o_ref[...] = acc

    vm = pl.BlockSpec(memory_space=pltpu.VMEM)
    return pl.pallas_call(
        body, name="all_reduce_small", in_specs=[vm], out_specs=vm,
        out_shape=jax.ShapeDtypeStruct((rows, 128), F32),
        scratch_shapes=[pltpu.VMEM((8, rows, 128), F32), pltpu.SemaphoreType.DMA((7,)), pltpu.SemaphoreType.DMA((7,))],
    )(v)


def _adamw(w, g, m, v, name, g_col_blk=0):
    rows, cols = w.shape
    tr = _divtile(rows, max(8, (2 << 20) // (4 * cols) // 8 * 8), 8)

    def body(w_ref, g_ref, m_ref, v_ref, go_ref, d_ref, mo_ref, vo_ref):
        gv = g_ref[...]
        mn = ADAM_B1 * m_ref[...] + (1.0 - ADAM_B1) * gv
        vn = ADAM_B2 * v_ref[...] + (1.0 - ADAM_B2) * (gv * gv)
        m_hat = mn / (1.0 - ADAM_B1 ** ADAM_STEP)
        v_hat = vn / (1.0 - ADAM_B2 ** ADAM_STEP)
        go_ref[...] = gv
        d_ref[...] = -ADAM_LR * (m_hat / (jnp.sqrt(v_hat) + ADAM_EPS) + ADAM_WD * w_ref[...])
        mo_ref[...] = mn
        vo_ref[...] = vn

    blk = pl.BlockSpec((tr, cols), lambda i: (i, 0))
    return pl.pallas_call(
        body, name=name, grid=(rows // tr,),
        in_specs=[blk, pl.BlockSpec((tr, cols), lambda i: (i, g_col_blk)), blk, blk],
        out_specs=[blk] * 4, out_shape=[jax.ShapeDtypeStruct((rows, cols), F32)] * 4,
        compiler_params=_params(("parallel",), 48 << 20),
    )(w, g, m, v)


SMALL = ["ln1_g", "ln1_b", "conv_w", "conv_b", "dt_bias", "a_log", "d_skip", "ssd_norm_g", "attn_sinks",
         "ln2_g", "ln2_b", "ln3_g", "ln3_b"]


def _pack_rows(vs):
    parts = []
    for v in vs:
        v = v.reshape(-1)
        parts.append(jnp.pad(v, (0, (-v.shape[0]) % 128)))
    flat = jnp.concatenate(parts)
    flat = jnp.pad(flat, (0, (-flat.shape[0]) % 1024))
    return flat.reshape(-1, 128)


def _unpack_rows(packed, shapes):
    flat = packed.reshape(-1)
    out, at = [], 0
    for s in shapes:
        nel = int(np.prod(s))
        out.append(flat[at:at + nel].reshape(s))
        at += nel + (-nel) % 128
    return out


def kernel(x, positions, ffn1_w_gate, ffn1_w_up, ffn1_w_down, ln1_g, ln1_b, w_in, conv_w, conv_b, dt_bias, a_log, d_skip, ssd_norm_g, w_ssd_o, attn_sinks, w_attn_o, w_out, ln2_g, ln2_b, ffn2_w_gate, ffn2_w_up, ffn2_w_down, ln3_g, ln3_b, loss_target, m_ffn1_w_gate, m_ffn1_w_up, m_ffn1_w_down, m_ln1_g, m_ln1_b, m_w_in, m_conv_w, m_conv_b, m_dt_bias, m_a_log, m_d_skip, m_ssd_norm_g, m_w_ssd_o, m_attn_sinks, m_w_attn_o, m_w_out, m_ln2_g, m_ln2_b, m_ffn2_w_gate, m_ffn2_w_up, m_ffn2_w_down, m_ln3_g, m_ln3_b, v_ffn1_w_gate, v_ffn1_w_up, v_ffn1_w_down, v_ln1_g, v_ln1_b, v_w_in, v_conv_w, v_conv_b, v_dt_bias, v_a_log, v_d_skip, v_ssd_norm_g, v_w_ssd_o, v_attn_sinks, v_w_attn_o, v_w_out, v_ln2_g, v_ln2_b, v_ffn2_w_gate, v_ffn2_w_up, v_ffn2_w_down, v_ln3_g, v_ln3_b):
    args = dict(locals())
    wts = {n: args[n][0] for n in [b[0] for b in BIG] + SMALL}
    mom_m = {n: args["m_" + n][0] for n in wts}
    mom_v = {n: args["v_" + n][0] for n in wts}
    t = x.shape[1]
    xi, yi, ci = lax.axis_index("x"), lax.axis_index("y"), lax.axis_index("c")
    chip = 2 * xi + yi

    c_idx = ci.astype(jnp.int32).reshape(1)
    chip_idx = chip.astype(jnp.int32).reshape(1)
    placed = {o: _cast_place([wts[b[0]] for b in BIG if b[3] == o], o, chip_idx) for o in GATHERED}
    w = _gather_weights(placed, FIRST_USED)
    g_send, g_recv, g_flight, g_token = _gather_ici_start(placed, LATER_USED)

    def later_weights(h1b):
        arrived = _gather_d2d(_gather_ici_wait(g_send, g_recv, g_flight, LATER_USED, h1b), LATER_USED)
        rest = {k: arrived[k] for k in ("so", "ao", "out", "gu2", "d2")}
        rest["win"] = _win_to_internal(arrived["win4"])
        return rest

    def slabs_of(gw, names):
        view = {"gu1": lambda: gw["gu1"], "gu2": lambda: gw["gu2"],
                "d1": lambda: gw["d1"].reshape(N_CHIPS, SHARD_H, D), "d2": lambda: gw["d2"].reshape(N_CHIPS, SHARD_H, D),
                "win": lambda: _win_from_internal(gw["win"]),
                "so": lambda: gw["so"].reshape(N_CHIPS, SSD_INNER // N_CHIPS, D),
                "ao": lambda: gw["ao"].reshape(N_CHIPS, D // N_CHIPS, D),
                "out": lambda: gw["out"].reshape(N_CHIPS, D // N_CHIPS, D)}
        return [view[nm]() for nm in names]

    early = ["win", "so", "ao", "out", "gu2", "d2"]
    late = ["gu1", "d1"]
    flight = {}

    def early_grads(gw):
        slabs = slabs_of(gw, early)
        from_sib = _rs_pair_exchange(slabs, "early")
        pair = [_rs_pair_sum(g, r, c_idx, "rs_pair_sum_" + nm) for g, r, nm in zip(slabs, from_sib, early)]
        send, recv, parts, lands, token = _rs_chip_start([p[0] for p in pair])
        flight.update(send=send, recv=recv, parts=parts, lands=lands, own=[p[1] for p in pair])
        return token
    cw_rows = _pack_rows([lax.dynamic_update_slice(jnp.zeros((4, XBC), F32), wts["conv_w"], (0, chip * (XBC // N_CHIPS)))])
    cw_rows = jnp.where(ci == 0, cw_rows, 0.0)
    conv_w_full = _all_reduce_small(cw_rows)[:4 * XBC // 128].reshape(4, XBC)

    small = {n: (wts[n][None, :] if wts[n].ndim == 1 else wts[n]) for n in SMALL}
    small["conv_w"] = conv_w_full
    loss, grad_x, gw, gs = _local_step(x[0], positions[0].astype(F32)[:, None], loss_target[0], w, small,
                                       start_token=g_token, later_weights=later_weights, early_grads=early_grads)

    got_early = _rs_chip_wait(flight["send"], flight["recv"], flight["parts"], flight["lands"], grad_x)
    slabs = slabs_of(gw, late)
    from_sib = _rs_pair_exchange(slabs, "late")
    pair = [_rs_pair_sum(g, r, c_idx, "rs_pair_sum_" + nm) for g, r, nm in zip(slabs, from_sib, late)]
    got_late = _rs_chip_exchange([p[0] for p in pair], "late")
    names = early + late
    own = flight["own"] + [p[1] for p in pair]
    halves = [_rs_final_sum(o, gt, chip_idx, c_idx, "rs_final_sum_" + nm)
              for o, gt, nm in zip(own, list(got_early) + list(got_late), names)]
    full = dict(zip(names, _rs_share_halves(halves)))

    outs = {}
    big_src = {"ffn1_w_gate": ("gu1", 0), "ffn1_w_up": ("gu1", 1), "ffn1_w_down": ("d1", 0), "w_in": ("win", 0),
               "w_ssd_o": ("so", 0), "w_attn_o": ("ao", 0), "w_out": ("out", 0),
               "ffn2_w_gate": ("gu2", 0), "ffn2_w_up": ("gu2", 1), "ffn2_w_down": ("d2", 0)}
    for nm, (src, blk) in big_src.items():
        outs[nm] = _adamw(wts[nm], full[src], mom_m[nm], mom_v[nm], "adamw_" + nm, g_col_blk=blk)

    gvec = {n: gs[n] for n in SMALL}
    gvec["dt_bias"], gvec["a_log"], gvec["d_skip"] = gs["dt_bias"][:, :64], gs["a_log"][:, :64], gs["d_skip"][:, :64]
    gvec["attn_sinks"] = gs["attn_sinks"][:, :NQ]
    red = _all_reduce_small(_pack_rows([gvec[n] for n in SMALL] + [loss]))
    shapes = [(4, XBC) if n == "conv_w" else wts[n].shape for n in SMALL] + [(1,)]
    red_list = _unpack_rows(red, shapes)
    loss_out = red_list[-1].reshape(())
    gsm = dict(zip(SMALL, red_list[:-1]))
    gsm["conv_w"] = lax.dynamic_slice_in_dim(gsm["conv_w"], chip * (XBC // N_CHIPS), XBC // N_CHIPS, axis=1)
    sm_shapes = [wts[n].shape for n in SMALL]
    res = _adamw(_pack_rows([wts[n] for n in SMALL]), _pack_rows([gsm[n] for n in SMALL]),
                 _pack_rows([mom_m[n] for n in SMALL]), _pack_rows([mom_v[n] for n in SMALL]), "adamw_small")
    res = [_unpack_rows(r, sm_shapes) for r in res]
    for i, nm in enumerate(SMALL):
        outs[nm] = tuple(r[i] for r in res)

    order = ["ffn1_w_gate", "ffn1_w_up", "ffn1_w_down", "ln1_g", "ln1_b", "w_in", "conv_w", "conv_b", "dt_bias", "a_log",
             "d_skip", "ssd_norm_g", "w_ssd_o", "attn_sinks", "w_attn_o", "w_out", "ln2_g", "ln2_b",
             "ffn2_w_gate", "ffn2_w_up", "ffn2_w_down", "ln3_g", "ln3_b"]
    result = [loss_out, grad_x[None]]
    for kind in range(4):
        result += [outs[nm][kind][None] for nm in order]
    return tuple(result)
```

```python
import functools
import math

import numpy as np
import jax
import jax.numpy as jnp
from jax import lax
from jax.experimental import pallas as pl
from jax.experimental.pallas import tpu as pltpu

F32 = jnp.float32
BF16 = jnp.bfloat16
HI = lax.Precision.HIGHEST

D = 2048
FFN_H = 5632
SSD_INNER = 4096
SSD_HEADS = 64
SSD_P = 64
SSD_G = 8
SSD_R = 8
SSD_N = 128
CHUNK = 128
XBC = 6144
NQ = 32
NKV = 4
HD = 64
QW = 2048
KVW = 256
WINDOW = 128
ROPE_THETA = 10000.0
ALPHA = 2.0 ** 0.25
LN_EPS = 1e-5
RMS_EPS = 1e-5
PROJ_W = 16960
N_CHIPS = 4
SHARD_IN = PROJ_W // N_CHIPS
SHARD_H = FFN_H // N_CHIPS

SEGS = {
    "z": (0, 4096, 0),
    "xbc": (4096, 6144, 10240),
    "dt": (10240, 64, 16896),
    "q": (10304, 2048, 4096),
    "k": (12352, 256, 16384),
    "v": (12608, 256, 16640),
    "gs": (12864, 2048, 6144),
    "ga": (14912, 2048, 8192),
}
PROJ_PAD = 17024

ADAM_LR = 0.001
ADAM_B1 = 0.9
ADAM_B2 = 0.999
ADAM_EPS = 1e-08
ADAM_WD = 0.01
ADAM_STEP = 10

VMEM_CAP = 60 * 1024 * 1024


def _params(sem, vmem_bytes):
    return pltpu.CompilerParams(dimension_semantics=sem, vmem_limit_bytes=int(min(VMEM_CAP, vmem_bytes)))


def _divtile(n, cap, q=128):
    best = None
    for d in range(q, min(n, cap) + 1, q):
        if n % d == 0:
            best = d
    return n if best is None else best


def _sigmoid(x):
    return 1.0 / (1.0 + jnp.exp(-x))


def _mm(a, b, mode, out_dtype, name, add=None, add_scale=1.0, caps=(1024, 1024, 2048), n_slabs=1):
    if mode == "nn":
        (m, k), (k2, n) = a.shape, b.shape
    elif mode == "nt":
        (m, k), (n, k2) = a.shape, b.shape
    else:
        (k, m), (k2, n) = a.shape, b.shape
    assert k == k2, (a.shape, b.shape, mode)
    tm, tn, tk = _divtile(m, caps[0]), _divtile(n // n_slabs, caps[1]), _divtile(k, caps[2])
    nk = k // tk
    per_slab = n // n_slabs // tn
    dims = {"nn": ((1,), (0,)), "nt": ((1,), (1,)), "tn": ((0,), (0,))}[mode]
    has_add = add is not None

    def body(*refs):
        if has_add:
            a_ref, b_ref, add_ref, o_ref = refs[:4]
            scr = refs[4:]
        else:
            a_ref, b_ref, o_ref = refs[:3]
            add_ref = None
            scr = refs[3:]
        part = lax.dot_general(a_ref[...].astype(BF16), b_ref[...].astype(BF16), (dims, ((), ())),
                               preferred_element_type=F32)

        def finish(acc):
            if has_add:
                acc = acc + add_scale * add_ref[...].astype(F32)
            o_ref[...] = acc.astype(o_ref.dtype)

        if nk == 1:
            finish(part)
        else:
            acc_ref = scr[0]
            kk = pl.program_id(2)

            @pl.when(kk == 0)
            def _():
                acc_ref[...] = part

            @pl.when(kk > 0)
            def _():
                acc_ref[...] += part

            @pl.when(kk == nk - 1)
            def _():
                finish(acc_ref[...])

    if mode == "nn":
        a_spec = pl.BlockSpec((tm, tk), lambda i, j, kk: (i, kk))
        b_spec = pl.BlockSpec((tk, tn), lambda i, j, kk: (kk, j))
    elif mode == "nt":
        a_spec = pl.BlockSpec((tm, tk), lambda i, j, kk: (i, kk))
        b_spec = pl.BlockSpec((tn, tk), lambda i, j, kk: (j, kk))
    else:
        a_spec = pl.BlockSpec((tk, tm), lambda i, j, kk: (kk, i))
        b_spec = pl.BlockSpec((tk, tn), lambda i, j, kk: (kk, j))
    o_spec = pl.BlockSpec((tm, tn), lambda i, j, kk: (i, j))
    out_shape = jax.ShapeDtypeStruct((m, n), out_dtype)
    if n_slabs > 1:
        assert not has_add
        o_spec = pl.BlockSpec((None, tm, tn), lambda i, j, kk: (j // per_slab, i, j % per_slab))
        out_shape = jax.ShapeDtypeStruct((n_slabs, m, n // n_slabs), out_dtype)
    in_specs = [a_spec, b_spec] + ([o_spec] if has_add else [])
    args = (a, b) + ((add,) if has_add else ())
    osz = jnp.dtype(out_dtype).itemsize
    vmem = (2 * (tm * tk * a.dtype.itemsize + tk * tn * b.dtype.itemsize) + 2 * tm * tn * osz
            + (2 * tm * tn * add.dtype.itemsize if has_add else 0) + 2 * tm * tn * 4
            + 2 * (tm * tk + tk * tn) + (8 << 20))
    return pl.pallas_call(
        body, name=name, grid=(m // tm, n // tn, nk),
        in_specs=in_specs, out_specs=o_spec, out_shape=out_shape,
        scratch_shapes=[pltpu.VMEM((tm, tn), F32)] if nk > 1 else [],
        compiler_params=_params(("parallel", "parallel", "arbitrary"), vmem),
    )(*args)


def _mm_swiglu(a, b, name):
    m, k = a.shape
    w = SHARD_H
    tm = _divtile(m, 512)

    def body(a_ref, b_ref, gu_ref, act_ref):
        gu = jnp.dot(a_ref[...], b_ref[...], preferred_element_type=F32)
        g = gu[:, :w]
        gu_ref[...] = gu.astype(BF16)
        act_ref[...] = (g * _sigmoid(g) * gu[:, w:]).astype(BF16)

    return pl.pallas_call(
        body, name=name, grid=(N_CHIPS, m // tm),
        in_specs=[pl.BlockSpec((tm, k), lambda j, i: (i, 0)), pl.BlockSpec((k, 2 * w), lambda j, i: (0, j))],
        out_specs=[pl.BlockSpec((tm, 2 * w), lambda j, i: (i, j)), pl.BlockSpec((tm, w), lambda j, i: (i, j))],
        out_shape=[jax.ShapeDtypeStruct((m, 2 * FFN_H), BF16), jax.ShapeDtypeStruct((m, FFN_H), BF16)],
        compiler_params=_params(("parallel", "parallel"), 56 << 20),
    )(a, b)


def _swiglu_bwd(gu, da, name):
    t = gu.shape[0]
    tt = _divtile(t, 512)
    w = SHARD_H

    def body(gu_ref, da_ref, o_ref):
        g = gu_ref[:, :w].astype(F32)
        u = gu_ref[:, w:].astype(F32)
        d = da_ref[...].astype(F32)
        s = _sigmoid(g)
        o_ref[:, :w] = (d * u * (s * (1.0 + g * (1.0 - s)))).astype(BF16)
        o_ref[:, w:] = (d * (g * s)).astype(BF16)

    return pl.pallas_call(
        body, name=name, grid=(t // tt, N_CHIPS),
        in_specs=[pl.BlockSpec((tt, 2 * w), lambda i, j: (i, j)), pl.BlockSpec((tt, w), lambda i, j: (i, j))],
        out_specs=pl.BlockSpec((tt, 2 * w), lambda i, j: (i, j)),
        out_shape=jax.ShapeDtypeStruct((t, 2 * FFN_H), BF16),
        compiler_params=_params(("parallel", "parallel"), 40 << 20),
    )(gu, da)


def _ln_fwd(base, f, g, b, c, name, target=None):
    t = base.shape[0]
    tt = _divtile(t, 256)
    with_loss = target is not None

    def body(*refs):
        if with_loss:
            base_ref, f_ref, g_ref, b_ref, tg_ref, h_ref, hb_ref, xh_ref, rs_ref, dh_ref, loss_ref = refs
        else:
            base_ref, f_ref, g_ref, b_ref, h_ref, hb_ref, xh_ref, rs_ref = refs
        r = ALPHA * base_ref[...] + c * f_ref[...]
        mu = jnp.mean(r, axis=-1, keepdims=True)
        xc = r - mu
        var = jnp.mean(xc * xc, axis=-1, keepdims=True)
        rstd = lax.rsqrt(var + LN_EPS)
        xh = xc * rstd
        h = xh * g_ref[...] + b_ref[...]
        h_ref[...] = h
        hb_ref[...] = h.astype(BF16)
        xh_ref[...] = xh
        rs_ref[...] = rstd
        if with_loss:
            e = h - tg_ref[...]
            dh_ref[...] = e * (1.0 / D)
            part = 0.5 * jnp.sum(jnp.sum(e * e, axis=-1, keepdims=True) * (1.0 / D), axis=0, keepdims=True)

            @pl.when(pl.program_id(0) == 0)
            def _():
                loss_ref[...] = jnp.zeros_like(loss_ref)

            loss_ref[...] += part

    row = pl.BlockSpec((tt, D), lambda i: (i, 0))
    vec = pl.BlockSpec((1, D), lambda i: (0, 0))
    col = pl.BlockSpec((tt, 1), lambda i: (i, 0))
    in_specs = [row, row, vec, vec] + ([row] if with_loss else [])
    out_specs = [row, row, row, col] + ([row, pl.BlockSpec((1, 1), lambda i: (0, 0))] if with_loss else [])
    out_shape = [jax.ShapeDtypeStruct((t, D), F32), jax.ShapeDtypeStruct((t, D), BF16),
                 jax.ShapeDtypeStruct((t, D), F32), jax.ShapeDtypeStruct((t, 1), F32)]
    if with_loss:
        out_shape += [jax.ShapeDtypeStruct((t, D), F32), jax.ShapeDtypeStruct((1, 1), F32)]
    args = (base, f, g, b) + ((target,) if with_loss else ())
    return pl.pallas_call(
        body, name=name, grid=(t // tt,), in_specs=in_specs, out_specs=out_specs, out_shape=out_shape,
        compiler_params=_params(("arbitrary",) if with_loss else ("parallel",), 48 << 20),
    )(*args)


def _ln_bwd(dy, xh, rstd, g, c, name):
    t = dy.shape[0]
    tt = _divtile(t, 256)

    def body(dy_ref, xh_ref, rs_ref, g_ref, dr_ref, drb_ref, dg_ref, db_ref):
        dyv = dy_ref[...]
        xhv = xh_ref[...]
        dxh = dyv * g_ref[...]
        m1 = jnp.mean(dxh, axis=-1, keepdims=True)
        m2 = jnp.mean(dxh * xhv, axis=-1, keepdims=True)
        dr = rs_ref[...] * (dxh - m1 - xhv * m2)
        dr_ref[...] = dr
        drb_ref[...] = (c * dr).astype(BF16)

        @pl.when(pl.program_id(0) == 0)
        def _():
            dg_ref[...] = jnp.zeros_like(dg_ref)
            db_ref[...] = jnp.zeros_like(db_ref)

        dg_ref[...] += jnp.sum(dyv * xhv, axis=0, keepdims=True)
        db_ref[...] += jnp.sum(dyv, axis=0, keepdims=True)

    row = pl.BlockSpec((tt, D), lambda i: (i, 0))
    vec = pl.BlockSpec((1, D), lambda i: (0, 0))
    col = pl.BlockSpec((tt, 1), lambda i: (i, 0))
    return pl.pallas_call(
        body, name=name, grid=(t // tt,), in_specs=[row, row, col, vec], out_specs=[row, row, vec, vec],
        out_shape=[jax.ShapeDtypeStruct((t, D), F32), jax.ShapeDtypeStruct((t, D), BF16),
                   jax.ShapeDtypeStruct((1, D), F32), jax.ShapeDtypeStruct((1, D), F32)],
        compiler_params=_params(("arbitrary",), 40 << 20),
    )(dy, xh, rstd, g)


DT_BLK = SEGS["dt"][2] // 128


def _dt_prep(proj, bias128, alog128):
    t = proj.shape[0]
    tt = _divtile(t, 1024)

    def body(p_ref, bias_ref, alog_ref, dt_ref, adt_ref):
        dtv = jax.nn.softplus(p_ref[...] + bias_ref[...])
        dt_ref[...] = dtv
        adt_ref[...] = dtv * (-jnp.exp(alog_ref[...]))

    blk = pl.BlockSpec((tt, 128), lambda i: (i, 0))
    vec = pl.BlockSpec((1, 128), lambda i: (0, 0))
    return pl.pallas_call(
        body, name="dt_prep", grid=(t // tt,),
        in_specs=[pl.BlockSpec((tt, 128), lambda i: (i, DT_BLK)), vec, vec], out_specs=[blk, blk],
        out_shape=[jax.ShapeDtypeStruct((t, 128), F32)] * 2,
        compiler_params=_params(("parallel",), 16 << 20),
    )(proj, bias128, alog128)


def _dt_bwd(dadt, dxdx, proj, bias128, alog128):
    t = proj.shape[0]
    tt = _divtile(t, 1024)

    def body(dadt_ref, dxdx_ref, p_ref, bias_ref, alog_ref, o_ref, dbias_ref, dalog_ref):
        pre = p_ref[...] + bias_ref[...]
        dtv = jax.nn.softplus(pre)
        a = -jnp.exp(alog_ref[...])
        ddt = a * dadt_ref[...] + dxdx_ref[...]
        draw = ddt * _sigmoid(pre)
        o_ref[...] = draw.astype(BF16)

        @pl.when(pl.program_id(0) == 0)
        def _():
            dbias_ref[...] = jnp.zeros_like(dbias_ref)
            dalog_ref[...] = jnp.zeros_like(dalog_ref)

        dbias_ref[...] += jnp.sum(draw, axis=0, keepdims=True)
        dalog_ref[...] += jnp.sum(dadt_ref[...] * dtv * a, axis=0, keepdims=True)

    blk = pl.BlockSpec((tt, 128), lambda i: (i, 0))
    vec = pl.BlockSpec((1, 128), lambda i: (0, 0))
    return pl.pallas_call(
        body, name="dt_bwd", grid=(t // tt,),
        in_specs=[blk, blk, pl.BlockSpec((tt, 128), lambda i: (i, DT_BLK)), vec, vec],
        out_specs=[blk, vec, vec],
        out_shape=[jax.ShapeDtypeStruct((t, 128), BF16), jax.ShapeDtypeStruct((1, 128), F32),
                   jax.ShapeDtypeStruct((1, 128), F32)],
        compiler_params=_params(("arbitrary",), 16 << 20),
    )(dadt, dxdx, proj, bias128, alog128)


CONV_CB = 512
CONV_TT = 512


def _shift_down(cur, prev8, s):
    if s == 0:
        return cur
    rolled = pltpu.roll(cur, s, 0)
    head = pltpu.roll(prev8, s, 0)
    r8 = lax.broadcasted_iota(jnp.int32, (8, 1), 0)
    top = jnp.where(r8 < s, head, rolled[:8])
    return jnp.concatenate([top, rolled[8:]], axis=0)


def _shift_up(cur, next8, s):
    if s == 0:
        return cur
    n = cur.shape[0]
    rolled = pltpu.roll(cur, n - s, 0)
    tail = pltpu.roll(next8, 8 - s, 0)
    r8 = lax.broadcasted_iota(jnp.int32, (8, 1), 0)
    bot = jnp.where(r8 >= 8 - s, tail, rolled[n - 8:])
    return jnp.concatenate([rolled[:n - 8], bot], axis=0)


def _conv_fwd(proj, conv_w, conv_b):
    t = proj.shape[0]
    tt = _divtile(t, CONV_TT)
    base = SEGS["xbc"][2] // CONV_CB
    r8 = tt // 8

    def body(u_ref, up_ref, w_ref, b_ref, o_ref):
        cur = u_ref[...]
        prev8 = jnp.where(pl.program_id(1) > 0, up_ref[...], 0.0)
        acc = b_ref[...] + w_ref[3:4, :] * cur
        for k in range(3):
            acc = acc + w_ref[k:k + 1, :] * _shift_down(cur, prev8, 3 - k)
        o_ref[...] = acc * _sigmoid(acc)

    return pl.pallas_call(
        body, name="conv_fwd", grid=(XBC // CONV_CB, t // tt),
        in_specs=[pl.BlockSpec((tt, CONV_CB), lambda c, i: (i, base + c)),
                  pl.BlockSpec((8, CONV_CB), lambda c, i: (jnp.maximum(i * r8 - 1, 0), base + c)),
                  pl.BlockSpec((4, CONV_CB), lambda c, i: (0, c)),
                  pl.BlockSpec((1, CONV_CB), lambda c, i: (0, c))],
        out_specs=pl.BlockSpec((tt, CONV_CB), lambda c, i: (i, c)),
        out_shape=jax.ShapeDtypeStruct((t, XBC), F32),
        compiler_params=_params(("parallel", "parallel"), 24 << 20),
    )(proj, proj, conv_w, conv_b)


def _conv_bwd(proj, dout, conv_w, conv_b, col0, width, name, skip=None):
    t = proj.shape[0]
    tt = _divtile(t, CONV_TT)
    nt = t // tt
    base = SEGS["xbc"][2] // CONV_CB + col0 // CONV_CB
    wb = col0 // CONV_CB
    r8 = tt // 8
    has_skip = skip is not None

    def body(*refs):
        if has_skip:
            u_ref, up_ref, d_ref, w_ref, b_ref, sk_ref, skw_ref, du_ref, dw_ref, db_ref, nx_ref = refs
        else:
            u_ref, up_ref, d_ref, w_ref, b_ref, du_ref, dw_ref, db_ref, nx_ref = refs
        i = pl.program_id(1)
        cur = u_ref[...]
        prev8 = jnp.where(i < nt - 1, up_ref[...], 0.0)
        sh = [_shift_down(cur, prev8, 3 - k) for k in range(3)] + [cur]
        pre = b_ref[...]
        for k in range(4):
            pre = pre + w_ref[k:k + 1, :] * sh[k]
        sg = _sigmoid(pre)
        dout_v = d_ref[...]
        if has_skip:
            dout_v = dout_v + sk_ref[...] * skw_ref[...]
        dpre = dout_v * (sg * (1.0 + pre * (1.0 - sg)))

        @pl.when(i == 0)
        def _():
            nx_ref[...] = jnp.zeros_like(nx_ref)
            dw_ref[...] = jnp.zeros_like(dw_ref)
            db_ref[...] = jnp.zeros_like(db_ref)

        next8 = nx_ref[...]
        du = w_ref[3:4, :] * dpre
        for s in range(1, 4):
            du = du + w_ref[3 - s:4 - s, :] * _shift_up(dpre, next8, s)
        du_ref[...] = du.astype(BF16)
        nx_ref[...] = dpre[:8]
        rows = [jnp.sum(dpre * sh[k], axis=0, keepdims=True) for k in range(4)]
        dw_ref[...] += jnp.concatenate(rows + [jnp.zeros((4, CONV_CB), F32)], axis=0)
        db_ref[...] += jnp.sum(dpre, axis=0, keepdims=True)

    rev = lambda c, i: (nt - 1 - i, c)
    in_specs = [pl.BlockSpec((tt, CONV_CB), lambda c, i: (nt - 1 - i, base + c)),
                pl.BlockSpec((8, CONV_CB), lambda c, i: (jnp.maximum((nt - 1 - i) * r8 - 1, 0), base + c)),
                pl.BlockSpec((tt, CONV_CB), rev),
                pl.BlockSpec((4, CONV_CB), lambda c, i: (0, wb + c)),
                pl.BlockSpec((1, CONV_CB), lambda c, i: (0, wb + c))]
    args = [proj, proj, dout, conv_w, conv_b]
    if has_skip:
        in_specs += [pl.BlockSpec((tt, CONV_CB), rev), pl.BlockSpec((1, CONV_CB), lambda c, i: (0, c))]
        args += [skip[0], skip[1]]
    return pl.pallas_call(
        body, name=name, grid=(width // CONV_CB, nt),
        in_specs=in_specs,
        out_specs=[pl.BlockSpec((tt, CONV_CB), rev), pl.BlockSpec((8, CONV_CB), lambda c, i: (0, c)),
                   pl.BlockSpec((1, CONV_CB), lambda c, i: (0, c))],
        out_shape=[jax.ShapeDtypeStruct((t, width), BF16), jax.ShapeDtypeStruct((8, width), F32),
                   jax.ShapeDtypeStruct((1, width), F32)],
        scratch_shapes=[pltpu.VMEM((8, CONV_CB), F32)],
        compiler_params=_params(("parallel", "arbitrary"), 32 << 20),
    )(*args)


GW = SSD_R * SSD_P


def _expand8(v):
    r = v.shape[0]
    return jnp.concatenate([jnp.broadcast_to(v[:, h:h + 1], (r, SSD_P)) for h in range(SSD_R)], axis=1)


def _sel(rows, group):
    ri = lax.broadcasted_iota(jnp.int32, (rows, rows // group), 0)
    ci = lax.broadcasted_iota(jnp.int32, (rows, rows // group), 1)
    lo = ci * group
    return jnp.where((ri >= lo) & (ri < lo + group), 1.0, 0.0).astype(F32)


def _dot01(lhs, rhs, passes, split_lhs=True, dims=((1,), (0,))):
    val, m01 = (lhs, rhs) if split_lhs else (rhs, lhs)
    m01 = m01.astype(BF16)
    out = None
    for p in range(passes):
        piece = val.astype(BF16)
        ops = (piece, m01) if split_lhs else (m01, piece)
        d = lax.dot_general(ops[0], ops[1], (dims, ((), ())), preferred_element_type=F32)
        out = d if out is None else out + d
        if p + 1 < passes:
            val = val - piece.astype(F32)
    return out


def _ssd_chunk_terms(adt):
    li = lax.broadcasted_iota(jnp.int32, (CHUNK, CHUNK), 0)
    si = lax.broadcasted_iota(jnp.int32, (CHUNK, CHUNK), 1)
    causal = li >= si
    a_cs = _dot01(jnp.where(causal, 1.0, 0.0), adt, 3, split_lhs=False)
    a_cs_t = _dot01(adt, jnp.where(li <= si, 1.0, 0.0), 3, dims=((0,), (0,)))
    return a_cs, a_cs_t, causal


def _ssd_fwd(xc, dt3, adt3):
    t = xc.shape[0]
    nc = t // CHUNK

    gs = 2

    def body(xs_ref, b_ref, c_ref, dt_ref, adt_ref, y_ref, hp_ref, h_ref):
        @pl.when(pl.program_id(1) == 0)
        def _():
            h_ref[...] = jnp.zeros_like(h_ref)

        for gg in range(gs):
            a_cs, a_cs_t, causal = _ssd_chunk_terms(adt_ref[gg])
            a_last = a_cs[CHUNK - 1:CHUNK, :]
            h = h_ref[gg]
            hp_ref[gg, 0] = h
            xd = xs_ref[:, GW * gg:GW * (gg + 1)] * _expand8(dt_ref[gg])
            bb = b_ref[:, SSD_N * gg:SSD_N * (gg + 1)].astype(BF16)
            cbf = c_ref[:, SSD_N * gg:SSD_N * (gg + 1)].astype(BF16)
            cb = lax.dot_general(cbf, bb, (((1,), (1,)), ((), ())), preferred_element_type=F32)
            yoff = jnp.dot(cbf, h.astype(BF16), preferred_element_type=F32) * _expand8(jnp.exp(a_cs))
            for r in range(SSD_R):
                seg = jnp.exp(jnp.where(causal, a_cs[:, r:r + 1] - a_cs_t[r:r + 1, :], -jnp.inf))
                lmat = (cb * seg).astype(BF16)
                sl = slice(SSD_P * r, SSD_P * (r + 1))
                y_ref[:, GW * gg + SSD_P * r:GW * gg + SSD_P * (r + 1)] = (
                    jnp.dot(lmat, xd[:, sl].astype(BF16), preferred_element_type=F32) + yoff[:, sl])
            xdd = (xd * _expand8(jnp.exp(a_last - a_cs))).astype(BF16)
            h_ref[gg] = _expand8(jnp.exp(a_last)) * h + lax.dot_general(
                bb, xdd, (((0,), (0,)), ((), ())), preferred_element_type=F32)

    nb = SSD_INNER // (gs * SSD_N)
    return pl.pallas_call(
        body, name="ssd_fwd", grid=(SSD_G // gs, nc),
        in_specs=[pl.BlockSpec((CHUNK, gs * GW), lambda g, c: (c, g)),
                  pl.BlockSpec((CHUNK, gs * SSD_N), lambda g, c: (c, nb + g)),
                  pl.BlockSpec((CHUNK, gs * SSD_N), lambda g, c: (c, nb + SSD_G // gs + g)),
                  pl.BlockSpec((gs, CHUNK, SSD_R), lambda g, c: (g, c, 0)),
                  pl.BlockSpec((gs, CHUNK, SSD_R), lambda g, c: (g, c, 0))],
        out_specs=[pl.BlockSpec((CHUNK, gs * GW), lambda g, c: (c, g)),
                   pl.BlockSpec((gs, 1, SSD_N, GW), lambda g, c: (g, c, 0, 0))],
        out_shape=[jax.ShapeDtypeStruct((t, SSD_INNER), F32), jax.ShapeDtypeStruct((SSD_G, nc, SSD_N, GW), F32)],
        scratch_shapes=[pltpu.VMEM((gs, SSD_N, GW), F32)],
        compiler_params=_params(("parallel", "arbitrary"), 32 << 20),
    )(xc, xc, xc, dt3, adt3)


def _ssd_bwd(xc, dt3, adt3, hprev, dy):
    t = xc.shape[0]
    nc = t // CHUNK

    def body(xs_ref, b_ref, c_ref, dt_ref, adt_ref, hp_ref, dy_ref,
             dx_ref, db_ref, dc_ref, dadt_ref, dxdx_ref, dh_ref):
        @pl.when(pl.program_id(1) == 0)
        def _():
            dh_ref[...] = jnp.zeros_like(dh_ref)

        a_cs, a_cs_t, causal = _ssd_chunk_terms(adt_ref[0])
        a_last = a_cs[CHUNK - 1:CHUNK, :]
        e_last = jnp.exp(a_last)
        ex = _expand8(jnp.exp(a_cs))
        dtex = _expand8(jnp.exp(a_last - a_cs))
        dtx = _expand8(dt_ref[0])
        xs = xs_ref[...]
        dyv = dy_ref[...]
        hp = hp_ref[0, 0]
        dh = dh_ref[...]
        sel = _sel(GW, SSD_P)
        seg8 = lambda v: _dot01(v, sel, 2)

        xd = xs * dtx
        xdd = xd * dtex
        bb = b_ref[...].astype(BF16)
        cbf = c_ref[...].astype(BF16)
        hpb = hp.astype(BF16)
        dhb = dh.astype(BF16)
        xdb = xd.astype(BF16)
        dyb = dyv.astype(BF16)
        cb = lax.dot_general(cbf, bb, (((1,), (1,)), ((), ())), preferred_element_type=F32)
        dye = (dyv * ex).astype(BF16)
        yoff = jnp.dot(cbf, hpb, preferred_element_type=F32) * ex
        dc = lax.dot_general(dye, hpb, (((1,), (1,)), ((), ())), preferred_element_type=F32)
        d_a = seg8(dyv * yoff)
        bdh = jnp.dot(bb, dhb, preferred_element_type=F32)
        db = lax.dot_general(xdd.astype(BF16), dhb, (((1,), (1,)), ((), ())), preferred_element_type=F32)
        dxd_state = bdh * dtex
        q = seg8(xdd * bdh)
        d_a = d_a - q
        d_a_last = (jnp.sum(q, axis=0, keepdims=True)
                    + e_last * seg8(jnp.sum(hp * dh, axis=0, keepdims=True)))
        dh_ref[...] = (lax.dot_general(cbf, dye, (((0,), (0,)), ((), ())), preferred_element_type=F32)
                       + _expand8(e_last) * dh)
        dcb = jnp.zeros((CHUNK, CHUNK), F32)
        w_all = []
        dxd_parts = []
        for r in range(SSD_R):
            seg = jnp.exp(jnp.where(causal, a_cs[:, r:r + 1] - a_cs_t[r:r + 1, :], -jnp.inf))
            lmat = cb * seg
            sl = slice(SSD_P * r, SSD_P * (r + 1))
            dm = lax.dot_general(dyb[:, sl], xdb[:, sl], (((1,), (1,)), ((), ())), preferred_element_type=F32)
            dxd_parts.append(lax.dot_general(lmat.astype(BF16), dyb[:, sl], (((0,), (0,)), ((), ())),
                                             preferred_element_type=F32))
            dcb = dcb + dm * seg
            w_all.append(dm * lmat)
        row_sums = _dot01(jnp.concatenate(w_all, axis=1), _sel(SSD_R * CHUNK, CHUNK), 2)
        cs_rows = jnp.concatenate([jnp.sum(wr, axis=0, keepdims=True) for wr in w_all], axis=0)
        col_sums = _dot01(cs_rows, _sel(SSD_R, 1), 3, dims=((0,), (0,)))
        d_a = d_a + row_sums - col_sums
        li = lax.broadcasted_iota(jnp.int32, (CHUNK, SSD_R), 0)
        d_a = d_a + jnp.where(li == CHUNK - 1, d_a_last, 0.0)
        l2 = lax.broadcasted_iota(jnp.int32, (CHUNK, CHUNK), 0)
        s2 = lax.broadcasted_iota(jnp.int32, (CHUNK, CHUNK), 1)
        dadt_ref[0] = _dot01(jnp.where(s2 >= l2, 1.0, 0.0), d_a, 3, split_lhs=False)
        dxd = dxd_state + jnp.concatenate(dxd_parts, axis=1)
        dxdx_ref[0] = seg8(dxd * xs)
        dx_ref[...] = dxd * dtx
        dcbb = dcb.astype(BF16)
        db_ref[...] = db + lax.dot_general(dcbb, cbf, (((0,), (0,)), ((), ())), preferred_element_type=F32)
        dc_ref[...] = dc + jnp.dot(dcbb, bb, preferred_element_type=F32)

    nb = SSD_INNER // SSD_N
    rc = lambda g, c: (nc - 1 - c, g)
    r3 = lambda g, c: (g, nc - 1 - c, 0)
    return pl.pallas_call(
        body, name="ssd_bwd", grid=(SSD_G, nc),
        in_specs=[pl.BlockSpec((CHUNK, GW), rc),
                  pl.BlockSpec((CHUNK, SSD_N), lambda g, c: (nc - 1 - c, nb + g)),
                  pl.BlockSpec((CHUNK, SSD_N), lambda g, c: (nc - 1 - c, nb + SSD_G + g)),
                  pl.BlockSpec((1, CHUNK, SSD_R), r3),
                  pl.BlockSpec((1, CHUNK, SSD_R), r3),
                  pl.BlockSpec((1, 1, SSD_N, GW), lambda g, c: (g, nc - 1 - c, 0, 0)),
                  pl.BlockSpec((CHUNK, GW), rc)],
        out_specs=[pl.BlockSpec((CHUNK, GW), rc),
                   pl.BlockSpec((CHUNK, SSD_N), rc),
                   pl.BlockSpec((CHUNK, SSD_N), rc),
                   pl.BlockSpec((1, CHUNK, SSD_R), r3),
                   pl.BlockSpec((1, CHUNK, SSD_R), r3)],
        out_shape=[jax.ShapeDtypeStruct((t, SSD_INNER), F32),
                   jax.ShapeDtypeStruct((t, SSD_G * SSD_N), F32),
                   jax.ShapeDtypeStruct((t, SSD_G * SSD_N), F32),
                   jax.ShapeDtypeStruct((SSD_G, t, SSD_R), F32),
                   jax.ShapeDtypeStruct((SSD_G, t, SSD_R), F32)],
        scratch_shapes=[pltpu.VMEM((SSD_N, GW), F32)],
        compiler_params=_params(("parallel", "arbitrary"), 40 << 20),
    )(xc, xc, xc, dt3, adt3, hprev, dy)


def _gated_norm_fwd(y, xc, proj, dexp, ng):
    t = y.shape[0]
    tt = _divtile(t, 256)

    def body(y_ref, x_ref, z_ref, d_ref, g_ref, o_ref):
        z = z_ref[...]
        y2 = (y_ref[...] + d_ref[...] * x_ref[...]) * (z * _sigmoid(z))
        for gi in range(SSD_G):
            sl = slice(GW * gi, GW * (gi + 1))
            seg = y2[:, sl]
            rinv = lax.rsqrt(jnp.mean(seg * seg, axis=-1, keepdims=True) + RMS_EPS)
            o_ref[:, sl] = (seg * rinv * g_ref[:, sl]).astype(BF16)

    row = pl.BlockSpec((tt, SSD_INNER), lambda i: (i, 0))
    vec = pl.BlockSpec((1, SSD_INNER), lambda i: (0, 0))
    return pl.pallas_call(
        body, name="gated_norm_fwd", grid=(t // tt,), in_specs=[row, row, row, vec, vec], out_specs=row,
        out_shape=jax.ShapeDtypeStruct((t, SSD_INNER), BF16),
        compiler_params=_params(("parallel",), 48 << 20),
    )(y, xc, proj, dexp, ng)


def _gated_norm_bwd(dout, y, xc, proj, dexp, ng):
    t = y.shape[0]
    tt = _divtile(t, 128)

    def body(do_ref, y_ref, x_ref, z_ref, d_ref, g_ref, dz_ref, dy_ref, dg_ref, dd_ref):
        z = z_ref[...]
        sg = _sigmoid(z)
        sz = z * sg
        xs = x_ref[...]
        y1 = y_ref[...] + d_ref[...] * xs
        y2 = y1 * sz
        dov = do_ref[...]

        @pl.when(pl.program_id(0) == 0)
        def _():
            dg_ref[...] = jnp.zeros_like(dg_ref)
            dd_ref[...] = jnp.zeros_like(dd_ref)

        for gi in range(SSD_G):
            sl = slice(GW * gi, GW * (gi + 1))
            seg = y2[:, sl]
            rinv = lax.rsqrt(jnp.mean(seg * seg, axis=-1, keepdims=True) + RMS_EPS)
            yn = seg * rinv
            dsl = dov[:, sl]
            dg_ref[:, sl] += jnp.sum(dsl * yn, axis=0, keepdims=True)
            dyn = dsl * g_ref[:, sl]
            dy2 = rinv * (dyn - yn * jnp.mean(dyn * yn, axis=-1, keepdims=True))
            dz_ref[:, sl] = (dy2 * y1[:, sl] * (sg[:, sl] * (1.0 + z[:, sl] * (1.0 - sg[:, sl])))).astype(BF16)
            dy1 = dy2 * sz[:, sl]
            dy_ref[:, sl] = dy1
            dd_ref[:, sl] += jnp.sum(dy1 * xs[:, sl], axis=0, keepdims=True)

    row = pl.BlockSpec((tt, SSD_INNER), lambda i: (i, 0))
    vec = pl.BlockSpec((1, SSD_INNER), lambda i: (0, 0))
    return pl.pallas_call(
        body, name="gated_norm_bwd", grid=(t // tt,), in_specs=[row, row, row, row, vec, vec],
        out_specs=[row, row, vec, vec],
        out_shape=[jax.ShapeDtypeStruct((t, SSD_INNER), BF16), jax.ShapeDtypeStruct((t, SSD_INNER), F32),
                   jax.ShapeDtypeStruct((1, SSD_INNER), F32), jax.ShapeDtypeStruct((1, SSD_INNER), F32)],
        compiler_params=_params(("arbitrary",), 48 << 20),
    )(dout, y, xc, proj, dexp, ng)


def _fold_heads(v, name):
    def body(v_ref, o_ref):
        ri = lax.broadcasted_iota(jnp.int32, (SSD_INNER, 128), 0)
        ci = lax.broadcasted_iota(jnp.int32, (SSD_INNER, 128), 1)
        fold = jnp.where((ri >= ci * SSD_P) & (ri < (ci + 1) * SSD_P), 1.0, 0.0).astype(F32)
        o_ref[...] = jnp.dot(v_ref[...], fold, preferred_element_type=F32, precision=HI)

    return pl.pallas_call(body, name=name, out_shape=jax.ShapeDtypeStruct((1, 128), F32))(v)


Q_BLK = SEGS["q"][2] // QW
K_BLK = SEGS["k"][2] // KVW
V_BLK = SEGS["v"][2] // KVW


def _rope_tables(pos_ref, invf_ref, width):
    ang = pos_ref[...] * invf_ref[...]
    lane = lax.broadcasted_iota(jnp.int32, (1, 128), 1)
    sign = jnp.where((lane % HD) < (HD // 2), -1.0, 1.0)
    cos = jnp.tile(jnp.cos(ang), (1, width // 128))
    sin = jnp.tile(sign * jnp.sin(ang), (1, width // 128))
    first = (lax.broadcasted_iota(jnp.int32, (1, width), 1) % HD) < (HD // 2)
    return cos, sin, first


def _rot_half(u, first):
    w = u.shape[1]
    return jnp.where(first, pltpu.roll(u, w - HD // 2, 1), pltpu.roll(u, HD // 2, 1))


def _rope_fwd(proj, pos, invf):
    t = proj.shape[0]
    tt = _divtile(t, 512)

    def body(q_ref, k_ref, pos_ref, invf_ref, qo_ref, ko_ref):
        cos, sin, first = _rope_tables(pos_ref, invf_ref, QW)
        q = q_ref[...]
        qo_ref[...] = (q * cos + _rot_half(q, first) * sin).astype(BF16)
        k = k_ref[...]
        ko_ref[...] = (k * cos[:, :KVW] + _rot_half(k, first[:, :KVW]) * sin[:, :KVW]).astype(BF16)

    return pl.pallas_call(
        body, name="rope_fwd", grid=(t // tt,),
        in_specs=[pl.BlockSpec((tt, QW), lambda i: (i, Q_BLK)), pl.BlockSpec((tt, KVW), lambda i: (i, K_BLK)),
                  pl.BlockSpec((tt, 1), lambda i: (i, 0)), pl.BlockSpec((1, 128), lambda i: (0, 0))],
        out_specs=[pl.BlockSpec((tt, QW), lambda i: (i, 0)), pl.BlockSpec((tt, KVW), lambda i: (i, 0))],
        out_shape=[jax.ShapeDtypeStruct((t, QW), BF16), jax.ShapeDtypeStruct((t, KVW), BF16)],
        compiler_params=_params(("parallel",), 40 << 20),
    )(proj, proj, pos, invf)


def _rope_bwd(dq, dk, pos, invf):
    t = dq.shape[0]
    tt = _divtile(t, 512)

    def body(dq_ref, dk_ref, pos_ref, invf_ref, qo_ref, ko_ref):
        cos, sin, first = _rope_tables(pos_ref, invf_ref, QW)
        q = dq_ref[...]
        qo_ref[...] = (q * cos + _rot_half(q * sin, first)).astype(BF16)
        k = dk_ref[...]
        ko_ref[...] = (k * cos[:, :KVW] + _rot_half(k * sin[:, :KVW], first[:, :KVW])).astype(BF16)

    return pl.pallas_call(
        body, name="rope_bwd", grid=(t // tt,),
        in_specs=[pl.BlockSpec((tt, QW), lambda i: (i, 0)), pl.BlockSpec((tt, KVW), lambda i: (i, 0)),
                  pl.BlockSpec((tt, 1), lambda i: (i, 0)), pl.BlockSpec((1, 128), lambda i: (0, 0))],
        out_specs=[pl.BlockSpec((tt, QW), lambda i: (i, 0)), pl.BlockSpec((tt, KVW), lambda i: (i, 0))],
        out_shape=[jax.ShapeDtypeStruct((t, QW), BF16), jax.ShapeDtypeStruct((t, KVW), BF16)],
        compiler_params=_params(("parallel",), 40 << 20),
    )(dq, dk, pos, invf)


GQ = NQ // NKV
NT_DIMS = (((1,), (1,)), ((), ()))
TN_DIMS = (((0,), (0,)), ((), ()))


def _attn_stack(ref, j, dtype=None):
    parts = [ref[:, HD * h:HD * (h + 1)] for h in range(j * GQ, (j + 1) * GQ)]
    out = jnp.concatenate(parts, axis=0)
    return out if dtype is None else out.astype(dtype)


def _attn_sink_col(s_ref, j):
    return jnp.concatenate([jnp.broadcast_to(s_ref[:, h:h + 1], (WINDOW, 1)) for h in range(j * GQ, (j + 1) * GQ)],
                           axis=0)


def _attn_mask(n):
    qi = lax.broadcasted_iota(jnp.int32, (GQ * WINDOW, 2 * WINDOW), 0) % WINDOW
    kj = lax.broadcasted_iota(jnp.int32, (GQ * WINDOW, 2 * WINDOW), 1)
    return (kj > qi) & (kj <= qi + WINDOW) & ((n > 0) | (kj >= WINDOW))


def _attn_exp(qg, kk, sink, mask):
    s = jnp.where(mask, lax.dot_general(qg, kk, NT_DIMS, preferred_element_type=F32) * (HD ** -0.5), -jnp.inf)
    m = jnp.maximum(jnp.max(s, axis=-1, keepdims=True), sink)
    return jnp.exp(s - m), jnp.exp(sink - m)


def _attn_fwd(qr, kr, proj, sinks):
    t = qr.shape[0]
    nb = t // WINDOW

    def body(q_ref, kc_ref, kp_ref, vc_ref, vp_ref, s_ref, o_ref):
        mask = _attn_mask(pl.program_id(0))
        ones = jnp.ones((2 * WINDOW, HD), BF16)
        for j in range(NKV):
            ks = slice(HD * j, HD * (j + 1))
            kk = jnp.concatenate([kp_ref[:, ks], kc_ref[:, ks]], axis=0)
            vv = jnp.concatenate([vp_ref[:, ks], vc_ref[:, ks]], axis=0).astype(BF16)
            p, ps = _attn_exp(_attn_stack(q_ref, j), kk, _attn_sink_col(s_ref, j), mask)
            oa = jnp.dot(p.astype(BF16), jnp.concatenate([vv, ones], axis=1), preferred_element_type=F32)
            o = (oa[:, :HD] * (1.0 / (oa[:, HD:HD + 1] + ps))).astype(BF16)
            for g in range(GQ):
                h = j * GQ + g
                o_ref[:, HD * h:HD * (h + 1)] = o[WINDOW * g:WINDOW * (g + 1)]

    prev = lambda n: (jnp.maximum(n - 1, 0), 0)
    return pl.pallas_call(
        body, name="attn_fwd", grid=(nb,),
        in_specs=[pl.BlockSpec((WINDOW, QW), lambda n: (n, 0)),
                  pl.BlockSpec((WINDOW, KVW), lambda n: (n, 0)), pl.BlockSpec((WINDOW, KVW), prev),
                  pl.BlockSpec((WINDOW, KVW), lambda n: (n, V_BLK)),
                  pl.BlockSpec((WINDOW, KVW), lambda n: (jnp.maximum(n - 1, 0), V_BLK)),
                  pl.BlockSpec((1, 128), lambda n: (0, 0))],
        out_specs=pl.BlockSpec((WINDOW, QW), lambda n: (n, 0)),
        out_shape=jax.ShapeDtypeStruct((t, QW), BF16),
        compiler_params=_params(("parallel",), 24 << 20),
    )(qr, kr, kr, proj, proj, sinks)


def _attn_bwd(qr, kr, proj, sinks, do):
    t = qr.shape[0]
    nb = t // WINDOW

    def body(q_ref, kc_ref, kp_ref, vc_ref, vp_ref, s_ref, do_ref,
             dq_ref, dk_ref, dv_ref, ds_ref, dkc_ref, dvc_ref):
        i = pl.program_id(0)
        mask = _attn_mask(nb - 1 - i)

        @pl.when(i == 0)
        def _():
            dkc_ref[...] = jnp.zeros_like(dkc_ref)
            dvc_ref[...] = jnp.zeros_like(dvc_ref)
            ds_ref[...] = jnp.zeros_like(ds_ref)

        lane = lax.broadcasted_iota(jnp.int32, (1, 128), 1)
        ds_acc = jnp.zeros((1, 128), F32)
        for j in range(NKV):
            ks = slice(HD * j, HD * (j + 1))
            kk = jnp.concatenate([kp_ref[:, ks], kc_ref[:, ks]], axis=0)
            vv = jnp.concatenate([vp_ref[:, ks], vc_ref[:, ks]], axis=0).astype(BF16)
            qg = _attn_stack(q_ref, j)
            p, ps = _attn_exp(qg, kk, _attn_sink_col(s_ref, j), mask)
            inv = 1.0 / (jnp.sum(p, axis=-1, keepdims=True) + ps)
            pn = p * inv
            dog = _attn_stack(do_ref, j, BF16)
            dp = lax.dot_general(dog, vv, NT_DIMS, preferred_element_type=F32)
            delta = jnp.sum(dp * pn, axis=-1, keepdims=True)
            dsb = (pn * (dp - delta) * (HD ** -0.5)).astype(BF16)
            dsink = -(ps * inv) * delta
            dq = jnp.dot(dsb, kk, preferred_element_type=F32)
            for g in range(GQ):
                h = j * GQ + g
                rows = slice(WINDOW * g, WINDOW * (g + 1))
                dq_ref[:, HD * h:HD * (h + 1)] = dq[rows]
                ds_acc = ds_acc + jnp.where(lane == h, jnp.sum(dsink[rows], axis=0, keepdims=True), 0.0)
            dkk = lax.dot_general(dsb, qg, TN_DIMS, preferred_element_type=F32)
            dvv = lax.dot_general(pn.astype(BF16), dog, TN_DIMS, preferred_element_type=F32)
            dk_ref[:, ks] = dkk[WINDOW:] + dkc_ref[:, ks]
            dv_ref[:, ks] = (dvv[WINDOW:] + dvc_ref[:, ks]).astype(BF16)
            dkc_ref[:, ks] = dkk[:WINDOW]
            dvc_ref[:, ks] = dvv[:WINDOW]
        ds_ref[...] += ds_acc

    cur = lambda i: (nb - 1 - i, 0)
    prev = lambda i: (jnp.maximum(nb - 2 - i, 0), 0)
    return pl.pallas_call(
        body, name="attn_bwd", grid=(nb,),
        in_specs=[pl.BlockSpec((WINDOW, QW), cur),
                  pl.BlockSpec((WINDOW, KVW), cur), pl.BlockSpec((WINDOW, KVW), prev),
                  pl.BlockSpec((WINDOW, KVW), lambda i: (nb - 1 - i, V_BLK)),
                  pl.BlockSpec((WINDOW, KVW), lambda i: (jnp.maximum(nb - 2 - i, 0), V_BLK)),
                  pl.BlockSpec((1, 128), lambda i: (0, 0)),
                  pl.BlockSpec((WINDOW, QW), cur)],
        out_specs=[pl.BlockSpec((WINDOW, QW), cur), pl.BlockSpec((WINDOW, KVW), cur),
                   pl.BlockSpec((WINDOW, KVW), cur), pl.BlockSpec((1, 128), lambda i: (0, 0))],
        out_shape=[jax.ShapeDtypeStruct((t, QW), F32), jax.ShapeDtypeStruct((t, KVW), F32),
                   jax.ShapeDtypeStruct((t, KVW), BF16), jax.ShapeDtypeStruct((1, 128), F32)],
        scratch_shapes=[pltpu.VMEM((WINDOW, KVW), F32), pltpu.VMEM((WINDOW, KVW), F32)],
        compiler_params=_params(("arbitrary",), 32 << 20),
    )(qr, kr, kr, proj, proj, sinks, do)


GS_BLK = SEGS["gs"][2] // D
GA_BLK = SEGS["ga"][2] // D


def _merge_fwd(ys, ya, proj):
    t = ys.shape[0]
    tt = _divtile(t, 256)

    def body(ys_ref, ya_ref, gs_ref, ga_ref, o_ref):
        o_ref[...] = (_sigmoid(gs_ref[...]) * ys_ref[...] + _sigmoid(ga_ref[...]) * ya_ref[...]).astype(BF16)

    row = pl.BlockSpec((tt, D), lambda i: (i, 0))
    return pl.pallas_call(
        body, name="merge_fwd", grid=(t // tt,),
        in_specs=[row, row, pl.BlockSpec((tt, D), lambda i: (i, GS_BLK)), pl.BlockSpec((tt, D), lambda i: (i, GA_BLK))],
        out_specs=row, out_shape=jax.ShapeDtypeStruct((t, D), BF16),
        compiler_params=_params(("parallel",), 32 << 20),
    )(ys, ya, proj, proj)


def _merge_bwd(dm, ys, ya, proj):
    t = ys.shape[0]
    tt = _divtile(t, 256)

    def body(dm_ref, ys_ref, ya_ref, gs_ref, ga_ref, dys_ref, dya_ref, dgs_ref, dga_ref):
        d = dm_ref[...]
        s = _sigmoid(gs_ref[...])
        a = _sigmoid(ga_ref[...])
        dys_ref[...] = (d * s).astype(BF16)
        dya_ref[...] = (d * a).astype(BF16)
        dgs_ref[...] = (d * ys_ref[...] * (s * (1.0 - s))).astype(BF16)
        dga_ref[...] = (d * ya_ref[...] * (a * (1.0 - a))).astype(BF16)

    row = pl.BlockSpec((tt, D), lambda i: (i, 0))
    return pl.pallas_call(
        body, name="merge_bwd", grid=(t // tt,),
        in_specs=[row, row, row, pl.BlockSpec((tt, D), lambda i: (i, GS_BLK)),
                  pl.BlockSpec((tt, D), lambda i: (i, GA_BLK))],
        out_specs=[row, row, row, row], out_shape=[jax.ShapeDtypeStruct((t, D), BF16)] * 4,
        compiler_params=_params(("parallel",), 40 << 20),
    )(dm, ys, ya, proj, proj)


def _pad128(v):
    return jnp.pad(v, ((0, 0), (0, 128 - v.shape[1])))


def _group_major(v):
    t = v.shape[0]
    return jnp.transpose(v[:, :SSD_HEADS].reshape(t, SSD_G, SSD_R), (1, 0, 2))


def _token_major(v3):
    t = v3.shape[1]
    return _pad128(jnp.transpose(v3, (1, 0, 2)).reshape(t, SSD_HEADS))


def _local_step(x, pos, target, w, small, start_token=None, later_weights=None, early_grads=None):
    xb = x.astype(BF16) if start_token is None else (x + start_token[0:1, 0:1]).astype(BF16)
    gu1, a1 = _mm_swiglu(xb, w["gu1"], "ffn1_gu")
    f1 = _mm(a1, w["d1"], "nn", F32, "ffn1_down", caps=(1024, 1024, 1408))
    h1, h1b, xh1, rs1 = _ln_fwd(x, f1, small["ln1_g"], small["ln1_b"], 0.5, "ln1_fwd")
    if later_weights is not None:
        w = {**w, **later_weights(h1b)}
    proj = _mm(h1b, w["win"], "nn", F32, "proj", caps=(1024, 896, 2048))
    bias128 = _pad128(small["dt_bias"])
    alog128 = _pad128(small["a_log"])
    dt, adt = _dt_prep(proj, bias128, alog128)
    dt3, adt3 = _group_major(dt), _group_major(adt)
    xc = _conv_fwd(proj, small["conv_w"], small["conv_b"])
    y_ssd, hprev = _ssd_fwd(xc, dt3, adt3)
    dexp = jnp.repeat(small["d_skip"], SSD_P, axis=1)
    ysn = _gated_norm_fwd(y_ssd, xc, proj, dexp, small["ssd_norm_g"])
    ys = _mm(ysn, w["so"], "nn", F32, "ssd_out")
    invf = jnp.tile(ROPE_THETA ** (-jnp.arange(HD // 2, dtype=F32) * 2.0 / HD), 4)[None, :]
    qr, kr = _rope_fwd(proj, pos, invf)
    sinks128 = _pad128(small["attn_sinks"])
    o = _attn_fwd(qr, kr, proj, sinks128)
    ya = _mm(o, w["ao"], "nn", F32, "attn_out")
    mg = _merge_fwd(ys, ya, proj)
    mix = _mm(mg, w["out"], "nn", F32, "mix_out")
    h2, h2b, xh2, rs2 = _ln_fwd(h1, mix, small["ln2_g"], small["ln2_b"], 1.0, "ln2_fwd")
    gu2, a2 = _mm_swiglu(h2b, w["gu2"], "ffn2_gu")
    f2 = _mm(a2, w["d2"], "nn", F32, "ffn2_down", caps=(1024, 1024, 1408))
    _, _, xh3, rs3, dh3, loss = _ln_fwd(h2, f2, small["ln3_g"], small["ln3_b"], 0.5, "ln3_fwd", target=target)

    gw, gs = {}, {}
    dr3, dr3h, gs["ln3_g"], gs["ln3_b"] = _ln_bwd(dh3, xh3, rs3, small["ln3_g"], 0.5, "ln3_bwd")
    gw["d2"] = _mm(a2, dr3h, "tn", F32, "ffn2_down_dw")
    da2 = _mm(dr3h, w["d2"], "nt", BF16, "ffn2_down_dx", caps=(1024, 1408, 2048))
    dgu2 = _swiglu_bwd(gu2, da2, "ffn2_act_bwd")
    gw["gu2"] = _mm(h2b, dgu2, "tn", F32, "ffn2_gu_dw", caps=(1024, 1408, 2048), n_slabs=N_CHIPS)
    dh2 = _mm(dgu2, w["gu2"], "nt", F32, "ffn2_gu_dx", add=dr3, add_scale=ALPHA, caps=(1024, 1024, 2816))
    dr2, dr2b, gs["ln2_g"], gs["ln2_b"] = _ln_bwd(dh2, xh2, rs2, small["ln2_g"], 1.0, "ln2_bwd")
    gw["out"] = _mm(mg, dr2b, "tn", F32, "mix_out_dw")
    dmg = _mm(dr2b, w["out"], "nt", F32, "mix_out_dx")
    dys, dya, dgs, dga = _merge_bwd(dmg, ys, ya, proj)
    gw["ao"] = _mm(o, dya, "tn", F32, "attn_out_dw")
    do = _mm(dya, w["ao"], "nt", BF16, "attn_out_dx")
    dqr, dkr, dv, gs["attn_sinks"] = _attn_bwd(qr, kr, proj, sinks128, do)
    dq, dk = _rope_bwd(dqr, dkr, pos, invf)
    gw["so"] = _mm(ysn, dys, "tn", F32, "ssd_out_dw")
    dysn = _mm(dys, w["so"], "nt", F32, "ssd_out_dx")
    dz, dy1, gs["ssd_norm_g"], dd_ch = _gated_norm_bwd(dysn, y_ssd, xc, proj, dexp, small["ssd_norm_g"])
    gs["d_skip"] = _fold_heads(dd_ch, "d_skip_fold")
    dxs, db, dc, dadt3, dxdx3 = _ssd_bwd(xc, dt3, adt3, hprev, dy1)
    ddt, gs["dt_bias"], gs["a_log"] = _dt_bwd(_token_major(dadt3), _token_major(dxdx3), proj, bias128, alog128)
    cw, cbias = small["conv_w"], small["conv_b"]
    dux, dwx, dbx = _conv_bwd(proj, dxs, cw, cbias, 0, SSD_INNER, "conv_bwd_x", skip=(dy1, dexp))
    dub, dwb, dbb = _conv_bwd(proj, db, cw, cbias, SSD_INNER, SSD_G * SSD_N, "conv_bwd_b")
    duc, dwc, dbc = _conv_bwd(proj, dc, cw, cbias, SSD_INNER + SSD_G * SSD_N, SSD_G * SSD_N, "conv_bwd_c")
    gs["conv_w"] = jnp.concatenate([dwx[:4], dwb[:4], dwc[:4]], axis=1)
    gs["conv_b"] = jnp.concatenate([dbx, dbb, dbc], axis=1)
    dproj = jnp.concatenate([dz, dq, dgs, dga, dux, dub, duc, dk, dv, ddt], axis=1)
    gw["win"] = _mm(h1b, dproj, "tn", F32, "proj_dw", caps=(1024, 896, 2048))
    if early_grads is not None:
        early_grads[0](gw)
    dh1 = _mm(dproj, w["win"], "nt", F32, "proj_dx", add=dr2, add_scale=ALPHA, caps=(1024, 1024, 2432))
    ln1_g = small["ln1_g"]
    if early_grads is not None:
        ln1_g = ln1_g + early_grads[1](dh1)[0:1, 0:1]
    dr1, dr1h, gs["ln1_g"], gs["ln1_b"] = _ln_bwd(dh1, xh1, rs1, ln1_g, 0.5, "ln1_bwd")
    gw["d1"] = _mm(a1, dr1h, "tn", F32, "ffn1_down_dw")
    da1 = _mm(dr1h, w["d1"], "nt", BF16, "ffn1_down_dx", caps=(1024, 1408, 2048))
    dgu1 = _swiglu_bwd(gu1, da1, "ffn1_act_bwd")
    gw["gu1"] = _mm(xb, dgu1, "tn", F32, "ffn1_gu_dw", caps=(1024, 1408, 2048), n_slabs=N_CHIPS)
    grad_x = _mm(dgu1, w["gu1"], "nt", F32, "ffn1_gu_dx", add=dr1, add_scale=ALPHA, caps=(1024, 1024, 2816))
    return loss, grad_x, gw, gs


MESH = pl.DeviceIdType.MESH
ANY = pl.BlockSpec(memory_space=pl.ANY)


def _place():
    x, y, c = lax.axis_index("x"), lax.axis_index("y"), lax.axis_index("c")
    peers = [(1 - x, y), (x, 1 - y), (1 - x, 1 - y)]
    return x, y, c, peers


BIG = [
    ("ffn1_w_gate", D, SHARD_H, "gu1", "col", 0),
    ("ffn1_w_up", D, SHARD_H, "gu1", "col", SHARD_H),
    ("ffn1_w_down", SHARD_H, D, "d1", "row", 0),
    ("w_in", D, SHARD_IN, "win4", "lead", 0),
    ("w_ssd_o", SSD_INNER // N_CHIPS, D, "so", "row", 0),
    ("w_attn_o", D // N_CHIPS, D, "ao", "row", 0),
    ("w_out", D // N_CHIPS, D, "out", "row", 0),
    ("ffn2_w_gate", D, SHARD_H, "gu2", "col", 0),
    ("ffn2_w_up", D, SHARD_H, "gu2", "col", SHARD_H),
    ("ffn2_w_down", SHARD_H, D, "d2", "row", 0),
]
GATHERED = {"gu1": (D, 2 * FFN_H), "d1": (FFN_H, D), "win4": (N_CHIPS, D, SHARD_IN), "so": (SSD_INNER, D),
            "ao": (D, D), "out": (D, D), "gu2": (D, 2 * FFN_H), "d2": (FFN_H, D)}


def _cast_place(srcs, oname, chip_idx):
    rows, cols = srcs[0].shape
    tr = _divtile(rows, 256, 16)
    kind = [b[4] for b in BIG if b[3] == oname][0]

    def body(chip_ref, *refs):
        o_ref = refs[-1]
        for k, s_ref in enumerate(refs[:-1]):
            o_ref[:, k * cols:(k + 1) * cols] = s_ref[...].astype(BF16)

    nt = rows // tr
    if kind == "col":
        o_spec = pl.BlockSpec((tr, len(srcs) * cols), lambda i, chip_ref: (i, chip_ref[0]))
    elif kind == "row":
        o_spec = pl.BlockSpec((tr, cols), lambda i, chip_ref: (chip_ref[0] * nt + i, 0))
    else:
        o_spec = pl.BlockSpec((None, tr, cols), lambda i, chip_ref: (chip_ref[0], i, 0))
    return pl.pallas_call(
        body, name="cast_place_" + oname,
        grid_spec=pltpu.PrefetchScalarGridSpec(
            num_scalar_prefetch=1, grid=(nt,),
            in_specs=[pl.BlockSpec((tr, cols), lambda i, chip_ref: (i, 0))] * len(srcs), out_specs=o_spec),
        out_shape=jax.ShapeDtypeStruct(GATHERED[oname], BF16),
        compiler_params=_params(("parallel",), 32 << 20),
    )(chip_idx, *srcs)


def _slot(outs, entry, j, half):
    _, rows, cols, oname, kind, off = entry
    o = outs[oname]
    hr = rows // 2
    if kind == "col":
        cs = pl.ds(pl.multiple_of(j * (2 * SHARD_H) + off, 128), cols)
        return o.at[pl.ds(pl.multiple_of(half * hr, 16), hr), cs]
    if kind == "row":
        return o.at[pl.ds(pl.multiple_of(j * rows + half * hr, 16), hr), :]
    return o.at[j, pl.ds(pl.multiple_of(half * hr, 16), hr), :]


HBM = pl.BlockSpec(memory_space=pltpu.HBM)
SEM = pl.BlockSpec(memory_space=pltpu.SEMAPHORE)


def _ici_copy(outs, entry, j, c, to, send, recv, k):
    ref = _slot(outs, entry, j, c)
    return pltpu.make_async_remote_copy(src_ref=ref, dst_ref=ref, send_sem=send.at[k], recv_sem=recv.at[k],
                                        device_id=to, device_id_type=MESH)


def _gather_ici_start(placed, names, after):
    big = [b for b in BIG if b[3] in names]
    n = len(big)
    n_in = len(names) + 1

    def body(*refs):
        outs = dict(zip(names, refs[n_in + 2:n_in + 2 + len(names)]))
        send, recv = refs[n_in], refs[n_in + 1]
        token = refs[-1]
        x, y, c, peers = _place()
        for i, entry in enumerate(big):
            for k, (px, py) in enumerate(peers):
                _ici_copy(outs, entry, 2 * x + y, c, (px, py, c), send, recv, 3 * i + k).start()
        token[...] = jnp.zeros_like(token)

    res = pl.pallas_call(
        body, name="gather_ici_start",
        in_specs=[HBM] * len(names) + [pl.BlockSpec(memory_space=pl.ANY)],
        out_specs=[SEM, SEM] + [HBM] * len(names) + [pl.BlockSpec(memory_space=pltpu.VMEM)],
        out_shape=[pltpu.SemaphoreType.DMA((3 * n,)), pltpu.SemaphoreType.DMA((3 * n,))]
        + [pltpu.HBM(GATHERED[k], BF16) for k in names] + [jax.ShapeDtypeStruct((8, 128), F32)],
        input_output_aliases={i: i + 2 for i in range(len(names))},
        compiler_params=pltpu.CompilerParams(has_side_effects=pltpu.SideEffectType.DATAFLOW_SIDE_EFFECTING),
    )(*[pltpu.with_memory_space_constraint(placed[k], pltpu.HBM) for k in names], after)
    return res[0], res[1], dict(zip(names, res[2:2 + len(names)])), res[-1]


def _gather_ici_wait(send, recv, arrays, names, after):
    big = [b for b in BIG if b[3] in names]

    def body(*refs):
        outs = dict(zip(names, refs[:len(names)]))
        send_ref, recv_ref = refs[len(names)], refs[len(names) + 1]
        x, y, c, peers = _place()
        for i, entry in enumerate(big):
            for k, (px, py) in enumerate(peers):
                mine = _ici_copy(outs, entry, 2 * x + y, c, (px, py, c), send_ref, recv_ref, 3 * i + k)
                mine.wait_send()
                theirs = _ici_copy(outs, entry, 2 * px + py, c, (px, py, c), send_ref, recv_ref, 3 * i + k)
                theirs.wait_recv()

    res = pl.pallas_call(
        body, name="gather_ici_wait",
        in_specs=[HBM] * len(names) + [SEM, SEM, pl.BlockSpec(memory_space=pl.ANY)],
        out_specs=[HBM] * len(names),
        out_shape=[pltpu.HBM(GATHERED[k], BF16) for k in names],
        input_output_aliases={i: i for i in range(len(names))},
        compiler_params=pltpu.CompilerParams(has_side_effects=pltpu.SideEffectType.DATAFLOW_SIDE_EFFECTING),
    )(*[arrays[k] for k in names], send, recv, after)
    return dict(zip(names, res))


def _gather_d2d(arrays, names):
    big = [b for b in BIG if b[3] in names]
    n = len(big)

    def body(*refs):
        outs = dict(zip(names, refs[len(names):2 * len(names)]))
        fsend, frecv = refs[2 * len(names):]
        x, y, c, peers = _place()
        cps = []
        for i, entry in enumerate(big):
            for k, (px, py) in enumerate(peers):
                cp = _ici_copy(outs, entry, 2 * px + py, c, (x, y, 1 - c), fsend, frecv, 3 * i + k)
                cp.start()
                cps.append(cp)
        for i, entry in enumerate(big):
            for k, (px, py) in enumerate(peers):
                _ici_copy(outs, entry, 2 * px + py, 1 - c, (x, y, 1 - c), fsend, frecv, 3 * i + k).wait_recv()
        for cp in cps:
            cp.wait_send()

    res = pl.pallas_call(
        body, name="gather_d2d",
        in_specs=[ANY] * len(names), out_specs=[ANY] * len(names),
        out_shape=[jax.ShapeDtypeStruct(GATHERED[k], BF16) for k in names],
        input_output_aliases={i: i for i in range(len(names))},
        scratch_shapes=[pltpu.SemaphoreType.DMA((3 * n,))] * 2,
    )(*[arrays[k] for k in names])
    return dict(zip(names, res))


def _gather_weights(placed, out_names):
    big = [b for b in BIG if b[3] in out_names]
    n = len(big)

    def body(*refs):
        outs = dict(zip(out_names, refs[len(out_names):2 * len(out_names)]))
        send, recv, fsend, frecv = refs[2 * len(out_names):]
        x, y, c, peers = _place()
        me = 2 * x + y

        def slot(i, j, half):
            return _slot(outs, big[i], j, half)

        sends = []
        for i in range(n):
            for k, (px, py) in enumerate(peers):
                cp = pltpu.make_async_remote_copy(src_ref=slot(i, me, c), dst_ref=slot(i, me, c),
                                                  send_sem=send.at[3 * i + k], recv_sem=recv.at[3 * i + k],
                                                  device_id=(px, py, c), device_id_type=MESH)
                cp.start()
                sends.append(cp)
        fwds = []
        for i in range(n):
            for k, (px, py) in enumerate(peers):
                pj = 2 * px + py
                pltpu.make_async_remote_copy(src_ref=slot(i, pj, c), dst_ref=slot(i, pj, c),
                                             send_sem=send.at[3 * i + k], recv_sem=recv.at[3 * i + k],
                                             device_id=(px, py, c), device_id_type=MESH).wait_recv()
                cp = pltpu.make_async_remote_copy(src_ref=slot(i, pj, c), dst_ref=slot(i, pj, c),
                                                  send_sem=fsend.at[3 * i + k], recv_sem=frecv.at[3 * i + k],
                                                  device_id=(x, y, 1 - c), device_id_type=MESH)
                cp.start()
                fwds.append(cp)
        for i in range(n):
            for k, (px, py) in enumerate(peers):
                pj = 2 * px + py
                pltpu.make_async_remote_copy(src_ref=slot(i, pj, 1 - c), dst_ref=slot(i, pj, 1 - c),
                                             send_sem=fsend.at[3 * i + k], recv_sem=frecv.at[3 * i + k],
                                             device_id=(x, y, 1 - c), device_id_type=MESH).wait_recv()
        for cp in sends + fwds:
            cp.wait_send()

    outs = pl.pallas_call(
        body, name="gather_weights",
        in_specs=[ANY] * len(out_names), out_specs=[ANY] * len(out_names),
        out_shape=[jax.ShapeDtypeStruct(GATHERED[k], BF16) for k in out_names],
        input_output_aliases={i: i for i in range(len(out_names))},
        scratch_shapes=[pltpu.SemaphoreType.DMA((3 * n,))] * 4,
    )(*[placed[k] for k in out_names])
    return dict(zip(out_names, outs))


FIRST_USED = ["gu1", "d1"]
LATER_USED = ["win4", "so", "ao", "out", "gu2", "d2"]


def _win_pieces():
    pieces = []
    for g0, wd, i0 in SEGS.values():
        for j in range(N_CHIPS):
            lo, hi = max(g0, j * SHARD_IN), min(g0 + wd, (j + 1) * SHARD_IN)
            if lo < hi:
                pieces.append((j, lo - j * SHARD_IN, hi - j * SHARD_IN, i0 + lo - g0))
    return pieces


def _win_to_internal(win4):
    tr = 128

    def body(i_ref, o_ref):
        for j, s0, s1, d0 in _win_pieces():
            o_ref[:, d0:d0 + s1 - s0] = i_ref[j, :, s0:s1]
        o_ref[:, PROJ_W:] = jnp.zeros((tr, PROJ_PAD - PROJ_W), o_ref.dtype)

    return pl.pallas_call(
        body, name="win_to_internal", grid=(D // tr,),
        in_specs=[pl.BlockSpec((N_CHIPS, tr, SHARD_IN), lambda i: (0, i, 0))],
        out_specs=pl.BlockSpec((tr, PROJ_PAD), lambda i: (i, 0)),
        out_shape=jax.ShapeDtypeStruct((D, PROJ_PAD), win4.dtype),
        compiler_params=_params(("parallel",), 40 << 20),
    )(win4)


def _win_from_internal(g):
    tr = 64

    def body(i_ref, o_ref):
        for j, s0, s1, d0 in _win_pieces():
            o_ref[j, :, s0:s1] = i_ref[:, d0:d0 + s1 - s0]

    return pl.pallas_call(
        body, name="win_from_internal", grid=(D // tr,),
        in_specs=[pl.BlockSpec((tr, PROJ_PAD), lambda i: (i, 0))],
        out_specs=pl.BlockSpec((N_CHIPS, tr, SHARD_IN), lambda i: (0, i, 0)),
        out_shape=jax.ShapeDtypeStruct((N_CHIPS, D, SHARD_IN), g.dtype),
        compiler_params=_params(("parallel",), 40 << 20),
    )(g)


def _rs_pair_exchange(grads, tag):
    n = len(grads)

    def body(*refs):
        srcs, dsts = refs[:n], refs[n:2 * n]
        send, recv = refs[2 * n:]
        x, y, c, _ = _place()
        cps = []
        for i in range(n):
            hr = srcs[i].shape[1] // 2
            cp = pltpu.make_async_remote_copy(
                src_ref=srcs[i].at[:, pl.ds(pl.multiple_of((1 - c) * hr, 16), hr), :], dst_ref=dsts[i],
                send_sem=send.at[i], recv_sem=recv.at[i], device_id=(x, y, 1 - c), device_id_type=MESH)
            cp.start()
            cps.append(cp)
        for cp in cps:
            cp.wait()

    return pl.pallas_call(
        body, name="rs_pair_exchange_" + tag, in_specs=[ANY] * n, out_specs=[ANY] * n,
        out_shape=[jax.ShapeDtypeStruct((g.shape[0], g.shape[1] // 2, g.shape[2]), F32) for g in grads],
        scratch_shapes=[pltpu.SemaphoreType.DMA((n,))] * 2,
    )(*grads)


def _pair_copy(src, dst, c, to, send, recv, k):
    hr = src.shape[1] // 2
    return pltpu.make_async_remote_copy(
        src_ref=src.at[:, pl.ds(pl.multiple_of((1 - c) * hr, 16), hr), :], dst_ref=dst,
        send_sem=send.at[k], recv_sem=recv.at[k], device_id=to, device_id_type=MESH)


def _rs_pair_start(grads):
    n = len(grads)

    def body(*refs):
        send, recv = refs[2 * n], refs[2 * n + 1]
        srcs, dsts = refs[2 * n + 2:3 * n + 2], refs[3 * n + 2:4 * n + 2]
        x, y, c, _ = _place()
        for i in range(n):
            _pair_copy(srcs[i], dsts[i], c, (x, y, 1 - c), send, recv, i).start()

    lands = [lax.empty((g.shape[0], g.shape[1] // 2, g.shape[2]), F32) for g in grads]
    res = pl.pallas_call(
        body, name="rs_pair_start",
        in_specs=[HBM] * (2 * n), out_specs=[SEM, SEM] + [HBM] * (2 * n),
        out_shape=[pltpu.SemaphoreType.DMA((n,)), pltpu.SemaphoreType.DMA((n,))]
        + [pltpu.HBM(g.shape, F32) for g in grads] + [pltpu.HBM(l.shape, F32) for l in lands],
        input_output_aliases={i: i + 2 for i in range(2 * n)},
        compiler_params=pltpu.CompilerParams(has_side_effects=pltpu.SideEffectType.DATAFLOW_SIDE_EFFECTING),
    )(*[pltpu.with_memory_space_constraint(a, pltpu.HBM) for a in list(grads) + lands])
    return res[0], res[1], list(res[2:2 + n]), list(res[2 + n:2 + 2 * n])


def _rs_pair_wait(send, recv, grads, lands, after):
    n = len(grads)

    def body(*refs):
        srcs, dsts = refs[:n], refs[n:2 * n]
        send_ref, recv_ref = refs[2 * n], refs[2 * n + 1]
        x, y, c, _ = _place()
        for i in range(n):
            cp = _pair_copy(srcs[i], dsts[i], c, (x, y, 1 - c), send_ref, recv_ref, i)
            cp.wait_send()
            cp.wait_recv()

    res = pl.pallas_call(
        body, name="rs_pair_wait",
        in_specs=[HBM] * (2 * n) + [SEM, SEM, pl.BlockSpec(memory_space=pl.ANY)],
        out_specs=[HBM] * (2 * n),
        out_shape=[pltpu.HBM(g.shape, F32) for g in grads] + [pltpu.HBM(l.shape, F32) for l in lands],
        input_output_aliases={i: i for i in range(2 * n)},
        compiler_params=pltpu.CompilerParams(has_side_effects=pltpu.SideEffectType.DATAFLOW_SIDE_EFFECTING),
    )(*grads, *lands, send, recv, after)
    return list(res[:n]), list(res[n:])


def _half_tile(hr):
    return _divtile(hr, 256, 16) if hr % 256 == 0 else _divtile(hr, 512, 16)


def _rs_pair_sum(g, r, c_idx, name):
    ns, rows, cols = g.shape
    hr = rows // 2
    tr = _half_tile(hr)
    nt = hr // tr

    def body(c_ref, g_ref, r_ref, ob_ref, of_ref):
        s = g_ref[...] + r_ref[...]
        ob_ref[...] = s.astype(BF16)
        of_ref[...] = s

    blk = pl.BlockSpec((None, tr, cols), lambda j, t, c_ref: (j, t, 0))
    return pl.pallas_call(
        body, name=name,
        grid_spec=pltpu.PrefetchScalarGridSpec(
            num_scalar_prefetch=1, grid=(ns, nt),
            in_specs=[pl.BlockSpec((None, tr, cols), lambda j, t, c_ref: (j, c_ref[0] * nt + t, 0)), blk],
            out_specs=[blk, blk]),
        out_shape=[jax.ShapeDtypeStruct((ns, hr, cols), BF16), jax.ShapeDtypeStruct((ns, hr, cols), F32)],
        compiler_params=_params(("parallel", "parallel"), 48 << 20),
    )(c_idx, g, r)


def _rs_chip_exchange(parts, tag):
    n = len(parts)

    def body(*refs):
        srcs, dsts = refs[:n], refs[n:2 * n]
        send, recv = refs[2 * n:]
        x, y, c, peers = _place()
        cps = []
        for i in range(n):
            for k, (px, py) in enumerate(peers):
                cp = pltpu.make_async_remote_copy(
                    src_ref=srcs[i].at[2 * px + py], dst_ref=dsts[i].at[k],
                    send_sem=send.at[3 * i + k], recv_sem=recv.at[3 * i + k],
                    device_id=(px, py, c), device_id_type=MESH)
                cp.start()
                cps.append(cp)
        for cp in cps:
            cp.wait()

    return pl.pallas_call(
        body, name="rs_chip_exchange_" + tag, in_specs=[ANY] * n, out_specs=[ANY] * n,
        out_shape=[jax.ShapeDtypeStruct((3,) + p.shape[1:], BF16) for p in parts],
        scratch_shapes=[pltpu.SemaphoreType.DMA((3 * n,))] * 2,
    )(*parts)


def _rs_chip_start(parts):
    n = len(parts)

    def body(*refs):
        send, recv = refs[2 * n], refs[2 * n + 1]
        srcs, dsts = refs[2 * n + 2:3 * n + 2], refs[3 * n + 2:4 * n + 2]
        token = refs[-1]
        x, y, c, peers = _place()
        for i in range(n):
            for k, (px, py) in enumerate(peers):
                pltpu.make_async_remote_copy(
                    src_ref=srcs[i].at[2 * px + py], dst_ref=dsts[i].at[k],
                    send_sem=send.at[3 * i + k], recv_sem=recv.at[3 * i + k],
                    device_id=(px, py, c), device_id_type=MESH).start()
        token[...] = jnp.zeros_like(token)

    lands = [lax.empty((3,) + p.shape[1:], BF16) for p in parts]
    res = pl.pallas_call(
        body, name="rs_chip_start",
        in_specs=[HBM] * (2 * n),
        out_specs=[SEM, SEM] + [HBM] * (2 * n) + [pl.BlockSpec(memory_space=pltpu.VMEM)],
        out_shape=[pltpu.SemaphoreType.DMA((3 * n,)), pltpu.SemaphoreType.DMA((3 * n,))]
        + [pltpu.HBM(p.shape, BF16) for p in parts] + [pltpu.HBM(l.shape, BF16) for l in lands]
        + [jax.ShapeDtypeStruct((8, 128), F32)],
        input_output_aliases={i: i + 2 for i in range(2 * n)},
        compiler_params=pltpu.CompilerParams(has_side_effects=pltpu.SideEffectType.DATAFLOW_SIDE_EFFECTING),
    )(*[pltpu.with_memory_space_constraint(a, pltpu.HBM) for a in list(parts) + lands])
    return res[0], res[1], list(res[2:2 + n]), list(res[2 + n:2 + 2 * n]), res[-1]


def _rs_chip_wait(send, recv, parts, lands, after):
    n = len(parts)

    def body(*refs):
        srcs, dsts = refs[:n], refs[n:2 * n]
        send_ref, recv_ref = refs[2 * n], refs[2 * n + 1]
        x, y, c, peers = _place()
        for i in range(n):
            for k, (px, py) in enumerate(peers):
                cp = pltpu.make_async_remote_copy(
                    src_ref=srcs[i].at[2 * px + py], dst_ref=dsts[i].at[k],
                    send_sem=send_ref.at[3 * i + k], recv_sem=recv_ref.at[3 * i + k],
                    device_id=(px, py, c), device_id_type=MESH)
                cp.wait_send()
                cp.wait_recv()

    res = pl.pallas_call(
        body, name="rs_chip_wait",
        in_specs=[HBM] * (2 * n) + [SEM, SEM, pl.BlockSpec(memory_space=pl.ANY)],
        out_specs=[HBM] * (2 * n),
        out_shape=[pltpu.HBM(p.shape, BF16) for p in parts] + [pltpu.HBM(l.shape, BF16) for l in lands],
        input_output_aliases={i: i for i in range(2 * n)},
        compiler_params=pltpu.CompilerParams(has_side_effects=pltpu.SideEffectType.DATAFLOW_SIDE_EFFECTING),
    )(*parts, *lands, send, recv, after)
    return list(res[n:])


def _rs_final_sum(own, got, chip_idx, c_idx, name):
    ns, hr, cols = own.shape
    tr = _half_tile(hr)
    nt = hr // tr

    def body(chip_ref, c_ref, o_ref, g_ref, out_ref):
        s = o_ref[...]
        for k in range(3):
            s = s + g_ref[k].astype(F32)
        out_ref[...] = s

    return pl.pallas_call(
        body, name=name,
        grid_spec=pltpu.PrefetchScalarGridSpec(
            num_scalar_prefetch=2, grid=(nt,),
            in_specs=[pl.BlockSpec((None, tr, cols), lambda t, chip_ref, c_ref: (chip_ref[0], t, 0)),
                      pl.BlockSpec((3, tr, cols), lambda t, chip_ref, c_ref: (0, t, 0))],
            out_specs=pl.BlockSpec((tr, cols), lambda t, chip_ref, c_ref: (c_ref[0] * nt + t, 0))),
        out_shape=jax.ShapeDtypeStruct((2 * hr, cols), F32),
        compiler_params=_params(("parallel",), 48 << 20),
    )(chip_idx, c_idx, own, got)


def _rs_share_halves(fulls):
    n = len(fulls)

    def body(*refs):
        dsts = refs[n:2 * n]
        send, recv = refs[2 * n:]
        x, y, c, _ = _place()
        cps = []
        for i in range(n):
            hr = dsts[i].shape[0] // 2
            rows = dsts[i].at[pl.ds(pl.multiple_of(c * hr, 8), hr), :]
            cp = pltpu.make_async_remote_copy(src_ref=rows, dst_ref=rows, send_sem=send.at[i], recv_sem=recv.at[i],
                                              device_id=(x, y, 1 - c), device_id_type=MESH)
            cp.start()
            cps.append(cp)
        for i in range(n):
            hr = dsts[i].shape[0] // 2
            other = dsts[i].at[pl.ds(pl.multiple_of((1 - c) * hr, 8), hr), :]
            pltpu.make_async_remote_copy(src_ref=other, dst_ref=other, send_sem=send.at[i], recv_sem=recv.at[i],
                                         device_id=(x, y, 1 - c), device_id_type=MESH).wait_recv()
        for cp in cps:
            cp.wait_send()

    return pl.pallas_call(
        body, name="rs_share_halves", in_specs=[ANY] * n, out_specs=[ANY] * n,
        out_shape=[jax.ShapeDtypeStruct(f.shape, F32) for f in fulls],
        input_output_aliases={i: i for i in range(n)},
        scratch_shapes=[pltpu.SemaphoreType.DMA((n,))] * 2,
    )(*fulls)


def _all_reduce_small(v):
    rows = v.shape[0]

    def body(v_ref, o_ref, buf, send, recv):
        x, y, c, _ = _place()
        me = 4 * x + 2 * y + c
        buf[me] = v_ref[...]
        cps = []
        for d in range(1, 8):
            px, py, pc = x ^ (d >> 2), y ^ ((d >> 1) & 1), c ^ (d & 1)
            cp = pltpu.make_async_remote_copy(src_ref=v_ref, dst_ref=buf.at[me], send_sem=send.at[d - 1],
                                              recv_sem=recv.at[d - 1], device_id=(px, py, pc), device_id_type=MESH)
            cp.start()
            cps.append(cp)
        for d in range(1, 8):
            px, py, pc = x ^ (d >> 2), y ^ ((d >> 1) & 1), c ^ (d & 1)
            pltpu.make_async_remote_copy(src_ref=v_ref, dst_ref=buf.at[4 * px + 2 * py + pc], send_sem=send.at[d - 1],
                                         recv_sem=recv.at[d - 1], device_id=(px, py, pc),
                                         device_id_type=MESH).wait_recv()
        for cp in cps:
            cp.wait_send()
        acc = buf[0]
        for d in range(1, 8):
            acc = acc + buf[d]
        o_ref[...] = acc

    vm = pl.BlockSpec(memory_space=pltpu.VMEM)
    return pl.pallas_call(
        body, name="all_reduce_small", in_specs=[vm], out_specs=vm,
        out_shape=jax.ShapeDtypeStruct((rows, 128), F32),
        scratch_shapes=[pltpu.VMEM((8, rows, 128), F32), pltpu.SemaphoreType.DMA((7,)), pltpu.SemaphoreType.DMA((7,))],
    )(v)


def _adamw(w, g, m, v, name, g_col_blk=0):
    rows, cols = w.shape
    tr = _divtile(rows, max(8, (2 << 20) // (4 * cols) // 8 * 8), 8)

    def body(w_ref, g_ref, m_ref, v_ref, go_ref, d_ref, mo_ref, vo_ref):
        gv = g_ref[...]
        mn = ADAM_B1 * m_ref[...] + (1.0 - ADAM_B1) * gv
        vn = ADAM_B2 * v_ref[...] + (1.0 - ADAM_B2) * (gv * gv)
        m_hat = mn / (1.0 - ADAM_B1 ** ADAM_STEP)
        v_hat = vn / (1.0 - ADAM_B2 ** ADAM_STEP)
        go_ref[...] = gv
        d_ref[...] = -ADAM_LR * (m_hat / (jnp.sqrt(v_hat) + ADAM_EPS) + ADAM_WD * w_ref[...])
        mo_ref[...] = mn
        vo_ref[...] = vn

    blk = pl.BlockSpec((tr, cols), lambda i: (i, 0))
    return pl.pallas_call(
        body, name=name, grid=(rows // tr,),
        in_specs=[blk, pl.BlockSpec((tr, cols), lambda i: (i, g_col_blk)), blk, blk],
        out_specs=[blk] * 4, out_shape=[jax.ShapeDtypeStruct((rows, cols), F32)] * 4,
        compiler_params=_params(("parallel",), 48 << 20),
    )(w, g, m, v)


SMALL = ["ln1_g", "ln1_b", "conv_w", "conv_b", "dt_bias", "a_log", "d_skip", "ssd_norm_g", "attn_sinks",
         "ln2_g", "ln2_b", "ln3_g", "ln3_b"]


def _pack_rows(vs):
    parts = []
    for v in vs:
        v = v.reshape(-1)
        parts.append(jnp.pad(v, (0, (-v.shape[0]) % 128)))
    flat = jnp.concatenate(parts)
    flat = jnp.pad(flat, (0, (-flat.shape[0]) % 1024))
    return flat.reshape(-1, 128)


def _unpack_rows(packed, shapes):
    flat = packed.reshape(-1)
    out, at = [], 0
    for s in shapes:
        nel = int(np.prod(s))
        out.append(flat[at:at + nel].reshape(s))
        at += nel + (-nel) % 128
    return out


def kernel(x, positions, ffn1_w_gate, ffn1_w_up, ffn1_w_down, ln1_g, ln1_b, w_in, conv_w, conv_b, dt_bias, a_log, d_skip, ssd_norm_g, w_ssd_o, attn_sinks, w_attn_o, w_out, ln2_g, ln2_b, ffn2_w_gate, ffn2_w_up, ffn2_w_down, ln3_g, ln3_b, loss_target, m_ffn1_w_gate, m_ffn1_w_up, m_ffn1_w_down, m_ln1_g, m_ln1_b, m_w_in, m_conv_w, m_conv_b, m_dt_bias, m_a_log, m_d_skip, m_ssd_norm_g, m_w_ssd_o, m_attn_sinks, m_w_attn_o, m_w_out, m_ln2_g, m_ln2_b, m_ffn2_w_gate, m_ffn2_w_up, m_ffn2_w_down, m_ln3_g, m_ln3_b, v_ffn1_w_gate, v_ffn1_w_up, v_ffn1_w_down, v_ln1_g, v_ln1_b, v_w_in, v_conv_w, v_conv_b, v_dt_bias, v_a_log, v_d_skip, v_ssd_norm_g, v_w_ssd_o, v_attn_sinks, v_w_attn_o, v_w_out, v_ln2_g, v_ln2_b, v_ffn2_w_gate, v_ffn2_w_up, v_ffn2_w_down, v_ln3_g, v_ln3_b):
    args = dict(locals())
    wts = {n: args[n][0] for n in [b[0] for b in BIG] + SMALL}
    mom_m = {n: args["m_" + n][0] for n in wts}
    mom_v = {n: args["v_" + n][0] for n in wts}
    t = x.shape[1]
    xi, yi, ci = lax.axis_index("x"), lax.axis_index("y"), lax.axis_index("c")
    chip = 2 * xi + yi

    c_idx = ci.astype(jnp.int32).reshape(1)
    chip_idx = chip.astype(jnp.int32).reshape(1)
    placed = {o: _cast_place([wts[b[0]] for b in BIG if b[3] == o], o, chip_idx) for o in GATHERED}
    w = _gather_weights(placed, FIRST_USED)
    g_send, g_recv, g_flight, g_token = _gather_ici_start(placed, LATER_USED, w["d1"])

    def later_weights(h1b):
        arrived = _gather_d2d(_gather_ici_wait(g_send, g_recv, g_flight, LATER_USED, h1b), LATER_USED)
        rest = {k: arrived[k] for k in ("so", "ao", "out", "gu2", "d2")}
        rest["win"] = _win_to_internal(arrived["win4"])
        return rest

    def slabs_of(gw, names):
        view = {"gu1": lambda: gw["gu1"], "gu2": lambda: gw["gu2"],
                "d1": lambda: gw["d1"].reshape(N_CHIPS, SHARD_H, D), "d2": lambda: gw["d2"].reshape(N_CHIPS, SHARD_H, D),
                "win": lambda: _win_from_internal(gw["win"]),
                "so": lambda: gw["so"].reshape(N_CHIPS, SSD_INNER // N_CHIPS, D),
                "ao": lambda: gw["ao"].reshape(N_CHIPS, D // N_CHIPS, D),
                "out": lambda: gw["out"].reshape(N_CHIPS, D // N_CHIPS, D)}
        return [view[nm]() for nm in names]

    early = ["win", "so", "ao", "out", "gu2", "d2"]
    late = ["gu1", "d1"]
    flight = {}

    def early_start(gw):
        flight["pair"] = _rs_pair_start(slabs_of(gw, early))

    def early_mid(dh1):
        slabs, from_sib = _rs_pair_wait(*flight["pair"], dh1)
        pair = [_rs_pair_sum(g, r, c_idx, "rs_pair_sum_" + nm) for g, r, nm in zip(slabs, from_sib, early)]
        send, recv, parts, lands, token = _rs_chip_start([p[0] for p in pair])
        flight.update(send=send, recv=recv, parts=parts, lands=lands, own=[p[1] for p in pair])
        return token

    early_grads = (early_start, early_mid)
    cw_rows = _pack_rows([lax.dynamic_update_slice(jnp.zeros((4, XBC), F32), wts["conv_w"], (0, chip * (XBC // N_CHIPS)))])
    cw_rows = jnp.where(ci == 0, cw_rows, 0.0)
    conv_w_full = _all_reduce_small(cw_rows)[:4 * XBC // 128].reshape(4, XBC)

    small = {n: (wts[n][None, :] if wts[n].ndim == 1 else wts[n]) for n in SMALL}
    small["conv_w"] = conv_w_full
    loss, grad_x, gw, gs = _local_step(x[0], positions[0].astype(F32)[:, None], loss_target[0], w, small,
                                       start_token=g_token, later_weights=later_weights, early_grads=early_grads)

    got_early = _rs_chip_wait(flight["send"], flight["recv"], flight["parts"], flight["lands"], grad_x)
    slabs = slabs_of(gw, late)
    from_sib = _rs_pair_exchange(slabs, "late")
    pair = [_rs_pair_sum(g, r, c_idx, "rs_pair_sum_" + nm) for g, r, nm in zip(slabs, from_sib, late)]
    got_late = _rs_chip_exchange([p[0] for p in pair], "late")
    names = early + late
    own = flight["own"] + [p[1] for p in pair]
    halves = [_rs_final_sum(o, gt, chip_idx, c_idx, "rs_final_sum_" + nm)
              for o, gt, nm in zip(own, list(got_early) + list(got_late), names)]
    full = dict(zip(names, _rs_share_halves(halves)))

    outs = {}
    big_src = {"ffn1_w_gate": ("gu1", 0), "ffn1_w_up": ("gu1", 1), "ffn1_w_down": ("d1", 0), "w_in": ("win", 0),
               "w_ssd_o": ("so", 0), "w_attn_o": ("ao", 0), "w_out": ("out", 0),
               "ffn2_w_gate": ("gu2", 0), "ffn2_w_up": ("gu2", 1), "ffn2_w_down": ("d2", 0)}
    for nm, (src, blk) in big_src.items():
        outs[nm] = _adamw(wts[nm], full[src], mom_m[nm], mom_v[nm], "adamw_" + nm, g_col_blk=blk)

    gvec = {n: gs[n] for n in SMALL}
    gvec["dt_bias"], gvec["a_log"], gvec["d_skip"] = gs["dt_bias"][:, :64], gs["a_log"][:, :64], gs["d_skip"][:, :64]
    gvec["attn_sinks"] = gs["attn_sinks"][:, :NQ]
    red = _all_reduce_small(_pack_rows([gvec[n] for n in SMALL] + [loss]))
    shapes = [(4, XBC) if n == "conv_w" else wts[n].shape for n in SMALL] + [(1,)]
    red_list = _unpack_rows(red, shapes)
    loss_out = red_list[-1].reshape(())
    gsm = dict(zip(SMALL, red_list[:-1]))
    gsm["conv_w"] = lax.dynamic_slice_in_dim(gsm["conv_w"], chip * (XBC // N_CHIPS), XBC // N_CHIPS, axis=1)
    sm_shapes = [wts[n].shape for n in SMALL]
    res = _adamw(_pack_rows([wts[n] for n in SMALL]), _pack_rows([gsm[n] for n in SMALL]),
                 _pack_rows([mom_m[n] for n in SMALL]), _pack_rows([mom_v[n] for n in SMALL]), "adamw_small")
    res = [_unpack_rows(r, sm_shapes) for r in res]
    for i, nm in enumerate(SMALL):
        outs[nm] = tuple(r[i] for r in res)

    order = ["ffn1_w_gate", "ffn1_w_up", "ffn1_w_down", "ln1_g", "ln1_b", "w_in", "conv_w", "conv_b", "dt_bias", "a_log",
             "d_skip", "ssd_norm_g", "w_ssd_o", "attn_sinks", "w_attn_o", "w_out", "ln2_g", "ln2_b",
             "ffn2_w_gate", "ffn2_w_up", "ffn2_w_down", "ln3_g", "ln3_b"]
    result = [loss_out, grad_x[None]]
    for kind in range(4):
        result += [outs[nm][kind][None] for nm in order]
    return tuple(result)
```

```python
import functools
import math

import numpy as np
import jax
import jax.numpy as jnp
from jax import lax
from jax.experimental import pallas as pl
from jax.experimental.pallas import tpu as pltpu

F32 = jnp.float32
BF16 = jnp.bfloat16
HI = lax.Precision.HIGHEST

D = 2048
FFN_H = 5632
SSD_INNER = 4096
SSD_HEADS = 64
SSD_P = 64
SSD_G = 8
SSD_R = 8
SSD_N = 128
CHUNK = 128
XBC = 6144
NQ = 32
NKV = 4
HD = 64
QW = 2048
KVW = 256
WINDOW = 128
ROPE_THETA = 10000.0
ALPHA = 2.0 ** 0.25
LN_EPS = 1e-5
RMS_EPS = 1e-5
PROJ_W = 16960
N_CHIPS = 4
SHARD_IN = PROJ_W // N_CHIPS
SHARD_H = FFN_H // N_CHIPS

SEGS = {
    "z": (0, 4096, 0),
    "xbc": (4096, 6144, 10240),
    "dt": (10240, 64, 16896),
    "q": (10304, 2048, 4096),
    "k": (12352, 256, 16384),
    "v": (12608, 256, 16640),
    "gs": (12864, 2048, 6144),
    "ga": (14912, 2048, 8192),
}
PROJ_PAD = 17024

ADAM_LR = 0.001
ADAM_B1 = 0.9
ADAM_B2 = 0.999
ADAM_EPS = 1e-08
ADAM_WD = 0.01
ADAM_STEP = 10

VMEM_CAP = 60 * 1024 * 1024


def _params(sem, vmem_bytes):
    return pltpu.CompilerParams(dimension_semantics=sem, vmem_limit_bytes=int(min(VMEM_CAP, vmem_bytes)))


def _divtile(n, cap, q=128):
    best = None
    for d in range(q, min(n, cap) + 1, q):
        if n % d == 0:
            best = d
    return n if best is None else best


def _sigmoid(x):
    return 1.0 / (1.0 + jnp.exp(-x))


def _mm(a, b, mode, out_dtype, name, add=None, add_scale=1.0, caps=(1024, 1024, 2048), n_slabs=1):
    if mode == "nn":
        (m, k), (k2, n) = a.shape, b.shape
    elif mode == "nt":
        (m, k), (n, k2) = a.shape, b.shape
    else:
        (k, m), (k2, n) = a.shape, b.shape
    assert k == k2, (a.shape, b.shape, mode)
    tm, tn, tk = _divtile(m, caps[0]), _divtile(n // n_slabs, caps[1]), _divtile(k, caps[2])
    nk = k // tk
    per_slab = n // n_slabs // tn
    dims = {"nn": ((1,), (0,)), "nt": ((1,), (1,)), "tn": ((0,), (0,))}[mode]
    has_add = add is not None

    def body(*refs):
        if has_add:
            a_ref, b_ref, add_ref, o_ref = refs[:4]
            scr = refs[4:]
        else:
            a_ref, b_ref, o_ref = refs[:3]
            add_ref = None
            scr = refs[3:]
        part = lax.dot_general(a_ref[...].astype(BF16), b_ref[...].astype(BF16), (dims, ((), ())),
                               preferred_element_type=F32)

        def finish(acc):
            if has_add:
                acc = acc + add_scale * add_ref[...].astype(F32)
            o_ref[...] = acc.astype(o_ref.dtype)

        if nk == 1:
            finish(part)
        else:
            acc_ref = scr[0]
            kk = pl.program_id(2)

            @pl.when(kk == 0)
            def _():
                acc_ref[...] = part

            @pl.when(kk > 0)
            def _():
                acc_ref[...] += part

            @pl.when(kk == nk - 1)
            def _():
                finish(acc_ref[...])

    if mode == "nn":
        a_spec = pl.BlockSpec((tm, tk), lambda i, j, kk: (i, kk))
        b_spec = pl.BlockSpec((tk, tn), lambda i, j, kk: (kk, j))
    elif mode == "nt":
        a_spec = pl.BlockSpec((tm, tk), lambda i, j, kk: (i, kk))
        b_spec = pl.BlockSpec((tn, tk), lambda i, j, kk: (j, kk))
    else:
        a_spec = pl.BlockSpec((tk, tm), lambda i, j, kk: (kk, i))
        b_spec = pl.BlockSpec((tk, tn), lambda i, j, kk: (kk, j))
    o_spec = pl.BlockSpec((tm, tn), lambda i, j, kk: (i, j))
    out_shape = jax.ShapeDtypeStruct((m, n), out_dtype)
    if n_slabs > 1:
        assert not has_add
        o_spec = pl.BlockSpec((None, tm, tn), lambda i, j, kk: (j // per_slab, i, j % per_slab))
        out_shape = jax.ShapeDtypeStruct((n_slabs, m, n // n_slabs), out_dtype)
    in_specs = [a_spec, b_spec] + ([o_spec] if has_add else [])
    args = (a, b) + ((add,) if has_add else ())
    osz = jnp.dtype(out_dtype).itemsize
    vmem = (2 * (tm * tk * a.dtype.itemsize + tk * tn * b.dtype.itemsize) + 2 * tm * tn * osz
            + (2 * tm * tn * add.dtype.itemsize if has_add else 0) + 2 * tm * tn * 4
            + 2 * (tm * tk + tk * tn) + (8 << 20))
    return pl.pallas_call(
        body, name=name, grid=(m // tm, n // tn, nk),
        in_specs=in_specs, out_specs=o_spec, out_shape=out_shape,
        scratch_shapes=[pltpu.VMEM((tm, tn), F32)] if nk > 1 else [],
        compiler_params=_params(("parallel", "parallel", "arbitrary"), vmem),
    )(*args)


def _mm_swiglu(a, b, name):
    m, k = a.shape
    w = SHARD_H
    tm = _divtile(m, 512)

    def body(a_ref, b_ref, gu_ref, act_ref):
        gu = jnp.dot(a_ref[...], b_ref[...], preferred_element_type=F32)
        g = gu[:, :w]
        gu_ref[...] = gu.astype(BF16)
        act_ref[...] = (g * _sigmoid(g) * gu[:, w:]).astype(BF16)

    return pl.pallas_call(
        body, name=name, grid=(N_CHIPS, m // tm),
        in_specs=[pl.BlockSpec((tm, k), lambda j, i: (i, 0)), pl.BlockSpec((k, 2 * w), lambda j, i: (0, j))],
        out_specs=[pl.BlockSpec((tm, 2 * w), lambda j, i: (i, j)), pl.BlockSpec((tm, w), lambda j, i: (i, j))],
        out_shape=[jax.ShapeDtypeStruct((m, 2 * FFN_H), BF16), jax.ShapeDtypeStruct((m, FFN_H), BF16)],
        compiler_params=_params(("parallel", "parallel"), 56 << 20),
    )(a, b)


def _swiglu_bwd(gu, da, name):
    t = gu.shape[0]
    tt = _divtile(t, 512)
    w = SHARD_H

    def body(gu_ref, da_ref, o_ref):
        g = gu_ref[:, :w].astype(F32)
        u = gu_ref[:, w:].astype(F32)
        d = da_ref[...].astype(F32)
        s = _sigmoid(g)
        o_ref[:, :w] = (d * u * (s * (1.0 + g * (1.0 - s)))).astype(BF16)
        o_ref[:, w:] = (d * (g * s)).astype(BF16)

    return pl.pallas_call(
        body, name=name, grid=(t // tt, N_CHIPS),
        in_specs=[pl.BlockSpec((tt, 2 * w), lambda i, j: (i, j)), pl.BlockSpec((tt, w), lambda i, j: (i, j))],
        out_specs=pl.BlockSpec((tt, 2 * w), lambda i, j: (i, j)),
        out_shape=jax.ShapeDtypeStruct((t, 2 * FFN_H), BF16),
        compiler_params=_params(("parallel", "parallel"), 40 << 20),
    )(gu, da)


def _ln_fwd(base, f, g, b, c, name, target=None):
    t = base.shape[0]
    tt = _divtile(t, 256)
    with_loss = target is not None

    def body(*refs):
        if with_loss:
            base_ref, f_ref, g_ref, b_ref, tg_ref, h_ref, hb_ref, xh_ref, rs_ref, dh_ref, loss_ref = refs
        else:
            base_ref, f_ref, g_ref, b_ref, h_ref, hb_ref, xh_ref, rs_ref = refs
        r = ALPHA * base_ref[...] + c * f_ref[...]
        mu = jnp.mean(r, axis=-1, keepdims=True)
        xc = r - mu
        var = jnp.mean(xc * xc, axis=-1, keepdims=True)
        rstd = lax.rsqrt(var + LN_EPS)
        xh = xc * rstd
        h = xh * g_ref[...] + b_ref[...]
        h_ref[...] = h
        hb_ref[...] = h.astype(BF16)
        xh_ref[...] = xh
        rs_ref[...] = rstd
        if with_loss:
            e = h - tg_ref[...]
            dh_ref[...] = e * (1.0 / D)
            part = 0.5 * jnp.sum(jnp.sum(e * e, axis=-1, keepdims=True) * (1.0 / D), axis=0, keepdims=True)

            @pl.when(pl.program_id(0) == 0)
            def _():
                loss_ref[...] = jnp.zeros_like(loss_ref)

            loss_ref[...] += part

    row = pl.BlockSpec((tt, D), lambda i: (i, 0))
    vec = pl.BlockSpec((1, D), lambda i: (0, 0))
    col = pl.BlockSpec((tt, 1), lambda i: (i, 0))
    in_specs = [row, row, vec, vec] + ([row] if with_loss else [])
    out_specs = [row, row, row, col] + ([row, pl.BlockSpec((1, 1), lambda i: (0, 0))] if with_loss else [])
    out_shape = [jax.ShapeDtypeStruct((t, D), F32), jax.ShapeDtypeStruct((t, D), BF16),
                 jax.ShapeDtypeStruct((t, D), F32), jax.ShapeDtypeStruct((t, 1), F32)]
    if with_loss:
        out_shape += [jax.ShapeDtypeStruct((t, D), F32), jax.ShapeDtypeStruct((1, 1), F32)]
    args = (base, f, g, b) + ((target,) if with_loss else ())
    return pl.pallas_call(
        body, name=name, grid=(t // tt,), in_specs=in_specs, out_specs=out_specs, out_shape=out_shape,
        compiler_params=_params(("arbitrary",) if with_loss else ("parallel",), 48 << 20),
    )(*args)


def _ln_bwd(dy, xh, rstd, g, c, name):
    t = dy.shape[0]
    tt = _divtile(t, 256)

    def body(dy_ref, xh_ref, rs_ref, g_ref, dr_ref, drb_ref, dg_ref, db_ref):
        dyv = dy_ref[...]
        xhv = xh_ref[...]
        dxh = dyv * g_ref[...]
        m1 = jnp.mean(dxh, axis=-1, keepdims=True)
        m2 = jnp.mean(dxh * xhv, axis=-1, keepdims=True)
        dr = rs_ref[...] * (dxh - m1 - xhv * m2)
        dr_ref[...] = dr
        drb_ref[...] = (c * dr).astype(BF16)

        @pl.when(pl.program_id(0) == 0)
        def _():
            dg_ref[...] = jnp.zeros_like(dg_ref)
            db_ref[...] = jnp.zeros_like(db_ref)

        dg_ref[...] += jnp.sum(dyv * xhv, axis=0, keepdims=True)
        db_ref[...] += jnp.sum(dyv, axis=0, keepdims=True)

    row = pl.BlockSpec((tt, D), lambda i: (i, 0))
    vec = pl.BlockSpec((1, D), lambda i: (0, 0))
    col = pl.BlockSpec((tt, 1), lambda i: (i, 0))
    return pl.pallas_call(
        body, name=name, grid=(t // tt,), in_specs=[row, row, col, vec], out_specs=[row, row, vec, vec],
        out_shape=[jax.ShapeDtypeStruct((t, D), F32), jax.ShapeDtypeStruct((t, D), BF16),
                   jax.ShapeDtypeStruct((1, D), F32), jax.ShapeDtypeStruct((1, D), F32)],
        compiler_params=_params(("arbitrary",), 40 << 20),
    )(dy, xh, rstd, g)


DT_BLK = SEGS["dt"][2] // 128


def _dt_prep(proj, bias128, alog128):
    t = proj.shape[0]
    tt = _divtile(t, 1024)

    def body(p_ref, bias_ref, alog_ref, dt_ref, adt_ref):
        dtv = jax.nn.softplus(p_ref[...] + bias_ref[...])
        dt_ref[...] = dtv
        adt_ref[...] = dtv * (-jnp.exp(alog_ref[...]))

    blk = pl.BlockSpec((tt, 128), lambda i: (i, 0))
    vec = pl.BlockSpec((1, 128), lambda i: (0, 0))
    return pl.pallas_call(
        body, name="dt_prep", grid=(t // tt,),
        in_specs=[pl.BlockSpec((tt, 128), lambda i: (i, DT_BLK)), vec, vec], out_specs=[blk, blk],
        out_shape=[jax.ShapeDtypeStruct((t, 128), F32)] * 2,
        compiler_params=_params(("parallel",), 16 << 20),
    )(proj, bias128, alog128)


def _dt_bwd(dadt, dxdx, proj, bias128, alog128):
    t = proj.shape[0]
    tt = _divtile(t, 1024)

    def body(dadt_ref, dxdx_ref, p_ref, bias_ref, alog_ref, o_ref, dbias_ref, dalog_ref):
        pre = p_ref[...] + bias_ref[...]
        dtv = jax.nn.softplus(pre)
        a = -jnp.exp(alog_ref[...])
        ddt = a * dadt_ref[...] + dxdx_ref[...]
        draw = ddt * _sigmoid(pre)
        o_ref[...] = draw.astype(BF16)

        @pl.when(pl.program_id(0) == 0)
        def _():
            dbias_ref[...] = jnp.zeros_like(dbias_ref)
            dalog_ref[...] = jnp.zeros_like(dalog_ref)

        dbias_ref[...] += jnp.sum(draw, axis=0, keepdims=True)
        dalog_ref[...] += jnp.sum(dadt_ref[...] * dtv * a, axis=0, keepdims=True)

    blk = pl.BlockSpec((tt, 128), lambda i: (i, 0))
    vec = pl.BlockSpec((1, 128), lambda i: (0, 0))
    return pl.pallas_call(
        body, name="dt_bwd", grid=(t // tt,),
        in_specs=[blk, blk, pl.BlockSpec((tt, 128), lambda i: (i, DT_BLK)), vec, vec],
        out_specs=[blk, vec, vec],
        out_shape=[jax.ShapeDtypeStruct((t, 128), BF16), jax.ShapeDtypeStruct((1, 128), F32),
                   jax.ShapeDtypeStruct((1, 128), F32)],
        compiler_params=_params(("arbitrary",), 16 << 20),
    )(dadt, dxdx, proj, bias128, alog128)


CONV_CB = 512
CONV_TT = 512


def _shift_down(cur, prev8, s):
    if s == 0:
        return cur
    rolled = pltpu.roll(cur, s, 0)
    head = pltpu.roll(prev8, s, 0)
    r8 = lax.broadcasted_iota(jnp.int32, (8, 1), 0)
    top = jnp.where(r8 < s, head, rolled[:8])
    return jnp.concatenate([top, rolled[8:]], axis=0)


def _shift_up(cur, next8, s):
    if s == 0:
        return cur
    n = cur.shape[0]
    rolled = pltpu.roll(cur, n - s, 0)
    tail = pltpu.roll(next8, 8 - s, 0)
    r8 = lax.broadcasted_iota(jnp.int32, (8, 1), 0)
    bot = jnp.where(r8 >= 8 - s, tail, rolled[n - 8:])
    return jnp.concatenate([rolled[:n - 8], bot], axis=0)


def _conv_fwd(proj, conv_w, conv_b):
    t = proj.shape[0]
    tt = _divtile(t, CONV_TT)
    base = SEGS["xbc"][2] // CONV_CB
    r8 = tt // 8

    def body(u_ref, up_ref, w_ref, b_ref, o_ref):
        cur = u_ref[...]
        prev8 = jnp.where(pl.program_id(1) > 0, up_ref[...], 0.0)
        acc = b_ref[...] + w_ref[3:4, :] * cur
        for k in range(3):
            acc = acc + w_ref[k:k + 1, :] * _shift_down(cur, prev8, 3 - k)
        o_ref[...] = acc * _sigmoid(acc)

    return pl.pallas_call(
        body, name="conv_fwd", grid=(XBC // CONV_CB, t // tt),
        in_specs=[pl.BlockSpec((tt, CONV_CB), lambda c, i: (i, base + c)),
                  pl.BlockSpec((8, CONV_CB), lambda c, i: (jnp.maximum(i * r8 - 1, 0), base + c)),
                  pl.BlockSpec((4, CONV_CB), lambda c, i: (0, c)),
                  pl.BlockSpec((1, CONV_CB), lambda c, i: (0, c))],
        out_specs=pl.BlockSpec((tt, CONV_CB), lambda c, i: (i, c)),
        out_shape=jax.ShapeDtypeStruct((t, XBC), F32),
        compiler_params=_params(("parallel", "parallel"), 24 << 20),
    )(proj, proj, conv_w, conv_b)


def _conv_bwd(proj, dout, conv_w, conv_b, col0, width, name, skip=None):
    t = proj.shape[0]
    tt = _divtile(t, CONV_TT)
    nt = t // tt
    base = SEGS["xbc"][2] // CONV_CB + col0 // CONV_CB
    wb = col0 // CONV_CB
    r8 = tt // 8
    has_skip = skip is not None

    def body(*refs):
        if has_skip:
            u_ref, up_ref, d_ref, w_ref, b_ref, sk_ref, skw_ref, du_ref, dw_ref, db_ref, nx_ref = refs
        else:
            u_ref, up_ref, d_ref, w_ref, b_ref, du_ref, dw_ref, db_ref, nx_ref = refs
        i = pl.program_id(1)
        cur = u_ref[...]
        prev8 = jnp.where(i < nt - 1, up_ref[...], 0.0)
        sh = [_shift_down(cur, prev8, 3 - k) for k in range(3)] + [cur]
        pre = b_ref[...]
        for k in range(4):
            pre = pre + w_ref[k:k + 1, :] * sh[k]
        sg = _sigmoid(pre)
        dout_v = d_ref[...]
        if has_skip:
            dout_v = dout_v + sk_ref[...] * skw_ref[...]
        dpre = dout_v * (sg * (1.0 + pre * (1.0 - sg)))

        @pl.when(i == 0)
        def _():
            nx_ref[...] = jnp.zeros_like(nx_ref)
            dw_ref[...] = jnp.zeros_like(dw_ref)
            db_ref[...] = jnp.zeros_like(db_ref)

        next8 = nx_ref[...]
        du = w_ref[3:4, :] * dpre
        for s in range(1, 4):
            du = du + w_ref[3 - s:4 - s, :] * _shift_up(dpre, next8, s)
        du_ref[...] = du.astype(BF16)
        nx_ref[...] = dpre[:8]
        rows = [jnp.sum(dpre * sh[k], axis=0, keepdims=True) for k in range(4)]
        dw_ref[...] += jnp.concatenate(rows + [jnp.zeros((4, CONV_CB), F32)], axis=0)
        db_ref[...] += jnp.sum(dpre, axis=0, keepdims=True)

    rev = lambda c, i: (nt - 1 - i, c)
    in_specs = [pl.BlockSpec((tt, CONV_CB), lambda c, i: (nt - 1 - i, base + c)),
                pl.BlockSpec((8, CONV_CB), lambda c, i: (jnp.maximum((nt - 1 - i) * r8 - 1, 0), base + c)),
                pl.BlockSpec((tt, CONV_CB), rev),
                pl.BlockSpec((4, CONV_CB), lambda c, i: (0, wb + c)),
                pl.BlockSpec((1, CONV_CB), lambda c, i: (0, wb + c))]
    args = [proj, proj, dout, conv_w, conv_b]
    if has_skip:
        in_specs += [pl.BlockSpec((tt, CONV_CB), rev), pl.BlockSpec((1, CONV_CB), lambda c, i: (0, c))]
        args += [skip[0], skip[1]]
    return pl.pallas_call(
        body, name=name, grid=(width // CONV_CB, nt),
        in_specs=in_specs,
        out_specs=[pl.BlockSpec((tt, CONV_CB), rev), pl.BlockSpec((8, CONV_CB), lambda c, i: (0, c)),
                   pl.BlockSpec((1, CONV_CB), lambda c, i: (0, c))],
        out_shape=[jax.ShapeDtypeStruct((t, width), BF16), jax.ShapeDtypeStruct((8, width), F32),
                   jax.ShapeDtypeStruct((1, width), F32)],
        scratch_shapes=[pltpu.VMEM((8, CONV_CB), F32)],
        compiler_params=_params(("parallel", "arbitrary"), 32 << 20),
    )(*args)


GW = SSD_R * SSD_P


def _expand8(v, passes=2):
    r = v.shape[0]
    if r < 8:
        v = jnp.broadcast_to(v, (8, SSD_R))
    ri = lax.broadcasted_iota(jnp.int32, (SSD_R, GW), 0)
    ci = lax.broadcasted_iota(jnp.int32, (SSD_R, GW), 1)
    spread = jnp.where((ci >= ri * SSD_P) & (ci < (ri + 1) * SSD_P), 1.0, 0.0)
    return _dot01(v, spread, passes)[:r]


def _head_pair_split(tile):
    first = lax.broadcasted_iota(jnp.int32, (1, 2 * SSD_P), 1) < SSD_P
    return jnp.where(first, tile, 0.0), jnp.where(first, 0.0, tile)


def _sel(rows, group):
    ri = lax.broadcasted_iota(jnp.int32, (rows, rows // group), 0)
    ci = lax.broadcasted_iota(jnp.int32, (rows, rows // group), 1)
    lo = ci * group
    return jnp.where((ri >= lo) & (ri < lo + group), 1.0, 0.0).astype(F32)


def _dot01(lhs, rhs, passes, split_lhs=True, dims=((1,), (0,))):
    val, m01 = (lhs, rhs) if split_lhs else (rhs, lhs)
    m01 = m01.astype(BF16)
    out = None
    for p in range(passes):
        piece = val.astype(BF16)
        ops = (piece, m01) if split_lhs else (m01, piece)
        d = lax.dot_general(ops[0], ops[1], (dims, ((), ())), preferred_element_type=F32)
        out = d if out is None else out + d
        if p + 1 < passes:
            val = val - piece.astype(F32)
    return out


def _ssd_chunk_terms(adt):
    li = lax.broadcasted_iota(jnp.int32, (CHUNK, CHUNK), 0)
    si = lax.broadcasted_iota(jnp.int32, (CHUNK, CHUNK), 1)
    causal = li >= si
    a_cs = _dot01(jnp.where(causal, 1.0, 0.0), adt, 3, split_lhs=False)
    a_cs_t = _dot01(adt, jnp.where(li <= si, 1.0, 0.0), 3, dims=((0,), (0,)))
    return a_cs, a_cs_t, causal


def _ssd_fwd(xc, dt3, adt3):
    t = xc.shape[0]
    nc = t // CHUNK

    gs = 2

    def body(xs_ref, b_ref, c_ref, dt_ref, adt_ref, y_ref, hp_ref, h_ref):
        @pl.when(pl.program_id(1) == 0)
        def _():
            h_ref[...] = jnp.zeros_like(h_ref)

        for gg in range(gs):
            a_cs, a_cs_t, causal = _ssd_chunk_terms(adt_ref[gg])
            a_last = a_cs[CHUNK - 1:CHUNK, :]
            h = h_ref[gg]
            hp_ref[gg, 0] = h
            xd = xs_ref[:, GW * gg:GW * (gg + 1)] * _expand8(dt_ref[gg])
            bb = b_ref[:, SSD_N * gg:SSD_N * (gg + 1)].astype(BF16)
            cbf = c_ref[:, SSD_N * gg:SSD_N * (gg + 1)].astype(BF16)
            cb = lax.dot_general(cbf, bb, (((1,), (1,)), ((), ())), preferred_element_type=F32)
            yoff = jnp.dot(cbf, h.astype(BF16), preferred_element_type=F32) * _expand8(jnp.exp(a_cs))
            for q in range(SSD_R // 2):
                lmats = []
                for r in (2 * q, 2 * q + 1):
                    seg = jnp.exp(jnp.where(causal, a_cs[:, r:r + 1] - a_cs_t[r:r + 1, :], -jnp.inf))
                    lmats.append((cb * seg).astype(BF16))
                tile = slice(2 * SSD_P * q, 2 * SSD_P * (q + 1))
                xa, xb = _head_pair_split(xd[:, tile])
                y_ref[:, GW * gg + 2 * SSD_P * q:GW * gg + 2 * SSD_P * (q + 1)] = (
                    jnp.dot(jnp.concatenate(lmats, axis=1), jnp.concatenate([xa, xb], axis=0).astype(BF16),
                            preferred_element_type=F32) + yoff[:, tile])
            xdd = (xd * _expand8(jnp.exp(a_last - a_cs))).astype(BF16)
            h_ref[gg] = _expand8(jnp.exp(a_last), 3) * h + lax.dot_general(
                bb, xdd, (((0,), (0,)), ((), ())), preferred_element_type=F32)

    nb = SSD_INNER // (gs * SSD_N)
    return pl.pallas_call(
        body, name="ssd_fwd", grid=(SSD_G // gs, nc),
        in_specs=[pl.BlockSpec((CHUNK, gs * GW), lambda g, c: (c, g)),
                  pl.BlockSpec((CHUNK, gs * SSD_N), lambda g, c: (c, nb + g)),
                  pl.BlockSpec((CHUNK, gs * SSD_N), lambda g, c: (c, nb + SSD_G // gs + g)),
                  pl.BlockSpec((gs, CHUNK, SSD_R), lambda g, c: (g, c, 0)),
                  pl.BlockSpec((gs, CHUNK, SSD_R), lambda g, c: (g, c, 0))],
        out_specs=[pl.BlockSpec((CHUNK, gs * GW), lambda g, c: (c, g)),
                   pl.BlockSpec((gs, 1, SSD_N, GW), lambda g, c: (g, c, 0, 0))],
        out_shape=[jax.ShapeDtypeStruct((t, SSD_INNER), F32), jax.ShapeDtypeStruct((SSD_G, nc, SSD_N, GW), F32)],
        scratch_shapes=[pltpu.VMEM((gs, SSD_N, GW), F32)],
        compiler_params=_params(("parallel", "arbitrary"), 32 << 20),
    )(xc, xc, xc, dt3, adt3)


def _ssd_bwd(xc, dt3, adt3, hprev, dy):
    t = xc.shape[0]
    nc = t // CHUNK

    def body(xs_ref, b_ref, c_ref, dt_ref, adt_ref, hp_ref, dy_ref,
             dx_ref, db_ref, dc_ref, dadt_ref, dxdx_ref, dh_ref):
        @pl.when(pl.program_id(1) == 0)
        def _():
            dh_ref[...] = jnp.zeros_like(dh_ref)

        a_cs, a_cs_t, causal = _ssd_chunk_terms(adt_ref[0])
        a_last = a_cs[CHUNK - 1:CHUNK, :]
        e_last = jnp.exp(a_last)
        ex = _expand8(jnp.exp(a_cs))
        dtex = _expand8(jnp.exp(a_last - a_cs))
        dtx = _expand8(dt_ref[0])
        xs = xs_ref[...]
        dyv = dy_ref[...]
        hp = hp_ref[0, 0]
        dh = dh_ref[...]
        sel = _sel(GW, SSD_P)
        seg8 = lambda v: _dot01(v, sel, 2)

        xd = xs * dtx
        xdd = xd * dtex
        bb = b_ref[...].astype(BF16)
        cbf = c_ref[...].astype(BF16)
        hpb = hp.astype(BF16)
        dhb = dh.astype(BF16)
        xdb = xd.astype(BF16)
        cb = lax.dot_general(cbf, bb, (((1,), (1,)), ((), ())), preferred_element_type=F32)
        dye = (dyv * ex).astype(BF16)
        yoff = jnp.dot(cbf, hpb, preferred_element_type=F32) * ex
        dc = lax.dot_general(dye, hpb, (((1,), (1,)), ((), ())), preferred_element_type=F32)
        d_a = seg8(dyv * yoff)
        bdh = jnp.dot(bb, dhb, preferred_element_type=F32)
        db = lax.dot_general(xdd.astype(BF16), dhb, (((1,), (1,)), ((), ())), preferred_element_type=F32)
        dxd_state = bdh * dtex
        q = seg8(xdd * bdh)
        d_a = d_a - q
        d_a_last = (jnp.sum(q, axis=0, keepdims=True)
                    + e_last * seg8(jnp.sum(hp * dh, axis=0, keepdims=True)))
        dh_ref[...] = (lax.dot_general(cbf, dye, (((0,), (0,)), ((), ())), preferred_element_type=F32)
                       + _expand8(e_last, 3) * dh)
        dcb = jnp.zeros((CHUNK, CHUNK), F32)
        w_all = []
        dxd_parts = []
        for q2 in range(SSD_R // 2):
            tile = slice(2 * SSD_P * q2, 2 * SSD_P * (q2 + 1))
            dy_pair = [part.astype(BF16) for part in _head_pair_split(dyv[:, tile])]
            lmats = []
            for k, r in enumerate((2 * q2, 2 * q2 + 1)):
                seg = jnp.exp(jnp.where(causal, a_cs[:, r:r + 1] - a_cs_t[r:r + 1, :], -jnp.inf))
                lmat = cb * seg
                dm = lax.dot_general(dy_pair[k], xdb[:, tile], (((1,), (1,)), ((), ())), preferred_element_type=F32)
                dcb = dcb + dm * seg
                w_all.append(dm * lmat)
                lmats.append(lmat.astype(BF16))
            dxd_parts.append(lax.dot_general(jnp.concatenate(lmats, axis=0), jnp.concatenate(dy_pair, axis=0),
                                             (((0,), (0,)), ((), ())), preferred_element_type=F32))
        row_sums = _dot01(jnp.concatenate(w_all, axis=1), _sel(SSD_R * CHUNK, CHUNK), 2)
        cs_rows = jnp.concatenate([jnp.sum(wr, axis=0, keepdims=True) for wr in w_all], axis=0)
        col_sums = _dot01(cs_rows, _sel(SSD_R, 1), 3, dims=((0,), (0,)))
        d_a = d_a + row_sums - col_sums
        li = lax.broadcasted_iota(jnp.int32, (CHUNK, SSD_R), 0)
        d_a = d_a + jnp.where(li == CHUNK - 1, d_a_last, 0.0)
        l2 = lax.broadcasted_iota(jnp.int32, (CHUNK, CHUNK), 0)
        s2 = lax.broadcasted_iota(jnp.int32, (CHUNK, CHUNK), 1)
        dadt_ref[0] = _dot01(jnp.where(s2 >= l2, 1.0, 0.0), d_a, 3, split_lhs=False)
        dxd = dxd_state + jnp.concatenate(dxd_parts, axis=1)
        dxdx_ref[0] = seg8(dxd * xs)
        dx_ref[...] = dxd * dtx
        dcbb = dcb.astype(BF16)
        db_ref[...] = db + lax.dot_general(dcbb, cbf, (((0,), (0,)), ((), ())), preferred_element_type=F32)
        dc_ref[...] = dc + jnp.dot(dcbb, bb, preferred_element_type=F32)

    nb = SSD_INNER // SSD_N
    rc = lambda g, c: (nc - 1 - c, g)
    r3 = lambda g, c: (g, nc - 1 - c, 0)
    return pl.pallas_call(
        body, name="ssd_bwd", grid=(SSD_G, nc),
        in_specs=[pl.BlockSpec((CHUNK, GW), rc),
                  pl.BlockSpec((CHUNK, SSD_N), lambda g, c: (nc - 1 - c, nb + g)),
                  pl.BlockSpec((CHUNK, SSD_N), lambda g, c: (nc - 1 - c, nb + SSD_G + g)),
                  pl.BlockSpec((1, CHUNK, SSD_R), r3),
                  pl.BlockSpec((1, CHUNK, SSD_R), r3),
                  pl.BlockSpec((1, 1, SSD_N, GW), lambda g, c: (g, nc - 1 - c, 0, 0)),
                  pl.BlockSpec((CHUNK, GW), rc)],
        out_specs=[pl.BlockSpec((CHUNK, GW), rc),
                   pl.BlockSpec((CHUNK, SSD_N), rc),
                   pl.BlockSpec((CHUNK, SSD_N), rc),
                   pl.BlockSpec((1, CHUNK, SSD_R), r3),
                   pl.BlockSpec((1, CHUNK, SSD_R), r3)],
        out_shape=[jax.ShapeDtypeStruct((t, SSD_INNER), F32),
                   jax.ShapeDtypeStruct((t, SSD_G * SSD_N), F32),
                   jax.ShapeDtypeStruct((t, SSD_G * SSD_N), F32),
                   jax.ShapeDtypeStruct((SSD_G, t, SSD_R), F32),
                   jax.ShapeDtypeStruct((SSD_G, t, SSD_R), F32)],
        scratch_shapes=[pltpu.VMEM((SSD_N, GW), F32)],
        compiler_params=_params(("parallel", "arbitrary"), 40 << 20),
    )(xc, xc, xc, dt3, adt3, hprev, dy)


def _gated_norm_fwd(y, xc, proj, dexp, ng):
    t = y.shape[0]
    tt = _divtile(t, 256)

    def body(y_ref, x_ref, z_ref, d_ref, g_ref, o_ref):
        z = z_ref[...]
        y2 = (y_ref[...] + d_ref[...] * x_ref[...]) * (z * _sigmoid(z))
        for gi in range(SSD_G):
            sl = slice(GW * gi, GW * (gi + 1))
            seg = y2[:, sl]
            rinv = lax.rsqrt(jnp.mean(seg * seg, axis=-1, keepdims=True) + RMS_EPS)
            o_ref[:, sl] = (seg * rinv * g_ref[:, sl]).astype(BF16)

    row = pl.BlockSpec((tt, SSD_INNER), lambda i: (i, 0))
    vec = pl.BlockSpec((1, SSD_INNER), lambda i: (0, 0))
    return pl.pallas_call(
        body, name="gated_norm_fwd", grid=(t // tt,), in_specs=[row, row, row, vec, vec], out_specs=row,
        out_shape=jax.ShapeDtypeStruct((t, SSD_INNER), BF16),
        compiler_params=_params(("parallel",), 48 << 20),
    )(y, xc, proj, dexp, ng)


def _gated_norm_bwd(dout, y, xc, proj, dexp, ng):
    t = y.shape[0]
    tt = _divtile(t, 128)

    def body(do_ref, y_ref, x_ref, z_ref, d_ref, g_ref, dz_ref, dy_ref, dg_ref, dd_ref):
        z = z_ref[...]
        sg = _sigmoid(z)
        sz = z * sg
        xs = x_ref[...]
        y1 = y_ref[...] + d_ref[...] * xs
        y2 = y1 * sz
        dov = do_ref[...]

        @pl.when(pl.program_id(0) == 0)
        def _():
            dg_ref[...] = jnp.zeros_like(dg_ref)
            dd_ref[...] = jnp.zeros_like(dd_ref)

        for gi in range(SSD_G):
            sl = slice(GW * gi, GW * (gi + 1))
            seg = y2[:, sl]
            rinv = lax.rsqrt(jnp.mean(seg * seg, axis=-1, keepdims=True) + RMS_EPS)
            yn = seg * rinv
            dsl = dov[:, sl]
            dg_ref[:, sl] += jnp.sum(dsl * yn, axis=0, keepdims=True)
            dyn = dsl * g_ref[:, sl]
            dy2 = rinv * (dyn - yn * jnp.mean(dyn * yn, axis=-1, keepdims=True))
            dz_ref[:, sl] = (dy2 * y1[:, sl] * (sg[:, sl] * (1.0 + z[:, sl] * (1.0 - sg[:, sl])))).astype(BF16)
            dy1 = dy2 * sz[:, sl]
            dy_ref[:, sl] = dy1
            dd_ref[:, sl] += jnp.sum(dy1 * xs[:, sl], axis=0, keepdims=True)

    row = pl.BlockSpec((tt, SSD_INNER), lambda i: (i, 0))
    vec = pl.BlockSpec((1, SSD_INNER), lambda i: (0, 0))
    return pl.pallas_call(
        body, name="gated_norm_bwd", grid=(t // tt,), in_specs=[row, row, row, row, vec, vec],
        out_specs=[row, row, vec, vec],
        out_shape=[jax.ShapeDtypeStruct((t, SSD_INNER), BF16), jax.ShapeDtypeStruct((t, SSD_INNER), F32),
                   jax.ShapeDtypeStruct((1, SSD_INNER), F32), jax.ShapeDtypeStruct((1, SSD_INNER), F32)],
        compiler_params=_params(("arbitrary",), 48 << 20),
    )(dout, y, xc, proj, dexp, ng)


def _fold_heads(v, name):
    def body(v_ref, o_ref):
        ri = lax.broadcasted_iota(jnp.int32, (SSD_INNER, 128), 0)
        ci = lax.broadcasted_iota(jnp.int32, (SSD_INNER, 128), 1)
        fold = jnp.where((ri >= ci * SSD_P) & (ri < (ci + 1) * SSD_P), 1.0, 0.0).astype(F32)
        o_ref[...] = jnp.dot(v_ref[...], fold, preferred_element_type=F32, precision=HI)

    return pl.pallas_call(body, name=name, out_shape=jax.ShapeDtypeStruct((1, 128), F32))(v)


Q_BLK = SEGS["q"][2] // QW
K_BLK = SEGS["k"][2] // KVW
V_BLK = SEGS["v"][2] // KVW


def _rope_tables(pos_ref, invf_ref, width):
    ang = pos_ref[...] * invf_ref[...]
    lane = lax.broadcasted_iota(jnp.int32, (1, 128), 1)
    sign = jnp.where((lane % HD) < (HD // 2), -1.0, 1.0)
    cos = jnp.tile(jnp.cos(ang), (1, width // 128))
    sin = jnp.tile(sign * jnp.sin(ang), (1, width // 128))
    first = (lax.broadcasted_iota(jnp.int32, (1, width), 1) % HD) < (HD // 2)
    return cos, sin, first


def _rot_half(u, first):
    w = u.shape[1]
    return jnp.where(first, pltpu.roll(u, w - HD // 2, 1), pltpu.roll(u, HD // 2, 1))


def _rope_fwd(proj, pos, invf):
    t = proj.shape[0]
    tt = _divtile(t, 512)

    def body(q_ref, k_ref, pos_ref, invf_ref, qo_ref, ko_ref):
        cos, sin, first = _rope_tables(pos_ref, invf_ref, QW)
        q = q_ref[...]
        qo_ref[...] = (q * cos + _rot_half(q, first) * sin).astype(BF16)
        k = k_ref[...]
        ko_ref[...] = (k * cos[:, :KVW] + _rot_half(k, first[:, :KVW]) * sin[:, :KVW]).astype(BF16)

    return pl.pallas_call(
        body, name="rope_fwd", grid=(t // tt,),
        in_specs=[pl.BlockSpec((tt, QW), lambda i: (i, Q_BLK)), pl.BlockSpec((tt, KVW), lambda i: (i, K_BLK)),
                  pl.BlockSpec((tt, 1), lambda i: (i, 0)), pl.BlockSpec((1, 128), lambda i: (0, 0))],
        out_specs=[pl.BlockSpec((tt, QW), lambda i: (i, 0)), pl.BlockSpec((tt, KVW), lambda i: (i, 0))],
        out_shape=[jax.ShapeDtypeStruct((t, QW), BF16), jax.ShapeDtypeStruct((t, KVW), BF16)],
        compiler_params=_params(("parallel",), 40 << 20),
    )(proj, proj, pos, invf)


def _rope_bwd(dq, dk, pos, invf):
    t = dq.shape[0]
    tt = _divtile(t, 512)

    def body(dq_ref, dk_ref, pos_ref, invf_ref, qo_ref, ko_ref):
        cos, sin, first = _rope_tables(pos_ref, invf_ref, QW)
        q = dq_ref[...]
        qo_ref[...] = (q * cos + _rot_half(q * sin, first)).astype(BF16)
        k = dk_ref[...]
        ko_ref[...] = (k * cos[:, :KVW] + _rot_half(k * sin[:, :KVW], first[:, :KVW])).astype(BF16)

    return pl.pallas_call(
        body, name="rope_bwd", grid=(t // tt,),
        in_specs=[pl.BlockSpec((tt, QW), lambda i: (i, 0)), pl.BlockSpec((tt, KVW), lambda i: (i, 0)),
                  pl.BlockSpec((tt, 1), lambda i: (i, 0)), pl.BlockSpec((1, 128), lambda i: (0, 0))],
        out_specs=[pl.BlockSpec((tt, QW), lambda i: (i, 0)), pl.BlockSpec((tt, KVW), lambda i: (i, 0))],
        out_shape=[jax.ShapeDtypeStruct((t, QW), BF16), jax.ShapeDtypeStruct((t, KVW), BF16)],
        compiler_params=_params(("parallel",), 40 << 20),
    )(dq, dk, pos, invf)


GQ = NQ // NKV
NT_DIMS = (((1,), (1,)), ((), ()))
TN_DIMS = (((0,), (0,)), ((), ()))


def _attn_stack(ref, j, dtype=None):
    parts = [ref[:, HD * h:HD * (h + 1)] for h in range(j * GQ, (j + 1) * GQ)]
    out = jnp.concatenate(parts, axis=0)
    return out if dtype is None else out.astype(dtype)


def _attn_sink_col(s_ref, j):
    return jnp.concatenate([jnp.broadcast_to(s_ref[:, h:h + 1], (WINDOW, 1)) for h in range(j * GQ, (j + 1) * GQ)],
                           axis=0)


def _attn_mask(n):
    qi = lax.broadcasted_iota(jnp.int32, (GQ * WINDOW, 2 * WINDOW), 0) % WINDOW
    kj = lax.broadcasted_iota(jnp.int32, (GQ * WINDOW, 2 * WINDOW), 1)
    return (kj > qi) & (kj <= qi + WINDOW) & ((n > 0) | (kj >= WINDOW))


def _attn_exp(qg, kk, sink, mask):
    s = jnp.where(mask, lax.dot_general(qg, kk, NT_DIMS, preferred_element_type=F32) * (HD ** -0.5), -jnp.inf)
    m = jnp.maximum(jnp.max(s, axis=-1, keepdims=True), sink)
    return jnp.exp(s - m), jnp.exp(sink - m)


def _attn_fwd(qr, kr, proj, sinks):
    t = qr.shape[0]
    nb = t // WINDOW

    def body(q_ref, kc_ref, kp_ref, vc_ref, vp_ref, s_ref, o_ref):
        mask = _attn_mask(pl.program_id(0))
        ones = jnp.ones((2 * WINDOW, HD), BF16)
        for j in range(NKV):
            ks = slice(HD * j, HD * (j + 1))
            kk = jnp.concatenate([kp_ref[:, ks], kc_ref[:, ks]], axis=0)
            vv = jnp.concatenate([vp_ref[:, ks], vc_ref[:, ks]], axis=0).astype(BF16)
            p, ps = _attn_exp(_attn_stack(q_ref, j), kk, _attn_sink_col(s_ref, j), mask)
            oa = jnp.dot(p.astype(BF16), jnp.concatenate([vv, ones], axis=1), preferred_element_type=F32)
            o = (oa[:, :HD] * (1.0 / (oa[:, HD:HD + 1] + ps))).astype(BF16)
            for g in range(GQ):
                h = j * GQ + g
                o_ref[:, HD * h:HD * (h + 1)] = o[WINDOW * g:WINDOW * (g + 1)]

    prev = lambda n: (jnp.maximum(n - 1, 0), 0)
    return pl.pallas_call(
        body, name="attn_fwd", grid=(nb,),
        in_specs=[pl.BlockSpec((WINDOW, QW), lambda n: (n, 0)),
                  pl.BlockSpec((WINDOW, KVW), lambda n: (n, 0)), pl.BlockSpec((WINDOW, KVW), prev),
                  pl.BlockSpec((WINDOW, KVW), lambda n: (n, V_BLK)),
                  pl.BlockSpec((WINDOW, KVW), lambda n: (jnp.maximum(n - 1, 0), V_BLK)),
                  pl.BlockSpec((1, 128), lambda n: (0, 0))],
        out_specs=pl.BlockSpec((WINDOW, QW), lambda n: (n, 0)),
        out_shape=jax.ShapeDtypeStruct((t, QW), BF16),
        compiler_params=_params(("parallel",), 24 << 20),
    )(qr, kr, kr, proj, proj, sinks)


def _attn_bwd(qr, kr, proj, sinks, do):
    t = qr.shape[0]
    nb = t // WINDOW

    def body(q_ref, kc_ref, kp_ref, vc_ref, vp_ref, s_ref, do_ref,
             dq_ref, dk_ref, dv_ref, ds_ref, dkc_ref, dvc_ref):
        i = pl.program_id(0)
        mask = _attn_mask(nb - 1 - i)

        @pl.when(i == 0)
        def _():
            dkc_ref[...] = jnp.zeros_like(dkc_ref)
            dvc_ref[...] = jnp.zeros_like(dvc_ref)
            ds_ref[...] = jnp.zeros_like(ds_ref)

        lane = lax.broadcasted_iota(jnp.int32, (1, 128), 1)
        ds_acc = jnp.zeros((1, 128), F32)
        for j in range(NKV):
            ks = slice(HD * j, HD * (j + 1))
            kk = jnp.concatenate([kp_ref[:, ks], kc_ref[:, ks]], axis=0)
            vv = jnp.concatenate([vp_ref[:, ks], vc_ref[:, ks]], axis=0).astype(BF16)
            qg = _attn_stack(q_ref, j)
            p, ps = _attn_exp(qg, kk, _attn_sink_col(s_ref, j), mask)
            inv = 1.0 / (jnp.sum(p, axis=-1, keepdims=True) + ps)
            pn = p * inv
            dog = _attn_stack(do_ref, j, BF16)
            dp = lax.dot_general(dog, vv, NT_DIMS, preferred_element_type=F32)
            delta = jnp.sum(dp * pn, axis=-1, keepdims=True)
            dsb = (pn * (dp - delta) * (HD ** -0.5)).astype(BF16)
            dsink = -(ps * inv) * delta
            dq = jnp.dot(dsb, kk, preferred_element_type=F32)
            for g in range(GQ):
                h = j * GQ + g
                rows = slice(WINDOW * g, WINDOW * (g + 1))
                dq_ref[:, HD * h:HD * (h + 1)] = dq[rows]
                ds_acc = ds_acc + jnp.where(lane == h, jnp.sum(dsink[rows], axis=0, keepdims=True), 0.0)
            dkk = lax.dot_general(dsb, qg, TN_DIMS, preferred_element_type=F32)
            dvv = lax.dot_general(pn.astype(BF16), dog, TN_DIMS, preferred_element_type=F32)
            dk_ref[:, ks] = dkk[WINDOW:] + dkc_ref[:, ks]
            dv_ref[:, ks] = (dvv[WINDOW:] + dvc_ref[:, ks]).astype(BF16)
            dkc_ref[:, ks] = dkk[:WINDOW]
            dvc_ref[:, ks] = dvv[:WINDOW]
        ds_ref[...] += ds_acc

    cur = lambda i: (nb - 1 - i, 0)
    prev = lambda i: (jnp.maximum(nb - 2 - i, 0), 0)
    return pl.pallas_call(
        body, name="attn_bwd", grid=(nb,),
        in_specs=[pl.BlockSpec((WINDOW, QW), cur),
                  pl.BlockSpec((WINDOW, KVW), cur), pl.BlockSpec((WINDOW, KVW), prev),
                  pl.BlockSpec((WINDOW, KVW), lambda i: (nb - 1 - i, V_BLK)),
                  pl.BlockSpec((WINDOW, KVW), lambda i: (jnp.maximum(nb - 2 - i, 0), V_BLK)),
                  pl.BlockSpec((1, 128), lambda i: (0, 0)),
                  pl.BlockSpec((WINDOW, QW), cur)],
        out_specs=[pl.BlockSpec((WINDOW, QW), cur), pl.BlockSpec((WINDOW, KVW), cur),
                   pl.BlockSpec((WINDOW, KVW), cur), pl.BlockSpec((1, 128), lambda i: (0, 0))],
        out_shape=[jax.ShapeDtypeStruct((t, QW), F32), jax.ShapeDtypeStruct((t, KVW), F32),
                   jax.ShapeDtypeStruct((t, KVW), BF16), jax.ShapeDtypeStruct((1, 128), F32)],
        scratch_shapes=[pltpu.VMEM((WINDOW, KVW), F32), pltpu.VMEM((WINDOW, KVW), F32)],
        compiler_params=_params(("arbitrary",), 32 << 20),
    )(qr, kr, kr, proj, proj, sinks, do)


GS_BLK = SEGS["gs"][2] // D
GA_BLK = SEGS["ga"][2] // D


def _merge_fwd(ys, ya, proj):
    t = ys.shape[0]
    tt = _divtile(t, 256)

    def body(ys_ref, ya_ref, gs_ref, ga_ref, o_ref):
        o_ref[...] = (_sigmoid(gs_ref[...]) * ys_ref[...] + _sigmoid(ga_ref[...]) * ya_ref[...]).astype(BF16)

    row = pl.BlockSpec((tt, D), lambda i: (i, 0))
    return pl.pallas_call(
        body, name="merge_fwd", grid=(t // tt,),
        in_specs=[row, row, pl.BlockSpec((tt, D), lambda i: (i, GS_BLK)), pl.BlockSpec((tt, D), lambda i: (i, GA_BLK))],
        out_specs=row, out_shape=jax.ShapeDtypeStruct((t, D), BF16),
        compiler_params=_params(("parallel",), 32 << 20),
    )(ys, ya, proj, proj)


def _merge_bwd(dm, ys, ya, proj):
    t = ys.shape[0]
    tt = _divtile(t, 256)

    def body(dm_ref, ys_ref, ya_ref, gs_ref, ga_ref, dys_ref, dya_ref, dgs_ref, dga_ref):
        d = dm_ref[...]
        s = _sigmoid(gs_ref[...])
        a = _sigmoid(ga_ref[...])
        dys_ref[...] = (d * s).astype(BF16)
        dya_ref[...] = (d * a).astype(BF16)
        dgs_ref[...] = (d * ys_ref[...] * (s * (1.0 - s))).astype(BF16)
        dga_ref[...] = (d * ya_ref[...] * (a * (1.0 - a))).astype(BF16)

    row = pl.BlockSpec((tt, D), lambda i: (i, 0))
    return pl.pallas_call(
        body, name="merge_bwd", grid=(t // tt,),
        in_specs=[row, row, row, pl.BlockSpec((tt, D), lambda i: (i, GS_BLK)),
                  pl.BlockSpec((tt, D), lambda i: (i, GA_BLK))],
        out_specs=[row, row, row, row], out_shape=[jax.ShapeDtypeStruct((t, D), BF16)] * 4,
        compiler_params=_params(("parallel",), 40 << 20),
    )(dm, ys, ya, proj, proj)


def _pad128(v):
    return jnp.pad(v, ((0, 0), (0, 128 - v.shape[1])))


def _group_major(v):
    t = v.shape[0]
    return jnp.transpose(v[:, :SSD_HEADS].reshape(t, SSD_G, SSD_R), (1, 0, 2))


def _token_major(v3):
    t = v3.shape[1]
    return _pad128(jnp.transpose(v3, (1, 0, 2)).reshape(t, SSD_HEADS))


def _local_step(x, pos, target, w, small, start_token=None, later_weights=None, early_grads=None):
    xb = x.astype(BF16) if start_token is None else (x + start_token[0:1, 0:1]).astype(BF16)
    gu1, a1 = _mm_swiglu(xb, w["gu1"], "ffn1_gu")
    f1 = _mm(a1, w["d1"], "nn", F32, "ffn1_down", caps=(1024, 1024, 1408))
    h1, h1b, xh1, rs1 = _ln_fwd(x, f1, small["ln1_g"], small["ln1_b"], 0.5, "ln1_fwd")
    if later_weights is not None:
        w = {**w, **later_weights(h1b)}
    proj = _mm(h1b, w["win"], "nn", F32, "proj", caps=(1024, 896, 2048))
    bias128 = _pad128(small["dt_bias"])
    alog128 = _pad128(small["a_log"])
    dt, adt = _dt_prep(proj, bias128, alog128)
    dt3, adt3 = _group_major(dt), _group_major(adt)
    xc = _conv_fwd(proj, small["conv_w"], small["conv_b"])
    y_ssd, hprev = _ssd_fwd(xc, dt3, adt3)
    dexp = jnp.repeat(small["d_skip"], SSD_P, axis=1)
    ysn = _gated_norm_fwd(y_ssd, xc, proj, dexp, small["ssd_norm_g"])
    ys = _mm(ysn, w["so"], "nn", F32, "ssd_out")
    invf = jnp.tile(ROPE_THETA ** (-jnp.arange(HD // 2, dtype=F32) * 2.0 / HD), 4)[None, :]
    qr, kr = _rope_fwd(proj, pos, invf)
    sinks128 = _pad128(small["attn_sinks"])
    o = _attn_fwd(qr, kr, proj, sinks128)
    ya = _mm(o, w["ao"], "nn", F32, "attn_out")
    mg = _merge_fwd(ys, ya, proj)
    mix = _mm(mg, w["out"], "nn", F32, "mix_out")
    h2, h2b, xh2, rs2 = _ln_fwd(h1, mix, small["ln2_g"], small["ln2_b"], 1.0, "ln2_fwd")
    gu2, a2 = _mm_swiglu(h2b, w["gu2"], "ffn2_gu")
    f2 = _mm(a2, w["d2"], "nn", F32, "ffn2_down", caps=(1024, 1024, 1408))
    _, _, xh3, rs3, dh3, loss = _ln_fwd(h2, f2, small["ln3_g"], small["ln3_b"], 0.5, "ln3_fwd", target=target)

    gw, gs = {}, {}
    dr3, dr3h, gs["ln3_g"], gs["ln3_b"] = _ln_bwd(dh3, xh3, rs3, small["ln3_g"], 0.5, "ln3_bwd")
    gw["d2"] = _mm(a2, dr3h, "tn", F32, "ffn2_down_dw")
    da2 = _mm(dr3h, w["d2"], "nt", BF16, "ffn2_down_dx", caps=(1024, 1408, 2048))
    dgu2 = _swiglu_bwd(gu2, da2, "ffn2_act_bwd")
    gw["gu2"] = _mm(h2b, dgu2, "tn", F32, "ffn2_gu_dw", caps=(1024, 1408, 2048), n_slabs=N_CHIPS)
    dh2 = _mm(dgu2, w["gu2"], "nt", F32, "ffn2_gu_dx", add=dr3, add_scale=ALPHA, caps=(1024, 1024, 2816))
    dr2, dr2b, gs["ln2_g"], gs["ln2_b"] = _ln_bwd(dh2, xh2, rs2, small["ln2_g"], 1.0, "ln2_bwd")
    gw["out"] = _mm(mg, dr2b, "tn", F32, "mix_out_dw")
    dmg = _mm(dr2b, w["out"], "nt", F32, "mix_out_dx")
    dys, dya, dgs, dga = _merge_bwd(dmg, ys, ya, proj)
    gw["ao"] = _mm(o, dya, "tn", F32, "attn_out_dw")
    do = _mm(dya, w["ao"], "nt", BF16, "attn_out_dx")
    dqr, dkr, dv, gs["attn_sinks"] = _attn_bwd(qr, kr, proj, sinks128, do)
    dq, dk = _rope_bwd(dqr, dkr, pos, invf)
    gw["so"] = _mm(ysn, dys, "tn", F32, "ssd_out_dw")
    dysn = _mm(dys, w["so"], "nt", F32, "ssd_out_dx")
    dz, dy1, gs["ssd_norm_g"], dd_ch = _gated_norm_bwd(dysn, y_ssd, xc, proj, dexp, small["ssd_norm_g"])
    gs["d_skip"] = _fold_heads(dd_ch, "d_skip_fold")
    dxs, db, dc, dadt3, dxdx3 = _ssd_bwd(xc, dt3, adt3, hprev, dy1)
    ddt, gs["dt_bias"], gs["a_log"] = _dt_bwd(_token_major(dadt3), _token_major(dxdx3), proj, bias128, alog128)
    cw, cbias = small["conv_w"], small["conv_b"]
    dux, dwx, dbx = _conv_bwd(proj, dxs, cw, cbias, 0, SSD_INNER, "conv_bwd_x", skip=(dy1, dexp))
    dub, dwb, dbb = _conv_bwd(proj, db, cw, cbias, SSD_INNER, SSD_G * SSD_N, "conv_bwd_b")
    duc, dwc, dbc = _conv_bwd(proj, dc, cw, cbias, SSD_INNER + SSD_G * SSD_N, SSD_G * SSD_N, "conv_bwd_c")
    gs["conv_w"] = jnp.concatenate([dwx[:4], dwb[:4], dwc[:4]], axis=1)
    gs["conv_b"] = jnp.concatenate([dbx, dbb, dbc], axis=1)
    dproj = jnp.concatenate([dz, dq, dgs, dga, dux, dub, duc, dk, dv, ddt], axis=1)
    gw["win"] = _mm(h1b, dproj, "tn", F32, "proj_dw", caps=(1024, 896, 2048))
    win = w["win"] if early_grads is None else early_grads[0](gw, w["win"])
    dh1 = _mm(dproj, win, "nt", F32, "proj_dx", add=dr2, add_scale=ALPHA, caps=(1024, 1024, 2432))
    ln1_g = small["ln1_g"]
    if early_grads is not None:
        ln1_g = ln1_g + early_grads[1](dh1)[0:1, 0:1]
    dr1, dr1h, gs["ln1_g"], gs["ln1_b"] = _ln_bwd(dh1, xh1, rs1, ln1_g, 0.5, "ln1_bwd")
    gw["d1"] = _mm(a1, dr1h, "tn", F32, "ffn1_down_dw")
    da1 = _mm(dr1h, w["d1"], "nt", BF16, "ffn1_down_dx", caps=(1024, 1408, 2048))
    dgu1 = _swiglu_bwd(gu1, da1, "ffn1_act_bwd")
    gw["gu1"] = _mm(xb, dgu1, "tn", F32, "ffn1_gu_dw", caps=(1024, 1408, 2048), n_slabs=N_CHIPS)
    grad_x = _mm(dgu1, w["gu1"], "nt", F32, "ffn1_gu_dx", add=dr1, add_scale=ALPHA, caps=(1024, 1024, 2816))
    return loss, grad_x, gw, gs


MESH = pl.DeviceIdType.MESH
ANY = pl.BlockSpec(memory_space=pl.ANY)


def _place():
    x, y, c = lax.axis_index("x"), lax.axis_index("y"), lax.axis_index("c")
    peers = [(1 - x, y), (x, 1 - y), (1 - x, 1 - y)]
    return x, y, c, peers


BIG = [
    ("ffn1_w_gate", D, SHARD_H, "gu1", "col", 0),
    ("ffn1_w_up", D, SHARD_H, "gu1", "col", SHARD_H),
    ("ffn1_w_down", SHARD_H, D, "d1", "row", 0),
    ("w_in", D, SHARD_IN, "win4", "lead", 0),
    ("w_ssd_o", SSD_INNER // N_CHIPS, D, "so", "row", 0),
    ("w_attn_o", D // N_CHIPS, D, "ao", "row", 0),
    ("w_out", D // N_CHIPS, D, "out", "row", 0),
    ("ffn2_w_gate", D, SHARD_H, "gu2", "col", 0),
    ("ffn2_w_up", D, SHARD_H, "gu2", "col", SHARD_H),
    ("ffn2_w_down", SHARD_H, D, "d2", "row", 0),
]
GATHERED = {"gu1": (D, 2 * FFN_H), "d1": (FFN_H, D), "win4": (N_CHIPS, D, SHARD_IN), "so": (SSD_INNER, D),
            "ao": (D, D), "out": (D, D), "gu2": (D, 2 * FFN_H), "d2": (FFN_H, D)}


def _cast_place(srcs, oname, chip_idx):
    rows, cols = srcs[0].shape
    tr = _divtile(rows, 256, 16)
    kind = [b[4] for b in BIG if b[3] == oname][0]

    def body(chip_ref, *refs):
        o_ref = refs[-1]
        for k, s_ref in enumerate(refs[:-1]):
            o_ref[:, k * cols:(k + 1) * cols] = s_ref[...].astype(BF16)

    nt = rows // tr
    if kind == "col":
        o_spec = pl.BlockSpec((tr, len(srcs) * cols), lambda i, chip_ref: (i, chip_ref[0]))
    elif kind == "row":
        o_spec = pl.BlockSpec((tr, cols), lambda i, chip_ref: (chip_ref[0] * nt + i, 0))
    else:
        o_spec = pl.BlockSpec((None, tr, cols), lambda i, chip_ref: (chip_ref[0], i, 0))
    return pl.pallas_call(
        body, name="cast_place_" + oname,
        grid_spec=pltpu.PrefetchScalarGridSpec(
            num_scalar_prefetch=1, grid=(nt,),
            in_specs=[pl.BlockSpec((tr, cols), lambda i, chip_ref: (i, 0))] * len(srcs), out_specs=o_spec),
        out_shape=jax.ShapeDtypeStruct(GATHERED[oname], BF16),
        compiler_params=_params(("parallel",), 32 << 20),
    )(chip_idx, *srcs)


def _slot(outs, entry, j, half):
    _, rows, cols, oname, kind, off = entry
    o = outs[oname]
    hr = rows // 2
    if kind == "col":
        cs = pl.ds(pl.multiple_of(j * (2 * SHARD_H) + off, 128), cols)
        return o.at[pl.ds(pl.multiple_of(half * hr, 16), hr), cs]
    if kind == "row":
        return o.at[pl.ds(pl.multiple_of(j * rows + half * hr, 16), hr), :]
    return o.at[j, pl.ds(pl.multiple_of(half * hr, 16), hr), :]


HBM = pl.BlockSpec(memory_space=pltpu.HBM)
SEM = pl.BlockSpec(memory_space=pltpu.SEMAPHORE)


def _ici_copy(outs, entry, j, c, to, send, recv, k):
    ref = _slot(outs, entry, j, c)
    return pltpu.make_async_remote_copy(src_ref=ref, dst_ref=ref, send_sem=send.at[k], recv_sem=recv.at[k],
                                        device_id=to, device_id_type=MESH)


def _gather_ici_start(placed, names, after):
    big = [b for b in BIG if b[3] in names]
    n = len(big)
    n_in = len(names) + 1

    def body(*refs):
        outs = dict(zip(names, refs[n_in + 2:n_in + 2 + len(names)]))
        send, recv = refs[n_in], refs[n_in + 1]
        token = refs[-1]
        x, y, c, peers = _place()
        for i, entry in enumerate(big):
            for k, (px, py) in enumerate(peers):
                _ici_copy(outs, entry, 2 * x + y, c, (px, py, c), send, recv, 3 * i + k).start()
        token[...] = jnp.zeros_like(token)

    res = pl.pallas_call(
        body, name="gather_ici_start",
        in_specs=[HBM] * len(names) + [pl.BlockSpec(memory_space=pl.ANY)],
        out_specs=[SEM, SEM] + [HBM] * len(names) + [pl.BlockSpec(memory_space=pltpu.VMEM)],
        out_shape=[pltpu.SemaphoreType.DMA((3 * n,)), pltpu.SemaphoreType.DMA((3 * n,))]
        + [pltpu.HBM(GATHERED[k], BF16) for k in names] + [jax.ShapeDtypeStruct((8, 128), F32)],
        input_output_aliases={i: i + 2 for i in range(len(names))},
        compiler_params=pltpu.CompilerParams(has_side_effects=pltpu.SideEffectType.DATAFLOW_SIDE_EFFECTING),
    )(*[pltpu.with_memory_space_constraint(placed[k], pltpu.HBM) for k in names], after)
    return res[0], res[1], dict(zip(names, res[2:2 + len(names)])), res[-1]


def _gather_ici_wait(send, recv, arrays, names, after):
    big = [b for b in BIG if b[3] in names]

    def body(*refs):
        outs = dict(zip(names, refs[:len(names)]))
        send_ref, recv_ref = refs[len(names)], refs[len(names) + 1]
        x, y, c, peers = _place()
        for i, entry in enumerate(big):
            for k, (px, py) in enumerate(peers):
                mine = _ici_copy(outs, entry, 2 * x + y, c, (px, py, c), send_ref, recv_ref, 3 * i + k)
                mine.wait_send()
                theirs = _ici_copy(outs, entry, 2 * px + py, c, (px, py, c), send_ref, recv_ref, 3 * i + k)
                theirs.wait_recv()

    res = pl.pallas_call(
        body, name="gather_ici_wait",
        in_specs=[HBM] * len(names) + [SEM, SEM, pl.BlockSpec(memory_space=pl.ANY)],
        out_specs=[HBM] * len(names),
        out_shape=[pltpu.HBM(GATHERED[k], BF16) for k in names],
        input_output_aliases={i: i for i in range(len(names))},
        compiler_params=pltpu.CompilerParams(has_side_effects=pltpu.SideEffectType.DATAFLOW_SIDE_EFFECTING),
    )(*[arrays[k] for k in names], send, recv, after)
    return dict(zip(names, res))


def _gather_d2d(arrays, names):
    big = [b for b in BIG if b[3] in names]
    n = len(big)

    def body(*refs):
        outs = dict(zip(names, refs[len(names):2 * len(names)]))
        fsend, frecv = refs[2 * len(names):]
        x, y, c, peers = _place()
        cps = []
        for i, entry in enumerate(big):
            for k, (px, py) in enumerate(peers):
                cp = _ici_copy(outs, entry, 2 * px + py, c, (x, y, 1 - c), fsend, frecv, 3 * i + k)
                cp.start()
                cps.append(cp)
        for i, entry in enumerate(big):
            for k, (px, py) in enumerate(peers):
                _ici_copy(outs, entry, 2 * px + py, 1 - c, (x, y, 1 - c), fsend, frecv, 3 * i + k).wait_recv()
        for cp in cps:
            cp.wait_send()

    res = pl.pallas_call(
        body, name="gather_d2d",
        in_specs=[ANY] * len(names), out_specs=[ANY] * len(names),
        out_shape=[jax.ShapeDtypeStruct(GATHERED[k], BF16) for k in names],
        input_output_aliases={i: i for i in range(len(names))},
        scratch_shapes=[pltpu.SemaphoreType.DMA((3 * n,))] * 2,
    )(*[arrays[k] for k in names])
    return dict(zip(names, res))


def _gather_weights(placed, out_names):
    big = [b for b in BIG if b[3] in out_names]
    n = len(big)

    def body(*refs):
        outs = dict(zip(out_names, refs[len(out_names):2 * len(out_names)]))
        send, recv, fsend, frecv = refs[2 * len(out_names):]
        x, y, c, peers = _place()
        me = 2 * x + y

        def slot(i, j, half):
            return _slot(outs, big[i], j, half)

        sends = []
        for i in range(n):
            for k, (px, py) in enumerate(peers):
                cp = pltpu.make_async_remote_copy(src_ref=slot(i, me, c), dst_ref=slot(i, me, c),
                                                  send_sem=send.at[3 * i + k], recv_sem=recv.at[3 * i + k],
                                                  device_id=(px, py, c), device_id_type=MESH)
                cp.start()
                sends.append(cp)
        fwds = []
        for i in range(n):
            for k, (px, py) in enumerate(peers):
                pj = 2 * px + py
                pltpu.make_async_remote_copy(src_ref=slot(i, pj, c), dst_ref=slot(i, pj, c),
                                             send_sem=send.at[3 * i + k], recv_sem=recv.at[3 * i + k],
                                             device_id=(px, py, c), device_id_type=MESH).wait_recv()
                cp = pltpu.make_async_remote_copy(src_ref=slot(i, pj, c), dst_ref=slot(i, pj, c),
                                                  send_sem=fsend.at[3 * i + k], recv_sem=frecv.at[3 * i + k],
                                                  device_id=(x, y, 1 - c), device_id_type=MESH)
                cp.start()
                fwds.append(cp)
        for i in range(n):
            for k, (px, py) in enumerate(peers):
                pj = 2 * px + py
                pltpu.make_async_remote_copy(src_ref=slot(i, pj, 1 - c), dst_ref=slot(i, pj, 1 - c),
                                             send_sem=fsend.at[3 * i + k], recv_sem=frecv.at[3 * i + k],
                                             device_id=(x, y, 1 - c), device_id_type=MESH).wait_recv()
        for cp in sends + fwds:
            cp.wait_send()

    outs = pl.pallas_call(
        body, name="gather_weights",
        in_specs=[ANY] * len(out_names), out_specs=[ANY] * len(out_names),
        out_shape=[jax.ShapeDtypeStruct(GATHERED[k], BF16) for k in out_names],
        input_output_aliases={i: i for i in range(len(out_names))},
        scratch_shapes=[pltpu.SemaphoreType.DMA((3 * n,))] * 4,
    )(*[placed[k] for k in out_names])
    return dict(zip(out_names, outs))


FIRST_USED = ["gu1", "d1"]
LATER_USED = ["win4", "so", "ao", "out", "gu2", "d2"]


def _win_pieces():
    pieces = []
    for g0, wd, i0 in SEGS.values():
        for j in range(N_CHIPS):
            lo, hi = max(g0, j * SHARD_IN), min(g0 + wd, (j + 1) * SHARD_IN)
            if lo < hi:
                pieces.append((j, lo - j * SHARD_IN, hi - j * SHARD_IN, i0 + lo - g0))
    return pieces


def _win_to_internal(win4):
    tr = 128

    def body(i_ref, o_ref):
        for j, s0, s1, d0 in _win_pieces():
            o_ref[:, d0:d0 + s1 - s0] = i_ref[j, :, s0:s1]
        o_ref[:, PROJ_W:] = jnp.zeros((tr, PROJ_PAD - PROJ_W), o_ref.dtype)

    return pl.pallas_call(
        body, name="win_to_internal", grid=(D // tr,),
        in_specs=[pl.BlockSpec((N_CHIPS, tr, SHARD_IN), lambda i: (0, i, 0))],
        out_specs=pl.BlockSpec((tr, PROJ_PAD), lambda i: (i, 0)),
        out_shape=jax.ShapeDtypeStruct((D, PROJ_PAD), win4.dtype),
        compiler_params=_params(("parallel",), 40 << 20),
    )(win4)


def _win_from_internal(g):
    tr = 64

    def body(i_ref, o_ref):
        for j, s0, s1, d0 in _win_pieces():
            o_ref[j, :, s0:s1] = i_ref[:, d0:d0 + s1 - s0]

    return pl.pallas_call(
        body, name="win_from_internal", grid=(D // tr,),
        in_specs=[pl.BlockSpec((tr, PROJ_PAD), lambda i: (i, 0))],
        out_specs=pl.BlockSpec((N_CHIPS, tr, SHARD_IN), lambda i: (0, i, 0)),
        out_shape=jax.ShapeDtypeStruct((N_CHIPS, D, SHARD_IN), g.dtype),
        compiler_params=_params(("parallel",), 40 << 20),
    )(g)


def _rs_pair_exchange(grads, tag):
    n = len(grads)

    def body(*refs):
        srcs, dsts = refs[:n], refs[n:2 * n]
        send, recv = refs[2 * n:]
        x, y, c, _ = _place()
        cps = []
        for i in range(n):
            hr = srcs[i].shape[1] // 2
            cp = pltpu.make_async_remote_copy(
                src_ref=srcs[i].at[:, pl.ds(pl.multiple_of((1 - c) * hr, 16), hr), :], dst_ref=dsts[i],
                send_sem=send.at[i], recv_sem=recv.at[i], device_id=(x, y, 1 - c), device_id_type=MESH)
            cp.start()
            cps.append(cp)
        for cp in cps:
            cp.wait()

    return pl.pallas_call(
        body, name="rs_pair_exchange_" + tag, in_specs=[ANY] * n, out_specs=[ANY] * n,
        out_shape=[jax.ShapeDtypeStruct((g.shape[0], g.shape[1] // 2, g.shape[2]), F32) for g in grads],
        scratch_shapes=[pltpu.SemaphoreType.DMA((n,))] * 2,
    )(*grads)


def _pair_copy(src, dst, c, to, send, recv, k):
    hr = src.shape[1] // 2
    return pltpu.make_async_remote_copy(
        src_ref=src.at[:, pl.ds(pl.multiple_of((1 - c) * hr, 16), hr), :], dst_ref=dst,
        send_sem=send.at[k], recv_sem=recv.at[k], device_id=to, device_id_type=MESH)


def _rs_pair_start(grads, carried):
    n = len(grads)

    def body(*refs):
        send, recv = refs[2 * n + 1], refs[2 * n + 2]
        srcs, dsts = refs[2 * n + 3:3 * n + 3], refs[3 * n + 3:4 * n + 3]
        x, y, c, _ = _place()
        for i in range(n):
            _pair_copy(srcs[i], dsts[i], c, (x, y, 1 - c), send, recv, i).start()

    lands = [lax.empty((g.shape[0], g.shape[1] // 2, g.shape[2]), F32) for g in grads]
    res = pl.pallas_call(
        body, name="rs_pair_start",
        in_specs=[HBM] * (2 * n + 1), out_specs=[SEM, SEM] + [HBM] * (2 * n + 1),
        out_shape=[pltpu.SemaphoreType.DMA((n,)), pltpu.SemaphoreType.DMA((n,))]
        + [pltpu.HBM(g.shape, F32) for g in grads] + [pltpu.HBM(l.shape, F32) for l in lands]
        + [pltpu.HBM(carried.shape, carried.dtype)],
        input_output_aliases={i: i + 2 for i in range(2 * n + 1)},
        compiler_params=pltpu.CompilerParams(has_side_effects=pltpu.SideEffectType.DATAFLOW_SIDE_EFFECTING),
    )(*[pltpu.with_memory_space_constraint(a, pltpu.HBM) for a in list(grads) + lands + [carried]])
    return (res[0], res[1], list(res[2:2 + n]), list(res[2 + n:2 + 2 * n])), res[-1]


def _rs_pair_wait(send, recv, grads, lands, after):
    n = len(grads)

    def body(*refs):
        srcs, dsts = refs[:n], refs[n:2 * n]
        send_ref, recv_ref = refs[2 * n], refs[2 * n + 1]
        x, y, c, _ = _place()
        for i in range(n):
            cp = _pair_copy(srcs[i], dsts[i], c, (x, y, 1 - c), send_ref, recv_ref, i)
            cp.wait_send()
            cp.wait_recv()

    res = pl.pallas_call(
        body, name="rs_pair_wait",
        in_specs=[HBM] * (2 * n) + [SEM, SEM, pl.BlockSpec(memory_space=pl.ANY)],
        out_specs=[HBM] * (2 * n),
        out_shape=[pltpu.HBM(g.shape, F32) for g in grads] + [pltpu.HBM(l.shape, F32) for l in lands],
        input_output_aliases={i: i for i in range(2 * n)},
        compiler_params=pltpu.CompilerParams(has_side_effects=pltpu.SideEffectType.DATAFLOW_SIDE_EFFECTING),
    )(*grads, *lands, send, recv, after)
    return list(res[:n]), list(res[n:])


def _half_tile(hr):
    return _divtile(hr, 256, 16) if hr % 256 == 0 else _divtile(hr, 512, 16)


def _rs_pair_sum(g, r, c_idx, name):
    ns, rows, cols = g.shape
    hr = rows // 2
    tr = _half_tile(hr)
    nt = hr // tr

    def body(c_ref, g_ref, r_ref, ob_ref, of_ref):
        s = g_ref[...] + r_ref[...]
        ob_ref[...] = s.astype(BF16)
        of_ref[...] = s

    blk = pl.BlockSpec((None, tr, cols), lambda j, t, c_ref: (j, t, 0))
    return pl.pallas_call(
        body, name=name,
        grid_spec=pltpu.PrefetchScalarGridSpec(
            num_scalar_prefetch=1, grid=(ns, nt),
            in_specs=[pl.BlockSpec((None, tr, cols), lambda j, t, c_ref: (j, c_ref[0] * nt + t, 0)), blk],
            out_specs=[blk, blk]),
        out_shape=[jax.ShapeDtypeStruct((ns, hr, cols), BF16), jax.ShapeDtypeStruct((ns, hr, cols), F32)],
        compiler_params=_params(("parallel", "parallel"), 48 << 20),
    )(c_idx, g, r)


def _rs_chip_exchange(parts, tag):
    n = len(parts)

    def body(*refs):
        srcs, dsts = refs[:n], refs[n:2 * n]
        send, recv = refs[2 * n:]
        x, y, c, peers = _place()
        cps = []
        for i in range(n):
            for k, (px, py) in enumerate(peers):
                cp = pltpu.make_async_remote_copy(
                    src_ref=srcs[i].at[2 * px + py], dst_ref=dsts[i].at[k],
                    send_sem=send.at[3 * i + k], recv_sem=recv.at[3 * i + k],
                    device_id=(px, py, c), device_id_type=MESH)
                cp.start()
                cps.append(cp)
        for cp in cps:
            cp.wait()

    return pl.pallas_call(
        body, name="rs_chip_exchange_" + tag, in_specs=[ANY] * n, out_specs=[ANY] * n,
        out_shape=[jax.ShapeDtypeStruct((3,) + p.shape[1:], BF16) for p in parts],
        scratch_shapes=[pltpu.SemaphoreType.DMA((3 * n,))] * 2,
    )(*parts)


def _rs_chip_start(parts, tag):
    n = len(parts)

    def body(*refs):
        send, recv = refs[2 * n], refs[2 * n + 1]
        srcs, dsts = refs[2 * n + 2:3 * n + 2], refs[3 * n + 2:4 * n + 2]
        token = refs[-1]
        x, y, c, peers = _place()
        for i in range(n):
            for k, (px, py) in enumerate(peers):
                pltpu.make_async_remote_copy(
                    src_ref=srcs[i].at[2 * px + py], dst_ref=dsts[i].at[k],
                    send_sem=send.at[3 * i + k], recv_sem=recv.at[3 * i + k],
                    device_id=(px, py, c), device_id_type=MESH).start()
        token[...] = jnp.zeros_like(token)

    lands = [lax.empty((3,) + p.shape[1:], BF16) for p in parts]
    res = pl.pallas_call(
        body, name="rs_chip_start_" + tag,
        in_specs=[HBM] * (2 * n),
        out_specs=[SEM, SEM] + [HBM] * (2 * n) + [pl.BlockSpec(memory_space=pltpu.VMEM)],
        out_shape=[pltpu.SemaphoreType.DMA((3 * n,)), pltpu.SemaphoreType.DMA((3 * n,))]
        + [pltpu.HBM(p.shape, BF16) for p in parts] + [pltpu.HBM(l.shape, BF16) for l in lands]
        + [jax.ShapeDtypeStruct((8, 128), F32)],
        input_output_aliases={i: i + 2 for i in range(2 * n)},
        compiler_params=pltpu.CompilerParams(has_side_effects=pltpu.SideEffectType.DATAFLOW_SIDE_EFFECTING),
    )(*[pltpu.with_memory_space_constraint(a, pltpu.HBM) for a in list(parts) + lands])
    return res[0], res[1], list(res[2:2 + n]), list(res[2 + n:2 + 2 * n]), res[-1]


def _rs_chip_wait(send, recv, parts, lands, after, tag):
    n = len(parts)

    def body(*refs):
        srcs, dsts = refs[:n], refs[n:2 * n]
        send_ref, recv_ref = refs[2 * n], refs[2 * n + 1]
        x, y, c, peers = _place()
        for i in range(n):
            for k, (px, py) in enumerate(peers):
                cp = pltpu.make_async_remote_copy(
                    src_ref=srcs[i].at[2 * px + py], dst_ref=dsts[i].at[k],
                    send_sem=send_ref.at[3 * i + k], recv_sem=recv_ref.at[3 * i + k],
                    device_id=(px, py, c), device_id_type=MESH)
                cp.wait_send()
                cp.wait_recv()

    res = pl.pallas_call(
        body, name="rs_chip_wait_" + tag,
        in_specs=[HBM] * (2 * n) + [SEM, SEM, pl.BlockSpec(memory_space=pl.ANY)],
        out_specs=[HBM] * (2 * n),
        out_shape=[pltpu.HBM(p.shape, BF16) for p in parts] + [pltpu.HBM(l.shape, BF16) for l in lands],
        input_output_aliases={i: i for i in range(2 * n)},
        compiler_params=pltpu.CompilerParams(has_side_effects=pltpu.SideEffectType.DATAFLOW_SIDE_EFFECTING),
    )(*parts, *lands, send, recv, after)
    return list(res[n:])


def _rs_final_sum(own, got, chip_idx, c_idx, name):
    ns, hr, cols = own.shape
    tr = _half_tile(hr)
    nt = hr // tr

    def body(chip_ref, c_ref, o_ref, g_ref, out_ref):
        s = o_ref[...]
        for k in range(3):
            s = s + g_ref[k].astype(F32)
        out_ref[...] = s

    return pl.pallas_call(
        body, name=name,
        grid_spec=pltpu.PrefetchScalarGridSpec(
            num_scalar_prefetch=2, grid=(nt,),
            in_specs=[pl.BlockSpec((None, tr, cols), lambda t, chip_ref, c_ref: (chip_ref[0], t, 0)),
                      pl.BlockSpec((3, tr, cols), lambda t, chip_ref, c_ref: (0, t, 0))],
            out_specs=pl.BlockSpec((tr, cols), lambda t, chip_ref, c_ref: (c_ref[0] * nt + t, 0))),
        out_shape=jax.ShapeDtypeStruct((2 * hr, cols), F32),
        compiler_params=_params(("parallel",), 48 << 20),
    )(chip_idx, c_idx, own, got)


def _rs_share_halves(fulls, tag):
    n = len(fulls)

    def body(*refs):
        dsts = refs[n:2 * n]
        send, recv = refs[2 * n:]
        x, y, c, _ = _place()
        cps = []
        for i in range(n):
            hr = dsts[i].shape[0] // 2
            rows = dsts[i].at[pl.ds(pl.multiple_of(c * hr, 8), hr), :]
            cp = pltpu.make_async_remote_copy(src_ref=rows, dst_ref=rows, send_sem=send.at[i], recv_sem=recv.at[i],
                                              device_id=(x, y, 1 - c), device_id_type=MESH)
            cp.start()
            cps.append(cp)
        for i in range(n):
            hr = dsts[i].shape[0] // 2
            other = dsts[i].at[pl.ds(pl.multiple_of((1 - c) * hr, 8), hr), :]
            pltpu.make_async_remote_copy(src_ref=other, dst_ref=other, send_sem=send.at[i], recv_sem=recv.at[i],
                                         device_id=(x, y, 1 - c), device_id_type=MESH).wait_recv()
        for cp in cps:
            cp.wait_send()

    return pl.pallas_call(
        body, name="rs_share_halves_" + tag, in_specs=[ANY] * n, out_specs=[ANY] * n,
        out_shape=[jax.ShapeDtypeStruct(f.shape, F32) for f in fulls],
        input_output_aliases={i: i for i in range(n)},
        scratch_shapes=[pltpu.SemaphoreType.DMA((n,))] * 2,
    )(*fulls)


def _all_reduce_small(v):
    rows = v.shape[0]

    def body(v_ref, o_ref, buf, send, recv):
        x, y, c, _ = _place()
        me = 4 * x + 2 * y + c
        buf[me] = v_ref[...]
        cps = []
        for d in range(1, 8):
            px, py, pc = x ^ (d >> 2), y ^ ((d >> 1) & 1), c ^ (d & 1)
            cp = pltpu.make_async_remote_copy(src_ref=v_ref, dst_ref=buf.at[me], send_sem=send.at[d - 1],
                                              recv_sem=recv.at[d - 1], device_id=(px, py, pc), device_id_type=MESH)
            cp.start()
            cps.append(cp)
        for d in range(1, 8):
            px, py, pc = x ^ (d >> 2), y ^ ((d >> 1) & 1), c ^ (d & 1)
            pltpu.make_async_remote_copy(src_ref=v_ref, dst_ref=buf.at[4 * px + 2 * py + pc], send_sem=send.at[d - 1],
                                         recv_sem=recv.at[d - 1], device_id=(px, py, pc),
                                         device_id_type=MESH).wait_recv()
        for cp in cps:
            cp.wait_send()
        acc = buf[0]
        for d in range(1, 8):
            acc = acc + buf[d]
        o_ref[...] = acc

    vm = pl.BlockSpec(memory_space=pltpu.VMEM)
    return pl.pallas_call(
        body, name="all_reduce_small", in_specs=[vm], out_specs=vm,
        out_shape=jax.ShapeDtypeStruct((rows, 128), F32),
        scratch_shapes=[pltpu.VMEM((8, rows, 128), F32), pltpu.SemaphoreType.DMA((7,)), pltpu.SemaphoreType.DMA((7,))],
    )(v)


def _adamw(w, g, m, v, name, g_col_blk=0):
    rows, cols = w.shape
    tr = _divtile(rows, max(8, (2 << 20) // (4 * cols) // 8 * 8), 8)

    def body(w_ref, g_ref, m_ref, v_ref, go_ref, d_ref, mo_ref, vo_ref):
        gv = g_ref[...]
        mn = ADAM_B1 * m_ref[...] + (1.0 - ADAM_B1) * gv
        vn = ADAM_B2 * v_ref[...] + (1.0 - ADAM_B2) * (gv * gv)
        m_hat = mn / (1.0 - ADAM_B1 ** ADAM_STEP)
        v_hat = vn / (1.0 - ADAM_B2 ** ADAM_STEP)
        go_ref[...] = gv
        d_ref[...] = -ADAM_LR * (m_hat / (jnp.sqrt(v_hat) + ADAM_EPS) + ADAM_WD * w_ref[...])
        mo_ref[...] = mn
        vo_ref[...] = vn

    blk = pl.BlockSpec((tr, cols), lambda i: (i, 0))
    return pl.pallas_call(
        body, name=name, grid=(rows // tr,),
        in_specs=[blk, pl.BlockSpec((tr, cols), lambda i: (i, g_col_blk)), blk, blk],
        out_specs=[blk] * 4, out_shape=[jax.ShapeDtypeStruct((rows, cols), F32)] * 4,
        compiler_params=_params(("parallel",), 48 << 20),
    )(w, g, m, v)


SMALL = ["ln1_g", "ln1_b", "conv_w", "conv_b", "dt_bias", "a_log", "d_skip", "ssd_norm_g", "attn_sinks",
         "ln2_g", "ln2_b", "ln3_g", "ln3_b"]


def _pack_rows(vs):
    parts = []
    for v in vs:
        v = v.reshape(-1)
        parts.append(jnp.pad(v, (0, (-v.shape[0]) % 128)))
    flat = jnp.concatenate(parts)
    flat = jnp.pad(flat, (0, (-flat.shape[0]) % 1024))
    return flat.reshape(-1, 128)


def _unpack_rows(packed, shapes):
    flat = packed.reshape(-1)
    out, at = [], 0
    for s in shapes:
        nel = int(np.prod(s))
        out.append(flat[at:at + nel].reshape(s))
        at += nel + (-nel) % 128
    return out


def kernel(x, positions, ffn1_w_gate, ffn1_w_up, ffn1_w_down, ln1_g, ln1_b, w_in, conv_w, conv_b, dt_bias, a_log, d_skip, ssd_norm_g, w_ssd_o, attn_sinks, w_attn_o, w_out, ln2_g, ln2_b, ffn2_w_gate, ffn2_w_up, ffn2_w_down, ln3_g, ln3_b, loss_target, m_ffn1_w_gate, m_ffn1_w_up, m_ffn1_w_down, m_ln1_g, m_ln1_b, m_w_in, m_conv_w, m_conv_b, m_dt_bias, m_a_log, m_d_skip, m_ssd_norm_g, m_w_ssd_o, m_attn_sinks, m_w_attn_o, m_w_out, m_ln2_g, m_ln2_b, m_ffn2_w_gate, m_ffn2_w_up, m_ffn2_w_down, m_ln3_g, m_ln3_b, v_ffn1_w_gate, v_ffn1_w_up, v_ffn1_w_down, v_ln1_g, v_ln1_b, v_w_in, v_conv_w, v_conv_b, v_dt_bias, v_a_log, v_d_skip, v_ssd_norm_g, v_w_ssd_o, v_attn_sinks, v_w_attn_o, v_w_out, v_ln2_g, v_ln2_b, v_ffn2_w_gate, v_ffn2_w_up, v_ffn2_w_down, v_ln3_g, v_ln3_b):
    args = dict(locals())
    wts = {n: args[n][0] for n in [b[0] for b in BIG] + SMALL}
    mom_m = {n: args["m_" + n][0] for n in wts}
    mom_v = {n: args["v_" + n][0] for n in wts}
    t = x.shape[1]
    xi, yi, ci = lax.axis_index("x"), lax.axis_index("y"), lax.axis_index("c")
    chip = 2 * xi + yi

    c_idx = ci.astype(jnp.int32).reshape(1)
    chip_idx = chip.astype(jnp.int32).reshape(1)
    placed = {o: _cast_place([wts[b[0]] for b in BIG if b[3] == o], o, chip_idx) for o in GATHERED}
    w = _gather_weights(placed, FIRST_USED)
    g_send, g_recv, g_flight, g_token = _gather_ici_start(placed, LATER_USED, w["d1"])

    def later_weights(h1b):
        arrived = _gather_d2d(_gather_ici_wait(g_send, g_recv, g_flight, LATER_USED, h1b), LATER_USED)
        rest = {k: arrived[k] for k in ("so", "ao", "out", "gu2", "d2")}
        rest["win"] = _win_to_internal(arrived["win4"])
        return rest

    def slabs_of(gw, names):
        view = {"gu1": lambda: gw["gu1"], "gu2": lambda: gw["gu2"],
                "d1": lambda: gw["d1"].reshape(N_CHIPS, SHARD_H, D), "d2": lambda: gw["d2"].reshape(N_CHIPS, SHARD_H, D),
                "win": lambda: _win_from_internal(gw["win"]),
                "so": lambda: gw["so"].reshape(N_CHIPS, SSD_INNER // N_CHIPS, D),
                "ao": lambda: gw["ao"].reshape(N_CHIPS, D // N_CHIPS, D),
                "out": lambda: gw["out"].reshape(N_CHIPS, D // N_CHIPS, D)}
        return [view[nm]() for nm in names]

    early = ["win", "so", "ao", "out", "gu2", "d2"]
    late = ["gu1", "d1"]
    flight = {}

    def early_start(gw, win):
        flight["pair"], win = _rs_pair_start(slabs_of(gw, early), win)
        return win

    def early_mid(dh1):
        slabs, from_sib = _rs_pair_wait(*flight["pair"], dh1)
        pair = [_rs_pair_sum(g, r, c_idx, "rs_pair_sum_" + nm) for g, r, nm in zip(slabs, from_sib, early)]
        send, recv, parts, lands, token = _rs_chip_start([p[0] for p in pair], "early")
        flight.update(send=send, recv=recv, parts=parts, lands=lands, own=[p[1] for p in pair])
        return token

    early_grads = (early_start, early_mid)
    cw_rows = _pack_rows([lax.dynamic_update_slice(jnp.zeros((4, XBC), F32), wts["conv_w"], (0, chip * (XBC // N_CHIPS)))])
    cw_rows = jnp.where(ci == 0, cw_rows, 0.0)
    conv_w_full = _all_reduce_small(cw_rows)[:4 * XBC // 128].reshape(4, XBC)

    small = {n: (wts[n][None, :] if wts[n].ndim == 1 else wts[n]) for n in SMALL}
    small["conv_w"] = conv_w_full
    loss, grad_x, gw, gs = _local_step(x[0], positions[0].astype(F32)[:, None], loss_target[0], w, small,
                                       start_token=g_token, later_weights=later_weights, early_grads=early_grads)

    slabs = slabs_of(gw, late)
    from_sib = _rs_pair_exchange(slabs, "late")
    pair = [_rs_pair_sum(g, r, c_idx, "rs_pair_sum_" + nm) for g, r, nm in zip(slabs, from_sib, late)]
    l_send, l_recv, l_parts, l_lands, l_token = _rs_chip_start([p[0] for p in pair], "late")
    got_early = _rs_chip_wait(flight["send"], flight["recv"], flight["parts"], flight["lands"], l_token, "early")

    outs = {}
    big_src = {"ffn1_w_gate": ("gu1", 0), "ffn1_w_up": ("gu1", 1), "ffn1_w_down": ("d1", 0), "w_in": ("win", 0),
               "w_ssd_o": ("so", 0), "w_attn_o": ("ao", 0), "w_out": ("out", 0),
               "ffn2_w_gate": ("gu2", 0), "ffn2_w_up": ("gu2", 1), "ffn2_w_down": ("d2", 0)}

    def finish(names, own, got, tag):
        halves = [_rs_final_sum(o, gt, chip_idx, c_idx, "rs_final_sum_" + nm) for o, gt, nm in zip(own, got, names)]
        full = dict(zip(names, _rs_share_halves(halves, tag)))
        for nm, (src, blk) in big_src.items():
            if src in full:
                outs[nm] = _adamw(wts[nm], full[src], mom_m[nm], mom_v[nm], "adamw_" + nm, g_col_blk=blk)

    finish(early, flight["own"], got_early, "early")
    got_late = _rs_chip_wait(l_send, l_recv, l_parts, l_lands, outs["w_in"][1], "late")
    finish(late, [p[1] for p in pair], got_late, "late")

    gvec = {n: gs[n] for n in SMALL}
    gvec["dt_bias"], gvec["a_log"], gvec["d_skip"] = gs["dt_bias"][:, :64], gs["a_log"][:, :64], gs["d_skip"][:, :64]
    gvec["attn_sinks"] = gs["attn_sinks"][:, :NQ]
    red = _all_reduce_small(_pack_rows([gvec[n] for n in SMALL] + [loss]))
    shapes = [(4, XBC) if n == "conv_w" else wts[n].shape for n in SMALL] + [(1,)]
    red_list = _unpack_rows(red, shapes)
    loss_out = red_list[-1].reshape(())
    gsm = dict(zip(SMALL, red_list[:-1]))
    gsm["conv_w"] = lax.dynamic_slice_in_dim(gsm["conv_w"], chip * (XBC // N_CHIPS), XBC // N_CHIPS, axis=1)
    sm_shapes = [wts[n].shape for n in SMALL]
    res = _adamw(_pack_rows([wts[n] for n in SMALL]), _pack_rows([gsm[n] for n in SMALL]),
                 _pack_rows([mom_m[n] for n in SMALL]), _pack_rows([mom_v[n] for n in SMALL]), "adamw_small")
    res = [_unpack_rows(r, sm_shapes) for r in res]
    for i, nm in enumerate(SMALL):
        outs[nm] = tuple(r[i] for r in res)

    order = ["ffn1_w_gate", "ffn1_w_up", "ffn1_w_down", "ln1_g", "ln1_b", "w_in", "conv_w", "conv_b", "dt_bias", "a_log",
             "d_skip", "ssd_norm_g", "w_ssd_o", "attn_sinks", "w_attn_o", "w_out", "ln2_g", "ln2_b",
             "ffn2_w_gate", "ffn2_w_up", "ffn2_w_down", "ln3_g", "ln3_b"]
    result = [loss_out, grad_x[None]]
    for kind in range(4):
        result += [outs[nm][kind][None] for nm in order]
    return tuple(result)
```

```python
import functools
import math

import numpy as np
import jax
import jax.numpy as jnp
from jax import lax
from jax.experimental import pallas as pl
from jax.experimental.pallas import tpu as pltpu

F32 = jnp.float32
BF16 = jnp.bfloat16
HI = lax.Precision.HIGHEST

D = 2048
FFN_H = 5632
SSD_INNER = 4096
SSD_HEADS = 64
SSD_P = 64
SSD_G = 8
SSD_R = 8
SSD_N = 128
CHUNK = 128
XBC = 6144
NQ = 32
NKV = 4
HD = 64
QW = 2048
KVW = 256
WINDOW = 128
ROPE_THETA = 10000.0
ALPHA = 2.0 ** 0.25
LN_EPS = 1e-5
RMS_EPS = 1e-5
PROJ_W = 16960
N_CHIPS = 4
SHARD_IN = PROJ_W // N_CHIPS
SHARD_H = FFN_H // N_CHIPS

SEGS = {
    "z": (0, 4096, 0),
    "xbc": (4096, 6144, 10240),
    "dt": (10240, 64, 16896),
    "q": (10304, 2048, 4096),
    "k": (12352, 256, 16384),
    "v": (12608, 256, 16640),
    "gs": (12864, 2048, 6144),
    "ga": (14912, 2048, 8192),
}
PROJ_PAD = 17024

ADAM_LR = 0.001
ADAM_B1 = 0.9
ADAM_B2 = 0.999
ADAM_EPS = 1e-08
ADAM_WD = 0.01
ADAM_STEP = 10

VMEM_CAP = 60 * 1024 * 1024


def _params(sem, vmem_bytes):
    return pltpu.CompilerParams(dimension_semantics=sem, vmem_limit_bytes=int(min(VMEM_CAP, vmem_bytes)))


def _divtile(n, cap, q=128):
    best = None
    for d in range(q, min(n, cap) + 1, q):
        if n % d == 0:
            best = d
    return n if best is None else best


def _sigmoid(x):
    return 1.0 / (1.0 + jnp.exp(-x))


def _mm(a, b, mode, out_dtype, name, add=None, add_scale=1.0, caps=(1024, 1024, 2048), n_slabs=1):
    if mode == "nn":
        (m, k), (k2, n) = a.shape, b.shape
    elif mode == "nt":
        (m, k), (n, k2) = a.shape, b.shape
    else:
        (k, m), (k2, n) = a.shape, b.shape
    assert k == k2, (a.shape, b.shape, mode)
    tm, tn, tk = _divtile(m, caps[0]), _divtile(n // n_slabs, caps[1]), _divtile(k, caps[2])
    nk = k // tk
    per_slab = n // n_slabs // tn
    dims = {"nn": ((1,), (0,)), "nt": ((1,), (1,)), "tn": ((0,), (0,))}[mode]
    has_add = add is not None

    def body(*refs):
        if has_add:
            a_ref, b_ref, add_ref, o_ref = refs[:4]
            scr = refs[4:]
        else:
            a_ref, b_ref, o_ref = refs[:3]
            add_ref = None
            scr = refs[3:]
        part = lax.dot_general(a_ref[...].astype(BF16), b_ref[...].astype(BF16), (dims, ((), ())),
                               preferred_element_type=F32)

        def finish(acc):
            if has_add:
                acc = acc + add_scale * add_ref[...].astype(F32)
            o_ref[...] = acc.astype(o_ref.dtype)

        if nk == 1:
            finish(part)
        else:
            acc_ref = scr[0]
            kk = pl.program_id(2)

            @pl.when(kk == 0)
            def _():
                acc_ref[...] = part

            @pl.when(kk > 0)
            def _():
                acc_ref[...] += part

            @pl.when(kk == nk - 1)
            def _():
                finish(acc_ref[...])

    if mode == "nn":
        a_spec = pl.BlockSpec((tm, tk), lambda i, j, kk: (i, kk))
        b_spec = pl.BlockSpec((tk, tn), lambda i, j, kk: (kk, j))
    elif mode == "nt":
        a_spec = pl.BlockSpec((tm, tk), lambda i, j, kk: (i, kk))
        b_spec = pl.BlockSpec((tn, tk), lambda i, j, kk: (j, kk))
    else:
        a_spec = pl.BlockSpec((tk, tm), lambda i, j, kk: (kk, i))
        b_spec = pl.BlockSpec((tk, tn), lambda i, j, kk: (kk, j))
    o_spec = pl.BlockSpec((tm, tn), lambda i, j, kk: (i, j))
    out_shape = jax.ShapeDtypeStruct((m, n), out_dtype)
    if n_slabs > 1:
        assert not has_add
        o_spec = pl.BlockSpec((None, tm, tn), lambda i, j, kk: (j // per_slab, i, j % per_slab))
        out_shape = jax.ShapeDtypeStruct((n_slabs, m, n // n_slabs), out_dtype)
    in_specs = [a_spec, b_spec] + ([o_spec] if has_add else [])
    args = (a, b) + ((add,) if has_add else ())
    osz = jnp.dtype(out_dtype).itemsize
    vmem = (2 * (tm * tk * a.dtype.itemsize + tk * tn * b.dtype.itemsize) + 2 * tm * tn * osz
            + (2 * tm * tn * add.dtype.itemsize if has_add else 0) + 2 * tm * tn * 4
            + 2 * (tm * tk + tk * tn) + (8 << 20))
    return pl.pallas_call(
        body, name=name, grid=(m // tm, n // tn, nk),
        in_specs=in_specs, out_specs=o_spec, out_shape=out_shape,
        scratch_shapes=[pltpu.VMEM((tm, tn), F32)] if nk > 1 else [],
        compiler_params=_params(("parallel", "parallel", "arbitrary"), vmem),
    )(*args)


def _mm_swiglu(a, b, name):
    m, k = a.shape
    w = SHARD_H
    tm = _divtile(m, 512)

    def body(a_ref, b_ref, gu_ref, act_ref):
        gu = jnp.dot(a_ref[...], b_ref[...], preferred_element_type=F32)
        g = gu[:, :w]
        gu_ref[...] = gu.astype(BF16)
        act_ref[...] = (g * _sigmoid(g) * gu[:, w:]).astype(BF16)

    return pl.pallas_call(
        body, name=name, grid=(N_CHIPS, m // tm),
        in_specs=[pl.BlockSpec((tm, k), lambda j, i: (i, 0)), pl.BlockSpec((k, 2 * w), lambda j, i: (0, j))],
        out_specs=[pl.BlockSpec((tm, 2 * w), lambda j, i: (i, j)), pl.BlockSpec((tm, w), lambda j, i: (i, j))],
        out_shape=[jax.ShapeDtypeStruct((m, 2 * FFN_H), BF16), jax.ShapeDtypeStruct((m, FFN_H), BF16)],
        compiler_params=_params(("parallel", "parallel"), 56 << 20),
    )(a, b)


def _swiglu_bwd(gu, da, name):
    t = gu.shape[0]
    tt = _divtile(t, 512)
    w = SHARD_H

    def body(gu_ref, da_ref, o_ref):
        g = gu_ref[:, :w].astype(F32)
        u = gu_ref[:, w:].astype(F32)
        d = da_ref[...].astype(F32)
        s = _sigmoid(g)
        o_ref[:, :w] = (d * u * (s * (1.0 + g * (1.0 - s)))).astype(BF16)
        o_ref[:, w:] = (d * (g * s)).astype(BF16)

    return pl.pallas_call(
        body, name=name, grid=(t // tt, N_CHIPS),
        in_specs=[pl.BlockSpec((tt, 2 * w), lambda i, j: (i, j)), pl.BlockSpec((tt, w), lambda i, j: (i, j))],
        out_specs=pl.BlockSpec((tt, 2 * w), lambda i, j: (i, j)),
        out_shape=jax.ShapeDtypeStruct((t, 2 * FFN_H), BF16),
        compiler_params=_params(("parallel", "parallel"), 40 << 20),
    )(gu, da)


def _ln_fwd(base, f, g, b, c, name, target=None):
    t = base.shape[0]
    tt = _divtile(t, 256)
    with_loss = target is not None

    def body(*refs):
        if with_loss:
            base_ref, f_ref, g_ref, b_ref, tg_ref, h_ref, hb_ref, xh_ref, rs_ref, dh_ref, loss_ref = refs
        else:
            base_ref, f_ref, g_ref, b_ref, h_ref, hb_ref, xh_ref, rs_ref = refs
        r = ALPHA * base_ref[...] + c * f_ref[...]
        mu = jnp.mean(r, axis=-1, keepdims=True)
        xc = r - mu
        var = jnp.mean(xc * xc, axis=-1, keepdims=True)
        rstd = lax.rsqrt(var + LN_EPS)
        xh = xc * rstd
        h = xh * g_ref[...] + b_ref[...]
        h_ref[...] = h
        hb_ref[...] = h.astype(BF16)
        xh_ref[...] = xh
        rs_ref[...] = rstd
        if with_loss:
            e = h - tg_ref[...]
            dh_ref[...] = e * (1.0 / D)
            part = 0.5 * jnp.sum(jnp.sum(e * e, axis=-1, keepdims=True) * (1.0 / D), axis=0, keepdims=True)

            @pl.when(pl.program_id(0) == 0)
            def _():
                loss_ref[...] = jnp.zeros_like(loss_ref)

            loss_ref[...] += part

    row = pl.BlockSpec((tt, D), lambda i: (i, 0))
    vec = pl.BlockSpec((1, D), lambda i: (0, 0))
    col = pl.BlockSpec((tt, 1), lambda i: (i, 0))
    in_specs = [row, row, vec, vec] + ([row] if with_loss else [])
    out_specs = [row, row, row, col] + ([row, pl.BlockSpec((1, 1), lambda i: (0, 0))] if with_loss else [])
    out_shape = [jax.ShapeDtypeStruct((t, D), F32), jax.ShapeDtypeStruct((t, D), BF16),
                 jax.ShapeDtypeStruct((t, D), F32), jax.ShapeDtypeStruct((t, 1), F32)]
    if with_loss:
        out_shape += [jax.ShapeDtypeStruct((t, D), F32), jax.ShapeDtypeStruct((1, 1), F32)]
    args = (base, f, g, b) + ((target,) if with_loss else ())
    return pl.pallas_call(
        body, name=name, grid=(t // tt,), in_specs=in_specs, out_specs=out_specs, out_shape=out_shape,
        compiler_params=_params(("arbitrary",) if with_loss else ("parallel",), 48 << 20),
    )(*args)


def _ln_bwd(dy, xh, rstd, g, c, name):
    t = dy.shape[0]
    tt = _divtile(t, 256)

    def body(dy_ref, xh_ref, rs_ref, g_ref, dr_ref, drb_ref, dg_ref, db_ref):
        dyv = dy_ref[...]
        xhv = xh_ref[...]
        dxh = dyv * g_ref[...]
        m1 = jnp.mean(dxh, axis=-1, keepdims=True)
        m2 = jnp.mean(dxh * xhv, axis=-1, keepdims=True)
        dr = rs_ref[...] * (dxh - m1 - xhv * m2)
        dr_ref[...] = dr
        drb_ref[...] = (c * dr).astype(BF16)

        @pl.when(pl.program_id(0) == 0)
        def _():
            dg_ref[...] = jnp.zeros_like(dg_ref)
            db_ref[...] = jnp.zeros_like(db_ref)

        dg_ref[...] += jnp.sum(dyv * xhv, axis=0, keepdims=True)
        db_ref[...] += jnp.sum(dyv, axis=0, keepdims=True)

    row = pl.BlockSpec((tt, D), lambda i: (i, 0))
    vec = pl.BlockSpec((1, D), lambda i: (0, 0))
    col = pl.BlockSpec((tt, 1), lambda i: (i, 0))
    return pl.pallas_call(
        body, name=name, grid=(t // tt,), in_specs=[row, row, col, vec], out_specs=[row, row, vec, vec],
        out_shape=[jax.ShapeDtypeStruct((t, D), F32), jax.ShapeDtypeStruct((t, D), BF16),
                   jax.ShapeDtypeStruct((1, D), F32), jax.ShapeDtypeStruct((1, D), F32)],
        compiler_params=_params(("arbitrary",), 40 << 20),
    )(dy, xh, rstd, g)


DT_BLK = SEGS["dt"][2] // 128


def _dt_prep(proj, bias128, alog128):
    t = proj.shape[0]
    tt = _divtile(t, 1024)

    def body(p_ref, bias_ref, alog_ref, dt_ref, adt_ref):
        dtv = jax.nn.softplus(p_ref[...] + bias_ref[...])
        dt_ref[...] = dtv
        adt_ref[...] = dtv * (-jnp.exp(alog_ref[...]))

    blk = pl.BlockSpec((tt, 128), lambda i: (i, 0))
    vec = pl.BlockSpec((1, 128), lambda i: (0, 0))
    return pl.pallas_call(
        body, name="dt_prep", grid=(t // tt,),
        in_specs=[pl.BlockSpec((tt, 128), lambda i: (i, DT_BLK)), vec, vec], out_specs=[blk, blk],
        out_shape=[jax.ShapeDtypeStruct((t, 128), F32)] * 2,
        compiler_params=_params(("parallel",), 16 << 20),
    )(proj, bias128, alog128)


def _dt_bwd(dadt, dxdx, proj, bias128, alog128):
    t = proj.shape[0]
    tt = _divtile(t, 1024)

    def body(dadt_ref, dxdx_ref, p_ref, bias_ref, alog_ref, o_ref, dbias_ref, dalog_ref):
        pre = p_ref[...] + bias_ref[...]
        dtv = jax.nn.softplus(pre)
        a = -jnp.exp(alog_ref[...])
        ddt = a * dadt_ref[...] + dxdx_ref[...]
        draw = ddt * _sigmoid(pre)
        o_ref[...] = draw.astype(BF16)

        @pl.when(pl.program_id(0) == 0)
        def _():
            dbias_ref[...] = jnp.zeros_like(dbias_ref)
            dalog_ref[...] = jnp.zeros_like(dalog_ref)

        dbias_ref[...] += jnp.sum(draw, axis=0, keepdims=True)
        dalog_ref[...] += jnp.sum(dadt_ref[...] * dtv * a, axis=0, keepdims=True)

    blk = pl.BlockSpec((tt, 128), lambda i: (i, 0))
    vec = pl.BlockSpec((1, 128), lambda i: (0, 0))
    return pl.pallas_call(
        body, name="dt_bwd", grid=(t // tt,),
        in_specs=[blk, blk, pl.BlockSpec((tt, 128), lambda i: (i, DT_BLK)), vec, vec],
        out_specs=[blk, vec, vec],
        out_shape=[jax.ShapeDtypeStruct((t, 128), BF16), jax.ShapeDtypeStruct((1, 128), F32),
                   jax.ShapeDtypeStruct((1, 128), F32)],
        compiler_params=_params(("arbitrary",), 16 << 20),
    )(dadt, dxdx, proj, bias128, alog128)


CONV_CB = 512
CONV_TT = 512


def _conv_fwd(proj, conv_w, conv_b):
    t = proj.shape[0]
    tt = _divtile(t, CONV_TT)
    base = SEGS["xbc"][2] // CONV_CB
    r8 = tt // 8

    def body(u_ref, up_ref, w_ref, b_ref, o_ref, st_ref):
        st_ref[0:8, :] = jnp.where(pl.program_id(1) > 0, up_ref[...], 0.0)
        st_ref[8:, :] = u_ref[...]
        acc = b_ref[...]
        for k in range(4):
            acc = acc + w_ref[k:k + 1, :] * st_ref[5 + k:5 + k + tt, :]
        o_ref[...] = acc * _sigmoid(acc)

    return pl.pallas_call(
        body, name="conv_fwd", grid=(XBC // CONV_CB, t // tt),
        in_specs=[pl.BlockSpec((tt, CONV_CB), lambda c, i: (i, base + c)),
                  pl.BlockSpec((8, CONV_CB), lambda c, i: (jnp.maximum(i * r8 - 1, 0), base + c)),
                  pl.BlockSpec((4, CONV_CB), lambda c, i: (0, c)),
                  pl.BlockSpec((1, CONV_CB), lambda c, i: (0, c))],
        out_specs=pl.BlockSpec((tt, CONV_CB), lambda c, i: (i, c)),
        out_shape=jax.ShapeDtypeStruct((t, XBC), F32),
        scratch_shapes=[pltpu.VMEM((tt + 8, CONV_CB), F32)],
        compiler_params=_params(("parallel", "parallel"), 24 << 20),
    )(proj, proj, conv_w, conv_b)


def _conv_bwd(proj, dout, conv_w, conv_b, col0, width, name, skip=None):
    t = proj.shape[0]
    tt = _divtile(t, CONV_TT)
    nt = t // tt
    base = SEGS["xbc"][2] // CONV_CB + col0 // CONV_CB
    wb = col0 // CONV_CB
    r8 = tt // 8
    has_skip = skip is not None

    def body(*refs):
        if has_skip:
            u_ref, up_ref, d_ref, w_ref, b_ref, sk_ref, skw_ref, du_ref, dw_ref, db_ref, st_ref, dst_ref = refs
        else:
            u_ref, up_ref, d_ref, w_ref, b_ref, du_ref, dw_ref, db_ref, st_ref, dst_ref = refs
        i = pl.program_id(1)

        @pl.when(i == 0)
        def _():
            dst_ref[tt:, :] = jnp.zeros((8, CONV_CB), F32)
            dw_ref[...] = jnp.zeros_like(dw_ref)
            db_ref[...] = jnp.zeros_like(db_ref)

        st_ref[0:8, :] = jnp.where(i < nt - 1, up_ref[...], 0.0)
        st_ref[8:, :] = u_ref[...]
        pre = b_ref[...]
        for k in range(4):
            pre = pre + w_ref[k:k + 1, :] * st_ref[5 + k:5 + k + tt, :]
        sg = _sigmoid(pre)
        dout_v = d_ref[...]
        if has_skip:
            dout_v = dout_v + sk_ref[...] * skw_ref[...]
        dpre = dout_v * (sg * (1.0 + pre * (1.0 - sg)))
        dst_ref[0:tt, :] = dpre
        du = w_ref[3:4, :] * dpre
        for s in range(1, 4):
            du = du + w_ref[3 - s:4 - s, :] * dst_ref[s:s + tt, :]
        du_ref[...] = du.astype(BF16)
        rows = [jnp.sum(dpre * st_ref[5 + k:5 + k + tt, :], axis=0, keepdims=True) for k in range(4)]
        dw_ref[...] += jnp.concatenate(rows + [jnp.zeros((4, CONV_CB), F32)], axis=0)
        db_ref[...] += jnp.sum(dpre, axis=0, keepdims=True)
        dst_ref[tt:, :] = dpre[:8]

    rev = lambda c, i: (nt - 1 - i, c)
    in_specs = [pl.BlockSpec((tt, CONV_CB), lambda c, i: (nt - 1 - i, base + c)),
                pl.BlockSpec((8, CONV_CB), lambda c, i: (jnp.maximum((nt - 1 - i) * r8 - 1, 0), base + c)),
                pl.BlockSpec((tt, CONV_CB), rev),
                pl.BlockSpec((4, CONV_CB), lambda c, i: (0, wb + c)),
                pl.BlockSpec((1, CONV_CB), lambda c, i: (0, wb + c))]
    args = [proj, proj, dout, conv_w, conv_b]
    if has_skip:
        in_specs += [pl.BlockSpec((tt, CONV_CB), rev), pl.BlockSpec((1, CONV_CB), lambda c, i: (0, c))]
        args += [skip[0], skip[1]]
    return pl.pallas_call(
        body, name=name, grid=(width // CONV_CB, nt),
        in_specs=in_specs,
        out_specs=[pl.BlockSpec((tt, CONV_CB), rev), pl.BlockSpec((8, CONV_CB), lambda c, i: (0, c)),
                   pl.BlockSpec((1, CONV_CB), lambda c, i: (0, c))],
        out_shape=[jax.ShapeDtypeStruct((t, width), BF16), jax.ShapeDtypeStruct((8, width), F32),
                   jax.ShapeDtypeStruct((1, width), F32)],
        scratch_shapes=[pltpu.VMEM((tt + 8, CONV_CB), F32), pltpu.VMEM((tt + 8, CONV_CB), F32)],
        compiler_params=_params(("parallel", "arbitrary"), 32 << 20),
    )(*args)


GW = SSD_R * SSD_P


def _expand8(v, passes=2):
    r = v.shape[0]
    if r < 8:
        v = jnp.broadcast_to(v, (8, SSD_R))
    ri = lax.broadcasted_iota(jnp.int32, (SSD_R, GW), 0)
    ci = lax.broadcasted_iota(jnp.int32, (SSD_R, GW), 1)
    spread = jnp.where((ci >= ri * SSD_P) & (ci < (ri + 1) * SSD_P), 1.0, 0.0)
    return _dot01(v, spread, passes)[:r]


def _head_pair_split(tile):
    first = lax.broadcasted_iota(jnp.int32, (1, 2 * SSD_P), 1) < SSD_P
    return jnp.where(first, tile, 0.0), jnp.where(first, 0.0, tile)


def _sel(rows, group):
    ri = lax.broadcasted_iota(jnp.int32, (rows, rows // group), 0)
    ci = lax.broadcasted_iota(jnp.int32, (rows, rows // group), 1)
    lo = ci * group
    return jnp.where((ri >= lo) & (ri < lo + group), 1.0, 0.0).astype(F32)


def _dot01(lhs, rhs, passes, split_lhs=True, dims=((1,), (0,))):
    val, m01 = (lhs, rhs) if split_lhs else (rhs, lhs)
    m01 = m01.astype(BF16)
    out = None
    for p in range(passes):
        piece = val.astype(BF16)
        ops = (piece, m01) if split_lhs else (m01, piece)
        d = lax.dot_general(ops[0], ops[1], (dims, ((), ())), preferred_element_type=F32)
        out = d if out is None else out + d
        if p + 1 < passes:
            val = val - piece.astype(F32)
    return out


def _ssd_chunk_terms(adt):
    li = lax.broadcasted_iota(jnp.int32, (CHUNK, CHUNK), 0)
    si = lax.broadcasted_iota(jnp.int32, (CHUNK, CHUNK), 1)
    causal = li >= si
    a_cs = _dot01(jnp.where(causal, 1.0, 0.0), adt, 3, split_lhs=False)
    a_cs_t = _dot01(adt, jnp.where(li <= si, 1.0, 0.0), 3, dims=((0,), (0,)))
    return a_cs, a_cs_t, causal


def _ssd_fwd(xc, dt3, adt3):
    t = xc.shape[0]
    nc = t // CHUNK

    gs = 2

    def body(xs_ref, b_ref, c_ref, dt_ref, adt_ref, y_ref, hp_ref, h_ref):
        @pl.when(pl.program_id(1) == 0)
        def _():
            h_ref[...] = jnp.zeros_like(h_ref)

        for gg in range(gs):
            a_cs, a_cs_t, causal = _ssd_chunk_terms(adt_ref[gg])
            a_last = a_cs[CHUNK - 1:CHUNK, :]
            h = h_ref[gg]
            hp_ref[gg, 0] = h
            xd = xs_ref[:, GW * gg:GW * (gg + 1)] * _expand8(dt_ref[gg])
            bb = b_ref[:, SSD_N * gg:SSD_N * (gg + 1)].astype(BF16)
            cbf = c_ref[:, SSD_N * gg:SSD_N * (gg + 1)].astype(BF16)
            cb = lax.dot_general(cbf, bb, (((1,), (1,)), ((), ())), preferred_element_type=F32)
            yoff = jnp.dot(cbf, h.astype(BF16), preferred_element_type=F32) * _expand8(jnp.exp(a_cs))
            for q in range(SSD_R // 2):
                lmats = []
                for r in (2 * q, 2 * q + 1):
                    seg = jnp.exp(jnp.where(causal, a_cs[:, r:r + 1] - a_cs_t[r:r + 1, :], -jnp.inf))
                    lmats.append((cb * seg).astype(BF16))
                tile = slice(2 * SSD_P * q, 2 * SSD_P * (q + 1))
                xa, xb = _head_pair_split(xd[:, tile])
                y_ref[:, GW * gg + 2 * SSD_P * q:GW * gg + 2 * SSD_P * (q + 1)] = (
                    jnp.dot(jnp.concatenate(lmats, axis=1), jnp.concatenate([xa, xb], axis=0).astype(BF16),
                            preferred_element_type=F32) + yoff[:, tile])
            xdd = (xd * _expand8(jnp.exp(a_last - a_cs))).astype(BF16)
            h_ref[gg] = _expand8(jnp.exp(a_last), 3) * h + lax.dot_general(
                bb, xdd, (((0,), (0,)), ((), ())), preferred_element_type=F32)

    nb = SSD_INNER // (gs * SSD_N)
    return pl.pallas_call(
        body, name="ssd_fwd", grid=(SSD_G // gs, nc),
        in_specs=[pl.BlockSpec((CHUNK, gs * GW), lambda g, c: (c, g)),
                  pl.BlockSpec((CHUNK, gs * SSD_N), lambda g, c: (c, nb + g)),
                  pl.BlockSpec((CHUNK, gs * SSD_N), lambda g, c: (c, nb + SSD_G // gs + g)),
                  pl.BlockSpec((gs, CHUNK, SSD_R), lambda g, c: (g, c, 0)),
                  pl.BlockSpec((gs, CHUNK, SSD_R), lambda g, c: (g, c, 0))],
        out_specs=[pl.BlockSpec((CHUNK, gs * GW), lambda g, c: (c, g)),
                   pl.BlockSpec((gs, 1, SSD_N, GW), lambda g, c: (g, c, 0, 0))],
        out_shape=[jax.ShapeDtypeStruct((t, SSD_INNER), F32), jax.ShapeDtypeStruct((SSD_G, nc, SSD_N, GW), F32)],
        scratch_shapes=[pltpu.VMEM((gs, SSD_N, GW), F32)],
        compiler_params=_params(("parallel", "arbitrary"), 32 << 20),
    )(xc, xc, xc, dt3, adt3)


def _ssd_bwd(xc, dt3, adt3, hprev, dy):
    t = xc.shape[0]
    nc = t // CHUNK

    def body(xs_ref, b_ref, c_ref, dt_ref, adt_ref, hp_ref, dy_ref,
             dx_ref, db_ref, dc_ref, dadt_ref, dxdx_ref, dh_ref):
        @pl.when(pl.program_id(1) == 0)
        def _():
            dh_ref[...] = jnp.zeros_like(dh_ref)

        a_cs, a_cs_t, causal = _ssd_chunk_terms(adt_ref[0])
        a_last = a_cs[CHUNK - 1:CHUNK, :]
        e_last = jnp.exp(a_last)
        ex = _expand8(jnp.exp(a_cs))
        dtex = _expand8(jnp.exp(a_last - a_cs))
        dtx = _expand8(dt_ref[0])
        xs = xs_ref[...]
        dyv = dy_ref[...]
        hp = hp_ref[0, 0]
        dh = dh_ref[...]
        sel = _sel(GW, SSD_P)
        seg8 = lambda v: _dot01(v, sel, 2)

        xd = xs * dtx
        xdd = xd * dtex
        bb = b_ref[...].astype(BF16)
        cbf = c_ref[...].astype(BF16)
        hpb = hp.astype(BF16)
        dhb = dh.astype(BF16)
        xdb = xd.astype(BF16)
        cb = lax.dot_general(cbf, bb, (((1,), (1,)), ((), ())), preferred_element_type=F32)
        dye = (dyv * ex).astype(BF16)
        yoff = jnp.dot(cbf, hpb, preferred_element_type=F32) * ex
        dc = lax.dot_general(dye, hpb, (((1,), (1,)), ((), ())), preferred_element_type=F32)
        d_a = seg8(dyv * yoff)
        bdh = jnp.dot(bb, dhb, preferred_element_type=F32)
        db = lax.dot_general(xdd.astype(BF16), dhb, (((1,), (1,)), ((), ())), preferred_element_type=F32)
        dxd_state = bdh * dtex
        q = seg8(xdd * bdh)
        d_a = d_a - q
        d_a_last = (jnp.sum(q, axis=0, keepdims=True)
                    + e_last * seg8(jnp.sum(hp * dh, axis=0, keepdims=True)))
        dh_ref[...] = (lax.dot_general(cbf, dye, (((0,), (0,)), ((), ())), preferred_element_type=F32)
                       + _expand8(e_last, 3) * dh)
        dcb = jnp.zeros((CHUNK, CHUNK), F32)
        w_all = []
        dxd_parts = []
        for q2 in range(SSD_R // 2):
            tile = slice(2 * SSD_P * q2, 2 * SSD_P * (q2 + 1))
            dy_pair = [part.astype(BF16) for part in _head_pair_split(dyv[:, tile])]
            lmats = []
            for k, r in enumerate((2 * q2, 2 * q2 + 1)):
                seg = jnp.exp(jnp.where(causal, a_cs[:, r:r + 1] - a_cs_t[r:r + 1, :], -jnp.inf))
                lmat = cb * seg
                dm = lax.dot_general(dy_pair[k], xdb[:, tile], (((1,), (1,)), ((), ())), preferred_element_type=F32)
                dcb = dcb + dm * seg
                w_all.append(dm * lmat)
                lmats.append(lmat.astype(BF16))
            dxd_parts.append(lax.dot_general(jnp.concatenate(lmats, axis=0), jnp.concatenate(dy_pair, axis=0),
                                             (((0,), (0,)), ((), ())), preferred_element_type=F32))
        row_sums = _dot01(jnp.concatenate(w_all, axis=1), _sel(SSD_R * CHUNK, CHUNK), 2)
        cs_rows = jnp.concatenate([jnp.sum(wr, axis=0, keepdims=True) for wr in w_all], axis=0)
        col_sums = _dot01(cs_rows, _sel(SSD_R, 1), 3, dims=((0,), (0,)))
        d_a = d_a + row_sums - col_sums
        li = lax.broadcasted_iota(jnp.int32, (CHUNK, SSD_R), 0)
        d_a = d_a + jnp.where(li == CHUNK - 1, d_a_last, 0.0)
        l2 = lax.broadcasted_iota(jnp.int32, (CHUNK, CHUNK), 0)
        s2 = lax.broadcasted_iota(jnp.int32, (CHUNK, CHUNK), 1)
        dadt_ref[0] = _dot01(jnp.where(s2 >= l2, 1.0, 0.0), d_a, 3, split_lhs=False)
        dxd = dxd_state + jnp.concatenate(dxd_parts, axis=1)
        dxdx_ref[0] = seg8(dxd * xs)
        dx_ref[...] = dxd * dtx
        dcbb = dcb.astype(BF16)
        db_ref[...] = db + lax.dot_general(dcbb, cbf, (((0,), (0,)), ((), ())), preferred_element_type=F32)
        dc_ref[...] = dc + jnp.dot(dcbb, bb, preferred_element_type=F32)

    nb = SSD_INNER // SSD_N
    rc = lambda g, c: (nc - 1 - c, g)
    r3 = lambda g, c: (g, nc - 1 - c, 0)
    return pl.pallas_call(
        body, name="ssd_bwd", grid=(SSD_G, nc),
        in_specs=[pl.BlockSpec((CHUNK, GW), rc),
                  pl.BlockSpec((CHUNK, SSD_N), lambda g, c: (nc - 1 - c, nb + g)),
                  pl.BlockSpec((CHUNK, SSD_N), lambda g, c: (nc - 1 - c, nb + SSD_G + g)),
                  pl.BlockSpec((1, CHUNK, SSD_R), r3),
                  pl.BlockSpec((1, CHUNK, SSD_R), r3),
                  pl.BlockSpec((1, 1, SSD_N, GW), lambda g, c: (g, nc - 1 - c, 0, 0)),
                  pl.BlockSpec((CHUNK, GW), rc)],
        out_specs=[pl.BlockSpec((CHUNK, GW), rc),
                   pl.BlockSpec((CHUNK, SSD_N), rc),
                   pl.BlockSpec((CHUNK, SSD_N), rc),
                   pl.BlockSpec((1, CHUNK, SSD_R), r3),
                   pl.BlockSpec((1, CHUNK, SSD_R), r3)],
        out_shape=[jax.ShapeDtypeStruct((t, SSD_INNER), F32),
                   jax.ShapeDtypeStruct((t, SSD_G * SSD_N), F32),
                   jax.ShapeDtypeStruct((t, SSD_G * SSD_N), F32),
                   jax.ShapeDtypeStruct((SSD_G, t, SSD_R), F32),
                   jax.ShapeDtypeStruct((SSD_G, t, SSD_R), F32)],
        scratch_shapes=[pltpu.VMEM((SSD_N, GW), F32)],
        compiler_params=_params(("parallel", "arbitrary"), 40 << 20),
    )(xc, xc, xc, dt3, adt3, hprev, dy)


def _gated_norm_fwd(y, xc, proj, dexp, ng):
    t = y.shape[0]
    tt = _divtile(t, 256)

    def body(y_ref, x_ref, z_ref, d_ref, g_ref, o_ref):
        z = z_ref[...]
        y2 = (y_ref[...] + d_ref[...] * x_ref[...]) * (z * _sigmoid(z))
        for gi in range(SSD_G):
            sl = slice(GW * gi, GW * (gi + 1))
            seg = y2[:, sl]
            rinv = lax.rsqrt(jnp.mean(seg * seg, axis=-1, keepdims=True) + RMS_EPS)
            o_ref[:, sl] = (seg * rinv * g_ref[:, sl]).astype(BF16)

    row = pl.BlockSpec((tt, SSD_INNER), lambda i: (i, 0))
    vec = pl.BlockSpec((1, SSD_INNER), lambda i: (0, 0))
    return pl.pallas_call(
        body, name="gated_norm_fwd", grid=(t // tt,), in_specs=[row, row, row, vec, vec], out_specs=row,
        out_shape=jax.ShapeDtypeStruct((t, SSD_INNER), BF16),
        compiler_params=_params(("parallel",), 48 << 20),
    )(y, xc, proj, dexp, ng)


def _gated_norm_bwd(dout, y, xc, proj, dexp, ng):
    t = y.shape[0]
    tt = _divtile(t, 128)

    def body(do_ref, y_ref, x_ref, z_ref, d_ref, g_ref, dz_ref, dy_ref, dg_ref, dd_ref):
        z = z_ref[...]
        sg = _sigmoid(z)
        sz = z * sg
        xs = x_ref[...]
        y1 = y_ref[...] + d_ref[...] * xs
        y2 = y1 * sz
        dov = do_ref[...]

        @pl.when(pl.program_id(0) == 0)
        def _():
            dg_ref[...] = jnp.zeros_like(dg_ref)
            dd_ref[...] = jnp.zeros_like(dd_ref)

        for gi in range(SSD_G):
            sl = slice(GW * gi, GW * (gi + 1))
            seg = y2[:, sl]
            rinv = lax.rsqrt(jnp.mean(seg * seg, axis=-1, keepdims=True) + RMS_EPS)
            yn = seg * rinv
            dsl = dov[:, sl]
            dg_ref[:, sl] += jnp.sum(dsl * yn, axis=0, keepdims=True)
            dyn = dsl * g_ref[:, sl]
            dy2 = rinv * (dyn - yn * jnp.mean(dyn * yn, axis=-1, keepdims=True))
            dz_ref[:, sl] = (dy2 * y1[:, sl] * (sg[:, sl] * (1.0 + z[:, sl] * (1.0 - sg[:, sl])))).astype(BF16)
            dy1 = dy2 * sz[:, sl]
            dy_ref[:, sl] = dy1
            dd_ref[:, sl] += jnp.sum(dy1 * xs[:, sl], axis=0, keepdims=True)

    row = pl.BlockSpec((tt, SSD_INNER), lambda i: (i, 0))
    vec = pl.BlockSpec((1, SSD_INNER), lambda i: (0, 0))
    return pl.pallas_call(
        body, name="gated_norm_bwd", grid=(t // tt,), in_specs=[row, row, row, row, vec, vec],
        out_specs=[row, row, vec, vec],
        out_shape=[jax.ShapeDtypeStruct((t, SSD_INNER), BF16), jax.ShapeDtypeStruct((t, SSD_INNER), F32),
                   jax.ShapeDtypeStruct((1, SSD_INNER), F32), jax.ShapeDtypeStruct((1, SSD_INNER), F32)],
        compiler_params=_params(("arbitrary",), 48 << 20),
    )(dout, y, xc, proj, dexp, ng)


def _fold_heads(v, name):
    def body(v_ref, o_ref):
        ri = lax.broadcasted_iota(jnp.int32, (SSD_INNER, 128), 0)
        ci = lax.broadcasted_iota(jnp.int32, (SSD_INNER, 128), 1)
        fold = jnp.where((ri >= ci * SSD_P) & (ri < (ci + 1) * SSD_P), 1.0, 0.0).astype(F32)
        o_ref[...] = jnp.dot(v_ref[...], fold, preferred_element_type=F32, precision=HI)

    return pl.pallas_call(body, name=name, out_shape=jax.ShapeDtypeStruct((1, 128), F32))(v)


Q_BLK = SEGS["q"][2] // QW
K_BLK = SEGS["k"][2] // KVW
V_BLK = SEGS["v"][2] // KVW


def _rope_tables(pos_ref, invf_ref, width):
    ang = pos_ref[...] * invf_ref[...]
    lane = lax.broadcasted_iota(jnp.int32, (1, 128), 1)
    sign = jnp.where((lane % HD) < (HD // 2), -1.0, 1.0)
    cos = jnp.tile(jnp.cos(ang), (1, width // 128))
    sin = jnp.tile(sign * jnp.sin(ang), (1, width // 128))
    first = (lax.broadcasted_iota(jnp.int32, (1, width), 1) % HD) < (HD // 2)
    return cos, sin, first


def _rot_half(u, first):
    w = u.shape[1]
    return jnp.where(first, pltpu.roll(u, w - HD // 2, 1), pltpu.roll(u, HD // 2, 1))


def _rope_fwd(proj, pos, invf):
    t = proj.shape[0]
    tt = _divtile(t, 512)

    def body(q_ref, k_ref, pos_ref, invf_ref, qo_ref, ko_ref):
        cos, sin, first = _rope_tables(pos_ref, invf_ref, QW)
        q = q_ref[...]
        qo_ref[...] = (q * cos + _rot_half(q, first) * sin).astype(BF16)
        k = k_ref[...]
        ko_ref[...] = (k * cos[:, :KVW] + _rot_half(k, first[:, :KVW]) * sin[:, :KVW]).astype(BF16)

    return pl.pallas_call(
        body, name="rope_fwd", grid=(t // tt,),
        in_specs=[pl.BlockSpec((tt, QW), lambda i: (i, Q_BLK)), pl.BlockSpec((tt, KVW), lambda i: (i, K_BLK)),
                  pl.BlockSpec((tt, 1), lambda i: (i, 0)), pl.BlockSpec((1, 128), lambda i: (0, 0))],
        out_specs=[pl.BlockSpec((tt, QW), lambda i: (i, 0)), pl.BlockSpec((tt, KVW), lambda i: (i, 0))],
        out_shape=[jax.ShapeDtypeStruct((t, QW), BF16), jax.ShapeDtypeStruct((t, KVW), BF16)],
        compiler_params=_params(("parallel",), 40 << 20),
    )(proj, proj, pos, invf)


def _rope_bwd(dq, dk, pos, invf):
    t = dq.shape[0]
    tt = _divtile(t, 512)

    def body(dq_ref, dk_ref, pos_ref, invf_ref, qo_ref, ko_ref):
        cos, sin, first = _rope_tables(pos_ref, invf_ref, QW)
        q = dq_ref[...]
        qo_ref[...] = (q * cos + _rot_half(q * sin, first)).astype(BF16)
        k = dk_ref[...]
        ko_ref[...] = (k * cos[:, :KVW] + _rot_half(k * sin[:, :KVW], first[:, :KVW])).astype(BF16)

    return pl.pallas_call(
        body, name="rope_bwd", grid=(t // tt,),
        in_specs=[pl.BlockSpec((tt, QW), lambda i: (i, 0)), pl.BlockSpec((tt, KVW), lambda i: (i, 0)),
                  pl.BlockSpec((tt, 1), lambda i: (i, 0)), pl.BlockSpec((1, 128), lambda i: (0, 0))],
        out_specs=[pl.BlockSpec((tt, QW), lambda i: (i, 0)), pl.BlockSpec((tt, KVW), lambda i: (i, 0))],
        out_shape=[jax.ShapeDtypeStruct((t, QW), BF16), jax.ShapeDtypeStruct((t, KVW), BF16)],
        compiler_params=_params(("parallel",), 40 << 20),
    )(dq, dk, pos, invf)


GQ = NQ // NKV
NT_DIMS = (((1,), (1,)), ((), ()))
TN_DIMS = (((0,), (0,)), ((), ()))


def _attn_stack(ref, j, dtype=None):
    parts = [ref[:, HD * h:HD * (h + 1)] for h in range(j * GQ, (j + 1) * GQ)]
    out = jnp.concatenate(parts, axis=0)
    return out if dtype is None else out.astype(dtype)


def _attn_sink_col(s_ref, j):
    return jnp.concatenate([jnp.broadcast_to(s_ref[:, h:h + 1], (WINDOW, 1)) for h in range(j * GQ, (j + 1) * GQ)],
                           axis=0)


def _attn_mask(n):
    qi = lax.broadcasted_iota(jnp.int32, (GQ * WINDOW, 2 * WINDOW), 0) % WINDOW
    kj = lax.broadcasted_iota(jnp.int32, (GQ * WINDOW, 2 * WINDOW), 1)
    return (kj > qi) & (kj <= qi + WINDOW) & ((n > 0) | (kj >= WINDOW))


def _attn_exp(qg, kk, sink, mask):
    s = jnp.where(mask, lax.dot_general(qg, kk, NT_DIMS, preferred_element_type=F32) * (HD ** -0.5), -jnp.inf)
    m = jnp.maximum(jnp.max(s, axis=-1, keepdims=True), sink)
    return jnp.exp(s - m), jnp.exp(sink - m)


def _attn_fwd(qr, kr, proj, sinks):
    t = qr.shape[0]
    nb = t // WINDOW

    def body(q_ref, kc_ref, kp_ref, vc_ref, vp_ref, s_ref, o_ref):
        mask = _attn_mask(pl.program_id(0))
        ones = jnp.ones((2 * WINDOW, HD), BF16)
        for j in range(NKV):
            ks = slice(HD * j, HD * (j + 1))
            kk = jnp.concatenate([kp_ref[:, ks], kc_ref[:, ks]], axis=0)
            vv = jnp.concatenate([vp_ref[:, ks], vc_ref[:, ks]], axis=0).astype(BF16)
            p, ps = _attn_exp(_attn_stack(q_ref, j), kk, _attn_sink_col(s_ref, j), mask)
            oa = jnp.dot(p.astype(BF16), jnp.concatenate([vv, ones], axis=1), preferred_element_type=F32)
            o = (oa[:, :HD] * (1.0 / (oa[:, HD:HD + 1] + ps))).astype(BF16)
            for g in range(GQ):
                h = j * GQ + g
                o_ref[:, HD * h:HD * (h + 1)] = o[WINDOW * g:WINDOW * (g + 1)]

    prev = lambda n: (jnp.maximum(n - 1, 0), 0)
    return pl.pallas_call(
        body, name="attn_fwd", grid=(nb,),
        in_specs=[pl.BlockSpec((WINDOW, QW), lambda n: (n, 0)),
                  pl.BlockSpec((WINDOW, KVW), lambda n: (n, 0)), pl.BlockSpec((WINDOW, KVW), prev),
                  pl.BlockSpec((WINDOW, KVW), lambda n: (n, V_BLK)),
                  pl.BlockSpec((WINDOW, KVW), lambda n: (jnp.maximum(n - 1, 0), V_BLK)),
                  pl.BlockSpec((1, 128), lambda n: (0, 0))],
        out_specs=pl.BlockSpec((WINDOW, QW), lambda n: (n, 0)),
        out_shape=jax.ShapeDtypeStruct((t, QW), BF16),
        compiler_params=_params(("parallel",), 24 << 20),
    )(qr, kr, kr, proj, proj, sinks)


def _attn_bwd(qr, kr, proj, sinks, do):
    t = qr.shape[0]
    nb = t // WINDOW

    def body(q_ref, kc_ref, kp_ref, vc_ref, vp_ref, s_ref, do_ref,
             dq_ref, dk_ref, dv_ref, ds_ref, dkc_ref, dvc_ref):
        i = pl.program_id(0)
        mask = _attn_mask(nb - 1 - i)

        @pl.when(i == 0)
        def _():
            dkc_ref[...] = jnp.zeros_like(dkc_ref)
            dvc_ref[...] = jnp.zeros_like(dvc_ref)
            ds_ref[...] = jnp.zeros_like(ds_ref)

        lane = lax.broadcasted_iota(jnp.int32, (1, 128), 1)
        ds_acc = jnp.zeros((1, 128), F32)
        for j in range(NKV):
            ks = slice(HD * j, HD * (j + 1))
            kk = jnp.concatenate([kp_ref[:, ks], kc_ref[:, ks]], axis=0)
            vv = jnp.concatenate([vp_ref[:, ks], vc_ref[:, ks]], axis=0).astype(BF16)
            qg = _attn_stack(q_ref, j)
            p, ps = _attn_exp(qg, kk, _attn_sink_col(s_ref, j), mask)
            inv = 1.0 / (jnp.sum(p, axis=-1, keepdims=True) + ps)
            pn = p * inv
            dog = _attn_stack(do_ref, j, BF16)
            dp = lax.dot_general(dog, vv, NT_DIMS, preferred_element_type=F32)
            delta = jnp.sum(dp * pn, axis=-1, keepdims=True)
            dsb = (pn * (dp - delta) * (HD ** -0.5)).astype(BF16)
            dsink = -(ps * inv) * delta
            dq = jnp.dot(dsb, kk, preferred_element_type=F32)
            for g in range(GQ):
                h = j * GQ + g
                rows = slice(WINDOW * g, WINDOW * (g + 1))
                dq_ref[:, HD * h:HD * (h + 1)] = dq[rows]
                ds_acc = ds_acc + jnp.where(lane == h, jnp.sum(dsink[rows], axis=0, keepdims=True), 0.0)
            dkk = lax.dot_general(dsb, qg, TN_DIMS, preferred_element_type=F32)
            dvv = lax.dot_general(pn.astype(BF16), dog, TN_DIMS, preferred_element_type=F32)
            dk_ref[:, ks] = dkk[WINDOW:] + dkc_ref[:, ks]
            dv_ref[:, ks] = (dvv[WINDOW:] + dvc_ref[:, ks]).astype(BF16)
            dkc_ref[:, ks] = dkk[:WINDOW]
            dvc_ref[:, ks] = dvv[:WINDOW]
        ds_ref[...] += ds_acc

    cur = lambda i: (nb - 1 - i, 0)
    prev = lambda i: (jnp.maximum(nb - 2 - i, 0), 0)
    return pl.pallas_call(
        body, name="attn_bwd", grid=(nb,),
        in_specs=[pl.BlockSpec((WINDOW, QW), cur),
                  pl.BlockSpec((WINDOW, KVW), cur), pl.BlockSpec((WINDOW, KVW), prev),
                  pl.BlockSpec((WINDOW, KVW), lambda i: (nb - 1 - i, V_BLK)),
                  pl.BlockSpec((WINDOW, KVW), lambda i: (jnp.maximum(nb - 2 - i, 0), V_BLK)),
                  pl.BlockSpec((1, 128), lambda i: (0, 0)),
                  pl.BlockSpec((WINDOW, QW), cur)],
        out_specs=[pl.BlockSpec((WINDOW, QW), cur), pl.BlockSpec((WINDOW, KVW), cur),
                   pl.BlockSpec((WINDOW, KVW), cur), pl.BlockSpec((1, 128), lambda i: (0, 0))],
        out_shape=[jax.ShapeDtypeStruct((t, QW), F32), jax.ShapeDtypeStruct((t, KVW), F32),
                   jax.ShapeDtypeStruct((t, KVW), BF16), jax.ShapeDtypeStruct((1, 128), F32)],
        scratch_shapes=[pltpu.VMEM((WINDOW, KVW), F32), pltpu.VMEM((WINDOW, KVW), F32)],
        compiler_params=_params(("arbitrary",), 32 << 20),
    )(qr, kr, kr, proj, proj, sinks, do)


GS_BLK = SEGS["gs"][2] // D
GA_BLK = SEGS["ga"][2] // D


def _merge_fwd(ys, ya, proj):
    t = ys.shape[0]
    tt = _divtile(t, 256)

    def body(ys_ref, ya_ref, gs_ref, ga_ref, o_ref):
        o_ref[...] = (_sigmoid(gs_ref[...]) * ys_ref[...] + _sigmoid(ga_ref[...]) * ya_ref[...]).astype(BF16)

    row = pl.BlockSpec((tt, D), lambda i: (i, 0))
    return pl.pallas_call(
        body, name="merge_fwd", grid=(t // tt,),
        in_specs=[row, row, pl.BlockSpec((tt, D), lambda i: (i, GS_BLK)), pl.BlockSpec((tt, D), lambda i: (i, GA_BLK))],
        out_specs=row, out_shape=jax.ShapeDtypeStruct((t, D), BF16),
        compiler_params=_params(("parallel",), 32 << 20),
    )(ys, ya, proj, proj)


def _merge_bwd(dm, ys, ya, proj):
    t = ys.shape[0]
    tt = _divtile(t, 256)

    def body(dm_ref, ys_ref, ya_ref, gs_ref, ga_ref, dys_ref, dya_ref, dgs_ref, dga_ref):
        d = dm_ref[...]
        s = _sigmoid(gs_ref[...])
        a = _sigmoid(ga_ref[...])
        dys_ref[...] = (d * s).astype(BF16)
        dya_ref[...] = (d * a).astype(BF16)
        dgs_ref[...] = (d * ys_ref[...] * (s * (1.0 - s))).astype(BF16)
        dga_ref[...] = (d * ya_ref[...] * (a * (1.0 - a))).astype(BF16)

    row = pl.BlockSpec((tt, D), lambda i: (i, 0))
    return pl.pallas_call(
        body, name="merge_bwd", grid=(t // tt,),
        in_specs=[row, row, row, pl.BlockSpec((tt, D), lambda i: (i, GS_BLK)),
                  pl.BlockSpec((tt, D), lambda i: (i, GA_BLK))],
        out_specs=[row, row, row, row], out_shape=[jax.ShapeDtypeStruct((t, D), BF16)] * 4,
        compiler_params=_params(("parallel",), 40 << 20),
    )(dm, ys, ya, proj, proj)


def _pad128(v):
    return jnp.pad(v, ((0, 0), (0, 128 - v.shape[1])))


def _group_major(v):
    t = v.shape[0]
    return jnp.transpose(v[:, :SSD_HEADS].reshape(t, SSD_G, SSD_R), (1, 0, 2))


def _token_major(v3):
    t = v3.shape[1]
    return _pad128(jnp.transpose(v3, (1, 0, 2)).reshape(t, SSD_HEADS))


def _local_step(x, pos, target, w, small, fetch=None, early_grads=None):
    w = dict(w)
    xb = x.astype(BF16)
    gu1, a1 = _mm_swiglu(xb, w["gu1"], "ffn1_gu")
    if fetch is not None:
        w.update(fetch(1, a1))
    f1 = _mm(a1, w["d1"], "nn", F32, "ffn1_down", caps=(512, 1024, FFN_H))
    h1, h1b, xh1, rs1 = _ln_fwd(x, f1, small["ln1_g"], small["ln1_b"], 0.5, "ln1_fwd")
    if fetch is not None:
        w.update(fetch(2, h1b))
    proj = _mm(h1b, w["win"], "nn", F32, "proj", caps=(1024, 896, 2048))
    if fetch is not None:
        w.update(fetch(3, proj))
    bias128 = _pad128(small["dt_bias"])
    alog128 = _pad128(small["a_log"])
    dt, adt = _dt_prep(proj, bias128, alog128)
    dt3, adt3 = _group_major(dt), _group_major(adt)
    xc = _conv_fwd(proj, small["conv_w"], small["conv_b"])
    y_ssd, hprev = _ssd_fwd(xc, dt3, adt3)
    dexp = jnp.repeat(small["d_skip"], SSD_P, axis=1)
    ysn = _gated_norm_fwd(y_ssd, xc, proj, dexp, small["ssd_norm_g"])
    ys = _mm(ysn, w["so"], "nn", F32, "ssd_out")
    invf = jnp.tile(ROPE_THETA ** (-jnp.arange(HD // 2, dtype=F32) * 2.0 / HD), 4)[None, :]
    qr, kr = _rope_fwd(proj, pos, invf)
    sinks128 = _pad128(small["attn_sinks"])
    o = _attn_fwd(qr, kr, proj, sinks128)
    ya = _mm(o, w["ao"], "nn", F32, "attn_out")
    mg = _merge_fwd(ys, ya, proj)
    mix = _mm(mg, w["out"], "nn", F32, "mix_out")
    h2, h2b, xh2, rs2 = _ln_fwd(h1, mix, small["ln2_g"], small["ln2_b"], 1.0, "ln2_fwd")
    gu2, a2 = _mm_swiglu(h2b, w["gu2"], "ffn2_gu")
    f2 = _mm(a2, w["d2"], "nn", F32, "ffn2_down", caps=(512, 1024, FFN_H))
    _, _, xh3, rs3, dh3, loss = _ln_fwd(h2, f2, small["ln3_g"], small["ln3_b"], 0.5, "ln3_fwd", target=target)

    gw, gs = {}, {}
    dr3, dr3h, gs["ln3_g"], gs["ln3_b"] = _ln_bwd(dh3, xh3, rs3, small["ln3_g"], 0.5, "ln3_bwd")
    gw["d2"] = _mm(a2, dr3h, "tn", F32, "ffn2_down_dw")
    da2 = _mm(dr3h, w["d2"], "nt", BF16, "ffn2_down_dx", caps=(1024, 1408, 2048))
    dgu2 = _swiglu_bwd(gu2, da2, "ffn2_act_bwd")
    gw["gu2"] = _mm(h2b, dgu2, "tn", F32, "ffn2_gu_dw", caps=(1024, 1408, 2048), n_slabs=N_CHIPS)
    dh2 = _mm(dgu2, w["gu2"], "nt", F32, "ffn2_gu_dx", add=dr3, add_scale=ALPHA, caps=(1024, 1024, 2816))
    dr2, dr2b, gs["ln2_g"], gs["ln2_b"] = _ln_bwd(dh2, xh2, rs2, small["ln2_g"], 1.0, "ln2_bwd")
    gw["out"] = _mm(mg, dr2b, "tn", F32, "mix_out_dw")
    dmg = _mm(dr2b, w["out"], "nt", F32, "mix_out_dx")
    dys, dya, dgs, dga = _merge_bwd(dmg, ys, ya, proj)
    gw["ao"] = _mm(o, dya, "tn", F32, "attn_out_dw")
    do = _mm(dya, w["ao"], "nt", BF16, "attn_out_dx")
    dqr, dkr, dv, gs["attn_sinks"] = _attn_bwd(qr, kr, proj, sinks128, do)
    dq, dk = _rope_bwd(dqr, dkr, pos, invf)
    gw["so"] = _mm(ysn, dys, "tn", F32, "ssd_out_dw")
    dysn = _mm(dys, w["so"], "nt", F32, "ssd_out_dx")
    dz, dy1, gs["ssd_norm_g"], dd_ch = _gated_norm_bwd(dysn, y_ssd, xc, proj, dexp, small["ssd_norm_g"])
    gs["d_skip"] = _fold_heads(dd_ch, "d_skip_fold")
    dxs, db, dc, dadt3, dxdx3 = _ssd_bwd(xc, dt3, adt3, hprev, dy1)
    ddt, gs["dt_bias"], gs["a_log"] = _dt_bwd(_token_major(dadt3), _token_major(dxdx3), proj, bias128, alog128)
    cw, cbias = small["conv_w"], small["conv_b"]
    dux, dwx, dbx = _conv_bwd(proj, dxs, cw, cbias, 0, SSD_INNER, "conv_bwd_x", skip=(dy1, dexp))
    dub, dwb, dbb = _conv_bwd(proj, db, cw, cbias, SSD_INNER, SSD_G * SSD_N, "conv_bwd_b")
    duc, dwc, dbc = _conv_bwd(proj, dc, cw, cbias, SSD_INNER + SSD_G * SSD_N, SSD_G * SSD_N, "conv_bwd_c")
    gs["conv_w"] = jnp.concatenate([dwx[:4], dwb[:4], dwc[:4]], axis=1)
    gs["conv_b"] = jnp.concatenate([dbx, dbb, dbc], axis=1)
    dproj = jnp.concatenate([dz, dq, dgs, dga, dux, dub, duc, dk, dv, ddt], axis=1)
    gw["win"] = _mm(h1b, dproj, "tn", F32, "proj_dw", caps=(1024, 896, 2048))
    win = w["win"] if early_grads is None else early_grads[0](gw, w["win"])
    dh1 = _mm(dproj, win, "nt", F32, "proj_dx", add=dr2, add_scale=ALPHA, caps=(1024, 1024, 2432))
    ln1_g = small["ln1_g"]
    if early_grads is not None:
        ln1_g = ln1_g + early_grads[1](dh1)[0:1, 0:1]
    dr1, dr1h, gs["ln1_g"], gs["ln1_b"] = _ln_bwd(dh1, xh1, rs1, ln1_g, 0.5, "ln1_bwd")
    gw["d1"] = _mm(a1, dr1h, "tn", F32, "ffn1_down_dw")
    da1 = _mm(dr1h, w["d1"], "nt", BF16, "ffn1_down_dx", caps=(1024, 1408, 2048))
    dgu1 = _swiglu_bwd(gu1, da1, "ffn1_act_bwd")
    gw["gu1"] = _mm(xb, dgu1, "tn", F32, "ffn1_gu_dw", caps=(1024, 1408, 2048), n_slabs=N_CHIPS)
    grad_x = _mm(dgu1, w["gu1"], "nt", F32, "ffn1_gu_dx", add=dr1, add_scale=ALPHA, caps=(1024, 1024, 2816))
    return loss, grad_x, gw, gs


MESH = pl.DeviceIdType.MESH
ANY = pl.BlockSpec(memory_space=pl.ANY)


def _place():
    x, y, c = lax.axis_index("x"), lax.axis_index("y"), lax.axis_index("c")
    peers = [(1 - x, y), (x, 1 - y), (1 - x, 1 - y)]
    return x, y, c, peers


BIG = [
    ("ffn1_w_gate", D, SHARD_H, "gu1", "col", 0),
    ("ffn1_w_up", D, SHARD_H, "gu1", "col", SHARD_H),
    ("ffn1_w_down", SHARD_H, D, "d1", "row", 0),
    ("w_in", D, SHARD_IN, "win4", "lead", 0),
    ("w_ssd_o", SSD_INNER // N_CHIPS, D, "so", "row", 0),
    ("w_attn_o", D // N_CHIPS, D, "ao", "row", 0),
    ("w_out", D // N_CHIPS, D, "out", "row", 0),
    ("ffn2_w_gate", D, SHARD_H, "gu2", "col", 0),
    ("ffn2_w_up", D, SHARD_H, "gu2", "col", SHARD_H),
    ("ffn2_w_down", SHARD_H, D, "d2", "row", 0),
]
GATHERED = {"gu1": (D, 2 * FFN_H), "d1": (FFN_H, D), "win4": (N_CHIPS, D, SHARD_IN), "so": (SSD_INNER, D),
            "ao": (D, D), "out": (D, D), "gu2": (D, 2 * FFN_H), "d2": (FFN_H, D)}


def _cast_place(srcs, oname, chip_idx):
    rows, cols = srcs[0].shape
    tr = _divtile(rows, 256, 16)
    kind = [b[4] for b in BIG if b[3] == oname][0]

    def body(chip_ref, *refs):
        o_ref = refs[-1]
        for k, s_ref in enumerate(refs[:-1]):
            o_ref[:, k * cols:(k + 1) * cols] = s_ref[...].astype(BF16)

    nt = rows // tr
    if kind == "col":
        o_spec = pl.BlockSpec((tr, len(srcs) * cols), lambda i, chip_ref: (i, chip_ref[0]))
    elif kind == "row":
        o_spec = pl.BlockSpec((tr, cols), lambda i, chip_ref: (chip_ref[0] * nt + i, 0))
    else:
        o_spec = pl.BlockSpec((None, tr, cols), lambda i, chip_ref: (chip_ref[0], i, 0))
    return pl.pallas_call(
        body, name="cast_place_" + oname,
        grid_spec=pltpu.PrefetchScalarGridSpec(
            num_scalar_prefetch=1, grid=(nt,),
            in_specs=[pl.BlockSpec((tr, cols), lambda i, chip_ref: (i, 0))] * len(srcs), out_specs=o_spec),
        out_shape=jax.ShapeDtypeStruct(GATHERED[oname], BF16),
        compiler_params=_params(("parallel",), 32 << 20),
    )(chip_idx, *srcs)


def _slot(outs, entry, j, half):
    _, rows, cols, oname, kind, off = entry
    o = outs[oname]
    hr = rows // 2
    if kind == "col":
        cs = pl.ds(pl.multiple_of(j * (2 * SHARD_H) + off, 128), cols)
        return o.at[pl.ds(pl.multiple_of(half * hr, 16), hr), cs]
    if kind == "row":
        return o.at[pl.ds(pl.multiple_of(j * rows + half * hr, 16), hr), :]
    return o.at[j, pl.ds(pl.multiple_of(half * hr, 16), hr), :]


HBM = pl.BlockSpec(memory_space=pltpu.HBM)
SEM = pl.BlockSpec(memory_space=pltpu.SEMAPHORE)


def _ici_copy(outs, entry, j, c, to, send, recv, k):
    ref = _slot(outs, entry, j, c)
    return pltpu.make_async_remote_copy(src_ref=ref, dst_ref=ref, send_sem=send.at[k], recv_sem=recv.at[k],
                                        device_id=to, device_id_type=MESH)


GATHER_GROUPS = [["gu1"], ["d1"], ["win4"], ["so", "ao", "out", "gu2", "d2"]]


def _gather_ici_start(placed):
    names = [k for grp in GATHER_GROUPS for k in grp]
    bigs = [[b for b in BIG if b[3] in grp] for grp in GATHER_GROUPS]
    ng = len(GATHER_GROUPS)
    n_in = len(names)

    def body(*refs):
        sems = refs[n_in:n_in + 2 * ng]
        outs = dict(zip(names, refs[n_in + 2 * ng:n_in + 2 * ng + len(names)]))
        token = refs[-1]
        x, y, c, peers = _place()
        for gi, big in enumerate(bigs):
            for i, entry in enumerate(big):
                for k, (px, py) in enumerate(peers):
                    _ici_copy(outs, entry, 2 * x + y, c, (px, py, c), sems[2 * gi], sems[2 * gi + 1], 3 * i + k).start()
        token[...] = jnp.zeros_like(token)

    sem_shapes = [pltpu.SemaphoreType.DMA((3 * len(big),)) for big in bigs for _ in range(2)]
    res = pl.pallas_call(
        body, name="gather_ici_start",
        in_specs=[HBM] * n_in,
        out_specs=[SEM] * (2 * ng) + [HBM] * n_in + [pl.BlockSpec(memory_space=pltpu.VMEM)],
        out_shape=sem_shapes + [pltpu.HBM(GATHERED[k], BF16) for k in names] + [jax.ShapeDtypeStruct((8, 128), F32)],
        input_output_aliases={i: i + 2 * ng for i in range(n_in)},
        compiler_params=pltpu.CompilerParams(has_side_effects=pltpu.SideEffectType.DATAFLOW_SIDE_EFFECTING),
    )(*[pltpu.with_memory_space_constraint(placed[k], pltpu.HBM) for k in names])
    sems = [(res[2 * gi], res[2 * gi + 1]) for gi in range(ng)]
    return sems, dict(zip(names, res[2 * ng:2 * ng + n_in])), res[-1]


def _gather_ici_wait(send, recv, arrays, names, after, tag):
    big = [b for b in BIG if b[3] in names]

    def body(*refs):
        outs = dict(zip(names, refs[:len(names)]))
        send_ref, recv_ref = refs[len(names)], refs[len(names) + 1]
        x, y, c, peers = _place()
        for i, entry in enumerate(big):
            for k, (px, py) in enumerate(peers):
                mine = _ici_copy(outs, entry, 2 * x + y, c, (px, py, c), send_ref, recv_ref, 3 * i + k)
                mine.wait_send()
                theirs = _ici_copy(outs, entry, 2 * px + py, c, (px, py, c), send_ref, recv_ref, 3 * i + k)
                theirs.wait_recv()

    res = pl.pallas_call(
        body, name="gather_ici_wait_" + tag,
        in_specs=[HBM] * len(names) + [SEM, SEM, pl.BlockSpec(memory_space=pl.ANY)],
        out_specs=[HBM] * len(names),
        out_shape=[pltpu.HBM(GATHERED[k], BF16) for k in names],
        input_output_aliases={i: i for i in range(len(names))},
        compiler_params=pltpu.CompilerParams(has_side_effects=pltpu.SideEffectType.DATAFLOW_SIDE_EFFECTING),
    )(*[arrays[k] for k in names], send, recv, after)
    return dict(zip(names, res))


def _gather_d2d(arrays, names, tag):
    big = [b for b in BIG if b[3] in names]
    n = len(big)

    def body(*refs):
        outs = dict(zip(names, refs[len(names):2 * len(names)]))
        fsend, frecv = refs[2 * len(names):]
        x, y, c, peers = _place()
        cps = []
        for i, entry in enumerate(big):
            for k, (px, py) in enumerate(peers):
                cp = _ici_copy(outs, entry, 2 * px + py, c, (x, y, 1 - c), fsend, frecv, 3 * i + k)
                cp.start()
                cps.append(cp)
        for i, entry in enumerate(big):
            for k, (px, py) in enumerate(peers):
                _ici_copy(outs, entry, 2 * px + py, 1 - c, (x, y, 1 - c), fsend, frecv, 3 * i + k).wait_recv()
        for cp in cps:
            cp.wait_send()

    res = pl.pallas_call(
        body, name="gather_d2d_" + tag,
        in_specs=[ANY] * len(names), out_specs=[ANY] * len(names),
        out_shape=[jax.ShapeDtypeStruct(GATHERED[k], BF16) for k in names],
        input_output_aliases={i: i for i in range(len(names))},
        scratch_shapes=[pltpu.SemaphoreType.DMA((3 * n,))] * 2,
    )(*[arrays[k] for k in names])
    return dict(zip(names, res))


def _win_pieces():
    pieces = []
    for g0, wd, i0 in SEGS.values():
        for j in range(N_CHIPS):
            lo, hi = max(g0, j * SHARD_IN), min(g0 + wd, (j + 1) * SHARD_IN)
            if lo < hi:
                pieces.append((j, lo - j * SHARD_IN, hi - j * SHARD_IN, i0 + lo - g0))
    return pieces


def _win_to_internal(win4):
    tr = 128

    def body(i_ref, o_ref):
        for j, s0, s1, d0 in _win_pieces():
            o_ref[:, d0:d0 + s1 - s0] = i_ref[j, :, s0:s1]
        o_ref[:, PROJ_W:] = jnp.zeros((tr, PROJ_PAD - PROJ_W), o_ref.dtype)

    return pl.pallas_call(
        body, name="win_to_internal", grid=(D // tr,),
        in_specs=[pl.BlockSpec((N_CHIPS, tr, SHARD_IN), lambda i: (0, i, 0))],
        out_specs=pl.BlockSpec((tr, PROJ_PAD), lambda i: (i, 0)),
        out_shape=jax.ShapeDtypeStruct((D, PROJ_PAD), win4.dtype),
        compiler_params=_params(("parallel",), 40 << 20),
    )(win4)


def _win_from_internal(g):
    tr = 64

    def body(i_ref, o_ref):
        for j, s0, s1, d0 in _win_pieces():
            o_ref[j, :, s0:s1] = i_ref[:, d0:d0 + s1 - s0]

    return pl.pallas_call(
        body, name="win_from_internal", grid=(D // tr,),
        in_specs=[pl.BlockSpec((tr, PROJ_PAD), lambda i: (i, 0))],
        out_specs=pl.BlockSpec((N_CHIPS, tr, SHARD_IN), lambda i: (0, i, 0)),
        out_shape=jax.ShapeDtypeStruct((N_CHIPS, D, SHARD_IN), g.dtype),
        compiler_params=_params(("parallel",), 40 << 20),
    )(g)


def _rs_pair_exchange(grads, tag):
    n = len(grads)

    def body(*refs):
        srcs, dsts = refs[:n], refs[n:2 * n]
        send, recv = refs[2 * n:]
        x, y, c, _ = _place()
        cps = []
        for i in range(n):
            hr = srcs[i].shape[1] // 2
            cp = pltpu.make_async_remote_copy(
                src_ref=srcs[i].at[:, pl.ds(pl.multiple_of((1 - c) * hr, 16), hr), :], dst_ref=dsts[i],
                send_sem=send.at[i], recv_sem=recv.at[i], device_id=(x, y, 1 - c), device_id_type=MESH)
            cp.start()
            cps.append(cp)
        for cp in cps:
            cp.wait()

    return pl.pallas_call(
        body, name="rs_pair_exchange_" + tag, in_specs=[ANY] * n, out_specs=[ANY] * n,
        out_shape=[jax.ShapeDtypeStruct((g.shape[0], g.shape[1] // 2, g.shape[2]), F32) for g in grads],
        scratch_shapes=[pltpu.SemaphoreType.DMA((n,))] * 2,
    )(*grads)


def _pair_copy(src, dst, c, to, send, recv, k):
    hr = src.shape[1] // 2
    return pltpu.make_async_remote_copy(
        src_ref=src.at[:, pl.ds(pl.multiple_of((1 - c) * hr, 16), hr), :], dst_ref=dst,
        send_sem=send.at[k], recv_sem=recv.at[k], device_id=to, device_id_type=MESH)


def _rs_pair_start(grads, carried):
    n = len(grads)

    def body(*refs):
        send, recv = refs[2 * n + 1], refs[2 * n + 2]
        srcs, dsts = refs[2 * n + 3:3 * n + 3], refs[3 * n + 3:4 * n + 3]
        x, y, c, _ = _place()
        for i in range(n):
            _pair_copy(srcs[i], dsts[i], c, (x, y, 1 - c), send, recv, i).start()

    lands = [lax.empty((g.shape[0], g.shape[1] // 2, g.shape[2]), F32) for g in grads]
    res = pl.pallas_call(
        body, name="rs_pair_start",
        in_specs=[HBM] * (2 * n + 1), out_specs=[SEM, SEM] + [HBM] * (2 * n + 1),
        out_shape=[pltpu.SemaphoreType.DMA((n,)), pltpu.SemaphoreType.DMA((n,))]
        + [pltpu.HBM(g.shape, F32) for g in grads] + [pltpu.HBM(l.shape, F32) for l in lands]
        + [pltpu.HBM(carried.shape, carried.dtype)],
        input_output_aliases={i: i + 2 for i in range(2 * n + 1)},
        compiler_params=pltpu.CompilerParams(has_side_effects=pltpu.SideEffectType.DATAFLOW_SIDE_EFFECTING),
    )(*[pltpu.with_memory_space_constraint(a, pltpu.HBM) for a in list(grads) + lands + [carried]])
    return (res[0], res[1], list(res[2:2 + n]), list(res[2 + n:2 + 2 * n])), res[-1]


def _rs_pair_wait(send, recv, grads, lands, after):
    n = len(grads)

    def body(*refs):
        srcs, dsts = refs[:n], refs[n:2 * n]
        send_ref, recv_ref = refs[2 * n], refs[2 * n + 1]
        x, y, c, _ = _place()
        for i in range(n):
            cp = _pair_copy(srcs[i], dsts[i], c, (x, y, 1 - c), send_ref, recv_ref, i)
            cp.wait_send()
            cp.wait_recv()

    res = pl.pallas_call(
        body, name="rs_pair_wait",
        in_specs=[HBM] * (2 * n) + [SEM, SEM, pl.BlockSpec(memory_space=pl.ANY)],
        out_specs=[HBM] * (2 * n),
        out_shape=[pltpu.HBM(g.shape, F32) for g in grads] + [pltpu.HBM(l.shape, F32) for l in lands],
        input_output_aliases={i: i for i in range(2 * n)},
        compiler_params=pltpu.CompilerParams(has_side_effects=pltpu.SideEffectType.DATAFLOW_SIDE_EFFECTING),
    )(*grads, *lands, send, recv, after)
    return list(res[:n]), list(res[n:])


def _half_tile(hr):
    return _divtile(hr, 256, 16) if hr % 256 == 0 else _divtile(hr, 512, 16)


def _rs_pair_sum(g, r, c_idx, name):
    ns, rows, cols = g.shape
    hr = rows // 2
    tr = _half_tile(hr)
    nt = hr // tr

    def body(c_ref, g_ref, r_ref, ob_ref, of_ref):
        s = g_ref[...] + r_ref[...]
        ob_ref[...] = s.astype(BF16)
        of_ref[...] = s

    blk = pl.BlockSpec((None, tr, cols), lambda j, t, c_ref: (j, t, 0))
    return pl.pallas_call(
        body, name=name,
        grid_spec=pltpu.PrefetchScalarGridSpec(
            num_scalar_prefetch=1, grid=(ns, nt),
            in_specs=[pl.BlockSpec((None, tr, cols), lambda j, t, c_ref: (j, c_ref[0] * nt + t, 0)), blk],
            out_specs=[blk, blk]),
        out_shape=[jax.ShapeDtypeStruct((ns, hr, cols), BF16), jax.ShapeDtypeStruct((ns, hr, cols), F32)],
        compiler_params=_params(("parallel", "parallel"), 48 << 20),
    )(c_idx, g, r)


def _rs_chip_start(parts, tag):
    n = len(parts)

    def body(*refs):
        send, recv = refs[2 * n], refs[2 * n + 1]
        srcs, dsts = refs[2 * n + 2:3 * n + 2], refs[3 * n + 2:4 * n + 2]
        token = refs[-1]
        x, y, c, peers = _place()
        for i in range(n):
            for k, (px, py) in enumerate(peers):
                pltpu.make_async_remote_copy(
                    src_ref=srcs[i].at[2 * px + py], dst_ref=dsts[i].at[k],
                    send_sem=send.at[3 * i + k], recv_sem=recv.at[3 * i + k],
                    device_id=(px, py, c), device_id_type=MESH).start()
        token[...] = jnp.zeros_like(token)

    lands = [lax.empty((3,) + p.shape[1:], BF16) for p in parts]
    res = pl.pallas_call(
        body, name="rs_chip_start_" + tag,
        in_specs=[HBM] * (2 * n),
        out_specs=[SEM, SEM] + [HBM] * (2 * n) + [pl.BlockSpec(memory_space=pltpu.VMEM)],
        out_shape=[pltpu.SemaphoreType.DMA((3 * n,)), pltpu.SemaphoreType.DMA((3 * n,))]
        + [pltpu.HBM(p.shape, BF16) for p in parts] + [pltpu.HBM(l.shape, BF16) for l in lands]
        + [jax.ShapeDtypeStruct((8, 128), F32)],
        input_output_aliases={i: i + 2 for i in range(2 * n)},
        compiler_params=pltpu.CompilerParams(has_side_effects=pltpu.SideEffectType.DATAFLOW_SIDE_EFFECTING),
    )(*[pltpu.with_memory_space_constraint(a, pltpu.HBM) for a in list(parts) + lands])
    return res[0], res[1], list(res[2:2 + n]), list(res[2 + n:2 + 2 * n]), res[-1]


def _rs_chip_wait(send, recv, parts, lands, after, tag):
    n = len(parts)

    def body(*refs):
        srcs, dsts = refs[:n], refs[n:2 * n]
        send_ref, recv_ref = refs[2 * n], refs[2 * n + 1]
        x, y, c, peers = _place()
        for i in range(n):
            for k, (px, py) in enumerate(peers):
                cp = pltpu.make_async_remote_copy(
                    src_ref=srcs[i].at[2 * px + py], dst_ref=dsts[i].at[k],
                    send_sem=send_ref.at[3 * i + k], recv_sem=recv_ref.at[3 * i + k],
                    device_id=(px, py, c), device_id_type=MESH)
                cp.wait_send()
                cp.wait_recv()

    res = pl.pallas_call(
        body, name="rs_chip_wait_" + tag,
        in_specs=[HBM] * (2 * n) + [SEM, SEM, pl.BlockSpec(memory_space=pl.ANY)],
        out_specs=[HBM] * (2 * n),
        out_shape=[pltpu.HBM(p.shape, BF16) for p in parts] + [pltpu.HBM(l.shape, BF16) for l in lands],
        input_output_aliases={i: i for i in range(2 * n)},
        compiler_params=pltpu.CompilerParams(has_side_effects=pltpu.SideEffectType.DATAFLOW_SIDE_EFFECTING),
    )(*parts, *lands, send, recv, after)
    return list(res[n:])


def _rs_final_sum(own, got, chip_idx, c_idx, name):
    ns, hr, cols = own.shape
    tr = _half_tile(hr)
    nt = hr // tr

    def body(chip_ref, c_ref, o_ref, g_ref, out_ref):
        s = o_ref[...]
        for k in range(3):
            s = s + g_ref[k].astype(F32)
        out_ref[...] = s

    return pl.pallas_call(
        body, name=name,
        grid_spec=pltpu.PrefetchScalarGridSpec(
            num_scalar_prefetch=2, grid=(nt,),
            in_specs=[pl.BlockSpec((None, tr, cols), lambda t, chip_ref, c_ref: (chip_ref[0], t, 0)),
                      pl.BlockSpec((3, tr, cols), lambda t, chip_ref, c_ref: (0, t, 0))],
            out_specs=pl.BlockSpec((tr, cols), lambda t, chip_ref, c_ref: (c_ref[0] * nt + t, 0))),
        out_shape=jax.ShapeDtypeStruct((2 * hr, cols), F32),
        compiler_params=_params(("parallel",), 48 << 20),
    )(chip_idx, c_idx, own, got)


def _rs_share_halves(fulls, tag):
    n = len(fulls)

    def body(*refs):
        dsts = refs[n:2 * n]
        send, recv = refs[2 * n:]
        x, y, c, _ = _place()
        cps = []
        for i in range(n):
            hr = dsts[i].shape[0] // 2
            rows = dsts[i].at[pl.ds(pl.multiple_of(c * hr, 8), hr), :]
            cp = pltpu.make_async_remote_copy(src_ref=rows, dst_ref=rows, send_sem=send.at[i], recv_sem=recv.at[i],
                                              device_id=(x, y, 1 - c), device_id_type=MESH)
            cp.start()
            cps.append(cp)
        for i in range(n):
            hr = dsts[i].shape[0] // 2
            other = dsts[i].at[pl.ds(pl.multiple_of((1 - c) * hr, 8), hr), :]
            pltpu.make_async_remote_copy(src_ref=other, dst_ref=other, send_sem=send.at[i], recv_sem=recv.at[i],
                                         device_id=(x, y, 1 - c), device_id_type=MESH).wait_recv()
        for cp in cps:
            cp.wait_send()

    return pl.pallas_call(
        body, name="rs_share_halves_" + tag, in_specs=[ANY] * n, out_specs=[ANY] * n,
        out_shape=[jax.ShapeDtypeStruct(f.shape, F32) for f in fulls],
        input_output_aliases={i: i for i in range(n)},
        scratch_shapes=[pltpu.SemaphoreType.DMA((n,))] * 2,
    )(*fulls)


def _all_reduce_small(v):
    rows = v.shape[0]

    def body(v_ref, o_ref, buf, send, recv):
        x, y, c, _ = _place()
        me = 4 * x + 2 * y + c
        buf[me] = v_ref[...]
        cps = []
        for d in range(1, 8):
            px, py, pc = x ^ (d >> 2), y ^ ((d >> 1) & 1), c ^ (d & 1)
            cp = pltpu.make_async_remote_copy(src_ref=v_ref, dst_ref=buf.at[me], send_sem=send.at[d - 1],
                                              recv_sem=recv.at[d - 1], device_id=(px, py, pc), device_id_type=MESH)
            cp.start()
            cps.append(cp)
        for d in range(1, 8):
            px, py, pc = x ^ (d >> 2), y ^ ((d >> 1) & 1), c ^ (d & 1)
            pltpu.make_async_remote_copy(src_ref=v_ref, dst_ref=buf.at[4 * px + 2 * py + pc], send_sem=send.at[d - 1],
                                         recv_sem=recv.at[d - 1], device_id=(px, py, pc),
                                         device_id_type=MESH).wait_recv()
        for cp in cps:
            cp.wait_send()
        acc = buf[0]
        for d in range(1, 8):
            acc = acc + buf[d]
        o_ref[...] = acc

    vm = pl.BlockSpec(memory_space=pltpu.VMEM)
    return pl.pallas_call(
        body, name="all_reduce_small", in_specs=[vm], out_specs=vm,
        out_shape=jax.ShapeDtypeStruct((rows, 128), F32),
        scratch_shapes=[pltpu.VMEM((8, rows, 128), F32), pltpu.SemaphoreType.DMA((7,)), pltpu.SemaphoreType.DMA((7,))],
    )(v)


def _adamw(w, g, m, v, name, g_col_blk=0):
    rows, cols = w.shape
    tr = _divtile(rows, max(8, (2 << 20) // (4 * cols) // 8 * 8), 8)

    def body(w_ref, g_ref, m_ref, v_ref, go_ref, d_ref, mo_ref, vo_ref):
        gv = g_ref[...]
        mn = ADAM_B1 * m_ref[...] + (1.0 - ADAM_B1) * gv
        vn = ADAM_B2 * v_ref[...] + (1.0 - ADAM_B2) * (gv * gv)
        m_hat = mn / (1.0 - ADAM_B1 ** ADAM_STEP)
        v_hat = vn / (1.0 - ADAM_B2 ** ADAM_STEP)
        go_ref[...] = gv
        d_ref[...] = -ADAM_LR * (m_hat / (jnp.sqrt(v_hat) + ADAM_EPS) + ADAM_WD * w_ref[...])
        mo_ref[...] = mn
        vo_ref[...] = vn

    blk = pl.BlockSpec((tr, cols), lambda i: (i, 0))
    return pl.pallas_call(
        body, name=name, grid=(rows // tr,),
        in_specs=[blk, pl.BlockSpec((tr, cols), lambda i: (i, g_col_blk)), blk, blk],
        out_specs=[blk] * 4, out_shape=[jax.ShapeDtypeStruct((rows, cols), F32)] * 4,
        compiler_params=_params(("parallel",), 48 << 20),
    )(w, g, m, v)


SMALL = ["ln1_g", "ln1_b", "conv_w", "conv_b", "dt_bias", "a_log", "d_skip", "ssd_norm_g", "attn_sinks",
         "ln2_g", "ln2_b", "ln3_g", "ln3_b"]


def _pack_rows(vs):
    parts = []
    for v in vs:
        v = v.reshape(-1)
        parts.append(jnp.pad(v, (0, (-v.shape[0]) % 128)))
    flat = jnp.concatenate(parts)
    flat = jnp.pad(flat, (0, (-flat.shape[0]) % 1024))
    return flat.reshape(-1, 128)


def _unpack_rows(packed, shapes):
    flat = packed.reshape(-1)
    out, at = [], 0
    for s in shapes:
        nel = int(np.prod(s))
        out.append(flat[at:at + nel].reshape(s))
        at += nel + (-nel) % 128
    return out


def kernel(x, positions, ffn1_w_gate, ffn1_w_up, ffn1_w_down, ln1_g, ln1_b, w_in, conv_w, conv_b, dt_bias, a_log, d_skip, ssd_norm_g, w_ssd_o, attn_sinks, w_attn_o, w_out, ln2_g, ln2_b, ffn2_w_gate, ffn2_w_up, ffn2_w_down, ln3_g, ln3_b, loss_target, m_ffn1_w_gate, m_ffn1_w_up, m_ffn1_w_down, m_ln1_g, m_ln1_b, m_w_in, m_conv_w, m_conv_b, m_dt_bias, m_a_log, m_d_skip, m_ssd_norm_g, m_w_ssd_o, m_attn_sinks, m_w_attn_o, m_w_out, m_ln2_g, m_ln2_b, m_ffn2_w_gate, m_ffn2_w_up, m_ffn2_w_down, m_ln3_g, m_ln3_b, v_ffn1_w_gate, v_ffn1_w_up, v_ffn1_w_down, v_ln1_g, v_ln1_b, v_w_in, v_conv_w, v_conv_b, v_dt_bias, v_a_log, v_d_skip, v_ssd_norm_g, v_w_ssd_o, v_attn_sinks, v_w_attn_o, v_w_out, v_ln2_g, v_ln2_b, v_ffn2_w_gate, v_ffn2_w_up, v_ffn2_w_down, v_ln3_g, v_ln3_b):
    args = dict(locals())
    wts = {n: args[n][0] for n in [b[0] for b in BIG] + SMALL}
    mom_m = {n: args["m_" + n][0] for n in wts}
    mom_v = {n: args["v_" + n][0] for n in wts}
    t = x.shape[1]
    xi, yi, ci = lax.axis_index("x"), lax.axis_index("y"), lax.axis_index("c")
    chip = 2 * xi + yi

    c_idx = ci.astype(jnp.int32).reshape(1)
    chip_idx = chip.astype(jnp.int32).reshape(1)
    placed = {o: _cast_place([wts[b[0]] for b in BIG if b[3] == o], o, chip_idx) for o in GATHERED}
    g_sems, g_flight, g_token = _gather_ici_start(placed)

    def fetch(group, after):
        names = GATHER_GROUPS[group]
        send, recv = g_sems[group]
        landed = _gather_ici_wait(send, recv, {k: g_flight[k] for k in names}, names, after, str(group))
        got = _gather_d2d(landed, names, str(group))
        if "win4" in got:
            got["win"] = _win_to_internal(got.pop("win4"))
        return got

    w = fetch(0, g_token)

    def slabs_of(gw, names):
        view = {"gu1": lambda: gw["gu1"], "gu2": lambda: gw["gu2"],
                "d1": lambda: gw["d1"].reshape(N_CHIPS, SHARD_H, D), "d2": lambda: gw["d2"].reshape(N_CHIPS, SHARD_H, D),
                "win": lambda: _win_from_internal(gw["win"]),
                "so": lambda: gw["so"].reshape(N_CHIPS, SSD_INNER // N_CHIPS, D),
                "ao": lambda: gw["ao"].reshape(N_CHIPS, D // N_CHIPS, D),
                "out": lambda: gw["out"].reshape(N_CHIPS, D // N_CHIPS, D)}
        return [view[nm]() for nm in names]

    early = ["win", "so", "ao", "out", "gu2", "d2"]
    late = ["gu1", "d1"]
    flight = {}

    def early_start(gw, win):
        flight["pair"], win = _rs_pair_start(slabs_of(gw, early), win)
        return win

    def early_mid(dh1):
        slabs, from_sib = _rs_pair_wait(*flight["pair"], dh1)
        pair = [_rs_pair_sum(g, r, c_idx, "rs_pair_sum_" + nm) for g, r, nm in zip(slabs, from_sib, early)]
        send, recv, parts, lands, token = _rs_chip_start([p[0] for p in pair], "early")
        flight.update(send=send, recv=recv, parts=parts, lands=lands, own=[p[1] for p in pair])
        return token

    early_grads = (early_start, early_mid)
    cw_rows = _pack_rows([lax.dynamic_update_slice(jnp.zeros((4, XBC), F32), wts["conv_w"], (0, chip * (XBC // N_CHIPS)))])
    cw_rows = jnp.where(ci == 0, cw_rows, 0.0)
    conv_w_full = _all_reduce_small(cw_rows)[:4 * XBC // 128].reshape(4, XBC)

    small = {n: (wts[n][None, :] if wts[n].ndim == 1 else wts[n]) for n in SMALL}
    small["conv_w"] = conv_w_full
    loss, grad_x, gw, gs = _local_step(x[0], positions[0].astype(F32)[:, None], loss_target[0], w, small,
                                       fetch=fetch, early_grads=early_grads)

    slabs = slabs_of(gw, late)
    from_sib = _rs_pair_exchange(slabs, "late")
    pair = [_rs_pair_sum(g, r, c_idx, "rs_pair_sum_" + nm) for g, r, nm in zip(slabs, from_sib, late)]
    l_send, l_recv, l_parts, l_lands, l_token = _rs_chip_start([p[0] for p in pair], "late")
    got_early = _rs_chip_wait(flight["send"], flight["recv"], flight["parts"], flight["lands"], l_token, "early")

    outs = {}
    big_src = {"ffn1_w_gate": ("gu1", 0), "ffn1_w_up": ("gu1", 1), "ffn1_w_down": ("d1", 0), "w_in": ("win", 0),
               "w_ssd_o": ("so", 0), "w_attn_o": ("ao", 0), "w_out": ("out", 0),
               "ffn2_w_gate": ("gu2", 0), "ffn2_w_up": ("gu2", 1), "ffn2_w_down": ("d2", 0)}

    def finish(names, own, got, tag):
        halves = [_rs_final_sum(o, gt, chip_idx, c_idx, "rs_final_sum_" + nm) for o, gt, nm in zip(own, got, names)]
        full = dict(zip(names, _rs_share_halves(halves, tag)))
        for nm, (src, blk) in big_src.items():
            if src in full:
                outs[nm] = _adamw(wts[nm], full[src], mom_m[nm], mom_v[nm], "adamw_" + nm, g_col_blk=blk)

    finish(early, flight["own"], got_early, "early")
    got_late = _rs_chip_wait(l_send, l_recv, l_parts, l_lands, outs["w_in"][1], "late")
    finish(late, [p[1] for p in pair], got_late, "late")

    gvec = {n: gs[n] for n in SMALL}
    gvec["dt_bias"], gvec["a_log"], gvec["d_skip"] = gs["dt_bias"][:, :64], gs["a_log"][:, :64], gs["d_skip"][:, :64]
    gvec["attn_sinks"] = gs["attn_sinks"][:, :NQ]
    red = _all_reduce_small(_pack_rows([gvec[n] for n in SMALL] + [loss]))
    shapes = [(4, XBC) if n == "conv_w" else wts[n].shape for n in SMALL] + [(1,)]
    red_list = _unpack_rows(red, shapes)
    loss_out = red_list[-1].reshape(())
    gsm = dict(zip(SMALL, red_list[:-1]))
    gsm["conv_w"] = lax.dynamic_slice_in_dim(gsm["conv_w"], chip * (XBC // N_CHIPS), XBC // N_CHIPS, axis=1)
    sm_shapes = [wts[n].shape for n in SMALL]
    res = _adamw(_pack_rows([wts[n] for n in SMALL]), _pack_rows([gsm[n] for n in SMALL]),
                 _pack_rows([mom_m[n] for n in SMALL]), _pack_rows([mom_v[n] for n in SMALL]), "adamw_small")
    res = [_unpack_rows(r, sm_shapes) for r in res]
    for i, nm in enumerate(SMALL):
        outs[nm] = tuple(r[i] for r in res)

    order = ["ffn1_w_gate", "ffn1_w_up", "ffn1_w_down", "ln1_g", "ln1_b", "w_in", "conv_w", "conv_b", "dt_bias", "a_log",
             "d_skip", "ssd_norm_g", "w_ssd_o", "attn_sinks", "w_attn_o", "w_out", "ln2_g", "ln2_b",
             "ffn2_w_gate", "ffn2_w_up", "ffn2_w_down", "ln3_g", "ln3_b"]
    result = [loss_out, grad_x[None]]
    for kind in range(4):
        result += [outs[nm][kind][None] for nm in order]
    return tuple(result)
```

```python
import functools
import math

import numpy as np
import jax
import jax.numpy as jnp
from jax import lax
from jax.experimental import pallas as pl
from jax.experimental.pallas import tpu as pltpu

F32 = jnp.float32
BF16 = jnp.bfloat16
HI = lax.Precision.HIGHEST

D = 2048
FFN_H = 5632
SSD_INNER = 4096
SSD_HEADS = 64
SSD_P = 64
SSD_G = 8
SSD_R = 8
SSD_N = 128
CHUNK = 128
XBC = 6144
NQ = 32
NKV = 4
HD = 64
QW = 2048
KVW = 256
WINDOW = 128
ROPE_THETA = 10000.0
ALPHA = 2.0 ** 0.25
LN_EPS = 1e-5
RMS_EPS = 1e-5
PROJ_W = 16960
N_CHIPS = 4
SHARD_IN = PROJ_W // N_CHIPS
SHARD_H = FFN_H // N_CHIPS

SEGS = {
    "z": (0, 4096, 0),
    "xbc": (4096, 6144, 10240),
    "dt": (10240, 64, 16896),
    "q": (10304, 2048, 4096),
    "k": (12352, 256, 16384),
    "v": (12608, 256, 16640),
    "gs": (12864, 2048, 6144),
    "ga": (14912, 2048, 8192),
}
PROJ_PAD = 17024

ADAM_LR = 0.001
ADAM_B1 = 0.9
ADAM_B2 = 0.999
ADAM_EPS = 1e-08
ADAM_WD = 0.01
ADAM_STEP = 10

VMEM_CAP = 60 * 1024 * 1024


def _params(sem, vmem_bytes):
    return pltpu.CompilerParams(dimension_semantics=sem, vmem_limit_bytes=int(min(VMEM_CAP, vmem_bytes)))


def _divtile(n, cap, q=128):
    best = None
    for d in range(q, min(n, cap) + 1, q):
        if n % d == 0:
            best = d
    return n if best is None else best


def _sigmoid(x):
    return 1.0 / (1.0 + jnp.exp(-x))


def _mm(a, b, mode, out_dtype, name, add=None, add_scale=1.0, caps=(1024, 1024, 2048), n_slabs=1):
    if mode == "nn":
        (m, k), (k2, n) = a.shape, b.shape
    elif mode == "nt":
        (m, k), (n, k2) = a.shape, b.shape
    else:
        (k, m), (k2, n) = a.shape, b.shape
    assert k == k2, (a.shape, b.shape, mode)
    tm, tn, tk = _divtile(m, caps[0]), _divtile(n // n_slabs, caps[1]), _divtile(k, caps[2])
    nk = k // tk
    per_slab = n // n_slabs // tn
    dims = {"nn": ((1,), (0,)), "nt": ((1,), (1,)), "tn": ((0,), (0,))}[mode]
    has_add = add is not None

    def body(*refs):
        if has_add:
            a_ref, b_ref, add_ref, o_ref = refs[:4]
            scr = refs[4:]
        else:
            a_ref, b_ref, o_ref = refs[:3]
            add_ref = None
            scr = refs[3:]
        part = lax.dot_general(a_ref[...].astype(BF16), b_ref[...].astype(BF16), (dims, ((), ())),
                               preferred_element_type=F32)

        def finish(acc):
            if has_add:
                acc = acc + add_scale * add_ref[...].astype(F32)
            o_ref[...] = acc.astype(o_ref.dtype)

        if nk == 1:
            finish(part)
        else:
            acc_ref = scr[0]
            kk = pl.program_id(2)

            @pl.when(kk == 0)
            def _():
                acc_ref[...] = part

            @pl.when(kk > 0)
            def _():
                acc_ref[...] += part

            @pl.when(kk == nk - 1)
            def _():
                finish(acc_ref[...])

    if mode == "nn":
        a_spec = pl.BlockSpec((tm, tk), lambda i, j, kk: (i, kk))
        b_spec = pl.BlockSpec((tk, tn), lambda i, j, kk: (kk, j))
    elif mode == "nt":
        a_spec = pl.BlockSpec((tm, tk), lambda i, j, kk: (i, kk))
        b_spec = pl.BlockSpec((tn, tk), lambda i, j, kk: (j, kk))
    else:
        a_spec = pl.BlockSpec((tk, tm), lambda i, j, kk: (kk, i))
        b_spec = pl.BlockSpec((tk, tn), lambda i, j, kk: (kk, j))
    o_spec = pl.BlockSpec((tm, tn), lambda i, j, kk: (i, j))
    out_shape = jax.ShapeDtypeStruct((m, n), out_dtype)
    if n_slabs > 1:
        assert not has_add
        o_spec = pl.BlockSpec((None, tm, tn), lambda i, j, kk: (j // per_slab, i, j % per_slab))
        out_shape = jax.ShapeDtypeStruct((n_slabs, m, n // n_slabs), out_dtype)
    in_specs = [a_spec, b_spec] + ([o_spec] if has_add else [])
    args = (a, b) + ((add,) if has_add else ())
    osz = jnp.dtype(out_dtype).itemsize
    vmem = (2 * (tm * tk * a.dtype.itemsize + tk * tn * b.dtype.itemsize) + 2 * tm * tn * osz
            + (2 * tm * tn * add.dtype.itemsize if has_add else 0) + 2 * tm * tn * 4
            + 2 * (tm * tk + tk * tn) + (8 << 20))
    return pl.pallas_call(
        body, name=name, grid=(m // tm, n // tn, nk),
        in_specs=in_specs, out_specs=o_spec, out_shape=out_shape,
        scratch_shapes=[pltpu.VMEM((tm, tn), F32)] if nk > 1 else [],
        compiler_params=_params(("parallel", "parallel", "arbitrary"), vmem),
    )(*args)


def _mm_swiglu(a, b, name):
    m, k = a.shape
    w = SHARD_H
    tm = _divtile(m, 512)

    def body(a_ref, b_ref, gu_ref, act_ref):
        gu = jnp.dot(a_ref[...], b_ref[...], preferred_element_type=F32)
        g = gu[:, :w]
        gu_ref[...] = gu.astype(BF16)
        act_ref[...] = (g * _sigmoid(g) * gu[:, w:]).astype(BF16)

    return pl.pallas_call(
        body, name=name, grid=(N_CHIPS, m // tm),
        in_specs=[pl.BlockSpec((tm, k), lambda j, i: (i, 0)), pl.BlockSpec((k, 2 * w), lambda j, i: (0, j))],
        out_specs=[pl.BlockSpec((tm, 2 * w), lambda j, i: (i, j)), pl.BlockSpec((tm, w), lambda j, i: (i, j))],
        out_shape=[jax.ShapeDtypeStruct((m, 2 * FFN_H), BF16), jax.ShapeDtypeStruct((m, FFN_H), BF16)],
        compiler_params=_params(("parallel", "parallel"), 56 << 20),
    )(a, b)


def _swiglu_bwd(gu, da, name):
    t = gu.shape[0]
    tt = _divtile(t, 512)
    w = SHARD_H

    def body(gu_ref, da_ref, o_ref):
        g = gu_ref[:, :w].astype(F32)
        u = gu_ref[:, w:].astype(F32)
        d = da_ref[...].astype(F32)
        s = _sigmoid(g)
        o_ref[:, :w] = (d * u * (s * (1.0 + g * (1.0 - s)))).astype(BF16)
        o_ref[:, w:] = (d * (g * s)).astype(BF16)

    return pl.pallas_call(
        body, name=name, grid=(t // tt, N_CHIPS),
        in_specs=[pl.BlockSpec((tt, 2 * w), lambda i, j: (i, j)), pl.BlockSpec((tt, w), lambda i, j: (i, j))],
        out_specs=pl.BlockSpec((tt, 2 * w), lambda i, j: (i, j)),
        out_shape=jax.ShapeDtypeStruct((t, 2 * FFN_H), BF16),
        compiler_params=_params(("parallel", "parallel"), 40 << 20),
    )(gu, da)


def _ln_fwd(base, f, g, b, c, name, target=None):
    t = base.shape[0]
    tt = _divtile(t, 256)
    with_loss = target is not None

    def body(*refs):
        if with_loss:
            base_ref, f_ref, g_ref, b_ref, tg_ref, h_ref, hb_ref, xh_ref, rs_ref, dh_ref, loss_ref = refs
        else:
            base_ref, f_ref, g_ref, b_ref, h_ref, hb_ref, xh_ref, rs_ref = refs
        r = ALPHA * base_ref[...] + c * f_ref[...]
        mu = jnp.mean(r, axis=-1, keepdims=True)
        xc = r - mu
        var = jnp.mean(xc * xc, axis=-1, keepdims=True)
        rstd = lax.rsqrt(var + LN_EPS)
        xh = xc * rstd
        h = xh * g_ref[...] + b_ref[...]
        h_ref[...] = h
        hb_ref[...] = h.astype(BF16)
        xh_ref[...] = xh
        rs_ref[...] = rstd
        if with_loss:
            e = h - tg_ref[...]
            dh_ref[...] = e * (1.0 / D)
            part = 0.5 * jnp.sum(jnp.sum(e * e, axis=-1, keepdims=True) * (1.0 / D), axis=0, keepdims=True)

            @pl.when(pl.program_id(0) == 0)
            def _():
                loss_ref[...] = jnp.zeros_like(loss_ref)

            loss_ref[...] += part

    row = pl.BlockSpec((tt, D), lambda i: (i, 0))
    vec = pl.BlockSpec((1, D), lambda i: (0, 0))
    col = pl.BlockSpec((tt, 1), lambda i: (i, 0))
    in_specs = [row, row, vec, vec] + ([row] if with_loss else [])
    out_specs = [row, row, row, col] + ([row, pl.BlockSpec((1, 1), lambda i: (0, 0))] if with_loss else [])
    out_shape = [jax.ShapeDtypeStruct((t, D), F32), jax.ShapeDtypeStruct((t, D), BF16),
                 jax.ShapeDtypeStruct((t, D), F32), jax.ShapeDtypeStruct((t, 1), F32)]
    if with_loss:
        out_shape += [jax.ShapeDtypeStruct((t, D), F32), jax.ShapeDtypeStruct((1, 1), F32)]
    args = (base, f, g, b) + ((target,) if with_loss else ())
    return pl.pallas_call(
        body, name=name, grid=(t // tt,), in_specs=in_specs, out_specs=out_specs, out_shape=out_shape,
        compiler_params=_params(("arbitrary",) if with_loss else ("parallel",), 48 << 20),
    )(*args)


def _ln_bwd(dy, xh, rstd, g, c, name):
    t = dy.shape[0]
    tt = _divtile(t, 256)

    def body(dy_ref, xh_ref, rs_ref, g_ref, dr_ref, drb_ref, dg_ref, db_ref):
        dyv = dy_ref[...]
        xhv = xh_ref[...]
        dxh = dyv * g_ref[...]
        m1 = jnp.mean(dxh, axis=-1, keepdims=True)
        m2 = jnp.mean(dxh * xhv, axis=-1, keepdims=True)
        dr = rs_ref[...] * (dxh - m1 - xhv * m2)
        dr_ref[...] = dr
        drb_ref[...] = (c * dr).astype(BF16)

        @pl.when(pl.program_id(0) == 0)
        def _():
            dg_ref[...] = jnp.zeros_like(dg_ref)
            db_ref[...] = jnp.zeros_like(db_ref)

        dg_ref[...] += jnp.sum(dyv * xhv, axis=0, keepdims=True)
        db_ref[...] += jnp.sum(dyv, axis=0, keepdims=True)

    row = pl.BlockSpec((tt, D), lambda i: (i, 0))
    vec = pl.BlockSpec((1, D), lambda i: (0, 0))
    col = pl.BlockSpec((tt, 1), lambda i: (i, 0))
    return pl.pallas_call(
        body, name=name, grid=(t // tt,), in_specs=[row, row, col, vec], out_specs=[row, row, vec, vec],
        out_shape=[jax.ShapeDtypeStruct((t, D), F32), jax.ShapeDtypeStruct((t, D), BF16),
                   jax.ShapeDtypeStruct((1, D), F32), jax.ShapeDtypeStruct((1, D), F32)],
        compiler_params=_params(("arbitrary",), 40 << 20),
    )(dy, xh, rstd, g)


DT_BLK = SEGS["dt"][2] // 128


def _dt_prep(proj, bias128, alog128):
    t = proj.shape[0]
    tt = _divtile(t, 1024)

    def body(p_ref, bias_ref, alog_ref, dt_ref, adt_ref):
        dtv = jax.nn.softplus(p_ref[...] + bias_ref[...])
        dt_ref[...] = dtv
        adt_ref[...] = dtv * (-jnp.exp(alog_ref[...]))

    blk = pl.BlockSpec((tt, 128), lambda i: (i, 0))
    vec = pl.BlockSpec((1, 128), lambda i: (0, 0))
    return pl.pallas_call(
        body, name="dt_prep", grid=(t // tt,),
        in_specs=[pl.BlockSpec((tt, 128), lambda i: (i, DT_BLK)), vec, vec], out_specs=[blk, blk],
        out_shape=[jax.ShapeDtypeStruct((t, 128), F32)] * 2,
        compiler_params=_params(("parallel",), 16 << 20),
    )(proj, bias128, alog128)


def _dt_bwd(dadt, dxdx, proj, bias128, alog128):
    t = proj.shape[0]
    tt = _divtile(t, 1024)

    def body(dadt_ref, dxdx_ref, p_ref, bias_ref, alog_ref, o_ref, dbias_ref, dalog_ref):
        pre = p_ref[...] + bias_ref[...]
        dtv = jax.nn.softplus(pre)
        a = -jnp.exp(alog_ref[...])
        ddt = a * dadt_ref[...] + dxdx_ref[...]
        draw = ddt * _sigmoid(pre)
        o_ref[...] = draw.astype(BF16)

        @pl.when(pl.program_id(0) == 0)
        def _():
            dbias_ref[...] = jnp.zeros_like(dbias_ref)
            dalog_ref[...] = jnp.zeros_like(dalog_ref)

        dbias_ref[...] += jnp.sum(draw, axis=0, keepdims=True)
        dalog_ref[...] += jnp.sum(dadt_ref[...] * dtv * a, axis=0, keepdims=True)

    blk = pl.BlockSpec((tt, 128), lambda i: (i, 0))
    vec = pl.BlockSpec((1, 128), lambda i: (0, 0))
    return pl.pallas_call(
        body, name="dt_bwd", grid=(t // tt,),
        in_specs=[blk, blk, pl.BlockSpec((tt, 128), lambda i: (i, DT_BLK)), vec, vec],
        out_specs=[blk, vec, vec],
        out_shape=[jax.ShapeDtypeStruct((t, 128), BF16), jax.ShapeDtypeStruct((1, 128), F32),
                   jax.ShapeDtypeStruct((1, 128), F32)],
        compiler_params=_params(("arbitrary",), 16 << 20),
    )(dadt, dxdx, proj, bias128, alog128)


CONV_CB = 512
CONV_TT = 512


def _shift_down(cur, prev8, s):
    if s == 0:
        return cur
    rolled = pltpu.roll(cur, s, 0)
    head = pltpu.roll(prev8, s, 0)
    r8 = lax.broadcasted_iota(jnp.int32, (8, 1), 0)
    top = jnp.where(r8 < s, head, rolled[:8])
    return jnp.concatenate([top, rolled[8:]], axis=0)


def _shift_up(cur, next8, s):
    if s == 0:
        return cur
    n = cur.shape[0]
    rolled = pltpu.roll(cur, n - s, 0)
    tail = pltpu.roll(next8, 8 - s, 0)
    r8 = lax.broadcasted_iota(jnp.int32, (8, 1), 0)
    bot = jnp.where(r8 >= 8 - s, tail, rolled[n - 8:])
    return jnp.concatenate([rolled[:n - 8], bot], axis=0)


def _conv_fwd(proj, conv_w, conv_b):
    t = proj.shape[0]
    tt = _divtile(t, CONV_TT)
    base = SEGS["xbc"][2] // CONV_CB
    r8 = tt // 8

    def body(u_ref, up_ref, w_ref, b_ref, o_ref):
        cur = u_ref[...]
        prev8 = jnp.where(pl.program_id(1) > 0, up_ref[...], 0.0)
        acc = b_ref[...] + w_ref[3:4, :] * cur
        for k in range(3):
            acc = acc + w_ref[k:k + 1, :] * _shift_down(cur, prev8, 3 - k)
        o_ref[...] = acc * _sigmoid(acc)

    return pl.pallas_call(
        body, name="conv_fwd", grid=(XBC // CONV_CB, t // tt),
        in_specs=[pl.BlockSpec((tt, CONV_CB), lambda c, i: (i, base + c)),
                  pl.BlockSpec((8, CONV_CB), lambda c, i: (jnp.maximum(i * r8 - 1, 0), base + c)),
                  pl.BlockSpec((4, CONV_CB), lambda c, i: (0, c)),
                  pl.BlockSpec((1, CONV_CB), lambda c, i: (0, c))],
        out_specs=pl.BlockSpec((tt, CONV_CB), lambda c, i: (i, c)),
        out_shape=jax.ShapeDtypeStruct((t, XBC), F32),
        compiler_params=_params(("parallel", "parallel"), 24 << 20),
    )(proj, proj, conv_w, conv_b)


def _conv_bwd(proj, dout, conv_w, conv_b, col0, width, name, skip=None):
    t = proj.shape[0]
    tt = _divtile(t, CONV_TT)
    nt = t // tt
    base = SEGS["xbc"][2] // CONV_CB + col0 // CONV_CB
    wb = col0 // CONV_CB
    r8 = tt // 8
    has_skip = skip is not None

    def body(*refs):
        if has_skip:
            u_ref, up_ref, d_ref, w_ref, b_ref, sk_ref, skw_ref, du_ref, dw_ref, db_ref, nx_ref = refs
        else:
            u_ref, up_ref, d_ref, w_ref, b_ref, du_ref, dw_ref, db_ref, nx_ref = refs
        i = pl.program_id(1)
        cur = u_ref[...]
        prev8 = jnp.where(i < nt - 1, up_ref[...], 0.0)
        sh = [_shift_down(cur, prev8, 3 - k) for k in range(3)] + [cur]
        pre = b_ref[...]
        for k in range(4):
            pre = pre + w_ref[k:k + 1, :] * sh[k]
        sg = _sigmoid(pre)
        dout_v = d_ref[...]
        if has_skip:
            dout_v = dout_v + sk_ref[...] * skw_ref[...]
        dpre = dout_v * (sg * (1.0 + pre * (1.0 - sg)))

        @pl.when(i == 0)
        def _():
            nx_ref[...] = jnp.zeros_like(nx_ref)
            dw_ref[...] = jnp.zeros_like(dw_ref)
            db_ref[...] = jnp.zeros_like(db_ref)

        next8 = nx_ref[...]
        du = w_ref[3:4, :] * dpre
        for s in range(1, 4):
            du = du + w_ref[3 - s:4 - s, :] * _shift_up(dpre, next8, s)
        du_ref[...] = du.astype(BF16)
        nx_ref[...] = dpre[:8]
        rows = [jnp.sum(dpre * sh[k], axis=0, keepdims=True) for k in range(4)]
        dw_ref[...] += jnp.concatenate(rows + [jnp.zeros((4, CONV_CB), F32)], axis=0)
        db_ref[...] += jnp.sum(dpre, axis=0, keepdims=True)

    rev = lambda c, i: (nt - 1 - i, c)
    in_specs = [pl.BlockSpec((tt, CONV_CB), lambda c, i: (nt - 1 - i, base + c)),
                pl.BlockSpec((8, CONV_CB), lambda c, i: (jnp.maximum((nt - 1 - i) * r8 - 1, 0), base + c)),
                pl.BlockSpec((tt, CONV_CB), rev),
                pl.BlockSpec((4, CONV_CB), lambda c, i: (0, wb + c)),
                pl.BlockSpec((1, CONV_CB), lambda c, i: (0, wb + c))]
    args = [proj, proj, dout, conv_w, conv_b]
    if has_skip:
        in_specs += [pl.BlockSpec((tt, CONV_CB), rev), pl.BlockSpec((1, CONV_CB), lambda c, i: (0, c))]
        args += [skip[0], skip[1]]
    return pl.pallas_call(
        body, name=name, grid=(width // CONV_CB, nt),
        in_specs=in_specs,
        out_specs=[pl.BlockSpec((tt, CONV_CB), rev), pl.BlockSpec((8, CONV_CB), lambda c, i: (0, c)),
                   pl.BlockSpec((1, CONV_CB), lambda c, i: (0, c))],
        out_shape=[jax.ShapeDtypeStruct((t, width), BF16), jax.ShapeDtypeStruct((8, width), F32),
                   jax.ShapeDtypeStruct((1, width), F32)],
        scratch_shapes=[pltpu.VMEM((8, CONV_CB), F32)],
        compiler_params=_params(("parallel", "arbitrary"), 32 << 20),
    )(*args)


GW = SSD_R * SSD_P


def _expand8(v, passes=2):
    r = v.shape[0]
    if r < 8:
        v = jnp.broadcast_to(v, (8, SSD_R))
    ri = lax.broadcasted_iota(jnp.int32, (SSD_R, GW), 0)
    ci = lax.broadcasted_iota(jnp.int32, (SSD_R, GW), 1)
    spread = jnp.where((ci >= ri * SSD_P) & (ci < (ri + 1) * SSD_P), 1.0, 0.0)
    return _dot01(v, spread, passes)[:r]


def _head_pair_split(tile):
    first = lax.broadcasted_iota(jnp.int32, (1, 2 * SSD_P), 1) < SSD_P
    return jnp.where(first, tile, 0.0), jnp.where(first, 0.0, tile)


def _sel(rows, group):
    ri = lax.broadcasted_iota(jnp.int32, (rows, rows // group), 0)
    ci = lax.broadcasted_iota(jnp.int32, (rows, rows // group), 1)
    lo = ci * group
    return jnp.where((ri >= lo) & (ri < lo + group), 1.0, 0.0).astype(F32)


def _dot01(lhs, rhs, passes, split_lhs=True, dims=((1,), (0,))):
    val, m01 = (lhs, rhs) if split_lhs else (rhs, lhs)
    m01 = m01.astype(BF16)
    out = None
    for p in range(passes):
        piece = val.astype(BF16)
        ops = (piece, m01) if split_lhs else (m01, piece)
        d = lax.dot_general(ops[0], ops[1], (dims, ((), ())), preferred_element_type=F32)
        out = d if out is None else out + d
        if p + 1 < passes:
            val = val - piece.astype(F32)
    return out


def _ssd_chunk_terms(adt):
    li = lax.broadcasted_iota(jnp.int32, (CHUNK, CHUNK), 0)
    si = lax.broadcasted_iota(jnp.int32, (CHUNK, CHUNK), 1)
    causal = li >= si
    a_cs = _dot01(jnp.where(causal, 1.0, 0.0), adt, 3, split_lhs=False)
    a_cs_t = _dot01(adt, jnp.where(li <= si, 1.0, 0.0), 3, dims=((0,), (0,)))
    return a_cs, a_cs_t, causal


def _ssd_fwd(xc, dt3, adt3):
    t = xc.shape[0]
    nc = t // CHUNK

    gs = 2

    def body(xs_ref, b_ref, c_ref, dt_ref, adt_ref, y_ref, hp_ref, h_ref):
        @pl.when(pl.program_id(1) == 0)
        def _():
            h_ref[...] = jnp.zeros_like(h_ref)

        for gg in range(gs):
            a_cs, a_cs_t, causal = _ssd_chunk_terms(adt_ref[gg])
            a_last = a_cs[CHUNK - 1:CHUNK, :]
            h = h_ref[gg]
            hp_ref[gg, 0] = h
            xd = xs_ref[:, GW * gg:GW * (gg + 1)] * _expand8(dt_ref[gg])
            bb = b_ref[:, SSD_N * gg:SSD_N * (gg + 1)].astype(BF16)
            cbf = c_ref[:, SSD_N * gg:SSD_N * (gg + 1)].astype(BF16)
            cb = lax.dot_general(cbf, bb, (((1,), (1,)), ((), ())), preferred_element_type=F32)
            yoff = jnp.dot(cbf, h.astype(BF16), preferred_element_type=F32) * _expand8(jnp.exp(a_cs))
            for q in range(SSD_R // 2):
                lmats = []
                for r in (2 * q, 2 * q + 1):
                    seg = jnp.exp(jnp.where(causal, a_cs[:, r:r + 1] - a_cs_t[r:r + 1, :], -jnp.inf))
                    lmats.append((cb * seg).astype(BF16))
                tile = slice(2 * SSD_P * q, 2 * SSD_P * (q + 1))
                xa, xb = _head_pair_split(xd[:, tile])
                y_ref[:, GW * gg + 2 * SSD_P * q:GW * gg + 2 * SSD_P * (q + 1)] = (
                    jnp.dot(jnp.concatenate(lmats, axis=1), jnp.concatenate([xa, xb], axis=0).astype(BF16),
                            preferred_element_type=F32) + yoff[:, tile])
            xdd = (xd * _expand8(jnp.exp(a_last - a_cs))).astype(BF16)
            h_ref[gg] = _expand8(jnp.exp(a_last), 3) * h + lax.dot_general(
                bb, xdd, (((0,), (0,)), ((), ())), preferred_element_type=F32)

    nb = SSD_INNER // (gs * SSD_N)
    return pl.pallas_call(
        body, name="ssd_fwd", grid=(SSD_G // gs, nc),
        in_specs=[pl.BlockSpec((CHUNK, gs * GW), lambda g, c: (c, g)),
                  pl.BlockSpec((CHUNK, gs * SSD_N), lambda g, c: (c, nb + g)),
                  pl.BlockSpec((CHUNK, gs * SSD_N), lambda g, c: (c, nb + SSD_G // gs + g)),
                  pl.BlockSpec((gs, CHUNK, SSD_R), lambda g, c: (g, c, 0)),
                  pl.BlockSpec((gs, CHUNK, SSD_R), lambda g, c: (g, c, 0))],
        out_specs=[pl.BlockSpec((CHUNK, gs * GW), lambda g, c: (c, g)),
                   pl.BlockSpec((gs, 1, SSD_N, GW), lambda g, c: (g, c, 0, 0))],
        out_shape=[jax.ShapeDtypeStruct((t, SSD_INNER), F32), jax.ShapeDtypeStruct((SSD_G, nc, SSD_N, GW), F32)],
        scratch_shapes=[pltpu.VMEM((gs, SSD_N, GW), F32)],
        compiler_params=_params(("parallel", "arbitrary"), 32 << 20),
    )(xc, xc, xc, dt3, adt3)


def _ssd_bwd(xc, dt3, adt3, hprev, dy):
    t = xc.shape[0]
    nc = t // CHUNK

    def body(xs_ref, b_ref, c_ref, dt_ref, adt_ref, hp_ref, dy_ref,
             dx_ref, db_ref, dc_ref, dadt_ref, dxdx_ref, dh_ref):
        @pl.when(pl.program_id(1) == 0)
        def _():
            dh_ref[...] = jnp.zeros_like(dh_ref)

        a_cs, a_cs_t, causal = _ssd_chunk_terms(adt_ref[0])
        a_last = a_cs[CHUNK - 1:CHUNK, :]
        e_last = jnp.exp(a_last)
        ex = _expand8(jnp.exp(a_cs))
        dtex = _expand8(jnp.exp(a_last - a_cs))
        dtx = _expand8(dt_ref[0])
        xs = xs_ref[...]
        dyv = dy_ref[...]
        hp = hp_ref[0, 0]
        dh = dh_ref[...]
        sel = _sel(GW, SSD_P)
        seg8 = lambda v: _dot01(v, sel, 2)

        xd = xs * dtx
        xdd = xd * dtex
        bb = b_ref[...].astype(BF16)
        cbf = c_ref[...].astype(BF16)
        hpb = hp.astype(BF16)
        dhb = dh.astype(BF16)
        xdb = xd.astype(BF16)
        cb = lax.dot_general(cbf, bb, (((1,), (1,)), ((), ())), preferred_element_type=F32)
        dye = (dyv * ex).astype(BF16)
        yoff = jnp.dot(cbf, hpb, preferred_element_type=F32) * ex
        dc = lax.dot_general(dye, hpb, (((1,), (1,)), ((), ())), preferred_element_type=F32)
        d_a = seg8(dyv * yoff)
        bdh = jnp.dot(bb, dhb, preferred_element_type=F32)
        db = lax.dot_general(xdd.astype(BF16), dhb, (((1,), (1,)), ((), ())), preferred_element_type=F32)
        dxd_state = bdh * dtex
        q = seg8(xdd * bdh)
        d_a = d_a - q
        d_a_last = (jnp.sum(q, axis=0, keepdims=True)
                    + e_last * seg8(jnp.sum(hp * dh, axis=0, keepdims=True)))
        dh_ref[...] = (lax.dot_general(cbf, dye, (((0,), (0,)), ((), ())), preferred_element_type=F32)
                       + _expand8(e_last, 3) * dh)
        dcb = jnp.zeros((CHUNK, CHUNK), F32)
        w_all = []
        dxd_parts = []
        for q2 in range(SSD_R // 2):
            tile = slice(2 * SSD_P * q2, 2 * SSD_P * (q2 + 1))
            dy_pair = [part.astype(BF16) for part in _head_pair_split(dyv[:, tile])]
            lmats = []
            for k, r in enumerate((2 * q2, 2 * q2 + 1)):
                seg = jnp.exp(jnp.where(causal, a_cs[:, r:r + 1] - a_cs_t[r:r + 1, :], -jnp.inf))
                lmat = cb * seg
                dm = lax.dot_general(dy_pair[k], xdb[:, tile], (((1,), (1,)), ((), ())), preferred_element_type=F32)
                dcb = dcb + dm * seg
                w_all.append(dm * lmat)
                lmats.append(lmat.astype(BF16))
            dxd_parts.append(lax.dot_general(jnp.concatenate(lmats, axis=0), jnp.concatenate(dy_pair, axis=0),
                                             (((0,), (0,)), ((), ())), preferred_element_type=F32))
        row_sums = _dot01(jnp.concatenate(w_all, axis=1), _sel(SSD_R * CHUNK, CHUNK), 2)
        cs_rows = jnp.concatenate([jnp.sum(wr, axis=0, keepdims=True) for wr in w_all], axis=0)
        col_sums = _dot01(cs_rows, _sel(SSD_R, 1), 3, dims=((0,), (0,)))
        d_a = d_a + row_sums - col_sums
        li = lax.broadcasted_iota(jnp.int32, (CHUNK, SSD_R), 0)
        d_a = d_a + jnp.where(li == CHUNK - 1, d_a_last, 0.0)
        l2 = lax.broadcasted_iota(jnp.int32, (CHUNK, CHUNK), 0)
        s2 = lax.broadcasted_iota(jnp.int32, (CHUNK, CHUNK), 1)
        dadt_ref[0] = _dot01(jnp.where(s2 >= l2, 1.0, 0.0), d_a, 3, split_lhs=False)
        dxd = dxd_state + jnp.concatenate(dxd_parts, axis=1)
        dxdx_ref[0] = seg8(dxd * xs)
        dx_ref[...] = dxd * dtx
        dcbb = dcb.astype(BF16)
        db_ref[...] = db + lax.dot_general(dcbb, cbf, (((0,), (0,)), ((), ())), preferred_element_type=F32)
        dc_ref[...] = dc + jnp.dot(dcbb, bb, preferred_element_type=F32)

    nb = SSD_INNER // SSD_N
    rc = lambda g, c: (nc - 1 - c, g)
    r3 = lambda g, c: (g, nc - 1 - c, 0)
    return pl.pallas_call(
        body, name="ssd_bwd", grid=(SSD_G, nc),
        in_specs=[pl.BlockSpec((CHUNK, GW), rc),
                  pl.BlockSpec((CHUNK, SSD_N), lambda g, c: (nc - 1 - c, nb + g)),
                  pl.BlockSpec((CHUNK, SSD_N), lambda g, c: (nc - 1 - c, nb + SSD_G + g)),
                  pl.BlockSpec((1, CHUNK, SSD_R), r3),
                  pl.BlockSpec((1, CHUNK, SSD_R), r3),
                  pl.BlockSpec((1, 1, SSD_N, GW), lambda g, c: (g, nc - 1 - c, 0, 0)),
                  pl.BlockSpec((CHUNK, GW), rc)],
        out_specs=[pl.BlockSpec((CHUNK, GW), rc),
                   pl.BlockSpec((CHUNK, SSD_N), rc),
                   pl.BlockSpec((CHUNK, SSD_N), rc),
                   pl.BlockSpec((1, CHUNK, SSD_R), r3),
                   pl.BlockSpec((1, CHUNK, SSD_R), r3)],
        out_shape=[jax.ShapeDtypeStruct((t, SSD_INNER), F32),
                   jax.ShapeDtypeStruct((t, SSD_G * SSD_N), F32),
                   jax.ShapeDtypeStruct((t, SSD_G * SSD_N), F32),
                   jax.ShapeDtypeStruct((SSD_G, t, SSD_R), F32),
                   jax.ShapeDtypeStruct((SSD_G, t, SSD_R), F32)],
        scratch_shapes=[pltpu.VMEM((SSD_N, GW), F32)],
        compiler_params=_params(("parallel", "arbitrary"), 40 << 20),
    )(xc, xc, xc, dt3, adt3, hprev, dy)


def _gated_norm_fwd(y, xc, proj, dexp, ng):
    t = y.shape[0]
    tt = _divtile(t, 256)

    def body(y_ref, x_ref, z_ref, d_ref, g_ref, o_ref):
        z = z_ref[...]
        y2 = (y_ref[...] + d_ref[...] * x_ref[...]) * (z * _sigmoid(z))
        for gi in range(SSD_G):
            sl = slice(GW * gi, GW * (gi + 1))
            seg = y2[:, sl]
            rinv = lax.rsqrt(jnp.mean(seg * seg, axis=-1, keepdims=True) + RMS_EPS)
            o_ref[:, sl] = (seg * rinv * g_ref[:, sl]).astype(BF16)

    row = pl.BlockSpec((tt, SSD_INNER), lambda i: (i, 0))
    vec = pl.BlockSpec((1, SSD_INNER), lambda i: (0, 0))
    return pl.pallas_call(
        body, name="gated_norm_fwd", grid=(t // tt,), in_specs=[row, row, row, vec, vec], out_specs=row,
        out_shape=jax.ShapeDtypeStruct((t, SSD_INNER), BF16),
        compiler_params=_params(("parallel",), 48 << 20),
    )(y, xc, proj, dexp, ng)


def _gated_norm_bwd(dout, y, xc, proj, dexp, ng):
    t = y.shape[0]
    tt = _divtile(t, 128)

    def body(do_ref, y_ref, x_ref, z_ref, d_ref, g_ref, dz_ref, dy_ref, dg_ref, dd_ref):
        z = z_ref[...]
        sg = _sigmoid(z)
        sz = z * sg
        xs = x_ref[...]
        y1 = y_ref[...] + d_ref[...] * xs
        y2 = y1 * sz
        dov = do_ref[...]

        @pl.when(pl.program_id(0) == 0)
        def _():
            dg_ref[...] = jnp.zeros_like(dg_ref)
            dd_ref[...] = jnp.zeros_like(dd_ref)

        for gi in range(SSD_G):
            sl = slice(GW * gi, GW * (gi + 1))
            seg = y2[:, sl]
            rinv = lax.rsqrt(jnp.mean(seg * seg, axis=-1, keepdims=True) + RMS_EPS)
            yn = seg * rinv
            dsl = dov[:, sl]
            dg_ref[:, sl] += jnp.sum(dsl * yn, axis=0, keepdims=True)
            dyn = dsl * g_ref[:, sl]
            dy2 = rinv * (dyn - yn * jnp.mean(dyn * yn, axis=-1, keepdims=True))
            dz_ref[:, sl] = (dy2 * y1[:, sl] * (sg[:, sl] * (1.0 + z[:, sl] * (1.0 - sg[:, sl])))).astype(BF16)
            dy1 = dy2 * sz[:, sl]
            dy_ref[:, sl] = dy1
            dd_ref[:, sl] += jnp.sum(dy1 * xs[:, sl], axis=0, keepdims=True)

    row = pl.BlockSpec((tt, SSD_INNER), lambda i: (i, 0))
    vec = pl.BlockSpec((1, SSD_INNER), lambda i: (0, 0))
    return pl.pallas_call(
        body, name="gated_norm_bwd", grid=(t // tt,), in_specs=[row, row, row, row, vec, vec],
        out_specs=[row, row, vec, vec],
        out_shape=[jax.ShapeDtypeStruct((t, SSD_INNER), BF16), jax.ShapeDtypeStruct((t, SSD_INNER), F32),
                   jax.ShapeDtypeStruct((1, SSD_INNER), F32), jax.ShapeDtypeStruct((1, SSD_INNER), F32)],
        compiler_params=_params(("arbitrary",), 48 << 20),
    )(dout, y, xc, proj, dexp, ng)


def _fold_heads(v, name):
    def body(v_ref, o_ref):
        ri = lax.broadcasted_iota(jnp.int32, (SSD_INNER, 128), 0)
        ci = lax.broadcasted_iota(jnp.int32, (SSD_INNER, 128), 1)
        fold = jnp.where((ri >= ci * SSD_P) & (ri < (ci + 1) * SSD_P), 1.0, 0.0).astype(F32)
        o_ref[...] = jnp.dot(v_ref[...], fold, preferred_element_type=F32, precision=HI)

    return pl.pallas_call(body, name=name, out_shape=jax.ShapeDtypeStruct((1, 128), F32))(v)


Q_BLK = SEGS["q"][2] // QW
K_BLK = SEGS["k"][2] // KVW
V_BLK = SEGS["v"][2] // KVW


def _rope_tables(pos_ref, invf_ref, width):
    ang = pos_ref[...] * invf_ref[...]
    lane = lax.broadcasted_iota(jnp.int32, (1, 128), 1)
    sign = jnp.where((lane % HD) < (HD // 2), -1.0, 1.0)
    cos = jnp.tile(jnp.cos(ang), (1, width // 128))
    sin = jnp.tile(sign * jnp.sin(ang), (1, width // 128))
    first = (lax.broadcasted_iota(jnp.int32, (1, width), 1) % HD) < (HD // 2)
    return cos, sin, first


def _rot_half(u, first):
    w = u.shape[1]
    return jnp.where(first, pltpu.roll(u, w - HD // 2, 1), pltpu.roll(u, HD // 2, 1))


def _rope_fwd(proj, pos, invf):
    t = proj.shape[0]
    tt = _divtile(t, 512)

    def body(q_ref, k_ref, pos_ref, invf_ref, qo_ref, ko_ref):
        cos, sin, first = _rope_tables(pos_ref, invf_ref, QW)
        q = q_ref[...]
        qr = q * cos + _rot_half(q, first) * sin
        for p in range(QW // 128):
            qo_ref[128 * p:128 * (p + 1), :] = qr[:, 128 * p:128 * (p + 1)].T.astype(BF16)
        k = k_ref[...]
        ko_ref[...] = (k * cos[:, :KVW] + _rot_half(k, first[:, :KVW]) * sin[:, :KVW]).astype(BF16)

    return pl.pallas_call(
        body, name="rope_fwd", grid=(t // tt,),
        in_specs=[pl.BlockSpec((tt, QW), lambda i: (i, Q_BLK)), pl.BlockSpec((tt, KVW), lambda i: (i, K_BLK)),
                  pl.BlockSpec((tt, 1), lambda i: (i, 0)), pl.BlockSpec((1, 128), lambda i: (0, 0))],
        out_specs=[pl.BlockSpec((QW, tt), lambda i: (0, i)), pl.BlockSpec((tt, KVW), lambda i: (i, 0))],
        out_shape=[jax.ShapeDtypeStruct((QW, t), BF16), jax.ShapeDtypeStruct((t, KVW), BF16)],
        compiler_params=_params(("parallel",), 40 << 20),
    )(proj, proj, pos, invf)


def _rope_bwd(dqt, dk, pos, invf):
    t = dk.shape[0]
    tt = _divtile(t, 512)

    def body(dq_ref, dk_ref, pos_ref, invf_ref, qo_ref, ko_ref):
        cos, sin, first = _rope_tables(pos_ref, invf_ref, QW)
        q = jnp.concatenate([dq_ref[128 * p:128 * (p + 1), :].T for p in range(QW // 128)], axis=1)
        qo_ref[...] = (q * cos + _rot_half(q * sin, first)).astype(BF16)
        k = dk_ref[...]
        ko_ref[...] = (k * cos[:, :KVW] + _rot_half(k * sin[:, :KVW], first[:, :KVW])).astype(BF16)

    return pl.pallas_call(
        body, name="rope_bwd", grid=(t // tt,),
        in_specs=[pl.BlockSpec((QW, tt), lambda i: (0, i)), pl.BlockSpec((tt, KVW), lambda i: (i, 0)),
                  pl.BlockSpec((tt, 1), lambda i: (i, 0)), pl.BlockSpec((1, 128), lambda i: (0, 0))],
        out_specs=[pl.BlockSpec((tt, QW), lambda i: (i, 0)), pl.BlockSpec((tt, KVW), lambda i: (i, 0))],
        out_shape=[jax.ShapeDtypeStruct((t, QW), BF16), jax.ShapeDtypeStruct((t, KVW), BF16)],
        compiler_params=_params(("parallel",), 40 << 20),
    )(dqt, dk, pos, invf)


GQ = NQ // NKV
NT_DIMS = (((1,), (1,)), ((), ()))
TN_DIMS = (((0,), (0,)), ((), ()))


def _attn_heads(ref, j, dtype=None):
    out = jnp.concatenate([ref[HD * h:HD * (h + 1), :] for h in range(j * GQ, (j + 1) * GQ)], axis=1)
    return out if dtype is None else out.astype(dtype)


def _attn_sink_row(s_ref, j):
    return jnp.concatenate([jnp.broadcast_to(s_ref[:, h:h + 1], (1, WINDOW)) for h in range(j * GQ, (j + 1) * GQ)],
                           axis=1)


def _attn_mask(n):
    kr = lax.broadcasted_iota(jnp.int32, (2 * WINDOW, GQ * WINDOW), 0)
    qi = lax.broadcasted_iota(jnp.int32, (2 * WINDOW, GQ * WINDOW), 1) % WINDOW
    return (kr > qi) & (kr <= qi + WINDOW) & ((n > 0) | (kr >= WINDOW))


def _attn_probs(qgt, kk, sink, mask):
    s = jnp.where(mask, jnp.dot(kk, qgt, preferred_element_type=F32) * (HD ** -0.5), -jnp.inf)
    m = jnp.maximum(jnp.max(s, axis=0, keepdims=True), sink)
    p = jnp.exp(s - m)
    ps = jnp.exp(sink - m)
    inv = 1.0 / (jnp.sum(p, axis=0, keepdims=True) + ps)
    return p * inv, ps * inv


def _attn_fwd(qt, kr, proj, sinks):
    t = kr.shape[0]
    nb = t // WINDOW

    def body(q_ref, kc_ref, kp_ref, vc_ref, vp_ref, s_ref, o_ref):
        mask = _attn_mask(pl.program_id(0))
        for j in range(NKV):
            ks = slice(HD * j, HD * (j + 1))
            kk = jnp.concatenate([kp_ref[:, ks], kc_ref[:, ks]], axis=0)
            vv = jnp.concatenate([vp_ref[:, ks], vc_ref[:, ks]], axis=0).astype(BF16)
            pn, _ = _attn_probs(_attn_heads(q_ref, j), kk, _attn_sink_row(s_ref, j), mask)
            ot = lax.dot_general(vv, pn.astype(BF16), TN_DIMS, preferred_element_type=F32).astype(BF16)
            for g in range(GQ):
                h = j * GQ + g
                o_ref[HD * h:HD * (h + 1), :] = ot[:, WINDOW * g:WINDOW * (g + 1)]

    prev = lambda n: (jnp.maximum(n - 1, 0), 0)
    return pl.pallas_call(
        body, name="attn_fwd", grid=(nb,),
        in_specs=[pl.BlockSpec((QW, WINDOW), lambda n: (0, n)),
                  pl.BlockSpec((WINDOW, KVW), lambda n: (n, 0)), pl.BlockSpec((WINDOW, KVW), prev),
                  pl.BlockSpec((WINDOW, KVW), lambda n: (n, V_BLK)),
                  pl.BlockSpec((WINDOW, KVW), lambda n: (jnp.maximum(n - 1, 0), V_BLK)),
                  pl.BlockSpec((1, 128), lambda n: (0, 0))],
        out_specs=pl.BlockSpec((QW, WINDOW), lambda n: (0, n)),
        out_shape=jax.ShapeDtypeStruct((QW, t), BF16),
        compiler_params=_params(("parallel",), 24 << 20),
    )(qt, kr, kr, proj, proj, sinks)


def _attn_bwd(qt, kr, proj, sinks, dot_):
    t = kr.shape[0]
    nb = t // WINDOW

    def body(q_ref, kc_ref, kp_ref, vc_ref, vp_ref, s_ref, do_ref,
             dq_ref, dk_ref, dv_ref, ds_ref, dkc_ref, dvc_ref):
        i = pl.program_id(0)
        mask = _attn_mask(nb - 1 - i)

        @pl.when(i == 0)
        def _():
            dkc_ref[...] = jnp.zeros_like(dkc_ref)
            dvc_ref[...] = jnp.zeros_like(dvc_ref)
            ds_ref[...] = jnp.zeros_like(ds_ref)

        lane = lax.broadcasted_iota(jnp.int32, (1, 128), 1)
        ds_acc = jnp.zeros((1, 128), F32)
        for j in range(NKV):
            ks = slice(HD * j, HD * (j + 1))
            kk = jnp.concatenate([kp_ref[:, ks], kc_ref[:, ks]], axis=0)
            vv = jnp.concatenate([vp_ref[:, ks], vc_ref[:, ks]], axis=0).astype(BF16)
            qgt = _attn_heads(q_ref, j)
            pn, psn = _attn_probs(qgt, kk, _attn_sink_row(s_ref, j), mask)
            dogt = _attn_heads(do_ref, j)
            dp = jnp.dot(vv, dogt, preferred_element_type=F32)
            delta = jnp.sum(dp * pn, axis=0, keepdims=True)
            dsb = (pn * (dp - delta) * (HD ** -0.5)).astype(BF16)
            dsink = -psn * delta
            dqt = lax.dot_general(kk, dsb, TN_DIMS, preferred_element_type=F32)
            for g in range(GQ):
                h = j * GQ + g
                cols = slice(WINDOW * g, WINDOW * (g + 1))
                dq_ref[HD * h:HD * (h + 1), :] = dqt[:, cols]
                ds_acc = ds_acc + jnp.where(lane == h, jnp.sum(dsink[:, cols], axis=1, keepdims=True), 0.0)
            dkk = lax.dot_general(dsb, qgt, NT_DIMS, preferred_element_type=F32)
            dvv = lax.dot_general(pn.astype(BF16), dogt, NT_DIMS, preferred_element_type=F32)
            dk_ref[:, ks] = dkk[WINDOW:] + dkc_ref[:, ks]
            dv_ref[:, ks] = (dvv[WINDOW:] + dvc_ref[:, ks]).astype(BF16)
            dkc_ref[:, ks] = dkk[:WINDOW]
            dvc_ref[:, ks] = dvv[:WINDOW]
        ds_ref[...] += ds_acc

    cur = lambda i: (nb - 1 - i, 0)
    cur_t = lambda i: (0, nb - 1 - i)
    prev = lambda i: (jnp.maximum(nb - 2 - i, 0), 0)
    return pl.pallas_call(
        body, name="attn_bwd", grid=(nb,),
        in_specs=[pl.BlockSpec((QW, WINDOW), cur_t),
                  pl.BlockSpec((WINDOW, KVW), cur), pl.BlockSpec((WINDOW, KVW), prev),
                  pl.BlockSpec((WINDOW, KVW), lambda i: (nb - 1 - i, V_BLK)),
                  pl.BlockSpec((WINDOW, KVW), lambda i: (jnp.maximum(nb - 2 - i, 0), V_BLK)),
                  pl.BlockSpec((1, 128), lambda i: (0, 0)),
                  pl.BlockSpec((QW, WINDOW), cur_t)],
        out_specs=[pl.BlockSpec((QW, WINDOW), cur_t), pl.BlockSpec((WINDOW, KVW), cur),
                   pl.BlockSpec((WINDOW, KVW), cur), pl.BlockSpec((1, 128), lambda i: (0, 0))],
        out_shape=[jax.ShapeDtypeStruct((QW, t), F32), jax.ShapeDtypeStruct((t, KVW), F32),
                   jax.ShapeDtypeStruct((t, KVW), BF16), jax.ShapeDtypeStruct((1, 128), F32)],
        scratch_shapes=[pltpu.VMEM((WINDOW, KVW), F32), pltpu.VMEM((WINDOW, KVW), F32)],
        compiler_params=_params(("arbitrary",), 32 << 20),
    )(qt, kr, kr, proj, proj, sinks, dot_)


GS_BLK = SEGS["gs"][2] // D
GA_BLK = SEGS["ga"][2] // D


def _merge_fwd(ys, ya, proj):
    t = ys.shape[0]
    tt = _divtile(t, 256)

    def body(ys_ref, ya_ref, gs_ref, ga_ref, o_ref):
        o_ref[...] = (_sigmoid(gs_ref[...]) * ys_ref[...] + _sigmoid(ga_ref[...]) * ya_ref[...]).astype(BF16)

    row = pl.BlockSpec((tt, D), lambda i: (i, 0))
    return pl.pallas_call(
        body, name="merge_fwd", grid=(t // tt,),
        in_specs=[row, row, pl.BlockSpec((tt, D), lambda i: (i, GS_BLK)), pl.BlockSpec((tt, D), lambda i: (i, GA_BLK))],
        out_specs=row, out_shape=jax.ShapeDtypeStruct((t, D), BF16),
        compiler_params=_params(("parallel",), 32 << 20),
    )(ys, ya, proj, proj)


def _merge_bwd(dm, ys, ya, proj):
    t = ys.shape[0]
    tt = _divtile(t, 256)

    def body(dm_ref, ys_ref, ya_ref, gs_ref, ga_ref, dys_ref, dya_ref, dgs_ref, dga_ref):
        d = dm_ref[...]
        s = _sigmoid(gs_ref[...])
        a = _sigmoid(ga_ref[...])
        dys_ref[...] = (d * s).astype(BF16)
        dya_ref[...] = (d * a).astype(BF16)
        dgs_ref[...] = (d * ys_ref[...] * (s * (1.0 - s))).astype(BF16)
        dga_ref[...] = (d * ya_ref[...] * (a * (1.0 - a))).astype(BF16)

    row = pl.BlockSpec((tt, D), lambda i: (i, 0))
    return pl.pallas_call(
        body, name="merge_bwd", grid=(t // tt,),
        in_specs=[row, row, row, pl.BlockSpec((tt, D), lambda i: (i, GS_BLK)),
                  pl.BlockSpec((tt, D), lambda i: (i, GA_BLK))],
        out_specs=[row, row, row, row], out_shape=[jax.ShapeDtypeStruct((t, D), BF16)] * 4,
        compiler_params=_params(("parallel",), 40 << 20),
    )(dm, ys, ya, proj, proj)


def _pad128(v):
    return jnp.pad(v, ((0, 0), (0, 128 - v.shape[1])))


def _group_major(v):
    t = v.shape[0]
    return jnp.transpose(v[:, :SSD_HEADS].reshape(t, SSD_G, SSD_R), (1, 0, 2))


def _token_major(v3):
    t = v3.shape[1]
    return _pad128(jnp.transpose(v3, (1, 0, 2)).reshape(t, SSD_HEADS))


def _local_step(x, pos, target, w, small, fetch=None, early_grads=None):
    w = dict(w)
    xb = x.astype(BF16)
    gu1, a1 = _mm_swiglu(xb, w["gu1"], "ffn1_gu")
    if fetch is not None:
        w.update(fetch(1, a1))
    f1 = _mm(a1, w["d1"], "nn", F32, "ffn1_down", caps=(512, 1024, FFN_H))
    h1, h1b, xh1, rs1 = _ln_fwd(x, f1, small["ln1_g"], small["ln1_b"], 0.5, "ln1_fwd")
    if fetch is not None:
        w.update(fetch(2, h1b))
    proj = _mm(h1b, w["win"], "nn", F32, "proj", caps=(1024, 896, 2048))
    if fetch is not None:
        w.update(fetch(3, proj))
    bias128 = _pad128(small["dt_bias"])
    alog128 = _pad128(small["a_log"])
    dt, adt = _dt_prep(proj, bias128, alog128)
    dt3, adt3 = _group_major(dt), _group_major(adt)
    xc = _conv_fwd(proj, small["conv_w"], small["conv_b"])
    y_ssd, hprev = _ssd_fwd(xc, dt3, adt3)
    dexp = jnp.repeat(small["d_skip"], SSD_P, axis=1)
    ysn = _gated_norm_fwd(y_ssd, xc, proj, dexp, small["ssd_norm_g"])
    ys = _mm(ysn, w["so"], "nn", F32, "ssd_out")
    invf = jnp.tile(ROPE_THETA ** (-jnp.arange(HD // 2, dtype=F32) * 2.0 / HD), 4)[None, :]
    qt, kr = _rope_fwd(proj, pos, invf)
    sinks128 = _pad128(small["attn_sinks"])
    ot = _attn_fwd(qt, kr, proj, sinks128)
    ya = _mm(ot, w["ao"], "tn", F32, "attn_out")
    mg = _merge_fwd(ys, ya, proj)
    mix = _mm(mg, w["out"], "nn", F32, "mix_out")
    h2, h2b, xh2, rs2 = _ln_fwd(h1, mix, small["ln2_g"], small["ln2_b"], 1.0, "ln2_fwd")
    gu2, a2 = _mm_swiglu(h2b, w["gu2"], "ffn2_gu")
    f2 = _mm(a2, w["d2"], "nn", F32, "ffn2_down", caps=(512, 1024, FFN_H))
    _, _, xh3, rs3, dh3, loss = _ln_fwd(h2, f2, small["ln3_g"], small["ln3_b"], 0.5, "ln3_fwd", target=target)

    gw, gs = {}, {}
    dr3, dr3h, gs["ln3_g"], gs["ln3_b"] = _ln_bwd(dh3, xh3, rs3, small["ln3_g"], 0.5, "ln3_bwd")
    gw["d2"] = _mm(a2, dr3h, "tn", F32, "ffn2_down_dw")
    da2 = _mm(dr3h, w["d2"], "nt", BF16, "ffn2_down_dx", caps=(1024, 1408, 2048))
    dgu2 = _swiglu_bwd(gu2, da2, "ffn2_act_bwd")
    gw["gu2"] = _mm(h2b, dgu2, "tn", F32, "ffn2_gu_dw", caps=(1024, 1408, 2048), n_slabs=N_CHIPS)
    dh2 = _mm(dgu2, w["gu2"], "nt", F32, "ffn2_gu_dx", add=dr3, add_scale=ALPHA, caps=(1024, 1024, 2816))
    dr2, dr2b, gs["ln2_g"], gs["ln2_b"] = _ln_bwd(dh2, xh2, rs2, small["ln2_g"], 1.0, "ln2_bwd")
    gw["out"] = _mm(mg, dr2b, "tn", F32, "mix_out_dw")
    dmg = _mm(dr2b, w["out"], "nt", F32, "mix_out_dx")
    dys, dya, dgs, dga = _merge_bwd(dmg, ys, ya, proj)
    gw["ao"] = _mm(ot, dya, "nn", F32, "attn_out_dw")
    dot_ = _mm(w["ao"], dya, "nt", BF16, "attn_out_dx")
    dqt, dkr, dv, gs["attn_sinks"] = _attn_bwd(qt, kr, proj, sinks128, dot_)
    dq, dk = _rope_bwd(dqt, dkr, pos, invf)
    gw["so"] = _mm(ysn, dys, "tn", F32, "ssd_out_dw")
    dysn = _mm(dys, w["so"], "nt", F32, "ssd_out_dx")
    dz, dy1, gs["ssd_norm_g"], dd_ch = _gated_norm_bwd(dysn, y_ssd, xc, proj, dexp, small["ssd_norm_g"])
    gs["d_skip"] = _fold_heads(dd_ch, "d_skip_fold")
    dxs, db, dc, dadt3, dxdx3 = _ssd_bwd(xc, dt3, adt3, hprev, dy1)
    ddt, gs["dt_bias"], gs["a_log"] = _dt_bwd(_token_major(dadt3), _token_major(dxdx3), proj, bias128, alog128)
    cw, cbias = small["conv_w"], small["conv_b"]
    dux, dwx, dbx = _conv_bwd(proj, dxs, cw, cbias, 0, SSD_INNER, "conv_bwd_x", skip=(dy1, dexp))
    dub, dwb, dbb = _conv_bwd(proj, db, cw, cbias, SSD_INNER, SSD_G * SSD_N, "conv_bwd_b")
    duc, dwc, dbc = _conv_bwd(proj, dc, cw, cbias, SSD_INNER + SSD_G * SSD_N, SSD_G * SSD_N, "conv_bwd_c")
    gs["conv_w"] = jnp.concatenate([dwx[:4], dwb[:4], dwc[:4]], axis=1)
    gs["conv_b"] = jnp.concatenate([dbx, dbb, dbc], axis=1)
    dproj = jnp.concatenate([dz, dq, dgs, dga, dux, dub, duc, dk, dv, ddt], axis=1)
    gw["win"] = _mm(h1b, dproj, "tn", F32, "proj_dw", caps=(1024, 896, 2048))
    win = w["win"] if early_grads is None else early_grads[0](gw, w["win"])
    dh1 = _mm(dproj, win, "nt", F32, "proj_dx", add=dr2, add_scale=ALPHA, caps=(1024, 1024, 2432))
    ln1_g = small["ln1_g"]
    if early_grads is not None:
        ln1_g = ln1_g + early_grads[1](dh1)[0:1, 0:1]
    dr1, dr1h, gs["ln1_g"], gs["ln1_b"] = _ln_bwd(dh1, xh1, rs1, ln1_g, 0.5, "ln1_bwd")
    gw["d1"] = _mm(a1, dr1h, "tn", F32, "ffn1_down_dw")
    da1 = _mm(dr1h, w["d1"], "nt", BF16, "ffn1_down_dx", caps=(1024, 1408, 2048))
    dgu1 = _swiglu_bwd(gu1, da1, "ffn1_act_bwd")
    gw["gu1"] = _mm(xb, dgu1, "tn", F32, "ffn1_gu_dw", caps=(1024, 1408, 2048), n_slabs=N_CHIPS)
    grad_x = _mm(dgu1, w["gu1"], "nt", F32, "ffn1_gu_dx", add=dr1, add_scale=ALPHA, caps=(1024, 1024, 2816))
    return loss, grad_x, gw, gs


MESH = pl.DeviceIdType.MESH
ANY = pl.BlockSpec(memory_space=pl.ANY)


def _place():
    x, y, c = lax.axis_index("x"), lax.axis_index("y"), lax.axis_index("c")
    peers = [(1 - x, y), (x, 1 - y), (1 - x, 1 - y)]
    return x, y, c, peers


BIG = [
    ("ffn1_w_gate", D, SHARD_H, "gu1", "col", 0),
    ("ffn1_w_up", D, SHARD_H, "gu1", "col", SHARD_H),
    ("ffn1_w_down", SHARD_H, D, "d1", "row", 0),
    ("w_in", D, SHARD_IN, "win4", "lead", 0),
    ("w_ssd_o", SSD_INNER // N_CHIPS, D, "so", "row", 0),
    ("w_attn_o", D // N_CHIPS, D, "ao", "row", 0),
    ("w_out", D // N_CHIPS, D, "out", "row", 0),
    ("ffn2_w_gate", D, SHARD_H, "gu2", "col", 0),
    ("ffn2_w_up", D, SHARD_H, "gu2", "col", SHARD_H),
    ("ffn2_w_down", SHARD_H, D, "d2", "row", 0),
]
GATHERED = {"gu1": (D, 2 * FFN_H), "d1": (FFN_H, D), "win4": (N_CHIPS, D, SHARD_IN), "so": (SSD_INNER, D),
            "ao": (D, D), "out": (D, D), "gu2": (D, 2 * FFN_H), "d2": (FFN_H, D)}


def _cast_place(srcs, oname, chip_idx):
    rows, cols = srcs[0].shape
    tr = _divtile(rows, 256, 16)
    kind = [b[4] for b in BIG if b[3] == oname][0]

    def body(chip_ref, *refs):
        o_ref = refs[-1]
        for k, s_ref in enumerate(refs[:-1]):
            o_ref[:, k * cols:(k + 1) * cols] = s_ref[...].astype(BF16)

    nt = rows // tr
    if kind == "col":
        o_spec = pl.BlockSpec((tr, len(srcs) * cols), lambda i, chip_ref: (i, chip_ref[0]))
    elif kind == "row":
        o_spec = pl.BlockSpec((tr, cols), lambda i, chip_ref: (chip_ref[0] * nt + i, 0))
    else:
        o_spec = pl.BlockSpec((None, tr, cols), lambda i, chip_ref: (chip_ref[0], i, 0))
    return pl.pallas_call(
        body, name="cast_place_" + oname,
        grid_spec=pltpu.PrefetchScalarGridSpec(
            num_scalar_prefetch=1, grid=(nt,),
            in_specs=[pl.BlockSpec((tr, cols), lambda i, chip_ref: (i, 0))] * len(srcs), out_specs=o_spec),
        out_shape=jax.ShapeDtypeStruct(GATHERED[oname], BF16),
        compiler_params=_params(("parallel",), 32 << 20),
    )(chip_idx, *srcs)


def _slot(outs, entry, j, half):
    _, rows, cols, oname, kind, off = entry
    o = outs[oname]
    hr = rows // 2
    if kind == "col":
        cs = pl.ds(pl.multiple_of(j * (2 * SHARD_H) + off, 128), cols)
        return o.at[pl.ds(pl.multiple_of(half * hr, 16), hr), cs]
    if kind == "row":
        return o.at[pl.ds(pl.multiple_of(j * rows + half * hr, 16), hr), :]
    return o.at[j, pl.ds(pl.multiple_of(half * hr, 16), hr), :]


HBM = pl.BlockSpec(memory_space=pltpu.HBM)
SEM = pl.BlockSpec(memory_space=pltpu.SEMAPHORE)


def _ici_copy(outs, entry, j, c, to, send, recv, k):
    ref = _slot(outs, entry, j, c)
    return pltpu.make_async_remote_copy(src_ref=ref, dst_ref=ref, send_sem=send.at[k], recv_sem=recv.at[k],
                                        device_id=to, device_id_type=MESH)


GATHER_GROUPS = [["gu1"], ["d1"], ["win4"], ["so", "ao", "out", "gu2", "d2"]]


def _gather_ici_start(placed):
    names = [k for grp in GATHER_GROUPS for k in grp]
    bigs = [[b for b in BIG if b[3] in grp] for grp in GATHER_GROUPS]
    ng = len(GATHER_GROUPS)
    n_in = len(names)

    def body(*refs):
        sems = refs[n_in:n_in + 2 * ng]
        outs = dict(zip(names, refs[n_in + 2 * ng:n_in + 2 * ng + len(names)]))
        token = refs[-1]
        x, y, c, peers = _place()
        for gi, big in enumerate(bigs):
            for i, entry in enumerate(big):
                for k, (px, py) in enumerate(peers):
                    _ici_copy(outs, entry, 2 * x + y, c, (px, py, c), sems[2 * gi], sems[2 * gi + 1], 3 * i + k).start()
        token[...] = jnp.zeros_like(token)

    sem_shapes = [pltpu.SemaphoreType.DMA((3 * len(big),)) for big in bigs for _ in range(2)]
    res = pl.pallas_call(
        body, name="gather_ici_start",
        in_specs=[HBM] * n_in,
        out_specs=[SEM] * (2 * ng) + [HBM] * n_in + [pl.BlockSpec(memory_space=pltpu.VMEM)],
        out_shape=sem_shapes + [pltpu.HBM(GATHERED[k], BF16) for k in names] + [jax.ShapeDtypeStruct((8, 128), F32)],
        input_output_aliases={i: i + 2 * ng for i in range(n_in)},
        compiler_params=pltpu.CompilerParams(has_side_effects=pltpu.SideEffectType.DATAFLOW_SIDE_EFFECTING),
    )(*[pltpu.with_memory_space_constraint(placed[k], pltpu.HBM) for k in names])
    sems = [(res[2 * gi], res[2 * gi + 1]) for gi in range(ng)]
    return sems, dict(zip(names, res[2 * ng:2 * ng + n_in])), res[-1]


def _gather_ici_wait(send, recv, arrays, names, after, tag):
    big = [b for b in BIG if b[3] in names]

    def body(*refs):
        outs = dict(zip(names, refs[:len(names)]))
        send_ref, recv_ref = refs[len(names)], refs[len(names) + 1]
        x, y, c, peers = _place()
        for i, entry in enumerate(big):
            for k, (px, py) in enumerate(peers):
                mine = _ici_copy(outs, entry, 2 * x + y, c, (px, py, c), send_ref, recv_ref, 3 * i + k)
                mine.wait_send()
                theirs = _ici_copy(outs, entry, 2 * px + py, c, (px, py, c), send_ref, recv_ref, 3 * i + k)
                theirs.wait_recv()

    res = pl.pallas_call(
        body, name="gather_ici_wait_" + tag,
        in_specs=[HBM] * len(names) + [SEM, SEM, pl.BlockSpec(memory_space=pl.ANY)],
        out_specs=[HBM] * len(names),
        out_shape=[pltpu.HBM(GATHERED[k], BF16) for k in names],
        input_output_aliases={i: i for i in range(len(names))},
        compiler_params=pltpu.CompilerParams(has_side_effects=pltpu.SideEffectType.DATAFLOW_SIDE_EFFECTING),
    )(*[arrays[k] for k in names], send, recv, after)
    return dict(zip(names, res))


def _gather_d2d(arrays, names, tag):
    big = [b for b in BIG if b[3] in names]
    n = len(big)

    def body(*refs):
        outs = dict(zip(names, refs[len(names):2 * len(names)]))
        fsend, frecv = refs[2 * len(names):]
        x, y, c, peers = _place()
        cps = []
        for i, entry in enumerate(big):
            for k, (px, py) in enumerate(peers):
                cp = _ici_copy(outs, entry, 2 * px + py, c, (x, y, 1 - c), fsend, frecv, 3 * i + k)
                cp.start()
                cps.append(cp)
        for i, entry in enumerate(big):
            for k, (px, py) in enumerate(peers):
                _ici_copy(outs, entry, 2 * px + py, 1 - c, (x, y, 1 - c), fsend, frecv, 3 * i + k).wait_recv()
        for cp in cps:
            cp.wait_send()

    res = pl.pallas_call(
        body, name="gather_d2d_" + tag,
        in_specs=[ANY] * len(names), out_specs=[ANY] * len(names),
        out_shape=[jax.ShapeDtypeStruct(GATHERED[k], BF16) for k in names],
        input_output_aliases={i: i for i in range(len(names))},
        scratch_shapes=[pltpu.SemaphoreType.DMA((3 * n,))] * 2,
    )(*[arrays[k] for k in names])
    return dict(zip(names, res))


def _win_pieces():
    pieces = []
    for g0, wd, i0 in SEGS.values():
        for j in range(N_CHIPS):
            lo, hi = max(g0, j * SHARD_IN), min(g0 + wd, (j + 1) * SHARD_IN)
            if lo < hi:
                pieces.append((j, lo - j * SHARD_IN, hi - j * SHARD_IN, i0 + lo - g0))
    return pieces


def _win_to_internal(win4):
    tr = 128

    def body(i_ref, o_ref):
        for j, s0, s1, d0 in _win_pieces():
            o_ref[:, d0:d0 + s1 - s0] = i_ref[j, :, s0:s1]
        o_ref[:, PROJ_W:] = jnp.zeros((tr, PROJ_PAD - PROJ_W), o_ref.dtype)

    return pl.pallas_call(
        body, name="win_to_internal", grid=(D // tr,),
        in_specs=[pl.BlockSpec((N_CHIPS, tr, SHARD_IN), lambda i: (0, i, 0))],
        out_specs=pl.BlockSpec((tr, PROJ_PAD), lambda i: (i, 0)),
        out_shape=jax.ShapeDtypeStruct((D, PROJ_PAD), win4.dtype),
        compiler_params=_params(("parallel",), 40 << 20),
    )(win4)


def _win_from_internal(g):
    tr = 64

    def body(i_ref, o_ref):
        for j, s0, s1, d0 in _win_pieces():
            o_ref[j, :, s0:s1] = i_ref[:, d0:d0 + s1 - s0]

    return pl.pallas_call(
        body, name="win_from_internal", grid=(D // tr,),
        in_specs=[pl.BlockSpec((tr, PROJ_PAD), lambda i: (i, 0))],
        out_specs=pl.BlockSpec((N_CHIPS, tr, SHARD_IN), lambda i: (0, i, 0)),
        out_shape=jax.ShapeDtypeStruct((N_CHIPS, D, SHARD_IN), g.dtype),
        compiler_params=_params(("parallel",), 40 << 20),
    )(g)


def _rs_pair_exchange(grads, tag):
    n = len(grads)

    def body(*refs):
        srcs, dsts = refs[:n], refs[n:2 * n]
        send, recv = refs[2 * n:]
        x, y, c, _ = _place()
        cps = []
        for i in range(n):
            hr = srcs[i].shape[1] // 2
            cp = pltpu.make_async_remote_copy(
                src_ref=srcs[i].at[:, pl.ds(pl.multiple_of((1 - c) * hr, 16), hr), :], dst_ref=dsts[i],
                send_sem=send.at[i], recv_sem=recv.at[i], device_id=(x, y, 1 - c), device_id_type=MESH)
            cp.start()
            cps.append(cp)
        for cp in cps:
            cp.wait()

    return pl.pallas_call(
        body, name="rs_pair_exchange_" + tag, in_specs=[ANY] * n, out_specs=[ANY] * n,
        out_shape=[jax.ShapeDtypeStruct((g.shape[0], g.shape[1] // 2, g.shape[2]), F32) for g in grads],
        scratch_shapes=[pltpu.SemaphoreType.DMA((n,))] * 2,
    )(*grads)


def _pair_copy(src, dst, c, to, send, recv, k):
    hr = src.shape[1] // 2
    return pltpu.make_async_remote_copy(
        src_ref=src.at[:, pl.ds(pl.multiple_of((1 - c) * hr, 16), hr), :], dst_ref=dst,
        send_sem=send.at[k], recv_sem=recv.at[k], device_id=to, device_id_type=MESH)


def _rs_pair_start(grads, carried):
    n = len(grads)

    def body(*refs):
        send, recv = refs[2 * n + 1], refs[2 * n + 2]
        srcs, dsts = refs[2 * n + 3:3 * n + 3], refs[3 * n + 3:4 * n + 3]
        x, y, c, _ = _place()
        for i in range(n):
            _pair_copy(srcs[i], dsts[i], c, (x, y, 1 - c), send, recv, i).start()

    lands = [lax.empty((g.shape[0], g.shape[1] // 2, g.shape[2]), F32) for g in grads]
    res = pl.pallas_call(
        body, name="rs_pair_start",
        in_specs=[HBM] * (2 * n + 1), out_specs=[SEM, SEM] + [HBM] * (2 * n + 1),
        out_shape=[pltpu.SemaphoreType.DMA((n,)), pltpu.SemaphoreType.DMA((n,))]
        + [pltpu.HBM(g.shape, F32) for g in grads] + [pltpu.HBM(l.shape, F32) for l in lands]
        + [pltpu.HBM(carried.shape, carried.dtype)],
        input_output_aliases={i: i + 2 for i in range(2 * n + 1)},
        compiler_params=pltpu.CompilerParams(has_side_effects=pltpu.SideEffectType.DATAFLOW_SIDE_EFFECTING),
    )(*[pltpu.with_memory_space_constraint(a, pltpu.HBM) for a in list(grads) + lands + [carried]])
    return (res[0], res[1], list(res[2:2 + n]), list(res[2 + n:2 + 2 * n])), res[-1]


def _rs_pair_wait(send, recv, grads, lands, after):
    n = len(grads)

    def body(*refs):
        srcs, dsts = refs[:n], refs[n:2 * n]
        send_ref, recv_ref = refs[2 * n], refs[2 * n + 1]
        x, y, c, _ = _place()
        for i in range(n):
            cp = _pair_copy(srcs[i], dsts[i], c, (x, y, 1 - c), send_ref, recv_ref, i)
            cp.wait_send()
            cp.wait_recv()

    res = pl.pallas_call(
        body, name="rs_pair_wait",
        in_specs=[HBM] * (2 * n) + [SEM, SEM, pl.BlockSpec(memory_space=pl.ANY)],
        out_specs=[HBM] * (2 * n),
        out_shape=[pltpu.HBM(g.shape, F32) for g in grads] + [pltpu.HBM(l.shape, F32) for l in lands],
        input_output_aliases={i: i for i in range(2 * n)},
        compiler_params=pltpu.CompilerParams(has_side_effects=pltpu.SideEffectType.DATAFLOW_SIDE_EFFECTING),
    )(*grads, *lands, send, recv, after)
    return list(res[:n]), list(res[n:])


def _half_tile(hr):
    return _divtile(hr, 256, 16) if hr % 256 == 0 else _divtile(hr, 512, 16)


def _rs_pair_sum(g, r, c_idx, name):
    ns, rows, cols = g.shape
    hr = rows // 2
    tr = _half_tile(hr)
    nt = hr // tr

    def body(c_ref, g_ref, r_ref, ob_ref, of_ref):
        s = g_ref[...] + r_ref[...]
        ob_ref[...] = s.astype(BF16)
        of_ref[...] = s

    blk = pl.BlockSpec((None, tr, cols), lambda j, t, c_ref: (j, t, 0))
    return pl.pallas_call(
        body, name=name,
        grid_spec=pltpu.PrefetchScalarGridSpec(
            num_scalar_prefetch=1, grid=(ns, nt),
            in_specs=[pl.BlockSpec((None, tr, cols), lambda j, t, c_ref: (j, c_ref[0] * nt + t, 0)), blk],
            out_specs=[blk, blk]),
        out_shape=[jax.ShapeDtypeStruct((ns, hr, cols), BF16), jax.ShapeDtypeStruct((ns, hr, cols), F32)],
        compiler_params=_params(("parallel", "parallel"), 48 << 20),
    )(c_idx, g, r)


def _rs_chip_start(parts, tag):
    n = len(parts)

    def body(*refs):
        send, recv = refs[2 * n], refs[2 * n + 1]
        srcs, dsts = refs[2 * n + 2:3 * n + 2], refs[3 * n + 2:4 * n + 2]
        token = refs[-1]
        x, y, c, peers = _place()
        for i in range(n):
            for k, (px, py) in enumerate(peers):
                pltpu.make_async_remote_copy(
                    src_ref=srcs[i].at[2 * px + py], dst_ref=dsts[i].at[k],
                    send_sem=send.at[3 * i + k], recv_sem=recv.at[3 * i + k],
                    device_id=(px, py, c), device_id_type=MESH).start()
        token[...] = jnp.zeros_like(token)

    lands = [lax.empty((3,) + p.shape[1:], BF16) for p in parts]
    res = pl.pallas_call(
        body, name="rs_chip_start_" + tag,
        in_specs=[HBM] * (2 * n),
        out_specs=[SEM, SEM] + [HBM] * (2 * n) + [pl.BlockSpec(memory_space=pltpu.VMEM)],
        out_shape=[pltpu.SemaphoreType.DMA((3 * n,)), pltpu.SemaphoreType.DMA((3 * n,))]
        + [pltpu.HBM(p.shape, BF16) for p in parts] + [pltpu.HBM(l.shape, BF16) for l in lands]
        + [jax.ShapeDtypeStruct((8, 128), F32)],
        input_output_aliases={i: i + 2 for i in range(2 * n)},
        compiler_params=pltpu.CompilerParams(has_side_effects=pltpu.SideEffectType.DATAFLOW_SIDE_EFFECTING),
    )(*[pltpu.with_memory_space_constraint(a, pltpu.HBM) for a in list(parts) + lands])
    return res[0], res[1], list(res[2:2 + n]), list(res[2 + n:2 + 2 * n]), res[-1]


def _rs_chip_wait(send, recv, parts, lands, after, tag):
    n = len(parts)

    def body(*refs):
        srcs, dsts = refs[:n], refs[n:2 * n]
        send_ref, recv_ref = refs[2 * n], refs[2 * n + 1]
        x, y, c, peers = _place()
        for i in range(n):
            for k, (px, py) in enumerate(peers):
                cp = pltpu.make_async_remote_copy(
                    src_ref=srcs[i].at[2 * px + py], dst_ref=dsts[i].at[k],
                    send_sem=send_ref.at[3 * i + k], recv_sem=recv_ref.at[3 * i + k],
                    device_id=(px, py, c), device_id_type=MESH)
                cp.wait_send()
                cp.wait_recv()

    res = pl.pallas_call(
        body, name="rs_chip_wait_" + tag,
        in_specs=[HBM] * (2 * n) + [SEM, SEM, pl.BlockSpec(memory_space=pl.ANY)],
        out_specs=[HBM] * (2 * n),
        out_shape=[pltpu.HBM(p.shape, BF16) for p in parts] + [pltpu.HBM(l.shape, BF16) for l in lands],
        input_output_aliases={i: i for i in range(2 * n)},
        compiler_params=pltpu.CompilerParams(has_side_effects=pltpu.SideEffectType.DATAFLOW_SIDE_EFFECTING),
    )(*parts, *lands, send, recv, after)
    return list(res[n:])


def _rs_final_sum(own, got, chip_idx, c_idx, name):
    ns, hr, cols = own.shape
    tr = _half_tile(hr)
    nt = hr // tr

    def body(chip_ref, c_ref, o_ref, g_ref, out_ref):
        s = o_ref[...]
        for k in range(3):
            s = s + g_ref[k].astype(F32)
        out_ref[...] = s

    return pl.pallas_call(
        body, name=name,
        grid_spec=pltpu.PrefetchScalarGridSpec(
            num_scalar_prefetch=2, grid=(nt,),
            in_specs=[pl.BlockSpec((None, tr, cols), lambda t, chip_ref, c_ref: (chip_ref[0], t, 0)),
                      pl.BlockSpec((3, tr, cols), lambda t, chip_ref, c_ref: (0, t, 0))],
            out_specs=pl.BlockSpec((tr, cols), lambda t, chip_ref, c_ref: (c_ref[0] * nt + t, 0))),
        out_shape=jax.ShapeDtypeStruct((2 * hr, cols), F32),
        compiler_params=_params(("parallel",), 48 << 20),
    )(chip_idx, c_idx, own, got)


def _rs_share_halves(fulls, tag):
    n = len(fulls)

    def body(*refs):
        dsts = refs[n:2 * n]
        send, recv = refs[2 * n:]
        x, y, c, _ = _place()
        cps = []
        for i in range(n):
            hr = dsts[i].shape[0] // 2
            rows = dsts[i].at[pl.ds(pl.multiple_of(c * hr, 8), hr), :]
            cp = pltpu.make_async_remote_copy(src_ref=rows, dst_ref=rows, send_sem=send.at[i], recv_sem=recv.at[i],
                                              device_id=(x, y, 1 - c), device_id_type=MESH)
            cp.start()
            cps.append(cp)
        for i in range(n):
            hr = dsts[i].shape[0] // 2
            other = dsts[i].at[pl.ds(pl.multiple_of((1 - c) * hr, 8), hr), :]
            pltpu.make_async_remote_copy(src_ref=other, dst_ref=other, send_sem=send.at[i], recv_sem=recv.at[i],
                                         device_id=(x, y, 1 - c), device_id_type=MESH).wait_recv()
        for cp in cps:
            cp.wait_send()

    return pl.pallas_call(
        body, name="rs_share_halves_" + tag, in_specs=[ANY] * n, out_specs=[ANY] * n,
        out_shape=[jax.ShapeDtypeStruct(f.shape, F32) for f in fulls],
        input_output_aliases={i: i for i in range(n)},
        scratch_shapes=[pltpu.SemaphoreType.DMA((n,))] * 2,
    )(*fulls)


def _all_reduce_small(v):
    rows = v.shape[0]

    def body(v_ref, o_ref, buf, send, recv):
        x, y, c, _ = _place()
        me = 4 * x + 2 * y + c
        buf[me] = v_ref[...]
        cps = []
        for d in range(1, 8):
            px, py, pc = x ^ (d >> 2), y ^ ((d >> 1) & 1), c ^ (d & 1)
            cp = pltpu.make_async_remote_copy(src_ref=v_ref, dst_ref=buf.at[me], send_sem=send.at[d - 1],
                                              recv_sem=recv.at[d - 1], device_id=(px, py, pc), device_id_type=MESH)
            cp.start()
            cps.append(cp)
        for d in range(1, 8):
            px, py, pc = x ^ (d >> 2), y ^ ((d >> 1) & 1), c ^ (d & 1)
            pltpu.make_async_remote_copy(src_ref=v_ref, dst_ref=buf.at[4 * px + 2 * py + pc], send_sem=send.at[d - 1],
                                         recv_sem=recv.at[d - 1], device_id=(px, py, pc),
                                         device_id_type=MESH).wait_recv()
        for cp in cps:
            cp.wait_send()
        acc = buf[0]
        for d in range(1, 8):
            acc = acc + buf[d]
        o_ref[...] = acc

    vm = pl.BlockSpec(memory_space=pltpu.VMEM)
    return pl.pallas_call(
        body, name="all_reduce_small", in_specs=[vm], out_specs=vm,
        out_shape=jax.ShapeDtypeStruct((rows, 128), F32),
        scratch_shapes=[pltpu.VMEM((8, rows, 128), F32), pltpu.SemaphoreType.DMA((7,)), pltpu.SemaphoreType.DMA((7,))],
    )(v)


def _adamw(w, g, m, v, name, g_col_blk=0):
    rows, cols = w.shape
    tr = _divtile(rows, max(8, (2 << 20) // (4 * cols) // 8 * 8), 8)

    def body(w_ref, g_ref, m_ref, v_ref, go_ref, d_ref, mo_ref, vo_ref):
        gv = g_ref[...]
        mn = ADAM_B1 * m_ref[...] + (1.0 - ADAM_B1) * gv
        vn = ADAM_B2 * v_ref[...] + (1.0 - ADAM_B2) * (gv * gv)
        m_hat = mn / (1.0 - ADAM_B1 ** ADAM_STEP)
        v_hat = vn / (1.0 - ADAM_B2 ** ADAM_STEP)
        go_ref[...] = gv
        d_ref[...] = -ADAM_LR * (m_hat / (jnp.sqrt(v_hat) + ADAM_EPS) + ADAM_WD * w_ref[...])
        mo_ref[...] = mn
        vo_ref[...] = vn

    blk = pl.BlockSpec((tr, cols), lambda i: (i, 0))
    return pl.pallas_call(
        body, name=name, grid=(rows // tr,),
        in_specs=[blk, pl.BlockSpec((tr, cols), lambda i: (i, g_col_blk)), blk, blk],
        out_specs=[blk] * 4, out_shape=[jax.ShapeDtypeStruct((rows, cols), F32)] * 4,
        compiler_params=_params(("parallel",), 48 << 20),
    )(w, g, m, v)


SMALL = ["ln1_g", "ln1_b", "conv_w", "conv_b", "dt_bias", "a_log", "d_skip", "ssd_norm_g", "attn_sinks",
         "ln2_g", "ln2_b", "ln3_g", "ln3_b"]


def _pack_rows(vs):
    parts = []
    for v in vs:
        v = v.reshape(-1)
        parts.append(jnp.pad(v, (0, (-v.shape[0]) % 128)))
    flat = jnp.concatenate(parts)
    flat = jnp.pad(flat, (0, (-flat.shape[0]) % 1024))
    return flat.reshape(-1, 128)


def _unpack_rows(packed, shapes):
    flat = packed.reshape(-1)
    out, at = [], 0
    for s in shapes:
        nel = int(np.prod(s))
        out.append(flat[at:at + nel].reshape(s))
        at += nel + (-nel) % 128
    return out


def kernel(x, positions, ffn1_w_gate, ffn1_w_up, ffn1_w_down, ln1_g, ln1_b, w_in, conv_w, conv_b, dt_bias, a_log, d_skip, ssd_norm_g, w_ssd_o, attn_sinks, w_attn_o, w_out, ln2_g, ln2_b, ffn2_w_gate, ffn2_w_up, ffn2_w_down, ln3_g, ln3_b, loss_target, m_ffn1_w_gate, m_ffn1_w_up, m_ffn1_w_down, m_ln1_g, m_ln1_b, m_w_in, m_conv_w, m_conv_b, m_dt_bias, m_a_log, m_d_skip, m_ssd_norm_g, m_w_ssd_o, m_attn_sinks, m_w_attn_o, m_w_out, m_ln2_g, m_ln2_b, m_ffn2_w_gate, m_ffn2_w_up, m_ffn2_w_down, m_ln3_g, m_ln3_b, v_ffn1_w_gate, v_ffn1_w_up, v_ffn1_w_down, v_ln1_g, v_ln1_b, v_w_in, v_conv_w, v_conv_b, v_dt_bias, v_a_log, v_d_skip, v_ssd_norm_g, v_w_ssd_o, v_attn_sinks, v_w_attn_o, v_w_out, v_ln2_g, v_ln2_b, v_ffn2_w_gate, v_ffn2_w_up, v_ffn2_w_down, v_ln3_g, v_ln3_b):
    args = dict(locals())
    wts = {n: args[n][0] for n in [b[0] for b in BIG] + SMALL}
    mom_m = {n: args["m_" + n][0] for n in wts}
    mom_v = {n: args["v_" + n][0] for n in wts}
    t = x.shape[1]
    xi, yi, ci = lax.axis_index("x"), lax.axis_index("y"), lax.axis_index("c")
    chip = 2 * xi + yi

    c_idx = ci.astype(jnp.int32).reshape(1)
    chip_idx = chip.astype(jnp.int32).reshape(1)
    placed = {o: _cast_place([wts[b[0]] for b in BIG if b[3] == o], o, chip_idx) for o in GATHERED}
    g_sems, g_flight, g_token = _gather_ici_start(placed)

    def fetch(group, after):
        names = GATHER_GROUPS[group]
        send, recv = g_sems[group]
        landed = _gather_ici_wait(send, recv, {k: g_flight[k] for k in names}, names, after, str(group))
        got = _gather_d2d(landed, names, str(group))
        if "win4" in got:
            got["win"] = _win_to_internal(got.pop("win4"))
        return got

    w = fetch(0, g_token)

    def slabs_of(gw, names):
        view = {"gu1": lambda: gw["gu1"], "gu2": lambda: gw["gu2"],
                "d1": lambda: gw["d1"].reshape(N_CHIPS, SHARD_H, D), "d2": lambda: gw["d2"].reshape(N_CHIPS, SHARD_H, D),
                "win": lambda: _win_from_internal(gw["win"]),
                "so": lambda: gw["so"].reshape(N_CHIPS, SSD_INNER // N_CHIPS, D),
                "ao": lambda: gw["ao"].reshape(N_CHIPS, D // N_CHIPS, D),
                "out": lambda: gw["out"].reshape(N_CHIPS, D // N_CHIPS, D)}
        return [view[nm]() for nm in names]

    early = ["win", "so", "ao", "out", "gu2", "d2"]
    late = ["gu1", "d1"]
    flight = {}

    def early_start(gw, win):
        flight["pair"], win = _rs_pair_start(slabs_of(gw, early), win)
        return win

    def early_mid(dh1):
        slabs, from_sib = _rs_pair_wait(*flight["pair"], dh1)
        pair = [_rs_pair_sum(g, r, c_idx, "rs_pair_sum_" + nm) for g, r, nm in zip(slabs, from_sib, early)]
        send, recv, parts, lands, token = _rs_chip_start([p[0] for p in pair], "early")
        flight.update(send=send, recv=recv, parts=parts, lands=lands, own=[p[1] for p in pair])
        return token

    early_grads = (early_start, early_mid)
    cw_rows = _pack_rows([lax.dynamic_update_slice(jnp.zeros((4, XBC), F32), wts["conv_w"], (0, chip * (XBC // N_CHIPS)))])
    cw_rows = jnp.where(ci == 0, cw_rows, 0.0)
    conv_w_full = _all_reduce_small(cw_rows)[:4 * XBC // 128].reshape(4, XBC)

    small = {n: (wts[n][None, :] if wts[n].ndim == 1 else wts[n]) for n in SMALL}
    small["conv_w"] = conv_w_full
    loss, grad_x, gw, gs = _local_step(x[0], positions[0].astype(F32)[:, None], loss_target[0], w, small,
                                       fetch=fetch, early_grads=early_grads)

    slabs = slabs_of(gw, late)
    from_sib = _rs_pair_exchange(slabs, "late")
    pair = [_rs_pair_sum(g, r, c_idx, "rs_pair_sum_" + nm) for g, r, nm in zip(slabs, from_sib, late)]
    l_send, l_recv, l_parts, l_lands, l_token = _rs_chip_start([p[0] for p in pair], "late")
    got_early = _rs_chip_wait(flight["send"], flight["recv"], flight["parts"], flight["lands"], l_token, "early")

    outs = {}
    big_src = {"ffn1_w_gate": ("gu1", 0), "ffn1_w_up": ("gu1", 1), "ffn1_w_down": ("d1", 0), "w_in": ("win", 0),
               "w_ssd_o": ("so", 0), "w_attn_o": ("ao", 0), "w_out": ("out", 0),
               "ffn2_w_gate": ("gu2", 0), "ffn2_w_up": ("gu2", 1), "ffn2_w_down": ("d2", 0)}

    def finish(names, own, got, tag):
        halves = [_rs_final_sum(o, gt, chip_idx, c_idx, "rs_final_sum_" + nm) for o, gt, nm in zip(own, got, names)]
        full = dict(zip(names, _rs_share_halves(halves, tag)))
        for nm, (src, blk) in big_src.items():
            if src in full:
                outs[nm] = _adamw(wts[nm], full[src], mom_m[nm], mom_v[nm], "adamw_" + nm, g_col_blk=blk)

    finish(early, flight["own"], got_early, "early")
    got_late = _rs_chip_wait(l_send, l_recv, l_parts, l_lands, outs["w_in"][1], "late")
    finish(late, [p[1] for p in pair], got_late, "late")

    gvec = {n: gs[n] for n in SMALL}
    gvec["dt_bias"], gvec["a_log"], gvec["d_skip"] = gs["dt_bias"][:, :64], gs["a_log"][:, :64], gs["d_skip"][:, :64]
    gvec["attn_sinks"] = gs["attn_sinks"][:, :NQ]
    red = _all_reduce_small(_pack_rows([gvec[n] for n in SMALL] + [loss]))
    shapes = [(4, XBC) if n == "conv_w" else wts[n].shape for n in SMALL] + [(1,)]
    red_list = _unpack_rows(red, shapes)
    loss_out = red_list[-1].reshape(())
    gsm = dict(zip(SMALL, red_list[:-1]))
    gsm["conv_w"] = lax.dynamic_slice_in_dim(gsm["conv_w"], chip * (XBC // N_CHIPS), XBC // N_CHIPS, axis=1)
    sm_shapes = [wts[n].shape for n in SMALL]
    res = _adamw(_pack_rows([wts[n] for n in SMALL]), _pack_rows([gsm[n] for n in SMALL]),
                 _pack_rows([mom_m[n] for n in SMALL]), _pack_rows([mom_v[n] for n in SMALL]), "adamw_small")
    res = [_unpack_rows(r, sm_shapes) for r in res]
    for i, nm in enumerate(SMALL):
        outs[nm] = tuple(r[i] for r in res)

    order = ["ffn1_w_gate", "ffn1_w_up", "ffn1_w_down", "ln1_g", "ln1_b", "w_in", "conv_w", "conv_b", "dt_bias", "a_log",
             "d_skip", "ssd_norm_g", "w_ssd_o", "attn_sinks", "w_attn_o", "w_out", "ln2_g", "ln2_b",
             "ffn2_w_gate", "ffn2_w_up", "ffn2_w_down", "ln3_g", "ln3_b"]
    result = [loss_out, grad_x[None]]
    for kind in range(4):
        result += [outs[nm][kind][None] for nm in order]
    return tuple(result)
```

```python
import functools
import math

import numpy as np
import jax
import jax.numpy as jnp
from jax import lax
from jax.experimental import pallas as pl
from jax.experimental.pallas import tpu as pltpu

F32 = jnp.float32
BF16 = jnp.bfloat16
HI = lax.Precision.HIGHEST

D = 2048
FFN_H = 5632
SSD_INNER = 4096
SSD_HEADS = 64
SSD_P = 64
SSD_G = 8
SSD_R = 8
SSD_N = 128
CHUNK = 128
XBC = 6144
NQ = 32
NKV = 4
HD = 64
QW = 2048
KVW = 256
WINDOW = 128
ROPE_THETA = 10000.0
ALPHA = 2.0 ** 0.25
LN_EPS = 1e-5
RMS_EPS = 1e-5
PROJ_W = 16960
N_CHIPS = 4
SHARD_IN = PROJ_W // N_CHIPS
SHARD_H = FFN_H // N_CHIPS

SEGS = {
    "z": (0, 4096, 0),
    "xbc": (4096, 6144, 10240),
    "dt": (10240, 64, 16896),
    "q": (10304, 2048, 4096),
    "k": (12352, 256, 16384),
    "v": (12608, 256, 16640),
    "gs": (12864, 2048, 6144),
    "ga": (14912, 2048, 8192),
}
PROJ_PAD = 17024

ADAM_LR = 0.001
ADAM_B1 = 0.9
ADAM_B2 = 0.999
ADAM_EPS = 1e-08
ADAM_WD = 0.01
ADAM_STEP = 10

VMEM_CAP = 60 * 1024 * 1024


def _params(sem, vmem_bytes):
    return pltpu.CompilerParams(dimension_semantics=sem, vmem_limit_bytes=int(min(VMEM_CAP, vmem_bytes)))


def _divtile(n, cap, q=128):
    best = None
    for d in range(q, min(n, cap) + 1, q):
        if n % d == 0:
            best = d
    return n if best is None else best


def _sigmoid(x):
    return 0.5 * jnp.tanh(0.5 * x) + 0.5


def _mm(a, b, mode, out_dtype, name, add=None, add_scale=1.0, caps=(1024, 1024, 2048), n_slabs=1):
    if mode == "nn":
        (m, k), (k2, n) = a.shape, b.shape
    elif mode == "nt":
        (m, k), (n, k2) = a.shape, b.shape
    else:
        (k, m), (k2, n) = a.shape, b.shape
    assert k == k2, (a.shape, b.shape, mode)
    tm, tn, tk = _divtile(m, caps[0]), _divtile(n // n_slabs, caps[1]), _divtile(k, caps[2])
    nk = k // tk
    per_slab = n // n_slabs // tn
    dims = {"nn": ((1,), (0,)), "nt": ((1,), (1,)), "tn": ((0,), (0,))}[mode]
    has_add = add is not None

    def body(*refs):
        if has_add:
            a_ref, b_ref, add_ref, o_ref = refs[:4]
            scr = refs[4:]
        else:
            a_ref, b_ref, o_ref = refs[:3]
            add_ref = None
            scr = refs[3:]
        part = lax.dot_general(a_ref[...].astype(BF16), b_ref[...].astype(BF16), (dims, ((), ())),
                               preferred_element_type=F32)

        def finish(acc):
            if has_add:
                acc = acc + add_scale * add_ref[...].astype(F32)
            o_ref[...] = acc.astype(o_ref.dtype)

        if nk == 1:
            finish(part)
        else:
            acc_ref = scr[0]
            kk = pl.program_id(2)

            @pl.when(kk == 0)
            def _():
                acc_ref[...] = part

            @pl.when(kk > 0)
            def _():
                acc_ref[...] += part

            @pl.when(kk == nk - 1)
            def _():
                finish(acc_ref[...])

    if mode == "nn":
        a_spec = pl.BlockSpec((tm, tk), lambda i, j, kk: (i, kk))
        b_spec = pl.BlockSpec((tk, tn), lambda i, j, kk: (kk, j))
    elif mode == "nt":
        a_spec = pl.BlockSpec((tm, tk), lambda i, j, kk: (i, kk))
        b_spec = pl.BlockSpec((tn, tk), lambda i, j, kk: (j, kk))
    else:
        a_spec = pl.BlockSpec((tk, tm), lambda i, j, kk: (kk, i))
        b_spec = pl.BlockSpec((tk, tn), lambda i, j, kk: (kk, j))
    o_spec = pl.BlockSpec((tm, tn), lambda i, j, kk: (i, j))
    out_shape = jax.ShapeDtypeStruct((m, n), out_dtype)
    if n_slabs > 1:
        assert not has_add
        o_spec = pl.BlockSpec((None, tm, tn), lambda i, j, kk: (j // per_slab, i, j % per_slab))
        out_shape = jax.ShapeDtypeStruct((n_slabs, m, n // n_slabs), out_dtype)
    in_specs = [a_spec, b_spec] + ([o_spec] if has_add else [])
    args = (a, b) + ((add,) if has_add else ())
    osz = jnp.dtype(out_dtype).itemsize
    vmem = (2 * (tm * tk * a.dtype.itemsize + tk * tn * b.dtype.itemsize) + 2 * tm * tn * osz
            + (2 * tm * tn * add.dtype.itemsize if has_add else 0) + 2 * tm * tn * 4
            + 2 * (tm * tk + tk * tn) + (8 << 20))
    return pl.pallas_call(
        body, name=name, grid=(m // tm, n // tn, nk),
        in_specs=in_specs, out_specs=o_spec, out_shape=out_shape,
        scratch_shapes=[pltpu.VMEM((tm, tn), F32)] if nk > 1 else [],
        compiler_params=_params(("parallel", "parallel", "arbitrary"), vmem),
    )(*args)


def _mm_swiglu(a, b, name):
    m, k = a.shape
    w = SHARD_H
    tm = _divtile(m, 512)

    def body(a_ref, b_ref, gu_ref, act_ref):
        gu = jnp.dot(a_ref[...], b_ref[...], preferred_element_type=F32)
        g = gu[:, :w]
        gu_ref[...] = gu.astype(BF16)
        act_ref[...] = (g * _sigmoid(g) * gu[:, w:]).astype(BF16)

    return pl.pallas_call(
        body, name=name, grid=(N_CHIPS, m // tm),
        in_specs=[pl.BlockSpec((tm, k), lambda j, i: (i, 0)), pl.BlockSpec((k, 2 * w), lambda j, i: (0, j))],
        out_specs=[pl.BlockSpec((tm, 2 * w), lambda j, i: (i, j)), pl.BlockSpec((tm, w), lambda j, i: (i, j))],
        out_shape=[jax.ShapeDtypeStruct((m, 2 * FFN_H), BF16), jax.ShapeDtypeStruct((m, FFN_H), BF16)],
        compiler_params=_params(("parallel", "parallel"), 56 << 20),
    )(a, b)


def _swiglu_bwd(gu, da, name):
    t = gu.shape[0]
    tt = _divtile(t, 512)
    w = SHARD_H

    def body(gu_ref, da_ref, o_ref):
        g = gu_ref[:, :w].astype(F32)
        u = gu_ref[:, w:].astype(F32)
        d = da_ref[...].astype(F32)
        s = _sigmoid(g)
        o_ref[:, :w] = (d * u * (s * (1.0 + g * (1.0 - s)))).astype(BF16)
        o_ref[:, w:] = (d * (g * s)).astype(BF16)

    return pl.pallas_call(
        body, name=name, grid=(t // tt, N_CHIPS),
        in_specs=[pl.BlockSpec((tt, 2 * w), lambda i, j: (i, j)), pl.BlockSpec((tt, w), lambda i, j: (i, j))],
        out_specs=pl.BlockSpec((tt, 2 * w), lambda i, j: (i, j)),
        out_shape=jax.ShapeDtypeStruct((t, 2 * FFN_H), BF16),
        compiler_params=_params(("parallel", "parallel"), 40 << 20),
    )(gu, da)


def _ln_fwd(base, f, g, b, c, name, target=None):
    t = base.shape[0]
    tt = _divtile(t, 256)
    with_loss = target is not None

    def body(*refs):
        if with_loss:
            base_ref, f_ref, g_ref, b_ref, tg_ref, h_ref, hb_ref, xh_ref, rs_ref, dh_ref, loss_ref = refs
        else:
            base_ref, f_ref, g_ref, b_ref, h_ref, hb_ref, xh_ref, rs_ref = refs
        r = ALPHA * base_ref[...] + c * f_ref[...]
        mu = jnp.mean(r, axis=-1, keepdims=True)
        xc = r - mu
        var = jnp.mean(xc * xc, axis=-1, keepdims=True)
        rstd = lax.rsqrt(var + LN_EPS)
        xh = xc * rstd
        h = xh * g_ref[...] + b_ref[...]
        h_ref[...] = h
        hb_ref[...] = h.astype(BF16)
        xh_ref[...] = xh
        rs_ref[...] = rstd
        if with_loss:
            e = h - tg_ref[...]
            dh_ref[...] = e * (1.0 / D)
            part = 0.5 * jnp.sum(jnp.sum(e * e, axis=-1, keepdims=True) * (1.0 / D), axis=0, keepdims=True)

            @pl.when(pl.program_id(0) == 0)
            def _():
                loss_ref[...] = jnp.zeros_like(loss_ref)

            loss_ref[...] += part

    row = pl.BlockSpec((tt, D), lambda i: (i, 0))
    vec = pl.BlockSpec((1, D), lambda i: (0, 0))
    col = pl.BlockSpec((tt, 1), lambda i: (i, 0))
    in_specs = [row, row, vec, vec] + ([row] if with_loss else [])
    out_specs = [row, row, row, col] + ([row, pl.BlockSpec((1, 1), lambda i: (0, 0))] if with_loss else [])
    out_shape = [jax.ShapeDtypeStruct((t, D), F32), jax.ShapeDtypeStruct((t, D), BF16),
                 jax.ShapeDtypeStruct((t, D), F32), jax.ShapeDtypeStruct((t, 1), F32)]
    if with_loss:
        out_shape += [jax.ShapeDtypeStruct((t, D), F32), jax.ShapeDtypeStruct((1, 1), F32)]
    args = (base, f, g, b) + ((target,) if with_loss else ())
    return pl.pallas_call(
        body, name=name, grid=(t // tt,), in_specs=in_specs, out_specs=out_specs, out_shape=out_shape,
        compiler_params=_params(("arbitrary",) if with_loss else ("parallel",), 48 << 20),
    )(*args)


def _ln_bwd(dy, xh, rstd, g, c, name):
    t = dy.shape[0]
    tt = _divtile(t, 256)

    def body(dy_ref, xh_ref, rs_ref, g_ref, dr_ref, drb_ref, dg_ref, db_ref):
        dyv = dy_ref[...]
        xhv = xh_ref[...]
        dxh = dyv * g_ref[...]
        m1 = jnp.mean(dxh, axis=-1, keepdims=True)
        m2 = jnp.mean(dxh * xhv, axis=-1, keepdims=True)
        dr = rs_ref[...] * (dxh - m1 - xhv * m2)
        dr_ref[...] = dr
        drb_ref[...] = (c * dr).astype(BF16)

        @pl.when(pl.program_id(0) == 0)
        def _():
            dg_ref[...] = jnp.zeros_like(dg_ref)
            db_ref[...] = jnp.zeros_like(db_ref)

        dg_ref[...] += jnp.sum(dyv * xhv, axis=0, keepdims=True)
        db_ref[...] += jnp.sum(dyv, axis=0, keepdims=True)

    row = pl.BlockSpec((tt, D), lambda i: (i, 0))
    vec = pl.BlockSpec((1, D), lambda i: (0, 0))
    col = pl.BlockSpec((tt, 1), lambda i: (i, 0))
    return pl.pallas_call(
        body, name=name, grid=(t // tt,), in_specs=[row, row, col, vec], out_specs=[row, row, vec, vec],
        out_shape=[jax.ShapeDtypeStruct((t, D), F32), jax.ShapeDtypeStruct((t, D), BF16),
                   jax.ShapeDtypeStruct((1, D), F32), jax.ShapeDtypeStruct((1, D), F32)],
        compiler_params=_params(("arbitrary",), 40 << 20),
    )(dy, xh, rstd, g)


DT_BLK = SEGS["dt"][2] // 128


def _dt_prep(proj, bias128, alog128):
    t = proj.shape[0]
    tt = _divtile(t, 1024)

    def body(p_ref, bias_ref, alog_ref, dt_ref, adt_ref):
        dtv = jax.nn.softplus(p_ref[...] + bias_ref[...])
        dt_ref[...] = dtv
        adt_ref[...] = dtv * (-jnp.exp(alog_ref[...]))

    blk = pl.BlockSpec((tt, 128), lambda i: (i, 0))
    vec = pl.BlockSpec((1, 128), lambda i: (0, 0))
    return pl.pallas_call(
        body, name="dt_prep", grid=(t // tt,),
        in_specs=[pl.BlockSpec((tt, 128), lambda i: (i, DT_BLK)), vec, vec], out_specs=[blk, blk],
        out_shape=[jax.ShapeDtypeStruct((t, 128), F32)] * 2,
        compiler_params=_params(("parallel",), 16 << 20),
    )(proj, bias128, alog128)


def _dt_bwd(dadt, dxdx, proj, bias128, alog128):
    t = proj.shape[0]
    tt = _divtile(t, 1024)

    def body(dadt_ref, dxdx_ref, p_ref, bias_ref, alog_ref, o_ref, dbias_ref, dalog_ref):
        pre = p_ref[...] + bias_ref[...]
        dtv = jax.nn.softplus(pre)
        a = -jnp.exp(alog_ref[...])
        ddt = a * dadt_ref[...] + dxdx_ref[...]
        draw = ddt * _sigmoid(pre)
        o_ref[...] = draw.astype(BF16)

        @pl.when(pl.program_id(0) == 0)
        def _():
            dbias_ref[...] = jnp.zeros_like(dbias_ref)
            dalog_ref[...] = jnp.zeros_like(dalog_ref)

        dbias_ref[...] += jnp.sum(draw, axis=0, keepdims=True)
        dalog_ref[...] += jnp.sum(dadt_ref[...] * dtv * a, axis=0, keepdims=True)

    blk = pl.BlockSpec((tt, 128), lambda i: (i, 0))
    vec = pl.BlockSpec((1, 128), lambda i: (0, 0))
    return pl.pallas_call(
        body, name="dt_bwd", grid=(t // tt,),
        in_specs=[blk, blk, pl.BlockSpec((tt, 128), lambda i: (i, DT_BLK)), vec, vec],
        out_specs=[blk, vec, vec],
        out_shape=[jax.ShapeDtypeStruct((t, 128), BF16), jax.ShapeDtypeStruct((1, 128), F32),
                   jax.ShapeDtypeStruct((1, 128), F32)],
        compiler_params=_params(("arbitrary",), 16 << 20),
    )(dadt, dxdx, proj, bias128, alog128)


CONV_CB = 512
CONV_TT = 512


def _shift_down(cur, prev8, s):
    if s == 0:
        return cur
    rolled = pltpu.roll(cur, s, 0)
    head = pltpu.roll(prev8, s, 0)
    r8 = lax.broadcasted_iota(jnp.int32, (8, 1), 0)
    top = jnp.where(r8 < s, head, rolled[:8])
    return jnp.concatenate([top, rolled[8:]], axis=0)


def _shift_up(cur, next8, s):
    if s == 0:
        return cur
    n = cur.shape[0]
    rolled = pltpu.roll(cur, n - s, 0)
    tail = pltpu.roll(next8, 8 - s, 0)
    r8 = lax.broadcasted_iota(jnp.int32, (8, 1), 0)
    bot = jnp.where(r8 >= 8 - s, tail, rolled[n - 8:])
    return jnp.concatenate([rolled[:n - 8], bot], axis=0)


def _conv_fwd(proj, conv_w, conv_b):
    t = proj.shape[0]
    tt = _divtile(t, CONV_TT)
    base = SEGS["xbc"][2] // CONV_CB
    r8 = tt // 8

    def body(u_ref, up_ref, w_ref, b_ref, o_ref):
        cur = u_ref[...]
        prev8 = jnp.where(pl.program_id(1) > 0, up_ref[...], 0.0)
        acc = b_ref[...] + w_ref[3:4, :] * cur
        for k in range(3):
            acc = acc + w_ref[k:k + 1, :] * _shift_down(cur, prev8, 3 - k)
        o_ref[...] = acc * _sigmoid(acc)

    return pl.pallas_call(
        body, name="conv_fwd", grid=(XBC // CONV_CB, t // tt),
        in_specs=[pl.BlockSpec((tt, CONV_CB), lambda c, i: (i, base + c)),
                  pl.BlockSpec((8, CONV_CB), lambda c, i: (jnp.maximum(i * r8 - 1, 0), base + c)),
                  pl.BlockSpec((4, CONV_CB), lambda c, i: (0, c)),
                  pl.BlockSpec((1, CONV_CB), lambda c, i: (0, c))],
        out_specs=pl.BlockSpec((tt, CONV_CB), lambda c, i: (i, c)),
        out_shape=jax.ShapeDtypeStruct((t, XBC), F32),
        compiler_params=_params(("parallel", "parallel"), 24 << 20),
    )(proj, proj, conv_w, conv_b)


def _conv_bwd(proj, dout, conv_w, conv_b, col0, width, name, skip=None):
    t = proj.shape[0]
    tt = _divtile(t, CONV_TT)
    nt = t // tt
    base = SEGS["xbc"][2] // CONV_CB + col0 // CONV_CB
    wb = col0 // CONV_CB
    r8 = tt // 8
    has_skip = skip is not None

    def body(*refs):
        if has_skip:
            u_ref, up_ref, d_ref, w_ref, b_ref, sk_ref, skw_ref, du_ref, dw_ref, db_ref, nx_ref = refs
        else:
            u_ref, up_ref, d_ref, w_ref, b_ref, du_ref, dw_ref, db_ref, nx_ref = refs
        i = pl.program_id(1)
        cur = u_ref[...]
        prev8 = jnp.where(i < nt - 1, up_ref[...], 0.0)
        sh = [_shift_down(cur, prev8, 3 - k) for k in range(3)] + [cur]
        pre = b_ref[...]
        for k in range(4):
            pre = pre + w_ref[k:k + 1, :] * sh[k]
        sg = _sigmoid(pre)
        dout_v = d_ref[...]
        if has_skip:
            dout_v = dout_v + sk_ref[...] * skw_ref[...]
        dpre = dout_v * (sg * (1.0 + pre * (1.0 - sg)))

        @pl.when(i == 0)
        def _():
            nx_ref[...] = jnp.zeros_like(nx_ref)
            dw_ref[...] = jnp.zeros_like(dw_ref)
            db_ref[...] = jnp.zeros_like(db_ref)

        next8 = nx_ref[...]
        du = w_ref[3:4, :] * dpre
        for s in range(1, 4):
            du = du + w_ref[3 - s:4 - s, :] * _shift_up(dpre, next8, s)
        du_ref[...] = du.astype(BF16)
        nx_ref[...] = dpre[:8]
        rows = [jnp.sum(dpre * sh[k], axis=0, keepdims=True) for k in range(4)]
        dw_ref[...] += jnp.concatenate(rows + [jnp.zeros((4, CONV_CB), F32)], axis=0)
        db_ref[...] += jnp.sum(dpre, axis=0, keepdims=True)

    rev = lambda c, i: (nt - 1 - i, c)
    in_specs = [pl.BlockSpec((tt, CONV_CB), lambda c, i: (nt - 1 - i, base + c)),
                pl.BlockSpec((8, CONV_CB), lambda c, i: (jnp.maximum((nt - 1 - i) * r8 - 1, 0), base + c)),
                pl.BlockSpec((tt, CONV_CB), rev),
                pl.BlockSpec((4, CONV_CB), lambda c, i: (0, wb + c)),
                pl.BlockSpec((1, CONV_CB), lambda c, i: (0, wb + c))]
    args = [proj, proj, dout, conv_w, conv_b]
    if has_skip:
        in_specs += [pl.BlockSpec((tt, CONV_CB), rev), pl.BlockSpec((1, CONV_CB), lambda c, i: (0, c))]
        args += [skip[0], skip[1]]
    return pl.pallas_call(
        body, name=name, grid=(width // CONV_CB, nt),
        in_specs=in_specs,
        out_specs=[pl.BlockSpec((tt, CONV_CB), rev), pl.BlockSpec((8, CONV_CB), lambda c, i: (0, c)),
                   pl.BlockSpec((1, CONV_CB), lambda c, i: (0, c))],
        out_shape=[jax.ShapeDtypeStruct((t, width), BF16), jax.ShapeDtypeStruct((8, width), F32),
                   jax.ShapeDtypeStruct((1, width), F32)],
        scratch_shapes=[pltpu.VMEM((8, CONV_CB), F32)],
        compiler_params=_params(("parallel", "arbitrary"), 32 << 20),
    )(*args)


GW = SSD_R * SSD_P


def _expand8(v, passes=2):
    r = v.shape[0]
    if r < 8:
        v = jnp.broadcast_to(v, (8, SSD_R))
    ri = lax.broadcasted_iota(jnp.int32, (SSD_R, GW), 0)
    ci = lax.broadcasted_iota(jnp.int32, (SSD_R, GW), 1)
    spread = jnp.where((ci >= ri * SSD_P) & (ci < (ri + 1) * SSD_P), 1.0, 0.0)
    return _dot01(v, spread, passes)[:r]


def _head_pair_split(tile):
    first = lax.broadcasted_iota(jnp.int32, (1, 2 * SSD_P), 1) < SSD_P
    return jnp.where(first, tile, 0.0), jnp.where(first, 0.0, tile)


def _sel(rows, group):
    ri = lax.broadcasted_iota(jnp.int32, (rows, rows // group), 0)
    ci = lax.broadcasted_iota(jnp.int32, (rows, rows // group), 1)
    lo = ci * group
    return jnp.where((ri >= lo) & (ri < lo + group), 1.0, 0.0).astype(F32)


def _dot01(lhs, rhs, passes, split_lhs=True, dims=((1,), (0,))):
    val, m01 = (lhs, rhs) if split_lhs else (rhs, lhs)
    m01 = m01.astype(BF16)
    out = None
    for p in range(passes):
        piece = val.astype(BF16)
        ops = (piece, m01) if split_lhs else (m01, piece)
        d = lax.dot_general(ops[0], ops[1], (dims, ((), ())), preferred_element_type=F32)
        out = d if out is None else out + d
        if p + 1 < passes:
            val = val - piece.astype(F32)
    return out


def _ssd_chunk_terms(adt):
    li = lax.broadcasted_iota(jnp.int32, (CHUNK, CHUNK), 0)
    si = lax.broadcasted_iota(jnp.int32, (CHUNK, CHUNK), 1)
    causal = li >= si
    a_cs = _dot01(jnp.where(causal, 1.0, 0.0), adt, 3, split_lhs=False)
    a_cs_t = _dot01(adt, jnp.where(li <= si, 1.0, 0.0), 3, dims=((0,), (0,)))
    return a_cs, a_cs_t, causal


def _ssd_fwd(xc, dt3, adt3):
    t = xc.shape[0]
    nc = t // CHUNK

    gs = 2

    def body(xs_ref, b_ref, c_ref, dt_ref, adt_ref, y_ref, hp_ref, h_ref):
        @pl.when(pl.program_id(1) == 0)
        def _():
            h_ref[...] = jnp.zeros_like(h_ref)

        for gg in range(gs):
            a_cs, a_cs_t, causal = _ssd_chunk_terms(adt_ref[gg])
            a_last = a_cs[CHUNK - 1:CHUNK, :]
            h = h_ref[gg]
            hp_ref[gg, 0] = h
            xd = xs_ref[:, GW * gg:GW * (gg + 1)] * _expand8(dt_ref[gg])
            bb = b_ref[:, SSD_N * gg:SSD_N * (gg + 1)].astype(BF16)
            cbf = c_ref[:, SSD_N * gg:SSD_N * (gg + 1)].astype(BF16)
            cb = lax.dot_general(cbf, bb, (((1,), (1,)), ((), ())), preferred_element_type=F32)
            yoff = jnp.dot(cbf, h.astype(BF16), preferred_element_type=F32) * _expand8(jnp.exp(a_cs))
            for q in range(SSD_R // 2):
                lmats = []
                for r in (2 * q, 2 * q + 1):
                    seg = jnp.exp(jnp.where(causal, a_cs[:, r:r + 1] - a_cs_t[r:r + 1, :], -jnp.inf))
                    lmats.append((cb * seg).astype(BF16))
                tile = slice(2 * SSD_P * q, 2 * SSD_P * (q + 1))
                xa, xb = _head_pair_split(xd[:, tile])
                y_ref[:, GW * gg + 2 * SSD_P * q:GW * gg + 2 * SSD_P * (q + 1)] = (
                    jnp.dot(jnp.concatenate(lmats, axis=1), jnp.concatenate([xa, xb], axis=0).astype(BF16),
                            preferred_element_type=F32) + yoff[:, tile])
            xdd = (xd * _expand8(jnp.exp(a_last - a_cs))).astype(BF16)
            h_ref[gg] = _expand8(jnp.exp(a_last), 3) * h + lax.dot_general(
                bb, xdd, (((0,), (0,)), ((), ())), preferred_element_type=F32)

    nb = SSD_INNER // (gs * SSD_N)
    return pl.pallas_call(
        body, name="ssd_fwd", grid=(SSD_G // gs, nc),
        in_specs=[pl.BlockSpec((CHUNK, gs * GW), lambda g, c: (c, g)),
                  pl.BlockSpec((CHUNK, gs * SSD_N), lambda g, c: (c, nb + g)),
                  pl.BlockSpec((CHUNK, gs * SSD_N), lambda g, c: (c, nb + SSD_G // gs + g)),
                  pl.BlockSpec((gs, CHUNK, SSD_R), lambda g, c: (g, c, 0)),
                  pl.BlockSpec((gs, CHUNK, SSD_R), lambda g, c: (g, c, 0))],
        out_specs=[pl.BlockSpec((CHUNK, gs * GW), lambda g, c: (c, g)),
                   pl.BlockSpec((gs, 1, SSD_N, GW), lambda g, c: (g, c, 0, 0))],
        out_shape=[jax.ShapeDtypeStruct((t, SSD_INNER), F32), jax.ShapeDtypeStruct((SSD_G, nc, SSD_N, GW), F32)],
        scratch_shapes=[pltpu.VMEM((gs, SSD_N, GW), F32)],
        compiler_params=_params(("parallel", "arbitrary"), 32 << 20),
    )(xc, xc, xc, dt3, adt3)


def _ssd_bwd(xc, dt3, adt3, hprev, dy):
    t = xc.shape[0]
    nc = t // CHUNK

    def body(xs_ref, b_ref, c_ref, dt_ref, adt_ref, hp_ref, dy_ref,
             dx_ref, db_ref, dc_ref, dadt_ref, dxdx_ref, dh_ref):
        @pl.when(pl.program_id(1) == 0)
        def _():
            dh_ref[...] = jnp.zeros_like(dh_ref)

        a_cs, a_cs_t, causal = _ssd_chunk_terms(adt_ref[0])
        a_last = a_cs[CHUNK - 1:CHUNK, :]
        e_last = jnp.exp(a_last)
        ex = _expand8(jnp.exp(a_cs))
        dtex = _expand8(jnp.exp(a_last - a_cs))
        dtx = _expand8(dt_ref[0])
        xs = xs_ref[...]
        dyv = dy_ref[...]
        hp = hp_ref[0, 0]
        dh = dh_ref[...]
        sel = _sel(GW, SSD_P)
        seg8 = lambda v: _dot01(v, sel, 2)

        xd = xs * dtx
        xdd = xd * dtex
        bb = b_ref[...].astype(BF16)
        cbf = c_ref[...].astype(BF16)
        hpb = hp.astype(BF16)
        dhb = dh.astype(BF16)
        xdb = xd.astype(BF16)
        cb = lax.dot_general(cbf, bb, (((1,), (1,)), ((), ())), preferred_element_type=F32)
        dye = (dyv * ex).astype(BF16)
        yoff = jnp.dot(cbf, hpb, preferred_element_type=F32) * ex
        dc = lax.dot_general(dye, hpb, (((1,), (1,)), ((), ())), preferred_element_type=F32)
        d_a = seg8(dyv * yoff)
        bdh = jnp.dot(bb, dhb, preferred_element_type=F32)
        db = lax.dot_general(xdd.astype(BF16), dhb, (((1,), (1,)), ((), ())), preferred_element_type=F32)
        dxd_state = bdh * dtex
        q = seg8(xdd * bdh)
        d_a = d_a - q
        d_a_last = (jnp.sum(q, axis=0, keepdims=True)
                    + e_last * seg8(jnp.sum(hp * dh, axis=0, keepdims=True)))
        dh_ref[...] = (lax.dot_general(cbf, dye, (((0,), (0,)), ((), ())), preferred_element_type=F32)
                       + _expand8(e_last, 3) * dh)
        dcb = jnp.zeros((CHUNK, CHUNK), F32)
        w_all = []
        dxd_parts = []
        for q2 in range(SSD_R // 2):
            tile = slice(2 * SSD_P * q2, 2 * SSD_P * (q2 + 1))
            dy_pair = [part.astype(BF16) for part in _head_pair_split(dyv[:, tile])]
            lmats = []
            for k, r in enumerate((2 * q2, 2 * q2 + 1)):
                seg = jnp.exp(jnp.where(causal, a_cs[:, r:r + 1] - a_cs_t[r:r + 1, :], -jnp.inf))
                lmat = cb * seg
                dm = lax.dot_general(dy_pair[k], xdb[:, tile], (((1,), (1,)), ((), ())), preferred_element_type=F32)
                dcb = dcb + dm * seg
                w_all.append(dm * lmat)
                lmats.append(lmat.astype(BF16))
            dxd_parts.append(lax.dot_general(jnp.concatenate(lmats, axis=0), jnp.concatenate(dy_pair, axis=0),
                                             (((0,), (0,)), ((), ())), preferred_element_type=F32))
        row_sums = _dot01(jnp.concatenate(w_all, axis=1), _sel(SSD_R * CHUNK, CHUNK), 2)
        cs_rows = jnp.concatenate([jnp.sum(wr, axis=0, keepdims=True) for wr in w_all], axis=0)
        col_sums = _dot01(cs_rows, _sel(SSD_R, 1), 3, dims=((0,), (0,)))
        d_a = d_a + row_sums - col_sums
        li = lax.broadcasted_iota(jnp.int32, (CHUNK, SSD_R), 0)
        d_a = d_a + jnp.where(li == CHUNK - 1, d_a_last, 0.0)
        l2 = lax.broadcasted_iota(jnp.int32, (CHUNK, CHUNK), 0)
        s2 = lax.broadcasted_iota(jnp.int32, (CHUNK, CHUNK), 1)
        dadt_ref[0] = _dot01(jnp.where(s2 >= l2, 1.0, 0.0), d_a, 3, split_lhs=False)
        dxd = dxd_state + jnp.concatenate(dxd_parts, axis=1)
        dxdx_ref[0] = seg8(dxd * xs)
        dx_ref[...] = dxd * dtx
        dcbb = dcb.astype(BF16)
        db_ref[...] = db + lax.dot_general(dcbb, cbf, (((0,), (0,)), ((), ())), preferred_element_type=F32)
        dc_ref[...] = dc + jnp.dot(dcbb, bb, preferred_element_type=F32)

    nb = SSD_INNER // SSD_N
    rc = lambda g, c: (nc - 1 - c, g)
    r3 = lambda g, c: (g, nc - 1 - c, 0)
    return pl.pallas_call(
        body, name="ssd_bwd", grid=(SSD_G, nc),
        in_specs=[pl.BlockSpec((CHUNK, GW), rc),
                  pl.BlockSpec((CHUNK, SSD_N), lambda g, c: (nc - 1 - c, nb + g)),
                  pl.BlockSpec((CHUNK, SSD_N), lambda g, c: (nc - 1 - c, nb + SSD_G + g)),
                  pl.BlockSpec((1, CHUNK, SSD_R), r3),
                  pl.BlockSpec((1, CHUNK, SSD_R), r3),
                  pl.BlockSpec((1, 1, SSD_N, GW), lambda g, c: (g, nc - 1 - c, 0, 0)),
                  pl.BlockSpec((CHUNK, GW), rc)],
        out_specs=[pl.BlockSpec((CHUNK, GW), rc),
                   pl.BlockSpec((CHUNK, SSD_N), rc),
                   pl.BlockSpec((CHUNK, SSD_N), rc),
                   pl.BlockSpec((1, CHUNK, SSD_R), r3),
                   pl.BlockSpec((1, CHUNK, SSD_R), r3)],
        out_shape=[jax.ShapeDtypeStruct((t, SSD_INNER), F32),
                   jax.ShapeDtypeStruct((t, SSD_G * SSD_N), F32),
                   jax.ShapeDtypeStruct((t, SSD_G * SSD_N), F32),
                   jax.ShapeDtypeStruct((SSD_G, t, SSD_R), F32),
                   jax.ShapeDtypeStruct((SSD_G, t, SSD_R), F32)],
        scratch_shapes=[pltpu.VMEM((SSD_N, GW), F32)],
        compiler_params=_params(("parallel", "arbitrary"), 40 << 20),
    )(xc, xc, xc, dt3, adt3, hprev, dy)


def _gated_norm_fwd(y, xc, proj, dexp, ng):
    t = y.shape[0]
    tt = _divtile(t, 256)

    def body(y_ref, x_ref, z_ref, d_ref, g_ref, o_ref):
        z = z_ref[...]
        y2 = (y_ref[...] + d_ref[...] * x_ref[...]) * (z * _sigmoid(z))
        for gi in range(SSD_G):
            sl = slice(GW * gi, GW * (gi + 1))
            seg = y2[:, sl]
            rinv = lax.rsqrt(jnp.mean(seg * seg, axis=-1, keepdims=True) + RMS_EPS)
            o_ref[:, sl] = (seg * rinv * g_ref[:, sl]).astype(BF16)

    row = pl.BlockSpec((tt, SSD_INNER), lambda i: (i, 0))
    vec = pl.BlockSpec((1, SSD_INNER), lambda i: (0, 0))
    return pl.pallas_call(
        body, name="gated_norm_fwd", grid=(t // tt,), in_specs=[row, row, row, vec, vec], out_specs=row,
        out_shape=jax.ShapeDtypeStruct((t, SSD_INNER), BF16),
        compiler_params=_params(("parallel",), 48 << 20),
    )(y, xc, proj, dexp, ng)


def _gated_norm_bwd(dout, y, xc, proj, dexp, ng):
    t = y.shape[0]
    tt = _divtile(t, 128)

    def body(do_ref, y_ref, x_ref, z_ref, d_ref, g_ref, dz_ref, dy_ref, dg_ref, dd_ref):
        z = z_ref[...]
        sg = _sigmoid(z)
        sz = z * sg
        xs = x_ref[...]
        y1 = y_ref[...] + d_ref[...] * xs
        y2 = y1 * sz
        dov = do_ref[...]

        @pl.when(pl.program_id(0) == 0)
        def _():
            dg_ref[...] = jnp.zeros_like(dg_ref)
            dd_ref[...] = jnp.zeros_like(dd_ref)

        for gi in range(SSD_G):
            sl = slice(GW * gi, GW * (gi + 1))
            seg = y2[:, sl]
            rinv = lax.rsqrt(jnp.mean(seg * seg, axis=-1, keepdims=True) + RMS_EPS)
            yn = seg * rinv
            dsl = dov[:, sl]
            dg_ref[:, sl] += jnp.sum(dsl * yn, axis=0, keepdims=True)
            dyn = dsl * g_ref[:, sl]
            dy2 = rinv * (dyn - yn * jnp.mean(dyn * yn, axis=-1, keepdims=True))
            dz_ref[:, sl] = (dy2 * y1[:, sl] * (sg[:, sl] * (1.0 + z[:, sl] * (1.0 - sg[:, sl])))).astype(BF16)
            dy1 = dy2 * sz[:, sl]
            dy_ref[:, sl] = dy1
            dd_ref[:, sl] += jnp.sum(dy1 * xs[:, sl], axis=0, keepdims=True)

    row = pl.BlockSpec((tt, SSD_INNER), lambda i: (i, 0))
    vec = pl.BlockSpec((1, SSD_INNER), lambda i: (0, 0))
    return pl.pallas_call(
        body, name="gated_norm_bwd", grid=(t // tt,), in_specs=[row, row, row, row, vec, vec],
        out_specs=[row, row, vec, vec],
        out_shape=[jax.ShapeDtypeStruct((t, SSD_INNER), BF16), jax.ShapeDtypeStruct((t, SSD_INNER), F32),
                   jax.ShapeDtypeStruct((1, SSD_INNER), F32), jax.ShapeDtypeStruct((1, SSD_INNER), F32)],
        compiler_params=_params(("arbitrary",), 48 << 20),
    )(dout, y, xc, proj, dexp, ng)


def _fold_heads(v, name):
    def body(v_ref, o_ref):
        ri = lax.broadcasted_iota(jnp.int32, (SSD_INNER, 128), 0)
        ci = lax.broadcasted_iota(jnp.int32, (SSD_INNER, 128), 1)
        fold = jnp.where((ri >= ci * SSD_P) & (ri < (ci + 1) * SSD_P), 1.0, 0.0).astype(F32)
        o_ref[...] = jnp.dot(v_ref[...], fold, preferred_element_type=F32, precision=HI)

    return pl.pallas_call(body, name=name, out_shape=jax.ShapeDtypeStruct((1, 128), F32))(v)


Q_BLK = SEGS["q"][2] // QW
K_BLK = SEGS["k"][2] // KVW
V_BLK = SEGS["v"][2] // KVW


def _rope_tables(pos_ref, invf_ref, width):
    ang = pos_ref[...] * invf_ref[...]
    lane = lax.broadcasted_iota(jnp.int32, (1, 128), 1)
    sign = jnp.where((lane % HD) < (HD // 2), -1.0, 1.0)
    cos = jnp.tile(jnp.cos(ang), (1, width // 128))
    sin = jnp.tile(sign * jnp.sin(ang), (1, width // 128))
    first = (lax.broadcasted_iota(jnp.int32, (1, width), 1) % HD) < (HD // 2)
    return cos, sin, first


def _rot_half(u, first):
    w = u.shape[1]
    return jnp.where(first, pltpu.roll(u, w - HD // 2, 1), pltpu.roll(u, HD // 2, 1))


def _rope_fwd(proj, pos, invf):
    t = proj.shape[0]
    tt = _divtile(t, 512)

    def body(q_ref, k_ref, pos_ref, invf_ref, qo_ref, ko_ref):
        cos, sin, first = _rope_tables(pos_ref, invf_ref, QW)
        q = q_ref[...]
        qr = q * cos + _rot_half(q, first) * sin
        for p in range(QW // 128):
            qo_ref[128 * p:128 * (p + 1), :] = qr[:, 128 * p:128 * (p + 1)].T.astype(BF16)
        k = k_ref[...]
        ko_ref[...] = (k * cos[:, :KVW] + _rot_half(k, first[:, :KVW]) * sin[:, :KVW]).astype(BF16)

    return pl.pallas_call(
        body, name="rope_fwd", grid=(t // tt,),
        in_specs=[pl.BlockSpec((tt, QW), lambda i: (i, Q_BLK)), pl.BlockSpec((tt, KVW), lambda i: (i, K_BLK)),
                  pl.BlockSpec((tt, 1), lambda i: (i, 0)), pl.BlockSpec((1, 128), lambda i: (0, 0))],
        out_specs=[pl.BlockSpec((QW, tt), lambda i: (0, i)), pl.BlockSpec((tt, KVW), lambda i: (i, 0))],
        out_shape=[jax.ShapeDtypeStruct((QW, t), BF16), jax.ShapeDtypeStruct((t, KVW), BF16)],
        compiler_params=_params(("parallel",), 40 << 20),
    )(proj, proj, pos, invf)


def _rope_bwd(dqt, dk, pos, invf):
    t = dk.shape[0]
    tt = _divtile(t, 512)

    def body(dq_ref, dk_ref, pos_ref, invf_ref, qo_ref, ko_ref):
        cos, sin, first = _rope_tables(pos_ref, invf_ref, QW)
        q = jnp.concatenate([dq_ref[128 * p:128 * (p + 1), :].T for p in range(QW // 128)], axis=1)
        qo_ref[...] = (q * cos + _rot_half(q * sin, first)).astype(BF16)
        k = dk_ref[...]
        ko_ref[...] = (k * cos[:, :KVW] + _rot_half(k * sin[:, :KVW], first[:, :KVW])).astype(BF16)

    return pl.pallas_call(
        body, name="rope_bwd", grid=(t // tt,),
        in_specs=[pl.BlockSpec((QW, tt), lambda i: (0, i)), pl.BlockSpec((tt, KVW), lambda i: (i, 0)),
                  pl.BlockSpec((tt, 1), lambda i: (i, 0)), pl.BlockSpec((1, 128), lambda i: (0, 0))],
        out_specs=[pl.BlockSpec((tt, QW), lambda i: (i, 0)), pl.BlockSpec((tt, KVW), lambda i: (i, 0))],
        out_shape=[jax.ShapeDtypeStruct((t, QW), BF16), jax.ShapeDtypeStruct((t, KVW), BF16)],
        compiler_params=_params(("parallel",), 40 << 20),
    )(dqt, dk, pos, invf)


GQ = NQ // NKV
NT_DIMS = (((1,), (1,)), ((), ()))
TN_DIMS = (((0,), (0,)), ((), ()))


def _attn_heads(ref, j, dtype=None):
    out = jnp.concatenate([ref[HD * h:HD * (h + 1), :] for h in range(j * GQ, (j + 1) * GQ)], axis=1)
    return out if dtype is None else out.astype(dtype)


def _attn_sink_row(s_ref, j):
    return jnp.concatenate([jnp.broadcast_to(s_ref[:, h:h + 1], (1, WINDOW)) for h in range(j * GQ, (j + 1) * GQ)],
                           axis=1)


def _attn_mask(n):
    kr = lax.broadcasted_iota(jnp.int32, (2 * WINDOW, GQ * WINDOW), 0)
    qi = lax.broadcasted_iota(jnp.int32, (2 * WINDOW, GQ * WINDOW), 1) % WINDOW
    return (kr > qi) & (kr <= qi + WINDOW) & ((n > 0) | (kr >= WINDOW))


def _attn_probs(qgt, kk, sink, mask):
    s = jnp.where(mask, jnp.dot(kk, qgt, preferred_element_type=F32) * (HD ** -0.5), -jnp.inf)
    m = jnp.maximum(jnp.max(s, axis=0, keepdims=True), sink)
    p = jnp.exp(s - m)
    ps = jnp.exp(sink - m)
    inv = 1.0 / (jnp.sum(p, axis=0, keepdims=True) + ps)
    return p * inv, ps * inv


def _attn_fwd(qt, kr, proj, sinks):
    t = kr.shape[0]
    nb = t // WINDOW

    def body(q_ref, kc_ref, kp_ref, vc_ref, vp_ref, s_ref, o_ref):
        mask = _attn_mask(pl.program_id(0))
        for j in range(NKV):
            ks = slice(HD * j, HD * (j + 1))
            kk = jnp.concatenate([kp_ref[:, ks], kc_ref[:, ks]], axis=0)
            vv = jnp.concatenate([vp_ref[:, ks], vc_ref[:, ks]], axis=0).astype(BF16)
            pn, _ = _attn_probs(_attn_heads(q_ref, j), kk, _attn_sink_row(s_ref, j), mask)
            ot = lax.dot_general(vv, pn.astype(BF16), TN_DIMS, preferred_element_type=F32).astype(BF16)
            for g in range(GQ):
                h = j * GQ + g
                o_ref[HD * h:HD * (h + 1), :] = ot[:, WINDOW * g:WINDOW * (g + 1)]

    prev = lambda n: (jnp.maximum(n - 1, 0), 0)
    return pl.pallas_call(
        body, name="attn_fwd", grid=(nb,),
        in_specs=[pl.BlockSpec((QW, WINDOW), lambda n: (0, n)),
                  pl.BlockSpec((WINDOW, KVW), lambda n: (n, 0)), pl.BlockSpec((WINDOW, KVW), prev),
                  pl.BlockSpec((WINDOW, KVW), lambda n: (n, V_BLK)),
                  pl.BlockSpec((WINDOW, KVW), lambda n: (jnp.maximum(n - 1, 0), V_BLK)),
                  pl.BlockSpec((1, 128), lambda n: (0, 0))],
        out_specs=pl.BlockSpec((QW, WINDOW), lambda n: (0, n)),
        out_shape=jax.ShapeDtypeStruct((QW, t), BF16),
        compiler_params=_params(("parallel",), 24 << 20),
    )(qt, kr, kr, proj, proj, sinks)


def _attn_bwd(qt, kr, proj, sinks, dot_):
    t = kr.shape[0]
    nb = t // WINDOW

    def body(q_ref, kc_ref, kp_ref, vc_ref, vp_ref, s_ref, do_ref,
             dq_ref, dk_ref, dv_ref, ds_ref, dkc_ref, dvc_ref):
        i = pl.program_id(0)
        mask = _attn_mask(nb - 1 - i)

        @pl.when(i == 0)
        def _():
            dkc_ref[...] = jnp.zeros_like(dkc_ref)
            dvc_ref[...] = jnp.zeros_like(dvc_ref)
            ds_ref[...] = jnp.zeros_like(ds_ref)

        lane = lax.broadcasted_iota(jnp.int32, (1, 128), 1)
        ds_acc = jnp.zeros((1, 128), F32)
        for j in range(NKV):
            ks = slice(HD * j, HD * (j + 1))
            kk = jnp.concatenate([kp_ref[:, ks], kc_ref[:, ks]], axis=0)
            vv = jnp.concatenate([vp_ref[:, ks], vc_ref[:, ks]], axis=0).astype(BF16)
            qgt = _attn_heads(q_ref, j)
            pn, psn = _attn_probs(qgt, kk, _attn_sink_row(s_ref, j), mask)
            dogt = _attn_heads(do_ref, j)
            dp = jnp.dot(vv, dogt, preferred_element_type=F32)
            delta = jnp.sum(dp * pn, axis=0, keepdims=True)
            dsb = (pn * (dp - delta) * (HD ** -0.5)).astype(BF16)
            dsink = -psn * delta
            dqt = lax.dot_general(kk, dsb, TN_DIMS, preferred_element_type=F32)
            for g in range(GQ):
                h = j * GQ + g
                cols = slice(WINDOW * g, WINDOW * (g + 1))
                dq_ref[HD * h:HD * (h + 1), :] = dqt[:, cols]
                ds_acc = ds_acc + jnp.where(lane == h, jnp.sum(dsink[:, cols], axis=1, keepdims=True), 0.0)
            dkk = lax.dot_general(dsb, qgt, NT_DIMS, preferred_element_type=F32)
            dvv = lax.dot_general(pn.astype(BF16), dogt, NT_DIMS, preferred_element_type=F32)
            dk_ref[:, ks] = dkk[WINDOW:] + dkc_ref[:, ks]
            dv_ref[:, ks] = (dvv[WINDOW:] + dvc_ref[:, ks]).astype(BF16)
            dkc_ref[:, ks] = dkk[:WINDOW]
            dvc_ref[:, ks] = dvv[:WINDOW]
        ds_ref[...] += ds_acc

    cur = lambda i: (nb - 1 - i, 0)
    cur_t = lambda i: (0, nb - 1 - i)
    prev = lambda i: (jnp.maximum(nb - 2 - i, 0), 0)
    return pl.pallas_call(
        body, name="attn_bwd", grid=(nb,),
        in_specs=[pl.BlockSpec((QW, WINDOW), cur_t),
                  pl.BlockSpec((WINDOW, KVW), cur), pl.BlockSpec((WINDOW, KVW), prev),
                  pl.BlockSpec((WINDOW, KVW), lambda i: (nb - 1 - i, V_BLK)),
                  pl.BlockSpec((WINDOW, KVW), lambda i: (jnp.maximum(nb - 2 - i, 0), V_BLK)),
                  pl.BlockSpec((1, 128), lambda i: (0, 0)),
                  pl.BlockSpec((QW, WINDOW), cur_t)],
        out_specs=[pl.BlockSpec((QW, WINDOW), cur_t), pl.BlockSpec((WINDOW, KVW), cur),
                   pl.BlockSpec((WINDOW, KVW), cur), pl.BlockSpec((1, 128), lambda i: (0, 0))],
        out_shape=[jax.ShapeDtypeStruct((QW, t), F32), jax.ShapeDtypeStruct((t, KVW), F32),
                   jax.ShapeDtypeStruct((t, KVW), BF16), jax.ShapeDtypeStruct((1, 128), F32)],
        scratch_shapes=[pltpu.VMEM((WINDOW, KVW), F32), pltpu.VMEM((WINDOW, KVW), F32)],
        compiler_params=_params(("arbitrary",), 32 << 20),
    )(qt, kr, kr, proj, proj, sinks, dot_)


GS_BLK = SEGS["gs"][2] // D
GA_BLK = SEGS["ga"][2] // D


def _merge_fwd(ys, ya, proj):
    t = ys.shape[0]
    tt = _divtile(t, 256)

    def body(ys_ref, ya_ref, gs_ref, ga_ref, o_ref):
        o_ref[...] = (_sigmoid(gs_ref[...]) * ys_ref[...] + _sigmoid(ga_ref[...]) * ya_ref[...]).astype(BF16)

    row = pl.BlockSpec((tt, D), lambda i: (i, 0))
    return pl.pallas_call(
        body, name="merge_fwd", grid=(t // tt,),
        in_specs=[row, row, pl.BlockSpec((tt, D), lambda i: (i, GS_BLK)), pl.BlockSpec((tt, D), lambda i: (i, GA_BLK))],
        out_specs=row, out_shape=jax.ShapeDtypeStruct((t, D), BF16),
        compiler_params=_params(("parallel",), 32 << 20),
    )(ys, ya, proj, proj)


def _merge_bwd(dm, ys, ya, proj):
    t = ys.shape[0]
    tt = _divtile(t, 256)

    def body(dm_ref, ys_ref, ya_ref, gs_ref, ga_ref, dys_ref, dya_ref, dgs_ref, dga_ref):
        d = dm_ref[...]
        s = _sigmoid(gs_ref[...])
        a = _sigmoid(ga_ref[...])
        dys_ref[...] = (d * s).astype(BF16)
        dya_ref[...] = (d * a).astype(BF16)
        dgs_ref[...] = (d * ys_ref[...] * (s * (1.0 - s))).astype(BF16)
        dga_ref[...] = (d * ya_ref[...] * (a * (1.0 - a))).astype(BF16)

    row = pl.BlockSpec((tt, D), lambda i: (i, 0))
    return pl.pallas_call(
        body, name="merge_bwd", grid=(t // tt,),
        in_specs=[row, row, row, pl.BlockSpec((tt, D), lambda i: (i, GS_BLK)),
                  pl.BlockSpec((tt, D), lambda i: (i, GA_BLK))],
        out_specs=[row, row, row, row], out_shape=[jax.ShapeDtypeStruct((t, D), BF16)] * 4,
        compiler_params=_params(("parallel",), 40 << 20),
    )(dm, ys, ya, proj, proj)


def _pad128(v):
    return jnp.pad(v, ((0, 0), (0, 128 - v.shape[1])))


def _group_major(v):
    t = v.shape[0]
    return jnp.transpose(v[:, :SSD_HEADS].reshape(t, SSD_G, SSD_R), (1, 0, 2))


def _token_major(v3):
    t = v3.shape[1]
    return _pad128(jnp.transpose(v3, (1, 0, 2)).reshape(t, SSD_HEADS))


def _local_step(x, pos, target, w, small, fetch=None, early_grads=None):
    w = dict(w)
    xb = x.astype(BF16)
    gu1, a1 = _mm_swiglu(xb, w["gu1"], "ffn1_gu")
    if fetch is not None:
        w.update(fetch(1, a1))
    f1 = _mm(a1, w["d1"], "nn", F32, "ffn1_down", caps=(512, 1024, FFN_H))
    h1, h1b, xh1, rs1 = _ln_fwd(x, f1, small["ln1_g"], small["ln1_b"], 0.5, "ln1_fwd")
    if fetch is not None:
        w.update(fetch(2, h1b))
    proj = _mm(h1b, w["win"], "nn", F32, "proj", caps=(1024, 896, 2048))
    if fetch is not None:
        w.update(fetch(3, proj))
    bias128 = _pad128(small["dt_bias"])
    alog128 = _pad128(small["a_log"])
    dt, adt = _dt_prep(proj, bias128, alog128)
    dt3, adt3 = _group_major(dt), _group_major(adt)
    xc = _conv_fwd(proj, small["conv_w"], small["conv_b"])
    y_ssd, hprev = _ssd_fwd(xc, dt3, adt3)
    dexp = jnp.repeat(small["d_skip"], SSD_P, axis=1)
    ysn = _gated_norm_fwd(y_ssd, xc, proj, dexp, small["ssd_norm_g"])
    ys = _mm(ysn, w["so"], "nn", F32, "ssd_out")
    invf = jnp.tile(ROPE_THETA ** (-jnp.arange(HD // 2, dtype=F32) * 2.0 / HD), 4)[None, :]
    qt, kr = _rope_fwd(proj, pos, invf)
    sinks128 = _pad128(small["attn_sinks"])
    ot = _attn_fwd(qt, kr, proj, sinks128)
    ya = _mm(ot, w["ao"], "tn", F32, "attn_out")
    mg = _merge_fwd(ys, ya, proj)
    mix = _mm(mg, w["out"], "nn", F32, "mix_out")
    h2, h2b, xh2, rs2 = _ln_fwd(h1, mix, small["ln2_g"], small["ln2_b"], 1.0, "ln2_fwd")
    gu2, a2 = _mm_swiglu(h2b, w["gu2"], "ffn2_gu")
    f2 = _mm(a2, w["d2"], "nn", F32, "ffn2_down", caps=(512, 1024, FFN_H))
    _, _, xh3, rs3, dh3, loss = _ln_fwd(h2, f2, small["ln3_g"], small["ln3_b"], 0.5, "ln3_fwd", target=target)

    gw, gs = {}, {}
    dr3, dr3h, gs["ln3_g"], gs["ln3_b"] = _ln_bwd(dh3, xh3, rs3, small["ln3_g"], 0.5, "ln3_bwd")
    gw["d2"] = _mm(a2, dr3h, "tn", F32, "ffn2_down_dw")
    da2 = _mm(dr3h, w["d2"], "nt", BF16, "ffn2_down_dx", caps=(1024, 1408, 2048))
    dgu2 = _swiglu_bwd(gu2, da2, "ffn2_act_bwd")
    gw["gu2"] = _mm(h2b, dgu2, "tn", F32, "ffn2_gu_dw", caps=(1024, 1408, 2048), n_slabs=N_CHIPS)
    dh2 = _mm(dgu2, w["gu2"], "nt", F32, "ffn2_gu_dx", add=dr3, add_scale=ALPHA, caps=(1024, 1024, 2816))
    dr2, dr2b, gs["ln2_g"], gs["ln2_b"] = _ln_bwd(dh2, xh2, rs2, small["ln2_g"], 1.0, "ln2_bwd")
    gw["out"] = _mm(mg, dr2b, "tn", F32, "mix_out_dw")
    dmg = _mm(dr2b, w["out"], "nt", F32, "mix_out_dx")
    dys, dya, dgs, dga = _merge_bwd(dmg, ys, ya, proj)
    gw["ao"] = _mm(ot, dya, "nn", F32, "attn_out_dw")
    dot_ = _mm(w["ao"], dya, "nt", BF16, "attn_out_dx")
    dqt, dkr, dv, gs["attn_sinks"] = _attn_bwd(qt, kr, proj, sinks128, dot_)
    dq, dk = _rope_bwd(dqt, dkr, pos, invf)
    gw["so"] = _mm(ysn, dys, "tn", F32, "ssd_out_dw")
    dysn = _mm(dys, w["so"], "nt", F32, "ssd_out_dx")
    dz, dy1, gs["ssd_norm_g"], dd_ch = _gated_norm_bwd(dysn, y_ssd, xc, proj, dexp, small["ssd_norm_g"])
    gs["d_skip"] = _fold_heads(dd_ch, "d_skip_fold")
    dxs, db, dc, dadt3, dxdx3 = _ssd_bwd(xc, dt3, adt3, hprev, dy1)
    ddt, gs["dt_bias"], gs["a_log"] = _dt_bwd(_token_major(dadt3), _token_major(dxdx3), proj, bias128, alog128)
    cw, cbias = small["conv_w"], small["conv_b"]
    dux, dwx, dbx = _conv_bwd(proj, dxs, cw, cbias, 0, SSD_INNER, "conv_bwd_x", skip=(dy1, dexp))
    dub, dwb, dbb = _conv_bwd(proj, db, cw, cbias, SSD_INNER, SSD_G * SSD_N, "conv_bwd_b")
    duc, dwc, dbc = _conv_bwd(proj, dc, cw, cbias, SSD_INNER + SSD_G * SSD_N, SSD_G * SSD_N, "conv_bwd_c")
    gs["conv_w"] = jnp.concatenate([dwx[:4], dwb[:4], dwc[:4]], axis=1)
    gs["conv_b"] = jnp.concatenate([dbx, dbb, dbc], axis=1)
    dproj = jnp.concatenate([dz, dq, dgs, dga, dux, dub, duc, dk, dv, ddt], axis=1)
    gw["win"] = _mm(h1b, dproj, "tn", F32, "proj_dw", caps=(1024, 896, 2048))
    win = w["win"] if early_grads is None else early_grads[0](gw, w["win"])
    dh1 = _mm(dproj, win, "nt", F32, "proj_dx", add=dr2, add_scale=ALPHA, caps=(1024, 1024, 2432))
    ln1_g = small["ln1_g"]
    if early_grads is not None:
        ln1_g = ln1_g + early_grads[1](dh1)[0:1, 0:1]
    dr1, dr1h, gs["ln1_g"], gs["ln1_b"] = _ln_bwd(dh1, xh1, rs1, ln1_g, 0.5, "ln1_bwd")
    gw["d1"] = _mm(a1, dr1h, "tn", F32, "ffn1_down_dw")
    da1 = _mm(dr1h, w["d1"], "nt", BF16, "ffn1_down_dx", caps=(1024, 1408, 2048))
    dgu1 = _swiglu_bwd(gu1, da1, "ffn1_act_bwd")
    gw["gu1"] = _mm(xb, dgu1, "tn", F32, "ffn1_gu_dw", caps=(1024, 1408, 2048), n_slabs=N_CHIPS)
    grad_x = _mm(dgu1, w["gu1"], "nt", F32, "ffn1_gu_dx", add=dr1, add_scale=ALPHA, caps=(1024, 1024, 2816))
    return loss, grad_x, gw, gs


MESH = pl.DeviceIdType.MESH
ANY = pl.BlockSpec(memory_space=pl.ANY)


def _place():
    x, y, c = lax.axis_index("x"), lax.axis_index("y"), lax.axis_index("c")
    peers = [(1 - x, y), (x, 1 - y), (1 - x, 1 - y)]
    return x, y, c, peers


BIG = [
    ("ffn1_w_gate", D, SHARD_H, "gu1", "col", 0),
    ("ffn1_w_up", D, SHARD_H, "gu1", "col", SHARD_H),
    ("ffn1_w_down", SHARD_H, D, "d1", "row", 0),
    ("w_in", D, SHARD_IN, "win4", "lead", 0),
    ("w_ssd_o", SSD_INNER // N_CHIPS, D, "so", "row", 0),
    ("w_attn_o", D // N_CHIPS, D, "ao", "row", 0),
    ("w_out", D // N_CHIPS, D, "out", "row", 0),
    ("ffn2_w_gate", D, SHARD_H, "gu2", "col", 0),
    ("ffn2_w_up", D, SHARD_H, "gu2", "col", SHARD_H),
    ("ffn2_w_down", SHARD_H, D, "d2", "row", 0),
]
GATHERED = {"gu1": (D, 2 * FFN_H), "d1": (FFN_H, D), "win4": (N_CHIPS, D, SHARD_IN), "so": (SSD_INNER, D),
            "ao": (D, D), "out": (D, D), "gu2": (D, 2 * FFN_H), "d2": (FFN_H, D)}


def _cast_place(srcs, oname, chip_idx):
    rows, cols = srcs[0].shape
    tr = _divtile(rows, 256, 16)
    kind = [b[4] for b in BIG if b[3] == oname][0]

    def body(chip_ref, *refs):
        o_ref = refs[-1]
        for k, s_ref in enumerate(refs[:-1]):
            o_ref[:, k * cols:(k + 1) * cols] = s_ref[...].astype(BF16)

    nt = rows // tr
    if kind == "col":
        o_spec = pl.BlockSpec((tr, len(srcs) * cols), lambda i, chip_ref: (i, chip_ref[0]))
    elif kind == "row":
        o_spec = pl.BlockSpec((tr, cols), lambda i, chip_ref: (chip_ref[0] * nt + i, 0))
    else:
        o_spec = pl.BlockSpec((None, tr, cols), lambda i, chip_ref: (chip_ref[0], i, 0))
    return pl.pallas_call(
        body, name="cast_place_" + oname,
        grid_spec=pltpu.PrefetchScalarGridSpec(
            num_scalar_prefetch=1, grid=(nt,),
            in_specs=[pl.BlockSpec((tr, cols), lambda i, chip_ref: (i, 0))] * len(srcs), out_specs=o_spec),
        out_shape=jax.ShapeDtypeStruct(GATHERED[oname], BF16),
        compiler_params=_params(("parallel",), 32 << 20),
    )(chip_idx, *srcs)


def _slot(outs, entry, j, half):
    _, rows, cols, oname, kind, off = entry
    o = outs[oname]
    hr = rows // 2
    if kind == "col":
        cs = pl.ds(pl.multiple_of(j * (2 * SHARD_H) + off, 128), cols)
        return o.at[pl.ds(pl.multiple_of(half * hr, 16), hr), cs]
    if kind == "row":
        return o.at[pl.ds(pl.multiple_of(j * rows + half * hr, 16), hr), :]
    return o.at[j, pl.ds(pl.multiple_of(half * hr, 16), hr), :]


HBM = pl.BlockSpec(memory_space=pltpu.HBM)
SEM = pl.BlockSpec(memory_space=pltpu.SEMAPHORE)


def _ici_copy(outs, entry, j, c, to, send, recv, k):
    ref = _slot(outs, entry, j, c)
    return pltpu.make_async_remote_copy(src_ref=ref, dst_ref=ref, send_sem=send.at[k], recv_sem=recv.at[k],
                                        device_id=to, device_id_type=MESH)


GATHER_GROUPS = [["gu1"], ["d1"], ["win4"], ["so", "ao", "out", "gu2", "d2"]]


def _gather_ici_start(placed, groups, tag, after):
    names = [k for grp in groups for k in grp]
    bigs = [[b for b in BIG if b[3] in grp] for grp in groups]
    ng = len(groups)
    n_in = len(names) + 1

    def body(*refs):
        sems = refs[n_in:n_in + 2 * ng]
        outs = dict(zip(names, refs[n_in + 2 * ng:n_in + 2 * ng + len(names)]))
        token = refs[-1]
        x, y, c, peers = _place()
        for gi, big in enumerate(bigs):
            for i, entry in enumerate(big):
                for k, (px, py) in enumerate(peers):
                    _ici_copy(outs, entry, 2 * x + y, c, (px, py, c), sems[2 * gi], sems[2 * gi + 1], 3 * i + k).start()
        token[...] = jnp.zeros_like(token)

    sem_shapes = [pltpu.SemaphoreType.DMA((3 * len(big),)) for big in bigs for _ in range(2)]
    res = pl.pallas_call(
        body, name="gather_ici_start_" + tag,
        in_specs=[HBM] * len(names) + [pl.BlockSpec(memory_space=pl.ANY)],
        out_specs=[SEM] * (2 * ng) + [HBM] * len(names) + [pl.BlockSpec(memory_space=pltpu.VMEM)],
        out_shape=sem_shapes + [pltpu.HBM(GATHERED[k], BF16) for k in names] + [jax.ShapeDtypeStruct((8, 128), F32)],
        input_output_aliases={i: i + 2 * ng for i in range(len(names))},
        compiler_params=pltpu.CompilerParams(has_side_effects=pltpu.SideEffectType.DATAFLOW_SIDE_EFFECTING),
    )(*[pltpu.with_memory_space_constraint(placed[k], pltpu.HBM) for k in names], after)
    sems = [(res[2 * gi], res[2 * gi + 1]) for gi in range(ng)]
    return sems, dict(zip(names, res[2 * ng:2 * ng + len(names)])), res[-1]


def _gather_ici_wait(send, recv, arrays, names, after, tag):
    big = [b for b in BIG if b[3] in names]

    def body(*refs):
        outs = dict(zip(names, refs[:len(names)]))
        send_ref, recv_ref = refs[len(names)], refs[len(names) + 1]
        x, y, c, peers = _place()
        for i, entry in enumerate(big):
            for k, (px, py) in enumerate(peers):
                mine = _ici_copy(outs, entry, 2 * x + y, c, (px, py, c), send_ref, recv_ref, 3 * i + k)
                mine.wait_send()
                theirs = _ici_copy(outs, entry, 2 * px + py, c, (px, py, c), send_ref, recv_ref, 3 * i + k)
                theirs.wait_recv()

    res = pl.pallas_call(
        body, name="gather_ici_wait_" + tag,
        in_specs=[HBM] * len(names) + [SEM, SEM, pl.BlockSpec(memory_space=pl.ANY)],
        out_specs=[HBM] * len(names),
        out_shape=[pltpu.HBM(GATHERED[k], BF16) for k in names],
        input_output_aliases={i: i for i in range(len(names))},
        compiler_params=pltpu.CompilerParams(has_side_effects=pltpu.SideEffectType.DATAFLOW_SIDE_EFFECTING),
    )(*[arrays[k] for k in names], send, recv, after)
    return dict(zip(names, res))


def _gather_d2d(arrays, names, tag):
    big = [b for b in BIG if b[3] in names]
    n = len(big)

    def body(*refs):
        outs = dict(zip(names, refs[len(names):2 * len(names)]))
        fsend, frecv = refs[2 * len(names):]
        x, y, c, peers = _place()
        cps = []
        for i, entry in enumerate(big):
            for k, (px, py) in enumerate(peers):
                cp = _ici_copy(outs, entry, 2 * px + py, c, (x, y, 1 - c), fsend, frecv, 3 * i + k)
                cp.start()
                cps.append(cp)
        for i, entry in enumerate(big):
            for k, (px, py) in enumerate(peers):
                _ici_copy(outs, entry, 2 * px + py, 1 - c, (x, y, 1 - c), fsend, frecv, 3 * i + k).wait_recv()
        for cp in cps:
            cp.wait_send()

    res = pl.pallas_call(
        body, name="gather_d2d_" + tag,
        in_specs=[ANY] * len(names), out_specs=[ANY] * len(names),
        out_shape=[jax.ShapeDtypeStruct(GATHERED[k], BF16) for k in names],
        input_output_aliases={i: i for i in range(len(names))},
        scratch_shapes=[pltpu.SemaphoreType.DMA((3 * n,))] * 2,
    )(*[arrays[k] for k in names])
    return dict(zip(names, res))


def _win_pieces():
    pieces = []
    for g0, wd, i0 in SEGS.values():
        for j in range(N_CHIPS):
            lo, hi = max(g0, j * SHARD_IN), min(g0 + wd, (j + 1) * SHARD_IN)
            if lo < hi:
                pieces.append((j, lo - j * SHARD_IN, hi - j * SHARD_IN, i0 + lo - g0))
    return pieces


def _win_to_internal(win4):
    tr = 128

    def body(i_ref, o_ref):
        for j, s0, s1, d0 in _win_pieces():
            o_ref[:, d0:d0 + s1 - s0] = i_ref[j, :, s0:s1]
        o_ref[:, PROJ_W:] = jnp.zeros((tr, PROJ_PAD - PROJ_W), o_ref.dtype)

    return pl.pallas_call(
        body, name="win_to_internal", grid=(D // tr,),
        in_specs=[pl.BlockSpec((N_CHIPS, tr, SHARD_IN), lambda i: (0, i, 0))],
        out_specs=pl.BlockSpec((tr, PROJ_PAD), lambda i: (i, 0)),
        out_shape=jax.ShapeDtypeStruct((D, PROJ_PAD), win4.dtype),
        compiler_params=_params(("parallel",), 40 << 20),
    )(win4)


def _win_from_internal(g):
    tr = 64

    def body(i_ref, o_ref):
        for j, s0, s1, d0 in _win_pieces():
            o_ref[j, :, s0:s1] = i_ref[:, d0:d0 + s1 - s0]

    return pl.pallas_call(
        body, name="win_from_internal", grid=(D // tr,),
        in_specs=[pl.BlockSpec((tr, PROJ_PAD), lambda i: (i, 0))],
        out_specs=pl.BlockSpec((N_CHIPS, tr, SHARD_IN), lambda i: (0, i, 0)),
        out_shape=jax.ShapeDtypeStruct((N_CHIPS, D, SHARD_IN), g.dtype),
        compiler_params=_params(("parallel",), 40 << 20),
    )(g)


def _rs_pair_exchange(grads, tag):
    n = len(grads)

    def body(*refs):
        srcs, dsts = refs[:n], refs[n:2 * n]
        send, recv = refs[2 * n:]
        x, y, c, _ = _place()
        cps = []
        for i in range(n):
            hr = srcs[i].shape[1] // 2
            cp = pltpu.make_async_remote_copy(
                src_ref=srcs[i].at[:, pl.ds(pl.multiple_of((1 - c) * hr, 16), hr), :], dst_ref=dsts[i],
                send_sem=send.at[i], recv_sem=recv.at[i], device_id=(x, y, 1 - c), device_id_type=MESH)
            cp.start()
            cps.append(cp)
        for cp in cps:
            cp.wait()

    return pl.pallas_call(
        body, name="rs_pair_exchange_" + tag, in_specs=[ANY] * n, out_specs=[ANY] * n,
        out_shape=[jax.ShapeDtypeStruct((g.shape[0], g.shape[1] // 2, g.shape[2]), F32) for g in grads],
        scratch_shapes=[pltpu.SemaphoreType.DMA((n,))] * 2,
    )(*grads)


def _pair_copy(src, dst, c, to, send, recv, k):
    hr = src.shape[1] // 2
    return pltpu.make_async_remote_copy(
        src_ref=src.at[:, pl.ds(pl.multiple_of((1 - c) * hr, 16), hr), :], dst_ref=dst,
        send_sem=send.at[k], recv_sem=recv.at[k], device_id=to, device_id_type=MESH)


def _rs_pair_start(grads, carried):
    n = len(grads)

    def body(*refs):
        send, recv = refs[2 * n + 1], refs[2 * n + 2]
        srcs, dsts = refs[2 * n + 3:3 * n + 3], refs[3 * n + 3:4 * n + 3]
        x, y, c, _ = _place()
        for i in range(n):
            _pair_copy(srcs[i], dsts[i], c, (x, y, 1 - c), send, recv, i).start()

    lands = [lax.empty((g.shape[0], g.shape[1] // 2, g.shape[2]), F32) for g in grads]
    res = pl.pallas_call(
        body, name="rs_pair_start",
        in_specs=[HBM] * (2 * n + 1), out_specs=[SEM, SEM] + [HBM] * (2 * n + 1),
        out_shape=[pltpu.SemaphoreType.DMA((n,)), pltpu.SemaphoreType.DMA((n,))]
        + [pltpu.HBM(g.shape, F32) for g in grads] + [pltpu.HBM(l.shape, F32) for l in lands]
        + [pltpu.HBM(carried.shape, carried.dtype)],
        input_output_aliases={i: i + 2 for i in range(2 * n + 1)},
        compiler_params=pltpu.CompilerParams(has_side_effects=pltpu.SideEffectType.DATAFLOW_SIDE_EFFECTING),
    )(*[pltpu.with_memory_space_constraint(a, pltpu.HBM) for a in list(grads) + lands + [carried]])
    return (res[0], res[1], list(res[2:2 + n]), list(res[2 + n:2 + 2 * n])), res[-1]


def _rs_pair_wait(send, recv, grads, lands, after):
    n = len(grads)

    def body(*refs):
        srcs, dsts = refs[:n], refs[n:2 * n]
        send_ref, recv_ref = refs[2 * n], refs[2 * n + 1]
        x, y, c, _ = _place()
        for i in range(n):
            cp = _pair_copy(srcs[i], dsts[i], c, (x, y, 1 - c), send_ref, recv_ref, i)
            cp.wait_send()
            cp.wait_recv()

    res = pl.pallas_call(
        body, name="rs_pair_wait",
        in_specs=[HBM] * (2 * n) + [SEM, SEM, pl.BlockSpec(memory_space=pl.ANY)],
        out_specs=[HBM] * (2 * n),
        out_shape=[pltpu.HBM(g.shape, F32) for g in grads] + [pltpu.HBM(l.shape, F32) for l in lands],
        input_output_aliases={i: i for i in range(2 * n)},
        compiler_params=pltpu.CompilerParams(has_side_effects=pltpu.SideEffectType.DATAFLOW_SIDE_EFFECTING),
    )(*grads, *lands, send, recv, after)
    return list(res[:n]), list(res[n:])


def _half_tile(hr):
    return _divtile(hr, 256, 16) if hr % 256 == 0 else _divtile(hr, 512, 16)


def _rs_pair_sum(g, r, c_idx, name):
    ns, rows, cols = g.shape
    hr = rows // 2
    tr = _half_tile(hr)
    nt = hr // tr

    def body(c_ref, g_ref, r_ref, ob_ref, of_ref):
        s = g_ref[...] + r_ref[...]
        ob_ref[...] = s.astype(BF16)
        of_ref[...] = s

    blk = pl.BlockSpec((None, tr, cols), lambda j, t, c_ref: (j, t, 0))
    return pl.pallas_call(
        body, name=name,
        grid_spec=pltpu.PrefetchScalarGridSpec(
            num_scalar_prefetch=1, grid=(ns, nt),
            in_specs=[pl.BlockSpec((None, tr, cols), lambda j, t, c_ref: (j, c_ref[0] * nt + t, 0)), blk],
            out_specs=[blk, blk]),
        out_shape=[jax.ShapeDtypeStruct((ns, hr, cols), BF16), jax.ShapeDtypeStruct((ns, hr, cols), F32)],
        compiler_params=_params(("parallel", "parallel"), 48 << 20),
    )(c_idx, g, r)


def _rs_chip_start(parts, tag):
    n = len(parts)

    def body(*refs):
        send, recv = refs[2 * n], refs[2 * n + 1]
        srcs, dsts = refs[2 * n + 2:3 * n + 2], refs[3 * n + 2:4 * n + 2]
        token = refs[-1]
        x, y, c, peers = _place()
        for i in range(n):
            for k, (px, py) in enumerate(peers):
                pltpu.make_async_remote_copy(
                    src_ref=srcs[i].at[2 * px + py], dst_ref=dsts[i].at[k],
                    send_sem=send.at[3 * i + k], recv_sem=recv.at[3 * i + k],
                    device_id=(px, py, c), device_id_type=MESH).start()
        token[...] = jnp.zeros_like(token)

    lands = [lax.empty((3,) + p.shape[1:], BF16) for p in parts]
    res = pl.pallas_call(
        body, name="rs_chip_start_" + tag,
        in_specs=[HBM] * (2 * n),
        out_specs=[SEM, SEM] + [HBM] * (2 * n) + [pl.BlockSpec(memory_space=pltpu.VMEM)],
        out_shape=[pltpu.SemaphoreType.DMA((3 * n,)), pltpu.SemaphoreType.DMA((3 * n,))]
        + [pltpu.HBM(p.shape, BF16) for p in parts] + [pltpu.HBM(l.shape, BF16) for l in lands]
        + [jax.ShapeDtypeStruct((8, 128), F32)],
        input_output_aliases={i: i + 2 for i in range(2 * n)},
        compiler_params=pltpu.CompilerParams(has_side_effects=pltpu.SideEffectType.DATAFLOW_SIDE_EFFECTING),
    )(*[pltpu.with_memory_space_constraint(a, pltpu.HBM) for a in list(parts) + lands])
    return res[0], res[1], list(res[2:2 + n]), list(res[2 + n:2 + 2 * n]), res[-1]


def _rs_chip_wait(send, recv, parts, lands, after, tag):
    n = len(parts)

    def body(*refs):
        srcs, dsts = refs[:n], refs[n:2 * n]
        send_ref, recv_ref = refs[2 * n], refs[2 * n + 1]
        x, y, c, peers = _place()
        for i in range(n):
            for k, (px, py) in enumerate(peers):
                cp = pltpu.make_async_remote_copy(
                    src_ref=srcs[i].at[2 * px + py], dst_ref=dsts[i].at[k],
                    send_sem=send_ref.at[3 * i + k], recv_sem=recv_ref.at[3 * i + k],
                    device_id=(px, py, c), device_id_type=MESH)
                cp.wait_send()
                cp.wait_recv()

    res = pl.pallas_call(
        body, name="rs_chip_wait_" + tag,
        in_specs=[HBM] * (2 * n) + [SEM, SEM, pl.BlockSpec(memory_space=pl.ANY)],
        out_specs=[HBM] * (2 * n),
        out_shape=[pltpu.HBM(p.shape, BF16) for p in parts] + [pltpu.HBM(l.shape, BF16) for l in lands],
        input_output_aliases={i: i for i in range(2 * n)},
        compiler_params=pltpu.CompilerParams(has_side_effects=pltpu.SideEffectType.DATAFLOW_SIDE_EFFECTING),
    )(*parts, *lands, send, recv, after)
    return list(res[n:])


def _rs_final_sum(own, got, chip_idx, c_idx, name):
    ns, hr, cols = own.shape
    tr = _half_tile(hr)
    nt = hr // tr

    def body(chip_ref, c_ref, o_ref, g_ref, out_ref):
        s = o_ref[...]
        for k in range(3):
            s = s + g_ref[k].astype(F32)
        out_ref[...] = s

    return pl.pallas_call(
        body, name=name,
        grid_spec=pltpu.PrefetchScalarGridSpec(
            num_scalar_prefetch=2, grid=(nt,),
            in_specs=[pl.BlockSpec((None, tr, cols), lambda t, chip_ref, c_ref: (chip_ref[0], t, 0)),
                      pl.BlockSpec((3, tr, cols), lambda t, chip_ref, c_ref: (0, t, 0))],
            out_specs=pl.BlockSpec((tr, cols), lambda t, chip_ref, c_ref: (c_ref[0] * nt + t, 0))),
        out_shape=jax.ShapeDtypeStruct((2 * hr, cols), F32),
        compiler_params=_params(("parallel",), 48 << 20),
    )(chip_idx, c_idx, own, got)


def _rs_share_halves(fulls, tag):
    n = len(fulls)

    def body(*refs):
        dsts = refs[n:2 * n]
        send, recv = refs[2 * n:]
        x, y, c, _ = _place()
        cps = []
        for i in range(n):
            hr = dsts[i].shape[0] // 2
            rows = dsts[i].at[pl.ds(pl.multiple_of(c * hr, 8), hr), :]
            cp = pltpu.make_async_remote_copy(src_ref=rows, dst_ref=rows, send_sem=send.at[i], recv_sem=recv.at[i],
                                              device_id=(x, y, 1 - c), device_id_type=MESH)
            cp.start()
            cps.append(cp)
        for i in range(n):
            hr = dsts[i].shape[0] // 2
            other = dsts[i].at[pl.ds(pl.multiple_of((1 - c) * hr, 8), hr), :]
            pltpu.make_async_remote_copy(src_ref=other, dst_ref=other, send_sem=send.at[i], recv_sem=recv.at[i],
                                         device_id=(x, y, 1 - c), device_id_type=MESH).wait_recv()
        for cp in cps:
            cp.wait_send()

    return pl.pallas_call(
        body, name="rs_share_halves_" + tag, in_specs=[ANY] * n, out_specs=[ANY] * n,
        out_shape=[jax.ShapeDtypeStruct(f.shape, F32) for f in fulls],
        input_output_aliases={i: i for i in range(n)},
        scratch_shapes=[pltpu.SemaphoreType.DMA((n,))] * 2,
    )(*fulls)


def _all_reduce_small(v):
    rows = v.shape[0]

    def body(v_ref, o_ref, buf, send, recv):
        x, y, c, _ = _place()
        me = 4 * x + 2 * y + c
        buf[me] = v_ref[...]
        cps = []
        for d in range(1, 8):
            px, py, pc = x ^ (d >> 2), y ^ ((d >> 1) & 1), c ^ (d & 1)
            cp = pltpu.make_async_remote_copy(src_ref=v_ref, dst_ref=buf.at[me], send_sem=send.at[d - 1],
                                              recv_sem=recv.at[d - 1], device_id=(px, py, pc), device_id_type=MESH)
            cp.start()
            cps.append(cp)
        for d in range(1, 8):
            px, py, pc = x ^ (d >> 2), y ^ ((d >> 1) & 1), c ^ (d & 1)
            pltpu.make_async_remote_copy(src_ref=v_ref, dst_ref=buf.at[4 * px + 2 * py + pc], send_sem=send.at[d - 1],
                                         recv_sem=recv.at[d - 1], device_id=(px, py, pc),
                                         device_id_type=MESH).wait_recv()
        for cp in cps:
            cp.wait_send()
        acc = buf[0]
        for d in range(1, 8):
            acc = acc + buf[d]
        o_ref[...] = acc

    vm = pl.BlockSpec(memory_space=pltpu.VMEM)
    return pl.pallas_call(
        body, name="all_reduce_small", in_specs=[vm], out_specs=vm,
        out_shape=jax.ShapeDtypeStruct((rows, 128), F32),
        scratch_shapes=[pltpu.VMEM((8, rows, 128), F32), pltpu.SemaphoreType.DMA((7,)), pltpu.SemaphoreType.DMA((7,))],
    )(v)


def _adamw(w, g, m, v, name, g_col_blk=0):
    rows, cols = w.shape
    tr = _divtile(rows, max(8, (2 << 20) // (4 * cols) // 8 * 8), 8)

    def body(w_ref, g_ref, m_ref, v_ref, go_ref, d_ref, mo_ref, vo_ref):
        gv = g_ref[...]
        mn = ADAM_B1 * m_ref[...] + (1.0 - ADAM_B1) * gv
        vn = ADAM_B2 * v_ref[...] + (1.0 - ADAM_B2) * (gv * gv)
        m_hat = mn / (1.0 - ADAM_B1 ** ADAM_STEP)
        v_hat = vn / (1.0 - ADAM_B2 ** ADAM_STEP)
        go_ref[...] = gv
        d_ref[...] = -ADAM_LR * (m_hat / (jnp.sqrt(v_hat) + ADAM_EPS) + ADAM_WD * w_ref[...])
        mo_ref[...] = mn
        vo_ref[...] = vn

    blk = pl.BlockSpec((tr, cols), lambda i: (i, 0))
    return pl.pallas_call(
        body, name=name, grid=(rows // tr,),
        in_specs=[blk, pl.BlockSpec((tr, cols), lambda i: (i, g_col_blk)), blk, blk],
        out_specs=[blk] * 4, out_shape=[jax.ShapeDtypeStruct((rows, cols), F32)] * 4,
        compiler_params=_params(("parallel",), 48 << 20),
    )(w, g, m, v)


SMALL = ["ln1_g", "ln1_b", "conv_w", "conv_b", "dt_bias", "a_log", "d_skip", "ssd_norm_g", "attn_sinks",
         "ln2_g", "ln2_b", "ln3_g", "ln3_b"]


def _pack_rows(vs):
    parts = []
    for v in vs:
        v = v.reshape(-1)
        parts.append(jnp.pad(v, (0, (-v.shape[0]) % 128)))
    flat = jnp.concatenate(parts)
    flat = jnp.pad(flat, (0, (-flat.shape[0]) % 1024))
    return flat.reshape(-1, 128)


def _unpack_rows(packed, shapes):
    flat = packed.reshape(-1)
    out, at = [], 0
    for s in shapes:
        nel = int(np.prod(s))
        out.append(flat[at:at + nel].reshape(s))
        at += nel + (-nel) % 128
    return out


def kernel(x, positions, ffn1_w_gate, ffn1_w_up, ffn1_w_down, ln1_g, ln1_b, w_in, conv_w, conv_b, dt_bias, a_log, d_skip, ssd_norm_g, w_ssd_o, attn_sinks, w_attn_o, w_out, ln2_g, ln2_b, ffn2_w_gate, ffn2_w_up, ffn2_w_down, ln3_g, ln3_b, loss_target, m_ffn1_w_gate, m_ffn1_w_up, m_ffn1_w_down, m_ln1_g, m_ln1_b, m_w_in, m_conv_w, m_conv_b, m_dt_bias, m_a_log, m_d_skip, m_ssd_norm_g, m_w_ssd_o, m_attn_sinks, m_w_attn_o, m_w_out, m_ln2_g, m_ln2_b, m_ffn2_w_gate, m_ffn2_w_up, m_ffn2_w_down, m_ln3_g, m_ln3_b, v_ffn1_w_gate, v_ffn1_w_up, v_ffn1_w_down, v_ln1_g, v_ln1_b, v_w_in, v_conv_w, v_conv_b, v_dt_bias, v_a_log, v_d_skip, v_ssd_norm_g, v_w_ssd_o, v_attn_sinks, v_w_attn_o, v_w_out, v_ln2_g, v_ln2_b, v_ffn2_w_gate, v_ffn2_w_up, v_ffn2_w_down, v_ln3_g, v_ln3_b):
    args = dict(locals())
    wts = {n: args[n][0] for n in [b[0] for b in BIG] + SMALL}
    mom_m = {n: args["m_" + n][0] for n in wts}
    mom_v = {n: args["v_" + n][0] for n in wts}
    t = x.shape[1]
    xi, yi, ci = lax.axis_index("x"), lax.axis_index("y"), lax.axis_index("c")
    chip = 2 * xi + yi

    c_idx = ci.astype(jnp.int32).reshape(1)
    chip_idx = chip.astype(jnp.int32).reshape(1)
    placed = {o: _cast_place([wts[b[0]] for b in BIG if b[3] == o], o, chip_idx) for o in GATHERED}
    g_sems, g_flight = {}, {}

    def fetch(group, after):
        names = GATHER_GROUPS[group]
        send, recv = g_sems[group]
        landed = _gather_ici_wait(send, recv, {k: g_flight[k] for k in names}, names, after, str(group))
        got = _gather_d2d(landed, names, str(group))
        if "win4" in got:
            got["win"] = _win_to_internal(got.pop("win4"))
        return got

    sems, arrays, token = _gather_ici_start(placed, GATHER_GROUPS[:1], "first", wts["ln1_g"])
    g_sems[0] = sems[0]
    g_flight.update(arrays)
    w = fetch(0, token)
    sems, arrays, _ = _gather_ici_start(placed, GATHER_GROUPS[1:], "rest", w["gu1"])
    g_sems.update({i + 1: s for i, s in enumerate(sems)})
    g_flight.update(arrays)

    def slabs_of(gw, names):
        view = {"gu1": lambda: gw["gu1"], "gu2": lambda: gw["gu2"],
                "d1": lambda: gw["d1"].reshape(N_CHIPS, SHARD_H, D), "d2": lambda: gw["d2"].reshape(N_CHIPS, SHARD_H, D),
                "win": lambda: _win_from_internal(gw["win"]),
                "so": lambda: gw["so"].reshape(N_CHIPS, SSD_INNER // N_CHIPS, D),
                "ao": lambda: gw["ao"].reshape(N_CHIPS, D // N_CHIPS, D),
                "out": lambda: gw["out"].reshape(N_CHIPS, D // N_CHIPS, D)}
        return [view[nm]() for nm in names]

    early = ["win", "so", "ao", "out", "gu2", "d2"]
    late = ["gu1", "d1"]
    flight = {}

    def early_start(gw, win):
        flight["pair"], win = _rs_pair_start(slabs_of(gw, early), win)
        return win

    def early_mid(dh1):
        slabs, from_sib = _rs_pair_wait(*flight["pair"], dh1)
        pair = [_rs_pair_sum(g, r, c_idx, "rs_pair_sum_" + nm) for g, r, nm in zip(slabs, from_sib, early)]
        send, recv, parts, lands, token = _rs_chip_start([p[0] for p in pair], "early")
        flight.update(send=send, recv=recv, parts=parts, lands=lands, own=[p[1] for p in pair])
        return token

    early_grads = (early_start, early_mid)
    cw_rows = _pack_rows([lax.dynamic_update_slice(jnp.zeros((4, XBC), F32), wts["conv_w"], (0, chip * (XBC // N_CHIPS)))])
    cw_rows = jnp.where(ci == 0, cw_rows, 0.0)
    conv_w_full = _all_reduce_small(cw_rows)[:4 * XBC // 128].reshape(4, XBC)

    small = {n: (wts[n][None, :] if wts[n].ndim == 1 else wts[n]) for n in SMALL}
    small["conv_w"] = conv_w_full
    loss, grad_x, gw, gs = _local_step(x[0], positions[0].astype(F32)[:, None], loss_target[0], w, small,
                                       fetch=fetch, early_grads=early_grads)

    slabs = slabs_of(gw, late)
    from_sib = _rs_pair_exchange(slabs, "late")
    pair = [_rs_pair_sum(g, r, c_idx, "rs_pair_sum_" + nm) for g, r, nm in zip(slabs, from_sib, late)]
    l_send, l_recv, l_parts, l_lands, l_token = _rs_chip_start([p[0] for p in pair], "late")
    got_early = _rs_chip_wait(flight["send"], flight["recv"], flight["parts"], flight["lands"], l_token, "early")

    outs = {}
    big_src = {"ffn1_w_gate": ("gu1", 0), "ffn1_w_up": ("gu1", 1), "ffn1_w_down": ("d1", 0), "w_in": ("win", 0),
               "w_ssd_o": ("so", 0), "w_attn_o": ("ao", 0), "w_out": ("out", 0),
               "ffn2_w_gate": ("gu2", 0), "ffn2_w_up": ("gu2", 1), "ffn2_w_down": ("d2", 0)}

    def finish(names, own, got, tag):
        halves = [_rs_final_sum(o, gt, chip_idx, c_idx, "rs_final_sum_" + nm) for o, gt, nm in zip(own, got, names)]
        full = dict(zip(names, _rs_share_halves(halves, tag)))
        for nm, (src, blk) in big_src.items():
            if src in full:
                outs[nm] = _adamw(wts[nm], full[src], mom_m[nm], mom_v[nm], "adamw_" + nm, g_col_blk=blk)

    finish(early, flight["own"], got_early, "early")
    got_late = _rs_chip_wait(l_send, l_recv, l_parts, l_lands, outs["w_in"][1], "late")
    finish(late, [p[1] for p in pair], got_late, "late")

    gvec = {n: gs[n] for n in SMALL}
    gvec["dt_bias"], gvec["a_log"], gvec["d_skip"] = gs["dt_bias"][:, :64], gs["a_log"][:, :64], gs["d_skip"][:, :64]
    gvec["attn_sinks"] = gs["attn_sinks"][:, :NQ]
    red = _all_reduce_small(_pack_rows([gvec[n] for n in SMALL] + [loss]))
    shapes = [(4, XBC) if n == "conv_w" else wts[n].shape for n in SMALL] + [(1,)]
    red_list = _unpack_rows(red, shapes)
    loss_out = red_list[-1].reshape(())
    gsm = dict(zip(SMALL, red_list[:-1]))
    gsm["conv_w"] = lax.dynamic_slice_in_dim(gsm["conv_w"], chip * (XBC // N_CHIPS), XBC // N_CHIPS, axis=1)
    sm_shapes = [wts[n].shape for n in SMALL]
    res = _adamw(_pack_rows([wts[n] for n in SMALL]), _pack_rows([gsm[n] for n in SMALL]),
                 _pack_rows([mom_m[n] for n in SMALL]), _pack_rows([mom_v[n] for n in SMALL]), "adamw_small")
    res = [_unpack_rows(r, sm_shapes) for r in res]
    for i, nm in enumerate(SMALL):
        outs[nm] = tuple(r[i] for r in res)

    order = ["ffn1_w_gate", "ffn1_w_up", "ffn1_w_down", "ln1_g", "ln1_b", "w_in", "conv_w", "conv_b", "dt_bias", "a_log",
             "d_skip", "ssd_norm_g", "w_ssd_o", "attn_sinks", "w_attn_o", "w_out", "ln2_g", "ln2_b",
             "ffn2_w_gate", "ffn2_w_up", "ffn2_w_down", "ln3_g", "ln3_b"]
    result = [loss_out, grad_x[None]]
    for kind in range(4):
        result += [outs[nm][kind][None] for nm in order]
    return tuple(result)
```

```python
import functools
import math

import numpy as np
import jax
import jax.numpy as jnp
from jax import lax
from jax.experimental import pallas as pl
from jax.experimental.pallas import tpu as pltpu

F32 = jnp.float32
BF16 = jnp.bfloat16
HI = lax.Precision.HIGHEST

D = 2048
FFN_H = 5632
SSD_INNER = 4096
SSD_HEADS = 64
SSD_P = 64
SSD_G = 8
SSD_R = 8
SSD_N = 128
CHUNK = 128
XBC = 6144
NQ = 32
NKV = 4
HD = 64
QW = 2048
KVW = 256
WINDOW = 128
ROPE_THETA = 10000.0
ALPHA = 2.0 ** 0.25
LN_EPS = 1e-5
RMS_EPS = 1e-5
PROJ_W = 16960
N_CHIPS = 4
SHARD_IN = PROJ_W // N_CHIPS
SHARD_H = FFN_H // N_CHIPS

SEGS = {
    "z": (0, 4096, 0),
    "xbc": (4096, 6144, 10240),
    "dt": (10240, 64, 16896),
    "q": (10304, 2048, 8192),
    "k": (12352, 256, 16384),
    "v": (12608, 256, 16640),
    "gs": (12864, 2048, 4096),
    "ga": (14912, 2048, 6144),
}
PROJ_PAD = 17024

ADAM_LR = 0.001
ADAM_B1 = 0.9
ADAM_B2 = 0.999
ADAM_EPS = 1e-08
ADAM_WD = 0.01
ADAM_STEP = 10

VMEM_CAP = 60 * 1024 * 1024


def _params(sem, vmem_bytes):
    return pltpu.CompilerParams(dimension_semantics=sem, vmem_limit_bytes=int(min(VMEM_CAP, vmem_bytes)))


def _divtile(n, cap, q=128):
    best = None
    for d in range(q, min(n, cap) + 1, q):
        if n % d == 0:
            best = d
    return n if best is None else best


def _sigmoid(x):
    return 0.5 * jnp.tanh(0.5 * x) + 0.5


def _mm(a, b, mode, out_dtype, name, add=None, add_scale=1.0, caps=(1024, 1024, 2048), n_slabs=1):
    if mode == "nn":
        (m, k), (k2, n) = a.shape, b.shape
    elif mode == "nt":
        (m, k), (n, k2) = a.shape, b.shape
    else:
        (k, m), (k2, n) = a.shape, b.shape
    assert k == k2, (a.shape, b.shape, mode)
    tm, tn, tk = _divtile(m, caps[0]), _divtile(n // n_slabs, caps[1]), _divtile(k, caps[2])
    nk = k // tk
    per_slab = n // n_slabs // tn
    dims = {"nn": ((1,), (0,)), "nt": ((1,), (1,)), "tn": ((0,), (0,))}[mode]
    has_add = add is not None

    def body(*refs):
        if has_add:
            a_ref, b_ref, add_ref, o_ref = refs[:4]
            scr = refs[4:]
        else:
            a_ref, b_ref, o_ref = refs[:3]
            add_ref = None
            scr = refs[3:]
        part = lax.dot_general(a_ref[...].astype(BF16), b_ref[...].astype(BF16), (dims, ((), ())),
                               preferred_element_type=F32)

        def finish(acc):
            if has_add:
                acc = acc + add_scale * add_ref[...].astype(F32)
            o_ref[...] = acc.astype(o_ref.dtype)

        if nk == 1:
            finish(part)
        else:
            acc_ref = scr[0]
            kk = pl.program_id(2)

            @pl.when(kk == 0)
            def _():
                acc_ref[...] = part

            @pl.when(kk > 0)
            def _():
                acc_ref[...] += part

            @pl.when(kk == nk - 1)
            def _():
                finish(acc_ref[...])

    if mode == "nn":
        a_spec = pl.BlockSpec((tm, tk), lambda i, j, kk: (i, kk))
        b_spec = pl.BlockSpec((tk, tn), lambda i, j, kk: (kk, j))
    elif mode == "nt":
        a_spec = pl.BlockSpec((tm, tk), lambda i, j, kk: (i, kk))
        b_spec = pl.BlockSpec((tn, tk), lambda i, j, kk: (j, kk))
    else:
        a_spec = pl.BlockSpec((tk, tm), lambda i, j, kk: (kk, i))
        b_spec = pl.BlockSpec((tk, tn), lambda i, j, kk: (kk, j))
    o_spec = pl.BlockSpec((tm, tn), lambda i, j, kk: (i, j))
    out_shape = jax.ShapeDtypeStruct((m, n), out_dtype)
    if n_slabs > 1:
        assert not has_add
        o_spec = pl.BlockSpec((None, tm, tn), lambda i, j, kk: (j // per_slab, i, j % per_slab))
        out_shape = jax.ShapeDtypeStruct((n_slabs, m, n // n_slabs), out_dtype)
    in_specs = [a_spec, b_spec] + ([o_spec] if has_add else [])
    args = (a, b) + ((add,) if has_add else ())
    osz = jnp.dtype(out_dtype).itemsize
    vmem = (2 * (tm * tk * a.dtype.itemsize + tk * tn * b.dtype.itemsize) + 2 * tm * tn * osz
            + (2 * tm * tn * add.dtype.itemsize if has_add else 0) + 2 * tm * tn * 4
            + 2 * (tm * tk + tk * tn) + (8 << 20))
    return pl.pallas_call(
        body, name=name, grid=(m // tm, n // tn, nk),
        in_specs=in_specs, out_specs=o_spec, out_shape=out_shape,
        scratch_shapes=[pltpu.VMEM((tm, tn), F32)] if nk > 1 else [],
        compiler_params=_params(("parallel", "parallel", "arbitrary"), vmem),
    )(*args)


def _mm_swiglu(a, b, name):
    m, k = a.shape
    w = SHARD_H
    tm = _divtile(m, 512)

    def body(a_ref, b_ref, gu_ref, act_ref):
        gu = jnp.dot(a_ref[...], b_ref[...], preferred_element_type=F32)
        g = gu[:, :w]
        gu_ref[...] = gu.astype(BF16)
        act_ref[...] = (g * _sigmoid(g) * gu[:, w:]).astype(BF16)

    return pl.pallas_call(
        body, name=name, grid=(N_CHIPS, m // tm),
        in_specs=[pl.BlockSpec((tm, k), lambda j, i: (i, 0)), pl.BlockSpec((k, 2 * w), lambda j, i: (0, j))],
        out_specs=[pl.BlockSpec((tm, 2 * w), lambda j, i: (i, j)), pl.BlockSpec((tm, w), lambda j, i: (i, j))],
        out_shape=[jax.ShapeDtypeStruct((m, 2 * FFN_H), BF16), jax.ShapeDtypeStruct((m, FFN_H), BF16)],
        compiler_params=_params(("parallel", "parallel"), 56 << 20),
    )(a, b)


def _swiglu_bwd(gu, da, name):
    t = gu.shape[0]
    tt = _divtile(t, 512)
    w = SHARD_H

    def body(gu_ref, da_ref, o_ref):
        g = gu_ref[:, :w].astype(F32)
        u = gu_ref[:, w:].astype(F32)
        d = da_ref[...].astype(F32)
        s = _sigmoid(g)
        o_ref[:, :w] = (d * u * (s * (1.0 + g * (1.0 - s)))).astype(BF16)
        o_ref[:, w:] = (d * (g * s)).astype(BF16)

    return pl.pallas_call(
        body, name=name, grid=(t // tt, N_CHIPS),
        in_specs=[pl.BlockSpec((tt, 2 * w), lambda i, j: (i, j)), pl.BlockSpec((tt, w), lambda i, j: (i, j))],
        out_specs=pl.BlockSpec((tt, 2 * w), lambda i, j: (i, j)),
        out_shape=jax.ShapeDtypeStruct((t, 2 * FFN_H), BF16),
        compiler_params=_params(("parallel", "parallel"), 40 << 20),
    )(gu, da)


def _ln_fwd(base, f, g, b, c, name, target=None):
    t = base.shape[0]
    tt = _divtile(t, 256)
    with_loss = target is not None

    def body(*refs):
        if with_loss:
            base_ref, f_ref, g_ref, b_ref, tg_ref, h_ref, hb_ref, xh_ref, rs_ref, dh_ref, loss_ref = refs
        else:
            base_ref, f_ref, g_ref, b_ref, h_ref, hb_ref, xh_ref, rs_ref = refs
        r = ALPHA * base_ref[...] + c * f_ref[...]
        mu = jnp.mean(r, axis=-1, keepdims=True)
        xc = r - mu
        var = jnp.mean(xc * xc, axis=-1, keepdims=True)
        rstd = lax.rsqrt(var + LN_EPS)
        xh = xc * rstd
        h = xh * g_ref[...] + b_ref[...]
        h_ref[...] = h
        hb_ref[...] = h.astype(BF16)
        xh_ref[...] = xh
        rs_ref[...] = rstd
        if with_loss:
            e = h - tg_ref[...]
            dh_ref[...] = e * (1.0 / D)
            part = 0.5 * jnp.sum(jnp.sum(e * e, axis=-1, keepdims=True) * (1.0 / D), axis=0, keepdims=True)

            @pl.when(pl.program_id(0) == 0)
            def _():
                loss_ref[...] = jnp.zeros_like(loss_ref)

            loss_ref[...] += part

    row = pl.BlockSpec((tt, D), lambda i: (i, 0))
    vec = pl.BlockSpec((1, D), lambda i: (0, 0))
    col = pl.BlockSpec((tt, 1), lambda i: (i, 0))
    in_specs = [row, row, vec, vec] + ([row] if with_loss else [])
    out_specs = [row, row, row, col] + ([row, pl.BlockSpec((1, 1), lambda i: (0, 0))] if with_loss else [])
    out_shape = [jax.ShapeDtypeStruct((t, D), F32), jax.ShapeDtypeStruct((t, D), BF16),
                 jax.ShapeDtypeStruct((t, D), F32), jax.ShapeDtypeStruct((t, 1), F32)]
    if with_loss:
        out_shape += [jax.ShapeDtypeStruct((t, D), F32), jax.ShapeDtypeStruct((1, 1), F32)]
    args = (base, f, g, b) + ((target,) if with_loss else ())
    return pl.pallas_call(
        body, name=name, grid=(t // tt,), in_specs=in_specs, out_specs=out_specs, out_shape=out_shape,
        compiler_params=_params(("arbitrary",) if with_loss else ("parallel",), 48 << 20),
    )(*args)


def _ln_bwd(dy, xh, rstd, g, c, name):
    t = dy.shape[0]
    tt = _divtile(t, 256)

    def body(dy_ref, xh_ref, rs_ref, g_ref, dr_ref, drb_ref, dg_ref, db_ref):
        dyv = dy_ref[...]
        xhv = xh_ref[...]
        dxh = dyv * g_ref[...]
        m1 = jnp.mean(dxh, axis=-1, keepdims=True)
        m2 = jnp.mean(dxh * xhv, axis=-1, keepdims=True)
        dr = rs_ref[...] * (dxh - m1 - xhv * m2)
        dr_ref[...] = dr
        drb_ref[...] = (c * dr).astype(BF16)

        @pl.when(pl.program_id(0) == 0)
        def _():
            dg_ref[...] = jnp.zeros_like(dg_ref)
            db_ref[...] = jnp.zeros_like(db_ref)

        dg_ref[...] += jnp.sum(dyv * xhv, axis=0, keepdims=True)
        db_ref[...] += jnp.sum(dyv, axis=0, keepdims=True)

    row = pl.BlockSpec((tt, D), lambda i: (i, 0))
    vec = pl.BlockSpec((1, D), lambda i: (0, 0))
    col = pl.BlockSpec((tt, 1), lambda i: (i, 0))
    return pl.pallas_call(
        body, name=name, grid=(t // tt,), in_specs=[row, row, col, vec], out_specs=[row, row, vec, vec],
        out_shape=[jax.ShapeDtypeStruct((t, D), F32), jax.ShapeDtypeStruct((t, D), BF16),
                   jax.ShapeDtypeStruct((1, D), F32), jax.ShapeDtypeStruct((1, D), F32)],
        compiler_params=_params(("arbitrary",), 40 << 20),
    )(dy, xh, rstd, g)


DT_BLK = SEGS["dt"][2] // 128


def _dt_prep(proj, bias128, alog128):
    t = proj.shape[0]
    tt = _divtile(t, 1024)

    def body(p_ref, bias_ref, alog_ref, dt_ref, adt_ref):
        dtv = jax.nn.softplus(p_ref[...] + bias_ref[...])
        dt_ref[...] = dtv
        adt_ref[...] = dtv * (-jnp.exp(alog_ref[...]))

    blk = pl.BlockSpec((tt, 128), lambda i: (i, 0))
    vec = pl.BlockSpec((1, 128), lambda i: (0, 0))
    return pl.pallas_call(
        body, name="dt_prep", grid=(t // tt,),
        in_specs=[pl.BlockSpec((tt, 128), lambda i: (i, DT_BLK)), vec, vec], out_specs=[blk, blk],
        out_shape=[jax.ShapeDtypeStruct((t, 128), F32)] * 2,
        compiler_params=_params(("parallel",), 16 << 20),
    )(proj, bias128, alog128)


def _dt_bwd(dadt, dxdx, proj, bias128, alog128):
    t = proj.shape[0]
    tt = _divtile(t, 1024)

    def body(dadt_ref, dxdx_ref, p_ref, bias_ref, alog_ref, o_ref, dbias_ref, dalog_ref):
        pre = p_ref[...] + bias_ref[...]
        dtv = jax.nn.softplus(pre)
        a = -jnp.exp(alog_ref[...])
        ddt = a * dadt_ref[...] + dxdx_ref[...]
        draw = ddt * _sigmoid(pre)
        o_ref[...] = draw.astype(BF16)

        @pl.when(pl.program_id(0) == 0)
        def _():
            dbias_ref[...] = jnp.zeros_like(dbias_ref)
            dalog_ref[...] = jnp.zeros_like(dalog_ref)

        dbias_ref[...] += jnp.sum(draw, axis=0, keepdims=True)
        dalog_ref[...] += jnp.sum(dadt_ref[...] * dtv * a, axis=0, keepdims=True)

    blk = pl.BlockSpec((tt, 128), lambda i: (i, 0))
    vec = pl.BlockSpec((1, 128), lambda i: (0, 0))
    return pl.pallas_call(
        body, name="dt_bwd", grid=(t // tt,),
        in_specs=[blk, blk, pl.BlockSpec((tt, 128), lambda i: (i, DT_BLK)), vec, vec],
        out_specs=[blk, vec, vec],
        out_shape=[jax.ShapeDtypeStruct((t, 128), BF16), jax.ShapeDtypeStruct((1, 128), F32),
                   jax.ShapeDtypeStruct((1, 128), F32)],
        compiler_params=_params(("arbitrary",), 16 << 20),
    )(dadt, dxdx, proj, bias128, alog128)


CONV_CB = 512
CONV_TT = 512


def _shift_down(cur, prev8, s):
    if s == 0:
        return cur
    rolled = pltpu.roll(cur, s, 0)
    head = pltpu.roll(prev8, s, 0)
    r8 = lax.broadcasted_iota(jnp.int32, (8, 1), 0)
    top = jnp.where(r8 < s, head, rolled[:8])
    return jnp.concatenate([top, rolled[8:]], axis=0)


def _shift_up(cur, next8, s):
    if s == 0:
        return cur
    n = cur.shape[0]
    rolled = pltpu.roll(cur, n - s, 0)
    tail = pltpu.roll(next8, 8 - s, 0)
    r8 = lax.broadcasted_iota(jnp.int32, (8, 1), 0)
    bot = jnp.where(r8 >= 8 - s, tail, rolled[n - 8:])
    return jnp.concatenate([rolled[:n - 8], bot], axis=0)


def _conv_fwd(proj, conv_w, conv_b):
    t = proj.shape[0]
    tt = _divtile(t, CONV_TT)
    base = SEGS["xbc"][2] // CONV_CB
    r8 = tt // 8

    def body(u_ref, up_ref, w_ref, b_ref, o_ref):
        cur = u_ref[...]
        prev8 = jnp.where(pl.program_id(1) > 0, up_ref[...], 0.0)
        acc = b_ref[...] + w_ref[3:4, :] * cur
        for k in range(3):
            acc = acc + w_ref[k:k + 1, :] * _shift_down(cur, prev8, 3 - k)
        o_ref[...] = acc * _sigmoid(acc)

    return pl.pallas_call(
        body, name="conv_fwd", grid=(XBC // CONV_CB, t // tt),
        in_specs=[pl.BlockSpec((tt, CONV_CB), lambda c, i: (i, base + c)),
                  pl.BlockSpec((8, CONV_CB), lambda c, i: (jnp.maximum(i * r8 - 1, 0), base + c)),
                  pl.BlockSpec((4, CONV_CB), lambda c, i: (0, c)),
                  pl.BlockSpec((1, CONV_CB), lambda c, i: (0, c))],
        out_specs=pl.BlockSpec((tt, CONV_CB), lambda c, i: (i, c)),
        out_shape=jax.ShapeDtypeStruct((t, XBC), F32),
        compiler_params=_params(("parallel", "parallel"), 24 << 20),
    )(proj, proj, conv_w, conv_b)


def _conv_bwd(proj, dout, conv_w, conv_b, col0, width, name, dproj, skip=None):
    t = proj.shape[0]
    tt = _divtile(t, CONV_TT)
    nt = t // tt
    base = SEGS["xbc"][2] // CONV_CB + col0 // CONV_CB
    wb = col0 // CONV_CB
    r8 = tt // 8
    has_skip = skip is not None

    def body(*refs):
        if has_skip:
            u_ref, up_ref, d_ref, w_ref, b_ref, _, sk_ref, skw_ref, du_ref, dw_ref, db_ref, nx_ref = refs
        else:
            u_ref, up_ref, d_ref, w_ref, b_ref, _, du_ref, dw_ref, db_ref, nx_ref = refs
        i = pl.program_id(1)
        cur = u_ref[...]
        prev8 = jnp.where(i < nt - 1, up_ref[...], 0.0)
        sh = [_shift_down(cur, prev8, 3 - k) for k in range(3)] + [cur]
        pre = b_ref[...]
        for k in range(4):
            pre = pre + w_ref[k:k + 1, :] * sh[k]
        sg = _sigmoid(pre)
        dout_v = d_ref[...]
        if has_skip:
            dout_v = dout_v + sk_ref[...] * skw_ref[...]
        dpre = dout_v * (sg * (1.0 + pre * (1.0 - sg)))

        @pl.when(i == 0)
        def _():
            nx_ref[...] = jnp.zeros_like(nx_ref)
            dw_ref[...] = jnp.zeros_like(dw_ref)
            db_ref[...] = jnp.zeros_like(db_ref)

        next8 = nx_ref[...]
        du = w_ref[3:4, :] * dpre
        for s in range(1, 4):
            du = du + w_ref[3 - s:4 - s, :] * _shift_up(dpre, next8, s)
        du_ref[...] = du.astype(BF16)
        nx_ref[...] = dpre[:8]
        rows = [jnp.sum(dpre * sh[k], axis=0, keepdims=True) for k in range(4)]
        dw_ref[...] += jnp.concatenate(rows + [jnp.zeros((4, CONV_CB), F32)], axis=0)
        db_ref[...] += jnp.sum(dpre, axis=0, keepdims=True)

    rev = lambda c, i: (nt - 1 - i, c)
    p_in, p_out, p_shape = _dproj_piece(t, tt, CONV_CB, lambda c, i: (nt - 1 - i, base + c))
    in_specs = [pl.BlockSpec((tt, CONV_CB), lambda c, i: (nt - 1 - i, base + c)),
                pl.BlockSpec((8, CONV_CB), lambda c, i: (jnp.maximum((nt - 1 - i) * r8 - 1, 0), base + c)),
                pl.BlockSpec((tt, CONV_CB), rev),
                pl.BlockSpec((4, CONV_CB), lambda c, i: (0, wb + c)),
                pl.BlockSpec((1, CONV_CB), lambda c, i: (0, wb + c)), p_in]
    args = [proj, proj, dout, conv_w, conv_b, dproj]
    if has_skip:
        in_specs += [pl.BlockSpec((tt, CONV_CB), rev), pl.BlockSpec((1, CONV_CB), lambda c, i: (0, c))]
        args += [skip[0], skip[1]]
    return pl.pallas_call(
        body, name=name, grid=(width // CONV_CB, nt),
        in_specs=in_specs,
        out_specs=[p_out, pl.BlockSpec((8, CONV_CB), lambda c, i: (0, c)),
                   pl.BlockSpec((1, CONV_CB), lambda c, i: (0, c))],
        out_shape=[p_shape, jax.ShapeDtypeStruct((8, width), F32), jax.ShapeDtypeStruct((1, width), F32)],
        input_output_aliases={5: 0},
        scratch_shapes=[pltpu.VMEM((8, CONV_CB), F32)],
        compiler_params=_params(("parallel", "arbitrary"), 32 << 20),
    )(*args)


GW = SSD_R * SSD_P


def _expand8(v, passes=2):
    r = v.shape[0]
    if r < 8:
        v = jnp.broadcast_to(v, (8, SSD_R))
    ri = lax.broadcasted_iota(jnp.int32, (SSD_R, GW), 0)
    ci = lax.broadcasted_iota(jnp.int32, (SSD_R, GW), 1)
    spread = jnp.where((ci >= ri * SSD_P) & (ci < (ri + 1) * SSD_P), 1.0, 0.0)
    return _dot01(v, spread, passes)[:r]


def _head_pair_split(tile):
    first = lax.broadcasted_iota(jnp.int32, (1, 2 * SSD_P), 1) < SSD_P
    return jnp.where(first, tile, 0.0), jnp.where(first, 0.0, tile)


def _sel(rows, group):
    ri = lax.broadcasted_iota(jnp.int32, (rows, rows // group), 0)
    ci = lax.broadcasted_iota(jnp.int32, (rows, rows // group), 1)
    lo = ci * group
    return jnp.where((ri >= lo) & (ri < lo + group), 1.0, 0.0).astype(F32)


def _dot01(lhs, rhs, passes, split_lhs=True, dims=((1,), (0,))):
    val, m01 = (lhs, rhs) if split_lhs else (rhs, lhs)
    m01 = m01.astype(BF16)
    out = None
    for p in range(passes):
        piece = val.astype(BF16)
        ops = (piece, m01) if split_lhs else (m01, piece)
        d = lax.dot_general(ops[0], ops[1], (dims, ((), ())), preferred_element_type=F32)
        out = d if out is None else out + d
        if p + 1 < passes:
            val = val - piece.astype(F32)
    return out


def _ssd_chunk_terms(adt):
    li = lax.broadcasted_iota(jnp.int32, (CHUNK, CHUNK), 0)
    si = lax.broadcasted_iota(jnp.int32, (CHUNK, CHUNK), 1)
    causal = li >= si
    a_cs = _dot01(jnp.where(causal, 1.0, 0.0), adt, 3, split_lhs=False)
    a_cs_t = _dot01(adt, jnp.where(li <= si, 1.0, 0.0), 3, dims=((0,), (0,)))
    return a_cs, a_cs_t, causal


def _ssd_fwd(xc, dt3, adt3):
    t = xc.shape[0]
    nc = t // CHUNK

    gs = 2

    def body(xs_ref, b_ref, c_ref, dt_ref, adt_ref, y_ref, hp_ref, h_ref):
        @pl.when(pl.program_id(1) == 0)
        def _():
            h_ref[...] = jnp.zeros_like(h_ref)

        for gg in range(gs):
            a_cs, a_cs_t, causal = _ssd_chunk_terms(adt_ref[gg])
            a_last = a_cs[CHUNK - 1:CHUNK, :]
            h = h_ref[gg]
            hp_ref[gg, 0] = h
            xd = xs_ref[:, GW * gg:GW * (gg + 1)] * _expand8(dt_ref[gg])
            bb = b_ref[:, SSD_N * gg:SSD_N * (gg + 1)].astype(BF16)
            cbf = c_ref[:, SSD_N * gg:SSD_N * (gg + 1)].astype(BF16)
            cb = lax.dot_general(cbf, bb, (((1,), (1,)), ((), ())), preferred_element_type=F32)
            yoff = jnp.dot(cbf, h.astype(BF16), preferred_element_type=F32) * _expand8(jnp.exp(a_cs))
            for q in range(SSD_R // 2):
                lmats = []
                for r in (2 * q, 2 * q + 1):
                    seg = jnp.exp(jnp.where(causal, a_cs[:, r:r + 1] - a_cs_t[r:r + 1, :], -jnp.inf))
                    lmats.append((cb * seg).astype(BF16))
                tile = slice(2 * SSD_P * q, 2 * SSD_P * (q + 1))
                xa, xb = _head_pair_split(xd[:, tile])
                y_ref[:, GW * gg + 2 * SSD_P * q:GW * gg + 2 * SSD_P * (q + 1)] = (
                    jnp.dot(jnp.concatenate(lmats, axis=1), jnp.concatenate([xa, xb], axis=0).astype(BF16),
                            preferred_element_type=F32) + yoff[:, tile])
            xdd = (xd * _expand8(jnp.exp(a_last - a_cs))).astype(BF16)
            h_ref[gg] = _expand8(jnp.exp(a_last), 3) * h + lax.dot_general(
                bb, xdd, (((0,), (0,)), ((), ())), preferred_element_type=F32)

    nb = SSD_INNER // (gs * SSD_N)
    return pl.pallas_call(
        body, name="ssd_fwd", grid=(SSD_G // gs, nc),
        in_specs=[pl.BlockSpec((CHUNK, gs * GW), lambda g, c: (c, g)),
                  pl.BlockSpec((CHUNK, gs * SSD_N), lambda g, c: (c, nb + g)),
                  pl.BlockSpec((CHUNK, gs * SSD_N), lambda g, c: (c, nb + SSD_G // gs + g)),
                  pl.BlockSpec((gs, CHUNK, SSD_R), lambda g, c: (g, c, 0)),
                  pl.BlockSpec((gs, CHUNK, SSD_R), lambda g, c: (g, c, 0))],
        out_specs=[pl.BlockSpec((CHUNK, gs * GW), lambda g, c: (c, g)),
                   pl.BlockSpec((gs, 1, SSD_N, GW), lambda g, c: (g, c, 0, 0))],
        out_shape=[jax.ShapeDtypeStruct((t, SSD_INNER), F32), jax.ShapeDtypeStruct((SSD_G, nc, SSD_N, GW), F32)],
        scratch_shapes=[pltpu.VMEM((gs, SSD_N, GW), F32)],
        compiler_params=_params(("parallel", "arbitrary"), 32 << 20),
    )(xc, xc, xc, dt3, adt3)


def _ssd_bwd(xc, dt3, adt3, hprev, dy):
    t = xc.shape[0]
    nc = t // CHUNK

    gs = 2

    def body(xs_ref, b_ref, c_ref, dt_ref, adt_ref, hp_ref, dy_ref,
             dx_ref, db_ref, dc_ref, dadt_ref, dxdx_ref, dh_ref):
        @pl.when(pl.program_id(1) == 0)
        def _():
            dh_ref[...] = jnp.zeros_like(dh_ref)

        for gg in range(gs):
            wide = slice(GW * gg, GW * (gg + 1))
            narrow = slice(SSD_N * gg, SSD_N * (gg + 1))
            dx, db, dc, dadt, dxdx, dh_new = group_bwd(
                xs_ref[:, wide], b_ref[:, narrow], c_ref[:, narrow], dt_ref[gg], adt_ref[gg], hp_ref[gg, 0],
                dy_ref[:, wide], dh_ref[gg])
            dx_ref[:, wide] = dx
            db_ref[:, narrow] = db
            dc_ref[:, narrow] = dc
            dadt_ref[gg] = dadt
            dxdx_ref[gg] = dxdx
            dh_ref[gg] = dh_new

    def group_bwd(xs, b, c, dt, adt, hp, dyv, dh):
        a_cs, a_cs_t, causal = _ssd_chunk_terms(adt)
        a_last = a_cs[CHUNK - 1:CHUNK, :]
        e_last = jnp.exp(a_last)
        ex = _expand8(jnp.exp(a_cs))
        dtex = _expand8(jnp.exp(a_last - a_cs))
        dtx = _expand8(dt)
        sel = _sel(GW, SSD_P)
        seg8 = lambda v: _dot01(v, sel, 2)

        xd = xs * dtx
        xdd = xd * dtex
        bb = b.astype(BF16)
        cbf = c.astype(BF16)
        hpb = hp.astype(BF16)
        dhb = dh.astype(BF16)
        xdb = xd.astype(BF16)
        cb = lax.dot_general(cbf, bb, (((1,), (1,)), ((), ())), preferred_element_type=F32)
        dye = (dyv * ex).astype(BF16)
        yoff = jnp.dot(cbf, hpb, preferred_element_type=F32) * ex
        dc = lax.dot_general(dye, hpb, (((1,), (1,)), ((), ())), preferred_element_type=F32)
        bdh = jnp.dot(bb, dhb, preferred_element_type=F32)
        db = lax.dot_general(xdd.astype(BF16), dhb, (((1,), (1,)), ((), ())), preferred_element_type=F32)
        dxd_state = bdh * dtex
        q_terms = xdd * bdh
        d_a = seg8(dyv * yoff - q_terms)
        d_a_last = seg8(jnp.sum(q_terms, axis=0, keepdims=True)
                        + _expand8(e_last, 3) * jnp.sum(hp * dh, axis=0, keepdims=True))
        dh_new = (lax.dot_general(cbf, dye, (((0,), (0,)), ((), ())), preferred_element_type=F32)
                  + _expand8(e_last, 3) * dh)
        dcb = jnp.zeros((CHUNK, CHUNK), F32)
        w_all = []
        dxd_parts = []
        for q2 in range(SSD_R // 2):
            tile = slice(2 * SSD_P * q2, 2 * SSD_P * (q2 + 1))
            dy_pair = [part.astype(BF16) for part in _head_pair_split(dyv[:, tile])]
            lmats = []
            for k, r in enumerate((2 * q2, 2 * q2 + 1)):
                seg = jnp.exp(jnp.where(causal, a_cs[:, r:r + 1] - a_cs_t[r:r + 1, :], -jnp.inf))
                lmat = cb * seg
                dm = lax.dot_general(dy_pair[k], xdb[:, tile], (((1,), (1,)), ((), ())), preferred_element_type=F32)
                dcb = dcb + dm * seg
                w_all.append(dm * lmat)
                lmats.append(lmat.astype(BF16))
            dxd_parts.append(lax.dot_general(jnp.concatenate(lmats, axis=0), jnp.concatenate(dy_pair, axis=0),
                                             (((0,), (0,)), ((), ())), preferred_element_type=F32))
        row_sums = _dot01(jnp.concatenate(w_all, axis=1), _sel(SSD_R * CHUNK, CHUNK), 2)
        cs_rows = jnp.concatenate([jnp.sum(wr, axis=0, keepdims=True) for wr in w_all], axis=0)
        col_sums = _dot01(cs_rows, _sel(SSD_R, 1), 3, dims=((0,), (0,)))
        d_a = d_a + row_sums - col_sums
        li = lax.broadcasted_iota(jnp.int32, (CHUNK, SSD_R), 0)
        d_a = d_a + jnp.where(li == CHUNK - 1, d_a_last, 0.0)
        l2 = lax.broadcasted_iota(jnp.int32, (CHUNK, CHUNK), 0)
        s2 = lax.broadcasted_iota(jnp.int32, (CHUNK, CHUNK), 1)
        dadt = _dot01(jnp.where(s2 >= l2, 1.0, 0.0), d_a, 3, split_lhs=False)
        dxd = dxd_state + jnp.concatenate(dxd_parts, axis=1)
        dcbb = dcb.astype(BF16)
        db = db + lax.dot_general(dcbb, cbf, (((0,), (0,)), ((), ())), preferred_element_type=F32)
        dc = dc + jnp.dot(dcbb, bb, preferred_element_type=F32)
        return dxd * dtx, db, dc, dadt, seg8(dxd * xs), dh_new

    nb = SSD_INNER // (gs * SSD_N)
    rc = lambda g, c: (nc - 1 - c, g)
    r3 = lambda g, c: (g, nc - 1 - c, 0)
    return pl.pallas_call(
        body, name="ssd_bwd", grid=(SSD_G // gs, nc),
        in_specs=[pl.BlockSpec((CHUNK, gs * GW), rc),
                  pl.BlockSpec((CHUNK, gs * SSD_N), lambda g, c: (nc - 1 - c, nb + g)),
                  pl.BlockSpec((CHUNK, gs * SSD_N), lambda g, c: (nc - 1 - c, nb + SSD_G // gs + g)),
                  pl.BlockSpec((gs, CHUNK, SSD_R), r3),
                  pl.BlockSpec((gs, CHUNK, SSD_R), r3),
                  pl.BlockSpec((gs, 1, SSD_N, GW), lambda g, c: (g, nc - 1 - c, 0, 0)),
                  pl.BlockSpec((CHUNK, gs * GW), rc)],
        out_specs=[pl.BlockSpec((CHUNK, gs * GW), rc),
                   pl.BlockSpec((CHUNK, gs * SSD_N), rc),
                   pl.BlockSpec((CHUNK, gs * SSD_N), rc),
                   pl.BlockSpec((gs, CHUNK, SSD_R), r3),
                   pl.BlockSpec((gs, CHUNK, SSD_R), r3)],
        out_shape=[jax.ShapeDtypeStruct((t, SSD_INNER), F32),
                   jax.ShapeDtypeStruct((t, SSD_G * SSD_N), F32),
                   jax.ShapeDtypeStruct((t, SSD_G * SSD_N), F32),
                   jax.ShapeDtypeStruct((SSD_G, t, SSD_R), F32),
                   jax.ShapeDtypeStruct((SSD_G, t, SSD_R), F32)],
        scratch_shapes=[pltpu.VMEM((gs, SSD_N, GW), F32)],
        compiler_params=_params(("parallel", "arbitrary"), 48 << 20),
    )(xc, xc, xc, dt3, adt3, hprev, dy)


def _gated_norm_fwd(y, xc, proj, dexp, ng):
    t = y.shape[0]
    tt = _divtile(t, 256)

    def body(y_ref, x_ref, z_ref, d_ref, g_ref, o_ref):
        z = z_ref[...]
        y2 = (y_ref[...] + d_ref[...] * x_ref[...]) * (z * _sigmoid(z))
        for gi in range(SSD_G):
            sl = slice(GW * gi, GW * (gi + 1))
            seg = y2[:, sl]
            rinv = lax.rsqrt(jnp.mean(seg * seg, axis=-1, keepdims=True) + RMS_EPS)
            o_ref[:, sl] = (seg * rinv * g_ref[:, sl]).astype(BF16)

    row = pl.BlockSpec((tt, SSD_INNER), lambda i: (i, 0))
    vec = pl.BlockSpec((1, SSD_INNER), lambda i: (0, 0))
    return pl.pallas_call(
        body, name="gated_norm_fwd", grid=(t // tt,), in_specs=[row, row, row, vec, vec], out_specs=row,
        out_shape=jax.ShapeDtypeStruct((t, SSD_INNER), BF16),
        compiler_params=_params(("parallel",), 48 << 20),
    )(y, xc, proj, dexp, ng)


def _gated_norm_bwd(dout, y, xc, proj, dexp, ng, dproj):
    t = y.shape[0]
    tt = _divtile(t, 128)

    def body(do_ref, y_ref, x_ref, z_ref, d_ref, g_ref, _, dz_ref, dy_ref, dg_ref, dd_ref):
        z = z_ref[...]
        sg = _sigmoid(z)
        sz = z * sg
        xs = x_ref[...]
        y1 = y_ref[...] + d_ref[...] * xs
        y2 = y1 * sz
        dov = do_ref[...]

        @pl.when(pl.program_id(0) == 0)
        def _():
            dg_ref[...] = jnp.zeros_like(dg_ref)
            dd_ref[...] = jnp.zeros_like(dd_ref)

        for gi in range(SSD_G):
            sl = slice(GW * gi, GW * (gi + 1))
            seg = y2[:, sl]
            rinv = lax.rsqrt(jnp.mean(seg * seg, axis=-1, keepdims=True) + RMS_EPS)
            yn = seg * rinv
            dsl = dov[:, sl]
            dg_ref[:, sl] += jnp.sum(dsl * yn, axis=0, keepdims=True)
            dyn = dsl * g_ref[:, sl]
            dy2 = rinv * (dyn - yn * jnp.mean(dyn * yn, axis=-1, keepdims=True))
            dz_ref[:, sl] = (dy2 * y1[:, sl] * (sg[:, sl] * (1.0 + z[:, sl] * (1.0 - sg[:, sl])))).astype(BF16)
            dy1 = dy2 * sz[:, sl]
            dy_ref[:, sl] = dy1
            dd_ref[:, sl] += jnp.sum(dy1 * xs[:, sl], axis=0, keepdims=True)

    row = pl.BlockSpec((tt, SSD_INNER), lambda i: (i, 0))
    vec = pl.BlockSpec((1, SSD_INNER), lambda i: (0, 0))
    p_in, p_out, p_shape = _dproj_piece(t, tt, SSD_INNER, lambda i: (i, 0))
    return pl.pallas_call(
        body, name="gated_norm_bwd", grid=(t // tt,), in_specs=[row, row, row, row, vec, vec, p_in],
        out_specs=[p_out, row, vec, vec],
        out_shape=[p_shape, jax.ShapeDtypeStruct((t, SSD_INNER), F32),
                   jax.ShapeDtypeStruct((1, SSD_INNER), F32), jax.ShapeDtypeStruct((1, SSD_INNER), F32)],
        input_output_aliases={6: 0},
        compiler_params=_params(("arbitrary",), 48 << 20),
    )(dout, y, xc, proj, dexp, ng, dproj)


def _fold_heads(v, name):
    def body(v_ref, o_ref):
        ri = lax.broadcasted_iota(jnp.int32, (SSD_INNER, 128), 0)
        ci = lax.broadcasted_iota(jnp.int32, (SSD_INNER, 128), 1)
        fold = jnp.where((ri >= ci * SSD_P) & (ri < (ci + 1) * SSD_P), 1.0, 0.0).astype(F32)
        o_ref[...] = jnp.dot(v_ref[...], fold, preferred_element_type=F32, precision=HI)

    return pl.pallas_call(body, name=name, out_shape=jax.ShapeDtypeStruct((1, 128), F32))(v)


Q_BLK = SEGS["q"][2] // QW
K_BLK = SEGS["k"][2] // KVW
V_BLK = SEGS["v"][2] // KVW


def _rope_tables(pos_ref, invf_ref, width):
    ang = pos_ref[...] * invf_ref[...]
    lane = lax.broadcasted_iota(jnp.int32, (1, 128), 1)
    sign = jnp.where((lane % HD) < (HD // 2), -1.0, 1.0)
    cos = jnp.tile(jnp.cos(ang), (1, width // 128))
    sin = jnp.tile(sign * jnp.sin(ang), (1, width // 128))
    first = (lax.broadcasted_iota(jnp.int32, (1, width), 1) % HD) < (HD // 2)
    return cos, sin, first


def _rot_half(u, first):
    w = u.shape[1]
    return jnp.where(first, pltpu.roll(u, w - HD // 2, 1), pltpu.roll(u, HD // 2, 1))


def _rope_fwd(proj, pos, invf):
    t = proj.shape[0]
    tt = _divtile(t, 512)

    def body(q_ref, k_ref, pos_ref, invf_ref, qo_ref, ko_ref):
        cos, sin, first = _rope_tables(pos_ref, invf_ref, QW)
        q = q_ref[...]
        qr = q * cos + _rot_half(q, first) * sin
        for p in range(QW // 128):
            qo_ref[128 * p:128 * (p + 1), :] = qr[:, 128 * p:128 * (p + 1)].T.astype(BF16)
        k = k_ref[...]
        ko_ref[...] = (k * cos[:, :KVW] + _rot_half(k, first[:, :KVW]) * sin[:, :KVW]).astype(BF16)

    return pl.pallas_call(
        body, name="rope_fwd", grid=(t // tt,),
        in_specs=[pl.BlockSpec((tt, QW), lambda i: (i, Q_BLK)), pl.BlockSpec((tt, KVW), lambda i: (i, K_BLK)),
                  pl.BlockSpec((tt, 1), lambda i: (i, 0)), pl.BlockSpec((1, 128), lambda i: (0, 0))],
        out_specs=[pl.BlockSpec((QW, tt), lambda i: (0, i)), pl.BlockSpec((tt, KVW), lambda i: (i, 0))],
        out_shape=[jax.ShapeDtypeStruct((QW, t), BF16), jax.ShapeDtypeStruct((t, KVW), BF16)],
        compiler_params=_params(("parallel",), 40 << 20),
    )(proj, proj, pos, invf)


def _rope_bwd(dqt, dk, pos, invf, dproj):
    t = dk.shape[0]
    tt = _divtile(t, 512)

    def body(dq_ref, dk_ref, pos_ref, invf_ref, _, qo_ref, ko_ref):
        cos, sin, first = _rope_tables(pos_ref, invf_ref, QW)
        q = jnp.concatenate([dq_ref[128 * p:128 * (p + 1), :].T for p in range(QW // 128)], axis=1)
        qo_ref[...] = (q * cos + _rot_half(q * sin, first)).astype(BF16)
        k = dk_ref[...]
        ko_ref[...] = (k * cos[:, :KVW] + _rot_half(k * sin[:, :KVW], first[:, :KVW])).astype(BF16)

    p_in, p_out, p_shape = _dproj_piece(t, tt, QW, lambda i: (i, Q_BLK))
    return pl.pallas_call(
        body, name="rope_bwd", grid=(t // tt,),
        in_specs=[pl.BlockSpec((QW, tt), lambda i: (0, i)), pl.BlockSpec((tt, KVW), lambda i: (i, 0)),
                  pl.BlockSpec((tt, 1), lambda i: (i, 0)), pl.BlockSpec((1, 128), lambda i: (0, 0)), p_in],
        out_specs=[p_out, pl.BlockSpec((tt, KVW), lambda i: (i, 0))],
        out_shape=[p_shape, jax.ShapeDtypeStruct((t, KVW), BF16)],
        input_output_aliases={4: 0},
        compiler_params=_params(("parallel",), 40 << 20),
    )(dqt, dk, pos, invf, dproj)


def _place_cols(piece, dproj, col_blk, name):
    t, w = piece.shape
    tt = _divtile(t, 1024)

    def body(p_ref, _, o_ref):
        o_ref[...] = p_ref[...]

    p_in, p_out, p_shape = _dproj_piece(t, tt, w, lambda i: (i, col_blk))
    return pl.pallas_call(
        body, name=name, grid=(t // tt,),
        in_specs=[pl.BlockSpec((tt, w), lambda i: (i, 0)), p_in], out_specs=p_out, out_shape=p_shape,
        input_output_aliases={1: 0},
        compiler_params=_params(("parallel",), 16 << 20),
    )(piece, dproj)


GQ = NQ // NKV
NT_DIMS = (((1,), (1,)), ((), ()))
TN_DIMS = (((0,), (0,)), ((), ()))


def _attn_heads(ref, j, dtype=None):
    out = jnp.concatenate([ref[HD * h:HD * (h + 1), :] for h in range(j * GQ, (j + 1) * GQ)], axis=1)
    return out if dtype is None else out.astype(dtype)


def _attn_sink_row(s_ref, j):
    return jnp.concatenate([jnp.broadcast_to(s_ref[:, h:h + 1], (1, WINDOW)) for h in range(j * GQ, (j + 1) * GQ)],
                           axis=1)


def _attn_mask(n):
    kr = lax.broadcasted_iota(jnp.int32, (2 * WINDOW, GQ * WINDOW), 0)
    qi = lax.broadcasted_iota(jnp.int32, (2 * WINDOW, GQ * WINDOW), 1) % WINDOW
    return (kr > qi) & (kr <= qi + WINDOW) & ((n > 0) | (kr >= WINDOW))


def _attn_probs(qgt, kk, sink, mask):
    s = jnp.where(mask, jnp.dot(kk, qgt, preferred_element_type=F32) * (HD ** -0.5), -jnp.inf)
    m = jnp.maximum(jnp.max(s, axis=0, keepdims=True), sink)
    p = jnp.exp(s - m)
    ps = jnp.exp(sink - m)
    inv = 1.0 / (jnp.sum(p, axis=0, keepdims=True) + ps)
    return p * inv, ps * inv


def _attn_fwd(qt, kr, proj, sinks):
    t = kr.shape[0]
    nb = t // WINDOW

    def body(q_ref, kc_ref, kp_ref, vc_ref, vp_ref, s_ref, o_ref):
        mask = _attn_mask(pl.program_id(0))
        for j in range(NKV):
            ks = slice(HD * j, HD * (j + 1))
            kk = jnp.concatenate([kp_ref[:, ks], kc_ref[:, ks]], axis=0)
            vv = jnp.concatenate([vp_ref[:, ks], vc_ref[:, ks]], axis=0).astype(BF16)
            pn, _ = _attn_probs(_attn_heads(q_ref, j), kk, _attn_sink_row(s_ref, j), mask)
            ot = lax.dot_general(vv, pn.astype(BF16), TN_DIMS, preferred_element_type=F32).astype(BF16)
            for g in range(GQ):
                h = j * GQ + g
                o_ref[HD * h:HD * (h + 1), :] = ot[:, WINDOW * g:WINDOW * (g + 1)]

    prev = lambda n: (jnp.maximum(n - 1, 0), 0)
    return pl.pallas_call(
        body, name="attn_fwd", grid=(nb,),
        in_specs=[pl.BlockSpec((QW, WINDOW), lambda n: (0, n)),
                  pl.BlockSpec((WINDOW, KVW), lambda n: (n, 0)), pl.BlockSpec((WINDOW, KVW), prev),
                  pl.BlockSpec((WINDOW, KVW), lambda n: (n, V_BLK)),
                  pl.BlockSpec((WINDOW, KVW), lambda n: (jnp.maximum(n - 1, 0), V_BLK)),
                  pl.BlockSpec((1, 128), lambda n: (0, 0))],
        out_specs=pl.BlockSpec((QW, WINDOW), lambda n: (0, n)),
        out_shape=jax.ShapeDtypeStruct((QW, t), BF16),
        compiler_params=_params(("parallel",), 24 << 20),
    )(qt, kr, kr, proj, proj, sinks)


def _attn_bwd(qt, kr, proj, sinks, dot_, dproj):
    t = kr.shape[0]
    nb = t // WINDOW

    def body(q_ref, kc_ref, kp_ref, vc_ref, vp_ref, s_ref, do_ref, _,
             dq_ref, dk_ref, dv_ref, ds_ref, dkc_ref, dvc_ref):
        i = pl.program_id(0)
        mask = _attn_mask(nb - 1 - i)

        @pl.when(i == 0)
        def _():
            dkc_ref[...] = jnp.zeros_like(dkc_ref)
            dvc_ref[...] = jnp.zeros_like(dvc_ref)
            ds_ref[...] = jnp.zeros_like(ds_ref)

        lane = lax.broadcasted_iota(jnp.int32, (1, 128), 1)
        ds_acc = jnp.zeros((1, 128), F32)
        for j in range(NKV):
            ks = slice(HD * j, HD * (j + 1))
            kk = jnp.concatenate([kp_ref[:, ks], kc_ref[:, ks]], axis=0)
            vv = jnp.concatenate([vp_ref[:, ks], vc_ref[:, ks]], axis=0).astype(BF16)
            qgt = _attn_heads(q_ref, j)
            pn, psn = _attn_probs(qgt, kk, _attn_sink_row(s_ref, j), mask)
            dogt = _attn_heads(do_ref, j)
            dp = jnp.dot(vv, dogt, preferred_element_type=F32)
            delta = jnp.sum(dp * pn, axis=0, keepdims=True)
            dsb = (pn * (dp - delta) * (HD ** -0.5)).astype(BF16)
            dsink = -psn * delta
            dqt = lax.dot_general(kk, dsb, TN_DIMS, preferred_element_type=F32)
            for g in range(GQ):
                h = j * GQ + g
                cols = slice(WINDOW * g, WINDOW * (g + 1))
                dq_ref[HD * h:HD * (h + 1), :] = dqt[:, cols]
                ds_acc = ds_acc + jnp.where(lane == h, jnp.sum(dsink[:, cols], axis=1, keepdims=True), 0.0)
            dkk = lax.dot_general(dsb, qgt, NT_DIMS, preferred_element_type=F32)
            dvv = lax.dot_general(pn.astype(BF16), dogt, NT_DIMS, preferred_element_type=F32)
            dk_ref[:, ks] = dkk[WINDOW:] + dkc_ref[:, ks]
            dv_ref[:, ks] = (dvv[WINDOW:] + dvc_ref[:, ks]).astype(BF16)
            dkc_ref[:, ks] = dkk[:WINDOW]
            dvc_ref[:, ks] = dvv[:WINDOW]
        ds_ref[...] += ds_acc

    cur = lambda i: (nb - 1 - i, 0)
    cur_t = lambda i: (0, nb - 1 - i)
    prev = lambda i: (jnp.maximum(nb - 2 - i, 0), 0)
    p_in, p_out, p_shape = _dproj_piece(t, WINDOW, KVW, lambda i: (nb - 1 - i, V_BLK))
    return pl.pallas_call(
        body, name="attn_bwd", grid=(nb,),
        in_specs=[pl.BlockSpec((QW, WINDOW), cur_t),
                  pl.BlockSpec((WINDOW, KVW), cur), pl.BlockSpec((WINDOW, KVW), prev),
                  pl.BlockSpec((WINDOW, KVW), lambda i: (nb - 1 - i, V_BLK)),
                  pl.BlockSpec((WINDOW, KVW), lambda i: (jnp.maximum(nb - 2 - i, 0), V_BLK)),
                  pl.BlockSpec((1, 128), lambda i: (0, 0)),
                  pl.BlockSpec((QW, WINDOW), cur_t), p_in],
        out_specs=[pl.BlockSpec((QW, WINDOW), cur_t), pl.BlockSpec((WINDOW, KVW), cur),
                   p_out, pl.BlockSpec((1, 128), lambda i: (0, 0))],
        out_shape=[jax.ShapeDtypeStruct((QW, t), F32), jax.ShapeDtypeStruct((t, KVW), F32),
                   p_shape, jax.ShapeDtypeStruct((1, 128), F32)],
        input_output_aliases={7: 2},
        scratch_shapes=[pltpu.VMEM((WINDOW, KVW), F32), pltpu.VMEM((WINDOW, KVW), F32)],
        compiler_params=_params(("arbitrary",), 32 << 20),
    )(qt, kr, kr, proj, proj, sinks, dot_, dproj)


GS_BLK = SEGS["gs"][2] // D
GA_BLK = SEGS["ga"][2] // D


def _merge_fwd(ys, ya, proj):
    t = ys.shape[0]
    tt = _divtile(t, 256)

    def body(ys_ref, ya_ref, gs_ref, ga_ref, o_ref):
        o_ref[...] = (_sigmoid(gs_ref[...]) * ys_ref[...] + _sigmoid(ga_ref[...]) * ya_ref[...]).astype(BF16)

    row = pl.BlockSpec((tt, D), lambda i: (i, 0))
    return pl.pallas_call(
        body, name="merge_fwd", grid=(t // tt,),
        in_specs=[row, row, pl.BlockSpec((tt, D), lambda i: (i, GS_BLK)), pl.BlockSpec((tt, D), lambda i: (i, GA_BLK))],
        out_specs=row, out_shape=jax.ShapeDtypeStruct((t, D), BF16),
        compiler_params=_params(("parallel",), 32 << 20),
    )(ys, ya, proj, proj)


def _dproj_piece(t, rows, width, index_map):
    return (pl.BlockSpec(memory_space=pl.ANY), pl.BlockSpec((rows, width), index_map),
            jax.ShapeDtypeStruct((t, PROJ_PAD), BF16))


def _merge_bwd(dm, ys, ya, proj, dproj):
    t = ys.shape[0]
    tt = _divtile(t, 256)

    def body(dm_ref, ys_ref, ya_ref, gs_ref, ga_ref, _, dys_ref, dya_ref, dg_ref):
        d = dm_ref[...]
        s = _sigmoid(gs_ref[...])
        a = _sigmoid(ga_ref[...])
        dys_ref[...] = (d * s).astype(BF16)
        dya_ref[...] = (d * a).astype(BF16)
        dg_ref[:, :D] = (d * ys_ref[...] * (s * (1.0 - s))).astype(BF16)
        dg_ref[:, D:] = (d * ya_ref[...] * (a * (1.0 - a))).astype(BF16)

    row = pl.BlockSpec((tt, D), lambda i: (i, 0))
    p_in, p_out, p_shape = _dproj_piece(t, tt, 2 * D, lambda i: (i, SEGS["gs"][2] // (2 * D)))
    return pl.pallas_call(
        body, name="merge_bwd", grid=(t // tt,),
        in_specs=[row, row, row, pl.BlockSpec((tt, D), lambda i: (i, GS_BLK)),
                  pl.BlockSpec((tt, D), lambda i: (i, GA_BLK)), p_in],
        out_specs=[row, row, p_out], out_shape=[jax.ShapeDtypeStruct((t, D), BF16)] * 2 + [p_shape],
        input_output_aliases={5: 2},
        compiler_params=_params(("parallel",), 40 << 20),
    )(dm, ys, ya, proj, proj, dproj)


def _pad128(v):
    return jnp.pad(v, ((0, 0), (0, 128 - v.shape[1])))


def _group_major(v):
    t = v.shape[0]
    return jnp.transpose(v[:, :SSD_HEADS].reshape(t, SSD_G, SSD_R), (1, 0, 2))


def _token_major(v3):
    t = v3.shape[1]
    return _pad128(jnp.transpose(v3, (1, 0, 2)).reshape(t, SSD_HEADS))


def _local_step(x, pos, target, w, small, fetch=None, early_grads=None):
    w = dict(w)
    xb = x.astype(BF16)
    gu1, a1 = _mm_swiglu(xb, w["gu1"], "ffn1_gu")
    if fetch is not None:
        w.update(fetch(1, a1))
    f1 = _mm(a1, w["d1"], "nn", F32, "ffn1_down", caps=(512, 1024, FFN_H))
    h1, h1b, xh1, rs1 = _ln_fwd(x, f1, small["ln1_g"], small["ln1_b"], 0.5, "ln1_fwd")
    if fetch is not None:
        w.update(fetch(2, h1b))
    proj = _mm(h1b, w["win"], "nn", F32, "proj", caps=(1024, 896, 2048))
    if fetch is not None:
        w.update(fetch(3, proj))
    bias128 = _pad128(small["dt_bias"])
    alog128 = _pad128(small["a_log"])
    dt, adt = _dt_prep(proj, bias128, alog128)
    dt3, adt3 = _group_major(dt), _group_major(adt)
    xc = _conv_fwd(proj, small["conv_w"], small["conv_b"])
    y_ssd, hprev = _ssd_fwd(xc, dt3, adt3)
    dexp = jnp.repeat(small["d_skip"], SSD_P, axis=1)
    ysn = _gated_norm_fwd(y_ssd, xc, proj, dexp, small["ssd_norm_g"])
    ys = _mm(ysn, w["so"], "nn", F32, "ssd_out")
    invf = jnp.tile(ROPE_THETA ** (-jnp.arange(HD // 2, dtype=F32) * 2.0 / HD), 4)[None, :]
    qt, kr = _rope_fwd(proj, pos, invf)
    sinks128 = _pad128(small["attn_sinks"])
    ot = _attn_fwd(qt, kr, proj, sinks128)
    ya = _mm(ot, w["ao"], "tn", F32, "attn_out")
    mg = _merge_fwd(ys, ya, proj)
    mix = _mm(mg, w["out"], "nn", F32, "mix_out")
    h2, h2b, xh2, rs2 = _ln_fwd(h1, mix, small["ln2_g"], small["ln2_b"], 1.0, "ln2_fwd")
    gu2, a2 = _mm_swiglu(h2b, w["gu2"], "ffn2_gu")
    f2 = _mm(a2, w["d2"], "nn", F32, "ffn2_down", caps=(512, 1024, FFN_H))
    _, _, xh3, rs3, dh3, loss = _ln_fwd(h2, f2, small["ln3_g"], small["ln3_b"], 0.5, "ln3_fwd", target=target)

    gw, gs = {}, {}
    dr3, dr3h, gs["ln3_g"], gs["ln3_b"] = _ln_bwd(dh3, xh3, rs3, small["ln3_g"], 0.5, "ln3_bwd")
    gw["d2"] = _mm(a2, dr3h, "tn", F32, "ffn2_down_dw")
    da2 = _mm(dr3h, w["d2"], "nt", BF16, "ffn2_down_dx", caps=(1024, 1408, 2048))
    dgu2 = _swiglu_bwd(gu2, da2, "ffn2_act_bwd")
    gw["gu2"] = _mm(h2b, dgu2, "tn", F32, "ffn2_gu_dw", caps=(1024, 1408, 2048), n_slabs=N_CHIPS)
    dh2 = _mm(dgu2, w["gu2"], "nt", F32, "ffn2_gu_dx", add=dr3, add_scale=ALPHA, caps=(1024, 1024, 2816))
    dr2, dr2b, gs["ln2_g"], gs["ln2_b"] = _ln_bwd(dh2, xh2, rs2, small["ln2_g"], 1.0, "ln2_bwd")
    gw["out"] = _mm(mg, dr2b, "tn", F32, "mix_out_dw")
    dmg = _mm(dr2b, w["out"], "nt", F32, "mix_out_dx")
    dproj = lax.empty((x.shape[0], PROJ_PAD), BF16)
    dys, dya, dproj = _merge_bwd(dmg, ys, ya, proj, dproj)
    gw["ao"] = _mm(ot, dya, "nn", F32, "attn_out_dw")
    dot_ = _mm(w["ao"], dya, "nt", BF16, "attn_out_dx")
    dqt, dkr, dproj, gs["attn_sinks"] = _attn_bwd(qt, kr, proj, sinks128, dot_, dproj)
    dproj, dk = _rope_bwd(dqt, dkr, pos, invf, dproj)
    dproj = _place_cols(dk, dproj, K_BLK, "place_dk")
    gw["so"] = _mm(ysn, dys, "tn", F32, "ssd_out_dw")
    dysn = _mm(dys, w["so"], "nt", F32, "ssd_out_dx")
    dproj, dy1, gs["ssd_norm_g"], dd_ch = _gated_norm_bwd(dysn, y_ssd, xc, proj, dexp, small["ssd_norm_g"], dproj)
    gs["d_skip"] = _fold_heads(dd_ch, "d_skip_fold")
    dxs, db, dc, dadt3, dxdx3 = _ssd_bwd(xc, dt3, adt3, hprev, dy1)
    ddt, gs["dt_bias"], gs["a_log"] = _dt_bwd(_token_major(dadt3), _token_major(dxdx3), proj, bias128, alog128)
    dproj = _place_cols(ddt, dproj, DT_BLK, "place_ddt")
    cw, cbias = small["conv_w"], small["conv_b"]
    dproj, dwx, dbx = _conv_bwd(proj, dxs, cw, cbias, 0, SSD_INNER, "conv_bwd_x", dproj, skip=(dy1, dexp))
    dproj, dwb, dbb = _conv_bwd(proj, db, cw, cbias, SSD_INNER, SSD_G * SSD_N, "conv_bwd_b", dproj)
    dproj, dwc, dbc = _conv_bwd(proj, dc, cw, cbias, SSD_INNER + SSD_G * SSD_N, SSD_G * SSD_N, "conv_bwd_c", dproj)
    gs["conv_w"] = jnp.concatenate([dwx[:4], dwb[:4], dwc[:4]], axis=1)
    gs["conv_b"] = jnp.concatenate([dbx, dbb, dbc], axis=1)
    gw["win"] = _mm(h1b, dproj, "tn", F32, "proj_dw", caps=(1024, 896, 2048))
    win = w["win"] if early_grads is None else early_grads[0](gw, w["win"])
    dh1 = _mm(dproj, win, "nt", F32, "proj_dx", add=dr2, add_scale=ALPHA, caps=(1024, 1024, 2432))
    ln1_g = small["ln1_g"]
    if early_grads is not None:
        ln1_g = ln1_g + early_grads[1](dh1)[0:1, 0:1]
    dr1, dr1h, gs["ln1_g"], gs["ln1_b"] = _ln_bwd(dh1, xh1, rs1, ln1_g, 0.5, "ln1_bwd")
    gw["d1"] = _mm(a1, dr1h, "tn", F32, "ffn1_down_dw")
    da1 = _mm(dr1h, w["d1"], "nt", BF16, "ffn1_down_dx", caps=(1024, 1408, 2048))
    dgu1 = _swiglu_bwd(gu1, da1, "ffn1_act_bwd")
    gw["gu1"] = _mm(xb, dgu1, "tn", F32, "ffn1_gu_dw", caps=(1024, 1408, 2048), n_slabs=N_CHIPS)
    grad_x = _mm(dgu1, w["gu1"], "nt", F32, "ffn1_gu_dx", add=dr1, add_scale=ALPHA, caps=(1024, 1024, 2816))
    return loss, grad_x, gw, gs


MESH = pl.DeviceIdType.MESH
ANY = pl.BlockSpec(memory_space=pl.ANY)


def _place():
    x, y, c = lax.axis_index("x"), lax.axis_index("y"), lax.axis_index("c")
    peers = [(1 - x, y), (x, 1 - y), (1 - x, 1 - y)]
    return x, y, c, peers


BIG = [
    ("ffn1_w_gate", D, SHARD_H, "gu1", "col", 0),
    ("ffn1_w_up", D, SHARD_H, "gu1", "col", SHARD_H),
    ("ffn1_w_down", SHARD_H, D, "d1", "row", 0),
    ("w_in", D, SHARD_IN, "win4", "lead", 0),
    ("w_ssd_o", SSD_INNER // N_CHIPS, D, "so", "row", 0),
    ("w_attn_o", D // N_CHIPS, D, "ao", "row", 0),
    ("w_out", D // N_CHIPS, D, "out", "row", 0),
    ("ffn2_w_gate", D, SHARD_H, "gu2", "col", 0),
    ("ffn2_w_up", D, SHARD_H, "gu2", "col", SHARD_H),
    ("ffn2_w_down", SHARD_H, D, "d2", "row", 0),
]
GATHERED = {"gu1": (D, 2 * FFN_H), "d1": (FFN_H, D), "win4": (N_CHIPS, D, SHARD_IN), "so": (SSD_INNER, D),
            "ao": (D, D), "out": (D, D), "gu2": (D, 2 * FFN_H), "d2": (FFN_H, D)}


def _cast_place(srcs, oname, chip_idx):
    rows, cols = srcs[0].shape
    tr = _divtile(rows, 256, 16)
    kind = [b[4] for b in BIG if b[3] == oname][0]

    def body(chip_ref, *refs):
        o_ref = refs[-1]
        for k, s_ref in enumerate(refs[:-1]):
            o_ref[:, k * cols:(k + 1) * cols] = s_ref[...].astype(BF16)

    nt = rows // tr
    if kind == "col":
        o_spec = pl.BlockSpec((tr, len(srcs) * cols), lambda i, chip_ref: (i, chip_ref[0]))
    elif kind == "row":
        o_spec = pl.BlockSpec((tr, cols), lambda i, chip_ref: (chip_ref[0] * nt + i, 0))
    else:
        o_spec = pl.BlockSpec((None, tr, cols), lambda i, chip_ref: (chip_ref[0], i, 0))
    return pl.pallas_call(
        body, name="cast_place_" + oname,
        grid_spec=pltpu.PrefetchScalarGridSpec(
            num_scalar_prefetch=1, grid=(nt,),
            in_specs=[pl.BlockSpec((tr, cols), lambda i, chip_ref: (i, 0))] * len(srcs), out_specs=o_spec),
        out_shape=jax.ShapeDtypeStruct(GATHERED[oname], BF16),
        compiler_params=_params(("parallel",), 32 << 20),
    )(chip_idx, *srcs)


def _slot(outs, entry, j, half):
    _, rows, cols, oname, kind, off = entry
    o = outs[oname]
    hr = rows // 2
    if kind == "col":
        cs = pl.ds(pl.multiple_of(j * (2 * SHARD_H) + off, 128), cols)
        return o.at[pl.ds(pl.multiple_of(half * hr, 16), hr), cs]
    if kind == "row":
        return o.at[pl.ds(pl.multiple_of(j * rows + half * hr, 16), hr), :]
    return o.at[j, pl.ds(pl.multiple_of(half * hr, 16), hr), :]


HBM = pl.BlockSpec(memory_space=pltpu.HBM)
SEM = pl.BlockSpec(memory_space=pltpu.SEMAPHORE)


def _ici_copy(outs, entry, j, c, to, send, recv, k):
    ref = _slot(outs, entry, j, c)
    return pltpu.make_async_remote_copy(src_ref=ref, dst_ref=ref, send_sem=send.at[k], recv_sem=recv.at[k],
                                        device_id=to, device_id_type=MESH)


GATHER_GROUPS = [["gu1"], ["d1"], ["win4"], ["so", "ao", "out", "gu2", "d2"]]


def _gather_ici_start(placed, groups, tag, after):
    names = [k for grp in groups for k in grp]
    bigs = [[b for b in BIG if b[3] in grp] for grp in groups]
    ng = len(groups)
    n_in = len(names) + 1

    def body(*refs):
        sems = refs[n_in:n_in + 2 * ng]
        outs = dict(zip(names, refs[n_in + 2 * ng:n_in + 2 * ng + len(names)]))
        token = refs[-1]
        x, y, c, peers = _place()
        for gi, big in enumerate(bigs):
            for i, entry in enumerate(big):
                for k, (px, py) in enumerate(peers):
                    _ici_copy(outs, entry, 2 * x + y, c, (px, py, c), sems[2 * gi], sems[2 * gi + 1], 3 * i + k).start()
        token[...] = jnp.zeros_like(token)

    sem_shapes = [pltpu.SemaphoreType.DMA((3 * len(big),)) for big in bigs for _ in range(2)]
    res = pl.pallas_call(
        body, name="gather_ici_start_" + tag,
        in_specs=[HBM] * len(names) + [pl.BlockSpec(memory_space=pl.ANY)],
        out_specs=[SEM] * (2 * ng) + [HBM] * len(names) + [pl.BlockSpec(memory_space=pltpu.VMEM)],
        out_shape=sem_shapes + [pltpu.HBM(GATHERED[k], BF16) for k in names] + [jax.ShapeDtypeStruct((8, 128), F32)],
        input_output_aliases={i: i + 2 * ng for i in range(len(names))},
        compiler_params=pltpu.CompilerParams(has_side_effects=pltpu.SideEffectType.DATAFLOW_SIDE_EFFECTING),
    )(*[pltpu.with_memory_space_constraint(placed[k], pltpu.HBM) for k in names], after)
    sems = [(res[2 * gi], res[2 * gi + 1]) for gi in range(ng)]
    return sems, dict(zip(names, res[2 * ng:2 * ng + len(names)])), res[-1]


def _gather_ici_wait(send, recv, arrays, names, after, tag):
    big = [b for b in BIG if b[3] in names]

    def body(*refs):
        outs = dict(zip(names, refs[:len(names)]))
        send_ref, recv_ref = refs[len(names)], refs[len(names) + 1]
        x, y, c, peers = _place()
        for i, entry in enumerate(big):
            for k, (px, py) in enumerate(peers):
                mine = _ici_copy(outs, entry, 2 * x + y, c, (px, py, c), send_ref, recv_ref, 3 * i + k)
                mine.wait_send()
                theirs = _ici_copy(outs, entry, 2 * px + py, c, (px, py, c), send_ref, recv_ref, 3 * i + k)
                theirs.wait_recv()

    res = pl.pallas_call(
        body, name="gather_ici_wait_" + tag,
        in_specs=[HBM] * len(names) + [SEM, SEM, pl.BlockSpec(memory_space=pl.ANY)],
        out_specs=[HBM] * len(names),
        out_shape=[pltpu.HBM(GATHERED[k], BF16) for k in names],
        input_output_aliases={i: i for i in range(len(names))},
        compiler_params=pltpu.CompilerParams(has_side_effects=pltpu.SideEffectType.DATAFLOW_SIDE_EFFECTING),
    )(*[arrays[k] for k in names], send, recv, after)
    return dict(zip(names, res))


def _gather_d2d(arrays, names, tag):
    big = [b for b in BIG if b[3] in names]
    n = len(big)

    def body(*refs):
        outs = dict(zip(names, refs[len(names):2 * len(names)]))
        fsend, frecv = refs[2 * len(names):]
        x, y, c, peers = _place()
        cps = []
        for i, entry in enumerate(big):
            for k, (px, py) in enumerate(peers):
                cp = _ici_copy(outs, entry, 2 * px + py, c, (x, y, 1 - c), fsend, frecv, 3 * i + k)
                cp.start()
                cps.append(cp)
        for i, entry in enumerate(big):
            for k, (px, py) in enumerate(peers):
                _ici_copy(outs, entry, 2 * px + py, 1 - c, (x, y, 1 - c), fsend, frecv, 3 * i + k).wait_recv()
        for cp in cps:
            cp.wait_send()

    res = pl.pallas_call(
        body, name="gather_d2d_" + tag,
        in_specs=[ANY] * len(names), out_specs=[ANY] * len(names),
        out_shape=[jax.ShapeDtypeStruct(GATHERED[k], BF16) for k in names],
        input_output_aliases={i: i for i in range(len(names))},
        scratch_shapes=[pltpu.SemaphoreType.DMA((3 * n,))] * 2,
    )(*[arrays[k] for k in names])
    return dict(zip(names, res))


def _win_pieces():
    pieces = []
    for g0, wd, i0 in SEGS.values():
        for j in range(N_CHIPS):
            lo, hi = max(g0, j * SHARD_IN), min(g0 + wd, (j + 1) * SHARD_IN)
            if lo < hi:
                pieces.append((j, lo - j * SHARD_IN, hi - j * SHARD_IN, i0 + lo - g0))
    return pieces


def _win_to_internal(win4):
    tr = 128

    def body(i_ref, o_ref):
        for j, s0, s1, d0 in _win_pieces():
            o_ref[:, d0:d0 + s1 - s0] = i_ref[j, :, s0:s1]
        o_ref[:, PROJ_W:] = jnp.zeros((tr, PROJ_PAD - PROJ_W), o_ref.dtype)

    return pl.pallas_call(
        body, name="win_to_internal", grid=(D // tr,),
        in_specs=[pl.BlockSpec((N_CHIPS, tr, SHARD_IN), lambda i: (0, i, 0))],
        out_specs=pl.BlockSpec((tr, PROJ_PAD), lambda i: (i, 0)),
        out_shape=jax.ShapeDtypeStruct((D, PROJ_PAD), win4.dtype),
        compiler_params=_params(("parallel",), 40 << 20),
    )(win4)


def _win_from_internal(g):
    tr = 64

    def body(i_ref, o_ref):
        for j, s0, s1, d0 in _win_pieces():
            o_ref[j, :, s0:s1] = i_ref[:, d0:d0 + s1 - s0]

    return pl.pallas_call(
        body, name="win_from_internal", grid=(D // tr,),
        in_specs=[pl.BlockSpec((tr, PROJ_PAD), lambda i: (i, 0))],
        out_specs=pl.BlockSpec((N_CHIPS, tr, SHARD_IN), lambda i: (0, i, 0)),
        out_shape=jax.ShapeDtypeStruct((N_CHIPS, D, SHARD_IN), g.dtype),
        compiler_params=_params(("parallel",), 40 << 20),
    )(g)


def _rs_pair_exchange(grads, tag):
    n = len(grads)

    def body(*refs):
        srcs, dsts = refs[:n], refs[n:2 * n]
        send, recv = refs[2 * n:]
        x, y, c, _ = _place()
        cps = []
        for i in range(n):
            hr = srcs[i].shape[1] // 2
            cp = pltpu.make_async_remote_copy(
                src_ref=srcs[i].at[:, pl.ds(pl.multiple_of((1 - c) * hr, 16), hr), :], dst_ref=dsts[i],
                send_sem=send.at[i], recv_sem=recv.at[i], device_id=(x, y, 1 - c), device_id_type=MESH)
            cp.start()
            cps.append(cp)
        for cp in cps:
            cp.wait()

    return pl.pallas_call(
        body, name="rs_pair_exchange_" + tag, in_specs=[ANY] * n, out_specs=[ANY] * n,
        out_shape=[jax.ShapeDtypeStruct((g.shape[0], g.shape[1] // 2, g.shape[2]), F32) for g in grads],
        scratch_shapes=[pltpu.SemaphoreType.DMA((n,))] * 2,
    )(*grads)


def _pair_copy(src, dst, c, to, send, recv, k):
    hr = src.shape[1] // 2
    return pltpu.make_async_remote_copy(
        src_ref=src.at[:, pl.ds(pl.multiple_of((1 - c) * hr, 16), hr), :], dst_ref=dst,
        send_sem=send.at[k], recv_sem=recv.at[k], device_id=to, device_id_type=MESH)


def _rs_pair_start(grads, carried):
    n = len(grads)

    def body(*refs):
        send, recv = refs[2 * n + 1], refs[2 * n + 2]
        srcs, dsts = refs[2 * n + 3:3 * n + 3], refs[3 * n + 3:4 * n + 3]
        x, y, c, _ = _place()
        for i in range(n):
            _pair_copy(srcs[i], dsts[i], c, (x, y, 1 - c), send, recv, i).start()

    lands = [lax.empty((g.shape[0], g.shape[1] // 2, g.shape[2]), F32) for g in grads]
    res = pl.pallas_call(
        body, name="rs_pair_start",
        in_specs=[HBM] * (2 * n + 1), out_specs=[SEM, SEM] + [HBM] * (2 * n + 1),
        out_shape=[pltpu.SemaphoreType.DMA((n,)), pltpu.SemaphoreType.DMA((n,))]
        + [pltpu.HBM(g.shape, F32) for g in grads] + [pltpu.HBM(l.shape, F32) for l in lands]
        + [pltpu.HBM(carried.shape, carried.dtype)],
        input_output_aliases={i: i + 2 for i in range(2 * n + 1)},
        compiler_params=pltpu.CompilerParams(has_side_effects=pltpu.SideEffectType.DATAFLOW_SIDE_EFFECTING),
    )(*[pltpu.with_memory_space_constraint(a, pltpu.HBM) for a in list(grads) + lands + [carried]])
    return (res[0], res[1], list(res[2:2 + n]), list(res[2 + n:2 + 2 * n])), res[-1]


def _rs_pair_wait(send, recv, grads, lands, after):
    n = len(grads)

    def body(*refs):
        srcs, dsts = refs[:n], refs[n:2 * n]
        send_ref, recv_ref = refs[2 * n], refs[2 * n + 1]
        x, y, c, _ = _place()
        for i in range(n):
            cp = _pair_copy(srcs[i], dsts[i], c, (x, y, 1 - c), send_ref, recv_ref, i)
            cp.wait_send()
            cp.wait_recv()

    res = pl.pallas_call(
        body, name="rs_pair_wait",
        in_specs=[HBM] * (2 * n) + [SEM, SEM, pl.BlockSpec(memory_space=pl.ANY)],
        out_specs=[HBM] * (2 * n),
        out_shape=[pltpu.HBM(g.shape, F32) for g in grads] + [pltpu.HBM(l.shape, F32) for l in lands],
        input_output_aliases={i: i for i in range(2 * n)},
        compiler_params=pltpu.CompilerParams(has_side_effects=pltpu.SideEffectType.DATAFLOW_SIDE_EFFECTING),
    )(*grads, *lands, send, recv, after)
    return list(res[:n]), list(res[n:])


def _half_tile(hr):
    return _divtile(hr, 256, 16) if hr % 256 == 0 else _divtile(hr, 512, 16)


def _rs_pair_sum(g, r, c_idx, name):
    ns, rows, cols = g.shape
    hr = rows // 2
    tr = _half_tile(hr)
    nt = hr // tr

    def body(c_ref, g_ref, r_ref, ob_ref, of_ref):
        s = g_ref[...] + r_ref[...]
        ob_ref[...] = s.astype(BF16)
        of_ref[...] = s

    blk = pl.BlockSpec((None, tr, cols), lambda j, t, c_ref: (j, t, 0))
    return pl.pallas_call(
        body, name=name,
        grid_spec=pltpu.PrefetchScalarGridSpec(
            num_scalar_prefetch=1, grid=(ns, nt),
            in_specs=[pl.BlockSpec((None, tr, cols), lambda j, t, c_ref: (j, c_ref[0] * nt + t, 0)), blk],
            out_specs=[blk, blk]),
        out_shape=[jax.ShapeDtypeStruct((ns, hr, cols), BF16), jax.ShapeDtypeStruct((ns, hr, cols), F32)],
        compiler_params=_params(("parallel", "parallel"), 48 << 20),
    )(c_idx, g, r)


def _rs_chip_start(parts, tag):
    n = len(parts)

    def body(*refs):
        send, recv = refs[2 * n], refs[2 * n + 1]
        srcs, dsts = refs[2 * n + 2:3 * n + 2], refs[3 * n + 2:4 * n + 2]
        token = refs[-1]
        x, y, c, peers = _place()
        for i in range(n):
            for k, (px, py) in enumerate(peers):
                pltpu.make_async_remote_copy(
                    src_ref=srcs[i].at[2 * px + py], dst_ref=dsts[i].at[k],
                    send_sem=send.at[3 * i + k], recv_sem=recv.at[3 * i + k],
                    device_id=(px, py, c), device_id_type=MESH).start()
        token[...] = jnp.zeros_like(token)

    lands = [lax.empty((3,) + p.shape[1:], BF16) for p in parts]
    res = pl.pallas_call(
        body, name="rs_chip_start_" + tag,
        in_specs=[HBM] * (2 * n),
        out_specs=[SEM, SEM] + [HBM] * (2 * n) + [pl.BlockSpec(memory_space=pltpu.VMEM)],
        out_shape=[pltpu.SemaphoreType.DMA((3 * n,)), pltpu.SemaphoreType.DMA((3 * n,))]
        + [pltpu.HBM(p.shape, BF16) for p in parts] + [pltpu.HBM(l.shape, BF16) for l in lands]
        + [jax.ShapeDtypeStruct((8, 128), F32)],
        input_output_aliases={i: i + 2 for i in range(2 * n)},
        compiler_params=pltpu.CompilerParams(has_side_effects=pltpu.SideEffectType.DATAFLOW_SIDE_EFFECTING),
    )(*[pltpu.with_memory_space_constraint(a, pltpu.HBM) for a in list(parts) + lands])
    return res[0], res[1], list(res[2:2 + n]), list(res[2 + n:2 + 2 * n]), res[-1]


def _rs_chip_wait(send, recv, parts, lands, after, tag):
    n = len(parts)

    def body(*refs):
        srcs, dsts = refs[:n], refs[n:2 * n]
        send_ref, recv_ref = refs[2 * n], refs[2 * n + 1]
        x, y, c, peers = _place()
        for i in range(n):
            for k, (px, py) in enumerate(peers):
                cp = pltpu.make_async_remote_copy(
                    src_ref=srcs[i].at[2 * px + py], dst_ref=dsts[i].at[k],
                    send_sem=send_ref.at[3 * i + k], recv_sem=recv_ref.at[3 * i + k],
                    device_id=(px, py, c), device_id_type=MESH)
                cp.wait_send()
                cp.wait_recv()

    res = pl.pallas_call(
        body, name="rs_chip_wait_" + tag,
        in_specs=[HBM] * (2 * n) + [SEM, SEM, pl.BlockSpec(memory_space=pl.ANY)],
        out_specs=[HBM] * (2 * n),
        out_shape=[pltpu.HBM(p.shape, BF16) for p in parts] + [pltpu.HBM(l.shape, BF16) for l in lands],
        input_output_aliases={i: i for i in range(2 * n)},
        compiler_params=pltpu.CompilerParams(has_side_effects=pltpu.SideEffectType.DATAFLOW_SIDE_EFFECTING),
    )(*parts, *lands, send, recv, after)
    return list(res[n:])


def _rs_final_sum(own, got, chip_idx, c_idx, name):
    ns, hr, cols = own.shape
    tr = _half_tile(hr)
    nt = hr // tr

    def body(chip_ref, c_ref, o_ref, g_ref, out_ref):
        s = o_ref[...]
        for k in range(3):
            s = s + g_ref[k].astype(F32)
        out_ref[...] = s

    return pl.pallas_call(
        body, name=name,
        grid_spec=pltpu.PrefetchScalarGridSpec(
            num_scalar_prefetch=2, grid=(nt,),
            in_specs=[pl.BlockSpec((None, tr, cols), lambda t, chip_ref, c_ref: (chip_ref[0], t, 0)),
                      pl.BlockSpec((3, tr, cols), lambda t, chip_ref, c_ref: (0, t, 0))],
            out_specs=pl.BlockSpec((tr, cols), lambda t, chip_ref, c_ref: (c_ref[0] * nt + t, 0))),
        out_shape=jax.ShapeDtypeStruct((2 * hr, cols), F32),
        compiler_params=_params(("parallel",), 48 << 20),
    )(chip_idx, c_idx, own, got)


def _rs_share_halves(fulls, tag):
    n = len(fulls)

    def body(*refs):
        dsts = refs[n:2 * n]
        send, recv = refs[2 * n:]
        x, y, c, _ = _place()
        cps = []
        for i in range(n):
            hr = dsts[i].shape[0] // 2
            rows = dsts[i].at[pl.ds(pl.multiple_of(c * hr, 8), hr), :]
            cp = pltpu.make_async_remote_copy(src_ref=rows, dst_ref=rows, send_sem=send.at[i], recv_sem=recv.at[i],
                                              device_id=(x, y, 1 - c), device_id_type=MESH)
            cp.start()
            cps.append(cp)
        for i in range(n):
            hr = dsts[i].shape[0] // 2
            other = dsts[i].at[pl.ds(pl.multiple_of((1 - c) * hr, 8), hr), :]
            pltpu.make_async_remote_copy(src_ref=other, dst_ref=other, send_sem=send.at[i], recv_sem=recv.at[i],
                                         device_id=(x, y, 1 - c), device_id_type=MESH).wait_recv()
        for cp in cps:
            cp.wait_send()

    return pl.pallas_call(
        body, name="rs_share_halves_" + tag, in_specs=[ANY] * n, out_specs=[ANY] * n,
        out_shape=[jax.ShapeDtypeStruct(f.shape, F32) for f in fulls],
        input_output_aliases={i: i for i in range(n)},
        scratch_shapes=[pltpu.SemaphoreType.DMA((n,))] * 2,
    )(*fulls)


def _all_reduce_small(v):
    rows = v.shape[0]

    def body(v_ref, o_ref, buf, send, recv):
        x, y, c, _ = _place()
        me = 4 * x + 2 * y + c
        buf[me] = v_ref[...]
        cps = []
        for d in range(1, 8):
            px, py, pc = x ^ (d >> 2), y ^ ((d >> 1) & 1), c ^ (d & 1)
            cp = pltpu.make_async_remote_copy(src_ref=v_ref, dst_ref=buf.at[me], send_sem=send.at[d - 1],
                                              recv_sem=recv.at[d - 1], device_id=(px, py, pc), device_id_type=MESH)
            cp.start()
            cps.append(cp)
        for d in range(1, 8):
            px, py, pc = x ^ (d >> 2), y ^ ((d >> 1) & 1), c ^ (d & 1)
            pltpu.make_async_remote_copy(src_ref=v_ref, dst_ref=buf.at[4 * px + 2 * py + pc], send_sem=send.at[d - 1],
                                         recv_sem=recv.at[d - 1], device_id=(px, py, pc),
                                         device_id_type=MESH).wait_recv()
        for cp in cps:
            cp.wait_send()
        acc = buf[0]
        for d in range(1, 8):
            acc = acc + buf[d]
        o_ref[...] = acc

    vm = pl.BlockSpec(memory_space=pltpu.VMEM)
    return pl.pallas_call(
        body, name="all_reduce_small", in_specs=[vm], out_specs=vm,
        out_shape=jax.ShapeDtypeStruct((rows, 128), F32),
        scratch_shapes=[pltpu.VMEM((8, rows, 128), F32), pltpu.SemaphoreType.DMA((7,)), pltpu.SemaphoreType.DMA((7,))],
    )(v)


def _adamw(w, g, m, v, name, g_col_blk=0):
    rows, cols = w.shape
    tr = _divtile(rows, max(8, (2 << 20) // (4 * cols) // 8 * 8), 8)

    def body(w_ref, g_ref, m_ref, v_ref, go_ref, d_ref, mo_ref, vo_ref):
        gv = g_ref[...]
        mn = ADAM_B1 * m_ref[...] + (1.0 - ADAM_B1) * gv
        vn = ADAM_B2 * v_ref[...] + (1.0 - ADAM_B2) * (gv * gv)
        m_hat = mn / (1.0 - ADAM_B1 ** ADAM_STEP)
        v_hat = vn / (1.0 - ADAM_B2 ** ADAM_STEP)
        go_ref[...] = gv
        d_ref[...] = -ADAM_LR * (m_hat / (jnp.sqrt(v_hat) + ADAM_EPS) + ADAM_WD * w_ref[...])
        mo_ref[...] = mn
        vo_ref[...] = vn

    blk = pl.BlockSpec((tr, cols), lambda i: (i, 0))
    return pl.pallas_call(
        body, name=name, grid=(rows // tr,),
        in_specs=[blk, pl.BlockSpec((tr, cols), lambda i: (i, g_col_blk)), blk, blk],
        out_specs=[blk] * 4, out_shape=[jax.ShapeDtypeStruct((rows, cols), F32)] * 4,
        compiler_params=_params(("parallel",), 48 << 20),
    )(w, g, m, v)


SMALL = ["ln1_g", "ln1_b", "conv_w", "conv_b", "dt_bias", "a_log", "d_skip", "ssd_norm_g", "attn_sinks",
         "ln2_g", "ln2_b", "ln3_g", "ln3_b"]


def _pack_rows(vs):
    parts = []
    for v in vs:
        v = v.reshape(-1)
        parts.append(jnp.pad(v, (0, (-v.shape[0]) % 128)))
    flat = jnp.concatenate(parts)
    flat = jnp.pad(flat, (0, (-flat.shape[0]) % 1024))
    return flat.reshape(-1, 128)


def _unpack_rows(packed, shapes):
    flat = packed.reshape(-1)
    out, at = [], 0
    for s in shapes:
        nel = int(np.prod(s))
        out.append(flat[at:at + nel].reshape(s))
        at += nel + (-nel) % 128
    return out


def kernel(x, positions, ffn1_w_gate, ffn1_w_up, ffn1_w_down, ln1_g, ln1_b, w_in, conv_w, conv_b, dt_bias, a_log, d_skip, ssd_norm_g, w_ssd_o, attn_sinks, w_attn_o, w_out, ln2_g, ln2_b, ffn2_w_gate, ffn2_w_up, ffn2_w_down, ln3_g, ln3_b, loss_target, m_ffn1_w_gate, m_ffn1_w_up, m_ffn1_w_down, m_ln1_g, m_ln1_b, m_w_in, m_conv_w, m_conv_b, m_dt_bias, m_a_log, m_d_skip, m_ssd_norm_g, m_w_ssd_o, m_attn_sinks, m_w_attn_o, m_w_out, m_ln2_g, m_ln2_b, m_ffn2_w_gate, m_ffn2_w_up, m_ffn2_w_down, m_ln3_g, m_ln3_b, v_ffn1_w_gate, v_ffn1_w_up, v_ffn1_w_down, v_ln1_g, v_ln1_b, v_w_in, v_conv_w, v_conv_b, v_dt_bias, v_a_log, v_d_skip, v_ssd_norm_g, v_w_ssd_o, v_attn_sinks, v_w_attn_o, v_w_out, v_ln2_g, v_ln2_b, v_ffn2_w_gate, v_ffn2_w_up, v_ffn2_w_down, v_ln3_g, v_ln3_b):
    args = dict(locals())
    wts = {n: args[n][0] for n in [b[0] for b in BIG] + SMALL}
    mom_m = {n: args["m_" + n][0] for n in wts}
    mom_v = {n: args["v_" + n][0] for n in wts}
    t = x.shape[1]
    xi, yi, ci = lax.axis_index("x"), lax.axis_index("y"), lax.axis_index("c")
    chip = 2 * xi + yi

    c_idx = ci.astype(jnp.int32).reshape(1)
    chip_idx = chip.astype(jnp.int32).reshape(1)
    placed = {o: _cast_place([wts[b[0]] for b in BIG if b[3] == o], o, chip_idx) for o in GATHERED}
    g_sems, g_flight = {}, {}

    def fetch(group, after):
        names = GATHER_GROUPS[group]
        send, recv = g_sems[group]
        landed = _gather_ici_wait(send, recv, {k: g_flight[k] for k in names}, names, after, str(group))
        got = _gather_d2d(landed, names, str(group))
        if "win4" in got:
            got["win"] = _win_to_internal(got.pop("win4"))
        return got

    sems, arrays, token = _gather_ici_start(placed, GATHER_GROUPS, "all", wts["ln1_g"])
    g_sems.update(dict(enumerate(sems)))
    g_flight.update(arrays)
    w = fetch(0, token)

    def slabs_of(gw, names):
        view = {"gu1": lambda: gw["gu1"], "gu2": lambda: gw["gu2"],
                "d1": lambda: gw["d1"].reshape(N_CHIPS, SHARD_H, D), "d2": lambda: gw["d2"].reshape(N_CHIPS, SHARD_H, D),
                "win": lambda: _win_from_internal(gw["win"]),
                "so": lambda: gw["so"].reshape(N_CHIPS, SSD_INNER // N_CHIPS, D),
                "ao": lambda: gw["ao"].reshape(N_CHIPS, D // N_CHIPS, D),
                "out": lambda: gw["out"].reshape(N_CHIPS, D // N_CHIPS, D)}
        return [view[nm]() for nm in names]

    early = ["win", "so", "ao", "out", "gu2", "d2"]
    late = ["gu1", "d1"]
    flight = {}

    def early_start(gw, win):
        flight["pair"], win = _rs_pair_start(slabs_of(gw, early), win)
        return win

    def early_mid(dh1):
        slabs, from_sib = _rs_pair_wait(*flight["pair"], dh1)
        pair = [_rs_pair_sum(g, r, c_idx, "rs_pair_sum_" + nm) for g, r, nm in zip(slabs, from_sib, early)]
        send, recv, parts, lands, token = _rs_chip_start([p[0] for p in pair], "early")
        flight.update(send=send, recv=recv, parts=parts, lands=lands, own=[p[1] for p in pair])
        return token

    early_grads = (early_start, early_mid)
    cw_rows = _pack_rows([lax.dynamic_update_slice(jnp.zeros((4, XBC), F32), wts["conv_w"], (0, chip * (XBC // N_CHIPS)))])
    cw_rows = jnp.where(ci == 0, cw_rows, 0.0)
    conv_w_full = _all_reduce_small(cw_rows)[:4 * XBC // 128].reshape(4, XBC)

    small = {n: (wts[n][None, :] if wts[n].ndim == 1 else wts[n]) for n in SMALL}
    small["conv_w"] = conv_w_full
    loss, grad_x, gw, gs = _local_step(x[0], positions[0].astype(F32)[:, None], loss_target[0], w, small,
                                       fetch=fetch, early_grads=early_grads)

    slabs = slabs_of(gw, late)
    from_sib = _rs_pair_exchange(slabs, "late")
    pair = [_rs_pair_sum(g, r, c_idx, "rs_pair_sum_" + nm) for g, r, nm in zip(slabs, from_sib, late)]
    l_send, l_recv, l_parts, l_lands, l_token = _rs_chip_start([p[0] for p in pair], "late")
    got_early = _rs_chip_wait(flight["send"], flight["recv"], flight["parts"], flight["lands"], l_token, "early")

    outs = {}
    big_src = {"ffn1_w_gate": ("gu1", 0), "ffn1_w_up": ("gu1", 1), "ffn1_w_down": ("d1", 0), "w_in": ("win", 0),
               "w_ssd_o": ("so", 0), "w_attn_o": ("ao", 0), "w_out": ("out", 0),
               "ffn2_w_gate": ("gu2", 0), "ffn2_w_up": ("gu2", 1), "ffn2_w_down": ("d2", 0)}

    def finish(names, own, got, tag):
        halves = [_rs_final_sum(o, gt, chip_idx, c_idx, "rs_final_sum_" + nm) for o, gt, nm in zip(own, got, names)]
        full = dict(zip(names, _rs_share_halves(halves, tag)))
        for nm, (src, blk) in big_src.items():
            if src in full:
                outs[nm] = _adamw(wts[nm], full[src], mom_m[nm], mom_v[nm], "adamw_" + nm, g_col_blk=blk)

    finish(early, flight["own"], got_early, "early")
    got_late = _rs_chip_wait(l_send, l_recv, l_parts, l_lands, outs["w_in"][1], "late")
    finish(late, [p[1] for p in pair], got_late, "late")

    gvec = {n: gs[n] for n in SMALL}
    gvec["dt_bias"], gvec["a_log"], gvec["d_skip"] = gs["dt_bias"][:, :64], gs["a_log"][:, :64], gs["d_skip"][:, :64]
    gvec["attn_sinks"] = gs["attn_sinks"][:, :NQ]
    red = _all_reduce_small(_pack_rows([gvec[n] for n in SMALL] + [loss]))
    shapes = [(4, XBC) if n == "conv_w" else wts[n].shape for n in SMALL] + [(1,)]
    red_list = _unpack_rows(red, shapes)
    loss_out = red_list[-1].reshape(())
    gsm = dict(zip(SMALL, red_list[:-1]))
    gsm["conv_w"] = lax.dynamic_slice_in_dim(gsm["conv_w"], chip * (XBC // N_CHIPS), XBC // N_CHIPS, axis=1)
    sm_shapes = [wts[n].shape for n in SMALL]
    res = _adamw(_pack_rows([wts[n] for n in SMALL]), _pack_rows([gsm[n] for n in SMALL]),
                 _pack_rows([mom_m[n] for n in SMALL]), _pack_rows([mom_v[n] for n in SMALL]), "adamw_small")
    res = [_unpack_rows(r, sm_shapes) for r in res]
    for i, nm in enumerate(SMALL):
        outs[nm] = tuple(r[i] for r in res)

    order = ["ffn1_w_gate", "ffn1_w_up", "ffn1_w_down", "ln1_g", "ln1_b", "w_in", "conv_w", "conv_b", "dt_bias", "a_log",
             "d_skip", "ssd_norm_g", "w_ssd_o", "attn_sinks", "w_attn_o", "w_out", "ln2_g", "ln2_b",
             "ffn2_w_gate", "ffn2_w_up", "ffn2_w_down", "ln3_g", "ln3_b"]
    result = [loss_out, grad_x[None]]
    for kind in range(4):
        result += [outs[nm][kind][None] for nm in order]
    return tuple(result)
```

```python
import functools
import math

import numpy as np
import jax
import jax.numpy as jnp
from jax import lax
from jax.experimental import pallas as pl
from jax.experimental.pallas import tpu as pltpu

F32 = jnp.float32
BF16 = jnp.bfloat16
HI = lax.Precision.HIGHEST

D = 2048
FFN_H = 5632
SSD_INNER = 4096
SSD_HEADS = 64
SSD_P = 64
SSD_G = 8
SSD_R = 8
SSD_N = 128
CHUNK = 128
XBC = 6144
NQ = 32
NKV = 4
HD = 64
QW = 2048
KVW = 256
WINDOW = 128
ROPE_THETA = 10000.0
ALPHA = 2.0 ** 0.25
LN_EPS = 1e-5
RMS_EPS = 1e-5
PROJ_W = 16960
N_CHIPS = 4
SHARD_IN = PROJ_W // N_CHIPS
SHARD_H = FFN_H // N_CHIPS

SEGS = {
    "z": (0, 4096, 0),
    "xbc": (4096, 6144, 10240),
    "dt": (10240, 64, 16896),
    "q": (10304, 2048, 8192),
    "k": (12352, 256, 16384),
    "v": (12608, 256, 16640),
    "gs": (12864, 2048, 4096),
    "ga": (14912, 2048, 6144),
}
PROJ_PAD = 17024

ADAM_LR = 0.001
ADAM_B1 = 0.9
ADAM_B2 = 0.999
ADAM_EPS = 1e-08
ADAM_WD = 0.01
ADAM_STEP = 10

VMEM_CAP = 60 * 1024 * 1024


def _params(sem, vmem_bytes):
    return pltpu.CompilerParams(dimension_semantics=sem, vmem_limit_bytes=int(min(VMEM_CAP, vmem_bytes)))


def _divtile(n, cap, q=128):
    best = None
    for d in range(q, min(n, cap) + 1, q):
        if n % d == 0:
            best = d
    return n if best is None else best


def _sigmoid(x):
    return 0.5 * jnp.tanh(0.5 * x) + 0.5


def _mm(a, b, mode, out_dtype, name, add=None, add_scale=1.0, caps=(1024, 1024, 2048), n_slabs=1):
    if mode == "nn":
        (m, k), (k2, n) = a.shape, b.shape
    elif mode == "nt":
        (m, k), (n, k2) = a.shape, b.shape
    else:
        (k, m), (k2, n) = a.shape, b.shape
    assert k == k2, (a.shape, b.shape, mode)
    tm, tn, tk = _divtile(m, caps[0]), _divtile(n // n_slabs, caps[1]), _divtile(k, caps[2])
    nk = k // tk
    per_slab = n // n_slabs // tn
    dims = {"nn": ((1,), (0,)), "nt": ((1,), (1,)), "tn": ((0,), (0,))}[mode]
    has_add = add is not None

    def body(*refs):
        if has_add:
            a_ref, b_ref, add_ref, o_ref = refs[:4]
            scr = refs[4:]
        else:
            a_ref, b_ref, o_ref = refs[:3]
            add_ref = None
            scr = refs[3:]
        part = lax.dot_general(a_ref[...].astype(BF16), b_ref[...].astype(BF16), (dims, ((), ())),
                               preferred_element_type=F32)

        def finish(acc):
            if has_add:
                acc = acc + add_scale * add_ref[...].astype(F32)
            o_ref[...] = acc.astype(o_ref.dtype)

        if nk == 1:
            finish(part)
        else:
            acc_ref = scr[0]
            kk = pl.program_id(2)

            @pl.when(kk == 0)
            def _():
                acc_ref[...] = part

            @pl.when(kk > 0)
            def _():
                acc_ref[...] += part

            @pl.when(kk == nk - 1)
            def _():
                finish(acc_ref[...])

    if mode == "nn":
        a_spec = pl.BlockSpec((tm, tk), lambda i, j, kk: (i, kk))
        b_spec = pl.BlockSpec((tk, tn), lambda i, j, kk: (kk, j))
    elif mode == "nt":
        a_spec = pl.BlockSpec((tm, tk), lambda i, j, kk: (i, kk))
        b_spec = pl.BlockSpec((tn, tk), lambda i, j, kk: (j, kk))
    else:
        a_spec = pl.BlockSpec((tk, tm), lambda i, j, kk: (kk, i))
        b_spec = pl.BlockSpec((tk, tn), lambda i, j, kk: (kk, j))
    o_spec = pl.BlockSpec((tm, tn), lambda i, j, kk: (i, j))
    out_shape = jax.ShapeDtypeStruct((m, n), out_dtype)
    if n_slabs > 1:
        assert not has_add
        o_spec = pl.BlockSpec((None, tm, tn), lambda i, j, kk: (j // per_slab, i, j % per_slab))
        out_shape = jax.ShapeDtypeStruct((n_slabs, m, n // n_slabs), out_dtype)
    in_specs = [a_spec, b_spec] + ([o_spec] if has_add else [])
    args = (a, b) + ((add,) if has_add else ())
    osz = jnp.dtype(out_dtype).itemsize
    vmem = (2 * (tm * tk * a.dtype.itemsize + tk * tn * b.dtype.itemsize) + 2 * tm * tn * osz
            + (2 * tm * tn * add.dtype.itemsize if has_add else 0) + 2 * tm * tn * 4
            + 2 * (tm * tk + tk * tn) + (8 << 20))
    return pl.pallas_call(
        body, name=name, grid=(m // tm, n // tn, nk),
        in_specs=in_specs, out_specs=o_spec, out_shape=out_shape,
        scratch_shapes=[pltpu.VMEM((tm, tn), F32)] if nk > 1 else [],
        compiler_params=_params(("parallel", "parallel", "arbitrary"), vmem),
    )(*args)


def _mm_swiglu(a, b, name):
    m, k = a.shape
    w = SHARD_H
    tm = _divtile(m, 512)

    def body(a_ref, b_ref, gu_ref, act_ref):
        gu = jnp.dot(a_ref[...], b_ref[...], preferred_element_type=F32)
        g = gu[:, :w]
        gu_ref[...] = gu.astype(BF16)
        act_ref[...] = (g * _sigmoid(g) * gu[:, w:]).astype(BF16)

    return pl.pallas_call(
        body, name=name, grid=(N_CHIPS, m // tm),
        in_specs=[pl.BlockSpec((tm, k), lambda j, i: (i, 0)), pl.BlockSpec((k, 2 * w), lambda j, i: (0, j))],
        out_specs=[pl.BlockSpec((tm, 2 * w), lambda j, i: (i, j)), pl.BlockSpec((tm, w), lambda j, i: (i, j))],
        out_shape=[jax.ShapeDtypeStruct((m, 2 * FFN_H), BF16), jax.ShapeDtypeStruct((m, FFN_H), BF16)],
        compiler_params=_params(("parallel", "parallel"), 56 << 20),
    )(a, b)


def _mm_swiglu_bwd(dr, wd, gu, name):
    m, k = dr.shape
    w = SHARD_H
    tm = _divtile(m, 512)

    def body(dr_ref, wd_ref, gu_ref, o_ref):
        d = lax.dot_general(dr_ref[...], wd_ref[...], NT_DIMS, preferred_element_type=F32)
        g = gu_ref[:, :w].astype(F32)
        u = gu_ref[:, w:].astype(F32)
        s = _sigmoid(g)
        o_ref[:, :w] = (d * u * (s * (1.0 + g * (1.0 - s)))).astype(BF16)
        o_ref[:, w:] = (d * (g * s)).astype(BF16)

    return pl.pallas_call(
        body, name=name, grid=(N_CHIPS, m // tm),
        in_specs=[pl.BlockSpec((tm, k), lambda j, i: (i, 0)), pl.BlockSpec((w, k), lambda j, i: (j, 0)),
                  pl.BlockSpec((tm, 2 * w), lambda j, i: (i, j))],
        out_specs=pl.BlockSpec((tm, 2 * w), lambda j, i: (i, j)),
        out_shape=jax.ShapeDtypeStruct((m, 2 * FFN_H), BF16),
        compiler_params=_params(("parallel", "parallel"), 48 << 20),
    )(dr, wd, gu)


def _ln_fwd(base, f, g, b, c, name, target=None):
    t = base.shape[0]
    tt = _divtile(t, 256)
    with_loss = target is not None

    def body(*refs):
        if with_loss:
            base_ref, f_ref, g_ref, b_ref, tg_ref, h_ref, hb_ref, xh_ref, rs_ref, dh_ref, loss_ref = refs
        else:
            base_ref, f_ref, g_ref, b_ref, h_ref, hb_ref, xh_ref, rs_ref = refs
        r = ALPHA * base_ref[...] + c * f_ref[...]
        mu = jnp.mean(r, axis=-1, keepdims=True)
        xc = r - mu
        var = jnp.mean(xc * xc, axis=-1, keepdims=True)
        rstd = lax.rsqrt(var + LN_EPS)
        xh = xc * rstd
        h = xh * g_ref[...] + b_ref[...]
        h_ref[...] = h
        hb_ref[...] = h.astype(BF16)
        xh_ref[...] = xh
        rs_ref[...] = rstd
        if with_loss:
            e = h - tg_ref[...]
            dh_ref[...] = e * (1.0 / D)
            part = 0.5 * jnp.sum(jnp.sum(e * e, axis=-1, keepdims=True) * (1.0 / D), axis=0, keepdims=True)

            @pl.when(pl.program_id(0) == 0)
            def _():
                loss_ref[...] = jnp.zeros_like(loss_ref)

            loss_ref[...] += part

    row = pl.BlockSpec((tt, D), lambda i: (i, 0))
    vec = pl.BlockSpec((1, D), lambda i: (0, 0))
    col = pl.BlockSpec((tt, 1), lambda i: (i, 0))
    in_specs = [row, row, vec, vec] + ([row] if with_loss else [])
    out_specs = [row, row, row, col] + ([row, pl.BlockSpec((1, 1), lambda i: (0, 0))] if with_loss else [])
    out_shape = [jax.ShapeDtypeStruct((t, D), F32), jax.ShapeDtypeStruct((t, D), BF16),
                 jax.ShapeDtypeStruct((t, D), F32), jax.ShapeDtypeStruct((t, 1), F32)]
    if with_loss:
        out_shape += [jax.ShapeDtypeStruct((t, D), F32), jax.ShapeDtypeStruct((1, 1), F32)]
    args = (base, f, g, b) + ((target,) if with_loss else ())
    return pl.pallas_call(
        body, name=name, grid=(t // tt,), in_specs=in_specs, out_specs=out_specs, out_shape=out_shape,
        compiler_params=_params(("arbitrary",) if with_loss else ("parallel",), 48 << 20),
    )(*args)


def _ln_bwd(dy, xh, rstd, g, c, name):
    t = dy.shape[0]
    tt = _divtile(t, 256)

    def body(dy_ref, xh_ref, rs_ref, g_ref, dr_ref, drb_ref, dg_ref, db_ref):
        dyv = dy_ref[...]
        xhv = xh_ref[...]
        dxh = dyv * g_ref[...]
        m1 = jnp.mean(dxh, axis=-1, keepdims=True)
        m2 = jnp.mean(dxh * xhv, axis=-1, keepdims=True)
        dr = rs_ref[...] * (dxh - m1 - xhv * m2)
        dr_ref[...] = dr
        drb_ref[...] = (c * dr).astype(BF16)

        @pl.when(pl.program_id(0) == 0)
        def _():
            dg_ref[...] = jnp.zeros_like(dg_ref)
            db_ref[...] = jnp.zeros_like(db_ref)

        dg_ref[...] += jnp.sum(dyv * xhv, axis=0, keepdims=True)
        db_ref[...] += jnp.sum(dyv, axis=0, keepdims=True)

    row = pl.BlockSpec((tt, D), lambda i: (i, 0))
    vec = pl.BlockSpec((1, D), lambda i: (0, 0))
    col = pl.BlockSpec((tt, 1), lambda i: (i, 0))
    return pl.pallas_call(
        body, name=name, grid=(t // tt,), in_specs=[row, row, col, vec], out_specs=[row, row, vec, vec],
        out_shape=[jax.ShapeDtypeStruct((t, D), F32), jax.ShapeDtypeStruct((t, D), BF16),
                   jax.ShapeDtypeStruct((1, D), F32), jax.ShapeDtypeStruct((1, D), F32)],
        compiler_params=_params(("arbitrary",), 40 << 20),
    )(dy, xh, rstd, g)


DT_BLK = SEGS["dt"][2] // 128


def _dt_prep(proj, bias128, alog128):
    t = proj.shape[0]
    tt = _divtile(t, 1024)

    def body(p_ref, bias_ref, alog_ref, dt_ref, adt_ref):
        dtv = jax.nn.softplus(p_ref[...] + bias_ref[...])
        dt_ref[...] = dtv
        adt_ref[...] = dtv * (-jnp.exp(alog_ref[...]))

    blk = pl.BlockSpec((tt, 128), lambda i: (i, 0))
    vec = pl.BlockSpec((1, 128), lambda i: (0, 0))
    return pl.pallas_call(
        body, name="dt_prep", grid=(t // tt,),
        in_specs=[pl.BlockSpec((tt, 128), lambda i: (i, DT_BLK)), vec, vec], out_specs=[blk, blk],
        out_shape=[jax.ShapeDtypeStruct((t, 128), F32)] * 2,
        compiler_params=_params(("parallel",), 16 << 20),
    )(proj, bias128, alog128)


def _dt_bwd(dadt, dxdx, proj, bias128, alog128):
    t = proj.shape[0]
    tt = _divtile(t, 1024)

    def body(dadt_ref, dxdx_ref, p_ref, bias_ref, alog_ref, o_ref, dbias_ref, dalog_ref):
        pre = p_ref[...] + bias_ref[...]
        dtv = jax.nn.softplus(pre)
        a = -jnp.exp(alog_ref[...])
        ddt = a * dadt_ref[...] + dxdx_ref[...]
        draw = ddt * _sigmoid(pre)
        o_ref[...] = draw.astype(BF16)

        @pl.when(pl.program_id(0) == 0)
        def _():
            dbias_ref[...] = jnp.zeros_like(dbias_ref)
            dalog_ref[...] = jnp.zeros_like(dalog_ref)

        dbias_ref[...] += jnp.sum(draw, axis=0, keepdims=True)
        dalog_ref[...] += jnp.sum(dadt_ref[...] * dtv * a, axis=0, keepdims=True)

    blk = pl.BlockSpec((tt, 128), lambda i: (i, 0))
    vec = pl.BlockSpec((1, 128), lambda i: (0, 0))
    return pl.pallas_call(
        body, name="dt_bwd", grid=(t // tt,),
        in_specs=[blk, blk, pl.BlockSpec((tt, 128), lambda i: (i, DT_BLK)), vec, vec],
        out_specs=[blk, vec, vec],
        out_shape=[jax.ShapeDtypeStruct((t, 128), BF16), jax.ShapeDtypeStruct((1, 128), F32),
                   jax.ShapeDtypeStruct((1, 128), F32)],
        compiler_params=_params(("arbitrary",), 16 << 20),
    )(dadt, dxdx, proj, bias128, alog128)


CONV_CB = 128
CONV_TT = 2048


def _shift_down(cur, prev8, s):
    if s == 0:
        return cur
    rolled = pltpu.roll(cur, s, 0)
    head = pltpu.roll(prev8, s, 0)
    r8 = lax.broadcasted_iota(jnp.int32, (8, 1), 0)
    top = jnp.where(r8 < s, head, rolled[:8])
    return jnp.concatenate([top, rolled[8:]], axis=0)


def _shift_up(cur, next8, s):
    if s == 0:
        return cur
    n = cur.shape[0]
    rolled = pltpu.roll(cur, n - s, 0)
    tail = pltpu.roll(next8, 8 - s, 0)
    r8 = lax.broadcasted_iota(jnp.int32, (8, 1), 0)
    bot = jnp.where(r8 >= 8 - s, tail, rolled[n - 8:])
    return jnp.concatenate([rolled[:n - 8], bot], axis=0)


def _conv_fwd(proj, conv_w, conv_b):
    t = proj.shape[0]
    tt = _divtile(t, CONV_TT)
    base = SEGS["xbc"][2] // CONV_CB
    r8 = tt // 8

    def body(u_ref, up_ref, w_ref, b_ref, o_ref):
        cur = u_ref[...]
        prev8 = jnp.where(pl.program_id(1) > 0, up_ref[...], 0.0)
        acc = b_ref[...] + w_ref[3:4, :] * cur
        for k in range(3):
            acc = acc + w_ref[k:k + 1, :] * _shift_down(cur, prev8, 3 - k)
        o_ref[...] = acc * _sigmoid(acc)

    return pl.pallas_call(
        body, name="conv_fwd", grid=(XBC // CONV_CB, t // tt),
        in_specs=[pl.BlockSpec((tt, CONV_CB), lambda c, i: (i, base + c)),
                  pl.BlockSpec((8, CONV_CB), lambda c, i: (jnp.maximum(i * r8 - 1, 0), base + c)),
                  pl.BlockSpec((4, CONV_CB), lambda c, i: (0, c)),
                  pl.BlockSpec((1, CONV_CB), lambda c, i: (0, c))],
        out_specs=pl.BlockSpec((tt, CONV_CB), lambda c, i: (i, c)),
        out_shape=jax.ShapeDtypeStruct((t, XBC), F32),
        compiler_params=_params(("parallel", "parallel"), 24 << 20),
    )(proj, proj, conv_w, conv_b)


def _conv_bwd(proj, dout, conv_w, conv_b, col0, width, name, dproj, skip=None):
    t = proj.shape[0]
    tt = _divtile(t, CONV_TT)
    nt = t // tt
    base = SEGS["xbc"][2] // CONV_CB + col0 // CONV_CB
    wb = col0 // CONV_CB
    r8 = tt // 8
    has_skip = skip is not None

    def body(*refs):
        if has_skip:
            u_ref, up_ref, d_ref, w_ref, b_ref, _, sk_ref, skw_ref, du_ref, dw_ref, db_ref, nx_ref = refs
        else:
            u_ref, up_ref, d_ref, w_ref, b_ref, _, du_ref, dw_ref, db_ref, nx_ref = refs
        i = pl.program_id(1)
        cur = u_ref[...]
        prev8 = jnp.where(i < nt - 1, up_ref[...], 0.0)
        sh = [_shift_down(cur, prev8, 3 - k) for k in range(3)] + [cur]
        pre = b_ref[...]
        for k in range(4):
            pre = pre + w_ref[k:k + 1, :] * sh[k]
        sg = _sigmoid(pre)
        dout_v = d_ref[...]
        if has_skip:
            dout_v = dout_v + sk_ref[...] * skw_ref[...]
        dpre = dout_v * (sg * (1.0 + pre * (1.0 - sg)))

        @pl.when(i == 0)
        def _():
            nx_ref[...] = jnp.zeros_like(nx_ref)
            dw_ref[...] = jnp.zeros_like(dw_ref)
            db_ref[...] = jnp.zeros_like(db_ref)

        next8 = nx_ref[...]
        du = w_ref[3:4, :] * dpre
        for s in range(1, 4):
            du = du + w_ref[3 - s:4 - s, :] * _shift_up(dpre, next8, s)
        du_ref[...] = du.astype(BF16)
        nx_ref[...] = dpre[:8]
        rows = [jnp.sum(dpre * sh[k], axis=0, keepdims=True) for k in range(4)]
        dw_ref[...] += jnp.concatenate(rows + [jnp.zeros((4, CONV_CB), F32)], axis=0)
        db_ref[...] += jnp.sum(dpre, axis=0, keepdims=True)

    rev = lambda c, i: (nt - 1 - i, c)
    p_in, p_out, p_shape = _dproj_piece(t, tt, CONV_CB, lambda c, i: (nt - 1 - i, base + c))
    in_specs = [pl.BlockSpec((tt, CONV_CB), lambda c, i: (nt - 1 - i, base + c)),
                pl.BlockSpec((8, CONV_CB), lambda c, i: (jnp.maximum((nt - 1 - i) * r8 - 1, 0), base + c)),
                pl.BlockSpec((tt, CONV_CB), rev),
                pl.BlockSpec((4, CONV_CB), lambda c, i: (0, wb + c)),
                pl.BlockSpec((1, CONV_CB), lambda c, i: (0, wb + c)), p_in]
    args = [proj, proj, dout, conv_w, conv_b, dproj]
    if has_skip:
        in_specs += [pl.BlockSpec((tt, CONV_CB), rev), pl.BlockSpec((1, CONV_CB), lambda c, i: (0, c))]
        args += [skip[0], skip[1]]
    return pl.pallas_call(
        body, name=name, grid=(width // CONV_CB, nt),
        in_specs=in_specs,
        out_specs=[p_out, pl.BlockSpec((8, CONV_CB), lambda c, i: (0, c)),
                   pl.BlockSpec((1, CONV_CB), lambda c, i: (0, c))],
        out_shape=[p_shape, jax.ShapeDtypeStruct((8, width), F32), jax.ShapeDtypeStruct((1, width), F32)],
        input_output_aliases={5: 0},
        scratch_shapes=[pltpu.VMEM((8, CONV_CB), F32)],
        compiler_params=_params(("parallel", "arbitrary"), 32 << 20),
    )(*args)


GW = SSD_R * SSD_P


def _expand8(v, passes=2):
    r = v.shape[0]
    if r < 8:
        v = jnp.broadcast_to(v, (8, SSD_R))
    ri = lax.broadcasted_iota(jnp.int32, (SSD_R, GW), 0)
    ci = lax.broadcasted_iota(jnp.int32, (SSD_R, GW), 1)
    spread = jnp.where((ci >= ri * SSD_P) & (ci < (ri + 1) * SSD_P), 1.0, 0.0)
    return _dot01(v, spread, passes)[:r]


def _head_pair_split(tile):
    first = lax.broadcasted_iota(jnp.int32, (1, 2 * SSD_P), 1) < SSD_P
    return jnp.where(first, tile, 0.0), jnp.where(first, 0.0, tile)


def _sel(rows, group):
    ri = lax.broadcasted_iota(jnp.int32, (rows, rows // group), 0)
    ci = lax.broadcasted_iota(jnp.int32, (rows, rows // group), 1)
    lo = ci * group
    return jnp.where((ri >= lo) & (ri < lo + group), 1.0, 0.0).astype(F32)


def _dot01(lhs, rhs, passes, split_lhs=True, dims=((1,), (0,))):
    val, m01 = (lhs, rhs) if split_lhs else (rhs, lhs)
    m01 = m01.astype(BF16)
    out = None
    for p in range(passes):
        piece = val.astype(BF16)
        ops = (piece, m01) if split_lhs else (m01, piece)
        d = lax.dot_general(ops[0], ops[1], (dims, ((), ())), preferred_element_type=F32)
        out = d if out is None else out + d
        if p + 1 < passes:
            val = val - piece.astype(F32)
    return out


def _ssd_chunk_terms(adt):
    li = lax.broadcasted_iota(jnp.int32, (CHUNK, CHUNK), 0)
    si = lax.broadcasted_iota(jnp.int32, (CHUNK, CHUNK), 1)
    causal = li >= si
    a_cs = _dot01(jnp.where(causal, 1.0, 0.0), adt, 3, split_lhs=False)
    a_cs_t = _dot01(adt, jnp.where(li <= si, 1.0, 0.0), 3, dims=((0,), (0,)))
    return a_cs, a_cs_t, causal


def _ssd_fwd(xc, dt3, adt3):
    t = xc.shape[0]
    nc = t // CHUNK

    gs = 2

    def body(xs_ref, b_ref, c_ref, dt_ref, adt_ref, y_ref, hp_ref, h_ref):
        @pl.when(pl.program_id(1) == 0)
        def _():
            h_ref[...] = jnp.zeros_like(h_ref)

        for gg in range(gs):
            a_cs, a_cs_t, causal = _ssd_chunk_terms(adt_ref[gg])
            a_last = a_cs[CHUNK - 1:CHUNK, :]
            h = h_ref[gg]
            hp_ref[gg, 0] = h
            xd = xs_ref[:, GW * gg:GW * (gg + 1)] * _expand8(dt_ref[gg])
            bb = b_ref[:, SSD_N * gg:SSD_N * (gg + 1)].astype(BF16)
            cbf = c_ref[:, SSD_N * gg:SSD_N * (gg + 1)].astype(BF16)
            cb = lax.dot_general(cbf, bb, (((1,), (1,)), ((), ())), preferred_element_type=F32)
            yoff = jnp.dot(cbf, h.astype(BF16), preferred_element_type=F32) * _expand8(jnp.exp(a_cs))
            for q in range(SSD_R // 2):
                lmats = []
                for r in (2 * q, 2 * q + 1):
                    seg = jnp.exp(jnp.where(causal, a_cs[:, r:r + 1] - a_cs_t[r:r + 1, :], -jnp.inf))
                    lmats.append((cb * seg).astype(BF16))
                tile = slice(2 * SSD_P * q, 2 * SSD_P * (q + 1))
                xa, xb = _head_pair_split(xd[:, tile])
                y_ref[:, GW * gg + 2 * SSD_P * q:GW * gg + 2 * SSD_P * (q + 1)] = (
                    jnp.dot(jnp.concatenate(lmats, axis=1), jnp.concatenate([xa, xb], axis=0).astype(BF16),
                            preferred_element_type=F32) + yoff[:, tile])
            xdd = (xd * _expand8(jnp.exp(a_last - a_cs))).astype(BF16)
            h_ref[gg] = _expand8(jnp.exp(a_last), 3) * h + lax.dot_general(
                bb, xdd, (((0,), (0,)), ((), ())), preferred_element_type=F32)

    nb = SSD_INNER // (gs * SSD_N)
    return pl.pallas_call(
        body, name="ssd_fwd", grid=(SSD_G // gs, nc),
        in_specs=[pl.BlockSpec((CHUNK, gs * GW), lambda g, c: (c, g)),
                  pl.BlockSpec((CHUNK, gs * SSD_N), lambda g, c: (c, nb + g)),
                  pl.BlockSpec((CHUNK, gs * SSD_N), lambda g, c: (c, nb + SSD_G // gs + g)),
                  pl.BlockSpec((gs, CHUNK, SSD_R), lambda g, c: (g, c, 0)),
                  pl.BlockSpec((gs, CHUNK, SSD_R), lambda g, c: (g, c, 0))],
        out_specs=[pl.BlockSpec((CHUNK, gs * GW), lambda g, c: (c, g)),
                   pl.BlockSpec((gs, 1, SSD_N, GW), lambda g, c: (g, c, 0, 0))],
        out_shape=[jax.ShapeDtypeStruct((t, SSD_INNER), F32), jax.ShapeDtypeStruct((SSD_G, nc, SSD_N, GW), F32)],
        scratch_shapes=[pltpu.VMEM((gs, SSD_N, GW), F32)],
        compiler_params=_params(("parallel", "arbitrary"), 32 << 20),
    )(xc, xc, xc, dt3, adt3)


def _ssd_bwd(xc, dt3, adt3, hprev, dy):
    t = xc.shape[0]
    nc = t // CHUNK

    gs = 2

    def body(xs_ref, b_ref, c_ref, dt_ref, adt_ref, hp_ref, dy_ref,
             dx_ref, db_ref, dc_ref, dadt_ref, dxdx_ref, dh_ref):
        @pl.when(pl.program_id(1) == 0)
        def _():
            dh_ref[...] = jnp.zeros_like(dh_ref)

        for gg in range(gs):
            wide = slice(GW * gg, GW * (gg + 1))
            narrow = slice(SSD_N * gg, SSD_N * (gg + 1))
            dx, db, dc, dadt, dxdx, dh_new = group_bwd(
                xs_ref[:, wide], b_ref[:, narrow], c_ref[:, narrow], dt_ref[gg], adt_ref[gg], hp_ref[gg, 0],
                dy_ref[:, wide], dh_ref[gg])
            dx_ref[:, wide] = dx
            db_ref[:, narrow] = db
            dc_ref[:, narrow] = dc
            dadt_ref[gg] = dadt
            dxdx_ref[gg] = dxdx
            dh_ref[gg] = dh_new

    def group_bwd(xs, b, c, dt, adt, hp, dyv, dh):
        a_cs, a_cs_t, causal = _ssd_chunk_terms(adt)
        a_last = a_cs[CHUNK - 1:CHUNK, :]
        e_last = jnp.exp(a_last)
        ex = _expand8(jnp.exp(a_cs))
        dtex = _expand8(jnp.exp(a_last - a_cs))
        dtx = _expand8(dt)
        sel = _sel(GW, SSD_P)
        seg8 = lambda v: _dot01(v, sel, 2)

        xd = xs * dtx
        xdd = xd * dtex
        bb = b.astype(BF16)
        cbf = c.astype(BF16)
        hpb = hp.astype(BF16)
        dhb = dh.astype(BF16)
        xdb = xd.astype(BF16)
        cb = lax.dot_general(cbf, bb, (((1,), (1,)), ((), ())), preferred_element_type=F32)
        dye = (dyv * ex).astype(BF16)
        yoff = jnp.dot(cbf, hpb, preferred_element_type=F32) * ex
        dc = lax.dot_general(dye, hpb, (((1,), (1,)), ((), ())), preferred_element_type=F32)
        bdh = jnp.dot(bb, dhb, preferred_element_type=F32)
        db = lax.dot_general(xdd.astype(BF16), dhb, (((1,), (1,)), ((), ())), preferred_element_type=F32)
        dxd_state = bdh * dtex
        q_terms = xdd * bdh
        d_a = seg8(dyv * yoff - q_terms)
        d_a_last = seg8(jnp.sum(q_terms, axis=0, keepdims=True)
                        + _expand8(e_last, 3) * jnp.sum(hp * dh, axis=0, keepdims=True))
        dh_new = (lax.dot_general(cbf, dye, (((0,), (0,)), ((), ())), preferred_element_type=F32)
                  + _expand8(e_last, 3) * dh)
        dcb = jnp.zeros((CHUNK, CHUNK), F32)
        w_all = []
        dxd_parts = []
        for q2 in range(SSD_R // 2):
            tile = slice(2 * SSD_P * q2, 2 * SSD_P * (q2 + 1))
            dy_pair = [part.astype(BF16) for part in _head_pair_split(dyv[:, tile])]
            lmats = []
            for k, r in enumerate((2 * q2, 2 * q2 + 1)):
                seg = jnp.exp(jnp.where(causal, a_cs[:, r:r + 1] - a_cs_t[r:r + 1, :], -jnp.inf))
                lmat = cb * seg
                dm = lax.dot_general(dy_pair[k], xdb[:, tile], (((1,), (1,)), ((), ())), preferred_element_type=F32)
                dcb = dcb + dm * seg
                w_all.append(dm * lmat)
                lmats.append(lmat.astype(BF16))
            dxd_parts.append(lax.dot_general(jnp.concatenate(lmats, axis=0), jnp.concatenate(dy_pair, axis=0),
                                             (((0,), (0,)), ((), ())), preferred_element_type=F32))
        row_sums = _dot01(jnp.concatenate(w_all, axis=1), _sel(SSD_R * CHUNK, CHUNK), 2)
        cs_rows = jnp.concatenate([jnp.sum(wr, axis=0, keepdims=True) for wr in w_all], axis=0)
        col_sums = _dot01(cs_rows, _sel(SSD_R, 1), 3, dims=((0,), (0,)))
        d_a = d_a + row_sums - col_sums
        li = lax.broadcasted_iota(jnp.int32, (CHUNK, SSD_R), 0)
        d_a = d_a + jnp.where(li == CHUNK - 1, d_a_last, 0.0)
        l2 = lax.broadcasted_iota(jnp.int32, (CHUNK, CHUNK), 0)
        s2 = lax.broadcasted_iota(jnp.int32, (CHUNK, CHUNK), 1)
        dadt = _dot01(jnp.where(s2 >= l2, 1.0, 0.0), d_a, 3, split_lhs=False)
        dxd = dxd_state + jnp.concatenate(dxd_parts, axis=1)
        dcbb = dcb.astype(BF16)
        db = db + lax.dot_general(dcbb, cbf, (((0,), (0,)), ((), ())), preferred_element_type=F32)
        dc = dc + jnp.dot(dcbb, bb, preferred_element_type=F32)
        return dxd * dtx, db, dc, dadt, seg8(dxd * xs), dh_new

    nb = SSD_INNER // (gs * SSD_N)
    rc = lambda g, c: (nc - 1 - c, g)
    r3 = lambda g, c: (g, nc - 1 - c, 0)
    return pl.pallas_call(
        body, name="ssd_bwd", grid=(SSD_G // gs, nc),
        in_specs=[pl.BlockSpec((CHUNK, gs * GW), rc),
                  pl.BlockSpec((CHUNK, gs * SSD_N), lambda g, c: (nc - 1 - c, nb + g)),
                  pl.BlockSpec((CHUNK, gs * SSD_N), lambda g, c: (nc - 1 - c, nb + SSD_G // gs + g)),
                  pl.BlockSpec((gs, CHUNK, SSD_R), r3),
                  pl.BlockSpec((gs, CHUNK, SSD_R), r3),
                  pl.BlockSpec((gs, 1, SSD_N, GW), lambda g, c: (g, nc - 1 - c, 0, 0)),
                  pl.BlockSpec((CHUNK, gs * GW), rc)],
        out_specs=[pl.BlockSpec((CHUNK, gs * GW), rc),
                   pl.BlockSpec((CHUNK, gs * SSD_N), rc),
                   pl.BlockSpec((CHUNK, gs * SSD_N), rc),
                   pl.BlockSpec((gs, CHUNK, SSD_R), r3),
                   pl.BlockSpec((gs, CHUNK, SSD_R), r3)],
        out_shape=[jax.ShapeDtypeStruct((t, SSD_INNER), F32),
                   jax.ShapeDtypeStruct((t, SSD_G * SSD_N), F32),
                   jax.ShapeDtypeStruct((t, SSD_G * SSD_N), F32),
                   jax.ShapeDtypeStruct((SSD_G, t, SSD_R), F32),
                   jax.ShapeDtypeStruct((SSD_G, t, SSD_R), F32)],
        scratch_shapes=[pltpu.VMEM((gs, SSD_N, GW), F32)],
        compiler_params=_params(("parallel", "arbitrary"), 48 << 20),
    )(xc, xc, xc, dt3, adt3, hprev, dy)


def _gated_norm_fwd(y, xc, proj, dexp, ng):
    t = y.shape[0]
    tt = _divtile(t, 256)

    def body(y_ref, x_ref, z_ref, d_ref, g_ref, o_ref):
        z = z_ref[...]
        y2 = (y_ref[...] + d_ref[...] * x_ref[...]) * (z * _sigmoid(z))
        for gi in range(SSD_G):
            sl = slice(GW * gi, GW * (gi + 1))
            seg = y2[:, sl]
            rinv = lax.rsqrt(jnp.mean(seg * seg, axis=-1, keepdims=True) + RMS_EPS)
            o_ref[:, sl] = (seg * rinv * g_ref[:, sl]).astype(BF16)

    row = pl.BlockSpec((tt, SSD_INNER), lambda i: (i, 0))
    vec = pl.BlockSpec((1, SSD_INNER), lambda i: (0, 0))
    return pl.pallas_call(
        body, name="gated_norm_fwd", grid=(t // tt,), in_specs=[row, row, row, vec, vec], out_specs=row,
        out_shape=jax.ShapeDtypeStruct((t, SSD_INNER), BF16),
        compiler_params=_params(("parallel",), 48 << 20),
    )(y, xc, proj, dexp, ng)


def _gated_norm_bwd(dout, y, xc, proj, dexp, ng, dproj):
    t = y.shape[0]
    tt = _divtile(t, 128)

    def body(do_ref, y_ref, x_ref, z_ref, d_ref, g_ref, _, dz_ref, dy_ref, dg_ref, dd_ref):
        z = z_ref[...]
        sg = _sigmoid(z)
        sz = z * sg
        xs = x_ref[...]
        y1 = y_ref[...] + d_ref[...] * xs
        y2 = y1 * sz
        dov = do_ref[...]

        @pl.when(pl.program_id(0) == 0)
        def _():
            dg_ref[...] = jnp.zeros_like(dg_ref)
            dd_ref[...] = jnp.zeros_like(dd_ref)

        for gi in range(SSD_G):
            sl = slice(GW * gi, GW * (gi + 1))
            seg = y2[:, sl]
            rinv = lax.rsqrt(jnp.mean(seg * seg, axis=-1, keepdims=True) + RMS_EPS)
            yn = seg * rinv
            dsl = dov[:, sl]
            dg_ref[:, sl] += jnp.sum(dsl * yn, axis=0, keepdims=True)
            dyn = dsl * g_ref[:, sl]
            dy2 = rinv * (dyn - yn * jnp.mean(dyn * yn, axis=-1, keepdims=True))
            dz_ref[:, sl] = (dy2 * y1[:, sl] * (sg[:, sl] * (1.0 + z[:, sl] * (1.0 - sg[:, sl])))).astype(BF16)
            dy1 = dy2 * sz[:, sl]
            dy_ref[:, sl] = dy1
            dd_ref[:, sl] += jnp.sum(dy1 * xs[:, sl], axis=0, keepdims=True)

    row = pl.BlockSpec((tt, SSD_INNER), lambda i: (i, 0))
    vec = pl.BlockSpec((1, SSD_INNER), lambda i: (0, 0))
    p_in, p_out, p_shape = _dproj_piece(t, tt, SSD_INNER, lambda i: (i, 0))
    return pl.pallas_call(
        body, name="gated_norm_bwd", grid=(t // tt,), in_specs=[row, row, row, row, vec, vec, p_in],
        out_specs=[p_out, row, vec, vec],
        out_shape=[p_shape, jax.ShapeDtypeStruct((t, SSD_INNER), F32),
                   jax.ShapeDtypeStruct((1, SSD_INNER), F32), jax.ShapeDtypeStruct((1, SSD_INNER), F32)],
        input_output_aliases={6: 0},
        compiler_params=_params(("arbitrary",), 48 << 20),
    )(dout, y, xc, proj, dexp, ng, dproj)


def _fold_heads(v, name):
    def body(v_ref, o_ref):
        ri = lax.broadcasted_iota(jnp.int32, (SSD_INNER, 128), 0)
        ci = lax.broadcasted_iota(jnp.int32, (SSD_INNER, 128), 1)
        fold = jnp.where((ri >= ci * SSD_P) & (ri < (ci + 1) * SSD_P), 1.0, 0.0).astype(F32)
        o_ref[...] = jnp.dot(v_ref[...], fold, preferred_element_type=F32, precision=HI)

    return pl.pallas_call(body, name=name, out_shape=jax.ShapeDtypeStruct((1, 128), F32))(v)


Q_BLK = SEGS["q"][2] // QW
K_BLK = SEGS["k"][2] // KVW
V_BLK = SEGS["v"][2] // KVW


def _rope_tables(pos_ref, invf_ref, width):
    ang = pos_ref[...] * invf_ref[...]
    lane = lax.broadcasted_iota(jnp.int32, (1, 128), 1)
    sign = jnp.where((lane % HD) < (HD // 2), -1.0, 1.0)
    cos = jnp.tile(jnp.cos(ang), (1, width // 128))
    sin = jnp.tile(sign * jnp.sin(ang), (1, width // 128))
    first = (lax.broadcasted_iota(jnp.int32, (1, width), 1) % HD) < (HD // 2)
    return cos, sin, first


def _rot_half(u, first):
    w = u.shape[1]
    return jnp.where(first, pltpu.roll(u, w - HD // 2, 1), pltpu.roll(u, HD // 2, 1))


def _rope_fwd(proj, pos, invf):
    t = proj.shape[0]
    tt = _divtile(t, 512)

    def body(q_ref, k_ref, pos_ref, invf_ref, qo_ref, ko_ref):
        cos, sin, first = _rope_tables(pos_ref, invf_ref, QW)
        q = q_ref[...]
        qr = q * cos + _rot_half(q, first) * sin
        for p in range(QW // 128):
            qo_ref[128 * p:128 * (p + 1), :] = qr[:, 128 * p:128 * (p + 1)].T.astype(BF16)
        k = k_ref[...]
        ko_ref[...] = (k * cos[:, :KVW] + _rot_half(k, first[:, :KVW]) * sin[:, :KVW]).astype(BF16)

    return pl.pallas_call(
        body, name="rope_fwd", grid=(t // tt,),
        in_specs=[pl.BlockSpec((tt, QW), lambda i: (i, Q_BLK)), pl.BlockSpec((tt, KVW), lambda i: (i, K_BLK)),
                  pl.BlockSpec((tt, 1), lambda i: (i, 0)), pl.BlockSpec((1, 128), lambda i: (0, 0))],
        out_specs=[pl.BlockSpec((QW, tt), lambda i: (0, i)), pl.BlockSpec((tt, KVW), lambda i: (i, 0))],
        out_shape=[jax.ShapeDtypeStruct((QW, t), BF16), jax.ShapeDtypeStruct((t, KVW), BF16)],
        compiler_params=_params(("parallel",), 40 << 20),
    )(proj, proj, pos, invf)


def _rope_bwd(dqt, dk, pos, invf, dproj):
    t = dk.shape[0]
    tt = _divtile(t, 512)

    def body(dq_ref, dk_ref, pos_ref, invf_ref, _, qo_ref, ko_ref):
        cos, sin, first = _rope_tables(pos_ref, invf_ref, QW)
        q = jnp.concatenate([dq_ref[128 * p:128 * (p + 1), :].T for p in range(QW // 128)], axis=1)
        qo_ref[...] = (q * cos + _rot_half(q * sin, first)).astype(BF16)
        k = dk_ref[...]
        ko_ref[...] = (k * cos[:, :KVW] + _rot_half(k * sin[:, :KVW], first[:, :KVW])).astype(BF16)

    p_in, p_out, p_shape = _dproj_piece(t, tt, QW, lambda i: (i, Q_BLK))
    return pl.pallas_call(
        body, name="rope_bwd", grid=(t // tt,),
        in_specs=[pl.BlockSpec((QW, tt), lambda i: (0, i)), pl.BlockSpec((tt, KVW), lambda i: (i, 0)),
                  pl.BlockSpec((tt, 1), lambda i: (i, 0)), pl.BlockSpec((1, 128), lambda i: (0, 0)), p_in],
        out_specs=[p_out, pl.BlockSpec((tt, KVW), lambda i: (i, 0))],
        out_shape=[p_shape, jax.ShapeDtypeStruct((t, KVW), BF16)],
        input_output_aliases={4: 0},
        compiler_params=_params(("parallel",), 40 << 20),
    )(dqt, dk, pos, invf, dproj)


def _place_cols(piece, dproj, col_blk, name):
    t, w = piece.shape
    tt = _divtile(t, 1024)

    def body(p_ref, _, o_ref):
        o_ref[...] = p_ref[...]

    p_in, p_out, p_shape = _dproj_piece(t, tt, w, lambda i: (i, col_blk))
    return pl.pallas_call(
        body, name=name, grid=(t // tt,),
        in_specs=[pl.BlockSpec((tt, w), lambda i: (i, 0)), p_in], out_specs=p_out, out_shape=p_shape,
        input_output_aliases={1: 0},
        compiler_params=_params(("parallel",), 16 << 20),
    )(piece, dproj)


GQ = NQ // NKV
NT_DIMS = (((1,), (1,)), ((), ()))
TN_DIMS = (((0,), (0,)), ((), ()))


def _attn_heads(ref, j, dtype=None):
    out = jnp.concatenate([ref[HD * h:HD * (h + 1), :] for h in range(j * GQ, (j + 1) * GQ)], axis=1)
    return out if dtype is None else out.astype(dtype)


def _attn_sink_row(s_ref, j):
    return jnp.concatenate([jnp.broadcast_to(s_ref[:, h:h + 1], (1, WINDOW)) for h in range(j * GQ, (j + 1) * GQ)],
                           axis=1)


def _attn_mask(n):
    kr = lax.broadcasted_iota(jnp.int32, (2 * WINDOW, GQ * WINDOW), 0)
    qi = lax.broadcasted_iota(jnp.int32, (2 * WINDOW, GQ * WINDOW), 1) % WINDOW
    return (kr > qi) & (kr <= qi + WINDOW) & ((n > 0) | (kr >= WINDOW))


def _attn_probs(qgt, kk, sink, mask):
    s = jnp.where(mask, jnp.dot(kk, qgt, preferred_element_type=F32) * (HD ** -0.5), -jnp.inf)
    m = jnp.maximum(jnp.max(s, axis=0, keepdims=True), sink)
    p = jnp.exp(s - m)
    ps = jnp.exp(sink - m)
    inv = 1.0 / (jnp.sum(p, axis=0, keepdims=True) + ps)
    return p * inv, ps * inv


def _attn_fwd(qt, kr, proj, sinks):
    t = kr.shape[0]
    nb = t // WINDOW

    def body(q_ref, kc_ref, kp_ref, vc_ref, vp_ref, s_ref, o_ref):
        mask = _attn_mask(pl.program_id(0))
        for j in range(NKV):
            ks = slice(HD * j, HD * (j + 1))
            kk = jnp.concatenate([kp_ref[:, ks], kc_ref[:, ks]], axis=0)
            vv = jnp.concatenate([vp_ref[:, ks], vc_ref[:, ks]], axis=0).astype(BF16)
            pn, _ = _attn_probs(_attn_heads(q_ref, j), kk, _attn_sink_row(s_ref, j), mask)
            ot = lax.dot_general(vv, pn.astype(BF16), TN_DIMS, preferred_element_type=F32).astype(BF16)
            for g in range(GQ):
                h = j * GQ + g
                o_ref[HD * h:HD * (h + 1), :] = ot[:, WINDOW * g:WINDOW * (g + 1)]

    prev = lambda n: (jnp.maximum(n - 1, 0), 0)
    return pl.pallas_call(
        body, name="attn_fwd", grid=(nb,),
        in_specs=[pl.BlockSpec((QW, WINDOW), lambda n: (0, n)),
                  pl.BlockSpec((WINDOW, KVW), lambda n: (n, 0)), pl.BlockSpec((WINDOW, KVW), prev),
                  pl.BlockSpec((WINDOW, KVW), lambda n: (n, V_BLK)),
                  pl.BlockSpec((WINDOW, KVW), lambda n: (jnp.maximum(n - 1, 0), V_BLK)),
                  pl.BlockSpec((1, 128), lambda n: (0, 0))],
        out_specs=pl.BlockSpec((QW, WINDOW), lambda n: (0, n)),
        out_shape=jax.ShapeDtypeStruct((QW, t), BF16),
        compiler_params=_params(("parallel",), 24 << 20),
    )(qt, kr, kr, proj, proj, sinks)


def _attn_bwd(qt, kr, proj, sinks, dot_, dproj):
    t = kr.shape[0]
    nb = t // WINDOW

    def body(q_ref, kc_ref, kp_ref, vc_ref, vp_ref, s_ref, do_ref, _,
             dq_ref, dk_ref, dv_ref, ds_ref, dkc_ref, dvc_ref):
        i = pl.program_id(0)
        mask = _attn_mask(nb - 1 - i)

        @pl.when(i == 0)
        def _():
            dkc_ref[...] = jnp.zeros_like(dkc_ref)
            dvc_ref[...] = jnp.zeros_like(dvc_ref)
            ds_ref[...] = jnp.zeros_like(ds_ref)

        lane = lax.broadcasted_iota(jnp.int32, (1, 128), 1)
        ds_acc = jnp.zeros((1, 128), F32)
        for j in range(NKV):
            ks = slice(HD * j, HD * (j + 1))
            kk = jnp.concatenate([kp_ref[:, ks], kc_ref[:, ks]], axis=0)
            vv = jnp.concatenate([vp_ref[:, ks], vc_ref[:, ks]], axis=0).astype(BF16)
            qgt = _attn_heads(q_ref, j)
            pn, psn = _attn_probs(qgt, kk, _attn_sink_row(s_ref, j), mask)
            dogt = _attn_heads(do_ref, j)
            dp = jnp.dot(vv, dogt, preferred_element_type=F32)
            delta = jnp.sum(dp * pn, axis=0, keepdims=True)
            dsb = (pn * (dp - delta) * (HD ** -0.5)).astype(BF16)
            dsink = -psn * delta
            dqt = lax.dot_general(kk, dsb, TN_DIMS, preferred_element_type=F32)
            for g in range(GQ):
                h = j * GQ + g
                cols = slice(WINDOW * g, WINDOW * (g + 1))
                dq_ref[HD * h:HD * (h + 1), :] = dqt[:, cols]
                ds_acc = ds_acc + jnp.where(lane == h, jnp.sum(dsink[:, cols], axis=1, keepdims=True), 0.0)
            dkk = lax.dot_general(dsb, qgt, NT_DIMS, preferred_element_type=F32)
            dvv = lax.dot_general(pn.astype(BF16), dogt, NT_DIMS, preferred_element_type=F32)
            dk_ref[:, ks] = dkk[WINDOW:] + dkc_ref[:, ks]
            dv_ref[:, ks] = (dvv[WINDOW:] + dvc_ref[:, ks]).astype(BF16)
            dkc_ref[:, ks] = dkk[:WINDOW]
            dvc_ref[:, ks] = dvv[:WINDOW]
        ds_ref[...] += ds_acc

    cur = lambda i: (nb - 1 - i, 0)
    cur_t = lambda i: (0, nb - 1 - i)
    prev = lambda i: (jnp.maximum(nb - 2 - i, 0), 0)
    p_in, p_out, p_shape = _dproj_piece(t, WINDOW, KVW, lambda i: (nb - 1 - i, V_BLK))
    return pl.pallas_call(
        body, name="attn_bwd", grid=(nb,),
        in_specs=[pl.BlockSpec((QW, WINDOW), cur_t),
                  pl.BlockSpec((WINDOW, KVW), cur), pl.BlockSpec((WINDOW, KVW), prev),
                  pl.BlockSpec((WINDOW, KVW), lambda i: (nb - 1 - i, V_BLK)),
                  pl.BlockSpec((WINDOW, KVW), lambda i: (jnp.maximum(nb - 2 - i, 0), V_BLK)),
                  pl.BlockSpec((1, 128), lambda i: (0, 0)),
                  pl.BlockSpec((QW, WINDOW), cur_t), p_in],
        out_specs=[pl.BlockSpec((QW, WINDOW), cur_t), pl.BlockSpec((WINDOW, KVW), cur),
                   p_out, pl.BlockSpec((1, 128), lambda i: (0, 0))],
        out_shape=[jax.ShapeDtypeStruct((QW, t), F32), jax.ShapeDtypeStruct((t, KVW), F32),
                   p_shape, jax.ShapeDtypeStruct((1, 128), F32)],
        input_output_aliases={7: 2},
        scratch_shapes=[pltpu.VMEM((WINDOW, KVW), F32), pltpu.VMEM((WINDOW, KVW), F32)],
        compiler_params=_params(("arbitrary",), 32 << 20),
    )(qt, kr, kr, proj, proj, sinks, dot_, dproj)


GS_BLK = SEGS["gs"][2] // D
GA_BLK = SEGS["ga"][2] // D


def _merge_fwd(ys, ya, proj):
    t = ys.shape[0]
    tt = _divtile(t, 256)

    def body(ys_ref, ya_ref, gs_ref, ga_ref, o_ref):
        o_ref[...] = (_sigmoid(gs_ref[...]) * ys_ref[...] + _sigmoid(ga_ref[...]) * ya_ref[...]).astype(BF16)

    row = pl.BlockSpec((tt, D), lambda i: (i, 0))
    return pl.pallas_call(
        body, name="merge_fwd", grid=(t // tt,),
        in_specs=[row, row, pl.BlockSpec((tt, D), lambda i: (i, GS_BLK)), pl.BlockSpec((tt, D), lambda i: (i, GA_BLK))],
        out_specs=row, out_shape=jax.ShapeDtypeStruct((t, D), BF16),
        compiler_params=_params(("parallel",), 32 << 20),
    )(ys, ya, proj, proj)


def _dproj_piece(t, rows, width, index_map):
    return (pl.BlockSpec(memory_space=pl.ANY), pl.BlockSpec((rows, width), index_map),
            jax.ShapeDtypeStruct((t, PROJ_PAD), BF16))


def _merge_bwd(dm, ys, ya, proj, dproj):
    t = ys.shape[0]
    tt = _divtile(t, 256)

    def body(dm_ref, ys_ref, ya_ref, gs_ref, ga_ref, _, dys_ref, dya_ref, dg_ref):
        d = dm_ref[...]
        s = _sigmoid(gs_ref[...])
        a = _sigmoid(ga_ref[...])
        dys_ref[...] = (d * s).astype(BF16)
        dya_ref[...] = (d * a).astype(BF16)
        dg_ref[:, :D] = (d * ys_ref[...] * (s * (1.0 - s))).astype(BF16)
        dg_ref[:, D:] = (d * ya_ref[...] * (a * (1.0 - a))).astype(BF16)

    row = pl.BlockSpec((tt, D), lambda i: (i, 0))
    p_in, p_out, p_shape = _dproj_piece(t, tt, 2 * D, lambda i: (i, SEGS["gs"][2] // (2 * D)))
    return pl.pallas_call(
        body, name="merge_bwd", grid=(t // tt,),
        in_specs=[row, row, row, pl.BlockSpec((tt, D), lambda i: (i, GS_BLK)),
                  pl.BlockSpec((tt, D), lambda i: (i, GA_BLK)), p_in],
        out_specs=[row, row, p_out], out_shape=[jax.ShapeDtypeStruct((t, D), BF16)] * 2 + [p_shape],
        input_output_aliases={5: 2},
        compiler_params=_params(("parallel",), 40 << 20),
    )(dm, ys, ya, proj, proj, dproj)


def _pad128(v):
    return jnp.pad(v, ((0, 0), (0, 128 - v.shape[1])))


def _group_major(v):
    t = v.shape[0]
    return jnp.transpose(v[:, :SSD_HEADS].reshape(t, SSD_G, SSD_R), (1, 0, 2))


def _token_major(v3):
    t = v3.shape[1]
    return _pad128(jnp.transpose(v3, (1, 0, 2)).reshape(t, SSD_HEADS))


def _local_step(x, pos, target, w, small, fetch=None, early_grads=None):
    w = dict(w)
    xb = x.astype(BF16)
    gu1, a1 = _mm_swiglu(xb, w["gu1"], "ffn1_gu")
    if fetch is not None:
        w.update(fetch(1, a1))
    f1 = _mm(a1, w["d1"], "nn", F32, "ffn1_down", caps=(512, 1024, FFN_H))
    h1, h1b, xh1, rs1 = _ln_fwd(x, f1, small["ln1_g"], small["ln1_b"], 0.5, "ln1_fwd")
    if fetch is not None:
        w.update(fetch(2, h1b))
    proj = _mm(h1b, w["win"], "nn", F32, "proj", caps=(1024, 896, 2048))
    if fetch is not None:
        w.update(fetch(3, proj))
    bias128 = _pad128(small["dt_bias"])
    alog128 = _pad128(small["a_log"])
    dt, adt = _dt_prep(proj, bias128, alog128)
    dt3, adt3 = _group_major(dt), _group_major(adt)
    xc = _conv_fwd(proj, small["conv_w"], small["conv_b"])
    y_ssd, hprev = _ssd_fwd(xc, dt3, adt3)
    dexp = jnp.repeat(small["d_skip"], SSD_P, axis=1)
    ysn = _gated_norm_fwd(y_ssd, xc, proj, dexp, small["ssd_norm_g"])
    ys = _mm(ysn, w["so"], "nn", F32, "ssd_out")
    invf = jnp.tile(ROPE_THETA ** (-jnp.arange(HD // 2, dtype=F32) * 2.0 / HD), 4)[None, :]
    qt, kr = _rope_fwd(proj, pos, invf)
    sinks128 = _pad128(small["attn_sinks"])
    ot = _attn_fwd(qt, kr, proj, sinks128)
    ya = _mm(ot, w["ao"], "tn", F32, "attn_out")
    mg = _merge_fwd(ys, ya, proj)
    mix = _mm(mg, w["out"], "nn", F32, "mix_out")
    h2, h2b, xh2, rs2 = _ln_fwd(h1, mix, small["ln2_g"], small["ln2_b"], 1.0, "ln2_fwd")
    gu2, a2 = _mm_swiglu(h2b, w["gu2"], "ffn2_gu")
    f2 = _mm(a2, w["d2"], "nn", F32, "ffn2_down", caps=(512, 1024, FFN_H))
    _, _, xh3, rs3, dh3, loss = _ln_fwd(h2, f2, small["ln3_g"], small["ln3_b"], 0.5, "ln3_fwd", target=target)

    gw, gs = {}, {}
    dr3, dr3h, gs["ln3_g"], gs["ln3_b"] = _ln_bwd(dh3, xh3, rs3, small["ln3_g"], 0.5, "ln3_bwd")
    gw["d2"] = _mm(a2, dr3h, "tn", F32, "ffn2_down_dw")
    dgu2 = _mm_swiglu_bwd(dr3h, w["d2"], gu2, "ffn2_down_dx")
    gw["gu2"] = _mm(h2b, dgu2, "tn", F32, "ffn2_gu_dw", caps=(1024, 1408, 2048), n_slabs=N_CHIPS)
    dh2 = _mm(dgu2, w["gu2"], "nt", F32, "ffn2_gu_dx", add=dr3, add_scale=ALPHA, caps=(1024, 1024, 2816))
    dr2, dr2b, gs["ln2_g"], gs["ln2_b"] = _ln_bwd(dh2, xh2, rs2, small["ln2_g"], 1.0, "ln2_bwd")
    gw["out"] = _mm(mg, dr2b, "tn", F32, "mix_out_dw")
    dmg = _mm(dr2b, w["out"], "nt", F32, "mix_out_dx")
    dproj = lax.empty((x.shape[0], PROJ_PAD), BF16)
    dys, dya, dproj = _merge_bwd(dmg, ys, ya, proj, dproj)
    gw["ao"] = _mm(ot, dya, "nn", F32, "attn_out_dw")
    dot_ = _mm(w["ao"], dya, "nt", BF16, "attn_out_dx")
    dqt, dkr, dproj, gs["attn_sinks"] = _attn_bwd(qt, kr, proj, sinks128, dot_, dproj)
    dproj, dk = _rope_bwd(dqt, dkr, pos, invf, dproj)
    dproj = _place_cols(dk, dproj, K_BLK, "place_dk")
    gw["so"] = _mm(ysn, dys, "tn", F32, "ssd_out_dw")
    dysn = _mm(dys, w["so"], "nt", F32, "ssd_out_dx")
    dproj, dy1, gs["ssd_norm_g"], dd_ch = _gated_norm_bwd(dysn, y_ssd, xc, proj, dexp, small["ssd_norm_g"], dproj)
    gs["d_skip"] = _fold_heads(dd_ch, "d_skip_fold")
    dxs, db, dc, dadt3, dxdx3 = _ssd_bwd(xc, dt3, adt3, hprev, dy1)
    ddt, gs["dt_bias"], gs["a_log"] = _dt_bwd(_token_major(dadt3), _token_major(dxdx3), proj, bias128, alog128)
    dproj = _place_cols(ddt, dproj, DT_BLK, "place_ddt")
    cw, cbias = small["conv_w"], small["conv_b"]
    dproj, dwx, dbx = _conv_bwd(proj, dxs, cw, cbias, 0, SSD_INNER, "conv_bwd_x", dproj, skip=(dy1, dexp))
    dproj, dwb, dbb = _conv_bwd(proj, db, cw, cbias, SSD_INNER, SSD_G * SSD_N, "conv_bwd_b", dproj)
    dproj, dwc, dbc = _conv_bwd(proj, dc, cw, cbias, SSD_INNER + SSD_G * SSD_N, SSD_G * SSD_N, "conv_bwd_c", dproj)
    gs["conv_w"] = jnp.concatenate([dwx[:4], dwb[:4], dwc[:4]], axis=1)
    gs["conv_b"] = jnp.concatenate([dbx, dbb, dbc], axis=1)
    gw["win"] = _mm(h1b, dproj, "tn", F32, "proj_dw", caps=(1024, 896, 2048))
    win = w["win"] if early_grads is None else early_grads[0](gw, w["win"])
    dh1 = _mm(dproj, win, "nt", F32, "proj_dx", add=dr2, add_scale=ALPHA, caps=(1024, 1024, 2432))
    ln1_g = small["ln1_g"]
    if early_grads is not None:
        ln1_g = ln1_g + early_grads[1](dh1)[0:1, 0:1]
    dr1, dr1h, gs["ln1_g"], gs["ln1_b"] = _ln_bwd(dh1, xh1, rs1, ln1_g, 0.5, "ln1_bwd")
    gw["d1"] = _mm(a1, dr1h, "tn", F32, "ffn1_down_dw")
    dgu1 = _mm_swiglu_bwd(dr1h, w["d1"], gu1, "ffn1_down_dx")
    gw["gu1"] = _mm(xb, dgu1, "tn", F32, "ffn1_gu_dw", caps=(1024, 1408, 2048), n_slabs=N_CHIPS)
    grad_x = _mm(dgu1, w["gu1"], "nt", F32, "ffn1_gu_dx", add=dr1, add_scale=ALPHA, caps=(1024, 1024, 2816))
    return loss, grad_x, gw, gs


MESH = pl.DeviceIdType.MESH
ANY = pl.BlockSpec(memory_space=pl.ANY)


def _place():
    x, y, c = lax.axis_index("x"), lax.axis_index("y"), lax.axis_index("c")
    peers = [(1 - x, y), (x, 1 - y), (1 - x, 1 - y)]
    return x, y, c, peers


BIG = [
    ("ffn1_w_gate", D, SHARD_H, "gu1", "col", 0),
    ("ffn1_w_up", D, SHARD_H, "gu1", "col", SHARD_H),
    ("ffn1_w_down", SHARD_H, D, "d1", "row", 0),
    ("w_in", D, SHARD_IN, "win4", "lead", 0),
    ("w_ssd_o", SSD_INNER // N_CHIPS, D, "so", "row", 0),
    ("w_attn_o", D // N_CHIPS, D, "ao", "row", 0),
    ("w_out", D // N_CHIPS, D, "out", "row", 0),
    ("ffn2_w_gate", D, SHARD_H, "gu2", "col", 0),
    ("ffn2_w_up", D, SHARD_H, "gu2", "col", SHARD_H),
    ("ffn2_w_down", SHARD_H, D, "d2", "row", 0),
]
GATHERED = {"gu1": (D, 2 * FFN_H), "d1": (FFN_H, D), "win4": (N_CHIPS, D, SHARD_IN), "so": (SSD_INNER, D),
            "ao": (D, D), "out": (D, D), "gu2": (D, 2 * FFN_H), "d2": (FFN_H, D)}


def _cast_place(srcs, oname, chip_idx):
    rows, cols = srcs[0].shape
    tr = _divtile(rows, 256, 16)
    kind = [b[4] for b in BIG if b[3] == oname][0]

    def body(chip_ref, *refs):
        o_ref = refs[-1]
        for k, s_ref in enumerate(refs[:-1]):
            o_ref[:, k * cols:(k + 1) * cols] = s_ref[...].astype(BF16)

    nt = rows // tr
    if kind == "col":
        o_spec = pl.BlockSpec((tr, len(srcs) * cols), lambda i, chip_ref: (i, chip_ref[0]))
    elif kind == "row":
        o_spec = pl.BlockSpec((tr, cols), lambda i, chip_ref: (chip_ref[0] * nt + i, 0))
    else:
        o_spec = pl.BlockSpec((None, tr, cols), lambda i, chip_ref: (chip_ref[0], i, 0))
    return pl.pallas_call(
        body, name="cast_place_" + oname,
        grid_spec=pltpu.PrefetchScalarGridSpec(
            num_scalar_prefetch=1, grid=(nt,),
            in_specs=[pl.BlockSpec((tr, cols), lambda i, chip_ref: (i, 0))] * len(srcs), out_specs=o_spec),
        out_shape=jax.ShapeDtypeStruct(GATHERED[oname], BF16),
        compiler_params=_params(("parallel",), 32 << 20),
    )(chip_idx, *srcs)


def _slot(outs, entry, j, half):
    _, rows, cols, oname, kind, off = entry
    o = outs[oname]
    hr = rows // 2
    if kind == "col":
        cs = pl.ds(pl.multiple_of(j * (2 * SHARD_H) + off, 128), cols)
        return o.at[pl.ds(pl.multiple_of(half * hr, 16), hr), cs]
    if kind == "row":
        return o.at[pl.ds(pl.multiple_of(j * rows + half * hr, 16), hr), :]
    return o.at[j, pl.ds(pl.multiple_of(half * hr, 16), hr), :]


HBM = pl.BlockSpec(memory_space=pltpu.HBM)
SEM = pl.BlockSpec(memory_space=pltpu.SEMAPHORE)


def _ici_copy(outs, entry, j, c, to, send, recv, k):
    ref = _slot(outs, entry, j, c)
    return pltpu.make_async_remote_copy(src_ref=ref, dst_ref=ref, send_sem=send.at[k], recv_sem=recv.at[k],
                                        device_id=to, device_id_type=MESH)


GATHER_GROUPS = [["gu1"], ["d1"], ["win4"], ["so", "ao", "out", "gu2", "d2"]]


def _gather_ici_start(placed, groups, tag, after):
    names = [k for grp in groups for k in grp]
    bigs = [[b for b in BIG if b[3] in grp] for grp in groups]
    ng = len(groups)
    n_in = len(names) + 1

    def body(*refs):
        sems = refs[n_in:n_in + 2 * ng]
        outs = dict(zip(names, refs[n_in + 2 * ng:n_in + 2 * ng + len(names)]))
        token = refs[-1]
        x, y, c, peers = _place()
        for gi, big in enumerate(bigs):
            for i, entry in enumerate(big):
                for k, (px, py) in enumerate(peers):
                    _ici_copy(outs, entry, 2 * x + y, c, (px, py, c), sems[2 * gi], sems[2 * gi + 1], 3 * i + k).start()
        token[...] = jnp.zeros_like(token)

    sem_shapes = [pltpu.SemaphoreType.DMA((3 * len(big),)) for big in bigs for _ in range(2)]
    res = pl.pallas_call(
        body, name="gather_ici_start_" + tag,
        in_specs=[HBM] * len(names) + [pl.BlockSpec(memory_space=pl.ANY)],
        out_specs=[SEM] * (2 * ng) + [HBM] * len(names) + [pl.BlockSpec(memory_space=pltpu.VMEM)],
        out_shape=sem_shapes + [pltpu.HBM(GATHERED[k], BF16) for k in names] + [jax.ShapeDtypeStruct((8, 128), F32)],
        input_output_aliases={i: i + 2 * ng for i in range(len(names))},
        compiler_params=pltpu.CompilerParams(has_side_effects=pltpu.SideEffectType.DATAFLOW_SIDE_EFFECTING),
    )(*[pltpu.with_memory_space_constraint(placed[k], pltpu.HBM) for k in names], after)
    sems = [(res[2 * gi], res[2 * gi + 1]) for gi in range(ng)]
    return sems, dict(zip(names, res[2 * ng:2 * ng + len(names)])), res[-1]


def _gather_ici_wait(send, recv, arrays, names, after, tag):
    big = [b for b in BIG if b[3] in names]

    def body(*refs):
        outs = dict(zip(names, refs[:len(names)]))
        send_ref, recv_ref = refs[len(names)], refs[len(names) + 1]
        x, y, c, peers = _place()
        for i, entry in enumerate(big):
            for k, (px, py) in enumerate(peers):
                mine = _ici_copy(outs, entry, 2 * x + y, c, (px, py, c), send_ref, recv_ref, 3 * i + k)
                mine.wait_send()
                theirs = _ici_copy(outs, entry, 2 * px + py, c, (px, py, c), send_ref, recv_ref, 3 * i + k)
                theirs.wait_recv()

    res = pl.pallas_call(
        body, name="gather_ici_wait_" + tag,
        in_specs=[HBM] * len(names) + [SEM, SEM, pl.BlockSpec(memory_space=pl.ANY)],
        out_specs=[HBM] * len(names),
        out_shape=[pltpu.HBM(GATHERED[k], BF16) for k in names],
        input_output_aliases={i: i for i in range(len(names))},
        compiler_params=pltpu.CompilerParams(has_side_effects=pltpu.SideEffectType.DATAFLOW_SIDE_EFFECTING),
    )(*[arrays[k] for k in names], send, recv, after)
    return dict(zip(names, res))


def _gather_d2d(arrays, names, tag):
    big = [b for b in BIG if b[3] in names]
    n = len(big)

    def body(*refs):
        outs = dict(zip(names, refs[len(names):2 * len(names)]))
        fsend, frecv = refs[2 * len(names):]
        x, y, c, peers = _place()
        cps = []
        for i, entry in enumerate(big):
            for k, (px, py) in enumerate(peers):
                cp = _ici_copy(outs, entry, 2 * px + py, c, (x, y, 1 - c), fsend, frecv, 3 * i + k)
                cp.start()
                cps.append(cp)
        for i, entry in enumerate(big):
            for k, (px, py) in enumerate(peers):
                _ici_copy(outs, entry, 2 * px + py, 1 - c, (x, y, 1 - c), fsend, frecv, 3 * i + k).wait_recv()
        for cp in cps:
            cp.wait_send()

    res = pl.pallas_call(
        body, name="gather_d2d_" + tag,
        in_specs=[ANY] * len(names), out_specs=[ANY] * len(names),
        out_shape=[jax.ShapeDtypeStruct(GATHERED[k], BF16) for k in names],
        input_output_aliases={i: i for i in range(len(names))},
        scratch_shapes=[pltpu.SemaphoreType.DMA((3 * n,))] * 2,
    )(*[arrays[k] for k in names])
    return dict(zip(names, res))


def _win_pieces():
    pieces = []
    for g0, wd, i0 in SEGS.values():
        for j in range(N_CHIPS):
            lo, hi = max(g0, j * SHARD_IN), min(g0 + wd, (j + 1) * SHARD_IN)
            if lo < hi:
                pieces.append((j, lo - j * SHARD_IN, hi - j * SHARD_IN, i0 + lo - g0))
    return pieces


def _win_to_internal(win4):
    tr = 128

    def body(i_ref, o_ref):
        for j, s0, s1, d0 in _win_pieces():
            o_ref[:, d0:d0 + s1 - s0] = i_ref[j, :, s0:s1]
        o_ref[:, PROJ_W:] = jnp.zeros((tr, PROJ_PAD - PROJ_W), o_ref.dtype)

    return pl.pallas_call(
        body, name="win_to_internal", grid=(D // tr,),
        in_specs=[pl.BlockSpec((N_CHIPS, tr, SHARD_IN), lambda i: (0, i, 0))],
        out_specs=pl.BlockSpec((tr, PROJ_PAD), lambda i: (i, 0)),
        out_shape=jax.ShapeDtypeStruct((D, PROJ_PAD), win4.dtype),
        compiler_params=_params(("parallel",), 40 << 20),
    )(win4)


def _win_from_internal(g):
    tr = 64

    def body(i_ref, o_ref):
        for j, s0, s1, d0 in _win_pieces():
            o_ref[j, :, s0:s1] = i_ref[:, d0:d0 + s1 - s0]

    return pl.pallas_call(
        body, name="win_from_internal", grid=(D // tr,),
        in_specs=[pl.BlockSpec((tr, PROJ_PAD), lambda i: (i, 0))],
        out_specs=pl.BlockSpec((N_CHIPS, tr, SHARD_IN), lambda i: (0, i, 0)),
        out_shape=jax.ShapeDtypeStruct((N_CHIPS, D, SHARD_IN), g.dtype),
        compiler_params=_params(("parallel",), 40 << 20),
    )(g)


def _rs_pair_exchange(grads, tag):
    n = len(grads)

    def body(*refs):
        srcs, dsts = refs[:n], refs[n:2 * n]
        send, recv = refs[2 * n:]
        x, y, c, _ = _place()
        cps = []
        for i in range(n):
            hr = srcs[i].shape[1] // 2
            cp = pltpu.make_async_remote_copy(
                src_ref=srcs[i].at[:, pl.ds(pl.multiple_of((1 - c) * hr, 16), hr), :], dst_ref=dsts[i],
                send_sem=send.at[i], recv_sem=recv.at[i], device_id=(x, y, 1 - c), device_id_type=MESH)
            cp.start()
            cps.append(cp)
        for cp in cps:
            cp.wait()

    return pl.pallas_call(
        body, name="rs_pair_exchange_" + tag, in_specs=[ANY] * n, out_specs=[ANY] * n,
        out_shape=[jax.ShapeDtypeStruct((g.shape[0], g.shape[1] // 2, g.shape[2]), F32) for g in grads],
        scratch_shapes=[pltpu.SemaphoreType.DMA((n,))] * 2,
    )(*grads)


def _pair_copy(src, dst, c, to, send, recv, k):
    hr = src.shape[1] // 2
    return pltpu.make_async_remote_copy(
        src_ref=src.at[:, pl.ds(pl.multiple_of((1 - c) * hr, 16), hr), :], dst_ref=dst,
        send_sem=send.at[k], recv_sem=recv.at[k], device_id=to, device_id_type=MESH)


def _rs_pair_start(grads, carried):
    n = len(grads)

    def body(*refs):
        send, recv = refs[2 * n + 1], refs[2 * n + 2]
        srcs, dsts = refs[2 * n + 3:3 * n + 3], refs[3 * n + 3:4 * n + 3]
        x, y, c, _ = _place()
        for i in range(n):
            _pair_copy(srcs[i], dsts[i], c, (x, y, 1 - c), send, recv, i).start()

    lands = [lax.empty((g.shape[0], g.shape[1] // 2, g.shape[2]), F32) for g in grads]
    res = pl.pallas_call(
        body, name="rs_pair_start",
        in_specs=[HBM] * (2 * n + 1), out_specs=[SEM, SEM] + [HBM] * (2 * n + 1),
        out_shape=[pltpu.SemaphoreType.DMA((n,)), pltpu.SemaphoreType.DMA((n,))]
        + [pltpu.HBM(g.shape, F32) for g in grads] + [pltpu.HBM(l.shape, F32) for l in lands]
        + [pltpu.HBM(carried.shape, carried.dtype)],
        input_output_aliases={i: i + 2 for i in range(2 * n + 1)},
        compiler_params=pltpu.CompilerParams(has_side_effects=pltpu.SideEffectType.DATAFLOW_SIDE_EFFECTING),
    )(*[pltpu.with_memory_space_constraint(a, pltpu.HBM) for a in list(grads) + lands + [carried]])
    return (res[0], res[1], list(res[2:2 + n]), list(res[2 + n:2 + 2 * n])), res[-1]


def _rs_pair_wait(send, recv, grads, lands, after):
    n = len(grads)

    def body(*refs):
        srcs, dsts = refs[:n], refs[n:2 * n]
        send_ref, recv_ref = refs[2 * n], refs[2 * n + 1]
        x, y, c, _ = _place()
        for i in range(n):
            cp = _pair_copy(srcs[i], dsts[i], c, (x, y, 1 - c), send_ref, recv_ref, i)
            cp.wait_send()
            cp.wait_recv()

    res = pl.pallas_call(
        body, name="rs_pair_wait",
        in_specs=[HBM] * (2 * n) + [SEM, SEM, pl.BlockSpec(memory_space=pl.ANY)],
        out_specs=[HBM] * (2 * n),
        out_shape=[pltpu.HBM(g.shape, F32) for g in grads] + [pltpu.HBM(l.shape, F32) for l in lands],
        input_output_aliases={i: i for i in range(2 * n)},
        compiler_params=pltpu.CompilerParams(has_side_effects=pltpu.SideEffectType.DATAFLOW_SIDE_EFFECTING),
    )(*grads, *lands, send, recv, after)
    return list(res[:n]), list(res[n:])


def _half_tile(hr):
    return _divtile(hr, 256, 16) if hr % 256 == 0 else _divtile(hr, 512, 16)


def _rs_pair_sum(g, r, c_idx, name):
    ns, rows, cols = g.shape
    hr = rows // 2
    tr = _half_tile(hr)
    nt = hr // tr

    def body(c_ref, g_ref, r_ref, ob_ref, of_ref):
        s = g_ref[...] + r_ref[...]
        ob_ref[...] = s.astype(BF16)
        of_ref[...] = s

    blk = pl.BlockSpec((None, tr, cols), lambda j, t, c_ref: (j, t, 0))
    return pl.pallas_call(
        body, name=name,
        grid_spec=pltpu.PrefetchScalarGridSpec(
            num_scalar_prefetch=1, grid=(ns, nt),
            in_specs=[pl.BlockSpec((None, tr, cols), lambda j, t, c_ref: (j, c_ref[0] * nt + t, 0)), blk],
            out_specs=[blk, blk]),
        out_shape=[jax.ShapeDtypeStruct((ns, hr, cols), BF16), jax.ShapeDtypeStruct((ns, hr, cols), F32)],
        compiler_params=_params(("parallel", "parallel"), 48 << 20),
    )(c_idx, g, r)


def _rs_chip_start(parts, tag):
    n = len(parts)

    def body(*refs):
        send, recv = refs[2 * n], refs[2 * n + 1]
        srcs, dsts = refs[2 * n + 2:3 * n + 2], refs[3 * n + 2:4 * n + 2]
        token = refs[-1]
        x, y, c, peers = _place()
        for i in range(n):
            for k, (px, py) in enumerate(peers):
                pltpu.make_async_remote_copy(
                    src_ref=srcs[i].at[2 * px + py], dst_ref=dsts[i].at[k],
                    send_sem=send.at[3 * i + k], recv_sem=recv.at[3 * i + k],
                    device_id=(px, py, c), device_id_type=MESH).start()
        token[...] = jnp.zeros_like(token)

    lands = [lax.empty((3,) + p.shape[1:], BF16) for p in parts]
    res = pl.pallas_call(
        body, name="rs_chip_start_" + tag,
        in_specs=[HBM] * (2 * n),
        out_specs=[SEM, SEM] + [HBM] * (2 * n) + [pl.BlockSpec(memory_space=pltpu.VMEM)],
        out_shape=[pltpu.SemaphoreType.DMA((3 * n,)), pltpu.SemaphoreType.DMA((3 * n,))]
        + [pltpu.HBM(p.shape, BF16) for p in parts] + [pltpu.HBM(l.shape, BF16) for l in lands]
        + [jax.ShapeDtypeStruct((8, 128), F32)],
        input_output_aliases={i: i + 2 for i in range(2 * n)},
        compiler_params=pltpu.CompilerParams(has_side_effects=pltpu.SideEffectType.DATAFLOW_SIDE_EFFECTING),
    )(*[pltpu.with_memory_space_constraint(a, pltpu.HBM) for a in list(parts) + lands])
    return res[0], res[1], list(res[2:2 + n]), list(res[2 + n:2 + 2 * n]), res[-1]


def _rs_chip_wait(send, recv, parts, lands, after, tag):
    n = len(parts)

    def body(*refs):
        srcs, dsts = refs[:n], refs[n:2 * n]
        send_ref, recv_ref = refs[2 * n], refs[2 * n + 1]
        x, y, c, peers = _place()
        for i in range(n):
            for k, (px, py) in enumerate(peers):
                cp = pltpu.make_async_remote_copy(
                    src_ref=srcs[i].at[2 * px + py], dst_ref=dsts[i].at[k],
                    send_sem=send_ref.at[3 * i + k], recv_sem=recv_ref.at[3 * i + k],
                    device_id=(px, py, c), device_id_type=MESH)
                cp.wait_send()
                cp.wait_recv()

    res = pl.pallas_call(
        body, name="rs_chip_wait_" + tag,
        in_specs=[HBM] * (2 * n) + [SEM, SEM, pl.BlockSpec(memory_space=pl.ANY)],
        out_specs=[HBM] * (2 * n),
        out_shape=[pltpu.HBM(p.shape, BF16) for p in parts] + [pltpu.HBM(l.shape, BF16) for l in lands],
        input_output_aliases={i: i for i in range(2 * n)},
        compiler_params=pltpu.CompilerParams(has_side_effects=pltpu.SideEffectType.DATAFLOW_SIDE_EFFECTING),
    )(*parts, *lands, send, recv, after)
    return list(res[n:])


def _rs_final_sum(own, got, chip_idx, c_idx, name):
    ns, hr, cols = own.shape
    tr = _half_tile(hr)
    nt = hr // tr

    def body(chip_ref, c_ref, o_ref, g_ref, out_ref):
        s = o_ref[...]
        for k in range(3):
            s = s + g_ref[k].astype(F32)
        out_ref[...] = s

    return pl.pallas_call(
        body, name=name,
        grid_spec=pltpu.PrefetchScalarGridSpec(
            num_scalar_prefetch=2, grid=(nt,),
            in_specs=[pl.BlockSpec((None, tr, cols), lambda t, chip_ref, c_ref: (chip_ref[0], t, 0)),
                      pl.BlockSpec((3, tr, cols), lambda t, chip_ref, c_ref: (0, t, 0))],
            out_specs=pl.BlockSpec((tr, cols), lambda t, chip_ref, c_ref: (c_ref[0] * nt + t, 0))),
        out_shape=jax.ShapeDtypeStruct((2 * hr, cols), F32),
        compiler_params=_params(("parallel",), 48 << 20),
    )(chip_idx, c_idx, own, got)


def _rs_share_halves(fulls, tag):
    n = len(fulls)

    def body(*refs):
        dsts = refs[n:2 * n]
        send, recv = refs[2 * n:]
        x, y, c, _ = _place()
        cps = []
        for i in range(n):
            hr = dsts[i].shape[0] // 2
            rows = dsts[i].at[pl.ds(pl.multiple_of(c * hr, 8), hr), :]
            cp = pltpu.make_async_remote_copy(src_ref=rows, dst_ref=rows, send_sem=send.at[i], recv_sem=recv.at[i],
                                              device_id=(x, y, 1 - c), device_id_type=MESH)
            cp.start()
            cps.append(cp)
        for i in range(n):
            hr = dsts[i].shape[0] // 2
            other = dsts[i].at[pl.ds(pl.multiple_of((1 - c) * hr, 8), hr), :]
            pltpu.make_async_remote_copy(src_ref=other, dst_ref=other, send_sem=send.at[i], recv_sem=recv.at[i],
                                         device_id=(x, y, 1 - c), device_id_type=MESH).wait_recv()
        for cp in cps:
            cp.wait_send()

    return pl.pallas_call(
        body, name="rs_share_halves_" + tag, in_specs=[ANY] * n, out_specs=[ANY] * n,
        out_shape=[jax.ShapeDtypeStruct(f.shape, F32) for f in fulls],
        input_output_aliases={i: i for i in range(n)},
        scratch_shapes=[pltpu.SemaphoreType.DMA((n,))] * 2,
    )(*fulls)


def _all_reduce_small(v):
    rows = v.shape[0]

    def body(v_ref, o_ref, buf, send, recv):
        x, y, c, _ = _place()
        me = 4 * x + 2 * y + c
        buf[me] = v_ref[...]
        cps = []
        for d in range(1, 8):
            px, py, pc = x ^ (d >> 2), y ^ ((d >> 1) & 1), c ^ (d & 1)
            cp = pltpu.make_async_remote_copy(src_ref=v_ref, dst_ref=buf.at[me], send_sem=send.at[d - 1],
                                              recv_sem=recv.at[d - 1], device_id=(px, py, pc), device_id_type=MESH)
            cp.start()
            cps.append(cp)
        for d in range(1, 8):
            px, py, pc = x ^ (d >> 2), y ^ ((d >> 1) & 1), c ^ (d & 1)
            pltpu.make_async_remote_copy(src_ref=v_ref, dst_ref=buf.at[4 * px + 2 * py + pc], send_sem=send.at[d - 1],
                                         recv_sem=recv.at[d - 1], device_id=(px, py, pc),
                                         device_id_type=MESH).wait_recv()
        for cp in cps:
            cp.wait_send()
        acc = buf[0]
        for d in range(1, 8):
            acc = acc + buf[d]
        o_ref[...] = acc

    vm = pl.BlockSpec(memory_space=pltpu.VMEM)
    return pl.pallas_call(
        body, name="all_reduce_small", in_specs=[vm], out_specs=vm,
        out_shape=jax.ShapeDtypeStruct((rows, 128), F32),
        scratch_shapes=[pltpu.VMEM((8, rows, 128), F32), pltpu.SemaphoreType.DMA((7,)), pltpu.SemaphoreType.DMA((7,))],
    )(v)


def _adamw(w, g, m, v, name, g_col_blk=0):
    rows, cols = w.shape
    tr = _divtile(rows, max(8, (2 << 20) // (4 * cols) // 8 * 8), 8)

    def body(w_ref, g_ref, m_ref, v_ref, go_ref, d_ref, mo_ref, vo_ref):
        gv = g_ref[...]
        mn = ADAM_B1 * m_ref[...] + (1.0 - ADAM_B1) * gv
        vn = ADAM_B2 * v_ref[...] + (1.0 - ADAM_B2) * (gv * gv)
        m_hat = mn / (1.0 - ADAM_B1 ** ADAM_STEP)
        v_hat = vn / (1.0 - ADAM_B2 ** ADAM_STEP)
        go_ref[...] = gv
        d_ref[...] = -ADAM_LR * (m_hat / (jnp.sqrt(v_hat) + ADAM_EPS) + ADAM_WD * w_ref[...])
        mo_ref[...] = mn
        vo_ref[...] = vn

    blk = pl.BlockSpec((tr, cols), lambda i: (i, 0))
    return pl.pallas_call(
        body, name=name, grid=(rows // tr,),
        in_specs=[blk, pl.BlockSpec((tr, cols), lambda i: (i, g_col_blk)), blk, blk],
        out_specs=[blk] * 4, out_shape=[jax.ShapeDtypeStruct((rows, cols), F32)] * 4,
        compiler_params=_params(("parallel",), 48 << 20),
    )(w, g, m, v)


SMALL = ["ln1_g", "ln1_b", "conv_w", "conv_b", "dt_bias", "a_log", "d_skip", "ssd_norm_g", "attn_sinks",
         "ln2_g", "ln2_b", "ln3_g", "ln3_b"]


def _pack_rows(vs):
    parts = []
    for v in vs:
        v = v.reshape(-1)
        parts.append(jnp.pad(v, (0, (-v.shape[0]) % 128)))
    flat = jnp.concatenate(parts)
    flat = jnp.pad(flat, (0, (-flat.shape[0]) % 1024))
    return flat.reshape(-1, 128)


def _unpack_rows(packed, shapes):
    flat = packed.reshape(-1)
    out, at = [], 0
    for s in shapes:
        nel = int(np.prod(s))
        out.append(flat[at:at + nel].reshape(s))
        at += nel + (-nel) % 128
    return out


def kernel(x, positions, ffn1_w_gate, ffn1_w_up, ffn1_w_down, ln1_g, ln1_b, w_in, conv_w, conv_b, dt_bias, a_log, d_skip, ssd_norm_g, w_ssd_o, attn_sinks, w_attn_o, w_out, ln2_g, ln2_b, ffn2_w_gate, ffn2_w_up, ffn2_w_down, ln3_g, ln3_b, loss_target, m_ffn1_w_gate, m_ffn1_w_up, m_ffn1_w_down, m_ln1_g, m_ln1_b, m_w_in, m_conv_w, m_conv_b, m_dt_bias, m_a_log, m_d_skip, m_ssd_norm_g, m_w_ssd_o, m_attn_sinks, m_w_attn_o, m_w_out, m_ln2_g, m_ln2_b, m_ffn2_w_gate, m_ffn2_w_up, m_ffn2_w_down, m_ln3_g, m_ln3_b, v_ffn1_w_gate, v_ffn1_w_up, v_ffn1_w_down, v_ln1_g, v_ln1_b, v_w_in, v_conv_w, v_conv_b, v_dt_bias, v_a_log, v_d_skip, v_ssd_norm_g, v_w_ssd_o, v_attn_sinks, v_w_attn_o, v_w_out, v_ln2_g, v_ln2_b, v_ffn2_w_gate, v_ffn2_w_up, v_ffn2_w_down, v_ln3_g, v_ln3_b):
    args = dict(locals())
    wts = {n: args[n][0] for n in [b[0] for b in BIG] + SMALL}
    mom_m = {n: args["m_" + n][0] for n in wts}
    mom_v = {n: args["v_" + n][0] for n in wts}
    t = x.shape[1]
    xi, yi, ci = lax.axis_index("x"), lax.axis_index("y"), lax.axis_index("c")
    chip = 2 * xi + yi

    c_idx = ci.astype(jnp.int32).reshape(1)
    chip_idx = chip.astype(jnp.int32).reshape(1)
    placed = {o: _cast_place([wts[b[0]] for b in BIG if b[3] == o], o, chip_idx) for o in GATHERED}
    g_sems, g_flight = {}, {}

    def fetch(group, after):
        names = GATHER_GROUPS[group]
        send, recv = g_sems[group]
        landed = _gather_ici_wait(send, recv, {k: g_flight[k] for k in names}, names, after, str(group))
        got = _gather_d2d(landed, names, str(group))
        if "win4" in got:
            got["win"] = _win_to_internal(got.pop("win4"))
        return got

    sems, arrays, token = _gather_ici_start(placed, GATHER_GROUPS, "all", wts["ln1_g"])
    g_sems.update(dict(enumerate(sems)))
    g_flight.update(arrays)
    w = fetch(0, token)

    def slabs_of(gw, names):
        view = {"gu1": lambda: gw["gu1"], "gu2": lambda: gw["gu2"],
                "d1": lambda: gw["d1"].reshape(N_CHIPS, SHARD_H, D), "d2": lambda: gw["d2"].reshape(N_CHIPS, SHARD_H, D),
                "win": lambda: _win_from_internal(gw["win"]),
                "so": lambda: gw["so"].reshape(N_CHIPS, SSD_INNER // N_CHIPS, D),
                "ao": lambda: gw["ao"].reshape(N_CHIPS, D // N_CHIPS, D),
                "out": lambda: gw["out"].reshape(N_CHIPS, D // N_CHIPS, D)}
        return [view[nm]() for nm in names]

    early = ["win", "so", "ao", "out", "gu2", "d2"]
    late = ["gu1", "d1"]
    flight = {}

    def early_start(gw, win):
        flight["pair"], win = _rs_pair_start(slabs_of(gw, early), win)
        return win

    def early_mid(dh1):
        slabs, from_sib = _rs_pair_wait(*flight["pair"], dh1)
        pair = [_rs_pair_sum(g, r, c_idx, "rs_pair_sum_" + nm) for g, r, nm in zip(slabs, from_sib, early)]
        send, recv, parts, lands, token = _rs_chip_start([p[0] for p in pair], "early")
        flight.update(send=send, recv=recv, parts=parts, lands=lands, own=[p[1] for p in pair])
        return token

    early_grads = (early_start, early_mid)
    cw_rows = _pack_rows([lax.dynamic_update_slice(jnp.zeros((4, XBC), F32), wts["conv_w"], (0, chip * (XBC // N_CHIPS)))])
    cw_rows = jnp.where(ci == 0, cw_rows, 0.0)
    conv_w_full = _all_reduce_small(cw_rows)[:4 * XBC // 128].reshape(4, XBC)

    small = {n: (wts[n][None, :] if wts[n].ndim == 1 else wts[n]) for n in SMALL}
    small["conv_w"] = conv_w_full
    loss, grad_x, gw, gs = _local_step(x[0], positions[0].astype(F32)[:, None], loss_target[0], w, small,
                                       fetch=fetch, early_grads=early_grads)

    slabs = slabs_of(gw, late)
    from_sib = _rs_pair_exchange(slabs, "late")
    pair = [_rs_pair_sum(g, r, c_idx, "rs_pair_sum_" + nm) for g, r, nm in zip(slabs, from_sib, late)]
    l_send, l_recv, l_parts, l_lands, l_token = _rs_chip_start([p[0] for p in pair], "late")
    got_early = _rs_chip_wait(flight["send"], flight["recv"], flight["parts"], flight["lands"], l_token, "early")

    outs = {}
    big_src = {"ffn1_w_gate": ("gu1", 0), "ffn1_w_up": ("gu1", 1), "ffn1_w_down": ("d1", 0), "w_in": ("win", 0),
               "w_ssd_o": ("so", 0), "w_attn_o": ("ao", 0), "w_out": ("out", 0),
               "ffn2_w_gate": ("gu2", 0), "ffn2_w_up": ("gu2", 1), "ffn2_w_down": ("d2", 0)}

    def finish(names, own, got, tag):
        halves = [_rs_final_sum(o, gt, chip_idx, c_idx, "rs_final_sum_" + nm) for o, gt, nm in zip(own, got, names)]
        full = dict(zip(names, _rs_share_halves(halves, tag)))
        for nm, (src, blk) in big_src.items():
            if src in full:
                outs[nm] = _adamw(wts[nm], full[src], mom_m[nm], mom_v[nm], "adamw_" + nm, g_col_blk=blk)

    finish(early, flight["own"], got_early, "early")
    got_late = _rs_chip_wait(l_send, l_recv, l_parts, l_lands, outs["w_in"][1], "late")
    finish(late, [p[1] for p in pair], got_late, "late")

    gvec = {n: gs[n] for n in SMALL}
    gvec["dt_bias"], gvec["a_log"], gvec["d_skip"] = gs["dt_bias"][:, :64], gs["a_log"][:, :64], gs["d_skip"][:, :64]
    gvec["attn_sinks"] = gs["attn_sinks"][:, :NQ]
    red = _all_reduce_small(_pack_rows([gvec[n] for n in SMALL] + [loss]))
    shapes = [(4, XBC) if n == "conv_w" else wts[n].shape for n in SMALL] + [(1,)]
    red_list = _unpack_rows(red, shapes)
    loss_out = red_list[-1].reshape(())
    gsm = dict(zip(SMALL, red_list[:-1]))
    gsm["conv_w"] = lax.dynamic_slice_in_dim(gsm["conv_w"], chip * (XBC // N_CHIPS), XBC // N_CHIPS, axis=1)
    sm_shapes = [wts[n].shape for n in SMALL]
    res = _adamw(_pack_rows([wts[n] for n in SMALL]), _pack_rows([gsm[n] for n in SMALL]),
                 _pack_rows([mom_m[n] for n in SMALL]), _pack_rows([mom_v[n] for n in SMALL]), "adamw_small")
    res = [_unpack_rows(r, sm_shapes) for r in res]
    for i, nm in enumerate(SMALL):
        outs[nm] = tuple(r[i] for r in res)

    order = ["ffn1_w_gate", "ffn1_w_up", "ffn1_w_down", "ln1_g", "ln1_b", "w_in", "conv_w", "conv_b", "dt_bias", "a_log",
             "d_skip", "ssd_norm_g", "w_ssd_o", "attn_sinks", "w_attn_o", "w_out", "ln2_g", "ln2_b",
             "ffn2_w_gate", "ffn2_w_up", "ffn2_w_down", "ln3_g", "ln3_b"]
    result = [loss_out, grad_x[None]]
    for kind in range(4):
        result += [outs[nm][kind][None] for nm in order]
    return tuple(result)
```

```python
import functools
import math

import numpy as np
import jax
import jax.numpy as jnp
from jax import lax
from jax.experimental import pallas as pl
from jax.experimental.pallas import tpu as pltpu

F32 = jnp.float32
BF16 = jnp.bfloat16
HI = lax.Precision.HIGHEST

D = 2048
FFN_H = 5632
SSD_INNER = 4096
SSD_HEADS = 64
SSD_P = 64
SSD_G = 8
SSD_R = 8
SSD_N = 128
CHUNK = 128
XBC = 6144
NQ = 32
NKV = 4
HD = 64
QW = 2048
KVW = 256
WINDOW = 128
ROPE_THETA = 10000.0
ALPHA = 2.0 ** 0.25
LN_EPS = 1e-5
RMS_EPS = 1e-5
PROJ_W = 16960
N_CHIPS = 4
SHARD_IN = PROJ_W // N_CHIPS
SHARD_H = FFN_H // N_CHIPS

SEGS = {
    "z": (0, 4096, 0),
    "xbc": (4096, 6144, 10240),
    "dt": (10240, 64, 16896),
    "q": (10304, 2048, 8192),
    "k": (12352, 256, 16384),
    "v": (12608, 256, 16640),
    "gs": (12864, 2048, 4096),
    "ga": (14912, 2048, 6144),
}
PROJ_PAD = 17024

ADAM_LR = 0.001
ADAM_B1 = 0.9
ADAM_B2 = 0.999
ADAM_EPS = 1e-08
ADAM_WD = 0.01
ADAM_STEP = 10

VMEM_CAP = 60 * 1024 * 1024


def _params(sem, vmem_bytes):
    return pltpu.CompilerParams(dimension_semantics=sem, vmem_limit_bytes=int(min(VMEM_CAP, vmem_bytes)))


def _divtile(n, cap, q=128):
    best = None
    for d in range(q, min(n, cap) + 1, q):
        if n % d == 0:
            best = d
    return n if best is None else best


def _sigmoid(x):
    return 0.5 * jnp.tanh(0.5 * x) + 0.5


def _mm(a, b, mode, out_dtype, name, add=None, add_scale=1.0, caps=(1024, 1024, 2048), n_slabs=1):
    if mode == "nn":
        (m, k), (k2, n) = a.shape, b.shape
    elif mode == "nt":
        (m, k), (n, k2) = a.shape, b.shape
    else:
        (k, m), (k2, n) = a.shape, b.shape
    assert k == k2, (a.shape, b.shape, mode)
    tm, tn, tk = _divtile(m, caps[0]), _divtile(n // n_slabs, caps[1]), _divtile(k, caps[2])
    nk = k // tk
    per_slab = n // n_slabs // tn
    dims = {"nn": ((1,), (0,)), "nt": ((1,), (1,)), "tn": ((0,), (0,))}[mode]
    has_add = add is not None

    def body(*refs):
        if has_add:
            a_ref, b_ref, add_ref, o_ref = refs[:4]
            scr = refs[4:]
        else:
            a_ref, b_ref, o_ref = refs[:3]
            add_ref = None
            scr = refs[3:]
        part = lax.dot_general(a_ref[...].astype(BF16), b_ref[...].astype(BF16), (dims, ((), ())),
                               preferred_element_type=F32)

        def finish(acc):
            if has_add:
                acc = acc + add_scale * add_ref[...].astype(F32)
            o_ref[...] = acc.astype(o_ref.dtype)

        if nk == 1:
            finish(part)
        else:
            acc_ref = scr[0]
            kk = pl.program_id(2)

            @pl.when(kk == 0)
            def _():
                acc_ref[...] = part

            @pl.when(kk > 0)
            def _():
                acc_ref[...] += part

            @pl.when(kk == nk - 1)
            def _():
                finish(acc_ref[...])

    if mode == "nn":
        a_spec = pl.BlockSpec((tm, tk), lambda i, j, kk: (i, kk))
        b_spec = pl.BlockSpec((tk, tn), lambda i, j, kk: (kk, j))
    elif mode == "nt":
        a_spec = pl.BlockSpec((tm, tk), lambda i, j, kk: (i, kk))
        b_spec = pl.BlockSpec((tn, tk), lambda i, j, kk: (j, kk))
    else:
        a_spec = pl.BlockSpec((tk, tm), lambda i, j, kk: (kk, i))
        b_spec = pl.BlockSpec((tk, tn), lambda i, j, kk: (kk, j))
    o_spec = pl.BlockSpec((tm, tn), lambda i, j, kk: (i, j))
    out_shape = jax.ShapeDtypeStruct((m, n), out_dtype)
    if n_slabs > 1:
        assert not has_add
        o_spec = pl.BlockSpec((None, tm, tn), lambda i, j, kk: (j // per_slab, i, j % per_slab))
        out_shape = jax.ShapeDtypeStruct((n_slabs, m, n // n_slabs), out_dtype)
    in_specs = [a_spec, b_spec] + ([o_spec] if has_add else [])
    args = (a, b) + ((add,) if has_add else ())
    osz = jnp.dtype(out_dtype).itemsize
    vmem = (2 * (tm * tk * a.dtype.itemsize + tk * tn * b.dtype.itemsize) + 2 * tm * tn * osz
            + (2 * tm * tn * add.dtype.itemsize if has_add else 0) + 2 * tm * tn * 4
            + 2 * (tm * tk + tk * tn) + (8 << 20))
    return pl.pallas_call(
        body, name=name, grid=(m // tm, n // tn, nk),
        in_specs=in_specs, out_specs=o_spec, out_shape=out_shape,
        scratch_shapes=[pltpu.VMEM((tm, tn), F32)] if nk > 1 else [],
        compiler_params=_params(("parallel", "parallel", "arbitrary"), vmem),
    )(*args)


def _mm_swiglu(a, b, name, chip_idx=None, done=None):
    m, k = a.shape
    w = SHARD_H
    tm = _divtile(m, 512)

    def body(*refs):
        a_ref, b_ref = refs[-4 if done is None else -6], refs[-3 if done is None else -5]
        gu_ref, act_ref = refs[-2:]
        gu = jnp.dot(a_ref[...], b_ref[...], preferred_element_type=F32)
        g = gu[:, :w]
        gu_ref[...] = gu.astype(BF16)
        act_ref[...] = (g * _sigmoid(g) * gu[:, w:]).astype(BF16)

    out_shape = [jax.ShapeDtypeStruct((m, 2 * FFN_H), BF16), jax.ShapeDtypeStruct((m, FFN_H), BF16)]
    cp = _params(("parallel", "parallel"), 56 << 20)
    if chip_idx is None:
        return pl.pallas_call(
            body, name=name, grid=(N_CHIPS, m // tm),
            in_specs=[pl.BlockSpec((tm, k), lambda j, i: (i, 0)), pl.BlockSpec((k, 2 * w), lambda j, i: (0, j))],
            out_specs=[pl.BlockSpec((tm, 2 * w), lambda j, i: (i, j)), pl.BlockSpec((tm, w), lambda j, i: (i, j))],
            out_shape=out_shape, compiler_params=cp,
        )(a, b)
    first = done is None
    tile = (lambda j, c: c[0]) if first else (lambda j, c: (c[0] + 1 + j) % N_CHIPS)
    in_specs = [pl.BlockSpec((tm, k), lambda j, i, c: (i, 0)),
                pl.BlockSpec((k, 2 * w), (lambda j, i, c: (0, 0)) if first else (lambda j, i, c: (0, tile(j, c))))]
    return pl.pallas_call(
        body, name=name,
        grid_spec=pltpu.PrefetchScalarGridSpec(
            num_scalar_prefetch=1, grid=(1 if first else N_CHIPS - 1, m // tm),
            in_specs=in_specs + ([] if first else [pl.BlockSpec(memory_space=pl.ANY)] * 2),
            out_specs=[pl.BlockSpec((tm, 2 * w), lambda j, i, c: (i, tile(j, c))),
                       pl.BlockSpec((tm, w), lambda j, i, c: (i, tile(j, c)))]),
        out_shape=out_shape, compiler_params=cp,
        input_output_aliases={} if first else {3: 0, 4: 1},
    )(chip_idx, a, b, *(() if first else done))


def _mm_swiglu_bwd(dr, wd, gu, name):
    m, k = dr.shape
    w = SHARD_H
    tm = _divtile(m, 512)

    def body(dr_ref, wd_ref, gu_ref, o_ref):
        d = lax.dot_general(dr_ref[...], wd_ref[...], NT_DIMS, preferred_element_type=F32)
        g = gu_ref[:, :w].astype(F32)
        u = gu_ref[:, w:].astype(F32)
        s = _sigmoid(g)
        o_ref[:, :w] = (d * u * (s * (1.0 + g * (1.0 - s)))).astype(BF16)
        o_ref[:, w:] = (d * (g * s)).astype(BF16)

    return pl.pallas_call(
        body, name=name, grid=(N_CHIPS, m // tm),
        in_specs=[pl.BlockSpec((tm, k), lambda j, i: (i, 0)), pl.BlockSpec((w, k), lambda j, i: (j, 0)),
                  pl.BlockSpec((tm, 2 * w), lambda j, i: (i, j))],
        out_specs=pl.BlockSpec((tm, 2 * w), lambda j, i: (i, j)),
        out_shape=jax.ShapeDtypeStruct((m, 2 * FFN_H), BF16),
        compiler_params=_params(("parallel", "parallel"), 48 << 20),
    )(dr, wd, gu)


def _ln_fwd(base, f, g, b, c, name, target=None):
    t = base.shape[0]
    tt = _divtile(t, 256)
    with_loss = target is not None

    def body(*refs):
        if with_loss:
            base_ref, f_ref, g_ref, b_ref, tg_ref, h_ref, hb_ref, xh_ref, rs_ref, dh_ref, loss_ref = refs
        else:
            base_ref, f_ref, g_ref, b_ref, h_ref, hb_ref, xh_ref, rs_ref = refs
        r = ALPHA * base_ref[...] + c * f_ref[...]
        mu = jnp.mean(r, axis=-1, keepdims=True)
        xc = r - mu
        var = jnp.mean(xc * xc, axis=-1, keepdims=True)
        rstd = lax.rsqrt(var + LN_EPS)
        xh = xc * rstd
        h = xh * g_ref[...] + b_ref[...]
        h_ref[...] = h
        hb_ref[...] = h.astype(BF16)
        xh_ref[...] = xh
        rs_ref[...] = rstd
        if with_loss:
            e = h - tg_ref[...]
            dh_ref[...] = e * (1.0 / D)
            part = 0.5 * jnp.sum(jnp.sum(e * e, axis=-1, keepdims=True) * (1.0 / D), axis=0, keepdims=True)

            @pl.when(pl.program_id(0) == 0)
            def _():
                loss_ref[...] = jnp.zeros_like(loss_ref)

            loss_ref[...] += part

    row = pl.BlockSpec((tt, D), lambda i: (i, 0))
    vec = pl.BlockSpec((1, D), lambda i: (0, 0))
    col = pl.BlockSpec((tt, 1), lambda i: (i, 0))
    in_specs = [row, row, vec, vec] + ([row] if with_loss else [])
    out_specs = [row, row, row, col] + ([row, pl.BlockSpec((1, 1), lambda i: (0, 0))] if with_loss else [])
    out_shape = [jax.ShapeDtypeStruct((t, D), F32), jax.ShapeDtypeStruct((t, D), BF16),
                 jax.ShapeDtypeStruct((t, D), F32), jax.ShapeDtypeStruct((t, 1), F32)]
    if with_loss:
        out_shape += [jax.ShapeDtypeStruct((t, D), F32), jax.ShapeDtypeStruct((1, 1), F32)]
    args = (base, f, g, b) + ((target,) if with_loss else ())
    return pl.pallas_call(
        body, name=name, grid=(t // tt,), in_specs=in_specs, out_specs=out_specs, out_shape=out_shape,
        compiler_params=_params(("arbitrary",) if with_loss else ("parallel",), 48 << 20),
    )(*args)


def _ln_bwd(dy, xh, rstd, g, c, name):
    t = dy.shape[0]
    tt = _divtile(t, 256)

    def body(dy_ref, xh_ref, rs_ref, g_ref, dr_ref, drb_ref, dg_ref, db_ref):
        dyv = dy_ref[...]
        xhv = xh_ref[...]
        dxh = dyv * g_ref[...]
        m1 = jnp.mean(dxh, axis=-1, keepdims=True)
        m2 = jnp.mean(dxh * xhv, axis=-1, keepdims=True)
        dr = rs_ref[...] * (dxh - m1 - xhv * m2)
        dr_ref[...] = dr
        drb_ref[...] = (c * dr).astype(BF16)

        @pl.when(pl.program_id(0) == 0)
        def _():
            dg_ref[...] = jnp.zeros_like(dg_ref)
            db_ref[...] = jnp.zeros_like(db_ref)

        dg_ref[...] += jnp.sum(dyv * xhv, axis=0, keepdims=True)
        db_ref[...] += jnp.sum(dyv, axis=0, keepdims=True)

    row = pl.BlockSpec((tt, D), lambda i: (i, 0))
    vec = pl.BlockSpec((1, D), lambda i: (0, 0))
    col = pl.BlockSpec((tt, 1), lambda i: (i, 0))
    return pl.pallas_call(
        body, name=name, grid=(t // tt,), in_specs=[row, row, col, vec], out_specs=[row, row, vec, vec],
        out_shape=[jax.ShapeDtypeStruct((t, D), F32), jax.ShapeDtypeStruct((t, D), BF16),
                   jax.ShapeDtypeStruct((1, D), F32), jax.ShapeDtypeStruct((1, D), F32)],
        compiler_params=_params(("arbitrary",), 40 << 20),
    )(dy, xh, rstd, g)


DT_BLK = SEGS["dt"][2] // 128


def _dt_prep(proj, bias128, alog128):
    t = proj.shape[0]
    tt = _divtile(t, 1024)

    def body(p_ref, bias_ref, alog_ref, dt_ref, adt_ref):
        dtv = jax.nn.softplus(p_ref[...] + bias_ref[...])
        dt_ref[...] = dtv
        adt_ref[...] = dtv * (-jnp.exp(alog_ref[...]))

    blk = pl.BlockSpec((tt, 128), lambda i: (i, 0))
    vec = pl.BlockSpec((1, 128), lambda i: (0, 0))
    return pl.pallas_call(
        body, name="dt_prep", grid=(t // tt,),
        in_specs=[pl.BlockSpec((tt, 128), lambda i: (i, DT_BLK)), vec, vec], out_specs=[blk, blk],
        out_shape=[jax.ShapeDtypeStruct((t, 128), F32)] * 2,
        compiler_params=_params(("parallel",), 16 << 20),
    )(proj, bias128, alog128)


def _dt_bwd(dadt, dxdx, proj, bias128, alog128):
    t = proj.shape[0]
    tt = _divtile(t, 1024)

    def body(dadt_ref, dxdx_ref, p_ref, bias_ref, alog_ref, o_ref, dbias_ref, dalog_ref):
        pre = p_ref[...] + bias_ref[...]
        dtv = jax.nn.softplus(pre)
        a = -jnp.exp(alog_ref[...])
        ddt = a * dadt_ref[...] + dxdx_ref[...]
        draw = ddt * _sigmoid(pre)
        o_ref[...] = draw.astype(BF16)

        @pl.when(pl.program_id(0) == 0)
        def _():
            dbias_ref[...] = jnp.zeros_like(dbias_ref)
            dalog_ref[...] = jnp.zeros_like(dalog_ref)

        dbias_ref[...] += jnp.sum(draw, axis=0, keepdims=True)
        dalog_ref[...] += jnp.sum(dadt_ref[...] * dtv * a, axis=0, keepdims=True)

    blk = pl.BlockSpec((tt, 128), lambda i: (i, 0))
    vec = pl.BlockSpec((1, 128), lambda i: (0, 0))
    return pl.pallas_call(
        body, name="dt_bwd", grid=(t // tt,),
        in_specs=[blk, blk, pl.BlockSpec((tt, 128), lambda i: (i, DT_BLK)), vec, vec],
        out_specs=[blk, vec, vec],
        out_shape=[jax.ShapeDtypeStruct((t, 128), BF16), jax.ShapeDtypeStruct((1, 128), F32),
                   jax.ShapeDtypeStruct((1, 128), F32)],
        compiler_params=_params(("arbitrary",), 16 << 20),
    )(dadt, dxdx, proj, bias128, alog128)


CONV_CB = 128
CONV_TT = 2048


def _shift_down(cur, prev8, s):
    if s == 0:
        return cur
    rolled = pltpu.roll(cur, s, 0)
    head = pltpu.roll(prev8, s, 0)
    r8 = lax.broadcasted_iota(jnp.int32, (8, 1), 0)
    top = jnp.where(r8 < s, head, rolled[:8])
    return jnp.concatenate([top, rolled[8:]], axis=0)


def _shift_up(cur, next8, s):
    if s == 0:
        return cur
    n = cur.shape[0]
    rolled = pltpu.roll(cur, n - s, 0)
    tail = pltpu.roll(next8, 8 - s, 0)
    r8 = lax.broadcasted_iota(jnp.int32, (8, 1), 0)
    bot = jnp.where(r8 >= 8 - s, tail, rolled[n - 8:])
    return jnp.concatenate([rolled[:n - 8], bot], axis=0)


def _conv_fwd(proj, conv_w, conv_b):
    t = proj.shape[0]
    tt = _divtile(t, CONV_TT)
    base = SEGS["xbc"][2] // CONV_CB
    r8 = tt // 8

    def body(u_ref, up_ref, w_ref, b_ref, o_ref):
        cur = u_ref[...]
        prev8 = jnp.where(pl.program_id(1) > 0, up_ref[...], 0.0)
        acc = b_ref[...] + w_ref[3:4, :] * cur
        for k in range(3):
            acc = acc + w_ref[k:k + 1, :] * _shift_down(cur, prev8, 3 - k)
        o_ref[...] = acc * _sigmoid(acc)

    return pl.pallas_call(
        body, name="conv_fwd", grid=(XBC // CONV_CB, t // tt),
        in_specs=[pl.BlockSpec((tt, CONV_CB), lambda c, i: (i, base + c)),
                  pl.BlockSpec((8, CONV_CB), lambda c, i: (jnp.maximum(i * r8 - 1, 0), base + c)),
                  pl.BlockSpec((4, CONV_CB), lambda c, i: (0, c)),
                  pl.BlockSpec((1, CONV_CB), lambda c, i: (0, c))],
        out_specs=pl.BlockSpec((tt, CONV_CB), lambda c, i: (i, c)),
        out_shape=jax.ShapeDtypeStruct((t, XBC), F32),
        compiler_params=_params(("parallel", "parallel"), 24 << 20),
    )(proj, proj, conv_w, conv_b)


def _conv_bwd(proj, dout, conv_w, conv_b, col0, width, name, dproj, skip=None):
    t = proj.shape[0]
    tt = _divtile(t, CONV_TT)
    nt = t // tt
    base = SEGS["xbc"][2] // CONV_CB + col0 // CONV_CB
    wb = col0 // CONV_CB
    r8 = tt // 8
    has_skip = skip is not None

    def body(*refs):
        if has_skip:
            u_ref, up_ref, d_ref, w_ref, b_ref, _, sk_ref, skw_ref, du_ref, dw_ref, db_ref, nx_ref = refs
        else:
            u_ref, up_ref, d_ref, w_ref, b_ref, _, du_ref, dw_ref, db_ref, nx_ref = refs
        i = pl.program_id(1)
        cur = u_ref[...]
        prev8 = jnp.where(i < nt - 1, up_ref[...], 0.0)
        sh = [_shift_down(cur, prev8, 3 - k) for k in range(3)] + [cur]
        pre = b_ref[...]
        for k in range(4):
            pre = pre + w_ref[k:k + 1, :] * sh[k]
        sg = _sigmoid(pre)
        dout_v = d_ref[...]
        if has_skip:
            dout_v = dout_v + sk_ref[...] * skw_ref[...]
        dpre = dout_v * (sg * (1.0 + pre * (1.0 - sg)))

        @pl.when(i == 0)
        def _():
            nx_ref[...] = jnp.zeros_like(nx_ref)
            dw_ref[...] = jnp.zeros_like(dw_ref)
            db_ref[...] = jnp.zeros_like(db_ref)

        next8 = nx_ref[...]
        du = w_ref[3:4, :] * dpre
        for s in range(1, 4):
            du = du + w_ref[3 - s:4 - s, :] * _shift_up(dpre, next8, s)
        du_ref[...] = du.astype(BF16)
        nx_ref[...] = dpre[:8]
        rows = [jnp.sum(dpre * sh[k], axis=0, keepdims=True) for k in range(4)]
        dw_ref[...] += jnp.concatenate(rows + [jnp.zeros((4, CONV_CB), F32)], axis=0)
        db_ref[...] += jnp.sum(dpre, axis=0, keepdims=True)

    rev = lambda c, i: (nt - 1 - i, c)
    p_in, p_out, p_shape = _dproj_piece(t, tt, CONV_CB, lambda c, i: (nt - 1 - i, base + c))
    in_specs = [pl.BlockSpec((tt, CONV_CB), lambda c, i: (nt - 1 - i, base + c)),
                pl.BlockSpec((8, CONV_CB), lambda c, i: (jnp.maximum((nt - 1 - i) * r8 - 1, 0), base + c)),
                pl.BlockSpec((tt, CONV_CB), rev),
                pl.BlockSpec((4, CONV_CB), lambda c, i: (0, wb + c)),
                pl.BlockSpec((1, CONV_CB), lambda c, i: (0, wb + c)), p_in]
    args = [proj, proj, dout, conv_w, conv_b, dproj]
    if has_skip:
        in_specs += [pl.BlockSpec((tt, CONV_CB), rev), pl.BlockSpec((1, CONV_CB), lambda c, i: (0, c))]
        args += [skip[0], skip[1]]
    return pl.pallas_call(
        body, name=name, grid=(width // CONV_CB, nt),
        in_specs=in_specs,
        out_specs=[p_out, pl.BlockSpec((8, CONV_CB), lambda c, i: (0, c)),
                   pl.BlockSpec((1, CONV_CB), lambda c, i: (0, c))],
        out_shape=[p_shape, jax.ShapeDtypeStruct((8, width), F32), jax.ShapeDtypeStruct((1, width), F32)],
        input_output_aliases={5: 0},
        scratch_shapes=[pltpu.VMEM((8, CONV_CB), F32)],
        compiler_params=_params(("parallel", "arbitrary"), 32 << 20),
    )(*args)


GW = SSD_R * SSD_P


def _expand8(v, passes=2):
    r = v.shape[0]
    if r < 8:
        v = jnp.broadcast_to(v, (8, SSD_R))
    ri = lax.broadcasted_iota(jnp.int32, (SSD_R, GW), 0)
    ci = lax.broadcasted_iota(jnp.int32, (SSD_R, GW), 1)
    spread = jnp.where((ci >= ri * SSD_P) & (ci < (ri + 1) * SSD_P), 1.0, 0.0)
    return _dot01(v, spread, passes)[:r]


def _head_pair_split(tile):
    first = lax.broadcasted_iota(jnp.int32, (1, 2 * SSD_P), 1) < SSD_P
    return jnp.where(first, tile, 0.0), jnp.where(first, 0.0, tile)


def _sel(rows, group):
    ri = lax.broadcasted_iota(jnp.int32, (rows, rows // group), 0)
    ci = lax.broadcasted_iota(jnp.int32, (rows, rows // group), 1)
    lo = ci * group
    return jnp.where((ri >= lo) & (ri < lo + group), 1.0, 0.0).astype(F32)


def _dot01(lhs, rhs, passes, split_lhs=True, dims=((1,), (0,))):
    val, m01 = (lhs, rhs) if split_lhs else (rhs, lhs)
    m01 = m01.astype(BF16)
    out = None
    for p in range(passes):
        piece = val.astype(BF16)
        ops = (piece, m01) if split_lhs else (m01, piece)
        d = lax.dot_general(ops[0], ops[1], (dims, ((), ())), preferred_element_type=F32)
        out = d if out is None else out + d
        if p + 1 < passes:
            val = val - piece.astype(F32)
    return out


def _ssd_chunk_terms(adt):
    li = lax.broadcasted_iota(jnp.int32, (CHUNK, CHUNK), 0)
    si = lax.broadcasted_iota(jnp.int32, (CHUNK, CHUNK), 1)
    causal = li >= si
    a_cs = _dot01(jnp.where(causal, 1.0, 0.0), adt, 3, split_lhs=False)
    a_cs_t = _dot01(adt, jnp.where(li <= si, 1.0, 0.0), 3, dims=((0,), (0,)))
    return a_cs, a_cs_t, causal


def _ssd_fwd(xc, dt3, adt3):
    t = xc.shape[0]
    nc = t // CHUNK

    gs = 2

    def body(xs_ref, b_ref, c_ref, dt_ref, adt_ref, y_ref, hp_ref, h_ref):
        @pl.when(pl.program_id(1) == 0)
        def _():
            h_ref[...] = jnp.zeros_like(h_ref)

        for gg in range(gs):
            a_cs, a_cs_t, causal = _ssd_chunk_terms(adt_ref[gg])
            a_last = a_cs[CHUNK - 1:CHUNK, :]
            h = h_ref[gg]
            hp_ref[gg, 0] = h
            xd = xs_ref[:, GW * gg:GW * (gg + 1)] * _expand8(dt_ref[gg])
            bb = b_ref[:, SSD_N * gg:SSD_N * (gg + 1)].astype(BF16)
            cbf = c_ref[:, SSD_N * gg:SSD_N * (gg + 1)].astype(BF16)
            cb = lax.dot_general(cbf, bb, (((1,), (1,)), ((), ())), preferred_element_type=F32)
            yoff = jnp.dot(cbf, h.astype(BF16), preferred_element_type=F32) * _expand8(jnp.exp(a_cs))
            for q in range(SSD_R // 2):
                lmats = []
                for r in (2 * q, 2 * q + 1):
                    seg = jnp.exp(jnp.where(causal, a_cs[:, r:r + 1] - a_cs_t[r:r + 1, :], -jnp.inf))
                    lmats.append((cb * seg).astype(BF16))
                tile = slice(2 * SSD_P * q, 2 * SSD_P * (q + 1))
                xa, xb = _head_pair_split(xd[:, tile])
                y_ref[:, GW * gg + 2 * SSD_P * q:GW * gg + 2 * SSD_P * (q + 1)] = (
                    jnp.dot(jnp.concatenate(lmats, axis=1), jnp.concatenate([xa, xb], axis=0).astype(BF16),
                            preferred_element_type=F32) + yoff[:, tile])
            xdd = (xd * _expand8(jnp.exp(a_last - a_cs))).astype(BF16)
            h_ref[gg] = _expand8(jnp.exp(a_last), 3) * h + lax.dot_general(
                bb, xdd, (((0,), (0,)), ((), ())), preferred_element_type=F32)

    nb = SSD_INNER // (gs * SSD_N)
    return pl.pallas_call(
        body, name="ssd_fwd", grid=(SSD_G // gs, nc),
        in_specs=[pl.BlockSpec((CHUNK, gs * GW), lambda g, c: (c, g)),
                  pl.BlockSpec((CHUNK, gs * SSD_N), lambda g, c: (c, nb + g)),
                  pl.BlockSpec((CHUNK, gs * SSD_N), lambda g, c: (c, nb + SSD_G // gs + g)),
                  pl.BlockSpec((gs, CHUNK, SSD_R), lambda g, c: (g, c, 0)),
                  pl.BlockSpec((gs, CHUNK, SSD_R), lambda g, c: (g, c, 0))],
        out_specs=[pl.BlockSpec((CHUNK, gs * GW), lambda g, c: (c, g)),
                   pl.BlockSpec((gs, 1, SSD_N, GW), lambda g, c: (g, c, 0, 0))],
        out_shape=[jax.ShapeDtypeStruct((t, SSD_INNER), F32), jax.ShapeDtypeStruct((SSD_G, nc, SSD_N, GW), F32)],
        scratch_shapes=[pltpu.VMEM((gs, SSD_N, GW), F32)],
        compiler_params=_params(("parallel", "arbitrary"), 32 << 20),
    )(xc, xc, xc, dt3, adt3)


def _ssd_bwd(xc, dt3, adt3, hprev, dy):
    t = xc.shape[0]
    nc = t // CHUNK

    gs = 2

    def body(xs_ref, b_ref, c_ref, dt_ref, adt_ref, hp_ref, dy_ref,
             dx_ref, db_ref, dc_ref, dadt_ref, dxdx_ref, dh_ref):
        @pl.when(pl.program_id(1) == 0)
        def _():
            dh_ref[...] = jnp.zeros_like(dh_ref)

        for gg in range(gs):
            wide = slice(GW * gg, GW * (gg + 1))
            narrow = slice(SSD_N * gg, SSD_N * (gg + 1))
            dx, db, dc, dadt, dxdx, dh_new = group_bwd(
                xs_ref[:, wide], b_ref[:, narrow], c_ref[:, narrow], dt_ref[gg], adt_ref[gg], hp_ref[gg, 0],
                dy_ref[:, wide], dh_ref[gg])
            dx_ref[:, wide] = dx
            db_ref[:, narrow] = db
            dc_ref[:, narrow] = dc
            dadt_ref[gg] = dadt
            dxdx_ref[gg] = dxdx
            dh_ref[gg] = dh_new

    def group_bwd(xs, b, c, dt, adt, hp, dyv, dh):
        a_cs, a_cs_t, causal = _ssd_chunk_terms(adt)
        a_last = a_cs[CHUNK - 1:CHUNK, :]
        e_last = jnp.exp(a_last)
        ex = _expand8(jnp.exp(a_cs))
        dtex = _expand8(jnp.exp(a_last - a_cs))
        dtx = _expand8(dt)
        sel = _sel(GW, SSD_P)
        seg8 = lambda v: _dot01(v, sel, 2)

        xd = xs * dtx
        xdd = xd * dtex
        bb = b.astype(BF16)
        cbf = c.astype(BF16)
        hpb = hp.astype(BF16)
        dhb = dh.astype(BF16)
        xdb = xd.astype(BF16)
        cb = lax.dot_general(cbf, bb, (((1,), (1,)), ((), ())), preferred_element_type=F32)
        dye = (dyv * ex).astype(BF16)
        yoff = jnp.dot(cbf, hpb, preferred_element_type=F32) * ex
        dc = lax.dot_general(dye, hpb, (((1,), (1,)), ((), ())), preferred_element_type=F32)
        bdh = jnp.dot(bb, dhb, preferred_element_type=F32)
        db = lax.dot_general(xdd.astype(BF16), dhb, (((1,), (1,)), ((), ())), preferred_element_type=F32)
        dxd_state = bdh * dtex
        q_terms = xdd * bdh
        d_a = seg8(dyv * yoff - q_terms)
        d_a_last = seg8(jnp.sum(q_terms, axis=0, keepdims=True)
                        + _expand8(e_last, 3) * jnp.sum(hp * dh, axis=0, keepdims=True))
        dh_new = (lax.dot_general(cbf, dye, (((0,), (0,)), ((), ())), preferred_element_type=F32)
                  + _expand8(e_last, 3) * dh)
        dcb = jnp.zeros((CHUNK, CHUNK), F32)
        w_all = []
        dxd_parts = []
        for q2 in range(SSD_R // 2):
            tile = slice(2 * SSD_P * q2, 2 * SSD_P * (q2 + 1))
            dy_pair = [part.astype(BF16) for part in _head_pair_split(dyv[:, tile])]
            lmats = []
            for k, r in enumerate((2 * q2, 2 * q2 + 1)):
                seg = jnp.exp(jnp.where(causal, a_cs[:, r:r + 1] - a_cs_t[r:r + 1, :], -jnp.inf))
                lmat = cb * seg
                dm = lax.dot_general(dy_pair[k], xdb[:, tile], (((1,), (1,)), ((), ())), preferred_element_type=F32)
                dcb = dcb + dm * seg
                w_all.append(dm * lmat)
                lmats.append(lmat.astype(BF16))
            dxd_parts.append(lax.dot_general(jnp.concatenate(lmats, axis=0), jnp.concatenate(dy_pair, axis=0),
                                             (((0,), (0,)), ((), ())), preferred_element_type=F32))
        row_sums = _dot01(jnp.concatenate(w_all, axis=1), _sel(SSD_R * CHUNK, CHUNK), 2)
        cs_rows = jnp.concatenate([jnp.sum(wr, axis=0, keepdims=True) for wr in w_all], axis=0)
        col_sums = _dot01(cs_rows, _sel(SSD_R, 1), 3, dims=((0,), (0,)))
        d_a = d_a + row_sums - col_sums
        li = lax.broadcasted_iota(jnp.int32, (CHUNK, SSD_R), 0)
        d_a = d_a + jnp.where(li == CHUNK - 1, d_a_last, 0.0)
        l2 = lax.broadcasted_iota(jnp.int32, (CHUNK, CHUNK), 0)
        s2 = lax.broadcasted_iota(jnp.int32, (CHUNK, CHUNK), 1)
        dadt = _dot01(jnp.where(s2 >= l2, 1.0, 0.0), d_a, 3, split_lhs=False)
        dxd = dxd_state + jnp.concatenate(dxd_parts, axis=1)
        dcbb = dcb.astype(BF16)
        db = db + lax.dot_general(dcbb, cbf, (((0,), (0,)), ((), ())), preferred_element_type=F32)
        dc = dc + jnp.dot(dcbb, bb, preferred_element_type=F32)
        return dxd * dtx, db, dc, dadt, seg8(dxd * xs), dh_new

    nb = SSD_INNER // (gs * SSD_N)
    rc = lambda g, c: (nc - 1 - c, g)
    r3 = lambda g, c: (g, nc - 1 - c, 0)
    return pl.pallas_call(
        body, name="ssd_bwd", grid=(SSD_G // gs, nc),
        in_specs=[pl.BlockSpec((CHUNK, gs * GW), rc),
                  pl.BlockSpec((CHUNK, gs * SSD_N), lambda g, c: (nc - 1 - c, nb + g)),
                  pl.BlockSpec((CHUNK, gs * SSD_N), lambda g, c: (nc - 1 - c, nb + SSD_G // gs + g)),
                  pl.BlockSpec((gs, CHUNK, SSD_R), r3),
                  pl.BlockSpec((gs, CHUNK, SSD_R), r3),
                  pl.BlockSpec((gs, 1, SSD_N, GW), lambda g, c: (g, nc - 1 - c, 0, 0)),
                  pl.BlockSpec((CHUNK, gs * GW), rc)],
        out_specs=[pl.BlockSpec((CHUNK, gs * GW), rc),
                   pl.BlockSpec((CHUNK, gs * SSD_N), rc),
                   pl.BlockSpec((CHUNK, gs * SSD_N), rc),
                   pl.BlockSpec((gs, CHUNK, SSD_R), r3),
                   pl.BlockSpec((gs, CHUNK, SSD_R), r3)],
        out_shape=[jax.ShapeDtypeStruct((t, SSD_INNER), F32),
                   jax.ShapeDtypeStruct((t, SSD_G * SSD_N), F32),
                   jax.ShapeDtypeStruct((t, SSD_G * SSD_N), F32),
                   jax.ShapeDtypeStruct((SSD_G, t, SSD_R), F32),
                   jax.ShapeDtypeStruct((SSD_G, t, SSD_R), F32)],
        scratch_shapes=[pltpu.VMEM((gs, SSD_N, GW), F32)],
        compiler_params=_params(("parallel", "arbitrary"), 48 << 20),
    )(xc, xc, xc, dt3, adt3, hprev, dy)


def _gated_norm_fwd(y, xc, proj, dexp, ng):
    t = y.shape[0]
    tt = _divtile(t, 256)

    def body(y_ref, x_ref, z_ref, d_ref, g_ref, o_ref):
        z = z_ref[...]
        y2 = (y_ref[...] + d_ref[...] * x_ref[...]) * (z * _sigmoid(z))
        for gi in range(SSD_G):
            sl = slice(GW * gi, GW * (gi + 1))
            seg = y2[:, sl]
            rinv = lax.rsqrt(jnp.mean(seg * seg, axis=-1, keepdims=True) + RMS_EPS)
            o_ref[:, sl] = (seg * rinv * g_ref[:, sl]).astype(BF16)

    row = pl.BlockSpec((tt, SSD_INNER), lambda i: (i, 0))
    vec = pl.BlockSpec((1, SSD_INNER), lambda i: (0, 0))
    return pl.pallas_call(
        body, name="gated_norm_fwd", grid=(t // tt,), in_specs=[row, row, row, vec, vec], out_specs=row,
        out_shape=jax.ShapeDtypeStruct((t, SSD_INNER), BF16),
        compiler_params=_params(("parallel",), 48 << 20),
    )(y, xc, proj, dexp, ng)


def _gated_norm_bwd(dout, y, xc, proj, dexp, ng, dproj):
    t = y.shape[0]
    tt = _divtile(t, 128)

    def body(do_ref, y_ref, x_ref, z_ref, d_ref, g_ref, _, dz_ref, dy_ref, dg_ref, dd_ref):
        z = z_ref[...]
        sg = _sigmoid(z)
        sz = z * sg
        xs = x_ref[...]
        y1 = y_ref[...] + d_ref[...] * xs
        y2 = y1 * sz
        dov = do_ref[...]

        @pl.when(pl.program_id(0) == 0)
        def _():
            dg_ref[...] = jnp.zeros_like(dg_ref)
            dd_ref[...] = jnp.zeros_like(dd_ref)

        for gi in range(SSD_G):
            sl = slice(GW * gi, GW * (gi + 1))
            seg = y2[:, sl]
            rinv = lax.rsqrt(jnp.mean(seg * seg, axis=-1, keepdims=True) + RMS_EPS)
            yn = seg * rinv
            dsl = dov[:, sl]
            dg_ref[:, sl] += jnp.sum(dsl * yn, axis=0, keepdims=True)
            dyn = dsl * g_ref[:, sl]
            dy2 = rinv * (dyn - yn * jnp.mean(dyn * yn, axis=-1, keepdims=True))
            dz_ref[:, sl] = (dy2 * y1[:, sl] * (sg[:, sl] * (1.0 + z[:, sl] * (1.0 - sg[:, sl])))).astype(BF16)
            dy1 = dy2 * sz[:, sl]
            dy_ref[:, sl] = dy1
            dd_ref[:, sl] += jnp.sum(dy1 * xs[:, sl], axis=0, keepdims=True)

    row = pl.BlockSpec((tt, SSD_INNER), lambda i: (i, 0))
    vec = pl.BlockSpec((1, SSD_INNER), lambda i: (0, 0))
    p_in, p_out, p_shape = _dproj_piece(t, tt, SSD_INNER, lambda i: (i, 0))
    return pl.pallas_call(
        body, name="gated_norm_bwd", grid=(t // tt,), in_specs=[row, row, row, row, vec, vec, p_in],
        out_specs=[p_out, row, vec, vec],
        out_shape=[p_shape, jax.ShapeDtypeStruct((t, SSD_INNER), F32),
                   jax.ShapeDtypeStruct((1, SSD_INNER), F32), jax.ShapeDtypeStruct((1, SSD_INNER), F32)],
        input_output_aliases={6: 0},
        compiler_params=_params(("arbitrary",), 48 << 20),
    )(dout, y, xc, proj, dexp, ng, dproj)


def _fold_heads(v, name):
    def body(v_ref, o_ref):
        ri = lax.broadcasted_iota(jnp.int32, (SSD_INNER, 128), 0)
        ci = lax.broadcasted_iota(jnp.int32, (SSD_INNER, 128), 1)
        fold = jnp.where((ri >= ci * SSD_P) & (ri < (ci + 1) * SSD_P), 1.0, 0.0).astype(F32)
        o_ref[...] = jnp.dot(v_ref[...], fold, preferred_element_type=F32, precision=HI)

    return pl.pallas_call(body, name=name, out_shape=jax.ShapeDtypeStruct((1, 128), F32))(v)


Q_BLK = SEGS["q"][2] // QW
K_BLK = SEGS["k"][2] // KVW
V_BLK = SEGS["v"][2] // KVW


def _rope_tables(pos_ref, invf_ref, width):
    ang = pos_ref[...] * invf_ref[...]
    lane = lax.broadcasted_iota(jnp.int32, (1, 128), 1)
    sign = jnp.where((lane % HD) < (HD // 2), -1.0, 1.0)
    cos = jnp.tile(jnp.cos(ang), (1, width // 128))
    sin = jnp.tile(sign * jnp.sin(ang), (1, width // 128))
    first = (lax.broadcasted_iota(jnp.int32, (1, width), 1) % HD) < (HD // 2)
    return cos, sin, first


def _rot_half(u, first):
    w = u.shape[1]
    return jnp.where(first, pltpu.roll(u, w - HD // 2, 1), pltpu.roll(u, HD // 2, 1))


def _rope_fwd(proj, pos, invf):
    t = proj.shape[0]
    tt = _divtile(t, 512)

    def body(q_ref, k_ref, pos_ref, invf_ref, qo_ref, ko_ref):
        cos, sin, first = _rope_tables(pos_ref, invf_ref, QW)
        q = q_ref[...]
        qr = q * cos + _rot_half(q, first) * sin
        for p in range(QW // 128):
            qo_ref[128 * p:128 * (p + 1), :] = qr[:, 128 * p:128 * (p + 1)].T.astype(BF16)
        k = k_ref[...]
        ko_ref[...] = (k * cos[:, :KVW] + _rot_half(k, first[:, :KVW]) * sin[:, :KVW]).astype(BF16)

    return pl.pallas_call(
        body, name="rope_fwd", grid=(t // tt,),
        in_specs=[pl.BlockSpec((tt, QW), lambda i: (i, Q_BLK)), pl.BlockSpec((tt, KVW), lambda i: (i, K_BLK)),
                  pl.BlockSpec((tt, 1), lambda i: (i, 0)), pl.BlockSpec((1, 128), lambda i: (0, 0))],
        out_specs=[pl.BlockSpec((QW, tt), lambda i: (0, i)), pl.BlockSpec((tt, KVW), lambda i: (i, 0))],
        out_shape=[jax.ShapeDtypeStruct((QW, t), BF16), jax.ShapeDtypeStruct((t, KVW), BF16)],
        compiler_params=_params(("parallel",), 40 << 20),
    )(proj, proj, pos, invf)


def _rope_bwd(dqt, dk, pos, invf, dproj):
    t = dk.shape[0]
    tt = _divtile(t, 512)

    def body(dq_ref, dk_ref, pos_ref, invf_ref, _, qo_ref, ko_ref):
        cos, sin, first = _rope_tables(pos_ref, invf_ref, QW)
        q = jnp.concatenate([dq_ref[128 * p:128 * (p + 1), :].T for p in range(QW // 128)], axis=1)
        qo_ref[...] = (q * cos + _rot_half(q * sin, first)).astype(BF16)
        k = dk_ref[...]
        ko_ref[...] = (k * cos[:, :KVW] + _rot_half(k * sin[:, :KVW], first[:, :KVW])).astype(BF16)

    p_in, p_out, p_shape = _dproj_piece(t, tt, QW, lambda i: (i, Q_BLK))
    return pl.pallas_call(
        body, name="rope_bwd", grid=(t // tt,),
        in_specs=[pl.BlockSpec((QW, tt), lambda i: (0, i)), pl.BlockSpec((tt, KVW), lambda i: (i, 0)),
                  pl.BlockSpec((tt, 1), lambda i: (i, 0)), pl.BlockSpec((1, 128), lambda i: (0, 0)), p_in],
        out_specs=[p_out, pl.BlockSpec((tt, KVW), lambda i: (i, 0))],
        out_shape=[p_shape, jax.ShapeDtypeStruct((t, KVW), BF16)],
        input_output_aliases={4: 0},
        compiler_params=_params(("parallel",), 40 << 20),
    )(dqt, dk, pos, invf, dproj)


def _place_cols(piece, dproj, col_blk, name):
    t, w = piece.shape
    tt = _divtile(t, 1024)

    def body(p_ref, _, o_ref):
        o_ref[...] = p_ref[...]

    p_in, p_out, p_shape = _dproj_piece(t, tt, w, lambda i: (i, col_blk))
    return pl.pallas_call(
        body, name=name, grid=(t // tt,),
        in_specs=[pl.BlockSpec((tt, w), lambda i: (i, 0)), p_in], out_specs=p_out, out_shape=p_shape,
        input_output_aliases={1: 0},
        compiler_params=_params(("parallel",), 16 << 20),
    )(piece, dproj)


GQ = NQ // NKV
NT_DIMS = (((1,), (1,)), ((), ()))
TN_DIMS = (((0,), (0,)), ((), ()))


def _attn_heads(ref, j, dtype=None):
    out = jnp.concatenate([ref[HD * h:HD * (h + 1), :] for h in range(j * GQ, (j + 1) * GQ)], axis=1)
    return out if dtype is None else out.astype(dtype)


def _attn_sink_row(s_ref, j):
    return jnp.concatenate([jnp.broadcast_to(s_ref[:, h:h + 1], (1, WINDOW)) for h in range(j * GQ, (j + 1) * GQ)],
                           axis=1)


def _attn_mask(n):
    kr = lax.broadcasted_iota(jnp.int32, (2 * WINDOW, GQ * WINDOW), 0)
    qi = lax.broadcasted_iota(jnp.int32, (2 * WINDOW, GQ * WINDOW), 1) % WINDOW
    return (kr > qi) & (kr <= qi + WINDOW) & ((n > 0) | (kr >= WINDOW))


def _attn_probs(qgt, kk, sink, mask):
    s = jnp.where(mask, jnp.dot(kk, qgt, preferred_element_type=F32) * (HD ** -0.5), -jnp.inf)
    m = jnp.maximum(jnp.max(s, axis=0, keepdims=True), sink)
    p = jnp.exp(s - m)
    ps = jnp.exp(sink - m)
    inv = 1.0 / (jnp.sum(p, axis=0, keepdims=True) + ps)
    return p * inv, ps * inv


def _attn_fwd(qt, kr, proj, sinks):
    t = kr.shape[0]
    nb = t // WINDOW

    def body(q_ref, kc_ref, kp_ref, vc_ref, vp_ref, s_ref, o_ref):
        mask = _attn_mask(pl.program_id(0))
        for j in range(NKV):
            ks = slice(HD * j, HD * (j + 1))
            kk = jnp.concatenate([kp_ref[:, ks], kc_ref[:, ks]], axis=0)
            vv = jnp.concatenate([vp_ref[:, ks], vc_ref[:, ks]], axis=0).astype(BF16)
            pn, _ = _attn_probs(_attn_heads(q_ref, j), kk, _attn_sink_row(s_ref, j), mask)
            ot = lax.dot_general(vv, pn.astype(BF16), TN_DIMS, preferred_element_type=F32).astype(BF16)
            for g in range(GQ):
                h = j * GQ + g
                o_ref[HD * h:HD * (h + 1), :] = ot[:, WINDOW * g:WINDOW * (g + 1)]

    prev = lambda n: (jnp.maximum(n - 1, 0), 0)
    return pl.pallas_call(
        body, name="attn_fwd", grid=(nb,),
        in_specs=[pl.BlockSpec((QW, WINDOW), lambda n: (0, n)),
                  pl.BlockSpec((WINDOW, KVW), lambda n: (n, 0)), pl.BlockSpec((WINDOW, KVW), prev),
                  pl.BlockSpec((WINDOW, KVW), lambda n: (n, V_BLK)),
                  pl.BlockSpec((WINDOW, KVW), lambda n: (jnp.maximum(n - 1, 0), V_BLK)),
                  pl.BlockSpec((1, 128), lambda n: (0, 0))],
        out_specs=pl.BlockSpec((QW, WINDOW), lambda n: (0, n)),
        out_shape=jax.ShapeDtypeStruct((QW, t), BF16),
        compiler_params=_params(("parallel",), 24 << 20),
    )(qt, kr, kr, proj, proj, sinks)


def _attn_bwd(qt, kr, proj, sinks, dot_, dproj):
    t = kr.shape[0]
    nb = t // WINDOW

    def body(q_ref, kc_ref, kp_ref, vc_ref, vp_ref, s_ref, do_ref, _,
             dq_ref, dk_ref, dv_ref, ds_ref, dkc_ref, dvc_ref):
        i = pl.program_id(0)
        mask = _attn_mask(nb - 1 - i)

        @pl.when(i == 0)
        def _():
            dkc_ref[...] = jnp.zeros_like(dkc_ref)
            dvc_ref[...] = jnp.zeros_like(dvc_ref)
            ds_ref[...] = jnp.zeros_like(ds_ref)

        lane = lax.broadcasted_iota(jnp.int32, (1, 128), 1)
        ds_acc = jnp.zeros((1, 128), F32)
        for j in range(NKV):
            ks = slice(HD * j, HD * (j + 1))
            kk = jnp.concatenate([kp_ref[:, ks], kc_ref[:, ks]], axis=0)
            vv = jnp.concatenate([vp_ref[:, ks], vc_ref[:, ks]], axis=0).astype(BF16)
            qgt = _attn_heads(q_ref, j)
            pn, psn = _attn_probs(qgt, kk, _attn_sink_row(s_ref, j), mask)
            dogt = _attn_heads(do_ref, j)
            dp = jnp.dot(vv, dogt, preferred_element_type=F32)
            delta = jnp.sum(dp * pn, axis=0, keepdims=True)
            dsb = (pn * (dp - delta) * (HD ** -0.5)).astype(BF16)
            dsink = -psn * delta
            dqt = lax.dot_general(kk, dsb, TN_DIMS, preferred_element_type=F32)
            for g in range(GQ):
                h = j * GQ + g
                cols = slice(WINDOW * g, WINDOW * (g + 1))
                dq_ref[HD * h:HD * (h + 1), :] = dqt[:, cols]
                ds_acc = ds_acc + jnp.where(lane == h, jnp.sum(dsink[:, cols], axis=1, keepdims=True), 0.0)
            dkk = lax.dot_general(dsb, qgt, NT_DIMS, preferred_element_type=F32)
            dvv = lax.dot_general(pn.astype(BF16), dogt, NT_DIMS, preferred_element_type=F32)
            dk_ref[:, ks] = dkk[WINDOW:] + dkc_ref[:, ks]
            dv_ref[:, ks] = (dvv[WINDOW:] + dvc_ref[:, ks]).astype(BF16)
            dkc_ref[:, ks] = dkk[:WINDOW]
            dvc_ref[:, ks] = dvv[:WINDOW]
        ds_ref[...] += ds_acc

    cur = lambda i: (nb - 1 - i, 0)
    cur_t = lambda i: (0, nb - 1 - i)
    prev = lambda i: (jnp.maximum(nb - 2 - i, 0), 0)
    p_in, p_out, p_shape = _dproj_piece(t, WINDOW, KVW, lambda i: (nb - 1 - i, V_BLK))
    return pl.pallas_call(
        body, name="attn_bwd", grid=(nb,),
        in_specs=[pl.BlockSpec((QW, WINDOW), cur_t),
                  pl.BlockSpec((WINDOW, KVW), cur), pl.BlockSpec((WINDOW, KVW), prev),
                  pl.BlockSpec((WINDOW, KVW), lambda i: (nb - 1 - i, V_BLK)),
                  pl.BlockSpec((WINDOW, KVW), lambda i: (jnp.maximum(nb - 2 - i, 0), V_BLK)),
                  pl.BlockSpec((1, 128), lambda i: (0, 0)),
                  pl.BlockSpec((QW, WINDOW), cur_t), p_in],
        out_specs=[pl.BlockSpec((QW, WINDOW), cur_t), pl.BlockSpec((WINDOW, KVW), cur),
                   p_out, pl.BlockSpec((1, 128), lambda i: (0, 0))],
        out_shape=[jax.ShapeDtypeStruct((QW, t), F32), jax.ShapeDtypeStruct((t, KVW), F32),
                   p_shape, jax.ShapeDtypeStruct((1, 128), F32)],
        input_output_aliases={7: 2},
        scratch_shapes=[pltpu.VMEM((WINDOW, KVW), F32), pltpu.VMEM((WINDOW, KVW), F32)],
        compiler_params=_params(("arbitrary",), 32 << 20),
    )(qt, kr, kr, proj, proj, sinks, dot_, dproj)


GS_BLK = SEGS["gs"][2] // D
GA_BLK = SEGS["ga"][2] // D


def _merge_fwd(ys, ya, proj):
    t = ys.shape[0]
    tt = _divtile(t, 256)

    def body(ys_ref, ya_ref, gs_ref, ga_ref, o_ref):
        o_ref[...] = (_sigmoid(gs_ref[...]) * ys_ref[...] + _sigmoid(ga_ref[...]) * ya_ref[...]).astype(BF16)

    row = pl.BlockSpec((tt, D), lambda i: (i, 0))
    return pl.pallas_call(
        body, name="merge_fwd", grid=(t // tt,),
        in_specs=[row, row, pl.BlockSpec((tt, D), lambda i: (i, GS_BLK)), pl.BlockSpec((tt, D), lambda i: (i, GA_BLK))],
        out_specs=row, out_shape=jax.ShapeDtypeStruct((t, D), BF16),
        compiler_params=_params(("parallel",), 32 << 20),
    )(ys, ya, proj, proj)


def _dproj_piece(t, rows, width, index_map):
    return (pl.BlockSpec(memory_space=pl.ANY), pl.BlockSpec((rows, width), index_map),
            jax.ShapeDtypeStruct((t, PROJ_PAD), BF16))


def _merge_bwd(dm, ys, ya, proj, dproj):
    t = ys.shape[0]
    tt = _divtile(t, 256)

    def body(dm_ref, ys_ref, ya_ref, gs_ref, ga_ref, _, dys_ref, dya_ref, dg_ref):
        d = dm_ref[...]
        s = _sigmoid(gs_ref[...])
        a = _sigmoid(ga_ref[...])
        dys_ref[...] = (d * s).astype(BF16)
        dya_ref[...] = (d * a).astype(BF16)
        dg_ref[:, :D] = (d * ys_ref[...] * (s * (1.0 - s))).astype(BF16)
        dg_ref[:, D:] = (d * ya_ref[...] * (a * (1.0 - a))).astype(BF16)

    row = pl.BlockSpec((tt, D), lambda i: (i, 0))
    p_in, p_out, p_shape = _dproj_piece(t, tt, 2 * D, lambda i: (i, SEGS["gs"][2] // (2 * D)))
    return pl.pallas_call(
        body, name="merge_bwd", grid=(t // tt,),
        in_specs=[row, row, row, pl.BlockSpec((tt, D), lambda i: (i, GS_BLK)),
                  pl.BlockSpec((tt, D), lambda i: (i, GA_BLK)), p_in],
        out_specs=[row, row, p_out], out_shape=[jax.ShapeDtypeStruct((t, D), BF16)] * 2 + [p_shape],
        input_output_aliases={5: 2},
        compiler_params=_params(("parallel",), 40 << 20),
    )(dm, ys, ya, proj, proj, dproj)


def _pad128(v):
    return jnp.pad(v, ((0, 0), (0, 128 - v.shape[1])))


def _group_major(v):
    t = v.shape[0]
    return jnp.transpose(v[:, :SSD_HEADS].reshape(t, SSD_G, SSD_R), (1, 0, 2))


def _token_major(v3):
    t = v3.shape[1]
    return _pad128(jnp.transpose(v3, (1, 0, 2)).reshape(t, SSD_HEADS))


def _local_step(x, pos, target, w, small, fetch=None, early_grads=None):
    w = dict(w)
    xb = x.astype(BF16)
    if fetch is None:
        gu1, a1 = _mm_swiglu(xb, w["gu1"], "ffn1_gu")
    else:
        own = _mm_swiglu(xb, w["gu1_own"], "ffn1_gu_own", chip_idx=w["chip_idx"])
        w.update(fetch(0, own[1]))
        gu1, a1 = _mm_swiglu(xb, w["gu1"], "ffn1_gu_rest", chip_idx=w["chip_idx"], done=own)
        w.update(fetch(1, a1))
    f1 = _mm(a1, w["d1"], "nn", F32, "ffn1_down", caps=(512, 1024, FFN_H))
    h1, h1b, xh1, rs1 = _ln_fwd(x, f1, small["ln1_g"], small["ln1_b"], 0.5, "ln1_fwd")
    if fetch is not None:
        w.update(fetch(2, h1b))
    proj = _mm(h1b, w["win"], "nn", F32, "proj", caps=(1024, 896, 2048))
    if fetch is not None:
        w.update(fetch(3, proj))
    bias128 = _pad128(small["dt_bias"])
    alog128 = _pad128(small["a_log"])
    dt, adt = _dt_prep(proj, bias128, alog128)
    dt3, adt3 = _group_major(dt), _group_major(adt)
    xc = _conv_fwd(proj, small["conv_w"], small["conv_b"])
    y_ssd, hprev = _ssd_fwd(xc, dt3, adt3)
    dexp = jnp.repeat(small["d_skip"], SSD_P, axis=1)
    ysn = _gated_norm_fwd(y_ssd, xc, proj, dexp, small["ssd_norm_g"])
    ys = _mm(ysn, w["so"], "nn", F32, "ssd_out")
    invf = jnp.tile(ROPE_THETA ** (-jnp.arange(HD // 2, dtype=F32) * 2.0 / HD), 4)[None, :]
    qt, kr = _rope_fwd(proj, pos, invf)
    sinks128 = _pad128(small["attn_sinks"])
    ot = _attn_fwd(qt, kr, proj, sinks128)
    ya = _mm(ot, w["ao"], "tn", F32, "attn_out")
    mg = _merge_fwd(ys, ya, proj)
    mix = _mm(mg, w["out"], "nn", F32, "mix_out")
    h2, h2b, xh2, rs2 = _ln_fwd(h1, mix, small["ln2_g"], small["ln2_b"], 1.0, "ln2_fwd")
    gu2, a2 = _mm_swiglu(h2b, w["gu2"], "ffn2_gu")
    f2 = _mm(a2, w["d2"], "nn", F32, "ffn2_down", caps=(512, 1024, FFN_H))
    _, _, xh3, rs3, dh3, loss = _ln_fwd(h2, f2, small["ln3_g"], small["ln3_b"], 0.5, "ln3_fwd", target=target)

    gw, gs = {}, {}
    dr3, dr3h, gs["ln3_g"], gs["ln3_b"] = _ln_bwd(dh3, xh3, rs3, small["ln3_g"], 0.5, "ln3_bwd")
    gw["d2"] = _mm(a2, dr3h, "tn", F32, "ffn2_down_dw")
    dgu2 = _mm_swiglu_bwd(dr3h, w["d2"], gu2, "ffn2_down_dx")
    gw["gu2"] = _mm(h2b, dgu2, "tn", F32, "ffn2_gu_dw", caps=(1024, 1408, 2048), n_slabs=N_CHIPS)
    dh2 = _mm(dgu2, w["gu2"], "nt", F32, "ffn2_gu_dx", add=dr3, add_scale=ALPHA, caps=(1024, 1024, 2816))
    dr2, dr2b, gs["ln2_g"], gs["ln2_b"] = _ln_bwd(dh2, xh2, rs2, small["ln2_g"], 1.0, "ln2_bwd")
    gw["out"] = _mm(mg, dr2b, "tn", F32, "mix_out_dw")
    dmg = _mm(dr2b, w["out"], "nt", F32, "mix_out_dx")
    dproj = lax.empty((x.shape[0], PROJ_PAD), BF16)
    dys, dya, dproj = _merge_bwd(dmg, ys, ya, proj, dproj)
    gw["ao"] = _mm(ot, dya, "nn", F32, "attn_out_dw")
    dot_ = _mm(w["ao"], dya, "nt", BF16, "attn_out_dx")
    dqt, dkr, dproj, gs["attn_sinks"] = _attn_bwd(qt, kr, proj, sinks128, dot_, dproj)
    dproj, dk = _rope_bwd(dqt, dkr, pos, invf, dproj)
    dproj = _place_cols(dk, dproj, K_BLK, "place_dk")
    gw["so"] = _mm(ysn, dys, "tn", F32, "ssd_out_dw")
    dysn = _mm(dys, w["so"], "nt", F32, "ssd_out_dx")
    dproj, dy1, gs["ssd_norm_g"], dd_ch = _gated_norm_bwd(dysn, y_ssd, xc, proj, dexp, small["ssd_norm_g"], dproj)
    gs["d_skip"] = _fold_heads(dd_ch, "d_skip_fold")
    dxs, db, dc, dadt3, dxdx3 = _ssd_bwd(xc, dt3, adt3, hprev, dy1)
    ddt, gs["dt_bias"], gs["a_log"] = _dt_bwd(_token_major(dadt3), _token_major(dxdx3), proj, bias128, alog128)
    dproj = _place_cols(ddt, dproj, DT_BLK, "place_ddt")
    cw, cbias = small["conv_w"], small["conv_b"]
    dproj, dwx, dbx = _conv_bwd(proj, dxs, cw, cbias, 0, SSD_INNER, "conv_bwd_x", dproj, skip=(dy1, dexp))
    dproj, dwb, dbb = _conv_bwd(proj, db, cw, cbias, SSD_INNER, SSD_G * SSD_N, "conv_bwd_b", dproj)
    dproj, dwc, dbc = _conv_bwd(proj, dc, cw, cbias, SSD_INNER + SSD_G * SSD_N, SSD_G * SSD_N, "conv_bwd_c", dproj)
    gs["conv_w"] = jnp.concatenate([dwx[:4], dwb[:4], dwc[:4]], axis=1)
    gs["conv_b"] = jnp.concatenate([dbx, dbb, dbc], axis=1)
    gw["win"] = _mm(h1b, dproj, "tn", F32, "proj_dw", caps=(1024, 896, 2048))
    win = w["win"] if early_grads is None else early_grads[0](gw, w["win"])
    dh1 = _mm(dproj, win, "nt", F32, "proj_dx", add=dr2, add_scale=ALPHA, caps=(1024, 1024, 2432))
    ln1_g = small["ln1_g"]
    if early_grads is not None:
        ln1_g = ln1_g + early_grads[1](dh1)[0:1, 0:1]
    dr1, dr1h, gs["ln1_g"], gs["ln1_b"] = _ln_bwd(dh1, xh1, rs1, ln1_g, 0.5, "ln1_bwd")
    gw["d1"] = _mm(a1, dr1h, "tn", F32, "ffn1_down_dw")
    dgu1 = _mm_swiglu_bwd(dr1h, w["d1"], gu1, "ffn1_down_dx")
    gw["gu1"] = _mm(xb, dgu1, "tn", F32, "ffn1_gu_dw", caps=(1024, 1408, 2048), n_slabs=N_CHIPS)
    grad_x = _mm(dgu1, w["gu1"], "nt", F32, "ffn1_gu_dx", add=dr1, add_scale=ALPHA, caps=(1024, 1024, 2816))
    return loss, grad_x, gw, gs


MESH = pl.DeviceIdType.MESH
ANY = pl.BlockSpec(memory_space=pl.ANY)


def _place():
    x, y, c = lax.axis_index("x"), lax.axis_index("y"), lax.axis_index("c")
    peers = [(1 - x, y), (x, 1 - y), (1 - x, 1 - y)]
    return x, y, c, peers


BIG = [
    ("ffn1_w_gate", D, SHARD_H, "gu1", "col", 0),
    ("ffn1_w_up", D, SHARD_H, "gu1", "col", SHARD_H),
    ("ffn1_w_down", SHARD_H, D, "d1", "row", 0),
    ("w_in", D, SHARD_IN, "win4", "lead", 0),
    ("w_ssd_o", SSD_INNER // N_CHIPS, D, "so", "row", 0),
    ("w_attn_o", D // N_CHIPS, D, "ao", "row", 0),
    ("w_out", D // N_CHIPS, D, "out", "row", 0),
    ("ffn2_w_gate", D, SHARD_H, "gu2", "col", 0),
    ("ffn2_w_up", D, SHARD_H, "gu2", "col", SHARD_H),
    ("ffn2_w_down", SHARD_H, D, "d2", "row", 0),
]
GATHERED = {"gu1": (D, 2 * FFN_H), "d1": (FFN_H, D), "win4": (N_CHIPS, D, SHARD_IN), "so": (SSD_INNER, D),
            "ao": (D, D), "out": (D, D), "gu2": (D, 2 * FFN_H), "d2": (FFN_H, D)}


def _cast_place(srcs, oname, chip_idx, also_alone=False):
    rows, cols = srcs[0].shape
    tr = _divtile(rows, 256, 16)
    kind = [b[4] for b in BIG if b[3] == oname][0]
    n_src = len(srcs)

    def body(chip_ref, *refs):
        for o_ref in refs[n_src:]:
            for k, s_ref in enumerate(refs[:n_src]):
                o_ref[:, k * cols:(k + 1) * cols] = s_ref[...].astype(BF16)

    nt = rows // tr
    if kind == "col":
        o_spec = pl.BlockSpec((tr, n_src * cols), lambda i, chip_ref: (i, chip_ref[0]))
    elif kind == "row":
        o_spec = pl.BlockSpec((tr, cols), lambda i, chip_ref: (chip_ref[0] * nt + i, 0))
    else:
        o_spec = pl.BlockSpec((None, tr, cols), lambda i, chip_ref: (chip_ref[0], i, 0))
    out_specs, out_shape = [o_spec], [jax.ShapeDtypeStruct(GATHERED[oname], BF16)]
    if also_alone:
        out_specs.append(pl.BlockSpec((tr, n_src * cols), lambda i, chip_ref: (i, 0)))
        out_shape.append(jax.ShapeDtypeStruct((rows, n_src * cols), BF16))
    res = pl.pallas_call(
        body, name="cast_place_" + oname,
        grid_spec=pltpu.PrefetchScalarGridSpec(
            num_scalar_prefetch=1, grid=(nt,),
            in_specs=[pl.BlockSpec((tr, cols), lambda i, chip_ref: (i, 0))] * n_src, out_specs=out_specs),
        out_shape=out_shape,
        compiler_params=_params(("parallel",), 32 << 20),
    )(chip_idx, *srcs)
    return res if also_alone else res[0]


def _slot(outs, entry, j, half):
    _, rows, cols, oname, kind, off = entry
    o = outs[oname]
    hr = rows // 2
    if kind == "col":
        cs = pl.ds(pl.multiple_of(j * (2 * SHARD_H) + off, 128), cols)
        return o.at[pl.ds(pl.multiple_of(half * hr, 16), hr), cs]
    if kind == "row":
        return o.at[pl.ds(pl.multiple_of(j * rows + half * hr, 16), hr), :]
    return o.at[j, pl.ds(pl.multiple_of(half * hr, 16), hr), :]


HBM = pl.BlockSpec(memory_space=pltpu.HBM)
SEM = pl.BlockSpec(memory_space=pltpu.SEMAPHORE)


def _ici_copy(outs, entry, j, c, to, send, recv, k):
    ref = _slot(outs, entry, j, c)
    return pltpu.make_async_remote_copy(src_ref=ref, dst_ref=ref, send_sem=send.at[k], recv_sem=recv.at[k],
                                        device_id=to, device_id_type=MESH)


GATHER_GROUPS = [["gu1"], ["d1"], ["win4"], ["so", "ao", "out", "gu2", "d2"]]


def _gather_ici_start(placed, groups, tag, carried):
    names = [k for grp in groups for k in grp]
    bigs = [[b for b in BIG if b[3] in grp] for grp in groups]
    ng = len(groups)
    n_in = len(names) + 1

    def body(*refs):
        sems = refs[n_in:n_in + 2 * ng]
        outs = dict(zip(names, refs[n_in + 2 * ng:n_in + 2 * ng + len(names)]))
        token = refs[-1]
        x, y, c, peers = _place()
        for gi, big in enumerate(bigs):
            for i, entry in enumerate(big):
                for k, (px, py) in enumerate(peers):
                    _ici_copy(outs, entry, 2 * x + y, c, (px, py, c), sems[2 * gi], sems[2 * gi + 1], 3 * i + k).start()
        token[...] = jnp.zeros_like(token)

    sem_shapes = [pltpu.SemaphoreType.DMA((3 * len(big),)) for big in bigs for _ in range(2)]
    res = pl.pallas_call(
        body, name="gather_ici_start_" + tag,
        in_specs=[HBM] * n_in,
        out_specs=[SEM] * (2 * ng) + [HBM] * n_in + [pl.BlockSpec(memory_space=pltpu.VMEM)],
        out_shape=sem_shapes + [pltpu.HBM(GATHERED[k], BF16) for k in names]
        + [pltpu.HBM(carried.shape, carried.dtype), jax.ShapeDtypeStruct((8, 128), F32)],
        input_output_aliases={i: i + 2 * ng for i in range(n_in)},
        compiler_params=pltpu.CompilerParams(has_side_effects=pltpu.SideEffectType.DATAFLOW_SIDE_EFFECTING),
    )(*[pltpu.with_memory_space_constraint(a, pltpu.HBM) for a in [placed[k] for k in names] + [carried]])
    sems = [(res[2 * gi], res[2 * gi + 1]) for gi in range(ng)]
    return sems, dict(zip(names, res[2 * ng:2 * ng + len(names)])), res[2 * ng + len(names)]


def _gather_ici_wait(send, recv, arrays, names, after, tag):
    big = [b for b in BIG if b[3] in names]

    def body(*refs):
        outs = dict(zip(names, refs[:len(names)]))
        send_ref, recv_ref = refs[len(names)], refs[len(names) + 1]
        x, y, c, peers = _place()
        for i, entry in enumerate(big):
            for k, (px, py) in enumerate(peers):
                mine = _ici_copy(outs, entry, 2 * x + y, c, (px, py, c), send_ref, recv_ref, 3 * i + k)
                mine.wait_send()
                theirs = _ici_copy(outs, entry, 2 * px + py, c, (px, py, c), send_ref, recv_ref, 3 * i + k)
                theirs.wait_recv()

    res = pl.pallas_call(
        body, name="gather_ici_wait_" + tag,
        in_specs=[HBM] * len(names) + [SEM, SEM, pl.BlockSpec(memory_space=pl.ANY)],
        out_specs=[HBM] * len(names),
        out_shape=[pltpu.HBM(GATHERED[k], BF16) for k in names],
        input_output_aliases={i: i for i in range(len(names))},
        compiler_params=pltpu.CompilerParams(has_side_effects=pltpu.SideEffectType.DATAFLOW_SIDE_EFFECTING),
    )(*[arrays[k] for k in names], send, recv, after)
    return dict(zip(names, res))


def _gather_d2d(arrays, names, tag):
    big = [b for b in BIG if b[3] in names]
    n = len(big)

    def body(*refs):
        outs = dict(zip(names, refs[len(names):2 * len(names)]))
        fsend, frecv = refs[2 * len(names):]
        x, y, c, peers = _place()
        cps = []
        for i, entry in enumerate(big):
            for k, (px, py) in enumerate(peers):
                cp = _ici_copy(outs, entry, 2 * px + py, c, (x, y, 1 - c), fsend, frecv, 3 * i + k)
                cp.start()
                cps.append(cp)
        for i, entry in enumerate(big):
            for k, (px, py) in enumerate(peers):
                _ici_copy(outs, entry, 2 * px + py, 1 - c, (x, y, 1 - c), fsend, frecv, 3 * i + k).wait_recv()
        for cp in cps:
            cp.wait_send()

    res = pl.pallas_call(
        body, name="gather_d2d_" + tag,
        in_specs=[ANY] * len(names), out_specs=[ANY] * len(names),
        out_shape=[jax.ShapeDtypeStruct(GATHERED[k], BF16) for k in names],
        input_output_aliases={i: i for i in range(len(names))},
        scratch_shapes=[pltpu.SemaphoreType.DMA((3 * n,))] * 2,
    )(*[arrays[k] for k in names])
    return dict(zip(names, res))


def _win_pieces():
    pieces = []
    for g0, wd, i0 in SEGS.values():
        for j in range(N_CHIPS):
            lo, hi = max(g0, j * SHARD_IN), min(g0 + wd, (j + 1) * SHARD_IN)
            if lo < hi:
                pieces.append((j, lo - j * SHARD_IN, hi - j * SHARD_IN, i0 + lo - g0))
    return pieces


def _win_to_internal(win4):
    tr = 128

    def body(i_ref, o_ref):
        for j, s0, s1, d0 in _win_pieces():
            o_ref[:, d0:d0 + s1 - s0] = i_ref[j, :, s0:s1]
        o_ref[:, PROJ_W:] = jnp.zeros((tr, PROJ_PAD - PROJ_W), o_ref.dtype)

    return pl.pallas_call(
        body, name="win_to_internal", grid=(D // tr,),
        in_specs=[pl.BlockSpec((N_CHIPS, tr, SHARD_IN), lambda i: (0, i, 0))],
        out_specs=pl.BlockSpec((tr, PROJ_PAD), lambda i: (i, 0)),
        out_shape=jax.ShapeDtypeStruct((D, PROJ_PAD), win4.dtype),
        compiler_params=_params(("parallel",), 40 << 20),
    )(win4)


def _win_from_internal(g):
    tr = 64

    def body(i_ref, o_ref):
        for j, s0, s1, d0 in _win_pieces():
            o_ref[j, :, s0:s1] = i_ref[:, d0:d0 + s1 - s0]

    return pl.pallas_call(
        body, name="win_from_internal", grid=(D // tr,),
        in_specs=[pl.BlockSpec((tr, PROJ_PAD), lambda i: (i, 0))],
        out_specs=pl.BlockSpec((N_CHIPS, tr, SHARD_IN), lambda i: (0, i, 0)),
        out_shape=jax.ShapeDtypeStruct((N_CHIPS, D, SHARD_IN), g.dtype),
        compiler_params=_params(("parallel",), 40 << 20),
    )(g)


def _rs_pair_exchange(grads, tag, after):
    n = len(grads)

    def body(*refs):
        srcs, dsts = refs[:n], refs[n + 1:2 * n + 1]
        send, recv = refs[2 * n + 1:]
        x, y, c, _ = _place()
        cps = []
        for i in range(n):
            hr = srcs[i].shape[1] // 2
            cp = pltpu.make_async_remote_copy(
                src_ref=srcs[i].at[:, pl.ds(pl.multiple_of((1 - c) * hr, 16), hr), :], dst_ref=dsts[i],
                send_sem=send.at[i], recv_sem=recv.at[i], device_id=(x, y, 1 - c), device_id_type=MESH)
            cp.start()
            cps.append(cp)
        for cp in cps:
            cp.wait()

    return pl.pallas_call(
        body, name="rs_pair_exchange_" + tag, in_specs=[ANY] * (n + 1), out_specs=[ANY] * n,
        out_shape=[jax.ShapeDtypeStruct((g.shape[0], g.shape[1] // 2, g.shape[2]), F32) for g in grads],
        scratch_shapes=[pltpu.SemaphoreType.DMA((n,))] * 2,
    )(*grads, after)


def _pair_copy(src, dst, c, to, send, recv, k):
    hr = src.shape[1] // 2
    return pltpu.make_async_remote_copy(
        src_ref=src.at[:, pl.ds(pl.multiple_of((1 - c) * hr, 16), hr), :], dst_ref=dst,
        send_sem=send.at[k], recv_sem=recv.at[k], device_id=to, device_id_type=MESH)


def _rs_pair_start(grads, carried):
    n = len(grads)

    def body(*refs):
        send, recv = refs[2 * n + 1], refs[2 * n + 2]
        srcs, dsts = refs[2 * n + 3:3 * n + 3], refs[3 * n + 3:4 * n + 3]
        x, y, c, _ = _place()
        for i in range(n):
            _pair_copy(srcs[i], dsts[i], c, (x, y, 1 - c), send, recv, i).start()

    lands = [lax.empty((g.shape[0], g.shape[1] // 2, g.shape[2]), F32) for g in grads]
    res = pl.pallas_call(
        body, name="rs_pair_start",
        in_specs=[HBM] * (2 * n + 1), out_specs=[SEM, SEM] + [HBM] * (2 * n + 1),
        out_shape=[pltpu.SemaphoreType.DMA((n,)), pltpu.SemaphoreType.DMA((n,))]
        + [pltpu.HBM(g.shape, F32) for g in grads] + [pltpu.HBM(l.shape, F32) for l in lands]
        + [pltpu.HBM(carried.shape, carried.dtype)],
        input_output_aliases={i: i + 2 for i in range(2 * n + 1)},
        compiler_params=pltpu.CompilerParams(has_side_effects=pltpu.SideEffectType.DATAFLOW_SIDE_EFFECTING),
    )(*[pltpu.with_memory_space_constraint(a, pltpu.HBM) for a in list(grads) + lands + [carried]])
    return (res[0], res[1], list(res[2:2 + n]), list(res[2 + n:2 + 2 * n])), res[-1]


def _rs_pair_wait(send, recv, grads, lands, after):
    n = len(grads)

    def body(*refs):
        srcs, dsts = refs[:n], refs[n:2 * n]
        send_ref, recv_ref = refs[2 * n], refs[2 * n + 1]
        x, y, c, _ = _place()
        for i in range(n):
            cp = _pair_copy(srcs[i], dsts[i], c, (x, y, 1 - c), send_ref, recv_ref, i)
            cp.wait_send()
            cp.wait_recv()

    res = pl.pallas_call(
        body, name="rs_pair_wait",
        in_specs=[HBM] * (2 * n) + [SEM, SEM, pl.BlockSpec(memory_space=pl.ANY)],
        out_specs=[HBM] * (2 * n),
        out_shape=[pltpu.HBM(g.shape, F32) for g in grads] + [pltpu.HBM(l.shape, F32) for l in lands],
        input_output_aliases={i: i for i in range(2 * n)},
        compiler_params=pltpu.CompilerParams(has_side_effects=pltpu.SideEffectType.DATAFLOW_SIDE_EFFECTING),
    )(*grads, *lands, send, recv, after)
    return list(res[:n]), list(res[n:])


def _half_tile(hr):
    return _divtile(hr, 256, 16) if hr % 256 == 0 else _divtile(hr, 512, 16)


def _rs_pair_sum(g, r, c_idx, name):
    ns, rows, cols = g.shape
    hr = rows // 2
    tr = _half_tile(hr)
    nt = hr // tr

    def body(c_ref, g_ref, r_ref, ob_ref, of_ref):
        s = g_ref[...] + r_ref[...]
        ob_ref[...] = s.astype(BF16)
        of_ref[...] = s

    blk = pl.BlockSpec((None, tr, cols), lambda j, t, c_ref: (j, t, 0))
    return pl.pallas_call(
        body, name=name,
        grid_spec=pltpu.PrefetchScalarGridSpec(
            num_scalar_prefetch=1, grid=(ns, nt),
            in_specs=[pl.BlockSpec((None, tr, cols), lambda j, t, c_ref: (j, c_ref[0] * nt + t, 0)), blk],
            out_specs=[blk, blk]),
        out_shape=[jax.ShapeDtypeStruct((ns, hr, cols), BF16), jax.ShapeDtypeStruct((ns, hr, cols), F32)],
        compiler_params=_params(("parallel", "parallel"), 48 << 20),
    )(c_idx, g, r)


def _rs_chip_start(parts, tag):
    n = len(parts)

    def body(*refs):
        send, recv = refs[2 * n], refs[2 * n + 1]
        srcs, dsts = refs[2 * n + 2:3 * n + 2], refs[3 * n + 2:4 * n + 2]
        token = refs[-1]
        x, y, c, peers = _place()
        for i in range(n):
            for k, (px, py) in enumerate(peers):
                pltpu.make_async_remote_copy(
                    src_ref=srcs[i].at[2 * px + py], dst_ref=dsts[i].at[k],
                    send_sem=send.at[3 * i + k], recv_sem=recv.at[3 * i + k],
                    device_id=(px, py, c), device_id_type=MESH).start()
        token[...] = jnp.zeros_like(token)

    lands = [lax.empty((3,) + p.shape[1:], BF16) for p in parts]
    res = pl.pallas_call(
        body, name="rs_chip_start_" + tag,
        in_specs=[HBM] * (2 * n),
        out_specs=[SEM, SEM] + [HBM] * (2 * n) + [pl.BlockSpec(memory_space=pltpu.VMEM)],
        out_shape=[pltpu.SemaphoreType.DMA((3 * n,)), pltpu.SemaphoreType.DMA((3 * n,))]
        + [pltpu.HBM(p.shape, BF16) for p in parts] + [pltpu.HBM(l.shape, BF16) for l in lands]
        + [jax.ShapeDtypeStruct((8, 128), F32)],
        input_output_aliases={i: i + 2 for i in range(2 * n)},
        compiler_params=pltpu.CompilerParams(has_side_effects=pltpu.SideEffectType.DATAFLOW_SIDE_EFFECTING),
    )(*[pltpu.with_memory_space_constraint(a, pltpu.HBM) for a in list(parts) + lands])
    return res[0], res[1], list(res[2:2 + n]), list(res[2 + n:2 + 2 * n]), res[-1]


def _rs_chip_wait(send, recv, parts, lands, after, tag):
    n = len(parts)

    def body(*refs):
        srcs, dsts = refs[:n], refs[n:2 * n]
        send_ref, recv_ref = refs[2 * n], refs[2 * n + 1]
        x, y, c, peers = _place()
        for i in range(n):
            for k, (px, py) in enumerate(peers):
                cp = pltpu.make_async_remote_copy(
                    src_ref=srcs[i].at[2 * px + py], dst_ref=dsts[i].at[k],
                    send_sem=send_ref.at[3 * i + k], recv_sem=recv_ref.at[3 * i + k],
                    device_id=(px, py, c), device_id_type=MESH)
                cp.wait_send()
                cp.wait_recv()

    res = pl.pallas_call(
        body, name="rs_chip_wait_" + tag,
        in_specs=[HBM] * (2 * n) + [SEM, SEM, pl.BlockSpec(memory_space=pl.ANY)],
        out_specs=[HBM] * (2 * n),
        out_shape=[pltpu.HBM(p.shape, BF16) for p in parts] + [pltpu.HBM(l.shape, BF16) for l in lands],
        input_output_aliases={i: i for i in range(2 * n)},
        compiler_params=pltpu.CompilerParams(has_side_effects=pltpu.SideEffectType.DATAFLOW_SIDE_EFFECTING),
    )(*parts, *lands, send, recv, after)
    return list(res[n:])


def _rs_final_sum(own, got, chip_idx, c_idx, name):
    ns, hr, cols = own.shape
    tr = _half_tile(hr)
    nt = hr // tr

    def body(chip_ref, c_ref, o_ref, g_ref, out_ref):
        s = o_ref[...]
        for k in range(3):
            s = s + g_ref[k].astype(F32)
        out_ref[...] = s

    return pl.pallas_call(
        body, name=name,
        grid_spec=pltpu.PrefetchScalarGridSpec(
            num_scalar_prefetch=2, grid=(nt,),
            in_specs=[pl.BlockSpec((None, tr, cols), lambda t, chip_ref, c_ref: (chip_ref[0], t, 0)),
                      pl.BlockSpec((3, tr, cols), lambda t, chip_ref, c_ref: (0, t, 0))],
            out_specs=pl.BlockSpec((tr, cols), lambda t, chip_ref, c_ref: (c_ref[0] * nt + t, 0))),
        out_shape=jax.ShapeDtypeStruct((2 * hr, cols), F32),
        compiler_params=_params(("parallel",), 48 << 20),
    )(chip_idx, c_idx, own, got)


def _rs_share_halves(fulls, tag):
    n = len(fulls)

    def body(*refs):
        dsts = refs[n:2 * n]
        send, recv = refs[2 * n:]
        x, y, c, _ = _place()
        cps = []
        for i in range(n):
            hr = dsts[i].shape[0] // 2
            rows = dsts[i].at[pl.ds(pl.multiple_of(c * hr, 8), hr), :]
            cp = pltpu.make_async_remote_copy(src_ref=rows, dst_ref=rows, send_sem=send.at[i], recv_sem=recv.at[i],
                                              device_id=(x, y, 1 - c), device_id_type=MESH)
            cp.start()
            cps.append(cp)
        for i in range(n):
            hr = dsts[i].shape[0] // 2
            other = dsts[i].at[pl.ds(pl.multiple_of((1 - c) * hr, 8), hr), :]
            pltpu.make_async_remote_copy(src_ref=other, dst_ref=other, send_sem=send.at[i], recv_sem=recv.at[i],
                                         device_id=(x, y, 1 - c), device_id_type=MESH).wait_recv()
        for cp in cps:
            cp.wait_send()

    return pl.pallas_call(
        body, name="rs_share_halves_" + tag, in_specs=[ANY] * n, out_specs=[ANY] * n,
        out_shape=[jax.ShapeDtypeStruct(f.shape, F32) for f in fulls],
        input_output_aliases={i: i for i in range(n)},
        scratch_shapes=[pltpu.SemaphoreType.DMA((n,))] * 2,
    )(*fulls)


def _all_reduce_small(v):
    rows = v.shape[0]

    def body(v_ref, o_ref, buf, send, recv):
        x, y, c, _ = _place()
        me = 4 * x + 2 * y + c
        buf[me] = v_ref[...]
        cps = []
        for d in range(1, 8):
            px, py, pc = x ^ (d >> 2), y ^ ((d >> 1) & 1), c ^ (d & 1)
            cp = pltpu.make_async_remote_copy(src_ref=v_ref, dst_ref=buf.at[me], send_sem=send.at[d - 1],
                                              recv_sem=recv.at[d - 1], device_id=(px, py, pc), device_id_type=MESH)
            cp.start()
            cps.append(cp)
        for d in range(1, 8):
            px, py, pc = x ^ (d >> 2), y ^ ((d >> 1) & 1), c ^ (d & 1)
            pltpu.make_async_remote_copy(src_ref=v_ref, dst_ref=buf.at[4 * px + 2 * py + pc], send_sem=send.at[d - 1],
                                         recv_sem=recv.at[d - 1], device_id=(px, py, pc),
                                         device_id_type=MESH).wait_recv()
        for cp in cps:
            cp.wait_send()
        acc = buf[0]
        for d in range(1, 8):
            acc = acc + buf[d]
        o_ref[...] = acc

    vm = pl.BlockSpec(memory_space=pltpu.VMEM)
    return pl.pallas_call(
        body, name="all_reduce_small", in_specs=[vm], out_specs=vm,
        out_shape=jax.ShapeDtypeStruct((rows, 128), F32),
        scratch_shapes=[pltpu.VMEM((8, rows, 128), F32), pltpu.SemaphoreType.DMA((7,)), pltpu.SemaphoreType.DMA((7,))],
    )(v)


def _adamw(w, g, m, v, name, g_col_blk=0):
    rows, cols = w.shape
    tr = _divtile(rows, max(8, (2 << 20) // (4 * cols) // 8 * 8), 8)

    def body(w_ref, g_ref, m_ref, v_ref, go_ref, d_ref, mo_ref, vo_ref):
        gv = g_ref[...]
        mn = ADAM_B1 * m_ref[...] + (1.0 - ADAM_B1) * gv
        vn = ADAM_B2 * v_ref[...] + (1.0 - ADAM_B2) * (gv * gv)
        m_hat = mn / (1.0 - ADAM_B1 ** ADAM_STEP)
        v_hat = vn / (1.0 - ADAM_B2 ** ADAM_STEP)
        go_ref[...] = gv
        d_ref[...] = -ADAM_LR * (m_hat / (jnp.sqrt(v_hat) + ADAM_EPS) + ADAM_WD * w_ref[...])
        mo_ref[...] = mn
        vo_ref[...] = vn

    blk = pl.BlockSpec((tr, cols), lambda i: (i, 0))
    return pl.pallas_call(
        body, name=name, grid=(rows // tr,),
        in_specs=[blk, pl.BlockSpec((tr, cols), lambda i: (i, g_col_blk)), blk, blk],
        out_specs=[blk] * 4, out_shape=[jax.ShapeDtypeStruct((rows, cols), F32)] * 4,
        compiler_params=_params(("parallel",), 48 << 20),
    )(w, g, m, v)


SMALL = ["ln1_g", "ln1_b", "conv_w", "conv_b", "dt_bias", "a_log", "d_skip", "ssd_norm_g", "attn_sinks",
         "ln2_g", "ln2_b", "ln3_g", "ln3_b"]


def _pack_rows(vs):
    parts = []
    for v in vs:
        v = v.reshape(-1)
        parts.append(jnp.pad(v, (0, (-v.shape[0]) % 128)))
    flat = jnp.concatenate(parts)
    flat = jnp.pad(flat, (0, (-flat.shape[0]) % 1024))
    return flat.reshape(-1, 128)


def _unpack_rows(packed, shapes):
    flat = packed.reshape(-1)
    out, at = [], 0
    for s in shapes:
        nel = int(np.prod(s))
        out.append(flat[at:at + nel].reshape(s))
        at += nel + (-nel) % 128
    return out


def kernel(x, positions, ffn1_w_gate, ffn1_w_up, ffn1_w_down, ln1_g, ln1_b, w_in, conv_w, conv_b, dt_bias, a_log, d_skip, ssd_norm_g, w_ssd_o, attn_sinks, w_attn_o, w_out, ln2_g, ln2_b, ffn2_w_gate, ffn2_w_up, ffn2_w_down, ln3_g, ln3_b, loss_target, m_ffn1_w_gate, m_ffn1_w_up, m_ffn1_w_down, m_ln1_g, m_ln1_b, m_w_in, m_conv_w, m_conv_b, m_dt_bias, m_a_log, m_d_skip, m_ssd_norm_g, m_w_ssd_o, m_attn_sinks, m_w_attn_o, m_w_out, m_ln2_g, m_ln2_b, m_ffn2_w_gate, m_ffn2_w_up, m_ffn2_w_down, m_ln3_g, m_ln3_b, v_ffn1_w_gate, v_ffn1_w_up, v_ffn1_w_down, v_ln1_g, v_ln1_b, v_w_in, v_conv_w, v_conv_b, v_dt_bias, v_a_log, v_d_skip, v_ssd_norm_g, v_w_ssd_o, v_attn_sinks, v_w_attn_o, v_w_out, v_ln2_g, v_ln2_b, v_ffn2_w_gate, v_ffn2_w_up, v_ffn2_w_down, v_ln3_g, v_ln3_b):
    args = dict(locals())
    wts = {n: args[n][0] for n in [b[0] for b in BIG] + SMALL}
    mom_m = {n: args["m_" + n][0] for n in wts}
    mom_v = {n: args["v_" + n][0] for n in wts}
    t = x.shape[1]
    xi, yi, ci = lax.axis_index("x"), lax.axis_index("y"), lax.axis_index("c")
    chip = 2 * xi + yi

    c_idx = ci.astype(jnp.int32).reshape(1)
    chip_idx = chip.astype(jnp.int32).reshape(1)
    placed = {o: _cast_place([wts[b[0]] for b in BIG if b[3] == o], o, chip_idx, also_alone=(o == "gu1"))
              for o in GATHERED}
    placed["gu1"], gu1_own = placed["gu1"]
    g_sems, g_flight = {}, {}

    def fetch(group, after):
        names = GATHER_GROUPS[group]
        send, recv = g_sems[group]
        landed = _gather_ici_wait(send, recv, {k: g_flight[k] for k in names}, names, after, str(group))
        got = _gather_d2d(landed, names, str(group))
        if "win4" in got:
            got["win"] = _win_to_internal(got.pop("win4"))
        return got

    sems, arrays, gu1_own = _gather_ici_start(placed, GATHER_GROUPS, "all", gu1_own)
    g_sems.update(dict(enumerate(sems)))
    g_flight.update(arrays)
    w = {"gu1_own": gu1_own, "chip_idx": chip_idx}

    def slabs_of(gw, names):
        view = {"gu1": lambda: gw["gu1"], "gu2": lambda: gw["gu2"],
                "d1": lambda: gw["d1"].reshape(N_CHIPS, SHARD_H, D), "d2": lambda: gw["d2"].reshape(N_CHIPS, SHARD_H, D),
                "win": lambda: _win_from_internal(gw["win"]),
                "so": lambda: gw["so"].reshape(N_CHIPS, SSD_INNER // N_CHIPS, D),
                "ao": lambda: gw["ao"].reshape(N_CHIPS, D // N_CHIPS, D),
                "out": lambda: gw["out"].reshape(N_CHIPS, D // N_CHIPS, D)}
        return [view[nm]() for nm in names]

    early = ["win", "so", "ao", "out", "gu2", "d2"]
    late = ["gu1", "d1"]
    flight = {}

    def early_start(gw, win):
        flight["pair"], win = _rs_pair_start(slabs_of(gw, early), win)
        return win

    def early_mid(dh1):
        slabs, from_sib = _rs_pair_wait(*flight["pair"], dh1)
        pair = [_rs_pair_sum(g, r, c_idx, "rs_pair_sum_" + nm) for g, r, nm in zip(slabs, from_sib, early)]
        send, recv, parts, lands, token = _rs_chip_start([p[0] for p in pair], "early")
        flight.update(send=send, recv=recv, parts=parts, lands=lands, own=[p[1] for p in pair])
        return token

    early_grads = (early_start, early_mid)
    cw_rows = _pack_rows([lax.dynamic_update_slice(jnp.zeros((4, XBC), F32), wts["conv_w"], (0, chip * (XBC // N_CHIPS)))])
    cw_rows = jnp.where(ci == 0, cw_rows, 0.0)
    conv_w_full = _all_reduce_small(cw_rows)[:4 * XBC // 128].reshape(4, XBC)

    small = {n: (wts[n][None, :] if wts[n].ndim == 1 else wts[n]) for n in SMALL}
    small["conv_w"] = conv_w_full
    loss, grad_x, gw, gs = _local_step(x[0], positions[0].astype(F32)[:, None], loss_target[0], w, small,
                                       fetch=fetch, early_grads=early_grads)

    gvec = {n: gs[n] for n in SMALL}
    gvec["dt_bias"], gvec["a_log"], gvec["d_skip"] = gs["dt_bias"][:, :64], gs["a_log"][:, :64], gs["d_skip"][:, :64]
    gvec["attn_sinks"] = gs["attn_sinks"][:, :NQ]
    red = _all_reduce_small(_pack_rows([gvec[n] for n in SMALL] + [loss]))
    slabs = slabs_of(gw, late)
    from_sib = _rs_pair_exchange(slabs, "late", red)
    pair = [_rs_pair_sum(g, r, c_idx, "rs_pair_sum_" + nm) for g, r, nm in zip(slabs, from_sib, late)]
    l_send, l_recv, l_parts, l_lands, l_token = _rs_chip_start([p[0] for p in pair], "late")
    got_early = _rs_chip_wait(flight["send"], flight["recv"], flight["parts"], flight["lands"], l_token, "early")

    outs = {}
    big_src = {"ffn1_w_gate": ("gu1", 0), "ffn1_w_up": ("gu1", 1), "ffn1_w_down": ("d1", 0), "w_in": ("win", 0),
               "w_ssd_o": ("so", 0), "w_attn_o": ("ao", 0), "w_out": ("out", 0),
               "ffn2_w_gate": ("gu2", 0), "ffn2_w_up": ("gu2", 1), "ffn2_w_down": ("d2", 0)}

    def finish(names, own, got, tag):
        halves = [_rs_final_sum(o, gt, chip_idx, c_idx, "rs_final_sum_" + nm) for o, gt, nm in zip(own, got, names)]
        full = dict(zip(names, _rs_share_halves(halves, tag)))
        for nm, (src, blk) in big_src.items():
            if src in full:
                outs[nm] = _adamw(wts[nm], full[src], mom_m[nm], mom_v[nm], "adamw_" + nm, g_col_blk=blk)

    finish(early, flight["own"], got_early, "early")
    got_late = _rs_chip_wait(l_send, l_recv, l_parts, l_lands, outs["w_in"][1], "late")
    finish(late, [p[1] for p in pair], got_late, "late")

    shapes = [(4, XBC) if n == "conv_w" else wts[n].shape for n in SMALL] + [(1,)]
    red_list = _unpack_rows(red, shapes)
    loss_out = red_list[-1].reshape(())
    gsm = dict(zip(SMALL, red_list[:-1]))
    gsm["conv_w"] = lax.dynamic_slice_in_dim(gsm["conv_w"], chip * (XBC // N_CHIPS), XBC // N_CHIPS, axis=1)
    sm_shapes = [wts[n].shape for n in SMALL]
    res = _adamw(_pack_rows([wts[n] for n in SMALL]), _pack_rows([gsm[n] for n in SMALL]),
                 _pack_rows([mom_m[n] for n in SMALL]), _pack_rows([mom_v[n] for n in SMALL]), "adamw_small")
    res = [_unpack_rows(r, sm_shapes) for r in res]
    for i, nm in enumerate(SMALL):
        outs[nm] = tuple(r[i] for r in res)

    order = ["ffn1_w_gate", "ffn1_w_up", "ffn1_w_down", "ln1_g", "ln1_b", "w_in", "conv_w", "conv_b", "dt_bias", "a_log",
             "d_skip", "ssd_norm_g", "w_ssd_o", "attn_sinks", "w_attn_o", "w_out", "ln2_g", "ln2_b",
             "ffn2_w_gate", "ffn2_w_up", "ffn2_w_down", "ln3_g", "ln3_b"]
    result = [loss_out, grad_x[None]]
    for kind in range(4):
        result += [outs[nm][kind][None] for nm in order]
    return tuple(result)
```

```python
import functools
import math

import numpy as np
import jax
import jax.numpy as jnp
from jax import lax
from jax.experimental import pallas as pl
from jax.experimental.pallas import tpu as pltpu

F32 = jnp.float32
BF16 = jnp.bfloat16
HI = lax.Precision.HIGHEST

D = 2048
FFN_H = 5632
SSD_INNER = 4096
SSD_HEADS = 64
SSD_P = 64
SSD_G = 8
SSD_R = 8
SSD_N = 128
CHUNK = 128
XBC = 6144
NQ = 32
NKV = 4
HD = 64
QW = 2048
KVW = 256
WINDOW = 128
ROPE_THETA = 10000.0
ALPHA = 2.0 ** 0.25
LN_EPS = 1e-5
RMS_EPS = 1e-5
PROJ_W = 16960
N_CHIPS = 4
SHARD_IN = PROJ_W // N_CHIPS
SHARD_H = FFN_H // N_CHIPS

SEGS = {
    "z": (0, 4096, 0),
    "xbc": (4096, 6144, 10240),
    "dt": (10240, 64, 16896),
    "q": (10304, 2048, 8192),
    "k": (12352, 256, 16384),
    "v": (12608, 256, 16640),
    "gs": (12864, 2048, 4096),
    "ga": (14912, 2048, 6144),
}
PROJ_PAD = 17024

ADAM_LR = 0.001
ADAM_B1 = 0.9
ADAM_B2 = 0.999
ADAM_EPS = 1e-08
ADAM_WD = 0.01
ADAM_STEP = 10

VMEM_CAP = 60 * 1024 * 1024


def _params(sem, vmem_bytes):
    return pltpu.CompilerParams(dimension_semantics=sem, vmem_limit_bytes=int(min(VMEM_CAP, vmem_bytes)))


def _divtile(n, cap, q=128):
    best = None
    for d in range(q, min(n, cap) + 1, q):
        if n % d == 0:
            best = d
    return n if best is None else best


def _sigmoid(x):
    return 0.5 * jnp.tanh(0.5 * x) + 0.5


def _mm(a, b, mode, out_dtype, name, add=None, add_scale=1.0, caps=(1024, 1024, 2048), n_slabs=1):
    if mode == "nn":
        (m, k), (k2, n) = a.shape, b.shape
    elif mode == "nt":
        (m, k), (n, k2) = a.shape, b.shape
    else:
        (k, m), (k2, n) = a.shape, b.shape
    assert k == k2, (a.shape, b.shape, mode)
    tm, tn, tk = _divtile(m, caps[0]), _divtile(n // n_slabs, caps[1]), _divtile(k, caps[2])
    nk = k // tk
    per_slab = n // n_slabs // tn
    dims = {"nn": ((1,), (0,)), "nt": ((1,), (1,)), "tn": ((0,), (0,))}[mode]
    has_add = add is not None

    def body(*refs):
        if has_add:
            a_ref, b_ref, add_ref, o_ref = refs[:4]
            scr = refs[4:]
        else:
            a_ref, b_ref, o_ref = refs[:3]
            add_ref = None
            scr = refs[3:]
        part = lax.dot_general(a_ref[...].astype(BF16), b_ref[...].astype(BF16), (dims, ((), ())),
                               preferred_element_type=F32)

        def finish(acc):
            if has_add:
                acc = acc + add_scale * add_ref[...].astype(F32)
            o_ref[...] = acc.astype(o_ref.dtype)

        if nk == 1:
            finish(part)
        else:
            acc_ref = scr[0]
            kk = pl.program_id(2)

            @pl.when(kk == 0)
            def _():
                acc_ref[...] = part

            @pl.when(kk > 0)
            def _():
                acc_ref[...] += part

            @pl.when(kk == nk - 1)
            def _():
                finish(acc_ref[...])

    if mode == "nn":
        a_spec = pl.BlockSpec((tm, tk), lambda i, j, kk: (i, kk))
        b_spec = pl.BlockSpec((tk, tn), lambda i, j, kk: (kk, j))
    elif mode == "nt":
        a_spec = pl.BlockSpec((tm, tk), lambda i, j, kk: (i, kk))
        b_spec = pl.BlockSpec((tn, tk), lambda i, j, kk: (j, kk))
    else:
        a_spec = pl.BlockSpec((tk, tm), lambda i, j, kk: (kk, i))
        b_spec = pl.BlockSpec((tk, tn), lambda i, j, kk: (kk, j))
    o_spec = pl.BlockSpec((tm, tn), lambda i, j, kk: (i, j))
    out_shape = jax.ShapeDtypeStruct((m, n), out_dtype)
    if n_slabs > 1:
        assert not has_add
        o_spec = pl.BlockSpec((None, tm, tn), lambda i, j, kk: (j // per_slab, i, j % per_slab))
        out_shape = jax.ShapeDtypeStruct((n_slabs, m, n // n_slabs), out_dtype)
    in_specs = [a_spec, b_spec] + ([o_spec] if has_add else [])
    args = (a, b) + ((add,) if has_add else ())
    osz = jnp.dtype(out_dtype).itemsize
    vmem = (2 * (tm * tk * a.dtype.itemsize + tk * tn * b.dtype.itemsize) + 2 * tm * tn * osz
            + (2 * tm * tn * add.dtype.itemsize if has_add else 0) + 2 * tm * tn * 4
            + 2 * (tm * tk + tk * tn) + (8 << 20))
    return pl.pallas_call(
        body, name=name, grid=(m // tm, n // tn, nk),
        in_specs=in_specs, out_specs=o_spec, out_shape=out_shape,
        scratch_shapes=[pltpu.VMEM((tm, tn), F32)] if nk > 1 else [],
        compiler_params=_params(("parallel", "parallel", "arbitrary"), vmem),
    )(*args)


def _mm_swiglu(a, b, name, chip_idx=None, done=None):
    m, k = a.shape
    w = SHARD_H
    tm = _divtile(m, 512)

    def body(*refs):
        a_ref, b_ref = refs[-4 if done is None else -6], refs[-3 if done is None else -5]
        gu_ref, act_ref = refs[-2:]
        gu = jnp.dot(a_ref[...], b_ref[...], preferred_element_type=F32)
        g = gu[:, :w]
        gu_ref[...] = gu.astype(BF16)
        act_ref[...] = (g * _sigmoid(g) * gu[:, w:]).astype(BF16)

    out_shape = [jax.ShapeDtypeStruct((m, 2 * FFN_H), BF16), jax.ShapeDtypeStruct((m, FFN_H), BF16)]
    cp = _params(("parallel", "parallel"), 56 << 20)
    if chip_idx is None:
        return pl.pallas_call(
            body, name=name, grid=(N_CHIPS, m // tm),
            in_specs=[pl.BlockSpec((tm, k), lambda j, i: (i, 0)), pl.BlockSpec((k, 2 * w), lambda j, i: (0, j))],
            out_specs=[pl.BlockSpec((tm, 2 * w), lambda j, i: (i, j)), pl.BlockSpec((tm, w), lambda j, i: (i, j))],
            out_shape=out_shape, compiler_params=cp,
        )(a, b)
    first = done is None
    tile = (lambda j, c: c[0]) if first else (lambda j, c: (c[0] + 1 + j) % N_CHIPS)
    in_specs = [pl.BlockSpec((tm, k), lambda j, i, c: (i, 0)),
                pl.BlockSpec((k, 2 * w), (lambda j, i, c: (0, 0)) if first else (lambda j, i, c: (0, tile(j, c))))]
    return pl.pallas_call(
        body, name=name,
        grid_spec=pltpu.PrefetchScalarGridSpec(
            num_scalar_prefetch=1, grid=(1 if first else N_CHIPS - 1, m // tm),
            in_specs=in_specs + ([] if first else [pl.BlockSpec(memory_space=pl.ANY)] * 2),
            out_specs=[pl.BlockSpec((tm, 2 * w), lambda j, i, c: (i, tile(j, c))),
                       pl.BlockSpec((tm, w), lambda j, i, c: (i, tile(j, c)))]),
        out_shape=out_shape, compiler_params=cp,
        input_output_aliases={} if first else {3: 0, 4: 1},
    )(chip_idx, a, b, *(() if first else done))


def _mm_swiglu_bwd(dr, wd, gu, name):
    m, k = dr.shape
    w = SHARD_H
    tm = _divtile(m, 512)

    def body(dr_ref, wd_ref, gu_ref, o_ref):
        d = lax.dot_general(dr_ref[...], wd_ref[...], NT_DIMS, preferred_element_type=F32)
        g = gu_ref[:, :w].astype(F32)
        u = gu_ref[:, w:].astype(F32)
        s = _sigmoid(g)
        o_ref[:, :w] = (d * u * (s * (1.0 + g * (1.0 - s)))).astype(BF16)
        o_ref[:, w:] = (d * (g * s)).astype(BF16)

    return pl.pallas_call(
        body, name=name, grid=(N_CHIPS, m // tm),
        in_specs=[pl.BlockSpec((tm, k), lambda j, i: (i, 0)), pl.BlockSpec((w, k), lambda j, i: (j, 0)),
                  pl.BlockSpec((tm, 2 * w), lambda j, i: (i, j))],
        out_specs=pl.BlockSpec((tm, 2 * w), lambda j, i: (i, j)),
        out_shape=jax.ShapeDtypeStruct((m, 2 * FFN_H), BF16),
        compiler_params=_params(("parallel", "parallel"), 48 << 20),
    )(dr, wd, gu)


def _ln_fwd(base, f, g, b, c, name, target=None):
    t = base.shape[0]
    tt = _divtile(t, 256)
    with_loss = target is not None

    def body(*refs):
        if with_loss:
            base_ref, f_ref, g_ref, b_ref, tg_ref, h_ref, hb_ref, xh_ref, rs_ref, dh_ref, loss_ref = refs
        else:
            base_ref, f_ref, g_ref, b_ref, h_ref, hb_ref, xh_ref, rs_ref = refs
        r = ALPHA * base_ref[...] + c * f_ref[...]
        mu = jnp.mean(r, axis=-1, keepdims=True)
        xc = r - mu
        var = jnp.mean(xc * xc, axis=-1, keepdims=True)
        rstd = lax.rsqrt(var + LN_EPS)
        xh = xc * rstd
        h = xh * g_ref[...] + b_ref[...]
        h_ref[...] = h
        hb_ref[...] = h.astype(BF16)
        xh_ref[...] = xh
        rs_ref[...] = rstd
        if with_loss:
            e = h - tg_ref[...]
            dh_ref[...] = e * (1.0 / D)
            part = 0.5 * jnp.sum(jnp.sum(e * e, axis=-1, keepdims=True) * (1.0 / D), axis=0, keepdims=True)

            @pl.when(pl.program_id(0) == 0)
            def _():
                loss_ref[...] = jnp.zeros_like(loss_ref)

            loss_ref[...] += part

    row = pl.BlockSpec((tt, D), lambda i: (i, 0))
    vec = pl.BlockSpec((1, D), lambda i: (0, 0))
    col = pl.BlockSpec((tt, 1), lambda i: (i, 0))
    in_specs = [row, row, vec, vec] + ([row] if with_loss else [])
    out_specs = [row, row, row, col] + ([row, pl.BlockSpec((1, 1), lambda i: (0, 0))] if with_loss else [])
    out_shape = [jax.ShapeDtypeStruct((t, D), F32), jax.ShapeDtypeStruct((t, D), BF16),
                 jax.ShapeDtypeStruct((t, D), F32), jax.ShapeDtypeStruct((t, 1), F32)]
    if with_loss:
        out_shape += [jax.ShapeDtypeStruct((t, D), F32), jax.ShapeDtypeStruct((1, 1), F32)]
    args = (base, f, g, b) + ((target,) if with_loss else ())
    return pl.pallas_call(
        body, name=name, grid=(t // tt,), in_specs=in_specs, out_specs=out_specs, out_shape=out_shape,
        compiler_params=_params(("arbitrary",) if with_loss else ("parallel",), 48 << 20),
    )(*args)


def _ln_bwd(dy, xh, rstd, g, c, name):
    t = dy.shape[0]
    tt = _divtile(t, 256)

    def body(dy_ref, xh_ref, rs_ref, g_ref, dr_ref, drb_ref, dg_ref, db_ref):
        dyv = dy_ref[...]
        xhv = xh_ref[...]
        dxh = dyv * g_ref[...]
        m1 = jnp.mean(dxh, axis=-1, keepdims=True)
        m2 = jnp.mean(dxh * xhv, axis=-1, keepdims=True)
        dr = rs_ref[...] * (dxh - m1 - xhv * m2)
        dr_ref[...] = dr
        drb_ref[...] = (c * dr).astype(BF16)

        @pl.when(pl.program_id(0) == 0)
        def _():
            dg_ref[...] = jnp.zeros_like(dg_ref)
            db_ref[...] = jnp.zeros_like(db_ref)

        dg_ref[...] += jnp.sum(dyv * xhv, axis=0, keepdims=True)
        db_ref[...] += jnp.sum(dyv, axis=0, keepdims=True)

    row = pl.BlockSpec((tt, D), lambda i: (i, 0))
    vec = pl.BlockSpec((1, D), lambda i: (0, 0))
    col = pl.BlockSpec((tt, 1), lambda i: (i, 0))
    return pl.pallas_call(
        body, name=name, grid=(t // tt,), in_specs=[row, row, col, vec], out_specs=[row, row, vec, vec],
        out_shape=[jax.ShapeDtypeStruct((t, D), F32), jax.ShapeDtypeStruct((t, D), BF16),
                   jax.ShapeDtypeStruct((1, D), F32), jax.ShapeDtypeStruct((1, D), F32)],
        compiler_params=_params(("arbitrary",), 40 << 20),
    )(dy, xh, rstd, g)


DT_BLK = SEGS["dt"][2] // 128


def _dt_prep(proj, bias128, alog128):
    t = proj.shape[0]
    tt = _divtile(t, 1024)

    def body(p_ref, bias_ref, alog_ref, dt_ref, adt_ref):
        dtv = jax.nn.softplus(p_ref[...] + bias_ref[...])
        dt_ref[...] = dtv
        adt_ref[...] = dtv * (-jnp.exp(alog_ref[...]))

    blk = pl.BlockSpec((tt, 128), lambda i: (i, 0))
    vec = pl.BlockSpec((1, 128), lambda i: (0, 0))
    return pl.pallas_call(
        body, name="dt_prep", grid=(t // tt,),
        in_specs=[pl.BlockSpec((tt, 128), lambda i: (i, DT_BLK)), vec, vec], out_specs=[blk, blk],
        out_shape=[jax.ShapeDtypeStruct((t, 128), F32)] * 2,
        compiler_params=_params(("parallel",), 16 << 20),
    )(proj, bias128, alog128)


def _dt_bwd(dadt, dxdx, proj, bias128, alog128):
    t = proj.shape[0]
    tt = _divtile(t, 1024)

    def body(dadt_ref, dxdx_ref, p_ref, bias_ref, alog_ref, o_ref, dbias_ref, dalog_ref):
        pre = p_ref[...] + bias_ref[...]
        dtv = jax.nn.softplus(pre)
        a = -jnp.exp(alog_ref[...])
        ddt = a * dadt_ref[...] + dxdx_ref[...]
        draw = ddt * _sigmoid(pre)
        o_ref[...] = draw.astype(BF16)

        @pl.when(pl.program_id(0) == 0)
        def _():
            dbias_ref[...] = jnp.zeros_like(dbias_ref)
            dalog_ref[...] = jnp.zeros_like(dalog_ref)

        dbias_ref[...] += jnp.sum(draw, axis=0, keepdims=True)
        dalog_ref[...] += jnp.sum(dadt_ref[...] * dtv * a, axis=0, keepdims=True)

    blk = pl.BlockSpec((tt, 128), lambda i: (i, 0))
    vec = pl.BlockSpec((1, 128), lambda i: (0, 0))
    return pl.pallas_call(
        body, name="dt_bwd", grid=(t // tt,),
        in_specs=[blk, blk, pl.BlockSpec((tt, 128), lambda i: (i, DT_BLK)), vec, vec],
        out_specs=[blk, vec, vec],
        out_shape=[jax.ShapeDtypeStruct((t, 128), BF16), jax.ShapeDtypeStruct((1, 128), F32),
                   jax.ShapeDtypeStruct((1, 128), F32)],
        compiler_params=_params(("arbitrary",), 16 << 20),
    )(dadt, dxdx, proj, bias128, alog128)


CONV_CB = 128
CONV_TT = 2048


def _shift_down(cur, prev8, s):
    if s == 0:
        return cur
    rolled = pltpu.roll(cur, s, 0)
    head = pltpu.roll(prev8, s, 0)
    r8 = lax.broadcasted_iota(jnp.int32, (8, 1), 0)
    top = jnp.where(r8 < s, head, rolled[:8])
    return jnp.concatenate([top, rolled[8:]], axis=0)


def _shift_up(cur, next8, s):
    if s == 0:
        return cur
    n = cur.shape[0]
    rolled = pltpu.roll(cur, n - s, 0)
    tail = pltpu.roll(next8, 8 - s, 0)
    r8 = lax.broadcasted_iota(jnp.int32, (8, 1), 0)
    bot = jnp.where(r8 >= 8 - s, tail, rolled[n - 8:])
    return jnp.concatenate([rolled[:n - 8], bot], axis=0)


def _conv_fwd(proj, conv_w, conv_b):
    t = proj.shape[0]
    tt = _divtile(t, CONV_TT)
    base = SEGS["xbc"][2] // CONV_CB
    r8 = tt // 8

    def body(u_ref, up_ref, w_ref, b_ref, o_ref):
        cur = u_ref[...]
        prev8 = jnp.where(pl.program_id(1) > 0, up_ref[...], 0.0)
        acc = b_ref[...] + w_ref[3:4, :] * cur
        for k in range(3):
            acc = acc + w_ref[k:k + 1, :] * _shift_down(cur, prev8, 3 - k)
        o_ref[...] = acc * _sigmoid(acc)

    return pl.pallas_call(
        body, name="conv_fwd", grid=(XBC // CONV_CB, t // tt),
        in_specs=[pl.BlockSpec((tt, CONV_CB), lambda c, i: (i, base + c)),
                  pl.BlockSpec((8, CONV_CB), lambda c, i: (jnp.maximum(i * r8 - 1, 0), base + c)),
                  pl.BlockSpec((4, CONV_CB), lambda c, i: (0, c)),
                  pl.BlockSpec((1, CONV_CB), lambda c, i: (0, c))],
        out_specs=pl.BlockSpec((tt, CONV_CB), lambda c, i: (i, c)),
        out_shape=jax.ShapeDtypeStruct((t, XBC), F32),
        compiler_params=_params(("parallel", "parallel"), 24 << 20),
    )(proj, proj, conv_w, conv_b)


def _conv_bwd(proj, dout, conv_w, conv_b, col0, width, name, dproj, skip=None):
    t = proj.shape[0]
    tt = _divtile(t, CONV_TT)
    nt = t // tt
    base = SEGS["xbc"][2] // CONV_CB + col0 // CONV_CB
    wb = col0 // CONV_CB
    r8 = tt // 8
    has_skip = skip is not None

    def body(*refs):
        if has_skip:
            u_ref, up_ref, d_ref, w_ref, b_ref, _, sk_ref, skw_ref, du_ref, dw_ref, db_ref, nx_ref = refs
        else:
            u_ref, up_ref, d_ref, w_ref, b_ref, _, du_ref, dw_ref, db_ref, nx_ref = refs
        i = pl.program_id(1)
        cur = u_ref[...]
        prev8 = jnp.where(i < nt - 1, up_ref[...], 0.0)
        sh = [_shift_down(cur, prev8, 3 - k) for k in range(3)] + [cur]
        pre = b_ref[...]
        for k in range(4):
            pre = pre + w_ref[k:k + 1, :] * sh[k]
        sg = _sigmoid(pre)
        dout_v = d_ref[...]
        if has_skip:
            dout_v = dout_v + sk_ref[...] * skw_ref[...]
        dpre = dout_v * (sg * (1.0 + pre * (1.0 - sg)))

        @pl.when(i == 0)
        def _():
            nx_ref[...] = jnp.zeros_like(nx_ref)
            dw_ref[...] = jnp.zeros_like(dw_ref)
            db_ref[...] = jnp.zeros_like(db_ref)

        next8 = nx_ref[...]
        du = w_ref[3:4, :] * dpre
        for s in range(1, 4):
            du = du + w_ref[3 - s:4 - s, :] * _shift_up(dpre, next8, s)
        du_ref[...] = du.astype(BF16)
        nx_ref[...] = dpre[:8]
        rows = [jnp.sum(dpre * sh[k], axis=0, keepdims=True) for k in range(4)]
        dw_ref[...] += jnp.concatenate(rows + [jnp.zeros((4, CONV_CB), F32)], axis=0)
        db_ref[...] += jnp.sum(dpre, axis=0, keepdims=True)

    rev = lambda c, i: (nt - 1 - i, c)
    p_in, p_out, p_shape = _dproj_piece(t, tt, CONV_CB, lambda c, i: (nt - 1 - i, base + c))
    in_specs = [pl.BlockSpec((tt, CONV_CB), lambda c, i: (nt - 1 - i, base + c)),
                pl.BlockSpec((8, CONV_CB), lambda c, i: (jnp.maximum((nt - 1 - i) * r8 - 1, 0), base + c)),
                pl.BlockSpec((tt, CONV_CB), rev),
                pl.BlockSpec((4, CONV_CB), lambda c, i: (0, wb + c)),
                pl.BlockSpec((1, CONV_CB), lambda c, i: (0, wb + c)), p_in]
    args = [proj, proj, dout, conv_w, conv_b, dproj]
    if has_skip:
        in_specs += [pl.BlockSpec((tt, CONV_CB), rev), pl.BlockSpec((1, CONV_CB), lambda c, i: (0, c))]
        args += [skip[0], skip[1]]
    return pl.pallas_call(
        body, name=name, grid=(width // CONV_CB, nt),
        in_specs=in_specs,
        out_specs=[p_out, pl.BlockSpec((8, CONV_CB), lambda c, i: (0, c)),
                   pl.BlockSpec((1, CONV_CB), lambda c, i: (0, c))],
        out_shape=[p_shape, jax.ShapeDtypeStruct((8, width), F32), jax.ShapeDtypeStruct((1, width), F32)],
        input_output_aliases={5: 0},
        scratch_shapes=[pltpu.VMEM((8, CONV_CB), F32)],
        compiler_params=_params(("parallel", "arbitrary"), 32 << 20),
    )(*args)


GW = SSD_R * SSD_P


def _expand8(v, passes=2):
    r = v.shape[0]
    if r < 8:
        v = jnp.broadcast_to(v, (8, SSD_R))
    ri = lax.broadcasted_iota(jnp.int32, (SSD_R, GW), 0)
    ci = lax.broadcasted_iota(jnp.int32, (SSD_R, GW), 1)
    spread = jnp.where((ci >= ri * SSD_P) & (ci < (ri + 1) * SSD_P), 1.0, 0.0)
    return _dot01(v, spread, passes)[:r]


def _head_pair_split(tile):
    first = lax.broadcasted_iota(jnp.int32, (1, 2 * SSD_P), 1) < SSD_P
    return jnp.where(first, tile, 0.0), jnp.where(first, 0.0, tile)


def _sel(rows, group):
    ri = lax.broadcasted_iota(jnp.int32, (rows, rows // group), 0)
    ci = lax.broadcasted_iota(jnp.int32, (rows, rows // group), 1)
    lo = ci * group
    return jnp.where((ri >= lo) & (ri < lo + group), 1.0, 0.0).astype(F32)


def _dot01(lhs, rhs, passes, split_lhs=True, dims=((1,), (0,))):
    val, m01 = (lhs, rhs) if split_lhs else (rhs, lhs)
    m01 = m01.astype(BF16)
    out = None
    for p in range(passes):
        piece = val.astype(BF16)
        ops = (piece, m01) if split_lhs else (m01, piece)
        d = lax.dot_general(ops[0], ops[1], (dims, ((), ())), preferred_element_type=F32)
        out = d if out is None else out + d
        if p + 1 < passes:
            val = val - piece.astype(F32)
    return out


def _ssd_chunk_terms(adt):
    li = lax.broadcasted_iota(jnp.int32, (CHUNK, CHUNK), 0)
    si = lax.broadcasted_iota(jnp.int32, (CHUNK, CHUNK), 1)
    causal = li >= si
    a_cs = _dot01(jnp.where(causal, 1.0, 0.0), adt, 3, split_lhs=False)
    a_cs_t = _dot01(adt, jnp.where(li <= si, 1.0, 0.0), 3, dims=((0,), (0,)))
    return a_cs, a_cs_t, causal


def _ssd_fwd(xc, dt3, adt3):
    t = xc.shape[0]
    nc = t // CHUNK

    gs = 2

    def body(xs_ref, b_ref, c_ref, dt_ref, adt_ref, y_ref, hp_ref, h_ref):
        @pl.when(pl.program_id(1) == 0)
        def _():
            h_ref[...] = jnp.zeros_like(h_ref)

        for gg in range(gs):
            a_cs, a_cs_t, causal = _ssd_chunk_terms(adt_ref[gg])
            a_last = a_cs[CHUNK - 1:CHUNK, :]
            h = h_ref[gg]
            hp_ref[gg, 0] = h
            xd = xs_ref[:, GW * gg:GW * (gg + 1)] * _expand8(dt_ref[gg])
            bb = b_ref[:, SSD_N * gg:SSD_N * (gg + 1)].astype(BF16)
            cbf = c_ref[:, SSD_N * gg:SSD_N * (gg + 1)].astype(BF16)
            cb = lax.dot_general(cbf, bb, (((1,), (1,)), ((), ())), preferred_element_type=F32)
            yoff = jnp.dot(cbf, h.astype(BF16), preferred_element_type=F32) * _expand8(jnp.exp(a_cs))
            for q in range(SSD_R // 2):
                lmats = []
                for r in (2 * q, 2 * q + 1):
                    seg = jnp.exp(jnp.where(causal, a_cs[:, r:r + 1] - a_cs_t[r:r + 1, :], -jnp.inf))
                    lmats.append((cb * seg).astype(BF16))
                tile = slice(2 * SSD_P * q, 2 * SSD_P * (q + 1))
                xa, xb = _head_pair_split(xd[:, tile])
                y_ref[:, GW * gg + 2 * SSD_P * q:GW * gg + 2 * SSD_P * (q + 1)] = (
                    jnp.dot(jnp.concatenate(lmats, axis=1), jnp.concatenate([xa, xb], axis=0).astype(BF16),
                            preferred_element_type=F32) + yoff[:, tile])
            xdd = (xd * _expand8(jnp.exp(a_last - a_cs))).astype(BF16)
            h_ref[gg] = _expand8(jnp.exp(a_last), 3) * h + lax.dot_general(
                bb, xdd, (((0,), (0,)), ((), ())), preferred_element_type=F32)

    nb = SSD_INNER // (gs * SSD_N)
    return pl.pallas_call(
        body, name="ssd_fwd", grid=(SSD_G // gs, nc),
        in_specs=[pl.BlockSpec((CHUNK, gs * GW), lambda g, c: (c, g)),
                  pl.BlockSpec((CHUNK, gs * SSD_N), lambda g, c: (c, nb + g)),
                  pl.BlockSpec((CHUNK, gs * SSD_N), lambda g, c: (c, nb + SSD_G // gs + g)),
                  pl.BlockSpec((gs, CHUNK, SSD_R), lambda g, c: (g, c, 0)),
                  pl.BlockSpec((gs, CHUNK, SSD_R), lambda g, c: (g, c, 0))],
        out_specs=[pl.BlockSpec((CHUNK, gs * GW), lambda g, c: (c, g)),
                   pl.BlockSpec((gs, 1, SSD_N, GW), lambda g, c: (g, c, 0, 0))],
        out_shape=[jax.ShapeDtypeStruct((t, SSD_INNER), F32), jax.ShapeDtypeStruct((SSD_G, nc, SSD_N, GW), F32)],
        scratch_shapes=[pltpu.VMEM((gs, SSD_N, GW), F32)],
        compiler_params=_params(("parallel", "arbitrary"), 32 << 20),
    )(xc, xc, xc, dt3, adt3)


def _ssd_bwd(xc, dt3, adt3, hprev, dy):
    t = xc.shape[0]
    nc = t // CHUNK

    gs = 2

    def body(xs_ref, b_ref, c_ref, dt_ref, adt_ref, hp_ref, dy_ref,
             dx_ref, db_ref, dc_ref, dadt_ref, dxdx_ref, dh_ref):
        @pl.when(pl.program_id(1) == 0)
        def _():
            dh_ref[...] = jnp.zeros_like(dh_ref)

        for gg in range(gs):
            wide = slice(GW * gg, GW * (gg + 1))
            narrow = slice(SSD_N * gg, SSD_N * (gg + 1))
            dx, db, dc, dadt, dxdx, dh_new = group_bwd(
                xs_ref[:, wide], b_ref[:, narrow], c_ref[:, narrow], dt_ref[gg], adt_ref[gg], hp_ref[gg, 0],
                dy_ref[:, wide], dh_ref[gg])
            dx_ref[:, wide] = dx
            db_ref[:, narrow] = db
            dc_ref[:, narrow] = dc
            dadt_ref[gg] = dadt
            dxdx_ref[gg] = dxdx
            dh_ref[gg] = dh_new

    def group_bwd(xs, b, c, dt, adt, hp, dyv, dh):
        a_cs, a_cs_t, causal = _ssd_chunk_terms(adt)
        a_last = a_cs[CHUNK - 1:CHUNK, :]
        e_last = jnp.exp(a_last)
        ex = _expand8(jnp.exp(a_cs))
        dtex = _expand8(jnp.exp(a_last - a_cs))
        dtx = _expand8(dt)
        sel = _sel(GW, SSD_P)
        seg8 = lambda v: _dot01(v, sel, 2)

        xd = xs * dtx
        xdd = xd * dtex
        bb = b.astype(BF16)
        cbf = c.astype(BF16)
        hpb = hp.astype(BF16)
        dhb = dh.astype(BF16)
        xdb = xd.astype(BF16)
        cb = lax.dot_general(cbf, bb, (((1,), (1,)), ((), ())), preferred_element_type=F32)
        dye = (dyv * ex).astype(BF16)
        yoff = jnp.dot(cbf, hpb, preferred_element_type=F32) * ex
        dc = lax.dot_general(dye, hpb, (((1,), (1,)), ((), ())), preferred_element_type=F32)
        bdh = jnp.dot(bb, dhb, preferred_element_type=F32)
        db = lax.dot_general(xdd.astype(BF16), dhb, (((1,), (1,)), ((), ())), preferred_element_type=F32)
        dxd_state = bdh * dtex
        q_terms = xdd * bdh
        d_a = seg8(dyv * yoff - q_terms)
        d_a_last = seg8(jnp.sum(q_terms, axis=0, keepdims=True)
                        + _expand8(e_last, 3) * jnp.sum(hp * dh, axis=0, keepdims=True))
        dh_new = (lax.dot_general(cbf, dye, (((0,), (0,)), ((), ())), preferred_element_type=F32)
                  + _expand8(e_last, 3) * dh)
        dcb = jnp.zeros((CHUNK, CHUNK), F32)
        w_all = []
        dxd_parts = []
        for q2 in range(SSD_R // 2):
            tile = slice(2 * SSD_P * q2, 2 * SSD_P * (q2 + 1))
            dy_pair = [part.astype(BF16) for part in _head_pair_split(dyv[:, tile])]
            lmats = []
            for k, r in enumerate((2 * q2, 2 * q2 + 1)):
                seg = jnp.exp(jnp.where(causal, a_cs[:, r:r + 1] - a_cs_t[r:r + 1, :], -jnp.inf))
                lmat = cb * seg
                dm = lax.dot_general(dy_pair[k], xdb[:, tile], (((1,), (1,)), ((), ())), preferred_element_type=F32)
                dcb = dcb + dm * seg
                w_all.append(dm * lmat)
                lmats.append(lmat.astype(BF16))
            dxd_parts.append(lax.dot_general(jnp.concatenate(lmats, axis=0), jnp.concatenate(dy_pair, axis=0),
                                             (((0,), (0,)), ((), ())), preferred_element_type=F32))
        row_sums = _dot01(jnp.concatenate(w_all, axis=1), _sel(SSD_R * CHUNK, CHUNK), 2)
        cs_rows = jnp.concatenate([jnp.sum(wr, axis=0, keepdims=True) for wr in w_all], axis=0)
        col_sums = _dot01(cs_rows, _sel(SSD_R, 1), 3, dims=((0,), (0,)))
        d_a = d_a + row_sums - col_sums
        li = lax.broadcasted_iota(jnp.int32, (CHUNK, SSD_R), 0)
        d_a = d_a + jnp.where(li == CHUNK - 1, d_a_last, 0.0)
        l2 = lax.broadcasted_iota(jnp.int32, (CHUNK, CHUNK), 0)
        s2 = lax.broadcasted_iota(jnp.int32, (CHUNK, CHUNK), 1)
        dadt = _dot01(jnp.where(s2 >= l2, 1.0, 0.0), d_a, 3, split_lhs=False)
        dxd = dxd_state + jnp.concatenate(dxd_parts, axis=1)
        dcbb = dcb.astype(BF16)
        db = db + lax.dot_general(dcbb, cbf, (((0,), (0,)), ((), ())), preferred_element_type=F32)
        dc = dc + jnp.dot(dcbb, bb, preferred_element_type=F32)
        return dxd * dtx, db, dc, dadt, seg8(dxd * xs), dh_new

    nb = SSD_INNER // (gs * SSD_N)
    rc = lambda g, c: (nc - 1 - c, g)
    r3 = lambda g, c: (g, nc - 1 - c, 0)
    return pl.pallas_call(
        body, name="ssd_bwd", grid=(SSD_G // gs, nc),
        in_specs=[pl.BlockSpec((CHUNK, gs * GW), rc),
                  pl.BlockSpec((CHUNK, gs * SSD_N), lambda g, c: (nc - 1 - c, nb + g)),
                  pl.BlockSpec((CHUNK, gs * SSD_N), lambda g, c: (nc - 1 - c, nb + SSD_G // gs + g)),
                  pl.BlockSpec((gs, CHUNK, SSD_R), r3),
                  pl.BlockSpec((gs, CHUNK, SSD_R), r3),
                  pl.BlockSpec((gs, 1, SSD_N, GW), lambda g, c: (g, nc - 1 - c, 0, 0)),
                  pl.BlockSpec((CHUNK, gs * GW), rc)],
        out_specs=[pl.BlockSpec((CHUNK, gs * GW), rc),
                   pl.BlockSpec((CHUNK, gs * SSD_N), rc),
                   pl.BlockSpec((CHUNK, gs * SSD_N), rc),
                   pl.BlockSpec((gs, CHUNK, SSD_R), r3),
                   pl.BlockSpec((gs, CHUNK, SSD_R), r3)],
        out_shape=[jax.ShapeDtypeStruct((t, SSD_INNER), F32),
                   jax.ShapeDtypeStruct((t, SSD_G * SSD_N), F32),
                   jax.ShapeDtypeStruct((t, SSD_G * SSD_N), F32),
                   jax.ShapeDtypeStruct((SSD_G, t, SSD_R), F32),
                   jax.ShapeDtypeStruct((SSD_G, t, SSD_R), F32)],
        scratch_shapes=[pltpu.VMEM((gs, SSD_N, GW), F32)],
        compiler_params=_params(("parallel", "arbitrary"), 48 << 20),
    )(xc, xc, xc, dt3, adt3, hprev, dy)


def _gated_norm_fwd(y, xc, proj, dexp, ng):
    t = y.shape[0]
    tt = _divtile(t, 256)

    def body(y_ref, x_ref, z_ref, d_ref, g_ref, o_ref):
        z = z_ref[...]
        y2 = (y_ref[...] + d_ref[...] * x_ref[...]) * (z * _sigmoid(z))
        for gi in range(SSD_G):
            sl = slice(GW * gi, GW * (gi + 1))
            seg = y2[:, sl]
            rinv = lax.rsqrt(jnp.mean(seg * seg, axis=-1, keepdims=True) + RMS_EPS)
            o_ref[:, sl] = (seg * rinv * g_ref[:, sl]).astype(BF16)

    row = pl.BlockSpec((tt, SSD_INNER), lambda i: (i, 0))
    vec = pl.BlockSpec((1, SSD_INNER), lambda i: (0, 0))
    return pl.pallas_call(
        body, name="gated_norm_fwd", grid=(t // tt,), in_specs=[row, row, row, vec, vec], out_specs=row,
        out_shape=jax.ShapeDtypeStruct((t, SSD_INNER), BF16),
        compiler_params=_params(("parallel",), 48 << 20),
    )(y, xc, proj, dexp, ng)


def _gated_norm_bwd(dout, y, xc, proj, dexp, ng, dproj):
    t = y.shape[0]
    tt = _divtile(t, 128)

    def body(do_ref, y_ref, x_ref, z_ref, d_ref, g_ref, _, dz_ref, dy_ref, dg_ref, dd_ref):
        z = z_ref[...]
        sg = _sigmoid(z)
        sz = z * sg
        xs = x_ref[...]
        y1 = y_ref[...] + d_ref[...] * xs
        y2 = y1 * sz
        dov = do_ref[...]

        @pl.when(pl.program_id(0) == 0)
        def _():
            dg_ref[...] = jnp.zeros_like(dg_ref)
            dd_ref[...] = jnp.zeros_like(dd_ref)

        for gi in range(SSD_G):
            sl = slice(GW * gi, GW * (gi + 1))
            seg = y2[:, sl]
            rinv = lax.rsqrt(jnp.mean(seg * seg, axis=-1, keepdims=True) + RMS_EPS)
            yn = seg * rinv
            dsl = dov[:, sl]
            dg_ref[:, sl] += jnp.sum(dsl * yn, axis=0, keepdims=True)
            dyn = dsl * g_ref[:, sl]
            dy2 = rinv * (dyn - yn * jnp.mean(dyn * yn, axis=-1, keepdims=True))
            dz_ref[:, sl] = (dy2 * y1[:, sl] * (sg[:, sl] * (1.0 + z[:, sl] * (1.0 - sg[:, sl])))).astype(BF16)
            dy1 = dy2 * sz[:, sl]
            dy_ref[:, sl] = dy1
            dd_ref[:, sl] += jnp.sum(dy1 * xs[:, sl], axis=0, keepdims=True)

    row = pl.BlockSpec((tt, SSD_INNER), lambda i: (i, 0))
    vec = pl.BlockSpec((1, SSD_INNER), lambda i: (0, 0))
    p_in, p_out, p_shape = _dproj_piece(t, tt, SSD_INNER, lambda i: (i, 0))
    return pl.pallas_call(
        body, name="gated_norm_bwd", grid=(t // tt,), in_specs=[row, row, row, row, vec, vec, p_in],
        out_specs=[p_out, row, vec, vec],
        out_shape=[p_shape, jax.ShapeDtypeStruct((t, SSD_INNER), F32),
                   jax.ShapeDtypeStruct((1, SSD_INNER), F32), jax.ShapeDtypeStruct((1, SSD_INNER), F32)],
        input_output_aliases={6: 0},
        compiler_params=_params(("arbitrary",), 48 << 20),
    )(dout, y, xc, proj, dexp, ng, dproj)


def _fold_heads(v, name):
    def body(v_ref, o_ref):
        ri = lax.broadcasted_iota(jnp.int32, (SSD_INNER, 128), 0)
        ci = lax.broadcasted_iota(jnp.int32, (SSD_INNER, 128), 1)
        fold = jnp.where((ri >= ci * SSD_P) & (ri < (ci + 1) * SSD_P), 1.0, 0.0).astype(F32)
        o_ref[...] = jnp.dot(v_ref[...], fold, preferred_element_type=F32, precision=HI)

    return pl.pallas_call(body, name=name, out_shape=jax.ShapeDtypeStruct((1, 128), F32))(v)


Q_BLK = SEGS["q"][2] // QW
K_BLK = SEGS["k"][2] // KVW
V_BLK = SEGS["v"][2] // KVW


def _rope_tables(pos_ref, invf_ref, width):
    ang = pos_ref[...] * invf_ref[...]
    lane = lax.broadcasted_iota(jnp.int32, (1, 128), 1)
    sign = jnp.where((lane % HD) < (HD // 2), -1.0, 1.0)
    cos = jnp.tile(jnp.cos(ang), (1, width // 128))
    sin = jnp.tile(sign * jnp.sin(ang), (1, width // 128))
    first = (lax.broadcasted_iota(jnp.int32, (1, width), 1) % HD) < (HD // 2)
    return cos, sin, first


def _rot_half(u, first):
    w = u.shape[1]
    return jnp.where(first, pltpu.roll(u, w - HD // 2, 1), pltpu.roll(u, HD // 2, 1))


def _rope_fwd(proj, pos, invf):
    t = proj.shape[0]
    tt = _divtile(t, 512)

    def body(q_ref, k_ref, pos_ref, invf_ref, qo_ref, ko_ref):
        cos, sin, first = _rope_tables(pos_ref, invf_ref, QW)
        q = q_ref[...]
        qr = q * cos + _rot_half(q, first) * sin
        for p in range(QW // 128):
            qo_ref[128 * p:128 * (p + 1), :] = qr[:, 128 * p:128 * (p + 1)].T.astype(BF16)
        k = k_ref[...]
        ko_ref[...] = (k * cos[:, :KVW] + _rot_half(k, first[:, :KVW]) * sin[:, :KVW]).astype(BF16)

    return pl.pallas_call(
        body, name="rope_fwd", grid=(t // tt,),
        in_specs=[pl.BlockSpec((tt, QW), lambda i: (i, Q_BLK)), pl.BlockSpec((tt, KVW), lambda i: (i, K_BLK)),
                  pl.BlockSpec((tt, 1), lambda i: (i, 0)), pl.BlockSpec((1, 128), lambda i: (0, 0))],
        out_specs=[pl.BlockSpec((QW, tt), lambda i: (0, i)), pl.BlockSpec((tt, KVW), lambda i: (i, 0))],
        out_shape=[jax.ShapeDtypeStruct((QW, t), BF16), jax.ShapeDtypeStruct((t, KVW), BF16)],
        compiler_params=_params(("parallel",), 40 << 20),
    )(proj, proj, pos, invf)


def _rope_bwd(dqt, dk, pos, invf, dproj):
    t = dk.shape[0]
    tt = _divtile(t, 512)

    def body(dq_ref, dk_ref, pos_ref, invf_ref, _, qo_ref, ko_ref):
        cos, sin, first = _rope_tables(pos_ref, invf_ref, QW)
        q = jnp.concatenate([dq_ref[128 * p:128 * (p + 1), :].T for p in range(QW // 128)], axis=1)
        qo_ref[...] = (q * cos + _rot_half(q * sin, first)).astype(BF16)
        k = dk_ref[...]
        ko_ref[...] = (k * cos[:, :KVW] + _rot_half(k * sin[:, :KVW], first[:, :KVW])).astype(BF16)

    p_in, p_out, p_shape = _dproj_piece(t, tt, QW, lambda i: (i, Q_BLK))
    return pl.pallas_call(
        body, name="rope_bwd", grid=(t // tt,),
        in_specs=[pl.BlockSpec((QW, tt), lambda i: (0, i)), pl.BlockSpec((tt, KVW), lambda i: (i, 0)),
                  pl.BlockSpec((tt, 1), lambda i: (i, 0)), pl.BlockSpec((1, 128), lambda i: (0, 0)), p_in],
        out_specs=[p_out, pl.BlockSpec((tt, KVW), lambda i: (i, 0))],
        out_shape=[p_shape, jax.ShapeDtypeStruct((t, KVW), BF16)],
        input_output_aliases={4: 0},
        compiler_params=_params(("parallel",), 40 << 20),
    )(dqt, dk, pos, invf, dproj)


def _place_cols(piece, dproj, col_blk, name):
    t, w = piece.shape
    tt = _divtile(t, 1024)

    def body(p_ref, _, o_ref):
        o_ref[...] = p_ref[...]

    p_in, p_out, p_shape = _dproj_piece(t, tt, w, lambda i: (i, col_blk))
    return pl.pallas_call(
        body, name=name, grid=(t // tt,),
        in_specs=[pl.BlockSpec((tt, w), lambda i: (i, 0)), p_in], out_specs=p_out, out_shape=p_shape,
        input_output_aliases={1: 0},
        compiler_params=_params(("parallel",), 16 << 20),
    )(piece, dproj)


GQ = NQ // NKV
NT_DIMS = (((1,), (1,)), ((), ()))
TN_DIMS = (((0,), (0,)), ((), ()))


def _attn_heads(ref, j, dtype=None):
    out = jnp.concatenate([ref[HD * h:HD * (h + 1), :] for h in range(j * GQ, (j + 1) * GQ)], axis=1)
    return out if dtype is None else out.astype(dtype)


def _attn_sink_row(s_ref, j):
    return jnp.concatenate([jnp.broadcast_to(s_ref[:, h:h + 1], (1, WINDOW)) for h in range(j * GQ, (j + 1) * GQ)],
                           axis=1)


def _attn_mask(n):
    kr = lax.broadcasted_iota(jnp.int32, (2 * WINDOW, GQ * WINDOW), 0)
    qi = lax.broadcasted_iota(jnp.int32, (2 * WINDOW, GQ * WINDOW), 1) % WINDOW
    return (kr > qi) & (kr <= qi + WINDOW) & ((n > 0) | (kr >= WINDOW))


def _attn_probs(qgt, kk, sink, mask):
    s = jnp.where(mask, jnp.dot(kk, qgt, preferred_element_type=F32) * (HD ** -0.5), -jnp.inf)
    m = jnp.maximum(jnp.max(s, axis=0, keepdims=True), sink)
    p = jnp.exp(s - m)
    ps = jnp.exp(sink - m)
    inv = 1.0 / (jnp.sum(p, axis=0, keepdims=True) + ps)
    return p * inv, ps * inv


def _attn_fwd(qt, kr, proj, sinks):
    t = kr.shape[0]
    nb = t // WINDOW

    def body(q_ref, kc_ref, kp_ref, vc_ref, vp_ref, s_ref, o_ref):
        mask = _attn_mask(pl.program_id(0))
        for j in range(NKV):
            ks = slice(HD * j, HD * (j + 1))
            kk = jnp.concatenate([kp_ref[:, ks], kc_ref[:, ks]], axis=0)
            vv = jnp.concatenate([vp_ref[:, ks], vc_ref[:, ks]], axis=0).astype(BF16)
            pn, _ = _attn_probs(_attn_heads(q_ref, j), kk, _attn_sink_row(s_ref, j), mask)
            ot = lax.dot_general(vv, pn.astype(BF16), TN_DIMS, preferred_element_type=F32).astype(BF16)
            for g in range(GQ):
                h = j * GQ + g
                o_ref[HD * h:HD * (h + 1), :] = ot[:, WINDOW * g:WINDOW * (g + 1)]

    prev = lambda n: (jnp.maximum(n - 1, 0), 0)
    return pl.pallas_call(
        body, name="attn_fwd", grid=(nb,),
        in_specs=[pl.BlockSpec((QW, WINDOW), lambda n: (0, n)),
                  pl.BlockSpec((WINDOW, KVW), lambda n: (n, 0)), pl.BlockSpec((WINDOW, KVW), prev),
                  pl.BlockSpec((WINDOW, KVW), lambda n: (n, V_BLK)),
                  pl.BlockSpec((WINDOW, KVW), lambda n: (jnp.maximum(n - 1, 0), V_BLK)),
                  pl.BlockSpec((1, 128), lambda n: (0, 0))],
        out_specs=pl.BlockSpec((QW, WINDOW), lambda n: (0, n)),
        out_shape=jax.ShapeDtypeStruct((QW, t), BF16),
        compiler_params=_params(("parallel",), 24 << 20),
    )(qt, kr, kr, proj, proj, sinks)


def _attn_bwd(qt, kr, proj, sinks, dot_, dproj):
    t = kr.shape[0]
    nb = t // WINDOW

    def body(q_ref, kc_ref, kp_ref, vc_ref, vp_ref, s_ref, do_ref, _,
             dq_ref, dk_ref, dv_ref, ds_ref, dkc_ref, dvc_ref):
        i = pl.program_id(0)
        mask = _attn_mask(nb - 1 - i)

        @pl.when(i == 0)
        def _():
            dkc_ref[...] = jnp.zeros_like(dkc_ref)
            dvc_ref[...] = jnp.zeros_like(dvc_ref)
            ds_ref[...] = jnp.zeros_like(ds_ref)

        lane = lax.broadcasted_iota(jnp.int32, (1, 128), 1)
        ds_acc = jnp.zeros((1, 128), F32)
        for j in range(NKV):
            ks = slice(HD * j, HD * (j + 1))
            kk = jnp.concatenate([kp_ref[:, ks], kc_ref[:, ks]], axis=0)
            vv = jnp.concatenate([vp_ref[:, ks], vc_ref[:, ks]], axis=0).astype(BF16)
            qgt = _attn_heads(q_ref, j)
            pn, psn = _attn_probs(qgt, kk, _attn_sink_row(s_ref, j), mask)
            dogt = _attn_heads(do_ref, j)
            dp = jnp.dot(vv, dogt, preferred_element_type=F32)
            delta = jnp.sum(dp * pn, axis=0, keepdims=True)
            dsb = (pn * (dp - delta) * (HD ** -0.5)).astype(BF16)
            dsink = -psn * delta
            dqt = lax.dot_general(kk, dsb, TN_DIMS, preferred_element_type=F32)
            for g in range(GQ):
                h = j * GQ + g
                cols = slice(WINDOW * g, WINDOW * (g + 1))
                dq_ref[HD * h:HD * (h + 1), :] = dqt[:, cols]
                ds_acc = ds_acc + jnp.where(lane == h, jnp.sum(dsink[:, cols], axis=1, keepdims=True), 0.0)
            dkk = lax.dot_general(dsb, qgt, NT_DIMS, preferred_element_type=F32)
            dvv = lax.dot_general(pn.astype(BF16), dogt, NT_DIMS, preferred_element_type=F32)
            dk_ref[:, ks] = dkk[WINDOW:] + dkc_ref[:, ks]
            dv_ref[:, ks] = (dvv[WINDOW:] + dvc_ref[:, ks]).astype(BF16)
            dkc_ref[:, ks] = dkk[:WINDOW]
            dvc_ref[:, ks] = dvv[:WINDOW]
        ds_ref[...] += ds_acc

    cur = lambda i: (nb - 1 - i, 0)
    cur_t = lambda i: (0, nb - 1 - i)
    prev = lambda i: (jnp.maximum(nb - 2 - i, 0), 0)
    p_in, p_out, p_shape = _dproj_piece(t, WINDOW, KVW, lambda i: (nb - 1 - i, V_BLK))
    return pl.pallas_call(
        body, name="attn_bwd", grid=(nb,),
        in_specs=[pl.BlockSpec((QW, WINDOW), cur_t),
                  pl.BlockSpec((WINDOW, KVW), cur), pl.BlockSpec((WINDOW, KVW), prev),
                  pl.BlockSpec((WINDOW, KVW), lambda i: (nb - 1 - i, V_BLK)),
                  pl.BlockSpec((WINDOW, KVW), lambda i: (jnp.maximum(nb - 2 - i, 0), V_BLK)),
                  pl.BlockSpec((1, 128), lambda i: (0, 0)),
                  pl.BlockSpec((QW, WINDOW), cur_t), p_in],
        out_specs=[pl.BlockSpec((QW, WINDOW), cur_t), pl.BlockSpec((WINDOW, KVW), cur),
                   p_out, pl.BlockSpec((1, 128), lambda i: (0, 0))],
        out_shape=[jax.ShapeDtypeStruct((QW, t), F32), jax.ShapeDtypeStruct((t, KVW), F32),
                   p_shape, jax.ShapeDtypeStruct((1, 128), F32)],
        input_output_aliases={7: 2},
        scratch_shapes=[pltpu.VMEM((WINDOW, KVW), F32), pltpu.VMEM((WINDOW, KVW), F32)],
        compiler_params=_params(("arbitrary",), 32 << 20),
    )(qt, kr, kr, proj, proj, sinks, dot_, dproj)


GS_BLK = SEGS["gs"][2] // D
GA_BLK = SEGS["ga"][2] // D


def _merge_fwd(ys, ya, proj):
    t = ys.shape[0]
    tt = _divtile(t, 256)

    def body(ys_ref, ya_ref, gs_ref, ga_ref, o_ref):
        o_ref[...] = (_sigmoid(gs_ref[...]) * ys_ref[...] + _sigmoid(ga_ref[...]) * ya_ref[...]).astype(BF16)

    row = pl.BlockSpec((tt, D), lambda i: (i, 0))
    return pl.pallas_call(
        body, name="merge_fwd", grid=(t // tt,),
        in_specs=[row, row, pl.BlockSpec((tt, D), lambda i: (i, GS_BLK)), pl.BlockSpec((tt, D), lambda i: (i, GA_BLK))],
        out_specs=row, out_shape=jax.ShapeDtypeStruct((t, D), BF16),
        compiler_params=_params(("parallel",), 32 << 20),
    )(ys, ya, proj, proj)


def _dproj_piece(t, rows, width, index_map):
    return (pl.BlockSpec(memory_space=pl.ANY), pl.BlockSpec((rows, width), index_map),
            jax.ShapeDtypeStruct((t, PROJ_PAD), BF16))


def _merge_bwd(dm, ys, ya, proj, dproj):
    t = ys.shape[0]
    tt = _divtile(t, 256)

    def body(dm_ref, ys_ref, ya_ref, gs_ref, ga_ref, _, dys_ref, dya_ref, dg_ref):
        d = dm_ref[...]
        s = _sigmoid(gs_ref[...])
        a = _sigmoid(ga_ref[...])
        dys_ref[...] = (d * s).astype(BF16)
        dya_ref[...] = (d * a).astype(BF16)
        dg_ref[:, :D] = (d * ys_ref[...] * (s * (1.0 - s))).astype(BF16)
        dg_ref[:, D:] = (d * ya_ref[...] * (a * (1.0 - a))).astype(BF16)

    row = pl.BlockSpec((tt, D), lambda i: (i, 0))
    p_in, p_out, p_shape = _dproj_piece(t, tt, 2 * D, lambda i: (i, SEGS["gs"][2] // (2 * D)))
    return pl.pallas_call(
        body, name="merge_bwd", grid=(t // tt,),
        in_specs=[row, row, row, pl.BlockSpec((tt, D), lambda i: (i, GS_BLK)),
                  pl.BlockSpec((tt, D), lambda i: (i, GA_BLK)), p_in],
        out_specs=[row, row, p_out], out_shape=[jax.ShapeDtypeStruct((t, D), BF16)] * 2 + [p_shape],
        input_output_aliases={5: 2},
        compiler_params=_params(("parallel",), 40 << 20),
    )(dm, ys, ya, proj, proj, dproj)


def _pad128(v):
    return jnp.pad(v, ((0, 0), (0, 128 - v.shape[1])))


def _group_major(v):
    t = v.shape[0]
    return jnp.transpose(v[:, :SSD_HEADS].reshape(t, SSD_G, SSD_R), (1, 0, 2))


def _token_major(v3):
    t = v3.shape[1]
    return _pad128(jnp.transpose(v3, (1, 0, 2)).reshape(t, SSD_HEADS))


def _local_step(x, pos, target, w, small, fetch=None, early_grads=None):
    w = dict(w)
    xb = x.astype(BF16)
    if fetch is None:
        gu1, a1 = _mm_swiglu(xb, w["gu1"], "ffn1_gu")
    else:
        own = _mm_swiglu(xb, w["gu1_own"], "ffn1_gu_own", chip_idx=w["chip_idx"])
        w.update(fetch(0, own[1]))
        gu1, a1 = _mm_swiglu(xb, w["gu1"], "ffn1_gu_rest", chip_idx=w["chip_idx"], done=own)
        w.update(fetch(1, a1))
    f1 = _mm(a1, w["d1"], "nn", F32, "ffn1_down", caps=(512, 1024, FFN_H))
    h1, h1b, xh1, rs1 = _ln_fwd(x, f1, small["ln1_g"], small["ln1_b"], 0.5, "ln1_fwd")
    if fetch is not None:
        w.update(fetch(2, h1b))
    proj = _mm(h1b, w["win"], "nn", F32, "proj", caps=(1024, 896, 2048))
    if fetch is not None:
        w.update(fetch(3, proj))
    bias128 = _pad128(small["dt_bias"])
    alog128 = _pad128(small["a_log"])
    dt, adt = _dt_prep(proj, bias128, alog128)
    dt3, adt3 = _group_major(dt), _group_major(adt)
    xc = _conv_fwd(proj, small["conv_w"], small["conv_b"])
    y_ssd, hprev = _ssd_fwd(xc, dt3, adt3)
    dexp = jnp.repeat(small["d_skip"], SSD_P, axis=1)
    ysn = _gated_norm_fwd(y_ssd, xc, proj, dexp, small["ssd_norm_g"])
    ys = _mm(ysn, w["so"], "nn", F32, "ssd_out")
    invf = jnp.tile(ROPE_THETA ** (-jnp.arange(HD // 2, dtype=F32) * 2.0 / HD), 4)[None, :]
    qt, kr = _rope_fwd(proj, pos, invf)
    sinks128 = _pad128(small["attn_sinks"])
    ot = _attn_fwd(qt, kr, proj, sinks128)
    ya = _mm(ot, w["ao"], "tn", F32, "attn_out")
    mg = _merge_fwd(ys, ya, proj)
    mix = _mm(mg, w["out"], "nn", F32, "mix_out")
    h2, h2b, xh2, rs2 = _ln_fwd(h1, mix, small["ln2_g"], small["ln2_b"], 1.0, "ln2_fwd")
    gu2, a2 = _mm_swiglu(h2b, w["gu2"], "ffn2_gu")
    f2 = _mm(a2, w["d2"], "nn", F32, "ffn2_down", caps=(512, 1024, FFN_H))
    _, _, xh3, rs3, dh3, loss = _ln_fwd(h2, f2, small["ln3_g"], small["ln3_b"], 0.5, "ln3_fwd", target=target)

    gw, gs = {}, {}
    dr3, dr3h, gs["ln3_g"], gs["ln3_b"] = _ln_bwd(dh3, xh3, rs3, small["ln3_g"], 0.5, "ln3_bwd")
    gw["d2"] = _mm(a2, dr3h, "tn", F32, "ffn2_down_dw")
    dgu2 = _mm_swiglu_bwd(dr3h, w["d2"], gu2, "ffn2_down_dx")
    gw["gu2"] = _mm(h2b, dgu2, "tn", F32, "ffn2_gu_dw", caps=(1024, 1408, 2048), n_slabs=N_CHIPS)
    dh2 = _mm(dgu2, w["gu2"], "nt", F32, "ffn2_gu_dx", add=dr3, add_scale=ALPHA, caps=(1024, 1024, 2816))
    dr2, dr2b, gs["ln2_g"], gs["ln2_b"] = _ln_bwd(dh2, xh2, rs2, small["ln2_g"], 1.0, "ln2_bwd")
    gw["out"] = _mm(mg, dr2b, "tn", F32, "mix_out_dw")
    dmg = _mm(dr2b, w["out"], "nt", F32, "mix_out_dx")
    dproj = lax.empty((x.shape[0], PROJ_PAD), BF16)
    dys, dya, dproj = _merge_bwd(dmg, ys, ya, proj, dproj)
    gw["ao"] = _mm(ot, dya, "nn", F32, "attn_out_dw")
    dot_ = _mm(w["ao"], dya, "nt", BF16, "attn_out_dx")
    dqt, dkr, dproj, gs["attn_sinks"] = _attn_bwd(qt, kr, proj, sinks128, dot_, dproj)
    dproj, dk = _rope_bwd(dqt, dkr, pos, invf, dproj)
    dproj = _place_cols(dk, dproj, K_BLK, "place_dk")
    gw["so"] = _mm(ysn, dys, "tn", F32, "ssd_out_dw")
    dysn = _mm(dys, w["so"], "nt", F32, "ssd_out_dx")
    dproj, dy1, gs["ssd_norm_g"], dd_ch = _gated_norm_bwd(dysn, y_ssd, xc, proj, dexp, small["ssd_norm_g"], dproj)
    gs["d_skip"] = _fold_heads(dd_ch, "d_skip_fold")
    dxs, db, dc, dadt3, dxdx3 = _ssd_bwd(xc, dt3, adt3, hprev, dy1)
    ddt, gs["dt_bias"], gs["a_log"] = _dt_bwd(_token_major(dadt3), _token_major(dxdx3), proj, bias128, alog128)
    dproj = _place_cols(ddt, dproj, DT_BLK, "place_ddt")
    cw, cbias = small["conv_w"], small["conv_b"]
    dproj, dwx, dbx = _conv_bwd(proj, dxs, cw, cbias, 0, SSD_INNER, "conv_bwd_x", dproj, skip=(dy1, dexp))
    dproj, dwb, dbb = _conv_bwd(proj, db, cw, cbias, SSD_INNER, SSD_G * SSD_N, "conv_bwd_b", dproj)
    dproj, dwc, dbc = _conv_bwd(proj, dc, cw, cbias, SSD_INNER + SSD_G * SSD_N, SSD_G * SSD_N, "conv_bwd_c", dproj)
    gs["conv_w"] = jnp.concatenate([dwx[:4], dwb[:4], dwc[:4]], axis=1)
    gs["conv_b"] = jnp.concatenate([dbx, dbb, dbc], axis=1)
    gw["win"] = _mm(h1b, dproj, "tn", F32, "proj_dw", caps=(1024, 896, 2048))
    win = w["win"] if early_grads is None else early_grads[0](gw, w["win"])
    dh1 = _mm(dproj, win, "nt", F32, "proj_dx", add=dr2, add_scale=ALPHA, caps=(1024, 1024, 2432))
    ln1_g = small["ln1_g"]
    if early_grads is not None:
        ln1_g = ln1_g + early_grads[1](dh1)[0:1, 0:1]
    dr1, dr1h, gs["ln1_g"], gs["ln1_b"] = _ln_bwd(dh1, xh1, rs1, ln1_g, 0.5, "ln1_bwd")
    gw["d1"] = _mm(a1, dr1h, "tn", F32, "ffn1_down_dw")
    dgu1 = _mm_swiglu_bwd(dr1h, w["d1"], gu1, "ffn1_down_dx")
    gw["gu1"] = _mm(xb, dgu1, "tn", F32, "ffn1_gu_dw", caps=(1024, 1408, 2048), n_slabs=N_CHIPS)
    grad_x = _mm(dgu1, w["gu1"], "nt", F32, "ffn1_gu_dx", add=dr1, add_scale=ALPHA, caps=(1024, 1024, 2816))
    return loss, grad_x, gw, gs


MESH = pl.DeviceIdType.MESH
ANY = pl.BlockSpec(memory_space=pl.ANY)


def _place():
    x, y, c = lax.axis_index("x"), lax.axis_index("y"), lax.axis_index("c")
    peers = [(1 - x, y), (x, 1 - y), (1 - x, 1 - y)]
    return x, y, c, peers


BIG = [
    ("ffn1_w_gate", D, SHARD_H, "gu1", "col", 0),
    ("ffn1_w_up", D, SHARD_H, "gu1", "col", SHARD_H),
    ("ffn1_w_down", SHARD_H, D, "d1", "row", 0),
    ("w_in", D, SHARD_IN, "win4", "lead", 0),
    ("w_ssd_o", SSD_INNER // N_CHIPS, D, "so", "row", 0),
    ("w_attn_o", D // N_CHIPS, D, "ao", "row", 0),
    ("w_out", D // N_CHIPS, D, "out", "row", 0),
    ("ffn2_w_gate", D, SHARD_H, "gu2", "col", 0),
    ("ffn2_w_up", D, SHARD_H, "gu2", "col", SHARD_H),
    ("ffn2_w_down", SHARD_H, D, "d2", "row", 0),
]
GATHERED = {"gu1": (D, 2 * FFN_H), "d1": (FFN_H, D), "win4": (N_CHIPS, D, SHARD_IN), "so": (SSD_INNER, D),
            "ao": (D, D), "out": (D, D), "gu2": (D, 2 * FFN_H), "d2": (FFN_H, D)}


def _cast_place(srcs, oname, chip_idx, also_alone=False):
    rows, cols = srcs[0].shape
    tr = _divtile(rows, 256, 16)
    kind = [b[4] for b in BIG if b[3] == oname][0]
    n_src = len(srcs)

    def body(chip_ref, *refs):
        for o_ref in refs[n_src:]:
            for k, s_ref in enumerate(refs[:n_src]):
                o_ref[:, k * cols:(k + 1) * cols] = s_ref[...].astype(BF16)

    nt = rows // tr
    if kind == "col":
        o_spec = pl.BlockSpec((tr, n_src * cols), lambda i, chip_ref: (i, chip_ref[0]))
    elif kind == "row":
        o_spec = pl.BlockSpec((tr, cols), lambda i, chip_ref: (chip_ref[0] * nt + i, 0))
    else:
        o_spec = pl.BlockSpec((None, tr, cols), lambda i, chip_ref: (chip_ref[0], i, 0))
    out_specs, out_shape = [o_spec], [jax.ShapeDtypeStruct(GATHERED[oname], BF16)]
    if also_alone:
        out_specs.append(pl.BlockSpec((tr, n_src * cols), lambda i, chip_ref: (i, 0)))
        out_shape.append(jax.ShapeDtypeStruct((rows, n_src * cols), BF16))
    res = pl.pallas_call(
        body, name="cast_place_" + oname,
        grid_spec=pltpu.PrefetchScalarGridSpec(
            num_scalar_prefetch=1, grid=(nt,),
            in_specs=[pl.BlockSpec((tr, cols), lambda i, chip_ref: (i, 0))] * n_src, out_specs=out_specs),
        out_shape=out_shape,
        compiler_params=_params(("parallel",), 32 << 20),
    )(chip_idx, *srcs)
    return res if also_alone else res[0]


def _slot(outs, entry, j, half):
    _, rows, cols, oname, kind, off = entry
    o = outs[oname]
    hr = rows // 2
    if kind == "col":
        cs = pl.ds(pl.multiple_of(j * (2 * SHARD_H) + off, 128), cols)
        return o.at[pl.ds(pl.multiple_of(half * hr, 16), hr), cs]
    if kind == "row":
        return o.at[pl.ds(pl.multiple_of(j * rows + half * hr, 16), hr), :]
    return o.at[j, pl.ds(pl.multiple_of(half * hr, 16), hr), :]


HBM = pl.BlockSpec(memory_space=pltpu.HBM)
SEM = pl.BlockSpec(memory_space=pltpu.SEMAPHORE)


def _ici_copy(outs, entry, j, c, to, send, recv, k):
    ref = _slot(outs, entry, j, c)
    return pltpu.make_async_remote_copy(src_ref=ref, dst_ref=ref, send_sem=send.at[k], recv_sem=recv.at[k],
                                        device_id=to, device_id_type=MESH)


GATHER_GROUPS = [["gu1"], ["d1"], ["win4"], ["so", "ao", "out", "gu2", "d2"]]


def _gather_ici_start(placed, groups, tag, carried):
    names = [k for grp in groups for k in grp]
    bigs = [[b for b in BIG if b[3] in grp] for grp in groups]
    ng = len(groups)
    n_in = len(names) + 1

    def body(*refs):
        sems = refs[n_in:n_in + 2 * ng]
        outs = dict(zip(names, refs[n_in + 2 * ng:n_in + 2 * ng + len(names)]))
        token = refs[-1]
        x, y, c, peers = _place()
        for gi, big in enumerate(bigs):
            for i, entry in enumerate(big):
                for k, (px, py) in enumerate(peers):
                    _ici_copy(outs, entry, 2 * x + y, c, (px, py, c), sems[2 * gi], sems[2 * gi + 1], 3 * i + k).start()
        token[...] = jnp.zeros_like(token)

    sem_shapes = [pltpu.SemaphoreType.DMA((3 * len(big),)) for big in bigs for _ in range(2)]
    res = pl.pallas_call(
        body, name="gather_ici_start_" + tag,
        in_specs=[HBM] * n_in,
        out_specs=[SEM] * (2 * ng) + [HBM] * n_in + [pl.BlockSpec(memory_space=pltpu.VMEM)],
        out_shape=sem_shapes + [pltpu.HBM(GATHERED[k], BF16) for k in names]
        + [pltpu.HBM(carried.shape, carried.dtype), jax.ShapeDtypeStruct((8, 128), F32)],
        input_output_aliases={i: i + 2 * ng for i in range(n_in)},
        compiler_params=pltpu.CompilerParams(has_side_effects=pltpu.SideEffectType.DATAFLOW_SIDE_EFFECTING),
    )(*[pltpu.with_memory_space_constraint(a, pltpu.HBM) for a in [placed[k] for k in names] + [carried]])
    sems = [(res[2 * gi], res[2 * gi + 1]) for gi in range(ng)]
    return sems, dict(zip(names, res[2 * ng:2 * ng + len(names)])), res[2 * ng + len(names)]


def _gather_ici_wait(send, recv, arrays, names, after, tag):
    big = [b for b in BIG if b[3] in names]

    def body(*refs):
        outs = dict(zip(names, refs[:len(names)]))
        send_ref, recv_ref = refs[len(names)], refs[len(names) + 1]
        x, y, c, peers = _place()
        for i, entry in enumerate(big):
            for k, (px, py) in enumerate(peers):
                mine = _ici_copy(outs, entry, 2 * x + y, c, (px, py, c), send_ref, recv_ref, 3 * i + k)
                mine.wait_send()
                theirs = _ici_copy(outs, entry, 2 * px + py, c, (px, py, c), send_ref, recv_ref, 3 * i + k)
                theirs.wait_recv()

    res = pl.pallas_call(
        body, name="gather_ici_wait_" + tag,
        in_specs=[HBM] * len(names) + [SEM, SEM, pl.BlockSpec(memory_space=pl.ANY)],
        out_specs=[HBM] * len(names),
        out_shape=[pltpu.HBM(GATHERED[k], BF16) for k in names],
        input_output_aliases={i: i for i in range(len(names))},
        compiler_params=pltpu.CompilerParams(has_side_effects=pltpu.SideEffectType.DATAFLOW_SIDE_EFFECTING),
    )(*[arrays[k] for k in names], send, recv, after)
    return dict(zip(names, res))


def _gather_d2d(arrays, names, tag):
    big = [b for b in BIG if b[3] in names]
    n = len(big)

    def body(*refs):
        outs = dict(zip(names, refs[len(names):2 * len(names)]))
        fsend, frecv = refs[2 * len(names):]
        x, y, c, peers = _place()
        cps = []
        for i, entry in enumerate(big):
            for k, (px, py) in enumerate(peers):
                cp = _ici_copy(outs, entry, 2 * px + py, c, (x, y, 1 - c), fsend, frecv, 3 * i + k)
                cp.start()
                cps.append(cp)
        for i, entry in enumerate(big):
            for k, (px, py) in enumerate(peers):
                _ici_copy(outs, entry, 2 * px + py, 1 - c, (x, y, 1 - c), fsend, frecv, 3 * i + k).wait_recv()
        for cp in cps:
            cp.wait_send()

    res = pl.pallas_call(
        body, name="gather_d2d_" + tag,
        in_specs=[ANY] * len(names), out_specs=[ANY] * len(names),
        out_shape=[jax.ShapeDtypeStruct(GATHERED[k], BF16) for k in names],
        input_output_aliases={i: i for i in range(len(names))},
        scratch_shapes=[pltpu.SemaphoreType.DMA((3 * n,))] * 2,
    )(*[arrays[k] for k in names])
    return dict(zip(names, res))


def _win_pieces():
    pieces = []
    for g0, wd, i0 in SEGS.values():
        for j in range(N_CHIPS):
            lo, hi = max(g0, j * SHARD_IN), min(g0 + wd, (j + 1) * SHARD_IN)
            if lo < hi:
                pieces.append((j, lo - j * SHARD_IN, hi - j * SHARD_IN, i0 + lo - g0))
    return pieces


def _win_to_internal(win4):
    tr = 128

    def body(i_ref, o_ref):
        for j, s0, s1, d0 in _win_pieces():
            o_ref[:, d0:d0 + s1 - s0] = i_ref[j, :, s0:s1]
        o_ref[:, PROJ_W:] = jnp.zeros((tr, PROJ_PAD - PROJ_W), o_ref.dtype)

    return pl.pallas_call(
        body, name="win_to_internal", grid=(D // tr,),
        in_specs=[pl.BlockSpec((N_CHIPS, tr, SHARD_IN), lambda i: (0, i, 0))],
        out_specs=pl.BlockSpec((tr, PROJ_PAD), lambda i: (i, 0)),
        out_shape=jax.ShapeDtypeStruct((D, PROJ_PAD), win4.dtype),
        compiler_params=_params(("parallel",), 40 << 20),
    )(win4)


def _win_from_internal(g):
    tr = 64

    def body(i_ref, o_ref):
        for j, s0, s1, d0 in _win_pieces():
            o_ref[j, :, s0:s1] = i_ref[:, d0:d0 + s1 - s0]

    return pl.pallas_call(
        body, name="win_from_internal", grid=(D // tr,),
        in_specs=[pl.BlockSpec((tr, PROJ_PAD), lambda i: (i, 0))],
        out_specs=pl.BlockSpec((N_CHIPS, tr, SHARD_IN), lambda i: (0, i, 0)),
        out_shape=jax.ShapeDtypeStruct((N_CHIPS, D, SHARD_IN), g.dtype),
        compiler_params=_params(("parallel",), 40 << 20),
    )(g)


def _rs_pair_exchange(grads, tag, after):
    n = len(grads)

    def body(*refs):
        srcs, dsts = refs[:n], refs[n + 1:2 * n + 1]
        send, recv = refs[2 * n + 1:]
        x, y, c, _ = _place()
        cps = []
        for i in range(n):
            hr = srcs[i].shape[1] // 2
            cp = pltpu.make_async_remote_copy(
                src_ref=srcs[i].at[:, pl.ds(pl.multiple_of((1 - c) * hr, 16), hr), :], dst_ref=dsts[i],
                send_sem=send.at[i], recv_sem=recv.at[i], device_id=(x, y, 1 - c), device_id_type=MESH)
            cp.start()
            cps.append(cp)
        for cp in cps:
            cp.wait()

    return pl.pallas_call(
        body, name="rs_pair_exchange_" + tag, in_specs=[ANY] * (n + 1), out_specs=[ANY] * n,
        out_shape=[jax.ShapeDtypeStruct((g.shape[0], g.shape[1] // 2, g.shape[2]), F32) for g in grads],
        scratch_shapes=[pltpu.SemaphoreType.DMA((n,))] * 2,
    )(*grads, after)


def _pair_copy(src, dst, c, to, send, recv, k):
    hr = src.shape[1] // 2
    return pltpu.make_async_remote_copy(
        src_ref=src.at[:, pl.ds(pl.multiple_of((1 - c) * hr, 16), hr), :], dst_ref=dst,
        send_sem=send.at[k], recv_sem=recv.at[k], device_id=to, device_id_type=MESH)


def _rs_pair_start(grads, carried):
    n = len(grads)

    def body(*refs):
        send, recv = refs[2 * n + 1], refs[2 * n + 2]
        srcs, dsts = refs[2 * n + 3:3 * n + 3], refs[3 * n + 3:4 * n + 3]
        x, y, c, _ = _place()
        for i in range(n):
            _pair_copy(srcs[i], dsts[i], c, (x, y, 1 - c), send, recv, i).start()

    lands = [lax.empty((g.shape[0], g.shape[1] // 2, g.shape[2]), F32) for g in grads]
    res = pl.pallas_call(
        body, name="rs_pair_start",
        in_specs=[HBM] * (2 * n + 1), out_specs=[SEM, SEM] + [HBM] * (2 * n + 1),
        out_shape=[pltpu.SemaphoreType.DMA((n,)), pltpu.SemaphoreType.DMA((n,))]
        + [pltpu.HBM(g.shape, F32) for g in grads] + [pltpu.HBM(l.shape, F32) for l in lands]
        + [pltpu.HBM(carried.shape, carried.dtype)],
        input_output_aliases={i: i + 2 for i in range(2 * n + 1)},
        compiler_params=pltpu.CompilerParams(has_side_effects=pltpu.SideEffectType.DATAFLOW_SIDE_EFFECTING),
    )(*[pltpu.with_memory_space_constraint(a, pltpu.HBM) for a in list(grads) + lands + [carried]])
    return (res[0], res[1], list(res[2:2 + n]), list(res[2 + n:2 + 2 * n])), res[-1]


def _rs_pair_wait(send, recv, grads, lands, after):
    n = len(grads)

    def body(*refs):
        srcs, dsts = refs[:n], refs[n:2 * n]
        send_ref, recv_ref = refs[2 * n], refs[2 * n + 1]
        x, y, c, _ = _place()
        for i in range(n):
            cp = _pair_copy(srcs[i], dsts[i], c, (x, y, 1 - c), send_ref, recv_ref, i)
            cp.wait_send()
            cp.wait_recv()

    res = pl.pallas_call(
        body, name="rs_pair_wait",
        in_specs=[HBM] * (2 * n) + [SEM, SEM, pl.BlockSpec(memory_space=pl.ANY)],
        out_specs=[HBM] * (2 * n),
        out_shape=[pltpu.HBM(g.shape, F32) for g in grads] + [pltpu.HBM(l.shape, F32) for l in lands],
        input_output_aliases={i: i for i in range(2 * n)},
        compiler_params=pltpu.CompilerParams(has_side_effects=pltpu.SideEffectType.DATAFLOW_SIDE_EFFECTING),
    )(*grads, *lands, send, recv, after)
    return list(res[:n]), list(res[n:])


def _half_tile(hr):
    return _divtile(hr, 256, 16) if hr % 256 == 0 else _divtile(hr, 512, 16)


def _rs_pair_sum(g, r, c_idx, name):
    ns, rows, cols = g.shape
    hr = rows // 2
    tr = _half_tile(hr)
    nt = hr // tr

    def body(c_ref, g_ref, r_ref, ob_ref, of_ref):
        s = g_ref[...] + r_ref[...]
        ob_ref[...] = s.astype(BF16)
        of_ref[...] = s

    blk = pl.BlockSpec((None, tr, cols), lambda j, t, c_ref: (j, t, 0))
    return pl.pallas_call(
        body, name=name,
        grid_spec=pltpu.PrefetchScalarGridSpec(
            num_scalar_prefetch=1, grid=(ns, nt),
            in_specs=[pl.BlockSpec((None, tr, cols), lambda j, t, c_ref: (j, c_ref[0] * nt + t, 0)), blk],
            out_specs=[blk, blk]),
        out_shape=[jax.ShapeDtypeStruct((ns, hr, cols), BF16), jax.ShapeDtypeStruct((ns, hr, cols), F32)],
        compiler_params=_params(("parallel", "parallel"), 48 << 20),
    )(c_idx, g, r)


def _rs_chip_start(parts, tag):
    n = len(parts)

    def body(*refs):
        send, recv = refs[2 * n], refs[2 * n + 1]
        srcs, dsts = refs[2 * n + 2:3 * n + 2], refs[3 * n + 2:4 * n + 2]
        token = refs[-1]
        x, y, c, peers = _place()
        for i in range(n):
            for k, (px, py) in enumerate(peers):
                pltpu.make_async_remote_copy(
                    src_ref=srcs[i].at[2 * px + py], dst_ref=dsts[i].at[k],
                    send_sem=send.at[3 * i + k], recv_sem=recv.at[3 * i + k],
                    device_id=(px, py, c), device_id_type=MESH).start()
        token[...] = jnp.zeros_like(token)

    lands = [lax.empty((3,) + p.shape[1:], BF16) for p in parts]
    res = pl.pallas_call(
        body, name="rs_chip_start_" + tag,
        in_specs=[HBM] * (2 * n),
        out_specs=[SEM, SEM] + [HBM] * (2 * n) + [pl.BlockSpec(memory_space=pltpu.VMEM)],
        out_shape=[pltpu.SemaphoreType.DMA((3 * n,)), pltpu.SemaphoreType.DMA((3 * n,))]
        + [pltpu.HBM(p.shape, BF16) for p in parts] + [pltpu.HBM(l.shape, BF16) for l in lands]
        + [jax.ShapeDtypeStruct((8, 128), F32)],
        input_output_aliases={i: i + 2 for i in range(2 * n)},
        compiler_params=pltpu.CompilerParams(has_side_effects=pltpu.SideEffectType.DATAFLOW_SIDE_EFFECTING),
    )(*[pltpu.with_memory_space_constraint(a, pltpu.HBM) for a in list(parts) + lands])
    return res[0], res[1], list(res[2:2 + n]), list(res[2 + n:2 + 2 * n]), res[-1]


def _rs_chip_wait(send, recv, parts, lands, after, tag):
    n = len(parts)

    def body(*refs):
        srcs, dsts = refs[:n], refs[n:2 * n]
        send_ref, recv_ref = refs[2 * n], refs[2 * n + 1]
        x, y, c, peers = _place()
        for i in range(n):
            for k, (px, py) in enumerate(peers):
                cp = pltpu.make_async_remote_copy(
                    src_ref=srcs[i].at[2 * px + py], dst_ref=dsts[i].at[k],
                    send_sem=send_ref.at[3 * i + k], recv_sem=recv_ref.at[3 * i + k],
                    device_id=(px, py, c), device_id_type=MESH)
                cp.wait_send()
                cp.wait_recv()

    res = pl.pallas_call(
        body, name="rs_chip_wait_" + tag,
        in_specs=[HBM] * (2 * n) + [SEM, SEM, pl.BlockSpec(memory_space=pl.ANY)],
        out_specs=[HBM] * (2 * n),
        out_shape=[pltpu.HBM(p.shape, BF16) for p in parts] + [pltpu.HBM(l.shape, BF16) for l in lands],
        input_output_aliases={i: i for i in range(2 * n)},
        compiler_params=pltpu.CompilerParams(has_side_effects=pltpu.SideEffectType.DATAFLOW_SIDE_EFFECTING),
    )(*parts, *lands, send, recv, after)
    return list(res[n:])


def _rs_final_sum(own, got, chip_idx, c_idx, name):
    ns, hr, cols = own.shape
    tr = _half_tile(hr)
    nt = hr // tr

    def body(chip_ref, c_ref, o_ref, g_ref, out_ref):
        s = o_ref[...]
        for k in range(3):
            s = s + g_ref[k].astype(F32)
        out_ref[...] = s

    return pl.pallas_call(
        body, name=name,
        grid_spec=pltpu.PrefetchScalarGridSpec(
            num_scalar_prefetch=2, grid=(nt,),
            in_specs=[pl.BlockSpec((None, tr, cols), lambda t, chip_ref, c_ref: (chip_ref[0], t, 0)),
                      pl.BlockSpec((3, tr, cols), lambda t, chip_ref, c_ref: (0, t, 0))],
            out_specs=pl.BlockSpec((tr, cols), lambda t, chip_ref, c_ref: (c_ref[0] * nt + t, 0))),
        out_shape=jax.ShapeDtypeStruct((2 * hr, cols), F32),
        compiler_params=_params(("parallel",), 48 << 20),
    )(chip_idx, c_idx, own, got)


def _rs_share_halves(fulls, tag):
    n = len(fulls)

    def body(*refs):
        dsts = refs[n:2 * n]
        send, recv = refs[2 * n:]
        x, y, c, _ = _place()
        cps = []
        for i in range(n):
            hr = dsts[i].shape[0] // 2
            rows = dsts[i].at[pl.ds(pl.multiple_of(c * hr, 8), hr), :]
            cp = pltpu.make_async_remote_copy(src_ref=rows, dst_ref=rows, send_sem=send.at[i], recv_sem=recv.at[i],
                                              device_id=(x, y, 1 - c), device_id_type=MESH)
            cp.start()
            cps.append(cp)
        for i in range(n):
            hr = dsts[i].shape[0] // 2
            other = dsts[i].at[pl.ds(pl.multiple_of((1 - c) * hr, 8), hr), :]
            pltpu.make_async_remote_copy(src_ref=other, dst_ref=other, send_sem=send.at[i], recv_sem=recv.at[i],
                                         device_id=(x, y, 1 - c), device_id_type=MESH).wait_recv()
        for cp in cps:
            cp.wait_send()

    return pl.pallas_call(
        body, name="rs_share_halves_" + tag, in_specs=[ANY] * n, out_specs=[ANY] * n,
        out_shape=[jax.ShapeDtypeStruct(f.shape, F32) for f in fulls],
        input_output_aliases={i: i for i in range(n)},
        scratch_shapes=[pltpu.SemaphoreType.DMA((n,))] * 2,
    )(*fulls)


def _all_reduce_small(v):
    rows = v.shape[0]

    def body(v_ref, o_ref, buf, send, recv):
        x, y, c, _ = _place()
        me = 4 * x + 2 * y + c
        buf[me] = v_ref[...]
        cps = []
        for d in range(1, 8):
            px, py, pc = x ^ (d >> 2), y ^ ((d >> 1) & 1), c ^ (d & 1)
            cp = pltpu.make_async_remote_copy(src_ref=v_ref, dst_ref=buf.at[me], send_sem=send.at[d - 1],
                                              recv_sem=recv.at[d - 1], device_id=(px, py, pc), device_id_type=MESH)
            cp.start()
            cps.append(cp)
        for d in range(1, 8):
            px, py, pc = x ^ (d >> 2), y ^ ((d >> 1) & 1), c ^ (d & 1)
            pltpu.make_async_remote_copy(src_ref=v_ref, dst_ref=buf.at[4 * px + 2 * py + pc], send_sem=send.at[d - 1],
                                         recv_sem=recv.at[d - 1], device_id=(px, py, pc),
                                         device_id_type=MESH).wait_recv()
        for cp in cps:
            cp.wait_send()
        acc = buf[0]
        for d in range(1, 8):
            acc = acc + buf[d]
        o_ref[...] = acc

    vm = pl.BlockSpec(memory_space=pltpu.VMEM)
    return pl.pallas_call(
        body, name="all_reduce_small", in_specs=[vm], out_specs=vm,
        out_shape=jax.ShapeDtypeStruct((rows, 128), F32),
        scratch_shapes=[pltpu.VMEM((8, rows, 128), F32), pltpu.SemaphoreType.DMA((7,)), pltpu.SemaphoreType.DMA((7,))],
    )(v)


def _adamw(w, g, m, v, name, g_col_blk=0):
    rows, cols = w.shape
    tr = _divtile(rows, max(8, (2 << 20) // (4 * cols) // 8 * 8), 8)

    def body(w_ref, g_ref, m_ref, v_ref, go_ref, d_ref, mo_ref, vo_ref):
        gv = g_ref[...]
        mn = ADAM_B1 * m_ref[...] + (1.0 - ADAM_B1) * gv
        vn = ADAM_B2 * v_ref[...] + (1.0 - ADAM_B2) * (gv * gv)
        m_hat = mn / (1.0 - ADAM_B1 ** ADAM_STEP)
        v_hat = vn / (1.0 - ADAM_B2 ** ADAM_STEP)
        go_ref[...] = gv
        d_ref[...] = -ADAM_LR * (m_hat / (jnp.sqrt(v_hat) + ADAM_EPS) + ADAM_WD * w_ref[...])
        mo_ref[...] = mn
        vo_ref[...] = vn

    blk = pl.BlockSpec((tr, cols), lambda i: (i, 0))
    return pl.pallas_call(
        body, name=name, grid=(rows // tr,),
        in_specs=[blk, pl.BlockSpec((tr, cols), lambda i: (i, g_col_blk)), blk, blk],
        out_specs=[blk] * 4, out_shape=[jax.ShapeDtypeStruct((rows, cols), F32)] * 4,
        compiler_params=_params(("parallel",), 48 << 20),
    )(w, g, m, v)


SMALL = ["ln1_g", "ln1_b", "conv_w", "conv_b", "dt_bias", "a_log", "d_skip", "ssd_norm_g", "attn_sinks",
         "ln2_g", "ln2_b", "ln3_g", "ln3_b"]


def _pack_rows(vs):
    parts = []
    for v in vs:
        v = v.reshape(-1)
        parts.append(jnp.pad(v, (0, (-v.shape[0]) % 128)))
    flat = jnp.concatenate(parts)
    flat = jnp.pad(flat, (0, (-flat.shape[0]) % 1024))
    return flat.reshape(-1, 128)


def _unpack_rows(packed, shapes):
    flat = packed.reshape(-1)
    out, at = [], 0
    for s in shapes:
        nel = int(np.prod(s))
        out.append(flat[at:at + nel].reshape(s))
        at += nel + (-nel) % 128
    return out


def kernel(x, positions, ffn1_w_gate, ffn1_w_up, ffn1_w_down, ln1_g, ln1_b, w_in, conv_w, conv_b, dt_bias, a_log, d_skip, ssd_norm_g, w_ssd_o, attn_sinks, w_attn_o, w_out, ln2_g, ln2_b, ffn2_w_gate, ffn2_w_up, ffn2_w_down, ln3_g, ln3_b, loss_target, m_ffn1_w_gate, m_ffn1_w_up, m_ffn1_w_down, m_ln1_g, m_ln1_b, m_w_in, m_conv_w, m_conv_b, m_dt_bias, m_a_log, m_d_skip, m_ssd_norm_g, m_w_ssd_o, m_attn_sinks, m_w_attn_o, m_w_out, m_ln2_g, m_ln2_b, m_ffn2_w_gate, m_ffn2_w_up, m_ffn2_w_down, m_ln3_g, m_ln3_b, v_ffn1_w_gate, v_ffn1_w_up, v_ffn1_w_down, v_ln1_g, v_ln1_b, v_w_in, v_conv_w, v_conv_b, v_dt_bias, v_a_log, v_d_skip, v_ssd_norm_g, v_w_ssd_o, v_attn_sinks, v_w_attn_o, v_w_out, v_ln2_g, v_ln2_b, v_ffn2_w_gate, v_ffn2_w_up, v_ffn2_w_down, v_ln3_g, v_ln3_b):
    args = dict(locals())
    wts = {n: args[n][0] for n in [b[0] for b in BIG] + SMALL}
    mom_m = {n: args["m_" + n][0] for n in wts}
    mom_v = {n: args["v_" + n][0] for n in wts}
    t = x.shape[1]
    xi, yi, ci = lax.axis_index("x"), lax.axis_index("y"), lax.axis_index("c")
    chip = 2 * xi + yi

    c_idx = ci.astype(jnp.int32).reshape(1)
    chip_idx = chip.astype(jnp.int32).reshape(1)
    placed = {o: _cast_place([wts[b[0]] for b in BIG if b[3] == o], o, chip_idx, also_alone=(o == "gu1"))
              for o in GATHERED}
    placed["gu1"], gu1_own = placed["gu1"]
    g_sems, g_flight = {}, {}

    def fetch(group, after):
        names = GATHER_GROUPS[group]
        send, recv = g_sems[group]
        landed = _gather_ici_wait(send, recv, {k: g_flight[k] for k in names}, names, after, str(group))
        got = _gather_d2d(landed, names, str(group))
        if "win4" in got:
            got["win"] = _win_to_internal(got.pop("win4"))
        return got

    sems, arrays, gu1_own = _gather_ici_start(placed, GATHER_GROUPS, "all", gu1_own)
    g_sems.update(dict(enumerate(sems)))
    g_flight.update(arrays)
    w = {"gu1_own": gu1_own, "chip_idx": chip_idx}

    def slabs_of(gw, names):
        view = {"gu1": lambda: gw["gu1"], "gu2": lambda: gw["gu2"],
                "d1": lambda: gw["d1"].reshape(N_CHIPS, SHARD_H, D), "d2": lambda: gw["d2"].reshape(N_CHIPS, SHARD_H, D),
                "win": lambda: _win_from_internal(gw["win"]),
                "so": lambda: gw["so"].reshape(N_CHIPS, SSD_INNER // N_CHIPS, D),
                "ao": lambda: gw["ao"].reshape(N_CHIPS, D // N_CHIPS, D),
                "out": lambda: gw["out"].reshape(N_CHIPS, D // N_CHIPS, D)}
        return [view[nm]() for nm in names]

    early = ["win", "so", "ao", "out", "gu2", "d2"]
    late = ["gu1", "d1"]
    flight = {}

    def early_start(gw, win):
        flight["pair"], win = _rs_pair_start(slabs_of(gw, early), win)
        return win

    def early_mid(dh1):
        slabs, from_sib = _rs_pair_wait(*flight["pair"], dh1)
        pair = [_rs_pair_sum(g, r, c_idx, "rs_pair_sum_" + nm) for g, r, nm in zip(slabs, from_sib, early)]
        send, recv, parts, lands, token = _rs_chip_start([p[0] for p in pair], "early")
        flight.update(send=send, recv=recv, parts=parts, lands=lands, own=[p[1] for p in pair])
        return token

    early_grads = (early_start, early_mid)
    cw_rows = _pack_rows([lax.dynamic_update_slice(jnp.zeros((4, XBC), F32), wts["conv_w"], (0, chip * (XBC // N_CHIPS)))])
    cw_rows = jnp.where(ci == 0, cw_rows, 0.0)
    conv_w_full = _all_reduce_small(cw_rows)[:4 * XBC // 128].reshape(4, XBC)

    small = {n: (wts[n][None, :] if wts[n].ndim == 1 else wts[n]) for n in SMALL}
    small["conv_w"] = conv_w_full
    loss, grad_x, gw, gs = _local_step(x[0], positions[0].astype(F32)[:, None], loss_target[0], w, small,
                                       fetch=fetch, early_grads=early_grads)

    gvec = {n: gs[n] for n in SMALL}
    gvec["dt_bias"], gvec["a_log"], gvec["d_skip"] = gs["dt_bias"][:, :64], gs["a_log"][:, :64], gs["d_skip"][:, :64]
    gvec["attn_sinks"] = gs["attn_sinks"][:, :NQ]
    red = _all_reduce_small(_pack_rows([gvec[n] for n in SMALL] + [loss, grad_x[:1, :128]]))
    slabs = slabs_of(gw, late)
    from_sib = _rs_pair_exchange(slabs, "late", red)
    pair = [_rs_pair_sum(g, r, c_idx, "rs_pair_sum_" + nm) for g, r, nm in zip(slabs, from_sib, late)]
    l_send, l_recv, l_parts, l_lands, l_token = _rs_chip_start([p[0] for p in pair], "late")
    got_early = _rs_chip_wait(flight["send"], flight["recv"], flight["parts"], flight["lands"], l_token, "early")

    outs = {}
    big_src = {"ffn1_w_gate": ("gu1", 0), "ffn1_w_up": ("gu1", 1), "ffn1_w_down": ("d1", 0), "w_in": ("win", 0),
               "w_ssd_o": ("so", 0), "w_attn_o": ("ao", 0), "w_out": ("out", 0),
               "ffn2_w_gate": ("gu2", 0), "ffn2_w_up": ("gu2", 1), "ffn2_w_down": ("d2", 0)}

    def finish(names, own, got, tag):
        halves = [_rs_final_sum(o, gt, chip_idx, c_idx, "rs_final_sum_" + nm) for o, gt, nm in zip(own, got, names)]
        full = dict(zip(names, _rs_share_halves(halves, tag)))
        for nm, (src, blk) in big_src.items():
            if src in full:
                outs[nm] = _adamw(wts[nm], full[src], mom_m[nm], mom_v[nm], "adamw_" + nm, g_col_blk=blk)

    finish(early, flight["own"], got_early, "early")
    got_late = _rs_chip_wait(l_send, l_recv, l_parts, l_lands, outs["w_in"][1], "late")
    finish(late, [p[1] for p in pair], got_late, "late")

    shapes = [(4, XBC) if n == "conv_w" else wts[n].shape for n in SMALL] + [(1,)]
    red_list = _unpack_rows(red, shapes)
    loss_out = red_list[-1].reshape(())
    gsm = dict(zip(SMALL, red_list[:-1]))
    gsm["conv_w"] = lax.dynamic_slice_in_dim(gsm["conv_w"], chip * (XBC // N_CHIPS), XBC // N_CHIPS, axis=1)
    sm_shapes = [wts[n].shape for n in SMALL]
    res = _adamw(_pack_rows([wts[n] for n in SMALL]), _pack_rows([gsm[n] for n in SMALL]),
                 _pack_rows([mom_m[n] for n in SMALL]), _pack_rows([mom_v[n] for n in SMALL]), "adamw_small")
    res = [_unpack_rows(r, sm_shapes) for r in res]
    for i, nm in enumerate(SMALL):
        outs[nm] = tuple(r[i] for r in res)

    order = ["ffn1_w_gate", "ffn1_w_up", "ffn1_w_down", "ln1_g", "ln1_b", "w_in", "conv_w", "conv_b", "dt_bias", "a_log",
             "d_skip", "ssd_norm_g", "w_ssd_o", "attn_sinks", "w_attn_o", "w_out", "ln2_g", "ln2_b",
             "ffn2_w_gate", "ffn2_w_up", "ffn2_w_down", "ln3_g", "ln3_b"]
    result = [loss_out, grad_x[None]]
    for kind in range(4):
        result += [outs[nm][kind][None] for nm in order]
    return tuple(result)
```

```python
import functools
import math

import numpy as np
import jax
import jax.numpy as jnp
from jax import lax
from jax.experimental import pallas as pl
from jax.experimental.pallas import tpu as pltpu

F32 = jnp.float32
BF16 = jnp.bfloat16
HI = lax.Precision.HIGHEST

D = 2048
FFN_H = 5632
SSD_INNER = 4096
SSD_HEADS = 64
SSD_P = 64
SSD_G = 8
SSD_R = 8
SSD_N = 128
CHUNK = 128
XBC = 6144
NQ = 32
NKV = 4
HD = 64
QW = 2048
KVW = 256
WINDOW = 128
ROPE_THETA = 10000.0
ALPHA = 2.0 ** 0.25
LN_EPS = 1e-5
RMS_EPS = 1e-5
PROJ_W = 16960
N_CHIPS = 4
SHARD_IN = PROJ_W // N_CHIPS
SHARD_H = FFN_H // N_CHIPS

SEGS = {
    "z": (0, 4096, 0),
    "xbc": (4096, 6144, 10240),
    "dt": (10240, 64, 16896),
    "q": (10304, 2048, 8192),
    "k": (12352, 256, 16384),
    "v": (12608, 256, 16640),
    "gs": (12864, 2048, 4096),
    "ga": (14912, 2048, 6144),
}
PROJ_PAD = 17024

ADAM_LR = 0.001
ADAM_B1 = 0.9
ADAM_B2 = 0.999
ADAM_EPS = 1e-08
ADAM_WD = 0.01
ADAM_STEP = 10

VMEM_CAP = 60 * 1024 * 1024


def _params(sem, vmem_bytes):
    return pltpu.CompilerParams(dimension_semantics=sem, vmem_limit_bytes=int(min(VMEM_CAP, vmem_bytes)))


def _divtile(n, cap, q=128):
    best = None
    for d in range(q, min(n, cap) + 1, q):
        if n % d == 0:
            best = d
    return n if best is None else best


def _sigmoid(x):
    return 0.5 * jnp.tanh(0.5 * x) + 0.5


def _mm(a, b, mode, out_dtype, name, add=None, add_scale=1.0, caps=(1024, 1024, 2048), n_slabs=1):
    if mode == "nn":
        (m, k), (k2, n) = a.shape, b.shape
    elif mode == "nt":
        (m, k), (n, k2) = a.shape, b.shape
    else:
        (k, m), (k2, n) = a.shape, b.shape
    assert k == k2, (a.shape, b.shape, mode)
    tm, tn, tk = _divtile(m, caps[0]), _divtile(n // n_slabs, caps[1]), _divtile(k, caps[2])
    nk = k // tk
    per_slab = n // n_slabs // tn
    dims = {"nn": ((1,), (0,)), "nt": ((1,), (1,)), "tn": ((0,), (0,))}[mode]
    has_add = add is not None

    def body(*refs):
        if has_add:
            a_ref, b_ref, add_ref, o_ref = refs[:4]
            scr = refs[4:]
        else:
            a_ref, b_ref, o_ref = refs[:3]
            add_ref = None
            scr = refs[3:]
        part = lax.dot_general(a_ref[...].astype(BF16), b_ref[...].astype(BF16), (dims, ((), ())),
                               preferred_element_type=F32)

        def finish(acc):
            if has_add:
                acc = acc + add_scale * add_ref[...].astype(F32)
            o_ref[...] = acc.astype(o_ref.dtype)

        if nk == 1:
            finish(part)
        else:
            acc_ref = scr[0]
            kk = pl.program_id(2)

            @pl.when(kk == 0)
            def _():
                acc_ref[...] = part

            @pl.when(kk > 0)
            def _():
                acc_ref[...] += part

            @pl.when(kk == nk - 1)
            def _():
                finish(acc_ref[...])

    if mode == "nn":
        a_spec = pl.BlockSpec((tm, tk), lambda i, j, kk: (i, kk))
        b_spec = pl.BlockSpec((tk, tn), lambda i, j, kk: (kk, j))
    elif mode == "nt":
        a_spec = pl.BlockSpec((tm, tk), lambda i, j, kk: (i, kk))
        b_spec = pl.BlockSpec((tn, tk), lambda i, j, kk: (j, kk))
    else:
        a_spec = pl.BlockSpec((tk, tm), lambda i, j, kk: (kk, i))
        b_spec = pl.BlockSpec((tk, tn), lambda i, j, kk: (kk, j))
    o_spec = pl.BlockSpec((tm, tn), lambda i, j, kk: (i, j))
    out_shape = jax.ShapeDtypeStruct((m, n), out_dtype)
    if n_slabs > 1:
        assert not has_add
        o_spec = pl.BlockSpec((None, tm, tn), lambda i, j, kk: (j // per_slab, i, j % per_slab))
        out_shape = jax.ShapeDtypeStruct((n_slabs, m, n // n_slabs), out_dtype)
    in_specs = [a_spec, b_spec] + ([o_spec] if has_add else [])
    args = (a, b) + ((add,) if has_add else ())
    osz = jnp.dtype(out_dtype).itemsize
    vmem = (2 * (tm * tk * a.dtype.itemsize + tk * tn * b.dtype.itemsize) + 2 * tm * tn * osz
            + (2 * tm * tn * add.dtype.itemsize if has_add else 0) + 2 * tm * tn * 4
            + 2 * (tm * tk + tk * tn) + (8 << 20))
    return pl.pallas_call(
        body, name=name, grid=(m // tm, n // tn, nk),
        in_specs=in_specs, out_specs=o_spec, out_shape=out_shape,
        scratch_shapes=[pltpu.VMEM((tm, tn), F32)] if nk > 1 else [],
        compiler_params=_params(("parallel", "parallel", "arbitrary"), vmem),
    )(*args)


def _mm_swiglu(a, b, name, chip_idx=None, done=None):
    m, k = a.shape
    w = SHARD_H
    tm = _divtile(m, 512)

    def body(*refs):
        a_ref, b_ref = refs[-4 if done is None else -6], refs[-3 if done is None else -5]
        gu_ref, act_ref = refs[-2:]
        gu = jnp.dot(a_ref[...], b_ref[...], preferred_element_type=F32)
        g = gu[:, :w]
        gu_ref[...] = gu.astype(BF16)
        act_ref[...] = (g * _sigmoid(g) * gu[:, w:]).astype(BF16)

    out_shape = [jax.ShapeDtypeStruct((m, 2 * FFN_H), BF16), jax.ShapeDtypeStruct((m, FFN_H), BF16)]
    cp = _params(("parallel", "parallel"), 56 << 20)
    if chip_idx is None:
        return pl.pallas_call(
            body, name=name, grid=(N_CHIPS, m // tm),
            in_specs=[pl.BlockSpec((tm, k), lambda j, i: (i, 0)), pl.BlockSpec((k, 2 * w), lambda j, i: (0, j))],
            out_specs=[pl.BlockSpec((tm, 2 * w), lambda j, i: (i, j)), pl.BlockSpec((tm, w), lambda j, i: (i, j))],
            out_shape=out_shape, compiler_params=cp,
        )(a, b)
    first = done is None
    tile = (lambda j, c: c[0]) if first else (lambda j, c: (c[0] + 1 + j) % N_CHIPS)
    in_specs = [pl.BlockSpec((tm, k), lambda j, i, c: (i, 0)),
                pl.BlockSpec((k, 2 * w), (lambda j, i, c: (0, 0)) if first else (lambda j, i, c: (0, tile(j, c))))]
    return pl.pallas_call(
        body, name=name,
        grid_spec=pltpu.PrefetchScalarGridSpec(
            num_scalar_prefetch=1, grid=(1 if first else N_CHIPS - 1, m // tm),
            in_specs=in_specs + ([] if first else [pl.BlockSpec(memory_space=pl.ANY)] * 2),
            out_specs=[pl.BlockSpec((tm, 2 * w), lambda j, i, c: (i, tile(j, c))),
                       pl.BlockSpec((tm, w), lambda j, i, c: (i, tile(j, c)))]),
        out_shape=out_shape, compiler_params=cp,
        input_output_aliases={} if first else {3: 0, 4: 1},
    )(chip_idx, a, b, *(() if first else done))


def _mm_swiglu_bwd(dr, wd, gu, name):
    m, k = dr.shape
    w = SHARD_H
    tm = _divtile(m, 512)

    def body(dr_ref, wd_ref, gu_ref, o_ref):
        d = lax.dot_general(dr_ref[...], wd_ref[...], NT_DIMS, preferred_element_type=F32)
        g = gu_ref[:, :w].astype(F32)
        u = gu_ref[:, w:].astype(F32)
        s = _sigmoid(g)
        o_ref[:, :w] = (d * u * (s * (1.0 + g * (1.0 - s)))).astype(BF16)
        o_ref[:, w:] = (d * (g * s)).astype(BF16)

    return pl.pallas_call(
        body, name=name, grid=(N_CHIPS, m // tm),
        in_specs=[pl.BlockSpec((tm, k), lambda j, i: (i, 0)), pl.BlockSpec((w, k), lambda j, i: (j, 0)),
                  pl.BlockSpec((tm, 2 * w), lambda j, i: (i, j))],
        out_specs=pl.BlockSpec((tm, 2 * w), lambda j, i: (i, j)),
        out_shape=jax.ShapeDtypeStruct((m, 2 * FFN_H), BF16),
        compiler_params=_params(("parallel", "parallel"), 48 << 20),
    )(dr, wd, gu)


def _ln_fwd(base, f, g, b, c, name, target=None):
    t = base.shape[0]
    tt = _divtile(t, 256)
    with_loss = target is not None

    def body(*refs):
        if with_loss:
            base_ref, f_ref, g_ref, b_ref, tg_ref, h_ref, hb_ref, xh_ref, rs_ref, dh_ref, loss_ref = refs
        else:
            base_ref, f_ref, g_ref, b_ref, h_ref, hb_ref, xh_ref, rs_ref = refs
        r = ALPHA * base_ref[...] + c * f_ref[...]
        mu = jnp.mean(r, axis=-1, keepdims=True)
        xc = r - mu
        var = jnp.mean(xc * xc, axis=-1, keepdims=True)
        rstd = lax.rsqrt(var + LN_EPS)
        xh = xc * rstd
        h = xh * g_ref[...] + b_ref[...]
        h_ref[...] = h
        hb_ref[...] = h.astype(BF16)
        xh_ref[...] = xh
        rs_ref[...] = rstd
        if with_loss:
            e = h - tg_ref[...]
            dh_ref[...] = e * (1.0 / D)
            part = 0.5 * jnp.sum(jnp.sum(e * e, axis=-1, keepdims=True) * (1.0 / D), axis=0, keepdims=True)

            @pl.when(pl.program_id(0) == 0)
            def _():
                loss_ref[...] = jnp.zeros_like(loss_ref)

            loss_ref[...] += part

    row = pl.BlockSpec((tt, D), lambda i: (i, 0))
    vec = pl.BlockSpec((1, D), lambda i: (0, 0))
    col = pl.BlockSpec((tt, 1), lambda i: (i, 0))
    in_specs = [row, row, vec, vec] + ([row] if with_loss else [])
    out_specs = [row, row, row, col] + ([row, pl.BlockSpec((1, 1), lambda i: (0, 0))] if with_loss else [])
    out_shape = [jax.ShapeDtypeStruct((t, D), F32), jax.ShapeDtypeStruct((t, D), BF16),
                 jax.ShapeDtypeStruct((t, D), F32), jax.ShapeDtypeStruct((t, 1), F32)]
    if with_loss:
        out_shape += [jax.ShapeDtypeStruct((t, D), F32), jax.ShapeDtypeStruct((1, 1), F32)]
    args = (base, f, g, b) + ((target,) if with_loss else ())
    return pl.pallas_call(
        body, name=name, grid=(t // tt,), in_specs=in_specs, out_specs=out_specs, out_shape=out_shape,
        compiler_params=_params(("arbitrary",) if with_loss else ("parallel",), 48 << 20),
    )(*args)


def _ln_bwd(dy, xh, rstd, g, c, name):
    t = dy.shape[0]
    tt = _divtile(t, 256)

    def body(dy_ref, xh_ref, rs_ref, g_ref, dr_ref, drb_ref, dg_ref, db_ref):
        dyv = dy_ref[...]
        xhv = xh_ref[...]
        dxh = dyv * g_ref[...]
        m1 = jnp.mean(dxh, axis=-1, keepdims=True)
        m2 = jnp.mean(dxh * xhv, axis=-1, keepdims=True)
        dr = rs_ref[...] * (dxh - m1 - xhv * m2)
        dr_ref[...] = dr
        drb_ref[...] = (c * dr).astype(BF16)

        @pl.when(pl.program_id(0) == 0)
        def _():
            dg_ref[...] = jnp.zeros_like(dg_ref)
            db_ref[...] = jnp.zeros_like(db_ref)

        dg_ref[...] += jnp.sum(dyv * xhv, axis=0, keepdims=True)
        db_ref[...] += jnp.sum(dyv, axis=0, keepdims=True)

    row = pl.BlockSpec((tt, D), lambda i: (i, 0))
    vec = pl.BlockSpec((1, D), lambda i: (0, 0))
    col = pl.BlockSpec((tt, 1), lambda i: (i, 0))
    return pl.pallas_call(
        body, name=name, grid=(t // tt,), in_specs=[row, row, col, vec], out_specs=[row, row, vec, vec],
        out_shape=[jax.ShapeDtypeStruct((t, D), F32), jax.ShapeDtypeStruct((t, D), BF16),
                   jax.ShapeDtypeStruct((1, D), F32), jax.ShapeDtypeStruct((1, D), F32)],
        compiler_params=_params(("arbitrary",), 40 << 20),
    )(dy, xh, rstd, g)


DT_BLK = SEGS["dt"][2] // 128


def _dt_prep(proj, bias128, alog128):
    t = proj.shape[0]
    tt = _divtile(t, 256)

    def body(p_ref, bias_ref, alog_ref, dt_ref, adt_ref):
        dtv = jax.nn.softplus(p_ref[...] + bias_ref[...])
        adt = dtv * (-jnp.exp(alog_ref[...]))
        for g in range(SSD_G):
            dt_ref[g] = dtv[:, SSD_R * g:SSD_R * (g + 1)]
            adt_ref[g] = adt[:, SSD_R * g:SSD_R * (g + 1)]

    blk3 = pl.BlockSpec((SSD_G, tt, SSD_R), lambda i: (0, i, 0))
    vec = pl.BlockSpec((1, 128), lambda i: (0, 0))
    return pl.pallas_call(
        body, name="dt_prep", grid=(t // tt,),
        in_specs=[pl.BlockSpec((tt, 128), lambda i: (i, DT_BLK)), vec, vec], out_specs=[blk3, blk3],
        out_shape=[jax.ShapeDtypeStruct((SSD_G, t, SSD_R), F32)] * 2,
        compiler_params=_params(("parallel",), 16 << 20),
    )(proj, bias128, alog128)


def _dt_bwd(dadt3, dxdx3, proj, bias128, alog128):
    t = proj.shape[0]
    tt = _divtile(t, 256)

    def token_major(ref):
        parts = [ref[g] for g in range(SSD_G)] + [jnp.zeros((tt, 128 - SSD_HEADS), F32)]
        return jnp.concatenate(parts, axis=1)

    def body(dadt_ref, dxdx_ref, p_ref, bias_ref, alog_ref, o_ref, dbias_ref, dalog_ref):
        pre = p_ref[...] + bias_ref[...]
        dtv = jax.nn.softplus(pre)
        a = -jnp.exp(alog_ref[...])
        dadt = token_major(dadt_ref)
        ddt = a * dadt + token_major(dxdx_ref)
        draw = ddt * _sigmoid(pre)
        o_ref[...] = draw.astype(BF16)

        @pl.when(pl.program_id(0) == 0)
        def _():
            dbias_ref[...] = jnp.zeros_like(dbias_ref)
            dalog_ref[...] = jnp.zeros_like(dalog_ref)

        dbias_ref[...] += jnp.sum(draw, axis=0, keepdims=True)
        dalog_ref[...] += jnp.sum(dadt * dtv * a, axis=0, keepdims=True)

    blk = pl.BlockSpec((tt, 128), lambda i: (i, 0))
    blk3 = pl.BlockSpec((SSD_G, tt, SSD_R), lambda i: (0, i, 0))
    vec = pl.BlockSpec((1, 128), lambda i: (0, 0))
    return pl.pallas_call(
        body, name="dt_bwd", grid=(t // tt,),
        in_specs=[blk3, blk3, pl.BlockSpec((tt, 128), lambda i: (i, DT_BLK)), vec, vec],
        out_specs=[blk, vec, vec],
        out_shape=[jax.ShapeDtypeStruct((t, 128), BF16), jax.ShapeDtypeStruct((1, 128), F32),
                   jax.ShapeDtypeStruct((1, 128), F32)],
        compiler_params=_params(("arbitrary",), 16 << 20),
    )(dadt3, dxdx3, proj, bias128, alog128)


CONV_CB = 128
CONV_TT = 2048


def _shift_down(cur, prev8, s):
    if s == 0:
        return cur
    rolled = pltpu.roll(cur, s, 0)
    head = pltpu.roll(prev8, s, 0)
    r8 = lax.broadcasted_iota(jnp.int32, (8, 1), 0)
    top = jnp.where(r8 < s, head, rolled[:8])
    return jnp.concatenate([top, rolled[8:]], axis=0)


def _shift_up(cur, next8, s):
    if s == 0:
        return cur
    n = cur.shape[0]
    rolled = pltpu.roll(cur, n - s, 0)
    tail = pltpu.roll(next8, 8 - s, 0)
    r8 = lax.broadcasted_iota(jnp.int32, (8, 1), 0)
    bot = jnp.where(r8 >= 8 - s, tail, rolled[n - 8:])
    return jnp.concatenate([rolled[:n - 8], bot], axis=0)


def _conv_fwd(proj, conv_w, conv_b):
    t = proj.shape[0]
    tt = _divtile(t, CONV_TT)
    base = SEGS["xbc"][2] // CONV_CB
    r8 = tt // 8

    def body(u_ref, up_ref, w_ref, b_ref, o_ref):
        cur = u_ref[...]
        prev8 = jnp.where(pl.program_id(1) > 0, up_ref[...], 0.0)
        acc = b_ref[...] + w_ref[3:4, :] * cur
        for k in range(3):
            acc = acc + w_ref[k:k + 1, :] * _shift_down(cur, prev8, 3 - k)
        o_ref[...] = acc * _sigmoid(acc)

    return pl.pallas_call(
        body, name="conv_fwd", grid=(XBC // CONV_CB, t // tt),
        in_specs=[pl.BlockSpec((tt, CONV_CB), lambda c, i: (i, base + c)),
                  pl.BlockSpec((8, CONV_CB), lambda c, i: (jnp.maximum(i * r8 - 1, 0), base + c)),
                  pl.BlockSpec((4, CONV_CB), lambda c, i: (0, c)),
                  pl.BlockSpec((1, CONV_CB), lambda c, i: (0, c))],
        out_specs=pl.BlockSpec((tt, CONV_CB), lambda c, i: (i, c)),
        out_shape=jax.ShapeDtypeStruct((t, XBC), F32),
        compiler_params=_params(("parallel", "parallel"), 24 << 20),
    )(proj, proj, conv_w, conv_b)


def _conv_bwd(proj, dout, conv_w, conv_b, col0, width, name, dproj, skip=None):
    t = proj.shape[0]
    tt = _divtile(t, CONV_TT)
    nt = t // tt
    base = SEGS["xbc"][2] // CONV_CB + col0 // CONV_CB
    wb = col0 // CONV_CB
    r8 = tt // 8
    has_skip = skip is not None

    def body(*refs):
        if has_skip:
            u_ref, up_ref, d_ref, w_ref, b_ref, _, sk_ref, skw_ref, du_ref, dw_ref, db_ref, nx_ref = refs
        else:
            u_ref, up_ref, d_ref, w_ref, b_ref, _, du_ref, dw_ref, db_ref, nx_ref = refs
        i = pl.program_id(1)
        cur = u_ref[...]
        prev8 = jnp.where(i < nt - 1, up_ref[...], 0.0)
        sh = [_shift_down(cur, prev8, 3 - k) for k in range(3)] + [cur]
        pre = b_ref[...]
        for k in range(4):
            pre = pre + w_ref[k:k + 1, :] * sh[k]
        sg = _sigmoid(pre)
        dout_v = d_ref[...]
        if has_skip:
            dout_v = dout_v + sk_ref[...] * skw_ref[...]
        dpre = dout_v * (sg * (1.0 + pre * (1.0 - sg)))

        @pl.when(i == 0)
        def _():
            nx_ref[...] = jnp.zeros_like(nx_ref)
            dw_ref[...] = jnp.zeros_like(dw_ref)
            db_ref[...] = jnp.zeros_like(db_ref)

        next8 = nx_ref[...]
        du = w_ref[3:4, :] * dpre
        for s in range(1, 4):
            du = du + w_ref[3 - s:4 - s, :] * _shift_up(dpre, next8, s)
        du_ref[...] = du.astype(BF16)
        nx_ref[...] = dpre[:8]
        rows = [jnp.sum(dpre * sh[k], axis=0, keepdims=True) for k in range(4)]
        dw_ref[...] += jnp.concatenate(rows + [jnp.zeros((4, CONV_CB), F32)], axis=0)
        db_ref[...] += jnp.sum(dpre, axis=0, keepdims=True)

    rev = lambda c, i: (nt - 1 - i, c)
    p_in, p_out, p_shape = _dproj_piece(t, tt, CONV_CB, lambda c, i: (nt - 1 - i, base + c))
    in_specs = [pl.BlockSpec((tt, CONV_CB), lambda c, i: (nt - 1 - i, base + c)),
                pl.BlockSpec((8, CONV_CB), lambda c, i: (jnp.maximum((nt - 1 - i) * r8 - 1, 0), base + c)),
                pl.BlockSpec((tt, CONV_CB), rev),
                pl.BlockSpec((4, CONV_CB), lambda c, i: (0, wb + c)),
                pl.BlockSpec((1, CONV_CB), lambda c, i: (0, wb + c)), p_in]
    args = [proj, proj, dout, conv_w, conv_b, dproj]
    if has_skip:
        in_specs += [pl.BlockSpec((tt, CONV_CB), rev), pl.BlockSpec((1, CONV_CB), lambda c, i: (0, c))]
        args += [skip[0], skip[1]]
    return pl.pallas_call(
        body, name=name, grid=(width // CONV_CB, nt),
        in_specs=in_specs,
        out_specs=[p_out, pl.BlockSpec((8, CONV_CB), lambda c, i: (0, c)),
                   pl.BlockSpec((1, CONV_CB), lambda c, i: (0, c))],
        out_shape=[p_shape, jax.ShapeDtypeStruct((8, width), F32), jax.ShapeDtypeStruct((1, width), F32)],
        input_output_aliases={5: 0},
        scratch_shapes=[pltpu.VMEM((8, CONV_CB), F32)],
        compiler_params=_params(("parallel", "arbitrary"), 32 << 20),
    )(*args)


GW = SSD_R * SSD_P


def _expand8(v, passes=2):
    r = v.shape[0]
    if r < 8:
        v = jnp.broadcast_to(v, (8, SSD_R))
    ri = lax.broadcasted_iota(jnp.int32, (SSD_R, GW), 0)
    ci = lax.broadcasted_iota(jnp.int32, (SSD_R, GW), 1)
    spread = jnp.where((ci >= ri * SSD_P) & (ci < (ri + 1) * SSD_P), 1.0, 0.0)
    return _dot01(v, spread, passes)[:r]


def _head_pair_split(tile):
    first = lax.broadcasted_iota(jnp.int32, (1, 2 * SSD_P), 1) < SSD_P
    return jnp.where(first, tile, 0.0), jnp.where(first, 0.0, tile)


def _sel(rows, group):
    ri = lax.broadcasted_iota(jnp.int32, (rows, rows // group), 0)
    ci = lax.broadcasted_iota(jnp.int32, (rows, rows // group), 1)
    lo = ci * group
    return jnp.where((ri >= lo) & (ri < lo + group), 1.0, 0.0).astype(F32)


def _dot01(lhs, rhs, passes, split_lhs=True, dims=((1,), (0,))):
    val, m01 = (lhs, rhs) if split_lhs else (rhs, lhs)
    m01 = m01.astype(BF16)
    out = None
    for p in range(passes):
        piece = val.astype(BF16)
        ops = (piece, m01) if split_lhs else (m01, piece)
        d = lax.dot_general(ops[0], ops[1], (dims, ((), ())), preferred_element_type=F32)
        out = d if out is None else out + d
        if p + 1 < passes:
            val = val - piece.astype(F32)
    return out


def _ssd_chunk_terms(adt):
    li = lax.broadcasted_iota(jnp.int32, (CHUNK, CHUNK), 0)
    si = lax.broadcasted_iota(jnp.int32, (CHUNK, CHUNK), 1)
    causal = li >= si
    a_cs = _dot01(jnp.where(causal, 1.0, 0.0), adt, 3, split_lhs=False)
    a_cs_t = _dot01(adt, jnp.where(li <= si, 1.0, 0.0), 3, dims=((0,), (0,)))
    return a_cs, a_cs_t, causal


def _ssd_fwd(xc, dt3, adt3):
    t = xc.shape[0]
    nc = t // CHUNK

    gs = 2

    def body(xs_ref, b_ref, c_ref, dt_ref, adt_ref, y_ref, hp_ref, h_ref):
        @pl.when(pl.program_id(1) == 0)
        def _():
            h_ref[...] = jnp.zeros_like(h_ref)

        for gg in range(gs):
            a_cs, a_cs_t, causal = _ssd_chunk_terms(adt_ref[gg])
            a_last = a_cs[CHUNK - 1:CHUNK, :]
            h = h_ref[gg]
            hp_ref[gg, 0] = h
            xd = xs_ref[:, GW * gg:GW * (gg + 1)] * _expand8(dt_ref[gg])
            bb = b_ref[:, SSD_N * gg:SSD_N * (gg + 1)].astype(BF16)
            cbf = c_ref[:, SSD_N * gg:SSD_N * (gg + 1)].astype(BF16)
            cb = lax.dot_general(cbf, bb, (((1,), (1,)), ((), ())), preferred_element_type=F32)
            yoff = jnp.dot(cbf, h.astype(BF16), preferred_element_type=F32) * _expand8(jnp.exp(a_cs))
            for q in range(SSD_R // 2):
                lmats = []
                for r in (2 * q, 2 * q + 1):
                    seg = jnp.exp(jnp.where(causal, a_cs[:, r:r + 1] - a_cs_t[r:r + 1, :], -jnp.inf))
                    lmats.append((cb * seg).astype(BF16))
                tile = slice(2 * SSD_P * q, 2 * SSD_P * (q + 1))
                xa, xb = _head_pair_split(xd[:, tile])
                y_ref[:, GW * gg + 2 * SSD_P * q:GW * gg + 2 * SSD_P * (q + 1)] = (
                    jnp.dot(jnp.concatenate(lmats, axis=1), jnp.concatenate([xa, xb], axis=0).astype(BF16),
                            preferred_element_type=F32) + yoff[:, tile])
            xdd = (xd * _expand8(jnp.exp(a_last - a_cs))).astype(BF16)
            h_ref[gg] = _expand8(jnp.exp(a_last), 3) * h + lax.dot_general(
                bb, xdd, (((0,), (0,)), ((), ())), preferred_element_type=F32)

    nb = SSD_INNER // (gs * SSD_N)
    return pl.pallas_call(
        body, name="ssd_fwd", grid=(SSD_G // gs, nc),
        in_specs=[pl.BlockSpec((CHUNK, gs * GW), lambda g, c: (c, g)),
                  pl.BlockSpec((CHUNK, gs * SSD_N), lambda g, c: (c, nb + g)),
                  pl.BlockSpec((CHUNK, gs * SSD_N), lambda g, c: (c, nb + SSD_G // gs + g)),
                  pl.BlockSpec((gs, CHUNK, SSD_R), lambda g, c: (g, c, 0)),
                  pl.BlockSpec((gs, CHUNK, SSD_R), lambda g, c: (g, c, 0))],
        out_specs=[pl.BlockSpec((CHUNK, gs * GW), lambda g, c: (c, g)),
                   pl.BlockSpec((gs, 1, SSD_N, GW), lambda g, c: (g, c, 0, 0))],
        out_shape=[jax.ShapeDtypeStruct((t, SSD_INNER), F32), jax.ShapeDtypeStruct((SSD_G, nc, SSD_N, GW), F32)],
        scratch_shapes=[pltpu.VMEM((gs, SSD_N, GW), F32)],
        compiler_params=_params(("parallel", "arbitrary"), 32 << 20),
    )(xc, xc, xc, dt3, adt3)


def _ssd_bwd(xc, dt3, adt3, hprev, dy):
    t = xc.shape[0]
    nc = t // CHUNK

    gs = 2

    def body(xs_ref, b_ref, c_ref, dt_ref, adt_ref, hp_ref, dy_ref,
             dx_ref, db_ref, dc_ref, dadt_ref, dxdx_ref, dh_ref):
        @pl.when(pl.program_id(1) == 0)
        def _():
            dh_ref[...] = jnp.zeros_like(dh_ref)

        for gg in range(gs):
            wide = slice(GW * gg, GW * (gg + 1))
            narrow = slice(SSD_N * gg, SSD_N * (gg + 1))
            dx, db, dc, dadt, dxdx, dh_new = group_bwd(
                xs_ref[:, wide], b_ref[:, narrow], c_ref[:, narrow], dt_ref[gg], adt_ref[gg], hp_ref[gg, 0],
                dy_ref[:, wide], dh_ref[gg])
            dx_ref[:, wide] = dx
            db_ref[:, narrow] = db
            dc_ref[:, narrow] = dc
            dadt_ref[gg] = dadt
            dxdx_ref[gg] = dxdx
            dh_ref[gg] = dh_new

    def group_bwd(xs, b, c, dt, adt, hp, dyv, dh):
        a_cs, a_cs_t, causal = _ssd_chunk_terms(adt)
        a_last = a_cs[CHUNK - 1:CHUNK, :]
        e_last = jnp.exp(a_last)
        ex = _expand8(jnp.exp(a_cs))
        dtex = _expand8(jnp.exp(a_last - a_cs))
        dtx = _expand8(dt)
        sel = _sel(GW, SSD_P)
        seg8 = lambda v: _dot01(v, sel, 2)

        xd = xs * dtx
        xdd = xd * dtex
        bb = b.astype(BF16)
        cbf = c.astype(BF16)
        hpb = hp.astype(BF16)
        dhb = dh.astype(BF16)
        xdb = xd.astype(BF16)
        cb = lax.dot_general(cbf, bb, (((1,), (1,)), ((), ())), preferred_element_type=F32)
        dye = (dyv * ex).astype(BF16)
        yoff = jnp.dot(cbf, hpb, preferred_element_type=F32) * ex
        dc = lax.dot_general(dye, hpb, (((1,), (1,)), ((), ())), preferred_element_type=F32)
        bdh = jnp.dot(bb, dhb, preferred_element_type=F32)
        db = lax.dot_general(xdd.astype(BF16), dhb, (((1,), (1,)), ((), ())), preferred_element_type=F32)
        dxd_state = bdh * dtex
        q_terms = xdd * bdh
        d_a = seg8(dyv * yoff - q_terms)
        d_a_last = seg8(jnp.sum(q_terms, axis=0, keepdims=True)
                        + _expand8(e_last, 3) * jnp.sum(hp * dh, axis=0, keepdims=True))
        dh_new = (lax.dot_general(cbf, dye, (((0,), (0,)), ((), ())), preferred_element_type=F32)
                  + _expand8(e_last, 3) * dh)
        dcb = jnp.zeros((CHUNK, CHUNK), F32)
        w_all = []
        dxd_parts = []
        for q2 in range(SSD_R // 2):
            tile = slice(2 * SSD_P * q2, 2 * SSD_P * (q2 + 1))
            dy_pair = [part.astype(BF16) for part in _head_pair_split(dyv[:, tile])]
            lmats = []
            for k, r in enumerate((2 * q2, 2 * q2 + 1)):
                seg = jnp.exp(jnp.where(causal, a_cs[:, r:r + 1] - a_cs_t[r:r + 1, :], -jnp.inf))
                lmat = cb * seg
                dm = lax.dot_general(dy_pair[k], xdb[:, tile], (((1,), (1,)), ((), ())), preferred_element_type=F32)
                dcb = dcb + dm * seg
                w_all.append(dm * lmat)
                lmats.append(lmat.astype(BF16))
            dxd_parts.append(lax.dot_general(jnp.concatenate(lmats, axis=0), jnp.concatenate(dy_pair, axis=0),
                                             (((0,), (0,)), ((), ())), preferred_element_type=F32))
        row_sums = _dot01(jnp.concatenate(w_all, axis=1), _sel(SSD_R * CHUNK, CHUNK), 2)
        cs_rows = jnp.concatenate([jnp.sum(wr, axis=0, keepdims=True) for wr in w_all], axis=0)
        col_sums = _dot01(cs_rows, _sel(SSD_R, 1), 3, dims=((0,), (0,)))
        d_a = d_a + row_sums - col_sums
        li = lax.broadcasted_iota(jnp.int32, (CHUNK, SSD_R), 0)
        d_a = d_a + jnp.where(li == CHUNK - 1, d_a_last, 0.0)
        l2 = lax.broadcasted_iota(jnp.int32, (CHUNK, CHUNK), 0)
        s2 = lax.broadcasted_iota(jnp.int32, (CHUNK, CHUNK), 1)
        dadt = _dot01(jnp.where(s2 >= l2, 1.0, 0.0), d_a, 3, split_lhs=False)
        dxd = dxd_state + jnp.concatenate(dxd_parts, axis=1)
        dcbb = dcb.astype(BF16)
        db = db + lax.dot_general(dcbb, cbf, (((0,), (0,)), ((), ())), preferred_element_type=F32)
        dc = dc + jnp.dot(dcbb, bb, preferred_element_type=F32)
        return dxd * dtx, db, dc, dadt, seg8(dxd * xs), dh_new

    nb = SSD_INNER // (gs * SSD_N)
    rc = lambda g, c: (nc - 1 - c, g)
    r3 = lambda g, c: (g, nc - 1 - c, 0)
    return pl.pallas_call(
        body, name="ssd_bwd", grid=(SSD_G // gs, nc),
        in_specs=[pl.BlockSpec((CHUNK, gs * GW), rc),
                  pl.BlockSpec((CHUNK, gs * SSD_N), lambda g, c: (nc - 1 - c, nb + g)),
                  pl.BlockSpec((CHUNK, gs * SSD_N), lambda g, c: (nc - 1 - c, nb + SSD_G // gs + g)),
                  pl.BlockSpec((gs, CHUNK, SSD_R), r3),
                  pl.BlockSpec((gs, CHUNK, SSD_R), r3),
                  pl.BlockSpec((gs, 1, SSD_N, GW), lambda g, c: (g, nc - 1 - c, 0, 0)),
                  pl.BlockSpec((CHUNK, gs * GW), rc)],
        out_specs=[pl.BlockSpec((CHUNK, gs * GW), rc),
                   pl.BlockSpec((CHUNK, gs * SSD_N), rc),
                   pl.BlockSpec((CHUNK, gs * SSD_N), rc),
                   pl.BlockSpec((gs, CHUNK, SSD_R), r3),
                   pl.BlockSpec((gs, CHUNK, SSD_R), r3)],
        out_shape=[jax.ShapeDtypeStruct((t, SSD_INNER), F32),
                   jax.ShapeDtypeStruct((t, SSD_G * SSD_N), F32),
                   jax.ShapeDtypeStruct((t, SSD_G * SSD_N), F32),
                   jax.ShapeDtypeStruct((SSD_G, t, SSD_R), F32),
                   jax.ShapeDtypeStruct((SSD_G, t, SSD_R), F32)],
        scratch_shapes=[pltpu.VMEM((gs, SSD_N, GW), F32)],
        compiler_params=_params(("parallel", "arbitrary"), 48 << 20),
    )(xc, xc, xc, dt3, adt3, hprev, dy)


def _gated_norm_fwd(y, xc, proj, dexp, ng):
    t = y.shape[0]
    tt = _divtile(t, 256)

    def body(y_ref, x_ref, z_ref, d_ref, g_ref, o_ref):
        z = z_ref[...]
        y2 = (y_ref[...] + d_ref[...] * x_ref[...]) * (z * _sigmoid(z))
        for gi in range(SSD_G):
            sl = slice(GW * gi, GW * (gi + 1))
            seg = y2[:, sl]
            rinv = lax.rsqrt(jnp.mean(seg * seg, axis=-1, keepdims=True) + RMS_EPS)
            o_ref[:, sl] = (seg * rinv * g_ref[:, sl]).astype(BF16)

    row = pl.BlockSpec((tt, SSD_INNER), lambda i: (i, 0))
    vec = pl.BlockSpec((1, SSD_INNER), lambda i: (0, 0))
    return pl.pallas_call(
        body, name="gated_norm_fwd", grid=(t // tt,), in_specs=[row, row, row, vec, vec], out_specs=row,
        out_shape=jax.ShapeDtypeStruct((t, SSD_INNER), BF16),
        compiler_params=_params(("parallel",), 48 << 20),
    )(y, xc, proj, dexp, ng)


def _gated_norm_bwd(dout, y, xc, proj, dexp, ng, dproj):
    t = y.shape[0]
    tt = _divtile(t, 128)

    def body(do_ref, y_ref, x_ref, z_ref, d_ref, g_ref, _, dz_ref, dy_ref, dg_ref, dd_ref):
        z = z_ref[...]
        sg = _sigmoid(z)
        sz = z * sg
        xs = x_ref[...]
        y1 = y_ref[...] + d_ref[...] * xs
        y2 = y1 * sz
        dov = do_ref[...]

        @pl.when(pl.program_id(0) == 0)
        def _():
            dg_ref[...] = jnp.zeros_like(dg_ref)
            dd_ref[...] = jnp.zeros_like(dd_ref)

        for gi in range(SSD_G):
            sl = slice(GW * gi, GW * (gi + 1))
            seg = y2[:, sl]
            rinv = lax.rsqrt(jnp.mean(seg * seg, axis=-1, keepdims=True) + RMS_EPS)
            yn = seg * rinv
            dsl = dov[:, sl]
            dg_ref[:, sl] += jnp.sum(dsl * yn, axis=0, keepdims=True)
            dyn = dsl * g_ref[:, sl]
            dy2 = rinv * (dyn - yn * jnp.mean(dyn * yn, axis=-1, keepdims=True))
            dz_ref[:, sl] = (dy2 * y1[:, sl] * (sg[:, sl] * (1.0 + z[:, sl] * (1.0 - sg[:, sl])))).astype(BF16)
            dy1 = dy2 * sz[:, sl]
            dy_ref[:, sl] = dy1
            dd_ref[:, sl] += jnp.sum(dy1 * xs[:, sl], axis=0, keepdims=True)

    row = pl.BlockSpec((tt, SSD_INNER), lambda i: (i, 0))
    vec = pl.BlockSpec((1, SSD_INNER), lambda i: (0, 0))
    p_in, p_out, p_shape = _dproj_piece(t, tt, SSD_INNER, lambda i: (i, 0))
    return pl.pallas_call(
        body, name="gated_norm_bwd", grid=(t // tt,), in_specs=[row, row, row, row, vec, vec, p_in],
        out_specs=[p_out, row, vec, vec],
        out_shape=[p_shape, jax.ShapeDtypeStruct((t, SSD_INNER), F32),
                   jax.ShapeDtypeStruct((1, SSD_INNER), F32), jax.ShapeDtypeStruct((1, SSD_INNER), F32)],
        input_output_aliases={6: 0},
        compiler_params=_params(("arbitrary",), 48 << 20),
    )(dout, y, xc, proj, dexp, ng, dproj)


def _fold_heads(v, name):
    def body(v_ref, o_ref):
        ri = lax.broadcasted_iota(jnp.int32, (SSD_INNER, 128), 0)
        ci = lax.broadcasted_iota(jnp.int32, (SSD_INNER, 128), 1)
        fold = jnp.where((ri >= ci * SSD_P) & (ri < (ci + 1) * SSD_P), 1.0, 0.0).astype(F32)
        o_ref[...] = jnp.dot(v_ref[...], fold, preferred_element_type=F32, precision=HI)

    return pl.pallas_call(body, name=name, out_shape=jax.ShapeDtypeStruct((1, 128), F32))(v)


Q_BLK = SEGS["q"][2] // QW
K_BLK = SEGS["k"][2] // KVW
V_BLK = SEGS["v"][2] // KVW


def _rope_tables(pos_ref, invf_ref, width):
    ang = pos_ref[...] * invf_ref[...]
    lane = lax.broadcasted_iota(jnp.int32, (1, 128), 1)
    sign = jnp.where((lane % HD) < (HD // 2), -1.0, 1.0)
    cos = jnp.tile(jnp.cos(ang), (1, width // 128))
    sin = jnp.tile(sign * jnp.sin(ang), (1, width // 128))
    first = (lax.broadcasted_iota(jnp.int32, (1, width), 1) % HD) < (HD // 2)
    return cos, sin, first


def _rot_half(u, first):
    w = u.shape[1]
    return jnp.where(first, pltpu.roll(u, w - HD // 2, 1), pltpu.roll(u, HD // 2, 1))


def _rope_fwd(proj, pos, invf):
    t = proj.shape[0]
    tt = _divtile(t, 512)

    def body(q_ref, k_ref, pos_ref, invf_ref, qo_ref, ko_ref):
        cos, sin, first = _rope_tables(pos_ref, invf_ref, QW)
        q = q_ref[...]
        qr = q * cos + _rot_half(q, first) * sin
        for p in range(QW // 128):
            qo_ref[128 * p:128 * (p + 1), :] = qr[:, 128 * p:128 * (p + 1)].T.astype(BF16)
        k = k_ref[...]
        ko_ref[...] = (k * cos[:, :KVW] + _rot_half(k, first[:, :KVW]) * sin[:, :KVW]).astype(BF16)

    return pl.pallas_call(
        body, name="rope_fwd", grid=(t // tt,),
        in_specs=[pl.BlockSpec((tt, QW), lambda i: (i, Q_BLK)), pl.BlockSpec((tt, KVW), lambda i: (i, K_BLK)),
                  pl.BlockSpec((tt, 1), lambda i: (i, 0)), pl.BlockSpec((1, 128), lambda i: (0, 0))],
        out_specs=[pl.BlockSpec((QW, tt), lambda i: (0, i)), pl.BlockSpec((tt, KVW), lambda i: (i, 0))],
        out_shape=[jax.ShapeDtypeStruct((QW, t), BF16), jax.ShapeDtypeStruct((t, KVW), BF16)],
        compiler_params=_params(("parallel",), 40 << 20),
    )(proj, proj, pos, invf)


def _rope_bwd(dqt, dk, pos, invf, dproj):
    t = dk.shape[0]
    tt = _divtile(t, 512)

    def body(dq_ref, dk_ref, pos_ref, invf_ref, _, qo_ref, ko_ref):
        cos, sin, first = _rope_tables(pos_ref, invf_ref, QW)
        q = jnp.concatenate([dq_ref[128 * p:128 * (p + 1), :].T for p in range(QW // 128)], axis=1)
        qo_ref[...] = (q * cos + _rot_half(q * sin, first)).astype(BF16)
        k = dk_ref[...]
        ko_ref[...] = (k * cos[:, :KVW] + _rot_half(k * sin[:, :KVW], first[:, :KVW])).astype(BF16)

    p_in, p_out, p_shape = _dproj_piece(t, tt, QW, lambda i: (i, Q_BLK))
    return pl.pallas_call(
        body, name="rope_bwd", grid=(t // tt,),
        in_specs=[pl.BlockSpec((QW, tt), lambda i: (0, i)), pl.BlockSpec((tt, KVW), lambda i: (i, 0)),
                  pl.BlockSpec((tt, 1), lambda i: (i, 0)), pl.BlockSpec((1, 128), lambda i: (0, 0)), p_in],
        out_specs=[p_out, pl.BlockSpec((tt, KVW), lambda i: (i, 0))],
        out_shape=[p_shape, jax.ShapeDtypeStruct((t, KVW), BF16)],
        input_output_aliases={4: 0},
        compiler_params=_params(("parallel",), 40 << 20),
    )(dqt, dk, pos, invf, dproj)


def _place_cols(piece, dproj, col_blk, name):
    t, w = piece.shape
    tt = _divtile(t, 1024)

    def body(p_ref, _, o_ref):
        o_ref[...] = p_ref[...]

    p_in, p_out, p_shape = _dproj_piece(t, tt, w, lambda i: (i, col_blk))
    return pl.pallas_call(
        body, name=name, grid=(t // tt,),
        in_specs=[pl.BlockSpec((tt, w), lambda i: (i, 0)), p_in], out_specs=p_out, out_shape=p_shape,
        input_output_aliases={1: 0},
        compiler_params=_params(("parallel",), 16 << 20),
    )(piece, dproj)


GQ = NQ // NKV
NT_DIMS = (((1,), (1,)), ((), ()))
TN_DIMS = (((0,), (0,)), ((), ()))


def _attn_heads(ref, j, dtype=None):
    out = jnp.concatenate([ref[HD * h:HD * (h + 1), :] for h in range(j * GQ, (j + 1) * GQ)], axis=1)
    return out if dtype is None else out.astype(dtype)


def _attn_sink_row(s_ref, j):
    return jnp.concatenate([jnp.broadcast_to(s_ref[:, h:h + 1], (1, WINDOW)) for h in range(j * GQ, (j + 1) * GQ)],
                           axis=1)


def _attn_mask(n):
    kr = lax.broadcasted_iota(jnp.int32, (2 * WINDOW, GQ * WINDOW), 0)
    qi = lax.broadcasted_iota(jnp.int32, (2 * WINDOW, GQ * WINDOW), 1) % WINDOW
    return (kr > qi) & (kr <= qi + WINDOW) & ((n > 0) | (kr >= WINDOW))


def _attn_probs(qgt, kk, sink, mask):
    s = jnp.where(mask, jnp.dot(kk, qgt, preferred_element_type=F32) * (HD ** -0.5), -jnp.inf)
    m = jnp.maximum(jnp.max(s, axis=0, keepdims=True), sink)
    p = jnp.exp(s - m)
    ps = jnp.exp(sink - m)
    inv = 1.0 / (jnp.sum(p, axis=0, keepdims=True) + ps)
    return p * inv, ps * inv


def _attn_fwd(qt, kr, proj, sinks):
    t = kr.shape[0]
    nb = t // WINDOW

    def body(q_ref, kc_ref, kp_ref, vc_ref, vp_ref, s_ref, o_ref):
        mask = _attn_mask(pl.program_id(0))
        for j in range(NKV):
            ks = slice(HD * j, HD * (j + 1))
            kk = jnp.concatenate([kp_ref[:, ks], kc_ref[:, ks]], axis=0)
            vv = jnp.concatenate([vp_ref[:, ks], vc_ref[:, ks]], axis=0).astype(BF16)
            pn, _ = _attn_probs(_attn_heads(q_ref, j), kk, _attn_sink_row(s_ref, j), mask)
            ot = lax.dot_general(vv, pn.astype(BF16), TN_DIMS, preferred_element_type=F32).astype(BF16)
            for g in range(GQ):
                h = j * GQ + g
                o_ref[HD * h:HD * (h + 1), :] = ot[:, WINDOW * g:WINDOW * (g + 1)]

    prev = lambda n: (jnp.maximum(n - 1, 0), 0)
    return pl.pallas_call(
        body, name="attn_fwd", grid=(nb,),
        in_specs=[pl.BlockSpec((QW, WINDOW), lambda n: (0, n)),
                  pl.BlockSpec((WINDOW, KVW), lambda n: (n, 0)), pl.BlockSpec((WINDOW, KVW), prev),
                  pl.BlockSpec((WINDOW, KVW), lambda n: (n, V_BLK)),
                  pl.BlockSpec((WINDOW, KVW), lambda n: (jnp.maximum(n - 1, 0), V_BLK)),
                  pl.BlockSpec((1, 128), lambda n: (0, 0))],
        out_specs=pl.BlockSpec((QW, WINDOW), lambda n: (0, n)),
        out_shape=jax.ShapeDtypeStruct((QW, t), BF16),
        compiler_params=_params(("parallel",), 24 << 20),
    )(qt, kr, kr, proj, proj, sinks)


def _attn_bwd(qt, kr, proj, sinks, dot_, dproj):
    t = kr.shape[0]
    nb = t // WINDOW

    def body(q_ref, kc_ref, kp_ref, vc_ref, vp_ref, s_ref, do_ref, _,
             dq_ref, dk_ref, dv_ref, ds_ref, dkc_ref, dvc_ref):
        i = pl.program_id(0)
        mask = _attn_mask(nb - 1 - i)

        @pl.when(i == 0)
        def _():
            dkc_ref[...] = jnp.zeros_like(dkc_ref)
            dvc_ref[...] = jnp.zeros_like(dvc_ref)
            ds_ref[...] = jnp.zeros_like(ds_ref)

        lane = lax.broadcasted_iota(jnp.int32, (1, 128), 1)
        ds_acc = jnp.zeros((1, 128), F32)
        for j in range(NKV):
            ks = slice(HD * j, HD * (j + 1))
            kk = jnp.concatenate([kp_ref[:, ks], kc_ref[:, ks]], axis=0)
            vv = jnp.concatenate([vp_ref[:, ks], vc_ref[:, ks]], axis=0).astype(BF16)
            qgt = _attn_heads(q_ref, j)
            pn, psn = _attn_probs(qgt, kk, _attn_sink_row(s_ref, j), mask)
            dogt = _attn_heads(do_ref, j)
            dp = jnp.dot(vv, dogt, preferred_element_type=F32)
            delta = jnp.sum(dp * pn, axis=0, keepdims=True)
            dsb = (pn * (dp - delta) * (HD ** -0.5)).astype(BF16)
            dsink = -psn * delta
            dqt = lax.dot_general(kk, dsb, TN_DIMS, preferred_element_type=F32)
            for g in range(GQ):
                h = j * GQ + g
                cols = slice(WINDOW * g, WINDOW * (g + 1))
                dq_ref[HD * h:HD * (h + 1), :] = dqt[:, cols]
                ds_acc = ds_acc + jnp.where(lane == h, jnp.sum(dsink[:, cols], axis=1, keepdims=True), 0.0)
            dkk = lax.dot_general(dsb, qgt, NT_DIMS, preferred_element_type=F32)
            dvv = lax.dot_general(pn.astype(BF16), dogt, NT_DIMS, preferred_element_type=F32)
            dk_ref[:, ks] = dkk[WINDOW:] + dkc_ref[:, ks]
            dv_ref[:, ks] = (dvv[WINDOW:] + dvc_ref[:, ks]).astype(BF16)
            dkc_ref[:, ks] = dkk[:WINDOW]
            dvc_ref[:, ks] = dvv[:WINDOW]
        ds_ref[...] += ds_acc

    cur = lambda i: (nb - 1 - i, 0)
    cur_t = lambda i: (0, nb - 1 - i)
    prev = lambda i: (jnp.maximum(nb - 2 - i, 0), 0)
    p_in, p_out, p_shape = _dproj_piece(t, WINDOW, KVW, lambda i: (nb - 1 - i, V_BLK))
    return pl.pallas_call(
        body, name="attn_bwd", grid=(nb,),
        in_specs=[pl.BlockSpec((QW, WINDOW), cur_t),
                  pl.BlockSpec((WINDOW, KVW), cur), pl.BlockSpec((WINDOW, KVW), prev),
                  pl.BlockSpec((WINDOW, KVW), lambda i: (nb - 1 - i, V_BLK)),
                  pl.BlockSpec((WINDOW, KVW), lambda i: (jnp.maximum(nb - 2 - i, 0), V_BLK)),
                  pl.BlockSpec((1, 128), lambda i: (0, 0)),
                  pl.BlockSpec((QW, WINDOW), cur_t), p_in],
        out_specs=[pl.BlockSpec((QW, WINDOW), cur_t), pl.BlockSpec((WINDOW, KVW), cur),
                   p_out, pl.BlockSpec((1, 128), lambda i: (0, 0))],
        out_shape=[jax.ShapeDtypeStruct((QW, t), F32), jax.ShapeDtypeStruct((t, KVW), F32),
                   p_shape, jax.ShapeDtypeStruct((1, 128), F32)],
        input_output_aliases={7: 2},
        scratch_shapes=[pltpu.VMEM((WINDOW, KVW), F32), pltpu.VMEM((WINDOW, KVW), F32)],
        compiler_params=_params(("arbitrary",), 32 << 20),
    )(qt, kr, kr, proj, proj, sinks, dot_, dproj)


GS_BLK = SEGS["gs"][2] // D
GA_BLK = SEGS["ga"][2] // D


def _merge_fwd(ys, ya, proj):
    t = ys.shape[0]
    tt = _divtile(t, 256)

    def body(ys_ref, ya_ref, gs_ref, ga_ref, o_ref):
        o_ref[...] = (_sigmoid(gs_ref[...]) * ys_ref[...] + _sigmoid(ga_ref[...]) * ya_ref[...]).astype(BF16)

    row = pl.BlockSpec((tt, D), lambda i: (i, 0))
    return pl.pallas_call(
        body, name="merge_fwd", grid=(t // tt,),
        in_specs=[row, row, pl.BlockSpec((tt, D), lambda i: (i, GS_BLK)), pl.BlockSpec((tt, D), lambda i: (i, GA_BLK))],
        out_specs=row, out_shape=jax.ShapeDtypeStruct((t, D), BF16),
        compiler_params=_params(("parallel",), 32 << 20),
    )(ys, ya, proj, proj)


def _dproj_piece(t, rows, width, index_map):
    return (pl.BlockSpec(memory_space=pl.ANY), pl.BlockSpec((rows, width), index_map),
            jax.ShapeDtypeStruct((t, PROJ_PAD), BF16))


def _merge_bwd(dm, ys, ya, proj, dproj):
    t = ys.shape[0]
    tt = _divtile(t, 256)

    def body(dm_ref, ys_ref, ya_ref, gs_ref, ga_ref, _, dys_ref, dya_ref, dg_ref):
        d = dm_ref[...]
        s = _sigmoid(gs_ref[...])
        a = _sigmoid(ga_ref[...])
        dys_ref[...] = (d * s).astype(BF16)
        dya_ref[...] = (d * a).astype(BF16)
        dg_ref[:, :D] = (d * ys_ref[...] * (s * (1.0 - s))).astype(BF16)
        dg_ref[:, D:] = (d * ya_ref[...] * (a * (1.0 - a))).astype(BF16)

    row = pl.BlockSpec((tt, D), lambda i: (i, 0))
    p_in, p_out, p_shape = _dproj_piece(t, tt, 2 * D, lambda i: (i, SEGS["gs"][2] // (2 * D)))
    return pl.pallas_call(
        body, name="merge_bwd", grid=(t // tt,),
        in_specs=[row, row, row, pl.BlockSpec((tt, D), lambda i: (i, GS_BLK)),
                  pl.BlockSpec((tt, D), lambda i: (i, GA_BLK)), p_in],
        out_specs=[row, row, p_out], out_shape=[jax.ShapeDtypeStruct((t, D), BF16)] * 2 + [p_shape],
        input_output_aliases={5: 2},
        compiler_params=_params(("parallel",), 40 << 20),
    )(dm, ys, ya, proj, proj, dproj)


def _pad128(v):
    return jnp.pad(v, ((0, 0), (0, 128 - v.shape[1])))


def _local_step(x, pos, target, w, small, fetch=None, early_grads=None):
    w = dict(w)
    xb = x.astype(BF16)
    if fetch is None:
        gu1, a1 = _mm_swiglu(xb, w["gu1"], "ffn1_gu")
    else:
        own = _mm_swiglu(xb, w["gu1_own"], "ffn1_gu_own", chip_idx=w["chip_idx"])
        w.update(fetch(0, own[1]))
        gu1, a1 = _mm_swiglu(xb, w["gu1"], "ffn1_gu_rest", chip_idx=w["chip_idx"], done=own)
        w.update(fetch(1, a1))
    f1 = _mm(a1, w["d1"], "nn", F32, "ffn1_down", caps=(512, 1024, FFN_H))
    h1, h1b, xh1, rs1 = _ln_fwd(x, f1, small["ln1_g"], small["ln1_b"], 0.5, "ln1_fwd")
    if fetch is not None:
        w.update(fetch(2, h1b))
    proj = _mm(h1b, w["win"], "nn", F32, "proj", caps=(1024, 896, 2048))
    if fetch is not None:
        w.update(fetch(3, proj))
    bias128 = _pad128(small["dt_bias"])
    alog128 = _pad128(small["a_log"])
    dt3, adt3 = _dt_prep(proj, bias128, alog128)
    xc = _conv_fwd(proj, small["conv_w"], small["conv_b"])
    y_ssd, hprev = _ssd_fwd(xc, dt3, adt3)
    dexp = jnp.repeat(small["d_skip"], SSD_P, axis=1)
    ysn = _gated_norm_fwd(y_ssd, xc, proj, dexp, small["ssd_norm_g"])
    ys = _mm(ysn, w["so"], "nn", F32, "ssd_out")
    invf = jnp.tile(ROPE_THETA ** (-jnp.arange(HD // 2, dtype=F32) * 2.0 / HD), 4)[None, :]
    qt, kr = _rope_fwd(proj, pos, invf)
    sinks128 = _pad128(small["attn_sinks"])
    ot = _attn_fwd(qt, kr, proj, sinks128)
    ya = _mm(ot, w["ao"], "tn", F32, "attn_out")
    mg = _merge_fwd(ys, ya, proj)
    mix = _mm(mg, w["out"], "nn", F32, "mix_out")
    h2, h2b, xh2, rs2 = _ln_fwd(h1, mix, small["ln2_g"], small["ln2_b"], 1.0, "ln2_fwd")
    gu2, a2 = _mm_swiglu(h2b, w["gu2"], "ffn2_gu")
    f2 = _mm(a2, w["d2"], "nn", F32, "ffn2_down", caps=(512, 1024, FFN_H))
    _, _, xh3, rs3, dh3, loss = _ln_fwd(h2, f2, small["ln3_g"], small["ln3_b"], 0.5, "ln3_fwd", target=target)

    gw, gs = {}, {}
    dr3, dr3h, gs["ln3_g"], gs["ln3_b"] = _ln_bwd(dh3, xh3, rs3, small["ln3_g"], 0.5, "ln3_bwd")
    gw["d2"] = _mm(a2, dr3h, "tn", F32, "ffn2_down_dw")
    dgu2 = _mm_swiglu_bwd(dr3h, w["d2"], gu2, "ffn2_down_dx")
    gw["gu2"] = _mm(h2b, dgu2, "tn", F32, "ffn2_gu_dw", caps=(1024, 1408, 2048), n_slabs=N_CHIPS)
    dh2 = _mm(dgu2, w["gu2"], "nt", F32, "ffn2_gu_dx", add=dr3, add_scale=ALPHA, caps=(1024, 1024, 2816))
    dr2, dr2b, gs["ln2_g"], gs["ln2_b"] = _ln_bwd(dh2, xh2, rs2, small["ln2_g"], 1.0, "ln2_bwd")
    gw["out"] = _mm(mg, dr2b, "tn", F32, "mix_out_dw")
    dmg = _mm(dr2b, w["out"], "nt", F32, "mix_out_dx")
    dproj = lax.empty((x.shape[0], PROJ_PAD), BF16)
    dys, dya, dproj = _merge_bwd(dmg, ys, ya, proj, dproj)
    gw["ao"] = _mm(ot, dya, "nn", F32, "attn_out_dw")
    dot_ = _mm(w["ao"], dya, "nt", BF16, "attn_out_dx")
    dqt, dkr, dproj, gs["attn_sinks"] = _attn_bwd(qt, kr, proj, sinks128, dot_, dproj)
    dproj, dk = _rope_bwd(dqt, dkr, pos, invf, dproj)
    dproj = _place_cols(dk, dproj, K_BLK, "place_dk")
    gw["so"] = _mm(ysn, dys, "tn", F32, "ssd_out_dw")
    dysn = _mm(dys, w["so"], "nt", F32, "ssd_out_dx")
    dproj, dy1, gs["ssd_norm_g"], dd_ch = _gated_norm_bwd(dysn, y_ssd, xc, proj, dexp, small["ssd_norm_g"], dproj)
    gs["d_skip"] = _fold_heads(dd_ch, "d_skip_fold")
    dxs, db, dc, dadt3, dxdx3 = _ssd_bwd(xc, dt3, adt3, hprev, dy1)
    ddt, gs["dt_bias"], gs["a_log"] = _dt_bwd(dadt3, dxdx3, proj, bias128, alog128)
    dproj = _place_cols(ddt, dproj, DT_BLK, "place_ddt")
    cw, cbias = small["conv_w"], small["conv_b"]
    dproj, dwx, dbx = _conv_bwd(proj, dxs, cw, cbias, 0, SSD_INNER, "conv_bwd_x", dproj, skip=(dy1, dexp))
    dproj, dwb, dbb = _conv_bwd(proj, db, cw, cbias, SSD_INNER, SSD_G * SSD_N, "conv_bwd_b", dproj)
    dproj, dwc, dbc = _conv_bwd(proj, dc, cw, cbias, SSD_INNER + SSD_G * SSD_N, SSD_G * SSD_N, "conv_bwd_c", dproj)
    gs["conv_w"] = jnp.concatenate([dwx[:4], dwb[:4], dwc[:4]], axis=1)
    gs["conv_b"] = jnp.concatenate([dbx, dbb, dbc], axis=1)
    gw["win"] = _mm(h1b, dproj, "tn", F32, "proj_dw", caps=(1024, 896, 2048))
    win = w["win"] if early_grads is None else early_grads[0](gw, w["win"])
    dh1 = _mm(dproj, win, "nt", F32, "proj_dx", add=dr2, add_scale=ALPHA, caps=(1024, 1024, 2432))
    ln1_g = small["ln1_g"]
    if early_grads is not None:
        ln1_g = ln1_g + early_grads[1](dh1)[0:1, 0:1]
    dr1, dr1h, gs["ln1_g"], gs["ln1_b"] = _ln_bwd(dh1, xh1, rs1, ln1_g, 0.5, "ln1_bwd")
    gw["d1"] = _mm(a1, dr1h, "tn", F32, "ffn1_down_dw")
    dgu1 = _mm_swiglu_bwd(dr1h, w["d1"], gu1, "ffn1_down_dx")
    gw["gu1"] = _mm(xb, dgu1, "tn", F32, "ffn1_gu_dw", caps=(1024, 1408, 2048), n_slabs=N_CHIPS)
    grad_x = _mm(dgu1, w["gu1"], "nt", F32, "ffn1_gu_dx", add=dr1, add_scale=ALPHA, caps=(1024, 1024, 2816))
    return loss, grad_x, gw, gs


MESH = pl.DeviceIdType.MESH
ANY = pl.BlockSpec(memory_space=pl.ANY)


def _place():
    x, y, c = lax.axis_index("x"), lax.axis_index("y"), lax.axis_index("c")
    peers = [(1 - x, y), (x, 1 - y), (1 - x, 1 - y)]
    return x, y, c, peers


BIG = [
    ("ffn1_w_gate", D, SHARD_H, "gu1", "col", 0),
    ("ffn1_w_up", D, SHARD_H, "gu1", "col", SHARD_H),
    ("ffn1_w_down", SHARD_H, D, "d1", "row", 0),
    ("w_in", D, SHARD_IN, "win4", "lead", 0),
    ("w_ssd_o", SSD_INNER // N_CHIPS, D, "so", "row", 0),
    ("w_attn_o", D // N_CHIPS, D, "ao", "row", 0),
    ("w_out", D // N_CHIPS, D, "out", "row", 0),
    ("ffn2_w_gate", D, SHARD_H, "gu2", "col", 0),
    ("ffn2_w_up", D, SHARD_H, "gu2", "col", SHARD_H),
    ("ffn2_w_down", SHARD_H, D, "d2", "row", 0),
]
GATHERED = {"gu1": (D, 2 * FFN_H), "d1": (FFN_H, D), "win4": (N_CHIPS, D, SHARD_IN), "so": (SSD_INNER, D),
            "ao": (D, D), "out": (D, D), "gu2": (D, 2 * FFN_H), "d2": (FFN_H, D)}


def _cast_place(srcs, oname, chip_idx, also_alone=False):
    rows, cols = srcs[0].shape
    tr = _divtile(rows, 256, 16)
    kind = [b[4] for b in BIG if b[3] == oname][0]
    n_src = len(srcs)

    def body(chip_ref, *refs):
        for o_ref in refs[n_src:]:
            for k, s_ref in enumerate(refs[:n_src]):
                o_ref[:, k * cols:(k + 1) * cols] = s_ref[...].astype(BF16)

    nt = rows // tr
    if kind == "col":
        o_spec = pl.BlockSpec((tr, n_src * cols), lambda i, chip_ref: (i, chip_ref[0]))
    elif kind == "row":
        o_spec = pl.BlockSpec((tr, cols), lambda i, chip_ref: (chip_ref[0] * nt + i, 0))
    else:
        o_spec = pl.BlockSpec((None, tr, cols), lambda i, chip_ref: (chip_ref[0], i, 0))
    out_specs, out_shape = [o_spec], [jax.ShapeDtypeStruct(GATHERED[oname], BF16)]
    if also_alone:
        out_specs.append(pl.BlockSpec((tr, n_src * cols), lambda i, chip_ref: (i, 0)))
        out_shape.append(jax.ShapeDtypeStruct((rows, n_src * cols), BF16))
    res = pl.pallas_call(
        body, name="cast_place_" + oname,
        grid_spec=pltpu.PrefetchScalarGridSpec(
            num_scalar_prefetch=1, grid=(nt,),
            in_specs=[pl.BlockSpec((tr, cols), lambda i, chip_ref: (i, 0))] * n_src, out_specs=out_specs),
        out_shape=out_shape,
        compiler_params=_params(("parallel",), 32 << 20),
    )(chip_idx, *srcs)
    return res if also_alone else res[0]


def _slot(outs, entry, j, half):
    _, rows, cols, oname, kind, off = entry
    o = outs[oname]
    hr = rows // 2
    if kind == "col":
        cs = pl.ds(pl.multiple_of(j * (2 * SHARD_H) + off, 128), cols)
        return o.at[pl.ds(pl.multiple_of(half * hr, 16), hr), cs]
    if kind == "row":
        return o.at[pl.ds(pl.multiple_of(j * rows + half * hr, 16), hr), :]
    return o.at[j, pl.ds(pl.multiple_of(half * hr, 16), hr), :]


HBM = pl.BlockSpec(memory_space=pltpu.HBM)
SEM = pl.BlockSpec(memory_space=pltpu.SEMAPHORE)


def _ici_copy(outs, entry, j, c, to, send, recv, k):
    ref = _slot(outs, entry, j, c)
    return pltpu.make_async_remote_copy(src_ref=ref, dst_ref=ref, send_sem=send.at[k], recv_sem=recv.at[k],
                                        device_id=to, device_id_type=MESH)


GATHER_GROUPS = [["gu1"], ["d1"], ["win4"], ["so", "ao", "out", "gu2", "d2"]]


def _gather_ici_start(placed, groups, tag, carried):
    names = [k for grp in groups for k in grp]
    bigs = [[b for b in BIG if b[3] in grp] for grp in groups]
    ng = len(groups)
    n_in = len(names) + 1

    def body(*refs):
        sems = refs[n_in:n_in + 2 * ng]
        outs = dict(zip(names, refs[n_in + 2 * ng:n_in + 2 * ng + len(names)]))
        token = refs[-1]
        x, y, c, peers = _place()
        for gi, big in enumerate(bigs):
            for i, entry in enumerate(big):
                for k, (px, py) in enumerate(peers):
                    _ici_copy(outs, entry, 2 * x + y, c, (px, py, c), sems[2 * gi], sems[2 * gi + 1], 3 * i + k).start()
        token[...] = jnp.zeros_like(token)

    sem_shapes = [pltpu.SemaphoreType.DMA((3 * len(big),)) for big in bigs for _ in range(2)]
    res = pl.pallas_call(
        body, name="gather_ici_start_" + tag,
        in_specs=[HBM] * n_in,
        out_specs=[SEM] * (2 * ng) + [HBM] * n_in + [pl.BlockSpec(memory_space=pltpu.VMEM)],
        out_shape=sem_shapes + [pltpu.HBM(GATHERED[k], BF16) for k in names]
        + [pltpu.HBM(carried.shape, carried.dtype), jax.ShapeDtypeStruct((8, 128), F32)],
        input_output_aliases={i: i + 2 * ng for i in range(n_in)},
        compiler_params=pltpu.CompilerParams(has_side_effects=pltpu.SideEffectType.DATAFLOW_SIDE_EFFECTING),
    )(*[pltpu.with_memory_space_constraint(a, pltpu.HBM) for a in [placed[k] for k in names] + [carried]])
    sems = [(res[2 * gi], res[2 * gi + 1]) for gi in range(ng)]
    return sems, dict(zip(names, res[2 * ng:2 * ng + len(names)])), res[2 * ng + len(names)]


def _gather_ici_wait(send, recv, arrays, names, after, tag):
    big = [b for b in BIG if b[3] in names]

    def body(*refs):
        outs = dict(zip(names, refs[:len(names)]))
        send_ref, recv_ref = refs[len(names)], refs[len(names) + 1]
        x, y, c, peers = _place()
        for i, entry in enumerate(big):
            for k, (px, py) in enumerate(peers):
                mine = _ici_copy(outs, entry, 2 * x + y, c, (px, py, c), send_ref, recv_ref, 3 * i + k)
                mine.wait_send()
                theirs = _ici_copy(outs, entry, 2 * px + py, c, (px, py, c), send_ref, recv_ref, 3 * i + k)
                theirs.wait_recv()

    res = pl.pallas_call(
        body, name="gather_ici_wait_" + tag,
        in_specs=[HBM] * len(names) + [SEM, SEM, pl.BlockSpec(memory_space=pl.ANY)],
        out_specs=[HBM] * len(names),
        out_shape=[pltpu.HBM(GATHERED[k], BF16) for k in names],
        input_output_aliases={i: i for i in range(len(names))},
        compiler_params=pltpu.CompilerParams(has_side_effects=pltpu.SideEffectType.DATAFLOW_SIDE_EFFECTING),
    )(*[arrays[k] for k in names], send, recv, after)
    return dict(zip(names, res))


def _gather_d2d(arrays, names, tag):
    big = [b for b in BIG if b[3] in names]
    n = len(big)

    def body(*refs):
        outs = dict(zip(names, refs[len(names):2 * len(names)]))
        fsend, frecv = refs[2 * len(names):]
        x, y, c, peers = _place()
        cps = []
        for i, entry in enumerate(big):
            for k, (px, py) in enumerate(peers):
                cp = _ici_copy(outs, entry, 2 * px + py, c, (x, y, 1 - c), fsend, frecv, 3 * i + k)
                cp.start()
                cps.append(cp)
        for i, entry in enumerate(big):
            for k, (px, py) in enumerate(peers):
                _ici_copy(outs, entry, 2 * px + py, 1 - c, (x, y, 1 - c), fsend, frecv, 3 * i + k).wait_recv()
        for cp in cps:
            cp.wait_send()

    res = pl.pallas_call(
        body, name="gather_d2d_" + tag,
        in_specs=[ANY] * len(names), out_specs=[ANY] * len(names),
        out_shape=[jax.ShapeDtypeStruct(GATHERED[k], BF16) for k in names],
        input_output_aliases={i: i for i in range(len(names))},
        scratch_shapes=[pltpu.SemaphoreType.DMA((3 * n,))] * 2,
    )(*[arrays[k] for k in names])
    return dict(zip(names, res))


def _win_pieces():
    pieces = []
    for g0, wd, i0 in SEGS.values():
        for j in range(N_CHIPS):
            lo, hi = max(g0, j * SHARD_IN), min(g0 + wd, (j + 1) * SHARD_IN)
            if lo < hi:
                pieces.append((j, lo - j * SHARD_IN, hi - j * SHARD_IN, i0 + lo - g0))
    return pieces


def _win_to_internal(win4):
    tr = 128

    def body(i_ref, o_ref):
        for j, s0, s1, d0 in _win_pieces():
            o_ref[:, d0:d0 + s1 - s0] = i_ref[j, :, s0:s1]
        o_ref[:, PROJ_W:] = jnp.zeros((tr, PROJ_PAD - PROJ_W), o_ref.dtype)

    return pl.pallas_call(
        body, name="win_to_internal", grid=(D // tr,),
        in_specs=[pl.BlockSpec((N_CHIPS, tr, SHARD_IN), lambda i: (0, i, 0))],
        out_specs=pl.BlockSpec((tr, PROJ_PAD), lambda i: (i, 0)),
        out_shape=jax.ShapeDtypeStruct((D, PROJ_PAD), win4.dtype),
        compiler_params=_params(("parallel",), 40 << 20),
    )(win4)


def _win_from_internal(g):
    tr = 64

    def body(i_ref, o_ref):
        for j, s0, s1, d0 in _win_pieces():
            o_ref[j, :, s0:s1] = i_ref[:, d0:d0 + s1 - s0]

    return pl.pallas_call(
        body, name="win_from_internal", grid=(D // tr,),
        in_specs=[pl.BlockSpec((tr, PROJ_PAD), lambda i: (i, 0))],
        out_specs=pl.BlockSpec((N_CHIPS, tr, SHARD_IN), lambda i: (0, i, 0)),
        out_shape=jax.ShapeDtypeStruct((N_CHIPS, D, SHARD_IN), g.dtype),
        compiler_params=_params(("parallel",), 40 << 20),
    )(g)


def _rs_pair_exchange(grads, tag, after):
    n = len(grads)

    def body(*refs):
        srcs, dsts = refs[:n], refs[n + 1:2 * n + 1]
        send, recv = refs[2 * n + 1:]
        x, y, c, _ = _place()
        cps = []
        for i in range(n):
            hr = srcs[i].shape[1] // 2
            cp = pltpu.make_async_remote_copy(
                src_ref=srcs[i].at[:, pl.ds(pl.multiple_of((1 - c) * hr, 16), hr), :], dst_ref=dsts[i],
                send_sem=send.at[i], recv_sem=recv.at[i], device_id=(x, y, 1 - c), device_id_type=MESH)
            cp.start()
            cps.append(cp)
        for cp in cps:
            cp.wait()

    return pl.pallas_call(
        body, name="rs_pair_exchange_" + tag, in_specs=[ANY] * (n + 1), out_specs=[ANY] * n,
        out_shape=[jax.ShapeDtypeStruct((g.shape[0], g.shape[1] // 2, g.shape[2]), F32) for g in grads],
        scratch_shapes=[pltpu.SemaphoreType.DMA((n,))] * 2,
    )(*grads, after)


def _pair_copy(src, dst, c, to, send, recv, k):
    hr = src.shape[1] // 2
    return pltpu.make_async_remote_copy(
        src_ref=src.at[:, pl.ds(pl.multiple_of((1 - c) * hr, 16), hr), :], dst_ref=dst,
        send_sem=send.at[k], recv_sem=recv.at[k], device_id=to, device_id_type=MESH)


def _rs_pair_start(grads, carried):
    n = len(grads)

    def body(*refs):
        send, recv = refs[2 * n + 1], refs[2 * n + 2]
        srcs, dsts = refs[2 * n + 3:3 * n + 3], refs[3 * n + 3:4 * n + 3]
        x, y, c, _ = _place()
        for i in range(n):
            _pair_copy(srcs[i], dsts[i], c, (x, y, 1 - c), send, recv, i).start()

    lands = [lax.empty((g.shape[0], g.shape[1] // 2, g.shape[2]), F32) for g in grads]
    res = pl.pallas_call(
        body, name="rs_pair_start",
        in_specs=[HBM] * (2 * n + 1), out_specs=[SEM, SEM] + [HBM] * (2 * n + 1),
        out_shape=[pltpu.SemaphoreType.DMA((n,)), pltpu.SemaphoreType.DMA((n,))]
        + [pltpu.HBM(g.shape, F32) for g in grads] + [pltpu.HBM(l.shape, F32) for l in lands]
        + [pltpu.HBM(carried.shape, carried.dtype)],
        input_output_aliases={i: i + 2 for i in range(2 * n + 1)},
        compiler_params=pltpu.CompilerParams(has_side_effects=pltpu.SideEffectType.DATAFLOW_SIDE_EFFECTING),
    )(*[pltpu.with_memory_space_constraint(a, pltpu.HBM) for a in list(grads) + lands + [carried]])
    return (res[0], res[1], list(res[2:2 + n]), list(res[2 + n:2 + 2 * n])), res[-1]


def _rs_pair_wait(send, recv, grads, lands, after):
    n = len(grads)

    def body(*refs):
        srcs, dsts = refs[:n], refs[n:2 * n]
        send_ref, recv_ref = refs[2 * n], refs[2 * n + 1]
        x, y, c, _ = _place()
        for i in range(n):
            cp = _pair_copy(srcs[i], dsts[i], c, (x, y, 1 - c), send_ref, recv_ref, i)
            cp.wait_send()
            cp.wait_recv()

    res = pl.pallas_call(
        body, name="rs_pair_wait",
        in_specs=[HBM] * (2 * n) + [SEM, SEM, pl.BlockSpec(memory_space=pl.ANY)],
        out_specs=[HBM] * (2 * n),
        out_shape=[pltpu.HBM(g.shape, F32) for g in grads] + [pltpu.HBM(l.shape, F32) for l in lands],
        input_output_aliases={i: i for i in range(2 * n)},
        compiler_params=pltpu.CompilerParams(has_side_effects=pltpu.SideEffectType.DATAFLOW_SIDE_EFFECTING),
    )(*grads, *lands, send, recv, after)
    return list(res[:n]), list(res[n:])


def _half_tile(hr):
    return _divtile(hr, 256, 16) if hr % 256 == 0 else _divtile(hr, 512, 16)


def _rs_pair_sum(g, r, c_idx, name):
    ns, rows, cols = g.shape
    hr = rows // 2
    tr = _half_tile(hr)
    nt = hr // tr

    def body(c_ref, g_ref, r_ref, ob_ref, of_ref):
        s = g_ref[...] + r_ref[...]
        ob_ref[...] = s.astype(BF16)
        of_ref[...] = s

    blk = pl.BlockSpec((None, tr, cols), lambda j, t, c_ref: (j, t, 0))
    return pl.pallas_call(
        body, name=name,
        grid_spec=pltpu.PrefetchScalarGridSpec(
            num_scalar_prefetch=1, grid=(ns, nt),
            in_specs=[pl.BlockSpec((None, tr, cols), lambda j, t, c_ref: (j, c_ref[0] * nt + t, 0)), blk],
            out_specs=[blk, blk]),
        out_shape=[jax.ShapeDtypeStruct((ns, hr, cols), BF16), jax.ShapeDtypeStruct((ns, hr, cols), F32)],
        compiler_params=_params(("parallel", "parallel"), 48 << 20),
    )(c_idx, g, r)


def _rs_chip_start(parts, tag):
    n = len(parts)

    def body(*refs):
        send, recv = refs[2 * n], refs[2 * n + 1]
        srcs, dsts = refs[2 * n + 2:3 * n + 2], refs[3 * n + 2:4 * n + 2]
        token = refs[-1]
        x, y, c, peers = _place()
        for i in range(n):
            for k, (px, py) in enumerate(peers):
                pltpu.make_async_remote_copy(
                    src_ref=srcs[i].at[2 * px + py], dst_ref=dsts[i].at[k],
                    send_sem=send.at[3 * i + k], recv_sem=recv.at[3 * i + k],
                    device_id=(px, py, c), device_id_type=MESH).start()
        token[...] = jnp.zeros_like(token)

    lands = [lax.empty((3,) + p.shape[1:], BF16) for p in parts]
    res = pl.pallas_call(
        body, name="rs_chip_start_" + tag,
        in_specs=[HBM] * (2 * n),
        out_specs=[SEM, SEM] + [HBM] * (2 * n) + [pl.BlockSpec(memory_space=pltpu.VMEM)],
        out_shape=[pltpu.SemaphoreType.DMA((3 * n,)), pltpu.SemaphoreType.DMA((3 * n,))]
        + [pltpu.HBM(p.shape, BF16) for p in parts] + [pltpu.HBM(l.shape, BF16) for l in lands]
        + [jax.ShapeDtypeStruct((8, 128), F32)],
        input_output_aliases={i: i + 2 for i in range(2 * n)},
        compiler_params=pltpu.CompilerParams(has_side_effects=pltpu.SideEffectType.DATAFLOW_SIDE_EFFECTING),
    )(*[pltpu.with_memory_space_constraint(a, pltpu.HBM) for a in list(parts) + lands])
    return res[0], res[1], list(res[2:2 + n]), list(res[2 + n:2 + 2 * n]), res[-1]


def _rs_chip_wait(send, recv, parts, lands, after, tag):
    n = len(parts)

    def body(*refs):
        srcs, dsts = refs[:n], refs[n:2 * n]
        send_ref, recv_ref = refs[2 * n], refs[2 * n + 1]
        x, y, c, peers = _place()
        for i in range(n):
            for k, (px, py) in enumerate(peers):
                cp = pltpu.make_async_remote_copy(
                    src_ref=srcs[i].at[2 * px + py], dst_ref=dsts[i].at[k],
                    send_sem=send_ref.at[3 * i + k], recv_sem=recv_ref.at[3 * i + k],
                    device_id=(px, py, c), device_id_type=MESH)
                cp.wait_send()
                cp.wait_recv()

    res = pl.pallas_call(
        body, name="rs_chip_wait_" + tag,
        in_specs=[HBM] * (2 * n) + [SEM, SEM, pl.BlockSpec(memory_space=pl.ANY)],
        out_specs=[HBM] * (2 * n),
        out_shape=[pltpu.HBM(p.shape, BF16) for p in parts] + [pltpu.HBM(l.shape, BF16) for l in lands],
        input_output_aliases={i: i for i in range(2 * n)},
        compiler_params=pltpu.CompilerParams(has_side_effects=pltpu.SideEffectType.DATAFLOW_SIDE_EFFECTING),
    )(*parts, *lands, send, recv, after)
    return list(res[n:])


def _rs_final_sum(own, got, chip_idx, c_idx, name):
    ns, hr, cols = own.shape
    tr = _half_tile(hr)
    nt = hr // tr

    def body(chip_ref, c_ref, o_ref, g_ref, out_ref):
        s = o_ref[...]
        for k in range(3):
            s = s + g_ref[k].astype(F32)
        out_ref[...] = s

    return pl.pallas_call(
        body, name=name,
        grid_spec=pltpu.PrefetchScalarGridSpec(
            num_scalar_prefetch=2, grid=(nt,),
            in_specs=[pl.BlockSpec((None, tr, cols), lambda t, chip_ref, c_ref: (chip_ref[0], t, 0)),
                      pl.BlockSpec((3, tr, cols), lambda t, chip_ref, c_ref: (0, t, 0))],
            out_specs=pl.BlockSpec((tr, cols), lambda t, chip_ref, c_ref: (c_ref[0] * nt + t, 0))),
        out_shape=jax.ShapeDtypeStruct((2 * hr, cols), F32),
        compiler_params=_params(("parallel",), 48 << 20),
    )(chip_idx, c_idx, own, got)


def _rs_share_halves(fulls, tag):
    n = len(fulls)

    def body(*refs):
        dsts = refs[n:2 * n]
        send, recv = refs[2 * n:]
        x, y, c, _ = _place()
        cps = []
        for i in range(n):
            hr = dsts[i].shape[0] // 2
            rows = dsts[i].at[pl.ds(pl.multiple_of(c * hr, 8), hr), :]
            cp = pltpu.make_async_remote_copy(src_ref=rows, dst_ref=rows, send_sem=send.at[i], recv_sem=recv.at[i],
                                              device_id=(x, y, 1 - c), device_id_type=MESH)
            cp.start()
            cps.append(cp)
        for i in range(n):
            hr = dsts[i].shape[0] // 2
            other = dsts[i].at[pl.ds(pl.multiple_of((1 - c) * hr, 8), hr), :]
            pltpu.make_async_remote_copy(src_ref=other, dst_ref=other, send_sem=send.at[i], recv_sem=recv.at[i],
                                         device_id=(x, y, 1 - c), device_id_type=MESH).wait_recv()
        for cp in cps:
            cp.wait_send()

    return pl.pallas_call(
        body, name="rs_share_halves_" + tag, in_specs=[ANY] * n, out_specs=[ANY] * n,
        out_shape=[jax.ShapeDtypeStruct(f.shape, F32) for f in fulls],
        input_output_aliases={i: i for i in range(n)},
        scratch_shapes=[pltpu.SemaphoreType.DMA((n,))] * 2,
    )(*fulls)


def _all_reduce_small(v):
    rows = v.shape[0]

    def body(v_ref, o_ref, buf, send, recv):
        x, y, c, _ = _place()
        me = 4 * x + 2 * y + c
        buf[me] = v_ref[...]
        cps = []
        for d in range(1, 8):
            px, py, pc = x ^ (d >> 2), y ^ ((d >> 1) & 1), c ^ (d & 1)
            cp = pltpu.make_async_remote_copy(src_ref=v_ref, dst_ref=buf.at[me], send_sem=send.at[d - 1],
                                              recv_sem=recv.at[d - 1], device_id=(px, py, pc), device_id_type=MESH)
            cp.start()
            cps.append(cp)
        for d in range(1, 8):
            px, py, pc = x ^ (d >> 2), y ^ ((d >> 1) & 1), c ^ (d & 1)
            pltpu.make_async_remote_copy(src_ref=v_ref, dst_ref=buf.at[4 * px + 2 * py + pc], send_sem=send.at[d - 1],
                                         recv_sem=recv.at[d - 1], device_id=(px, py, pc),
                                         device_id_type=MESH).wait_recv()
        for cp in cps:
            cp.wait_send()
        acc = buf[0]
        for d in range(1, 8):
            acc = acc + buf[d]
        o_ref[...] = acc

    vm = pl.BlockSpec(memory_space=pltpu.VMEM)
    return pl.pallas_call(
        body, name="all_reduce_small", in_specs=[vm], out_specs=vm,
        out_shape=jax.ShapeDtypeStruct((rows, 128), F32),
        scratch_shapes=[pltpu.VMEM((8, rows, 128), F32), pltpu.SemaphoreType.DMA((7,)), pltpu.SemaphoreType.DMA((7,))],
    )(v)


def _adamw(w, g, m, v, name, g_col_blk=0):
    rows, cols = w.shape
    tr = _divtile(rows, max(8, (2 << 20) // (4 * cols) // 8 * 8), 8)

    def body(w_ref, g_ref, m_ref, v_ref, go_ref, d_ref, mo_ref, vo_ref):
        gv = g_ref[...]
        mn = ADAM_B1 * m_ref[...] + (1.0 - ADAM_B1) * gv
        vn = ADAM_B2 * v_ref[...] + (1.0 - ADAM_B2) * (gv * gv)
        m_hat = mn / (1.0 - ADAM_B1 ** ADAM_STEP)
        v_hat = vn / (1.0 - ADAM_B2 ** ADAM_STEP)
        go_ref[...] = gv
        d_ref[...] = -ADAM_LR * (m_hat / (jnp.sqrt(v_hat) + ADAM_EPS) + ADAM_WD * w_ref[...])
        mo_ref[...] = mn
        vo_ref[...] = vn

    blk = pl.BlockSpec((tr, cols), lambda i: (i, 0))
    return pl.pallas_call(
        body, name=name, grid=(rows // tr,),
        in_specs=[blk, pl.BlockSpec((tr, cols), lambda i: (i, g_col_blk)), blk, blk],
        out_specs=[blk] * 4, out_shape=[jax.ShapeDtypeStruct((rows, cols), F32)] * 4,
        compiler_params=_params(("parallel",), 48 << 20),
    )(w, g, m, v)


SMALL = ["ln1_g", "ln1_b", "conv_w", "conv_b", "dt_bias", "a_log", "d_skip", "ssd_norm_g", "attn_sinks",
         "ln2_g", "ln2_b", "ln3_g", "ln3_b"]


def _pack_rows(vs):
    parts = []
    for v in vs:
        v = v.reshape(-1)
        parts.append(jnp.pad(v, (0, (-v.shape[0]) % 128)))
    flat = jnp.concatenate(parts)
    flat = jnp.pad(flat, (0, (-flat.shape[0]) % 1024))
    return flat.reshape(-1, 128)


def _unpack_rows(packed, shapes):
    flat = packed.reshape(-1)
    out, at = [], 0
    for s in shapes:
        nel = int(np.prod(s))
        out.append(flat[at:at + nel].reshape(s))
        at += nel + (-nel) % 128
    return out


def kernel(x, positions, ffn1_w_gate, ffn1_w_up, ffn1_w_down, ln1_g, ln1_b, w_in, conv_w, conv_b, dt_bias, a_log, d_skip, ssd_norm_g, w_ssd_o, attn_sinks, w_attn_o, w_out, ln2_g, ln2_b, ffn2_w_gate, ffn2_w_up, ffn2_w_down, ln3_g, ln3_b, loss_target, m_ffn1_w_gate, m_ffn1_w_up, m_ffn1_w_down, m_ln1_g, m_ln1_b, m_w_in, m_conv_w, m_conv_b, m_dt_bias, m_a_log, m_d_skip, m_ssd_norm_g, m_w_ssd_o, m_attn_sinks, m_w_attn_o, m_w_out, m_ln2_g, m_ln2_b, m_ffn2_w_gate, m_ffn2_w_up, m_ffn2_w_down, m_ln3_g, m_ln3_b, v_ffn1_w_gate, v_ffn1_w_up, v_ffn1_w_down, v_ln1_g, v_ln1_b, v_w_in, v_conv_w, v_conv_b, v_dt_bias, v_a_log, v_d_skip, v_ssd_norm_g, v_w_ssd_o, v_attn_sinks, v_w_attn_o, v_w_out, v_ln2_g, v_ln2_b, v_ffn2_w_gate, v_ffn2_w_up, v_ffn2_w_down, v_ln3_g, v_ln3_b):
    args = dict(locals())
    wts = {n: args[n][0] for n in [b[0] for b in BIG] + SMALL}
    mom_m = {n: args["m_" + n][0] for n in wts}
    mom_v = {n: args["v_" + n][0] for n in wts}
    t = x.shape[1]
    xi, yi, ci = lax.axis_index("x"), lax.axis_index("y"), lax.axis_index("c")
    chip = 2 * xi + yi

    c_idx = ci.astype(jnp.int32).reshape(1)
    chip_idx = chip.astype(jnp.int32).reshape(1)
    placed = {o: _cast_place([wts[b[0]] for b in BIG if b[3] == o], o, chip_idx, also_alone=(o == "gu1"))
              for o in GATHERED}
    placed["gu1"], gu1_own = placed["gu1"]
    g_sems, g_flight = {}, {}

    def fetch(group, after):
        names = GATHER_GROUPS[group]
        send, recv = g_sems[group]
        landed = _gather_ici_wait(send, recv, {k: g_flight[k] for k in names}, names, after, str(group))
        got = _gather_d2d(landed, names, str(group))
        if "win4" in got:
            got["win"] = _win_to_internal(got.pop("win4"))
        return got

    sems, arrays, gu1_own = _gather_ici_start(placed, GATHER_GROUPS, "all", gu1_own)
    g_sems.update(dict(enumerate(sems)))
    g_flight.update(arrays)
    w = {"gu1_own": gu1_own, "chip_idx": chip_idx}

    def slabs_of(gw, names):
        view = {"gu1": lambda: gw["gu1"], "gu2": lambda: gw["gu2"],
                "d1": lambda: gw["d1"].reshape(N_CHIPS, SHARD_H, D), "d2": lambda: gw["d2"].reshape(N_CHIPS, SHARD_H, D),
                "win": lambda: _win_from_internal(gw["win"]),
                "so": lambda: gw["so"].reshape(N_CHIPS, SSD_INNER // N_CHIPS, D),
                "ao": lambda: gw["ao"].reshape(N_CHIPS, D // N_CHIPS, D),
                "out": lambda: gw["out"].reshape(N_CHIPS, D // N_CHIPS, D)}
        return [view[nm]() for nm in names]

    early = ["win", "so", "ao", "out", "gu2", "d2"]
    late = ["gu1", "d1"]
    flight = {}

    def early_start(gw, win):
        flight["pair"], win = _rs_pair_start(slabs_of(gw, early), win)
        return win

    def early_mid(dh1):
        slabs, from_sib = _rs_pair_wait(*flight["pair"], dh1)
        pair = [_rs_pair_sum(g, r, c_idx, "rs_pair_sum_" + nm) for g, r, nm in zip(slabs, from_sib, early)]
        send, recv, parts, lands, token = _rs_chip_start([p[0] for p in pair], "early")
        flight.update(send=send, recv=recv, parts=parts, lands=lands, own=[p[1] for p in pair])
        return token

    early_grads = (early_start, early_mid)
    cw_rows = _pack_rows([lax.dynamic_update_slice(jnp.zeros((4, XBC), F32), wts["conv_w"], (0, chip * (XBC // N_CHIPS)))])
    cw_rows = jnp.where(ci == 0, cw_rows, 0.0)
    conv_w_full = _all_reduce_small(cw_rows)[:4 * XBC // 128].reshape(4, XBC)

    small = {n: (wts[n][None, :] if wts[n].ndim == 1 else wts[n]) for n in SMALL}
    small["conv_w"] = conv_w_full
    loss, grad_x, gw, gs = _local_step(x[0], positions[0].astype(F32)[:, None], loss_target[0], w, small,
                                       fetch=fetch, early_grads=early_grads)

    gvec = {n: gs[n] for n in SMALL}
    gvec["dt_bias"], gvec["a_log"], gvec["d_skip"] = gs["dt_bias"][:, :64], gs["a_log"][:, :64], gs["d_skip"][:, :64]
    gvec["attn_sinks"] = gs["attn_sinks"][:, :NQ]
    red = _all_reduce_small(_pack_rows([gvec[n] for n in SMALL] + [loss, grad_x[:1, :128]]))
    slabs = slabs_of(gw, late)
    from_sib = _rs_pair_exchange(slabs, "late", red)
    pair = [_rs_pair_sum(g, r, c_idx, "rs_pair_sum_" + nm) for g, r, nm in zip(slabs, from_sib, late)]
    l_send, l_recv, l_parts, l_lands, l_token = _rs_chip_start([p[0] for p in pair], "late")
    got_early = _rs_chip_wait(flight["send"], flight["recv"], flight["parts"], flight["lands"], l_token, "early")

    outs = {}
    big_src = {"ffn1_w_gate": ("gu1", 0), "ffn1_w_up": ("gu1", 1), "ffn1_w_down": ("d1", 0), "w_in": ("win", 0),
               "w_ssd_o": ("so", 0), "w_attn_o": ("ao", 0), "w_out": ("out", 0),
               "ffn2_w_gate": ("gu2", 0), "ffn2_w_up": ("gu2", 1), "ffn2_w_down": ("d2", 0)}

    def finish(names, own, got, tag):
        halves = [_rs_final_sum(o, gt, chip_idx, c_idx, "rs_final_sum_" + nm) for o, gt, nm in zip(own, got, names)]
        full = dict(zip(names, _rs_share_halves(halves, tag)))
        for nm, (src, blk) in big_src.items():
            if src in full:
                outs[nm] = _adamw(wts[nm], full[src], mom_m[nm], mom_v[nm], "adamw_" + nm, g_col_blk=blk)

    finish(early, flight["own"], got_early, "early")
    early_done = sum(o[1][:1, :1] for o in outs.values())
    got_late = _rs_chip_wait(l_send, l_recv, l_parts, l_lands, early_done, "late")
    finish(late, [p[1] for p in pair], got_late, "late")

    shapes = [(4, XBC) if n == "conv_w" else wts[n].shape for n in SMALL] + [(1,)]
    red_list = _unpack_rows(red, shapes)
    loss_out = red_list[-1].reshape(())
    gsm = dict(zip(SMALL, red_list[:-1]))
    gsm["conv_w"] = lax.dynamic_slice_in_dim(gsm["conv_w"], chip * (XBC // N_CHIPS), XBC // N_CHIPS, axis=1)
    sm_shapes = [wts[n].shape for n in SMALL]
    res = _adamw(_pack_rows([wts[n] for n in SMALL]), _pack_rows([gsm[n] for n in SMALL]),
                 _pack_rows([mom_m[n] for n in SMALL]), _pack_rows([mom_v[n] for n in SMALL]), "adamw_small")
    res = [_unpack_rows(r, sm_shapes) for r in res]
    for i, nm in enumerate(SMALL):
        outs[nm] = tuple(r[i] for r in res)

    order = ["ffn1_w_gate", "ffn1_w_up", "ffn1_w_down", "ln1_g", "ln1_b", "w_in", "conv_w", "conv_b", "dt_bias", "a_log",
             "d_skip", "ssd_norm_g", "w_ssd_o", "attn_sinks", "w_attn_o", "w_out", "ln2_g", "ln2_b",
             "ffn2_w_gate", "ffn2_w_up", "ffn2_w_down", "ln3_g", "ln3_b"]
    result = [loss_out, grad_x[None]]
    for kind in range(4):
        result += [outs[nm][kind][None] for nm in order]
    return tuple(result)
```

```python
import functools
import math

import numpy as np
import jax
import jax.numpy as jnp
from jax import lax
from jax.experimental import pallas as pl
from jax.experimental.pallas import tpu as pltpu

F32 = jnp.float32
BF16 = jnp.bfloat16
HI = lax.Precision.HIGHEST

D = 2048
FFN_H = 5632
SSD_INNER = 4096
SSD_HEADS = 64
SSD_P = 64
SSD_G = 8
SSD_R = 8
SSD_N = 128
CHUNK = 128
XBC = 6144
NQ = 32
NKV = 4
HD = 64
QW = 2048
KVW = 256
WINDOW = 128
ROPE_THETA = 10000.0
ALPHA = 2.0 ** 0.25
LN_EPS = 1e-5
RMS_EPS = 1e-5
PROJ_W = 16960
N_CHIPS = 4
SHARD_IN = PROJ_W // N_CHIPS
SHARD_H = FFN_H // N_CHIPS

SEGS = {
    "z": (0, 4096, 0),
    "xbc": (4096, 6144, 10240),
    "dt": (10240, 64, 16896),
    "q": (10304, 2048, 8192),
    "k": (12352, 256, 16384),
    "v": (12608, 256, 16640),
    "gs": (12864, 2048, 4096),
    "ga": (14912, 2048, 6144),
}
PROJ_PAD = 17024

ADAM_LR = 0.001
ADAM_B1 = 0.9
ADAM_B2 = 0.999
ADAM_EPS = 1e-08
ADAM_WD = 0.01
ADAM_STEP = 10

VMEM_CAP = 60 * 1024 * 1024


def _params(sem, vmem_bytes):
    return pltpu.CompilerParams(dimension_semantics=sem, vmem_limit_bytes=int(min(VMEM_CAP, vmem_bytes)))


def _divtile(n, cap, q=128):
    best = None
    for d in range(q, min(n, cap) + 1, q):
        if n % d == 0:
            best = d
    return n if best is None else best


def _sigmoid(x):
    return 0.5 * jnp.tanh(0.5 * x) + 0.5


def _mm(a, b, mode, out_dtype, name, add=None, add_scale=1.0, caps=(1024, 1024, 2048), n_slabs=1):
    if mode == "nn":
        (m, k), (k2, n) = a.shape, b.shape
    elif mode == "nt":
        (m, k), (n, k2) = a.shape, b.shape
    else:
        (k, m), (k2, n) = a.shape, b.shape
    assert k == k2, (a.shape, b.shape, mode)
    tm, tn, tk = _divtile(m, caps[0]), _divtile(n // n_slabs, caps[1]), _divtile(k, caps[2])
    nk = k // tk
    per_slab = n // n_slabs // tn
    dims = {"nn": ((1,), (0,)), "nt": ((1,), (1,)), "tn": ((0,), (0,))}[mode]
    has_add = add is not None

    def body(*refs):
        if has_add:
            a_ref, b_ref, add_ref, o_ref = refs[:4]
            scr = refs[4:]
        else:
            a_ref, b_ref, o_ref = refs[:3]
            add_ref = None
            scr = refs[3:]
        part = lax.dot_general(a_ref[...].astype(BF16), b_ref[...].astype(BF16), (dims, ((), ())),
                               preferred_element_type=F32)

        def finish(acc):
            if has_add:
                acc = acc + add_scale * add_ref[...].astype(F32)
            o_ref[...] = acc.astype(o_ref.dtype)

        if nk == 1:
            finish(part)
        else:
            acc_ref = scr[0]
            kk = pl.program_id(2)

            @pl.when(kk == 0)
            def _():
                acc_ref[...] = part

            @pl.when(kk > 0)
            def _():
                acc_ref[...] += part

            @pl.when(kk == nk - 1)
            def _():
                finish(acc_ref[...])

    if mode == "nn":
        a_spec = pl.BlockSpec((tm, tk), lambda i, j, kk: (i, kk))
        b_spec = pl.BlockSpec((tk, tn), lambda i, j, kk: (kk, j))
    elif mode == "nt":
        a_spec = pl.BlockSpec((tm, tk), lambda i, j, kk: (i, kk))
        b_spec = pl.BlockSpec((tn, tk), lambda i, j, kk: (j, kk))
    else:
        a_spec = pl.BlockSpec((tk, tm), lambda i, j, kk: (kk, i))
        b_spec = pl.BlockSpec((tk, tn), lambda i, j, kk: (kk, j))
    o_spec = pl.BlockSpec((tm, tn), lambda i, j, kk: (i, j))
    out_shape = jax.ShapeDtypeStruct((m, n), out_dtype)
    if n_slabs > 1:
        assert not has_add
        o_spec = pl.BlockSpec((None, tm, tn), lambda i, j, kk: (j // per_slab, i, j % per_slab))
        out_shape = jax.ShapeDtypeStruct((n_slabs, m, n // n_slabs), out_dtype)
    in_specs = [a_spec, b_spec] + ([o_spec] if has_add else [])
    args = (a, b) + ((add,) if has_add else ())
    osz = jnp.dtype(out_dtype).itemsize
    vmem = (2 * (tm * tk * a.dtype.itemsize + tk * tn * b.dtype.itemsize) + 2 * tm * tn * osz
            + (2 * tm * tn * add.dtype.itemsize if has_add else 0) + 2 * tm * tn * 4
            + 2 * (tm * tk + tk * tn) + (8 << 20))
    return pl.pallas_call(
        body, name=name, grid=(m // tm, n // tn, nk),
        in_specs=in_specs, out_specs=o_spec, out_shape=out_shape,
        scratch_shapes=[pltpu.VMEM((tm, tn), F32)] if nk > 1 else [],
        compiler_params=_params(("parallel", "parallel", "arbitrary"), vmem),
    )(*args)


def _mm_swiglu(a, b, name, chip_idx=None, done=None):
    m, k = a.shape
    w = SHARD_H
    tm = _divtile(m, 512)

    def body(*refs):
        a_ref, b_ref = refs[-4 if done is None else -6], refs[-3 if done is None else -5]
        gu_ref, act_ref = refs[-2:]
        gu = jnp.dot(a_ref[...], b_ref[...], preferred_element_type=F32)
        g = gu[:, :w]
        gu_ref[...] = gu.astype(BF16)
        act_ref[...] = (g * _sigmoid(g) * gu[:, w:]).astype(BF16)

    out_shape = [jax.ShapeDtypeStruct((m, 2 * FFN_H), BF16), jax.ShapeDtypeStruct((m, FFN_H), BF16)]
    cp = _params(("parallel", "parallel"), 56 << 20)
    if chip_idx is None:
        return pl.pallas_call(
            body, name=name, grid=(N_CHIPS, m // tm),
            in_specs=[pl.BlockSpec((tm, k), lambda j, i: (i, 0)), pl.BlockSpec((k, 2 * w), lambda j, i: (0, j))],
            out_specs=[pl.BlockSpec((tm, 2 * w), lambda j, i: (i, j)), pl.BlockSpec((tm, w), lambda j, i: (i, j))],
            out_shape=out_shape, compiler_params=cp,
        )(a, b)
    first = done is None
    tile = (lambda j, c: c[0]) if first else (lambda j, c: (c[0] + 1 + j) % N_CHIPS)
    in_specs = [pl.BlockSpec((tm, k), lambda j, i, c: (i, 0)),
                pl.BlockSpec((k, 2 * w), (lambda j, i, c: (0, 0)) if first else (lambda j, i, c: (0, tile(j, c))))]
    return pl.pallas_call(
        body, name=name,
        grid_spec=pltpu.PrefetchScalarGridSpec(
            num_scalar_prefetch=1, grid=(1 if first else N_CHIPS - 1, m // tm),
            in_specs=in_specs + ([] if first else [pl.BlockSpec(memory_space=pl.ANY)] * 2),
            out_specs=[pl.BlockSpec((tm, 2 * w), lambda j, i, c: (i, tile(j, c))),
                       pl.BlockSpec((tm, w), lambda j, i, c: (i, tile(j, c)))]),
        out_shape=out_shape, compiler_params=cp,
        input_output_aliases={} if first else {3: 0, 4: 1},
    )(chip_idx, a, b, *(() if first else done))


def _mm_swiglu_bwd(dr, wd, gu, name):
    m, k = dr.shape
    w = SHARD_H
    tm = _divtile(m, 512)

    def body(dr_ref, wd_ref, gu_ref, o_ref):
        d = lax.dot_general(dr_ref[...], wd_ref[...], NT_DIMS, preferred_element_type=F32)
        g = gu_ref[:, :w].astype(F32)
        u = gu_ref[:, w:].astype(F32)
        s = _sigmoid(g)
        o_ref[:, :w] = (d * u * (s * (1.0 + g * (1.0 - s)))).astype(BF16)
        o_ref[:, w:] = (d * (g * s)).astype(BF16)

    return pl.pallas_call(
        body, name=name, grid=(N_CHIPS, m // tm),
        in_specs=[pl.BlockSpec((tm, k), lambda j, i: (i, 0)), pl.BlockSpec((w, k), lambda j, i: (j, 0)),
                  pl.BlockSpec((tm, 2 * w), lambda j, i: (i, j))],
        out_specs=pl.BlockSpec((tm, 2 * w), lambda j, i: (i, j)),
        out_shape=jax.ShapeDtypeStruct((m, 2 * FFN_H), BF16),
        compiler_params=_params(("parallel", "parallel"), 48 << 20),
    )(dr, wd, gu)


def _ln_fwd(base, f, g, b, c, name, target=None):
    t = base.shape[0]
    tt = _divtile(t, 256)
    with_loss = target is not None

    def body(*refs):
        if with_loss:
            base_ref, f_ref, g_ref, b_ref, tg_ref, h_ref, hb_ref, xh_ref, rs_ref, dh_ref, loss_ref = refs
        else:
            base_ref, f_ref, g_ref, b_ref, h_ref, hb_ref, xh_ref, rs_ref = refs
        r = ALPHA * base_ref[...] + c * f_ref[...]
        mu = jnp.mean(r, axis=-1, keepdims=True)
        xc = r - mu
        var = jnp.mean(xc * xc, axis=-1, keepdims=True)
        rstd = lax.rsqrt(var + LN_EPS)
        xh = xc * rstd
        h = xh * g_ref[...] + b_ref[...]
        h_ref[...] = h
        hb_ref[...] = h.astype(BF16)
        xh_ref[...] = xh
        rs_ref[...] = rstd
        if with_loss:
            e = h - tg_ref[...]
            dh_ref[...] = e * (1.0 / D)
            part = 0.5 * jnp.sum(jnp.sum(e * e, axis=-1, keepdims=True) * (1.0 / D), axis=0, keepdims=True)

            @pl.when(pl.program_id(0) == 0)
            def _():
                loss_ref[...] = jnp.zeros_like(loss_ref)

            loss_ref[...] += part

    row = pl.BlockSpec((tt, D), lambda i: (i, 0))
    vec = pl.BlockSpec((1, D), lambda i: (0, 0))
    col = pl.BlockSpec((tt, 1), lambda i: (i, 0))
    in_specs = [row, row, vec, vec] + ([row] if with_loss else [])
    out_specs = [row, row, row, col] + ([row, pl.BlockSpec((1, 1), lambda i: (0, 0))] if with_loss else [])
    out_shape = [jax.ShapeDtypeStruct((t, D), F32), jax.ShapeDtypeStruct((t, D), BF16),
                 jax.ShapeDtypeStruct((t, D), F32), jax.ShapeDtypeStruct((t, 1), F32)]
    if with_loss:
        out_shape += [jax.ShapeDtypeStruct((t, D), F32), jax.ShapeDtypeStruct((1, 1), F32)]
    args = (base, f, g, b) + ((target,) if with_loss else ())
    return pl.pallas_call(
        body, name=name, grid=(t // tt,), in_specs=in_specs, out_specs=out_specs, out_shape=out_shape,
        compiler_params=_params(("arbitrary",) if with_loss else ("parallel",), 48 << 20),
    )(*args)


def _ln_bwd(dy, xh, rstd, g, c, name):
    t = dy.shape[0]
    tt = _divtile(t, 256)

    def body(dy_ref, xh_ref, rs_ref, g_ref, dr_ref, drb_ref, dg_ref, db_ref):
        dyv = dy_ref[...]
        xhv = xh_ref[...]
        dxh = dyv * g_ref[...]
        m1 = jnp.mean(dxh, axis=-1, keepdims=True)
        m2 = jnp.mean(dxh * xhv, axis=-1, keepdims=True)
        dr = rs_ref[...] * (dxh - m1 - xhv * m2)
        dr_ref[...] = dr
        drb_ref[...] = (c * dr).astype(BF16)

        @pl.when(pl.program_id(0) == 0)
        def _():
            dg_ref[...] = jnp.zeros_like(dg_ref)
            db_ref[...] = jnp.zeros_like(db_ref)

        dg_ref[...] += jnp.sum(dyv * xhv, axis=0, keepdims=True)
        db_ref[...] += jnp.sum(dyv, axis=0, keepdims=True)

    row = pl.BlockSpec((tt, D), lambda i: (i, 0))
    vec = pl.BlockSpec((1, D), lambda i: (0, 0))
    col = pl.BlockSpec((tt, 1), lambda i: (i, 0))
    return pl.pallas_call(
        body, name=name, grid=(t // tt,), in_specs=[row, row, col, vec], out_specs=[row, row, vec, vec],
        out_shape=[jax.ShapeDtypeStruct((t, D), F32), jax.ShapeDtypeStruct((t, D), BF16),
                   jax.ShapeDtypeStruct((1, D), F32), jax.ShapeDtypeStruct((1, D), F32)],
        compiler_params=_params(("arbitrary",), 40 << 20),
    )(dy, xh, rstd, g)


DT_BLK = SEGS["dt"][2] // 128


def _dt_prep(proj, bias128, alog128):
    t = proj.shape[0]
    tt = _divtile(t, 256)

    def body(p_ref, bias_ref, alog_ref, dt_ref, adt_ref):
        dtv = jax.nn.softplus(p_ref[...] + bias_ref[...])
        adt = dtv * (-jnp.exp(alog_ref[...]))
        for g in range(SSD_G):
            dt_ref[g] = dtv[:, SSD_R * g:SSD_R * (g + 1)]
            adt_ref[g] = adt[:, SSD_R * g:SSD_R * (g + 1)]

    blk3 = pl.BlockSpec((SSD_G, tt, SSD_R), lambda i: (0, i, 0))
    vec = pl.BlockSpec((1, 128), lambda i: (0, 0))
    return pl.pallas_call(
        body, name="dt_prep", grid=(t // tt,),
        in_specs=[pl.BlockSpec((tt, 128), lambda i: (i, DT_BLK)), vec, vec], out_specs=[blk3, blk3],
        out_shape=[jax.ShapeDtypeStruct((SSD_G, t, SSD_R), F32)] * 2,
        compiler_params=_params(("parallel",), 16 << 20),
    )(proj, bias128, alog128)


def _dt_bwd(dadt3, dxdx3, proj, bias128, alog128):
    t = proj.shape[0]
    tt = _divtile(t, 256)

    def token_major(ref):
        parts = [ref[g] for g in range(SSD_G)] + [jnp.zeros((tt, 128 - SSD_HEADS), F32)]
        return jnp.concatenate(parts, axis=1)

    def body(dadt_ref, dxdx_ref, p_ref, bias_ref, alog_ref, o_ref, dbias_ref, dalog_ref):
        pre = p_ref[...] + bias_ref[...]
        dtv = jax.nn.softplus(pre)
        a = -jnp.exp(alog_ref[...])
        dadt = token_major(dadt_ref)
        ddt = a * dadt + token_major(dxdx_ref)
        draw = ddt * _sigmoid(pre)
        o_ref[...] = draw.astype(BF16)

        @pl.when(pl.program_id(0) == 0)
        def _():
            dbias_ref[...] = jnp.zeros_like(dbias_ref)
            dalog_ref[...] = jnp.zeros_like(dalog_ref)

        dbias_ref[...] += jnp.sum(draw, axis=0, keepdims=True)
        dalog_ref[...] += jnp.sum(dadt * dtv * a, axis=0, keepdims=True)

    blk = pl.BlockSpec((tt, 128), lambda i: (i, 0))
    blk3 = pl.BlockSpec((SSD_G, tt, SSD_R), lambda i: (0, i, 0))
    vec = pl.BlockSpec((1, 128), lambda i: (0, 0))
    return pl.pallas_call(
        body, name="dt_bwd", grid=(t // tt,),
        in_specs=[blk3, blk3, pl.BlockSpec((tt, 128), lambda i: (i, DT_BLK)), vec, vec],
        out_specs=[blk, vec, vec],
        out_shape=[jax.ShapeDtypeStruct((t, 128), BF16), jax.ShapeDtypeStruct((1, 128), F32),
                   jax.ShapeDtypeStruct((1, 128), F32)],
        compiler_params=_params(("arbitrary",), 16 << 20),
    )(dadt3, dxdx3, proj, bias128, alog128)


CONV_CB = 128
CONV_TT = 2048


def _shift_down(cur, prev8, s):
    if s == 0:
        return cur
    rolled = pltpu.roll(cur, s, 0)
    head = pltpu.roll(prev8, s, 0)
    r8 = lax.broadcasted_iota(jnp.int32, (8, 1), 0)
    top = jnp.where(r8 < s, head, rolled[:8])
    return jnp.concatenate([top, rolled[8:]], axis=0)


def _shift_up(cur, next8, s):
    if s == 0:
        return cur
    n = cur.shape[0]
    rolled = pltpu.roll(cur, n - s, 0)
    tail = pltpu.roll(next8, 8 - s, 0)
    r8 = lax.broadcasted_iota(jnp.int32, (8, 1), 0)
    bot = jnp.where(r8 >= 8 - s, tail, rolled[n - 8:])
    return jnp.concatenate([rolled[:n - 8], bot], axis=0)


def _conv_fwd(proj, conv_w, conv_b):
    t = proj.shape[0]
    tt = _divtile(t, CONV_TT)
    base = SEGS["xbc"][2] // CONV_CB
    r8 = tt // 8

    def body(u_ref, up_ref, w_ref, b_ref, o_ref):
        cur = u_ref[...]
        prev8 = jnp.where(pl.program_id(1) > 0, up_ref[...], 0.0)
        acc = b_ref[...] + w_ref[3:4, :] * cur
        for k in range(3):
            acc = acc + w_ref[k:k + 1, :] * _shift_down(cur, prev8, 3 - k)
        o_ref[...] = acc * _sigmoid(acc)

    return pl.pallas_call(
        body, name="conv_fwd", grid=(XBC // CONV_CB, t // tt),
        in_specs=[pl.BlockSpec((tt, CONV_CB), lambda c, i: (i, base + c)),
                  pl.BlockSpec((8, CONV_CB), lambda c, i: (jnp.maximum(i * r8 - 1, 0), base + c)),
                  pl.BlockSpec((4, CONV_CB), lambda c, i: (0, c)),
                  pl.BlockSpec((1, CONV_CB), lambda c, i: (0, c))],
        out_specs=pl.BlockSpec((tt, CONV_CB), lambda c, i: (i, c)),
        out_shape=jax.ShapeDtypeStruct((t, XBC), F32),
        compiler_params=_params(("parallel", "parallel"), 24 << 20),
    )(proj, proj, conv_w, conv_b)


def _conv_bwd(proj, dout, conv_w, conv_b, col0, width, name, dproj, skip=None):
    t = proj.shape[0]
    tt = _divtile(t, CONV_TT)
    nt = t // tt
    base = SEGS["xbc"][2] // CONV_CB + col0 // CONV_CB
    wb = col0 // CONV_CB
    r8 = tt // 8
    has_skip = skip is not None

    def body(*refs):
        if has_skip:
            u_ref, up_ref, d_ref, w_ref, b_ref, _, sk_ref, skw_ref, du_ref, dw_ref, db_ref, nx_ref = refs
        else:
            u_ref, up_ref, d_ref, w_ref, b_ref, _, du_ref, dw_ref, db_ref, nx_ref = refs
        i = pl.program_id(1)
        cur = u_ref[...]
        prev8 = jnp.where(i < nt - 1, up_ref[...], 0.0)
        sh = [_shift_down(cur, prev8, 3 - k) for k in range(3)] + [cur]
        pre = b_ref[...]
        for k in range(4):
            pre = pre + w_ref[k:k + 1, :] * sh[k]
        sg = _sigmoid(pre)
        dout_v = d_ref[...]
        if has_skip:
            dout_v = dout_v + sk_ref[...] * skw_ref[...]
        dpre = dout_v * (sg * (1.0 + pre * (1.0 - sg)))

        @pl.when(i == 0)
        def _():
            nx_ref[...] = jnp.zeros_like(nx_ref)
            dw_ref[...] = jnp.zeros_like(dw_ref)
            db_ref[...] = jnp.zeros_like(db_ref)

        next8 = nx_ref[...]
        du = w_ref[3:4, :] * dpre
        for s in range(1, 4):
            du = du + w_ref[3 - s:4 - s, :] * _shift_up(dpre, next8, s)
        du_ref[...] = du.astype(BF16)
        nx_ref[...] = dpre[:8]
        rows = [jnp.sum(dpre * sh[k], axis=0, keepdims=True) for k in range(4)]
        dw_ref[...] += jnp.concatenate(rows + [jnp.zeros((4, CONV_CB), F32)], axis=0)
        db_ref[...] += jnp.sum(dpre, axis=0, keepdims=True)

    rev = lambda c, i: (nt - 1 - i, c)
    p_in, p_out, p_shape = _dproj_piece(t, tt, CONV_CB, lambda c, i: (nt - 1 - i, base + c))
    in_specs = [pl.BlockSpec((tt, CONV_CB), lambda c, i: (nt - 1 - i, base + c)),
                pl.BlockSpec((8, CONV_CB), lambda c, i: (jnp.maximum((nt - 1 - i) * r8 - 1, 0), base + c)),
                pl.BlockSpec((tt, CONV_CB), rev),
                pl.BlockSpec((4, CONV_CB), lambda c, i: (0, wb + c)),
                pl.BlockSpec((1, CONV_CB), lambda c, i: (0, wb + c)), p_in]
    args = [proj, proj, dout, conv_w, conv_b, dproj]
    if has_skip:
        in_specs += [pl.BlockSpec((tt, CONV_CB), rev), pl.BlockSpec((1, CONV_CB), lambda c, i: (0, c))]
        args += [skip[0], skip[1]]
    return pl.pallas_call(
        body, name=name, grid=(width // CONV_CB, nt),
        in_specs=in_specs,
        out_specs=[p_out, pl.BlockSpec((8, CONV_CB), lambda c, i: (0, c)),
                   pl.BlockSpec((1, CONV_CB), lambda c, i: (0, c))],
        out_shape=[p_shape, jax.ShapeDtypeStruct((8, width), F32), jax.ShapeDtypeStruct((1, width), F32)],
        input_output_aliases={5: 0},
        scratch_shapes=[pltpu.VMEM((8, CONV_CB), F32)],
        compiler_params=_params(("parallel", "arbitrary"), 32 << 20),
    )(*args)


GW = SSD_R * SSD_P


def _expand8(v, passes=2):
    r = v.shape[0]
    if r < 8:
        v = jnp.broadcast_to(v, (8, SSD_R))
    ri = lax.broadcasted_iota(jnp.int32, (SSD_R, GW), 0)
    ci = lax.broadcasted_iota(jnp.int32, (SSD_R, GW), 1)
    spread = jnp.where((ci >= ri * SSD_P) & (ci < (ri + 1) * SSD_P), 1.0, 0.0)
    return _dot01(v, spread, passes)[:r]


def _head_pair_split(tile):
    first = lax.broadcasted_iota(jnp.int32, (1, 2 * SSD_P), 1) < SSD_P
    return jnp.where(first, tile, 0.0), jnp.where(first, 0.0, tile)


def _sel(rows, group):
    ri = lax.broadcasted_iota(jnp.int32, (rows, rows // group), 0)
    ci = lax.broadcasted_iota(jnp.int32, (rows, rows // group), 1)
    lo = ci * group
    return jnp.where((ri >= lo) & (ri < lo + group), 1.0, 0.0).astype(F32)


def _dot01(lhs, rhs, passes, split_lhs=True, dims=((1,), (0,))):
    val, m01 = (lhs, rhs) if split_lhs else (rhs, lhs)
    m01 = m01.astype(BF16)
    out = None
    for p in range(passes):
        piece = val.astype(BF16)
        ops = (piece, m01) if split_lhs else (m01, piece)
        d = lax.dot_general(ops[0], ops[1], (dims, ((), ())), preferred_element_type=F32)
        out = d if out is None else out + d
        if p + 1 < passes:
            val = val - piece.astype(F32)
    return out


def _ssd_chunk_terms(adt):
    li = lax.broadcasted_iota(jnp.int32, (CHUNK, CHUNK), 0)
    si = lax.broadcasted_iota(jnp.int32, (CHUNK, CHUNK), 1)
    causal = li >= si
    a_cs = _dot01(jnp.where(causal, 1.0, 0.0), adt, 3, split_lhs=False)
    a_cs_t = _dot01(adt, jnp.where(li <= si, 1.0, 0.0), 3, dims=((0,), (0,)))
    return a_cs, a_cs_t, causal


def _ssd_fwd(xc, dt3, adt3):
    t = xc.shape[0]
    nc = t // CHUNK

    gs = 2

    def body(xs_ref, b_ref, c_ref, dt_ref, adt_ref, y_ref, hp_ref, h_ref):
        @pl.when(pl.program_id(1) == 0)
        def _():
            h_ref[...] = jnp.zeros_like(h_ref)

        for gg in range(gs):
            a_cs, a_cs_t, causal = _ssd_chunk_terms(adt_ref[gg])
            a_last = a_cs[CHUNK - 1:CHUNK, :]
            h = h_ref[gg]
            hp_ref[gg, 0] = h
            xd = xs_ref[:, GW * gg:GW * (gg + 1)] * _expand8(dt_ref[gg])
            bb = b_ref[:, SSD_N * gg:SSD_N * (gg + 1)].astype(BF16)
            cbf = c_ref[:, SSD_N * gg:SSD_N * (gg + 1)].astype(BF16)
            cb = lax.dot_general(cbf, bb, (((1,), (1,)), ((), ())), preferred_element_type=F32)
            yoff = jnp.dot(cbf, h.astype(BF16), preferred_element_type=F32) * _expand8(jnp.exp(a_cs))
            for q in range(SSD_R // 2):
                lmats = []
                for r in (2 * q, 2 * q + 1):
                    seg = jnp.exp(jnp.where(causal, a_cs[:, r:r + 1] - a_cs_t[r:r + 1, :], -jnp.inf))
                    lmats.append((cb * seg).astype(BF16))
                tile = slice(2 * SSD_P * q, 2 * SSD_P * (q + 1))
                xa, xb = _head_pair_split(xd[:, tile])
                y_ref[:, GW * gg + 2 * SSD_P * q:GW * gg + 2 * SSD_P * (q + 1)] = (
                    jnp.dot(jnp.concatenate(lmats, axis=1), jnp.concatenate([xa, xb], axis=0).astype(BF16),
                            preferred_element_type=F32) + yoff[:, tile])
            xdd = (xd * _expand8(jnp.exp(a_last - a_cs))).astype(BF16)
            h_ref[gg] = _expand8(jnp.exp(a_last), 3) * h + lax.dot_general(
                bb, xdd, (((0,), (0,)), ((), ())), preferred_element_type=F32)

    nb = SSD_INNER // (gs * SSD_N)
    return pl.pallas_call(
        body, name="ssd_fwd", grid=(SSD_G // gs, nc),
        in_specs=[pl.BlockSpec((CHUNK, gs * GW), lambda g, c: (c, g)),
                  pl.BlockSpec((CHUNK, gs * SSD_N), lambda g, c: (c, nb + g)),
                  pl.BlockSpec((CHUNK, gs * SSD_N), lambda g, c: (c, nb + SSD_G // gs + g)),
                  pl.BlockSpec((gs, CHUNK, SSD_R), lambda g, c: (g, c, 0)),
                  pl.BlockSpec((gs, CHUNK, SSD_R), lambda g, c: (g, c, 0))],
        out_specs=[pl.BlockSpec((CHUNK, gs * GW), lambda g, c: (c, g)),
                   pl.BlockSpec((gs, 1, SSD_N, GW), lambda g, c: (g, c, 0, 0))],
        out_shape=[jax.ShapeDtypeStruct((t, SSD_INNER), F32), jax.ShapeDtypeStruct((SSD_G, nc, SSD_N, GW), F32)],
        scratch_shapes=[pltpu.VMEM((gs, SSD_N, GW), F32)],
        compiler_params=_params(("parallel", "arbitrary"), 32 << 20),
    )(xc, xc, xc, dt3, adt3)


def _ssd_bwd(xc, dt3, adt3, hprev, dy):
    t = xc.shape[0]
    nc = t // CHUNK

    gs = 2

    def body(xs_ref, b_ref, c_ref, dt_ref, adt_ref, hp_ref, dy_ref,
             dx_ref, db_ref, dc_ref, dadt_ref, dxdx_ref, dh_ref):
        @pl.when(pl.program_id(1) == 0)
        def _():
            dh_ref[...] = jnp.zeros_like(dh_ref)

        for gg in range(gs):
            wide = slice(GW * gg, GW * (gg + 1))
            narrow = slice(SSD_N * gg, SSD_N * (gg + 1))
            dx, db, dc, dadt, dxdx, dh_new = group_bwd(
                xs_ref[:, wide], b_ref[:, narrow], c_ref[:, narrow], dt_ref[gg], adt_ref[gg], hp_ref[gg, 0],
                dy_ref[:, wide], dh_ref[gg])
            dx_ref[:, wide] = dx
            db_ref[:, narrow] = db
            dc_ref[:, narrow] = dc
            dadt_ref[gg] = dadt
            dxdx_ref[gg] = dxdx
            dh_ref[gg] = dh_new

    def group_bwd(xs, b, c, dt, adt, hp, dyv, dh):
        a_cs, a_cs_t, causal = _ssd_chunk_terms(adt)
        a_last = a_cs[CHUNK - 1:CHUNK, :]
        e_last = jnp.exp(a_last)
        ex = _expand8(jnp.exp(a_cs))
        dtex = _expand8(jnp.exp(a_last - a_cs))
        dtx = _expand8(dt)
        sel = _sel(GW, SSD_P)
        seg8 = lambda v: _dot01(v, sel, 2)

        xd = xs * dtx
        xdd = xd * dtex
        bb = b.astype(BF16)
        cbf = c.astype(BF16)
        hpb = hp.astype(BF16)
        dhb = dh.astype(BF16)
        xdb = xd.astype(BF16)
        cb = lax.dot_general(cbf, bb, (((1,), (1,)), ((), ())), preferred_element_type=F32)
        dye = (dyv * ex).astype(BF16)
        yoff = jnp.dot(cbf, hpb, preferred_element_type=F32) * ex
        dc = lax.dot_general(dye, hpb, (((1,), (1,)), ((), ())), preferred_element_type=F32)
        bdh = jnp.dot(bb, dhb, preferred_element_type=F32)
        db = lax.dot_general(xdd.astype(BF16), dhb, (((1,), (1,)), ((), ())), preferred_element_type=F32)
        dxd_state = bdh * dtex
        q_terms = xdd * bdh
        d_a = seg8(dyv * yoff - q_terms)
        d_a_last = seg8(jnp.sum(q_terms, axis=0, keepdims=True)
                        + _expand8(e_last, 3) * jnp.sum(hp * dh, axis=0, keepdims=True))
        dh_new = (lax.dot_general(cbf, dye, (((0,), (0,)), ((), ())), preferred_element_type=F32)
                  + _expand8(e_last, 3) * dh)
        dcb = jnp.zeros((CHUNK, CHUNK), F32)
        w_all = []
        dxd_parts = []
        for q2 in range(SSD_R // 2):
            tile = slice(2 * SSD_P * q2, 2 * SSD_P * (q2 + 1))
            dy_pair = [part.astype(BF16) for part in _head_pair_split(dyv[:, tile])]
            lmats = []
            for k, r in enumerate((2 * q2, 2 * q2 + 1)):
                seg = jnp.exp(jnp.where(causal, a_cs[:, r:r + 1] - a_cs_t[r:r + 1, :], -jnp.inf))
                lmat = cb * seg
                dm = lax.dot_general(dy_pair[k], xdb[:, tile], (((1,), (1,)), ((), ())), preferred_element_type=F32)
                dcb = dcb + dm * seg
                w_all.append(dm * lmat)
                lmats.append(lmat.astype(BF16))
            dxd_parts.append(lax.dot_general(jnp.concatenate(lmats, axis=0), jnp.concatenate(dy_pair, axis=0),
                                             (((0,), (0,)), ((), ())), preferred_element_type=F32))
        row_sums = _dot01(jnp.concatenate(w_all, axis=1), _sel(SSD_R * CHUNK, CHUNK), 2)
        cs_rows = jnp.concatenate([jnp.sum(wr, axis=0, keepdims=True) for wr in w_all], axis=0)
        col_sums = _dot01(cs_rows, _sel(SSD_R, 1), 3, dims=((0,), (0,)))
        d_a = d_a + row_sums - col_sums
        li = lax.broadcasted_iota(jnp.int32, (CHUNK, SSD_R), 0)
        d_a = d_a + jnp.where(li == CHUNK - 1, d_a_last, 0.0)
        l2 = lax.broadcasted_iota(jnp.int32, (CHUNK, CHUNK), 0)
        s2 = lax.broadcasted_iota(jnp.int32, (CHUNK, CHUNK), 1)
        dadt = _dot01(jnp.where(s2 >= l2, 1.0, 0.0), d_a, 3, split_lhs=False)
        dxd = dxd_state + jnp.concatenate(dxd_parts, axis=1)
        dcbb = dcb.astype(BF16)
        db = db + lax.dot_general(dcbb, cbf, (((0,), (0,)), ((), ())), preferred_element_type=F32)
        dc = dc + jnp.dot(dcbb, bb, preferred_element_type=F32)
        return dxd * dtx, db, dc, dadt, seg8(dxd * xs), dh_new

    nb = SSD_INNER // (gs * SSD_N)
    rc = lambda g, c: (nc - 1 - c, g)
    r3 = lambda g, c: (g, nc - 1 - c, 0)
    return pl.pallas_call(
        body, name="ssd_bwd", grid=(SSD_G // gs, nc),
        in_specs=[pl.BlockSpec((CHUNK, gs * GW), rc),
                  pl.BlockSpec((CHUNK, gs * SSD_N), lambda g, c: (nc - 1 - c, nb + g)),
                  pl.BlockSpec((CHUNK, gs * SSD_N), lambda g, c: (nc - 1 - c, nb + SSD_G // gs + g)),
                  pl.BlockSpec((gs, CHUNK, SSD_R), r3),
                  pl.BlockSpec((gs, CHUNK, SSD_R), r3),
                  pl.BlockSpec((gs, 1, SSD_N, GW), lambda g, c: (g, nc - 1 - c, 0, 0)),
                  pl.BlockSpec((CHUNK, gs * GW), rc)],
        out_specs=[pl.BlockSpec((CHUNK, gs * GW), rc),
                   pl.BlockSpec((CHUNK, gs * SSD_N), rc),
                   pl.BlockSpec((CHUNK, gs * SSD_N), rc),
                   pl.BlockSpec((gs, CHUNK, SSD_R), r3),
                   pl.BlockSpec((gs, CHUNK, SSD_R), r3)],
        out_shape=[jax.ShapeDtypeStruct((t, SSD_INNER), F32),
                   jax.ShapeDtypeStruct((t, SSD_G * SSD_N), F32),
                   jax.ShapeDtypeStruct((t, SSD_G * SSD_N), F32),
                   jax.ShapeDtypeStruct((SSD_G, t, SSD_R), F32),
                   jax.ShapeDtypeStruct((SSD_G, t, SSD_R), F32)],
        scratch_shapes=[pltpu.VMEM((gs, SSD_N, GW), F32)],
        compiler_params=_params(("parallel", "arbitrary"), 48 << 20),
    )(xc, xc, xc, dt3, adt3, hprev, dy)


def _gated_norm_fwd(y, xc, proj, dexp, ng):
    t = y.shape[0]
    tt = _divtile(t, 256)

    def body(y_ref, x_ref, z_ref, d_ref, g_ref, o_ref):
        z = z_ref[...]
        y2 = (y_ref[...] + d_ref[...] * x_ref[...]) * (z * _sigmoid(z))
        for gi in range(SSD_G):
            sl = slice(GW * gi, GW * (gi + 1))
            seg = y2[:, sl]
            rinv = lax.rsqrt(jnp.mean(seg * seg, axis=-1, keepdims=True) + RMS_EPS)
            o_ref[:, sl] = (seg * rinv * g_ref[:, sl]).astype(BF16)

    row = pl.BlockSpec((tt, SSD_INNER), lambda i: (i, 0))
    vec = pl.BlockSpec((1, SSD_INNER), lambda i: (0, 0))
    return pl.pallas_call(
        body, name="gated_norm_fwd", grid=(t // tt,), in_specs=[row, row, row, vec, vec], out_specs=row,
        out_shape=jax.ShapeDtypeStruct((t, SSD_INNER), BF16),
        compiler_params=_params(("parallel",), 48 << 20),
    )(y, xc, proj, dexp, ng)


def _gated_norm_bwd(dout, y, xc, proj, dexp, ng, dproj):
    t = y.shape[0]
    tt = _divtile(t, 128)

    def body(do_ref, y_ref, x_ref, z_ref, d_ref, g_ref, _, dz_ref, dy_ref, dg_ref, dd_ref):
        z = z_ref[...]
        sg = _sigmoid(z)
        sz = z * sg
        xs = x_ref[...]
        y1 = y_ref[...] + d_ref[...] * xs
        y2 = y1 * sz
        dov = do_ref[...]

        @pl.when(pl.program_id(0) == 0)
        def _():
            dg_ref[...] = jnp.zeros_like(dg_ref)
            dd_ref[...] = jnp.zeros_like(dd_ref)

        for gi in range(SSD_G):
            sl = slice(GW * gi, GW * (gi + 1))
            seg = y2[:, sl]
            rinv = lax.rsqrt(jnp.mean(seg * seg, axis=-1, keepdims=True) + RMS_EPS)
            yn = seg * rinv
            dsl = dov[:, sl]
            dg_ref[:, sl] += jnp.sum(dsl * yn, axis=0, keepdims=True)
            dyn = dsl * g_ref[:, sl]
            dy2 = rinv * (dyn - yn * jnp.mean(dyn * yn, axis=-1, keepdims=True))
            dz_ref[:, sl] = (dy2 * y1[:, sl] * (sg[:, sl] * (1.0 + z[:, sl] * (1.0 - sg[:, sl])))).astype(BF16)
            dy1 = dy2 * sz[:, sl]
            dy_ref[:, sl] = dy1
            dd_ref[:, sl] += jnp.sum(dy1 * xs[:, sl], axis=0, keepdims=True)

    row = pl.BlockSpec((tt, SSD_INNER), lambda i: (i, 0))
    vec = pl.BlockSpec((1, SSD_INNER), lambda i: (0, 0))
    p_in, p_out, p_shape = _dproj_piece(t, tt, SSD_INNER, lambda i: (i, 0))
    return pl.pallas_call(
        body, name="gated_norm_bwd", grid=(t // tt,), in_specs=[row, row, row, row, vec, vec, p_in],
        out_specs=[p_out, row, vec, vec],
        out_shape=[p_shape, jax.ShapeDtypeStruct((t, SSD_INNER), F32),
                   jax.ShapeDtypeStruct((1, SSD_INNER), F32), jax.ShapeDtypeStruct((1, SSD_INNER), F32)],
        input_output_aliases={6: 0},
        compiler_params=_params(("arbitrary",), 48 << 20),
    )(dout, y, xc, proj, dexp, ng, dproj)


def _fold_heads(v, name):
    def body(v_ref, o_ref):
        ri = lax.broadcasted_iota(jnp.int32, (SSD_INNER, 128), 0)
        ci = lax.broadcasted_iota(jnp.int32, (SSD_INNER, 128), 1)
        fold = jnp.where((ri >= ci * SSD_P) & (ri < (ci + 1) * SSD_P), 1.0, 0.0).astype(F32)
        o_ref[...] = jnp.dot(v_ref[...], fold, preferred_element_type=F32, precision=HI)

    return pl.pallas_call(body, name=name, out_shape=jax.ShapeDtypeStruct((1, 128), F32))(v)


Q_BLK = SEGS["q"][2] // QW
K_BLK = SEGS["k"][2] // KVW
V_BLK = SEGS["v"][2] // KVW


def _rope_tables(pos_ref, invf_ref, width):
    ang = pos_ref[...] * invf_ref[...]
    lane = lax.broadcasted_iota(jnp.int32, (1, 128), 1)
    sign = jnp.where((lane % HD) < (HD // 2), -1.0, 1.0)
    cos = jnp.tile(jnp.cos(ang), (1, width // 128))
    sin = jnp.tile(sign * jnp.sin(ang), (1, width // 128))
    first = (lax.broadcasted_iota(jnp.int32, (1, width), 1) % HD) < (HD // 2)
    return cos, sin, first


def _rot_half(u, first):
    w = u.shape[1]
    return jnp.where(first, pltpu.roll(u, w - HD // 2, 1), pltpu.roll(u, HD // 2, 1))


def _rope_fwd(proj, pos, invf):
    t = proj.shape[0]
    tt = _divtile(t, 512)

    def body(q_ref, k_ref, pos_ref, invf_ref, qo_ref, ko_ref):
        cos, sin, first = _rope_tables(pos_ref, invf_ref, QW)
        q = q_ref[...]
        qr = q * cos + _rot_half(q, first) * sin
        for p in range(QW // 128):
            qo_ref[128 * p:128 * (p + 1), :] = qr[:, 128 * p:128 * (p + 1)].T.astype(BF16)
        k = k_ref[...]
        ko_ref[...] = (k * cos[:, :KVW] + _rot_half(k, first[:, :KVW]) * sin[:, :KVW]).astype(BF16)

    return pl.pallas_call(
        body, name="rope_fwd", grid=(t // tt,),
        in_specs=[pl.BlockSpec((tt, QW), lambda i: (i, Q_BLK)), pl.BlockSpec((tt, KVW), lambda i: (i, K_BLK)),
                  pl.BlockSpec((tt, 1), lambda i: (i, 0)), pl.BlockSpec((1, 128), lambda i: (0, 0))],
        out_specs=[pl.BlockSpec((QW, tt), lambda i: (0, i)), pl.BlockSpec((tt, KVW), lambda i: (i, 0))],
        out_shape=[jax.ShapeDtypeStruct((QW, t), BF16), jax.ShapeDtypeStruct((t, KVW), BF16)],
        compiler_params=_params(("parallel",), 40 << 20),
    )(proj, proj, pos, invf)


def _rope_bwd(dqt, dk, pos, invf, dproj):
    t = dk.shape[0]
    tt = _divtile(t, 512)

    def body(dq_ref, dk_ref, pos_ref, invf_ref, _, qo_ref, ko_ref):
        cos, sin, first = _rope_tables(pos_ref, invf_ref, QW)
        q = jnp.concatenate([dq_ref[128 * p:128 * (p + 1), :].T for p in range(QW // 128)], axis=1)
        qo_ref[...] = (q * cos + _rot_half(q * sin, first)).astype(BF16)
        k = dk_ref[...]
        ko_ref[...] = (k * cos[:, :KVW] + _rot_half(k * sin[:, :KVW], first[:, :KVW])).astype(BF16)

    p_in, p_out, p_shape = _dproj_piece(t, tt, QW, lambda i: (i, Q_BLK))
    return pl.pallas_call(
        body, name="rope_bwd", grid=(t // tt,),
        in_specs=[pl.BlockSpec((QW, tt), lambda i: (0, i)), pl.BlockSpec((tt, KVW), lambda i: (i, 0)),
                  pl.BlockSpec((tt, 1), lambda i: (i, 0)), pl.BlockSpec((1, 128), lambda i: (0, 0)), p_in],
        out_specs=[p_out, pl.BlockSpec((tt, KVW), lambda i: (i, 0))],
        out_shape=[p_shape, jax.ShapeDtypeStruct((t, KVW), BF16)],
        input_output_aliases={4: 0},
        compiler_params=_params(("parallel",), 40 << 20),
    )(dqt, dk, pos, invf, dproj)


def _place_cols(piece, dproj, col_blk, name):
    t, w = piece.shape
    tt = _divtile(t, 1024)

    def body(p_ref, _, o_ref):
        o_ref[...] = p_ref[...]

    p_in, p_out, p_shape = _dproj_piece(t, tt, w, lambda i: (i, col_blk))
    return pl.pallas_call(
        body, name=name, grid=(t // tt,),
        in_specs=[pl.BlockSpec((tt, w), lambda i: (i, 0)), p_in], out_specs=p_out, out_shape=p_shape,
        input_output_aliases={1: 0},
        compiler_params=_params(("parallel",), 16 << 20),
    )(piece, dproj)


GQ = NQ // NKV
NT_DIMS = (((1,), (1,)), ((), ()))
TN_DIMS = (((0,), (0,)), ((), ()))


def _attn_heads(ref, j, dtype=None):
    out = jnp.concatenate([ref[HD * h:HD * (h + 1), :] for h in range(j * GQ, (j + 1) * GQ)], axis=1)
    return out if dtype is None else out.astype(dtype)


def _attn_sink_row(s_ref, j):
    return jnp.concatenate([jnp.broadcast_to(s_ref[:, h:h + 1], (1, WINDOW)) for h in range(j * GQ, (j + 1) * GQ)],
                           axis=1)


def _attn_mask(n):
    kr = lax.broadcasted_iota(jnp.int32, (2 * WINDOW, GQ * WINDOW), 0)
    qi = lax.broadcasted_iota(jnp.int32, (2 * WINDOW, GQ * WINDOW), 1) % WINDOW
    return (kr > qi) & (kr <= qi + WINDOW) & ((n > 0) | (kr >= WINDOW))


def _attn_probs(qgt, kk, sink, mask):
    s = jnp.where(mask, jnp.dot(kk, qgt, preferred_element_type=F32) * (HD ** -0.5), -jnp.inf)
    m = jnp.maximum(jnp.max(s, axis=0, keepdims=True), sink)
    p = jnp.exp(s - m)
    ps = jnp.exp(sink - m)
    inv = 1.0 / (jnp.sum(p, axis=0, keepdims=True) + ps)
    return p * inv, ps * inv


def _attn_fwd(qt, kr, proj, sinks):
    t = kr.shape[0]
    nb = t // WINDOW

    def body(q_ref, kc_ref, kp_ref, vc_ref, vp_ref, s_ref, o_ref):
        mask = _attn_mask(pl.program_id(0))
        for j in range(NKV):
            ks = slice(HD * j, HD * (j + 1))
            kk = jnp.concatenate([kp_ref[:, ks], kc_ref[:, ks]], axis=0)
            vv = jnp.concatenate([vp_ref[:, ks], vc_ref[:, ks]], axis=0).astype(BF16)
            pn, _ = _attn_probs(_attn_heads(q_ref, j), kk, _attn_sink_row(s_ref, j), mask)
            ot = lax.dot_general(vv, pn.astype(BF16), TN_DIMS, preferred_element_type=F32).astype(BF16)
            for g in range(GQ):
                h = j * GQ + g
                o_ref[HD * h:HD * (h + 1), :] = ot[:, WINDOW * g:WINDOW * (g + 1)]

    prev = lambda n: (jnp.maximum(n - 1, 0), 0)
    return pl.pallas_call(
        body, name="attn_fwd", grid=(nb,),
        in_specs=[pl.BlockSpec((QW, WINDOW), lambda n: (0, n)),
                  pl.BlockSpec((WINDOW, KVW), lambda n: (n, 0)), pl.BlockSpec((WINDOW, KVW), prev),
                  pl.BlockSpec((WINDOW, KVW), lambda n: (n, V_BLK)),
                  pl.BlockSpec((WINDOW, KVW), lambda n: (jnp.maximum(n - 1, 0), V_BLK)),
                  pl.BlockSpec((1, 128), lambda n: (0, 0))],
        out_specs=pl.BlockSpec((QW, WINDOW), lambda n: (0, n)),
        out_shape=jax.ShapeDtypeStruct((QW, t), BF16),
        compiler_params=_params(("parallel",), 24 << 20),
    )(qt, kr, kr, proj, proj, sinks)


def _attn_bwd(qt, kr, proj, sinks, dot_, dproj):
    t = kr.shape[0]
    nb = t // WINDOW

    def body(q_ref, kc_ref, kp_ref, vc_ref, vp_ref, s_ref, do_ref, _,
             dq_ref, dk_ref, dv_ref, ds_ref, dkc_ref, dvc_ref):
        i = pl.program_id(0)
        mask = _attn_mask(nb - 1 - i)

        @pl.when(i == 0)
        def _():
            dkc_ref[...] = jnp.zeros_like(dkc_ref)
            dvc_ref[...] = jnp.zeros_like(dvc_ref)
            ds_ref[...] = jnp.zeros_like(ds_ref)

        lane = lax.broadcasted_iota(jnp.int32, (1, 128), 1)
        ds_acc = jnp.zeros((1, 128), F32)
        for j in range(NKV):
            ks = slice(HD * j, HD * (j + 1))
            kk = jnp.concatenate([kp_ref[:, ks], kc_ref[:, ks]], axis=0)
            vv = jnp.concatenate([vp_ref[:, ks], vc_ref[:, ks]], axis=0).astype(BF16)
            qgt = _attn_heads(q_ref, j)
            pn, psn = _attn_probs(qgt, kk, _attn_sink_row(s_ref, j), mask)
            dogt = _attn_heads(do_ref, j)
            dp = jnp.dot(vv, dogt, preferred_element_type=F32)
            delta = jnp.sum(dp * pn, axis=0, keepdims=True)
            dsb = (pn * (dp - delta) * (HD ** -0.5)).astype(BF16)
            dsink = -psn * delta
            dqt = lax.dot_general(kk, dsb, TN_DIMS, preferred_element_type=F32)
            for g in range(GQ):
                h = j * GQ + g
                cols = slice(WINDOW * g, WINDOW * (g + 1))
                dq_ref[HD * h:HD * (h + 1), :] = dqt[:, cols]
                ds_acc = ds_acc + jnp.where(lane == h, jnp.sum(dsink[:, cols], axis=1, keepdims=True), 0.0)
            dkk = lax.dot_general(dsb, qgt, NT_DIMS, preferred_element_type=F32)
            dvv = lax.dot_general(pn.astype(BF16), dogt, NT_DIMS, preferred_element_type=F32)
            dk_ref[:, ks] = dkk[WINDOW:] + dkc_ref[:, ks]
            dv_ref[:, ks] = (dvv[WINDOW:] + dvc_ref[:, ks]).astype(BF16)
            dkc_ref[:, ks] = dkk[:WINDOW]
            dvc_ref[:, ks] = dvv[:WINDOW]
        ds_ref[...] += ds_acc

    cur = lambda i: (nb - 1 - i, 0)
    cur_t = lambda i: (0, nb - 1 - i)
    prev = lambda i: (jnp.maximum(nb - 2 - i, 0), 0)
    p_in, p_out, p_shape = _dproj_piece(t, WINDOW, KVW, lambda i: (nb - 1 - i, V_BLK))
    return pl.pallas_call(
        body, name="attn_bwd", grid=(nb,),
        in_specs=[pl.BlockSpec((QW, WINDOW), cur_t),
                  pl.BlockSpec((WINDOW, KVW), cur), pl.BlockSpec((WINDOW, KVW), prev),
                  pl.BlockSpec((WINDOW, KVW), lambda i: (nb - 1 - i, V_BLK)),
                  pl.BlockSpec((WINDOW, KVW), lambda i: (jnp.maximum(nb - 2 - i, 0), V_BLK)),
                  pl.BlockSpec((1, 128), lambda i: (0, 0)),
                  pl.BlockSpec((QW, WINDOW), cur_t), p_in],
        out_specs=[pl.BlockSpec((QW, WINDOW), cur_t), pl.BlockSpec((WINDOW, KVW), cur),
                   p_out, pl.BlockSpec((1, 128), lambda i: (0, 0))],
        out_shape=[jax.ShapeDtypeStruct((QW, t), F32), jax.ShapeDtypeStruct((t, KVW), F32),
                   p_shape, jax.ShapeDtypeStruct((1, 128), F32)],
        input_output_aliases={7: 2},
        scratch_shapes=[pltpu.VMEM((WINDOW, KVW), F32), pltpu.VMEM((WINDOW, KVW), F32)],
        compiler_params=_params(("arbitrary",), 32 << 20),
    )(qt, kr, kr, proj, proj, sinks, dot_, dproj)


GS_BLK = SEGS["gs"][2] // D
GA_BLK = SEGS["ga"][2] // D


def _merge_fwd(ys, ya, proj):
    t = ys.shape[0]
    tt = _divtile(t, 256)

    def body(ys_ref, ya_ref, gs_ref, ga_ref, o_ref):
        o_ref[...] = (_sigmoid(gs_ref[...]) * ys_ref[...] + _sigmoid(ga_ref[...]) * ya_ref[...]).astype(BF16)

    row = pl.BlockSpec((tt, D), lambda i: (i, 0))
    return pl.pallas_call(
        body, name="merge_fwd", grid=(t // tt,),
        in_specs=[row, row, pl.BlockSpec((tt, D), lambda i: (i, GS_BLK)), pl.BlockSpec((tt, D), lambda i: (i, GA_BLK))],
        out_specs=row, out_shape=jax.ShapeDtypeStruct((t, D), BF16),
        compiler_params=_params(("parallel",), 32 << 20),
    )(ys, ya, proj, proj)


def _dproj_piece(t, rows, width, index_map):
    return (pl.BlockSpec(memory_space=pl.ANY), pl.BlockSpec((rows, width), index_map),
            jax.ShapeDtypeStruct((t, PROJ_PAD), BF16))


def _merge_bwd(dm, ys, ya, proj, dproj):
    t = ys.shape[0]
    tt = _divtile(t, 256)

    def body(dm_ref, ys_ref, ya_ref, gs_ref, ga_ref, _, dys_ref, dya_ref, dg_ref):
        d = dm_ref[...]
        s = _sigmoid(gs_ref[...])
        a = _sigmoid(ga_ref[...])
        dys_ref[...] = (d * s).astype(BF16)
        dya_ref[...] = (d * a).astype(BF16)
        dg_ref[:, :D] = (d * ys_ref[...] * (s * (1.0 - s))).astype(BF16)
        dg_ref[:, D:] = (d * ya_ref[...] * (a * (1.0 - a))).astype(BF16)

    row = pl.BlockSpec((tt, D), lambda i: (i, 0))
    p_in, p_out, p_shape = _dproj_piece(t, tt, 2 * D, lambda i: (i, SEGS["gs"][2] // (2 * D)))
    return pl.pallas_call(
        body, name="merge_bwd", grid=(t // tt,),
        in_specs=[row, row, row, pl.BlockSpec((tt, D), lambda i: (i, GS_BLK)),
                  pl.BlockSpec((tt, D), lambda i: (i, GA_BLK)), p_in],
        out_specs=[row, row, p_out], out_shape=[jax.ShapeDtypeStruct((t, D), BF16)] * 2 + [p_shape],
        input_output_aliases={5: 2},
        compiler_params=_params(("parallel",), 40 << 20),
    )(dm, ys, ya, proj, proj, dproj)


def _pad128(v):
    return jnp.pad(v, ((0, 0), (0, 128 - v.shape[1])))


def _local_step(x, pos, target, w, small, fetch=None, early_grads=None):
    w = dict(w)
    xb = x.astype(BF16)
    if fetch is None:
        gu1, a1 = _mm_swiglu(xb, w["gu1"], "ffn1_gu")
    else:
        own = _mm_swiglu(xb, w["gu1_own"], "ffn1_gu_own", chip_idx=w["chip_idx"])
        w.update(fetch(0, own[1]))
        gu1, a1 = _mm_swiglu(xb, w["gu1"], "ffn1_gu_rest", chip_idx=w["chip_idx"], done=own)
        w.update(fetch(1, a1))
    f1 = _mm(a1, w["d1"], "nn", F32, "ffn1_down", caps=(512, 1024, FFN_H))
    h1, h1b, xh1, rs1 = _ln_fwd(x, f1, small["ln1_g"], small["ln1_b"], 0.5, "ln1_fwd")
    if fetch is not None:
        w.update(fetch(2, h1b))
    proj = _mm(h1b, w["win"], "nn", F32, "proj", caps=(1024, 896, 2048))
    if fetch is not None:
        w.update(fetch(3, proj))
    bias128 = _pad128(small["dt_bias"])
    alog128 = _pad128(small["a_log"])
    dt3, adt3 = _dt_prep(proj, bias128, alog128)
    xc = _conv_fwd(proj, small["conv_w"], small["conv_b"])
    y_ssd, hprev = _ssd_fwd(xc, dt3, adt3)
    dexp = jnp.repeat(small["d_skip"], SSD_P, axis=1)
    ysn = _gated_norm_fwd(y_ssd, xc, proj, dexp, small["ssd_norm_g"])
    ys = _mm(ysn, w["so"], "nn", F32, "ssd_out")
    invf = jnp.tile(ROPE_THETA ** (-jnp.arange(HD // 2, dtype=F32) * 2.0 / HD), 4)[None, :]
    qt, kr = _rope_fwd(proj, pos, invf)
    sinks128 = _pad128(small["attn_sinks"])
    ot = _attn_fwd(qt, kr, proj, sinks128)
    ya = _mm(ot, w["ao"], "tn", F32, "attn_out")
    mg = _merge_fwd(ys, ya, proj)
    mix = _mm(mg, w["out"], "nn", F32, "mix_out")
    h2, h2b, xh2, rs2 = _ln_fwd(h1, mix, small["ln2_g"], small["ln2_b"], 1.0, "ln2_fwd")
    gu2, a2 = _mm_swiglu(h2b, w["gu2"], "ffn2_gu")
    f2 = _mm(a2, w["d2"], "nn", F32, "ffn2_down", caps=(512, 1024, FFN_H))
    _, _, xh3, rs3, dh3, loss = _ln_fwd(h2, f2, small["ln3_g"], small["ln3_b"], 0.5, "ln3_fwd", target=target)

    gw, gs = {}, {}
    dr3, dr3h, gs["ln3_g"], gs["ln3_b"] = _ln_bwd(dh3, xh3, rs3, small["ln3_g"], 0.5, "ln3_bwd")
    gw["d2"] = _mm(a2, dr3h, "tn", F32, "ffn2_down_dw")
    dgu2 = _mm_swiglu_bwd(dr3h, w["d2"], gu2, "ffn2_down_dx")
    gw["gu2"] = _mm(h2b, dgu2, "tn", F32, "ffn2_gu_dw", caps=(1024, 1408, 2048), n_slabs=N_CHIPS)
    dh2 = _mm(dgu2, w["gu2"], "nt", F32, "ffn2_gu_dx", add=dr3, add_scale=ALPHA, caps=(1024, 1024, 2816))
    dr2, dr2b, gs["ln2_g"], gs["ln2_b"] = _ln_bwd(dh2, xh2, rs2, small["ln2_g"], 1.0, "ln2_bwd")
    gw["out"] = _mm(mg, dr2b, "tn", F32, "mix_out_dw")
    dmg = _mm(dr2b, w["out"], "nt", F32, "mix_out_dx")
    dproj = lax.empty((x.shape[0], PROJ_PAD), BF16)
    dys, dya, dproj = _merge_bwd(dmg, ys, ya, proj, dproj)
    gw["ao"] = _mm(ot, dya, "nn", F32, "attn_out_dw")
    dot_ = _mm(w["ao"], dya, "nt", BF16, "attn_out_dx")
    dqt, dkr, dproj, gs["attn_sinks"] = _attn_bwd(qt, kr, proj, sinks128, dot_, dproj)
    dproj, dk = _rope_bwd(dqt, dkr, pos, invf, dproj)
    dproj = _place_cols(dk, dproj, K_BLK, "place_dk")
    gw["so"] = _mm(ysn, dys, "tn", F32, "ssd_out_dw")
    dysn = _mm(dys, w["so"], "nt", F32, "ssd_out_dx")
    dproj, dy1, gs["ssd_norm_g"], dd_ch = _gated_norm_bwd(dysn, y_ssd, xc, proj, dexp, small["ssd_norm_g"], dproj)
    gs["d_skip"] = _fold_heads(dd_ch, "d_skip_fold")
    dxs, db, dc, dadt3, dxdx3 = _ssd_bwd(xc, dt3, adt3, hprev, dy1)
    ddt, gs["dt_bias"], gs["a_log"] = _dt_bwd(dadt3, dxdx3, proj, bias128, alog128)
    dproj = _place_cols(ddt, dproj, DT_BLK, "place_ddt")
    cw, cbias = small["conv_w"], small["conv_b"]
    dproj, dwx, dbx = _conv_bwd(proj, dxs, cw, cbias, 0, SSD_INNER, "conv_bwd_x", dproj, skip=(dy1, dexp))
    dproj, dwb, dbb = _conv_bwd(proj, db, cw, cbias, SSD_INNER, SSD_G * SSD_N, "conv_bwd_b", dproj)
    dproj, dwc, dbc = _conv_bwd(proj, dc, cw, cbias, SSD_INNER + SSD_G * SSD_N, SSD_G * SSD_N, "conv_bwd_c", dproj)
    gs["conv_w"] = jnp.concatenate([dwx[:4], dwb[:4], dwc[:4]], axis=1)
    gs["conv_b"] = jnp.concatenate([dbx, dbb, dbc], axis=1)
    gw["win"] = _mm(h1b, dproj, "tn", F32, "proj_dw", caps=(1024, 896, 2048))
    win = w["win"] if early_grads is None else early_grads[0](gw, w["win"])
    dh1 = _mm(dproj, win, "nt", F32, "proj_dx", add=dr2, add_scale=ALPHA, caps=(1024, 1024, 2432))
    ln1_g = small["ln1_g"]
    if early_grads is not None:
        ln1_g = ln1_g + early_grads[1](dh1)[0:1, 0:1]
    dr1, dr1h, gs["ln1_g"], gs["ln1_b"] = _ln_bwd(dh1, xh1, rs1, ln1_g, 0.5, "ln1_bwd")
    gw["d1"] = _mm(a1, dr1h, "tn", F32, "ffn1_down_dw")
    dgu1 = _mm_swiglu_bwd(dr1h, w["d1"], gu1, "ffn1_down_dx")
    gw["gu1"] = _mm(xb, dgu1, "tn", F32, "ffn1_gu_dw", caps=(1024, 1408, 2048), n_slabs=N_CHIPS)
    gu1w = w["gu1"] if early_grads is None else early_grads[2](gw, w["gu1"])
    grad_x = _mm(dgu1, gu1w, "nt", F32, "ffn1_gu_dx", add=dr1, add_scale=ALPHA, caps=(1024, 1024, 2816))
    return loss, grad_x, gw, gs


MESH = pl.DeviceIdType.MESH
ANY = pl.BlockSpec(memory_space=pl.ANY)


def _place():
    x, y, c = lax.axis_index("x"), lax.axis_index("y"), lax.axis_index("c")
    peers = [(1 - x, y), (x, 1 - y), (1 - x, 1 - y)]
    return x, y, c, peers


BIG = [
    ("ffn1_w_gate", D, SHARD_H, "gu1", "col", 0),
    ("ffn1_w_up", D, SHARD_H, "gu1", "col", SHARD_H),
    ("ffn1_w_down", SHARD_H, D, "d1", "row", 0),
    ("w_in", D, SHARD_IN, "win4", "lead", 0),
    ("w_ssd_o", SSD_INNER // N_CHIPS, D, "so", "row", 0),
    ("w_attn_o", D // N_CHIPS, D, "ao", "row", 0),
    ("w_out", D // N_CHIPS, D, "out", "row", 0),
    ("ffn2_w_gate", D, SHARD_H, "gu2", "col", 0),
    ("ffn2_w_up", D, SHARD_H, "gu2", "col", SHARD_H),
    ("ffn2_w_down", SHARD_H, D, "d2", "row", 0),
]
GATHERED = {"gu1": (D, 2 * FFN_H), "d1": (FFN_H, D), "win4": (N_CHIPS, D, SHARD_IN), "so": (SSD_INNER, D),
            "ao": (D, D), "out": (D, D), "gu2": (D, 2 * FFN_H), "d2": (FFN_H, D)}


def _cast_place(srcs, oname, chip_idx, also_alone=False):
    rows, cols = srcs[0].shape
    tr = _divtile(rows, 256, 16)
    kind = [b[4] for b in BIG if b[3] == oname][0]
    n_src = len(srcs)

    def body(chip_ref, *refs):
        for o_ref in refs[n_src:]:
            for k, s_ref in enumerate(refs[:n_src]):
                o_ref[:, k * cols:(k + 1) * cols] = s_ref[...].astype(BF16)

    nt = rows // tr
    if kind == "col":
        o_spec = pl.BlockSpec((tr, n_src * cols), lambda i, chip_ref: (i, chip_ref[0]))
    elif kind == "row":
        o_spec = pl.BlockSpec((tr, cols), lambda i, chip_ref: (chip_ref[0] * nt + i, 0))
    else:
        o_spec = pl.BlockSpec((None, tr, cols), lambda i, chip_ref: (chip_ref[0], i, 0))
    out_specs, out_shape = [o_spec], [jax.ShapeDtypeStruct(GATHERED[oname], BF16)]
    if also_alone:
        out_specs.append(pl.BlockSpec((tr, n_src * cols), lambda i, chip_ref: (i, 0)))
        out_shape.append(jax.ShapeDtypeStruct((rows, n_src * cols), BF16))
    res = pl.pallas_call(
        body, name="cast_place_" + oname,
        grid_spec=pltpu.PrefetchScalarGridSpec(
            num_scalar_prefetch=1, grid=(nt,),
            in_specs=[pl.BlockSpec((tr, cols), lambda i, chip_ref: (i, 0))] * n_src, out_specs=out_specs),
        out_shape=out_shape,
        compiler_params=_params(("parallel",), 32 << 20),
    )(chip_idx, *srcs)
    return res if also_alone else res[0]


def _slot(outs, entry, j, half):
    _, rows, cols, oname, kind, off = entry
    o = outs[oname]
    hr = rows // 2
    if kind == "col":
        cs = pl.ds(pl.multiple_of(j * (2 * SHARD_H) + off, 128), cols)
        return o.at[pl.ds(pl.multiple_of(half * hr, 16), hr), cs]
    if kind == "row":
        return o.at[pl.ds(pl.multiple_of(j * rows + half * hr, 16), hr), :]
    return o.at[j, pl.ds(pl.multiple_of(half * hr, 16), hr), :]


HBM = pl.BlockSpec(memory_space=pltpu.HBM)
SEM = pl.BlockSpec(memory_space=pltpu.SEMAPHORE)


def _ici_copy(outs, entry, j, c, to, send, recv, k):
    ref = _slot(outs, entry, j, c)
    return pltpu.make_async_remote_copy(src_ref=ref, dst_ref=ref, send_sem=send.at[k], recv_sem=recv.at[k],
                                        device_id=to, device_id_type=MESH)


GATHER_GROUPS = [["gu1"], ["d1"], ["win4"], ["so", "ao", "out", "gu2", "d2"]]


def _gather_ici_start(placed, groups, tag, carried):
    names = [k for grp in groups for k in grp]
    bigs = [[b for b in BIG if b[3] in grp] for grp in groups]
    ng = len(groups)
    n_in = len(names) + 1

    def body(*refs):
        sems = refs[n_in:n_in + 2 * ng]
        outs = dict(zip(names, refs[n_in + 2 * ng:n_in + 2 * ng + len(names)]))
        token = refs[-1]
        x, y, c, peers = _place()
        for gi, big in enumerate(bigs):
            for i, entry in enumerate(big):
                for k, (px, py) in enumerate(peers):
                    _ici_copy(outs, entry, 2 * x + y, c, (px, py, c), sems[2 * gi], sems[2 * gi + 1], 3 * i + k).start()
        token[...] = jnp.zeros_like(token)

    sem_shapes = [pltpu.SemaphoreType.DMA((3 * len(big),)) for big in bigs for _ in range(2)]
    res = pl.pallas_call(
        body, name="gather_ici_start_" + tag,
        in_specs=[HBM] * n_in,
        out_specs=[SEM] * (2 * ng) + [HBM] * n_in + [pl.BlockSpec(memory_space=pltpu.VMEM)],
        out_shape=sem_shapes + [pltpu.HBM(GATHERED[k], BF16) for k in names]
        + [pltpu.HBM(carried.shape, carried.dtype), jax.ShapeDtypeStruct((8, 128), F32)],
        input_output_aliases={i: i + 2 * ng for i in range(n_in)},
        compiler_params=pltpu.CompilerParams(has_side_effects=pltpu.SideEffectType.DATAFLOW_SIDE_EFFECTING),
    )(*[pltpu.with_memory_space_constraint(a, pltpu.HBM) for a in [placed[k] for k in names] + [carried]])
    sems = [(res[2 * gi], res[2 * gi + 1]) for gi in range(ng)]
    return sems, dict(zip(names, res[2 * ng:2 * ng + len(names)])), res[2 * ng + len(names)]


def _gather_ici_wait(send, recv, arrays, names, after, tag):
    big = [b for b in BIG if b[3] in names]

    def body(*refs):
        outs = dict(zip(names, refs[:len(names)]))
        send_ref, recv_ref = refs[len(names)], refs[len(names) + 1]
        x, y, c, peers = _place()
        for i, entry in enumerate(big):
            for k, (px, py) in enumerate(peers):
                mine = _ici_copy(outs, entry, 2 * x + y, c, (px, py, c), send_ref, recv_ref, 3 * i + k)
                mine.wait_send()
                theirs = _ici_copy(outs, entry, 2 * px + py, c, (px, py, c), send_ref, recv_ref, 3 * i + k)
                theirs.wait_recv()

    res = pl.pallas_call(
        body, name="gather_ici_wait_" + tag,
        in_specs=[HBM] * len(names) + [SEM, SEM, pl.BlockSpec(memory_space=pl.ANY)],
        out_specs=[HBM] * len(names),
        out_shape=[pltpu.HBM(GATHERED[k], BF16) for k in names],
        input_output_aliases={i: i for i in range(len(names))},
        compiler_params=pltpu.CompilerParams(has_side_effects=pltpu.SideEffectType.DATAFLOW_SIDE_EFFECTING),
    )(*[arrays[k] for k in names], send, recv, after)
    return dict(zip(names, res))


def _gather_d2d(arrays, names, tag):
    big = [b for b in BIG if b[3] in names]
    n = len(big)

    def body(*refs):
        outs = dict(zip(names, refs[len(names):2 * len(names)]))
        fsend, frecv = refs[2 * len(names):]
        x, y, c, peers = _place()
        cps = []
        for i, entry in enumerate(big):
            for k, (px, py) in enumerate(peers):
                cp = _ici_copy(outs, entry, 2 * px + py, c, (x, y, 1 - c), fsend, frecv, 3 * i + k)
                cp.start()
                cps.append(cp)
        for i, entry in enumerate(big):
            for k, (px, py) in enumerate(peers):
                _ici_copy(outs, entry, 2 * px + py, 1 - c, (x, y, 1 - c), fsend, frecv, 3 * i + k).wait_recv()
        for cp in cps:
            cp.wait_send()

    res = pl.pallas_call(
        body, name="gather_d2d_" + tag,
        in_specs=[ANY] * len(names), out_specs=[ANY] * len(names),
        out_shape=[jax.ShapeDtypeStruct(GATHERED[k], BF16) for k in names],
        input_output_aliases={i: i for i in range(len(names))},
        scratch_shapes=[pltpu.SemaphoreType.DMA((3 * n,))] * 2,
    )(*[arrays[k] for k in names])
    return dict(zip(names, res))


def _win_pieces():
    pieces = []
    for g0, wd, i0 in SEGS.values():
        for j in range(N_CHIPS):
            lo, hi = max(g0, j * SHARD_IN), min(g0 + wd, (j + 1) * SHARD_IN)
            if lo < hi:
                pieces.append((j, lo - j * SHARD_IN, hi - j * SHARD_IN, i0 + lo - g0))
    return pieces


def _win_to_internal(win4):
    tr = 128

    def body(i_ref, o_ref):
        for j, s0, s1, d0 in _win_pieces():
            o_ref[:, d0:d0 + s1 - s0] = i_ref[j, :, s0:s1]
        o_ref[:, PROJ_W:] = jnp.zeros((tr, PROJ_PAD - PROJ_W), o_ref.dtype)

    return pl.pallas_call(
        body, name="win_to_internal", grid=(D // tr,),
        in_specs=[pl.BlockSpec((N_CHIPS, tr, SHARD_IN), lambda i: (0, i, 0))],
        out_specs=pl.BlockSpec((tr, PROJ_PAD), lambda i: (i, 0)),
        out_shape=jax.ShapeDtypeStruct((D, PROJ_PAD), win4.dtype),
        compiler_params=_params(("parallel",), 40 << 20),
    )(win4)


def _win_from_internal(g):
    tr = 64

    def body(i_ref, o_ref):
        for j, s0, s1, d0 in _win_pieces():
            o_ref[j, :, s0:s1] = i_ref[:, d0:d0 + s1 - s0]

    return pl.pallas_call(
        body, name="win_from_internal", grid=(D // tr,),
        in_specs=[pl.BlockSpec((tr, PROJ_PAD), lambda i: (i, 0))],
        out_specs=pl.BlockSpec((N_CHIPS, tr, SHARD_IN), lambda i: (0, i, 0)),
        out_shape=jax.ShapeDtypeStruct((N_CHIPS, D, SHARD_IN), g.dtype),
        compiler_params=_params(("parallel",), 40 << 20),
    )(g)


def _pair_copy(src, dst, c, to, send, recv, k):
    hr = src.shape[1] // 2
    return pltpu.make_async_remote_copy(
        src_ref=src.at[:, pl.ds(pl.multiple_of((1 - c) * hr, 16), hr), :], dst_ref=dst,
        send_sem=send.at[k], recv_sem=recv.at[k], device_id=to, device_id_type=MESH)


def _rs_pair_start(grads, carried, tag):
    n = len(grads)

    def body(*refs):
        send, recv = refs[2 * n + 1], refs[2 * n + 2]
        srcs, dsts = refs[2 * n + 3:3 * n + 3], refs[3 * n + 3:4 * n + 3]
        x, y, c, _ = _place()
        for i in range(n):
            _pair_copy(srcs[i], dsts[i], c, (x, y, 1 - c), send, recv, i).start()

    lands = [lax.empty((g.shape[0], g.shape[1] // 2, g.shape[2]), F32) for g in grads]
    res = pl.pallas_call(
        body, name="rs_pair_start_" + tag,
        in_specs=[HBM] * (2 * n + 1), out_specs=[SEM, SEM] + [HBM] * (2 * n + 1),
        out_shape=[pltpu.SemaphoreType.DMA((n,)), pltpu.SemaphoreType.DMA((n,))]
        + [pltpu.HBM(g.shape, F32) for g in grads] + [pltpu.HBM(l.shape, F32) for l in lands]
        + [pltpu.HBM(carried.shape, carried.dtype)],
        input_output_aliases={i: i + 2 for i in range(2 * n + 1)},
        compiler_params=pltpu.CompilerParams(has_side_effects=pltpu.SideEffectType.DATAFLOW_SIDE_EFFECTING),
    )(*[pltpu.with_memory_space_constraint(a, pltpu.HBM) for a in list(grads) + lands + [carried]])
    return (res[0], res[1], list(res[2:2 + n]), list(res[2 + n:2 + 2 * n])), res[-1]


def _rs_pair_wait(send, recv, grads, lands, after, tag):
    n = len(grads)

    def body(*refs):
        srcs, dsts = refs[:n], refs[n:2 * n]
        send_ref, recv_ref = refs[2 * n], refs[2 * n + 1]
        x, y, c, _ = _place()
        for i in range(n):
            cp = _pair_copy(srcs[i], dsts[i], c, (x, y, 1 - c), send_ref, recv_ref, i)
            cp.wait_send()
            cp.wait_recv()

    res = pl.pallas_call(
        body, name="rs_pair_wait_" + tag,
        in_specs=[HBM] * (2 * n) + [SEM, SEM, pl.BlockSpec(memory_space=pl.ANY)],
        out_specs=[HBM] * (2 * n),
        out_shape=[pltpu.HBM(g.shape, F32) for g in grads] + [pltpu.HBM(l.shape, F32) for l in lands],
        input_output_aliases={i: i for i in range(2 * n)},
        compiler_params=pltpu.CompilerParams(has_side_effects=pltpu.SideEffectType.DATAFLOW_SIDE_EFFECTING),
    )(*grads, *lands, send, recv, after)
    return list(res[:n]), list(res[n:])


def _half_tile(hr):
    return _divtile(hr, 256, 16) if hr % 256 == 0 else _divtile(hr, 512, 16)


def _rs_pair_sum(g, r, c_idx, name):
    ns, rows, cols = g.shape
    hr = rows // 2
    tr = _half_tile(hr)
    nt = hr // tr

    def body(c_ref, g_ref, r_ref, ob_ref, of_ref):
        s = g_ref[...] + r_ref[...]
        ob_ref[...] = s.astype(BF16)
        of_ref[...] = s

    blk = pl.BlockSpec((None, tr, cols), lambda j, t, c_ref: (j, t, 0))
    return pl.pallas_call(
        body, name=name,
        grid_spec=pltpu.PrefetchScalarGridSpec(
            num_scalar_prefetch=1, grid=(ns, nt),
            in_specs=[pl.BlockSpec((None, tr, cols), lambda j, t, c_ref: (j, c_ref[0] * nt + t, 0)), blk],
            out_specs=[blk, blk]),
        out_shape=[jax.ShapeDtypeStruct((ns, hr, cols), BF16), jax.ShapeDtypeStruct((ns, hr, cols), F32)],
        compiler_params=_params(("parallel", "parallel"), 48 << 20),
    )(c_idx, g, r)


def _rs_chip_start(parts, tag):
    n = len(parts)

    def body(*refs):
        send, recv = refs[2 * n], refs[2 * n + 1]
        srcs, dsts = refs[2 * n + 2:3 * n + 2], refs[3 * n + 2:4 * n + 2]
        token = refs[-1]
        x, y, c, peers = _place()
        for i in range(n):
            for k, (px, py) in enumerate(peers):
                pltpu.make_async_remote_copy(
                    src_ref=srcs[i].at[2 * px + py], dst_ref=dsts[i].at[k],
                    send_sem=send.at[3 * i + k], recv_sem=recv.at[3 * i + k],
                    device_id=(px, py, c), device_id_type=MESH).start()
        token[...] = jnp.zeros_like(token)

    lands = [lax.empty((3,) + p.shape[1:], BF16) for p in parts]
    res = pl.pallas_call(
        body, name="rs_chip_start_" + tag,
        in_specs=[HBM] * (2 * n),
        out_specs=[SEM, SEM] + [HBM] * (2 * n) + [pl.BlockSpec(memory_space=pltpu.VMEM)],
        out_shape=[pltpu.SemaphoreType.DMA((3 * n,)), pltpu.SemaphoreType.DMA((3 * n,))]
        + [pltpu.HBM(p.shape, BF16) for p in parts] + [pltpu.HBM(l.shape, BF16) for l in lands]
        + [jax.ShapeDtypeStruct((8, 128), F32)],
        input_output_aliases={i: i + 2 for i in range(2 * n)},
        compiler_params=pltpu.CompilerParams(has_side_effects=pltpu.SideEffectType.DATAFLOW_SIDE_EFFECTING),
    )(*[pltpu.with_memory_space_constraint(a, pltpu.HBM) for a in list(parts) + lands])
    return res[0], res[1], list(res[2:2 + n]), list(res[2 + n:2 + 2 * n]), res[-1]


def _rs_chip_wait(send, recv, parts, lands, after, tag):
    n = len(parts)

    def body(*refs):
        srcs, dsts = refs[:n], refs[n:2 * n]
        send_ref, recv_ref = refs[2 * n], refs[2 * n + 1]
        x, y, c, peers = _place()
        for i in range(n):
            for k, (px, py) in enumerate(peers):
                cp = pltpu.make_async_remote_copy(
                    src_ref=srcs[i].at[2 * px + py], dst_ref=dsts[i].at[k],
                    send_sem=send_ref.at[3 * i + k], recv_sem=recv_ref.at[3 * i + k],
                    device_id=(px, py, c), device_id_type=MESH)
                cp.wait_send()
                cp.wait_recv()

    res = pl.pallas_call(
        body, name="rs_chip_wait_" + tag,
        in_specs=[HBM] * (2 * n) + [SEM, SEM, pl.BlockSpec(memory_space=pl.ANY)],
        out_specs=[HBM] * (2 * n),
        out_shape=[pltpu.HBM(p.shape, BF16) for p in parts] + [pltpu.HBM(l.shape, BF16) for l in lands],
        input_output_aliases={i: i for i in range(2 * n)},
        compiler_params=pltpu.CompilerParams(has_side_effects=pltpu.SideEffectType.DATAFLOW_SIDE_EFFECTING),
    )(*parts, *lands, send, recv, after)
    return list(res[n:])


def _rs_final_sum(own, got, chip_idx, c_idx, name):
    ns, hr, cols = own.shape
    tr = _half_tile(hr)
    nt = hr // tr

    def body(chip_ref, c_ref, o_ref, g_ref, out_ref):
        s = o_ref[...]
        for k in range(3):
            s = s + g_ref[k].astype(F32)
        out_ref[...] = s

    return pl.pallas_call(
        body, name=name,
        grid_spec=pltpu.PrefetchScalarGridSpec(
            num_scalar_prefetch=2, grid=(nt,),
            in_specs=[pl.BlockSpec((None, tr, cols), lambda t, chip_ref, c_ref: (chip_ref[0], t, 0)),
                      pl.BlockSpec((3, tr, cols), lambda t, chip_ref, c_ref: (0, t, 0))],
            out_specs=pl.BlockSpec((tr, cols), lambda t, chip_ref, c_ref: (c_ref[0] * nt + t, 0))),
        out_shape=jax.ShapeDtypeStruct((2 * hr, cols), F32),
        compiler_params=_params(("parallel",), 48 << 20),
    )(chip_idx, c_idx, own, got)


def _rs_share_halves(fulls, tag):
    n = len(fulls)

    def body(*refs):
        dsts = refs[n:2 * n]
        send, recv = refs[2 * n:]
        x, y, c, _ = _place()
        cps = []
        for i in range(n):
            hr = dsts[i].shape[0] // 2
            rows = dsts[i].at[pl.ds(pl.multiple_of(c * hr, 8), hr), :]
            cp = pltpu.make_async_remote_copy(src_ref=rows, dst_ref=rows, send_sem=send.at[i], recv_sem=recv.at[i],
                                              device_id=(x, y, 1 - c), device_id_type=MESH)
            cp.start()
            cps.append(cp)
        for i in range(n):
            hr = dsts[i].shape[0] // 2
            other = dsts[i].at[pl.ds(pl.multiple_of((1 - c) * hr, 8), hr), :]
            pltpu.make_async_remote_copy(src_ref=other, dst_ref=other, send_sem=send.at[i], recv_sem=recv.at[i],
                                         device_id=(x, y, 1 - c), device_id_type=MESH).wait_recv()
        for cp in cps:
            cp.wait_send()

    return pl.pallas_call(
        body, name="rs_share_halves_" + tag, in_specs=[ANY] * n, out_specs=[ANY] * n,
        out_shape=[jax.ShapeDtypeStruct(f.shape, F32) for f in fulls],
        input_output_aliases={i: i for i in range(n)},
        scratch_shapes=[pltpu.SemaphoreType.DMA((n,))] * 2,
    )(*fulls)


def _all_reduce_small(v):
    rows = v.shape[0]

    def body(v_ref, o_ref, buf, send, recv):
        x, y, c, _ = _place()
        me = 4 * x + 2 * y + c
        buf[me] = v_ref[...]
        cps = []
        for d in range(1, 8):
            px, py, pc = x ^ (d >> 2), y ^ ((d >> 1) & 1), c ^ (d & 1)
            cp = pltpu.make_async_remote_copy(src_ref=v_ref, dst_ref=buf.at[me], send_sem=send.at[d - 1],
                                              recv_sem=recv.at[d - 1], device_id=(px, py, pc), device_id_type=MESH)
            cp.start()
            cps.append(cp)
        for d in range(1, 8):
            px, py, pc = x ^ (d >> 2), y ^ ((d >> 1) & 1), c ^ (d & 1)
            pltpu.make_async_remote_copy(src_ref=v_ref, dst_ref=buf.at[4 * px + 2 * py + pc], send_sem=send.at[d - 1],
                                         recv_sem=recv.at[d - 1], device_id=(px, py, pc),
                                         device_id_type=MESH).wait_recv()
        for cp in cps:
            cp.wait_send()
        acc = buf[0]
        for d in range(1, 8):
            acc = acc + buf[d]
        o_ref[...] = acc

    vm = pl.BlockSpec(memory_space=pltpu.VMEM)
    return pl.pallas_call(
        body, name="all_reduce_small", in_specs=[vm], out_specs=vm,
        out_shape=jax.ShapeDtypeStruct((rows, 128), F32),
        scratch_shapes=[pltpu.VMEM((8, rows, 128), F32), pltpu.SemaphoreType.DMA((7,)), pltpu.SemaphoreType.DMA((7,))],
    )(v)


def _adamw(w, g, m, v, name, g_col_blk=0):
    rows, cols = w.shape
    tr = _divtile(rows, max(8, (2 << 20) // (4 * cols) // 8 * 8), 8)

    def body(w_ref, g_ref, m_ref, v_ref, go_ref, d_ref, mo_ref, vo_ref):
        gv = g_ref[...]
        mn = ADAM_B1 * m_ref[...] + (1.0 - ADAM_B1) * gv
        vn = ADAM_B2 * v_ref[...] + (1.0 - ADAM_B2) * (gv * gv)
        m_hat = mn / (1.0 - ADAM_B1 ** ADAM_STEP)
        v_hat = vn / (1.0 - ADAM_B2 ** ADAM_STEP)
        go_ref[...] = gv
        d_ref[...] = -ADAM_LR * (m_hat / (jnp.sqrt(v_hat) + ADAM_EPS) + ADAM_WD * w_ref[...])
        mo_ref[...] = mn
        vo_ref[...] = vn

    blk = pl.BlockSpec((tr, cols), lambda i: (i, 0))
    return pl.pallas_call(
        body, name=name, grid=(rows // tr,),
        in_specs=[blk, pl.BlockSpec((tr, cols), lambda i: (i, g_col_blk)), blk, blk],
        out_specs=[blk] * 4, out_shape=[jax.ShapeDtypeStruct((rows, cols), F32)] * 4,
        compiler_params=_params(("parallel",), 48 << 20),
    )(w, g, m, v)


SMALL = ["ln1_g", "ln1_b", "conv_w", "conv_b", "dt_bias", "a_log", "d_skip", "ssd_norm_g", "attn_sinks",
         "ln2_g", "ln2_b", "ln3_g", "ln3_b"]


def _pack_rows(vs):
    parts = []
    for v in vs:
        v = v.reshape(-1)
        parts.append(jnp.pad(v, (0, (-v.shape[0]) % 128)))
    flat = jnp.concatenate(parts)
    flat = jnp.pad(flat, (0, (-flat.shape[0]) % 1024))
    return flat.reshape(-1, 128)


def _unpack_rows(packed, shapes):
    flat = packed.reshape(-1)
    out, at = [], 0
    for s in shapes:
        nel = int(np.prod(s))
        out.append(flat[at:at + nel].reshape(s))
        at += nel + (-nel) % 128
    return out


def kernel(x, positions, ffn1_w_gate, ffn1_w_up, ffn1_w_down, ln1_g, ln1_b, w_in, conv_w, conv_b, dt_bias, a_log, d_skip, ssd_norm_g, w_ssd_o, attn_sinks, w_attn_o, w_out, ln2_g, ln2_b, ffn2_w_gate, ffn2_w_up, ffn2_w_down, ln3_g, ln3_b, loss_target, m_ffn1_w_gate, m_ffn1_w_up, m_ffn1_w_down, m_ln1_g, m_ln1_b, m_w_in, m_conv_w, m_conv_b, m_dt_bias, m_a_log, m_d_skip, m_ssd_norm_g, m_w_ssd_o, m_attn_sinks, m_w_attn_o, m_w_out, m_ln2_g, m_ln2_b, m_ffn2_w_gate, m_ffn2_w_up, m_ffn2_w_down, m_ln3_g, m_ln3_b, v_ffn1_w_gate, v_ffn1_w_up, v_ffn1_w_down, v_ln1_g, v_ln1_b, v_w_in, v_conv_w, v_conv_b, v_dt_bias, v_a_log, v_d_skip, v_ssd_norm_g, v_w_ssd_o, v_attn_sinks, v_w_attn_o, v_w_out, v_ln2_g, v_ln2_b, v_ffn2_w_gate, v_ffn2_w_up, v_ffn2_w_down, v_ln3_g, v_ln3_b):
    args = dict(locals())
    wts = {n: args[n][0] for n in [b[0] for b in BIG] + SMALL}
    mom_m = {n: args["m_" + n][0] for n in wts}
    mom_v = {n: args["v_" + n][0] for n in wts}
    t = x.shape[1]
    xi, yi, ci = lax.axis_index("x"), lax.axis_index("y"), lax.axis_index("c")
    chip = 2 * xi + yi

    c_idx = ci.astype(jnp.int32).reshape(1)
    chip_idx = chip.astype(jnp.int32).reshape(1)
    placed = {o: _cast_place([wts[b[0]] for b in BIG if b[3] == o], o, chip_idx, also_alone=(o == "gu1"))
              for o in GATHERED}
    placed["gu1"], gu1_own = placed["gu1"]
    g_sems, g_flight = {}, {}

    def fetch(group, after):
        names = GATHER_GROUPS[group]
        send, recv = g_sems[group]
        landed = _gather_ici_wait(send, recv, {k: g_flight[k] for k in names}, names, after, str(group))
        got = _gather_d2d(landed, names, str(group))
        if "win4" in got:
            got["win"] = _win_to_internal(got.pop("win4"))
        return got

    sems, arrays, gu1_own = _gather_ici_start(placed, GATHER_GROUPS, "all", gu1_own)
    g_sems.update(dict(enumerate(sems)))
    g_flight.update(arrays)
    w = {"gu1_own": gu1_own, "chip_idx": chip_idx}

    def slabs_of(gw, names):
        view = {"gu1": lambda: gw["gu1"], "gu2": lambda: gw["gu2"],
                "d1": lambda: gw["d1"].reshape(N_CHIPS, SHARD_H, D), "d2": lambda: gw["d2"].reshape(N_CHIPS, SHARD_H, D),
                "win": lambda: _win_from_internal(gw["win"]),
                "so": lambda: gw["so"].reshape(N_CHIPS, SSD_INNER // N_CHIPS, D),
                "ao": lambda: gw["ao"].reshape(N_CHIPS, D // N_CHIPS, D),
                "out": lambda: gw["out"].reshape(N_CHIPS, D // N_CHIPS, D)}
        return [view[nm]() for nm in names]

    early = ["win", "so", "ao", "out", "gu2", "d2"]
    late = ["gu1", "d1"]
    flight = {}

    def early_start(gw, win):
        flight["pair"], win = _rs_pair_start(slabs_of(gw, early), win, "early")
        return win

    def late_start(gw, gu1w):
        flight["pair_late"], gu1w = _rs_pair_start(slabs_of(gw, late), gu1w, "late")
        return gu1w

    def early_mid(dh1):
        slabs, from_sib = _rs_pair_wait(*flight["pair"], dh1, "early")
        pair = [_rs_pair_sum(g, r, c_idx, "rs_pair_sum_" + nm) for g, r, nm in zip(slabs, from_sib, early)]
        send, recv, parts, lands, token = _rs_chip_start([p[0] for p in pair], "early")
        flight.update(send=send, recv=recv, parts=parts, lands=lands, own=[p[1] for p in pair])
        return token

    early_grads = (early_start, early_mid, late_start)
    cw_rows = _pack_rows([lax.dynamic_update_slice(jnp.zeros((4, XBC), F32), wts["conv_w"], (0, chip * (XBC // N_CHIPS)))])
    cw_rows = jnp.where(ci == 0, cw_rows, 0.0)
    conv_w_full = _all_reduce_small(cw_rows)[:4 * XBC // 128].reshape(4, XBC)

    small = {n: (wts[n][None, :] if wts[n].ndim == 1 else wts[n]) for n in SMALL}
    small["conv_w"] = conv_w_full
    loss, grad_x, gw, gs = _local_step(x[0], positions[0].astype(F32)[:, None], loss_target[0], w, small,
                                       fetch=fetch, early_grads=early_grads)

    gvec = {n: gs[n] for n in SMALL}
    gvec["dt_bias"], gvec["a_log"], gvec["d_skip"] = gs["dt_bias"][:, :64], gs["a_log"][:, :64], gs["d_skip"][:, :64]
    gvec["attn_sinks"] = gs["attn_sinks"][:, :NQ]
    red = _all_reduce_small(_pack_rows([gvec[n] for n in SMALL] + [loss, grad_x[:1, :128]]))
    slabs, from_sib = _rs_pair_wait(*flight["pair_late"], red, "late")
    pair = [_rs_pair_sum(g, r, c_idx, "rs_pair_sum_" + nm) for g, r, nm in zip(slabs, from_sib, late)]
    l_send, l_recv, l_parts, l_lands, l_token = _rs_chip_start([p[0] for p in pair], "late")
    got_early = _rs_chip_wait(flight["send"], flight["recv"], flight["parts"], flight["lands"], l_token, "early")

    outs = {}
    big_src = {"ffn1_w_gate": ("gu1", 0), "ffn1_w_up": ("gu1", 1), "ffn1_w_down": ("d1", 0), "w_in": ("win", 0),
               "w_ssd_o": ("so", 0), "w_attn_o": ("ao", 0), "w_out": ("out", 0),
               "ffn2_w_gate": ("gu2", 0), "ffn2_w_up": ("gu2", 1), "ffn2_w_down": ("d2", 0)}

    def finish(names, own, got, tag):
        halves = [_rs_final_sum(o, gt, chip_idx, c_idx, "rs_final_sum_" + nm) for o, gt, nm in zip(own, got, names)]
        full = dict(zip(names, _rs_share_halves(halves, tag)))
        for nm, (src, blk) in big_src.items():
            if src in full:
                outs[nm] = _adamw(wts[nm], full[src], mom_m[nm], mom_v[nm], "adamw_" + nm, g_col_blk=blk)

    finish(early, flight["own"], got_early, "early")
    early_done = sum(o[1][:1, :1] for o in outs.values())
    got_late = _rs_chip_wait(l_send, l_recv, l_parts, l_lands, early_done, "late")
    finish(late, [p[1] for p in pair], got_late, "late")

    shapes = [(4, XBC) if n == "conv_w" else wts[n].shape for n in SMALL] + [(1,)]
    red_list = _unpack_rows(red, shapes)
    loss_out = red_list[-1].reshape(())
    gsm = dict(zip(SMALL, red_list[:-1]))
    gsm["conv_w"] = lax.dynamic_slice_in_dim(gsm["conv_w"], chip * (XBC // N_CHIPS), XBC // N_CHIPS, axis=1)
    sm_shapes = [wts[n].shape for n in SMALL]
    res = _adamw(_pack_rows([wts[n] for n in SMALL]), _pack_rows([gsm[n] for n in SMALL]),
                 _pack_rows([mom_m[n] for n in SMALL]), _pack_rows([mom_v[n] for n in SMALL]), "adamw_small")
    res = [_unpack_rows(r, sm_shapes) for r in res]
    for i, nm in enumerate(SMALL):
        outs[nm] = tuple(r[i] for r in res)

    order = ["ffn1_w_gate", "ffn1_w_up", "ffn1_w_down", "ln1_g", "ln1_b", "w_in", "conv_w", "conv_b", "dt_bias", "a_log",
             "d_skip", "ssd_norm_g", "w_ssd_o", "attn_sinks", "w_attn_o", "w_out", "ln2_g", "ln2_b",
             "ffn2_w_gate", "ffn2_w_up", "ffn2_w_down", "ln3_g", "ln3_b"]
    result = [loss_out, grad_x[None]]
    for kind in range(4):
        result += [outs[nm][kind][None] for nm in order]
    return tuple(result)
```

```python
import functools
import math

import numpy as np
import jax
import jax.numpy as jnp
from jax import lax
from jax.experimental import pallas as pl
from jax.experimental.pallas import tpu as pltpu

F32 = jnp.float32
BF16 = jnp.bfloat16
HI = lax.Precision.HIGHEST

D = 2048
FFN_H = 5632
SSD_INNER = 4096
SSD_HEADS = 64
SSD_P = 64
SSD_G = 8
SSD_R = 8
SSD_N = 128
CHUNK = 128
XBC = 6144
NQ = 32
NKV = 4
HD = 64
QW = 2048
KVW = 256
WINDOW = 128
ROPE_THETA = 10000.0
ALPHA = 2.0 ** 0.25
LN_EPS = 1e-5
RMS_EPS = 1e-5
PROJ_W = 16960
N_CHIPS = 4
SHARD_IN = PROJ_W // N_CHIPS
SHARD_H = FFN_H // N_CHIPS

SEGS = {
    "z": (0, 4096, 0),
    "xbc": (4096, 6144, 10240),
    "dt": (10240, 64, 16896),
    "q": (10304, 2048, 8192),
    "k": (12352, 256, 16384),
    "v": (12608, 256, 16640),
    "gs": (12864, 2048, 4096),
    "ga": (14912, 2048, 6144),
}
PROJ_PAD = 17024

ADAM_LR = 0.001
ADAM_B1 = 0.9
ADAM_B2 = 0.999
ADAM_EPS = 1e-08
ADAM_WD = 0.01
ADAM_STEP = 10

VMEM_CAP = 60 * 1024 * 1024


def _params(sem, vmem_bytes):
    return pltpu.CompilerParams(dimension_semantics=sem, vmem_limit_bytes=int(min(VMEM_CAP, vmem_bytes)))


def _divtile(n, cap, q=128):
    best = None
    for d in range(q, min(n, cap) + 1, q):
        if n % d == 0:
            best = d
    return n if best is None else best


def _sigmoid(x):
    return 0.5 * jnp.tanh(0.5 * x) + 0.5


def _mm(a, b, mode, out_dtype, name, add=None, add_scale=1.0, caps=(1024, 1024, 2048), n_slabs=1):
    if mode == "nn":
        (m, k), (k2, n) = a.shape, b.shape
    elif mode == "nt":
        (m, k), (n, k2) = a.shape, b.shape
    else:
        (k, m), (k2, n) = a.shape, b.shape
    assert k == k2, (a.shape, b.shape, mode)
    tm, tn, tk = _divtile(m, caps[0]), _divtile(n // n_slabs, caps[1]), _divtile(k, caps[2])
    nk = k // tk
    per_slab = n // n_slabs // tn
    dims = {"nn": ((1,), (0,)), "nt": ((1,), (1,)), "tn": ((0,), (0,))}[mode]
    has_add = add is not None

    def body(*refs):
        if has_add:
            a_ref, b_ref, add_ref, o_ref = refs[:4]
            scr = refs[4:]
        else:
            a_ref, b_ref, o_ref = refs[:3]
            add_ref = None
            scr = refs[3:]
        part = lax.dot_general(a_ref[...].astype(BF16), b_ref[...].astype(BF16), (dims, ((), ())),
                               preferred_element_type=F32)

        def finish(acc):
            if has_add:
                acc = acc + add_scale * add_ref[...].astype(F32)
            o_ref[...] = acc.astype(o_ref.dtype)

        if nk == 1:
            finish(part)
        else:
            acc_ref = scr[0]
            kk = pl.program_id(2)

            @pl.when(kk == 0)
            def _():
                acc_ref[...] = part

            @pl.when(kk > 0)
            def _():
                acc_ref[...] += part

            @pl.when(kk == nk - 1)
            def _():
                finish(acc_ref[...])

    if mode == "nn":
        a_spec = pl.BlockSpec((tm, tk), lambda i, j, kk: (i, kk))
        b_spec = pl.BlockSpec((tk, tn), lambda i, j, kk: (kk, j))
    elif mode == "nt":
        a_spec = pl.BlockSpec((tm, tk), lambda i, j, kk: (i, kk))
        b_spec = pl.BlockSpec((tn, tk), lambda i, j, kk: (j, kk))
    else:
        a_spec = pl.BlockSpec((tk, tm), lambda i, j, kk: (kk, i))
        b_spec = pl.BlockSpec((tk, tn), lambda i, j, kk: (kk, j))
    o_spec = pl.BlockSpec((tm, tn), lambda i, j, kk: (i, j))
    out_shape = jax.ShapeDtypeStruct((m, n), out_dtype)
    if n_slabs > 1:
        assert not has_add
        o_spec = pl.BlockSpec((None, tm, tn), lambda i, j, kk: (j // per_slab, i, j % per_slab))
        out_shape = jax.ShapeDtypeStruct((n_slabs, m, n // n_slabs), out_dtype)
    in_specs = [a_spec, b_spec] + ([o_spec] if has_add else [])
    args = (a, b) + ((add,) if has_add else ())
    osz = jnp.dtype(out_dtype).itemsize
    vmem = (2 * (tm * tk * a.dtype.itemsize + tk * tn * b.dtype.itemsize) + 2 * tm * tn * osz
            + (2 * tm * tn * add.dtype.itemsize if has_add else 0) + 2 * tm * tn * 4
            + 2 * (tm * tk + tk * tn) + (8 << 20))
    return pl.pallas_call(
        body, name=name, grid=(m // tm, n // tn, nk),
        in_specs=in_specs, out_specs=o_spec, out_shape=out_shape,
        scratch_shapes=[pltpu.VMEM((tm, tn), F32)] if nk > 1 else [],
        compiler_params=_params(("parallel", "parallel", "arbitrary"), vmem),
    )(*args)


def _mm_swiglu(a, b, name, chip_idx=None, done=None):
    m, k = a.shape
    w = SHARD_H
    tm = _divtile(m, 512)

    def body(*refs):
        a_ref, b_ref = refs[-4 if done is None else -6], refs[-3 if done is None else -5]
        gu_ref, act_ref = refs[-2:]
        gu = jnp.dot(a_ref[...], b_ref[...], preferred_element_type=F32)
        g = gu[:, :w]
        gu_ref[...] = gu.astype(BF16)
        act_ref[...] = (g * _sigmoid(g) * gu[:, w:]).astype(BF16)

    out_shape = [jax.ShapeDtypeStruct((m, 2 * FFN_H), BF16), jax.ShapeDtypeStruct((m, FFN_H), BF16)]
    cp = _params(("parallel", "parallel"), 56 << 20)
    if chip_idx is None:
        return pl.pallas_call(
            body, name=name, grid=(N_CHIPS, m // tm),
            in_specs=[pl.BlockSpec((tm, k), lambda j, i: (i, 0)), pl.BlockSpec((k, 2 * w), lambda j, i: (0, j))],
            out_specs=[pl.BlockSpec((tm, 2 * w), lambda j, i: (i, j)), pl.BlockSpec((tm, w), lambda j, i: (i, j))],
            out_shape=out_shape, compiler_params=cp,
        )(a, b)
    first = done is None
    tile = (lambda j, c: c[0]) if first else (lambda j, c: (c[0] + 1 + j) % N_CHIPS)
    in_specs = [pl.BlockSpec((tm, k), lambda j, i, c: (i, 0)),
                pl.BlockSpec((k, 2 * w), (lambda j, i, c: (0, 0)) if first else (lambda j, i, c: (0, tile(j, c))))]
    return pl.pallas_call(
        body, name=name,
        grid_spec=pltpu.PrefetchScalarGridSpec(
            num_scalar_prefetch=1, grid=(1 if first else N_CHIPS - 1, m // tm),
            in_specs=in_specs + ([] if first else [pl.BlockSpec(memory_space=pl.ANY)] * 2),
            out_specs=[pl.BlockSpec((tm, 2 * w), lambda j, i, c: (i, tile(j, c))),
                       pl.BlockSpec((tm, w), lambda j, i, c: (i, tile(j, c)))]),
        out_shape=out_shape, compiler_params=cp,
        input_output_aliases={} if first else {3: 0, 4: 1},
    )(chip_idx, a, b, *(() if first else done))


def _mm_swiglu_bwd(dr, wd, gu, name):
    m, k = dr.shape
    w = SHARD_H
    tm = _divtile(m, 512)

    def body(dr_ref, wd_ref, gu_ref, o_ref):
        d = lax.dot_general(dr_ref[...], wd_ref[...], NT_DIMS, preferred_element_type=F32)
        g = gu_ref[:, :w].astype(F32)
        u = gu_ref[:, w:].astype(F32)
        s = _sigmoid(g)
        o_ref[:, :w] = (d * u * (s * (1.0 + g * (1.0 - s)))).astype(BF16)
        o_ref[:, w:] = (d * (g * s)).astype(BF16)

    return pl.pallas_call(
        body, name=name, grid=(N_CHIPS, m // tm),
        in_specs=[pl.BlockSpec((tm, k), lambda j, i: (i, 0)), pl.BlockSpec((w, k), lambda j, i: (j, 0)),
                  pl.BlockSpec((tm, 2 * w), lambda j, i: (i, j))],
        out_specs=pl.BlockSpec((tm, 2 * w), lambda j, i: (i, j)),
        out_shape=jax.ShapeDtypeStruct((m, 2 * FFN_H), BF16),
        compiler_params=_params(("parallel", "parallel"), 48 << 20),
    )(dr, wd, gu)


def _ln_fwd(base, f, g, b, c, name, target=None):
    t = base.shape[0]
    tt = _divtile(t, 256)
    with_loss = target is not None

    def body(*refs):
        if with_loss:
            base_ref, f_ref, g_ref, b_ref, tg_ref, h_ref, hb_ref, xh_ref, rs_ref, dh_ref, loss_ref = refs
        else:
            base_ref, f_ref, g_ref, b_ref, h_ref, hb_ref, xh_ref, rs_ref = refs
        r = ALPHA * base_ref[...] + c * f_ref[...]
        mu = jnp.mean(r, axis=-1, keepdims=True)
        xc = r - mu
        var = jnp.mean(xc * xc, axis=-1, keepdims=True)
        rstd = lax.rsqrt(var + LN_EPS)
        xh = xc * rstd
        h = xh * g_ref[...] + b_ref[...]
        h_ref[...] = h
        hb_ref[...] = h.astype(BF16)
        xh_ref[...] = xh
        rs_ref[...] = rstd
        if with_loss:
            e = h - tg_ref[...]
            dh_ref[...] = e * (1.0 / D)
            part = 0.5 * jnp.sum(jnp.sum(e * e, axis=-1, keepdims=True) * (1.0 / D), axis=0, keepdims=True)

            @pl.when(pl.program_id(0) == 0)
            def _():
                loss_ref[...] = jnp.zeros_like(loss_ref)

            loss_ref[...] += part

    row = pl.BlockSpec((tt, D), lambda i: (i, 0))
    vec = pl.BlockSpec((1, D), lambda i: (0, 0))
    col = pl.BlockSpec((tt, 1), lambda i: (i, 0))
    in_specs = [row, row, vec, vec] + ([row] if with_loss else [])
    out_specs = [row, row, row, col] + ([row, pl.BlockSpec((1, 1), lambda i: (0, 0))] if with_loss else [])
    out_shape = [jax.ShapeDtypeStruct((t, D), F32), jax.ShapeDtypeStruct((t, D), BF16),
                 jax.ShapeDtypeStruct((t, D), F32), jax.ShapeDtypeStruct((t, 1), F32)]
    if with_loss:
        out_shape += [jax.ShapeDtypeStruct((t, D), F32), jax.ShapeDtypeStruct((1, 1), F32)]
    args = (base, f, g, b) + ((target,) if with_loss else ())
    return pl.pallas_call(
        body, name=name, grid=(t // tt,), in_specs=in_specs, out_specs=out_specs, out_shape=out_shape,
        compiler_params=_params(("arbitrary",) if with_loss else ("parallel",), 48 << 20),
    )(*args)


def _ln_bwd(dy, xh, rstd, g, c, name):
    t = dy.shape[0]
    tt = _divtile(t, 256)

    def body(dy_ref, xh_ref, rs_ref, g_ref, dr_ref, drb_ref, dg_ref, db_ref):
        dyv = dy_ref[...]
        xhv = xh_ref[...]
        dxh = dyv * g_ref[...]
        m1 = jnp.mean(dxh, axis=-1, keepdims=True)
        m2 = jnp.mean(dxh * xhv, axis=-1, keepdims=True)
        dr = rs_ref[...] * (dxh - m1 - xhv * m2)
        dr_ref[...] = dr
        drb_ref[...] = (c * dr).astype(BF16)

        @pl.when(pl.program_id(0) == 0)
        def _():
            dg_ref[...] = jnp.zeros_like(dg_ref)
            db_ref[...] = jnp.zeros_like(db_ref)

        dg_ref[...] += jnp.sum(dyv * xhv, axis=0, keepdims=True)
        db_ref[...] += jnp.sum(dyv, axis=0, keepdims=True)

    row = pl.BlockSpec((tt, D), lambda i: (i, 0))
    vec = pl.BlockSpec((1, D), lambda i: (0, 0))
    col = pl.BlockSpec((tt, 1), lambda i: (i, 0))
    return pl.pallas_call(
        body, name=name, grid=(t // tt,), in_specs=[row, row, col, vec], out_specs=[row, row, vec, vec],
        out_shape=[jax.ShapeDtypeStruct((t, D), F32), jax.ShapeDtypeStruct((t, D), BF16),
                   jax.ShapeDtypeStruct((1, D), F32), jax.ShapeDtypeStruct((1, D), F32)],
        compiler_params=_params(("arbitrary",), 40 << 20),
    )(dy, xh, rstd, g)


DT_BLK = SEGS["dt"][2] // 128


def _dt_prep(proj, bias128, alog128):
    t = proj.shape[0]
    tt = _divtile(t, 256)

    def body(p_ref, bias_ref, alog_ref, dt_ref, adt_ref):
        dtv = jax.nn.softplus(p_ref[...] + bias_ref[...])
        adt = dtv * (-jnp.exp(alog_ref[...]))
        for g in range(SSD_G):
            dt_ref[g] = dtv[:, SSD_R * g:SSD_R * (g + 1)]
            adt_ref[g] = adt[:, SSD_R * g:SSD_R * (g + 1)]

    blk3 = pl.BlockSpec((SSD_G, tt, SSD_R), lambda i: (0, i, 0))
    vec = pl.BlockSpec((1, 128), lambda i: (0, 0))
    return pl.pallas_call(
        body, name="dt_prep", grid=(t // tt,),
        in_specs=[pl.BlockSpec((tt, 128), lambda i: (i, DT_BLK)), vec, vec], out_specs=[blk3, blk3],
        out_shape=[jax.ShapeDtypeStruct((SSD_G, t, SSD_R), F32)] * 2,
        compiler_params=_params(("parallel",), 16 << 20),
    )(proj, bias128, alog128)


def _dt_bwd(dadt3, dxdx3, proj, bias128, alog128):
    t = proj.shape[0]
    tt = _divtile(t, 256)

    def token_major(ref):
        parts = [ref[g] for g in range(SSD_G)] + [jnp.zeros((tt, 128 - SSD_HEADS), F32)]
        return jnp.concatenate(parts, axis=1)

    def body(dadt_ref, dxdx_ref, p_ref, bias_ref, alog_ref, o_ref, dbias_ref, dalog_ref):
        pre = p_ref[...] + bias_ref[...]
        dtv = jax.nn.softplus(pre)
        a = -jnp.exp(alog_ref[...])
        dadt = token_major(dadt_ref)
        ddt = a * dadt + token_major(dxdx_ref)
        draw = ddt * _sigmoid(pre)
        o_ref[...] = draw.astype(BF16)

        @pl.when(pl.program_id(0) == 0)
        def _():
            dbias_ref[...] = jnp.zeros_like(dbias_ref)
            dalog_ref[...] = jnp.zeros_like(dalog_ref)

        dbias_ref[...] += jnp.sum(draw, axis=0, keepdims=True)
        dalog_ref[...] += jnp.sum(dadt * dtv * a, axis=0, keepdims=True)

    blk = pl.BlockSpec((tt, 128), lambda i: (i, 0))
    blk3 = pl.BlockSpec((SSD_G, tt, SSD_R), lambda i: (0, i, 0))
    vec = pl.BlockSpec((1, 128), lambda i: (0, 0))
    return pl.pallas_call(
        body, name="dt_bwd", grid=(t // tt,),
        in_specs=[blk3, blk3, pl.BlockSpec((tt, 128), lambda i: (i, DT_BLK)), vec, vec],
        out_specs=[blk, vec, vec],
        out_shape=[jax.ShapeDtypeStruct((t, 128), BF16), jax.ShapeDtypeStruct((1, 128), F32),
                   jax.ShapeDtypeStruct((1, 128), F32)],
        compiler_params=_params(("arbitrary",), 16 << 20),
    )(dadt3, dxdx3, proj, bias128, alog128)


CONV_CB = 128
CONV_TT = 2048


def _shift_down(cur, prev8, s):
    if s == 0:
        return cur
    rolled = pltpu.roll(cur, s, 0)
    head = pltpu.roll(prev8, s, 0)
    r8 = lax.broadcasted_iota(jnp.int32, (8, 1), 0)
    top = jnp.where(r8 < s, head, rolled[:8])
    return jnp.concatenate([top, rolled[8:]], axis=0)


def _shift_up(cur, next8, s):
    if s == 0:
        return cur
    n = cur.shape[0]
    rolled = pltpu.roll(cur, n - s, 0)
    tail = pltpu.roll(next8, 8 - s, 0)
    r8 = lax.broadcasted_iota(jnp.int32, (8, 1), 0)
    bot = jnp.where(r8 >= 8 - s, tail, rolled[n - 8:])
    return jnp.concatenate([rolled[:n - 8], bot], axis=0)


def _conv_fwd(proj, conv_w, conv_b):
    t = proj.shape[0]
    tt = _divtile(t, CONV_TT)
    base = SEGS["xbc"][2] // CONV_CB
    r8 = tt // 8

    def body(u_ref, up_ref, w_ref, b_ref, o_ref):
        cur = u_ref[...]
        prev8 = jnp.where(pl.program_id(1) > 0, up_ref[...], 0.0)
        acc = b_ref[...] + w_ref[3:4, :] * cur
        for k in range(3):
            acc = acc + w_ref[k:k + 1, :] * _shift_down(cur, prev8, 3 - k)
        o_ref[...] = acc * _sigmoid(acc)

    return pl.pallas_call(
        body, name="conv_fwd", grid=(XBC // CONV_CB, t // tt),
        in_specs=[pl.BlockSpec((tt, CONV_CB), lambda c, i: (i, base + c)),
                  pl.BlockSpec((8, CONV_CB), lambda c, i: (jnp.maximum(i * r8 - 1, 0), base + c)),
                  pl.BlockSpec((4, CONV_CB), lambda c, i: (0, c)),
                  pl.BlockSpec((1, CONV_CB), lambda c, i: (0, c))],
        out_specs=pl.BlockSpec((tt, CONV_CB), lambda c, i: (i, c)),
        out_shape=jax.ShapeDtypeStruct((t, XBC), F32),
        compiler_params=_params(("parallel", "parallel"), 24 << 20),
    )(proj, proj, conv_w, conv_b)


def _conv_bwd(proj, dout, conv_w, conv_b, col0, width, name, dproj, skip=None):
    t = proj.shape[0]
    tt = _divtile(t, CONV_TT)
    nt = t // tt
    base = SEGS["xbc"][2] // CONV_CB + col0 // CONV_CB
    wb = col0 // CONV_CB
    r8 = tt // 8
    has_skip = skip is not None

    def body(*refs):
        if has_skip:
            u_ref, up_ref, d_ref, w_ref, b_ref, _, sk_ref, skw_ref, du_ref, dw_ref, db_ref, nx_ref = refs
        else:
            u_ref, up_ref, d_ref, w_ref, b_ref, _, du_ref, dw_ref, db_ref, nx_ref = refs
        i = pl.program_id(1)
        cur = u_ref[...]
        prev8 = jnp.where(i < nt - 1, up_ref[...], 0.0)
        sh = [_shift_down(cur, prev8, 3 - k) for k in range(3)] + [cur]
        pre = b_ref[...]
        for k in range(4):
            pre = pre + w_ref[k:k + 1, :] * sh[k]
        sg = _sigmoid(pre)
        dout_v = d_ref[...]
        if has_skip:
            dout_v = dout_v + sk_ref[...] * skw_ref[...]
        dpre = dout_v * (sg * (1.0 + pre * (1.0 - sg)))

        @pl.when(i == 0)
        def _():
            nx_ref[...] = jnp.zeros_like(nx_ref)
            dw_ref[...] = jnp.zeros_like(dw_ref)
            db_ref[...] = jnp.zeros_like(db_ref)

        next8 = nx_ref[...]
        du = w_ref[3:4, :] * dpre
        for s in range(1, 4):
            du = du + w_ref[3 - s:4 - s, :] * _shift_up(dpre, next8, s)
        du_ref[...] = du.astype(BF16)
        nx_ref[...] = dpre[:8]
        rows = [jnp.sum(dpre * sh[k], axis=0, keepdims=True) for k in range(4)]
        dw_ref[...] += jnp.concatenate(rows + [jnp.zeros((4, CONV_CB), F32)], axis=0)
        db_ref[...] += jnp.sum(dpre, axis=0, keepdims=True)

    rev = lambda c, i: (nt - 1 - i, c)
    p_in, p_out, p_shape = _dproj_piece(t, tt, CONV_CB, lambda c, i: (nt - 1 - i, base + c))
    in_specs = [pl.BlockSpec((tt, CONV_CB), lambda c, i: (nt - 1 - i, base + c)),
                pl.BlockSpec((8, CONV_CB), lambda c, i: (jnp.maximum((nt - 1 - i) * r8 - 1, 0), base + c)),
                pl.BlockSpec((tt, CONV_CB), rev),
                pl.BlockSpec((4, CONV_CB), lambda c, i: (0, wb + c)),
                pl.BlockSpec((1, CONV_CB), lambda c, i: (0, wb + c)), p_in]
    args = [proj, proj, dout, conv_w, conv_b, dproj]
    if has_skip:
        in_specs += [pl.BlockSpec((tt, CONV_CB), rev), pl.BlockSpec((1, CONV_CB), lambda c, i: (0, c))]
        args += [skip[0], skip[1]]
    return pl.pallas_call(
        body, name=name, grid=(width // CONV_CB, nt),
        in_specs=in_specs,
        out_specs=[p_out, pl.BlockSpec((8, CONV_CB), lambda c, i: (0, c)),
                   pl.BlockSpec((1, CONV_CB), lambda c, i: (0, c))],
        out_shape=[p_shape, jax.ShapeDtypeStruct((8, width), F32), jax.ShapeDtypeStruct((1, width), F32)],
        input_output_aliases={5: 0},
        scratch_shapes=[pltpu.VMEM((8, CONV_CB), F32)],
        compiler_params=_params(("parallel", "arbitrary"), 32 << 20),
    )(*args)


GW = SSD_R * SSD_P


def _expand8(v, passes=2):
    r = v.shape[0]
    if r < 8:
        v = jnp.broadcast_to(v, (8, SSD_R))
    ri = lax.broadcasted_iota(jnp.int32, (SSD_R, GW), 0)
    ci = lax.broadcasted_iota(jnp.int32, (SSD_R, GW), 1)
    spread = jnp.where((ci >= ri * SSD_P) & (ci < (ri + 1) * SSD_P), 1.0, 0.0)
    return _dot01(v, spread, passes)[:r]


def _head_pair_split(tile):
    first = lax.broadcasted_iota(jnp.int32, (1, 2 * SSD_P), 1) < SSD_P
    return jnp.where(first, tile, 0.0), jnp.where(first, 0.0, tile)


def _sel(rows, group):
    ri = lax.broadcasted_iota(jnp.int32, (rows, rows // group), 0)
    ci = lax.broadcasted_iota(jnp.int32, (rows, rows // group), 1)
    lo = ci * group
    return jnp.where((ri >= lo) & (ri < lo + group), 1.0, 0.0).astype(F32)


def _dot01(lhs, rhs, passes, split_lhs=True, dims=((1,), (0,))):
    val, m01 = (lhs, rhs) if split_lhs else (rhs, lhs)
    m01 = m01.astype(BF16)
    out = None
    for p in range(passes):
        piece = val.astype(BF16)
        ops = (piece, m01) if split_lhs else (m01, piece)
        d = lax.dot_general(ops[0], ops[1], (dims, ((), ())), preferred_element_type=F32)
        out = d if out is None else out + d
        if p + 1 < passes:
            val = val - piece.astype(F32)
    return out


def _ssd_chunk_terms(adt):
    li = lax.broadcasted_iota(jnp.int32, (CHUNK, CHUNK), 0)
    si = lax.broadcasted_iota(jnp.int32, (CHUNK, CHUNK), 1)
    causal = li >= si
    a_cs = _dot01(jnp.where(causal, 1.0, 0.0), adt, 3, split_lhs=False)
    a_cs_t = _dot01(adt, jnp.where(li <= si, 1.0, 0.0), 3, dims=((0,), (0,)))
    return a_cs, a_cs_t, causal


def _ssd_fwd(xc, dt3, adt3):
    t = xc.shape[0]
    nc = t // CHUNK

    gs = 2

    def body(xs_ref, b_ref, c_ref, dt_ref, adt_ref, y_ref, hp_ref, h_ref):
        @pl.when(pl.program_id(1) == 0)
        def _():
            h_ref[...] = jnp.zeros_like(h_ref)

        for gg in range(gs):
            a_cs, a_cs_t, causal = _ssd_chunk_terms(adt_ref[gg])
            a_last = a_cs[CHUNK - 1:CHUNK, :]
            h = h_ref[gg]
            hp_ref[gg, 0] = h
            xd = xs_ref[:, GW * gg:GW * (gg + 1)] * _expand8(dt_ref[gg])
            bb = b_ref[:, SSD_N * gg:SSD_N * (gg + 1)].astype(BF16)
            cbf = c_ref[:, SSD_N * gg:SSD_N * (gg + 1)].astype(BF16)
            cb = lax.dot_general(cbf, bb, (((1,), (1,)), ((), ())), preferred_element_type=F32)
            yoff = jnp.dot(cbf, h.astype(BF16), preferred_element_type=F32) * _expand8(jnp.exp(a_cs))
            for q in range(SSD_R // 2):
                lmats = []
                for r in (2 * q, 2 * q + 1):
                    seg = jnp.exp(jnp.where(causal, a_cs[:, r:r + 1] - a_cs_t[r:r + 1, :], -jnp.inf))
                    lmats.append((cb * seg).astype(BF16))
                tile = slice(2 * SSD_P * q, 2 * SSD_P * (q + 1))
                xa, xb = _head_pair_split(xd[:, tile])
                y_ref[:, GW * gg + 2 * SSD_P * q:GW * gg + 2 * SSD_P * (q + 1)] = (
                    jnp.dot(jnp.concatenate(lmats, axis=1), jnp.concatenate([xa, xb], axis=0).astype(BF16),
                            preferred_element_type=F32) + yoff[:, tile])
            xdd = (xd * _expand8(jnp.exp(a_last - a_cs))).astype(BF16)
            h_ref[gg] = _expand8(jnp.exp(a_last), 3) * h + lax.dot_general(
                bb, xdd, (((0,), (0,)), ((), ())), preferred_element_type=F32)

    nb = SSD_INNER // (gs * SSD_N)
    return pl.pallas_call(
        body, name="ssd_fwd", grid=(SSD_G // gs, nc),
        in_specs=[pl.BlockSpec((CHUNK, gs * GW), lambda g, c: (c, g)),
                  pl.BlockSpec((CHUNK, gs * SSD_N), lambda g, c: (c, nb + g)),
                  pl.BlockSpec((CHUNK, gs * SSD_N), lambda g, c: (c, nb + SSD_G // gs + g)),
                  pl.BlockSpec((gs, CHUNK, SSD_R), lambda g, c: (g, c, 0)),
                  pl.BlockSpec((gs, CHUNK, SSD_R), lambda g, c: (g, c, 0))],
        out_specs=[pl.BlockSpec((CHUNK, gs * GW), lambda g, c: (c, g)),
                   pl.BlockSpec((gs, 1, SSD_N, GW), lambda g, c: (g, c, 0, 0))],
        out_shape=[jax.ShapeDtypeStruct((t, SSD_INNER), F32), jax.ShapeDtypeStruct((SSD_G, nc, SSD_N, GW), F32)],
        scratch_shapes=[pltpu.VMEM((gs, SSD_N, GW), F32)],
        compiler_params=_params(("parallel", "arbitrary"), 32 << 20),
    )(xc, xc, xc, dt3, adt3)


def _ssd_bwd(xc, dt3, adt3, hprev, dy):
    t = xc.shape[0]
    nc = t // CHUNK

    gs = 4

    def body(xs_ref, b_ref, c_ref, dt_ref, adt_ref, hp_ref, dy_ref,
             dx_ref, db_ref, dc_ref, dadt_ref, dxdx_ref, dh_ref):
        @pl.when(pl.program_id(1) == 0)
        def _():
            dh_ref[...] = jnp.zeros_like(dh_ref)

        for gg in range(gs):
            wide = slice(GW * gg, GW * (gg + 1))
            narrow = slice(SSD_N * gg, SSD_N * (gg + 1))
            dx, db, dc, dadt, dxdx, dh_new = group_bwd(
                xs_ref[:, wide], b_ref[:, narrow], c_ref[:, narrow], dt_ref[gg], adt_ref[gg], hp_ref[gg, 0],
                dy_ref[:, wide], dh_ref[gg])
            dx_ref[:, wide] = dx
            db_ref[:, narrow] = db
            dc_ref[:, narrow] = dc
            dadt_ref[gg] = dadt
            dxdx_ref[gg] = dxdx
            dh_ref[gg] = dh_new

    def group_bwd(xs, b, c, dt, adt, hp, dyv, dh):
        a_cs, a_cs_t, causal = _ssd_chunk_terms(adt)
        a_last = a_cs[CHUNK - 1:CHUNK, :]
        e_last = jnp.exp(a_last)
        ex = _expand8(jnp.exp(a_cs))
        dtex = _expand8(jnp.exp(a_last - a_cs))
        dtx = _expand8(dt)
        sel = _sel(GW, SSD_P)
        seg8 = lambda v: _dot01(v, sel, 2)

        xd = xs * dtx
        xdd = xd * dtex
        bb = b.astype(BF16)
        cbf = c.astype(BF16)
        hpb = hp.astype(BF16)
        dhb = dh.astype(BF16)
        xdb = xd.astype(BF16)
        cb = lax.dot_general(cbf, bb, (((1,), (1,)), ((), ())), preferred_element_type=F32)
        dye = (dyv * ex).astype(BF16)
        yoff = jnp.dot(cbf, hpb, preferred_element_type=F32) * ex
        dc = lax.dot_general(dye, hpb, (((1,), (1,)), ((), ())), preferred_element_type=F32)
        bdh = jnp.dot(bb, dhb, preferred_element_type=F32)
        db = lax.dot_general(xdd.astype(BF16), dhb, (((1,), (1,)), ((), ())), preferred_element_type=F32)
        dxd_state = bdh * dtex
        q_terms = xdd * bdh
        d_a = seg8(dyv * yoff - q_terms)
        d_a_last = seg8(jnp.sum(q_terms, axis=0, keepdims=True)
                        + _expand8(e_last, 3) * jnp.sum(hp * dh, axis=0, keepdims=True))
        dh_new = (lax.dot_general(cbf, dye, (((0,), (0,)), ((), ())), preferred_element_type=F32)
                  + _expand8(e_last, 3) * dh)
        dcb = jnp.zeros((CHUNK, CHUNK), F32)
        w_all = []
        dxd_parts = []
        for q2 in range(SSD_R // 2):
            tile = slice(2 * SSD_P * q2, 2 * SSD_P * (q2 + 1))
            dy_pair = [part.astype(BF16) for part in _head_pair_split(dyv[:, tile])]
            lmats = []
            for k, r in enumerate((2 * q2, 2 * q2 + 1)):
                seg = jnp.exp(jnp.where(causal, a_cs[:, r:r + 1] - a_cs_t[r:r + 1, :], -jnp.inf))
                lmat = cb * seg
                dm = lax.dot_general(dy_pair[k], xdb[:, tile], (((1,), (1,)), ((), ())), preferred_element_type=F32)
                dcb = dcb + dm * seg
                w_all.append(dm * lmat)
                lmats.append(lmat.astype(BF16))
            dxd_parts.append(lax.dot_general(jnp.concatenate(lmats, axis=0), jnp.concatenate(dy_pair, axis=0),
                                             (((0,), (0,)), ((), ())), preferred_element_type=F32))
        row_sums = _dot01(jnp.concatenate(w_all, axis=1), _sel(SSD_R * CHUNK, CHUNK), 2)
        cs_rows = jnp.concatenate([jnp.sum(wr, axis=0, keepdims=True) for wr in w_all], axis=0)
        col_sums = _dot01(cs_rows, _sel(SSD_R, 1), 3, dims=((0,), (0,)))
        d_a = d_a + row_sums - col_sums
        li = lax.broadcasted_iota(jnp.int32, (CHUNK, SSD_R), 0)
        d_a = d_a + jnp.where(li == CHUNK - 1, d_a_last, 0.0)
        l2 = lax.broadcasted_iota(jnp.int32, (CHUNK, CHUNK), 0)
        s2 = lax.broadcasted_iota(jnp.int32, (CHUNK, CHUNK), 1)
        dadt = _dot01(jnp.where(s2 >= l2, 1.0, 0.0), d_a, 3, split_lhs=False)
        dxd = dxd_state + jnp.concatenate(dxd_parts, axis=1)
        dcbb = dcb.astype(BF16)
        db = db + lax.dot_general(dcbb, cbf, (((0,), (0,)), ((), ())), preferred_element_type=F32)
        dc = dc + jnp.dot(dcbb, bb, preferred_element_type=F32)
        return dxd * dtx, db, dc, dadt, seg8(dxd * xs), dh_new

    nb = SSD_INNER // (gs * SSD_N)
    rc = lambda g, c: (nc - 1 - c, g)
    r3 = lambda g, c: (g, nc - 1 - c, 0)
    return pl.pallas_call(
        body, name="ssd_bwd", grid=(SSD_G // gs, nc),
        in_specs=[pl.BlockSpec((CHUNK, gs * GW), rc),
                  pl.BlockSpec((CHUNK, gs * SSD_N), lambda g, c: (nc - 1 - c, nb + g)),
                  pl.BlockSpec((CHUNK, gs * SSD_N), lambda g, c: (nc - 1 - c, nb + SSD_G // gs + g)),
                  pl.BlockSpec((gs, CHUNK, SSD_R), r3),
                  pl.BlockSpec((gs, CHUNK, SSD_R), r3),
                  pl.BlockSpec((gs, 1, SSD_N, GW), lambda g, c: (g, nc - 1 - c, 0, 0)),
                  pl.BlockSpec((CHUNK, gs * GW), rc)],
        out_specs=[pl.BlockSpec((CHUNK, gs * GW), rc),
                   pl.BlockSpec((CHUNK, gs * SSD_N), rc),
                   pl.BlockSpec((CHUNK, gs * SSD_N), rc),
                   pl.BlockSpec((gs, CHUNK, SSD_R), r3),
                   pl.BlockSpec((gs, CHUNK, SSD_R), r3)],
        out_shape=[jax.ShapeDtypeStruct((t, SSD_INNER), F32),
                   jax.ShapeDtypeStruct((t, SSD_G * SSD_N), F32),
                   jax.ShapeDtypeStruct((t, SSD_G * SSD_N), F32),
                   jax.ShapeDtypeStruct((SSD_G, t, SSD_R), F32),
                   jax.ShapeDtypeStruct((SSD_G, t, SSD_R), F32)],
        scratch_shapes=[pltpu.VMEM((gs, SSD_N, GW), F32)],
        compiler_params=_params(("parallel", "arbitrary"), 48 << 20),
    )(xc, xc, xc, dt3, adt3, hprev, dy)


def _gated_norm_fwd(y, xc, proj, dexp, ng):
    t = y.shape[0]
    tt = _divtile(t, 256)

    def body(y_ref, x_ref, z_ref, d_ref, g_ref, o_ref):
        z = z_ref[...]
        y2 = (y_ref[...] + d_ref[...] * x_ref[...]) * (z * _sigmoid(z))
        for gi in range(SSD_G):
            sl = slice(GW * gi, GW * (gi + 1))
            seg = y2[:, sl]
            rinv = lax.rsqrt(jnp.mean(seg * seg, axis=-1, keepdims=True) + RMS_EPS)
            o_ref[:, sl] = (seg * rinv * g_ref[:, sl]).astype(BF16)

    row = pl.BlockSpec((tt, SSD_INNER), lambda i: (i, 0))
    vec = pl.BlockSpec((1, SSD_INNER), lambda i: (0, 0))
    return pl.pallas_call(
        body, name="gated_norm_fwd", grid=(t // tt,), in_specs=[row, row, row, vec, vec], out_specs=row,
        out_shape=jax.ShapeDtypeStruct((t, SSD_INNER), BF16),
        compiler_params=_params(("parallel",), 48 << 20),
    )(y, xc, proj, dexp, ng)


def _gated_norm_bwd(dout, y, xc, proj, dexp, ng, dproj):
    t = y.shape[0]
    tt = _divtile(t, 128)

    def body(do_ref, y_ref, x_ref, z_ref, d_ref, g_ref, _, dz_ref, dy_ref, dg_ref, dd_ref):
        z = z_ref[...]
        sg = _sigmoid(z)
        sz = z * sg
        xs = x_ref[...]
        y1 = y_ref[...] + d_ref[...] * xs
        y2 = y1 * sz
        dov = do_ref[...]

        @pl.when(pl.program_id(0) == 0)
        def _():
            dg_ref[...] = jnp.zeros_like(dg_ref)
            dd_ref[...] = jnp.zeros_like(dd_ref)

        for gi in range(SSD_G):
            sl = slice(GW * gi, GW * (gi + 1))
            seg = y2[:, sl]
            rinv = lax.rsqrt(jnp.mean(seg * seg, axis=-1, keepdims=True) + RMS_EPS)
            yn = seg * rinv
            dsl = dov[:, sl]
            dg_ref[:, sl] += jnp.sum(dsl * yn, axis=0, keepdims=True)
            dyn = dsl * g_ref[:, sl]
            dy2 = rinv * (dyn - yn * jnp.mean(dyn * yn, axis=-1, keepdims=True))
            dz_ref[:, sl] = (dy2 * y1[:, sl] * (sg[:, sl] * (1.0 + z[:, sl] * (1.0 - sg[:, sl])))).astype(BF16)
            dy1 = dy2 * sz[:, sl]
            dy_ref[:, sl] = dy1
            dd_ref[:, sl] += jnp.sum(dy1 * xs[:, sl], axis=0, keepdims=True)

    row = pl.BlockSpec((tt, SSD_INNER), lambda i: (i, 0))
    vec = pl.BlockSpec((1, SSD_INNER), lambda i: (0, 0))
    p_in, p_out, p_shape = _dproj_piece(t, tt, SSD_INNER, lambda i: (i, 0))
    return pl.pallas_call(
        body, name="gated_norm_bwd", grid=(t // tt,), in_specs=[row, row, row, row, vec, vec, p_in],
        out_specs=[p_out, row, vec, vec],
        out_shape=[p_shape, jax.ShapeDtypeStruct((t, SSD_INNER), F32),
                   jax.ShapeDtypeStruct((1, SSD_INNER), F32), jax.ShapeDtypeStruct((1, SSD_INNER), F32)],
        input_output_aliases={6: 0},
        compiler_params=_params(("arbitrary",), 48 << 20),
    )(dout, y, xc, proj, dexp, ng, dproj)


def _fold_heads(v, name):
    def body(v_ref, o_ref):
        ri = lax.broadcasted_iota(jnp.int32, (SSD_INNER, 128), 0)
        ci = lax.broadcasted_iota(jnp.int32, (SSD_INNER, 128), 1)
        fold = jnp.where((ri >= ci * SSD_P) & (ri < (ci + 1) * SSD_P), 1.0, 0.0).astype(F32)
        o_ref[...] = jnp.dot(v_ref[...], fold, preferred_element_type=F32, precision=HI)

    return pl.pallas_call(body, name=name, out_shape=jax.ShapeDtypeStruct((1, 128), F32))(v)


Q_BLK = SEGS["q"][2] // QW
K_BLK = SEGS["k"][2] // KVW
V_BLK = SEGS["v"][2] // KVW


def _rope_tables(pos_ref, invf_ref, width):
    ang = pos_ref[...] * invf_ref[...]
    lane = lax.broadcasted_iota(jnp.int32, (1, 128), 1)
    sign = jnp.where((lane % HD) < (HD // 2), -1.0, 1.0)
    cos = jnp.tile(jnp.cos(ang), (1, width // 128))
    sin = jnp.tile(sign * jnp.sin(ang), (1, width // 128))
    first = (lax.broadcasted_iota(jnp.int32, (1, width), 1) % HD) < (HD // 2)
    return cos, sin, first


def _rot_half(u, first):
    w = u.shape[1]
    return jnp.where(first, pltpu.roll(u, w - HD // 2, 1), pltpu.roll(u, HD // 2, 1))


def _rope_fwd(proj, pos, invf):
    t = proj.shape[0]
    tt = _divtile(t, 512)

    def body(q_ref, k_ref, pos_ref, invf_ref, qo_ref, ko_ref):
        cos, sin, first = _rope_tables(pos_ref, invf_ref, QW)
        q = q_ref[...]
        qr = q * cos + _rot_half(q, first) * sin
        for p in range(QW // 128):
            qo_ref[128 * p:128 * (p + 1), :] = qr[:, 128 * p:128 * (p + 1)].T.astype(BF16)
        k = k_ref[...]
        ko_ref[...] = (k * cos[:, :KVW] + _rot_half(k, first[:, :KVW]) * sin[:, :KVW]).astype(BF16)

    return pl.pallas_call(
        body, name="rope_fwd", grid=(t // tt,),
        in_specs=[pl.BlockSpec((tt, QW), lambda i: (i, Q_BLK)), pl.BlockSpec((tt, KVW), lambda i: (i, K_BLK)),
                  pl.BlockSpec((tt, 1), lambda i: (i, 0)), pl.BlockSpec((1, 128), lambda i: (0, 0))],
        out_specs=[pl.BlockSpec((QW, tt), lambda i: (0, i)), pl.BlockSpec((tt, KVW), lambda i: (i, 0))],
        out_shape=[jax.ShapeDtypeStruct((QW, t), BF16), jax.ShapeDtypeStruct((t, KVW), BF16)],
        compiler_params=_params(("parallel",), 40 << 20),
    )(proj, proj, pos, invf)


def _rope_bwd(dqt, dk, pos, invf, dproj):
    t = dk.shape[0]
    tt = _divtile(t, 512)

    def body(dq_ref, dk_ref, pos_ref, invf_ref, _, qo_ref, ko_ref):
        cos, sin, first = _rope_tables(pos_ref, invf_ref, QW)
        q = jnp.concatenate([dq_ref[128 * p:128 * (p + 1), :].T for p in range(QW // 128)], axis=1)
        qo_ref[...] = (q * cos + _rot_half(q * sin, first)).astype(BF16)
        k = dk_ref[...]
        ko_ref[...] = (k * cos[:, :KVW] + _rot_half(k * sin[:, :KVW], first[:, :KVW])).astype(BF16)

    p_in, p_out, p_shape = _dproj_piece(t, tt, QW, lambda i: (i, Q_BLK))
    return pl.pallas_call(
        body, name="rope_bwd", grid=(t // tt,),
        in_specs=[pl.BlockSpec((QW, tt), lambda i: (0, i)), pl.BlockSpec((tt, KVW), lambda i: (i, 0)),
                  pl.BlockSpec((tt, 1), lambda i: (i, 0)), pl.BlockSpec((1, 128), lambda i: (0, 0)), p_in],
        out_specs=[p_out, pl.BlockSpec((tt, KVW), lambda i: (i, 0))],
        out_shape=[p_shape, jax.ShapeDtypeStruct((t, KVW), BF16)],
        input_output_aliases={4: 0},
        compiler_params=_params(("parallel",), 40 << 20),
    )(dqt, dk, pos, invf, dproj)


def _place_cols(piece, dproj, col_blk, name):
    t, w = piece.shape
    tt = _divtile(t, 1024)

    def body(p_ref, _, o_ref):
        o_ref[...] = p_ref[...]

    p_in, p_out, p_shape = _dproj_piece(t, tt, w, lambda i: (i, col_blk))
    return pl.pallas_call(
        body, name=name, grid=(t // tt,),
        in_specs=[pl.BlockSpec((tt, w), lambda i: (i, 0)), p_in], out_specs=p_out, out_shape=p_shape,
        input_output_aliases={1: 0},
        compiler_params=_params(("parallel",), 16 << 20),
    )(piece, dproj)


GQ = NQ // NKV
NT_DIMS = (((1,), (1,)), ((), ()))
TN_DIMS = (((0,), (0,)), ((), ()))


def _attn_heads(ref, j, dtype=None):
    out = jnp.concatenate([ref[HD * h:HD * (h + 1), :] for h in range(j * GQ, (j + 1) * GQ)], axis=1)
    return out if dtype is None else out.astype(dtype)


def _attn_sink_row(s_ref, j):
    return jnp.concatenate([jnp.broadcast_to(s_ref[:, h:h + 1], (1, WINDOW)) for h in range(j * GQ, (j + 1) * GQ)],
                           axis=1)


def _attn_mask(n):
    kr = lax.broadcasted_iota(jnp.int32, (2 * WINDOW, GQ * WINDOW), 0)
    qi = lax.broadcasted_iota(jnp.int32, (2 * WINDOW, GQ * WINDOW), 1) % WINDOW
    return (kr > qi) & (kr <= qi + WINDOW) & ((n > 0) | (kr >= WINDOW))


def _attn_probs(qgt, kk, sink, mask):
    s = jnp.where(mask, jnp.dot(kk, qgt, preferred_element_type=F32) * (HD ** -0.5), -jnp.inf)
    m = jnp.maximum(jnp.max(s, axis=0, keepdims=True), sink)
    p = jnp.exp(s - m)
    ps = jnp.exp(sink - m)
    inv = 1.0 / (jnp.sum(p, axis=0, keepdims=True) + ps)
    return p * inv, ps * inv


def _attn_fwd(qt, kr, proj, sinks):
    t = kr.shape[0]
    nb = t // WINDOW

    def body(q_ref, kc_ref, kp_ref, vc_ref, vp_ref, s_ref, o_ref):
        mask = _attn_mask(pl.program_id(0))
        for j in range(NKV):
            ks = slice(HD * j, HD * (j + 1))
            kk = jnp.concatenate([kp_ref[:, ks], kc_ref[:, ks]], axis=0)
            vv = jnp.concatenate([vp_ref[:, ks], vc_ref[:, ks]], axis=0).astype(BF16)
            pn, _ = _attn_probs(_attn_heads(q_ref, j), kk, _attn_sink_row(s_ref, j), mask)
            ot = lax.dot_general(vv, pn.astype(BF16), TN_DIMS, preferred_element_type=F32).astype(BF16)
            for g in range(GQ):
                h = j * GQ + g
                o_ref[HD * h:HD * (h + 1), :] = ot[:, WINDOW * g:WINDOW * (g + 1)]

    prev = lambda n: (jnp.maximum(n - 1, 0), 0)
    return pl.pallas_call(
        body, name="attn_fwd", grid=(nb,),
        in_specs=[pl.BlockSpec((QW, WINDOW), lambda n: (0, n)),
                  pl.BlockSpec((WINDOW, KVW), lambda n: (n, 0)), pl.BlockSpec((WINDOW, KVW), prev),
                  pl.BlockSpec((WINDOW, KVW), lambda n: (n, V_BLK)),
                  pl.BlockSpec((WINDOW, KVW), lambda n: (jnp.maximum(n - 1, 0), V_BLK)),
                  pl.BlockSpec((1, 128), lambda n: (0, 0))],
        out_specs=pl.BlockSpec((QW, WINDOW), lambda n: (0, n)),
        out_shape=jax.ShapeDtypeStruct((QW, t), BF16),
        compiler_params=_params(("parallel",), 24 << 20),
    )(qt, kr, kr, proj, proj, sinks)


def _attn_bwd(qt, kr, proj, sinks, dot_, dproj):
    t = kr.shape[0]
    nb = t // WINDOW

    def body(q_ref, kc_ref, kp_ref, vc_ref, vp_ref, s_ref, do_ref, _,
             dq_ref, dk_ref, dv_ref, ds_ref, dkc_ref, dvc_ref):
        i = pl.program_id(0)
        mask = _attn_mask(nb - 1 - i)

        @pl.when(i == 0)
        def _():
            dkc_ref[...] = jnp.zeros_like(dkc_ref)
            dvc_ref[...] = jnp.zeros_like(dvc_ref)
            ds_ref[...] = jnp.zeros_like(ds_ref)

        lane = lax.broadcasted_iota(jnp.int32, (1, 128), 1)
        ds_acc = jnp.zeros((1, 128), F32)
        for j in range(NKV):
            ks = slice(HD * j, HD * (j + 1))
            kk = jnp.concatenate([kp_ref[:, ks], kc_ref[:, ks]], axis=0)
            vv = jnp.concatenate([vp_ref[:, ks], vc_ref[:, ks]], axis=0).astype(BF16)
            qgt = _attn_heads(q_ref, j)
            pn, psn = _attn_probs(qgt, kk, _attn_sink_row(s_ref, j), mask)
            dogt = _attn_heads(do_ref, j)
            dp = jnp.dot(vv, dogt, preferred_element_type=F32)
            delta = jnp.sum(dp * pn, axis=0, keepdims=True)
            dsb = (pn * (dp - delta) * (HD ** -0.5)).astype(BF16)
            dsink = -psn * delta
            dqt = lax.dot_general(kk, dsb, TN_DIMS, preferred_element_type=F32)
            for g in range(GQ):
                h = j * GQ + g
                cols = slice(WINDOW * g, WINDOW * (g + 1))
                dq_ref[HD * h:HD * (h + 1), :] = dqt[:, cols]
                ds_acc = ds_acc + jnp.where(lane == h, jnp.sum(dsink[:, cols], axis=1, keepdims=True), 0.0)
            dkk = lax.dot_general(dsb, qgt, NT_DIMS, preferred_element_type=F32)
            dvv = lax.dot_general(pn.astype(BF16), dogt, NT_DIMS, preferred_element_type=F32)
            dk_ref[:, ks] = dkk[WINDOW:] + dkc_ref[:, ks]
            dv_ref[:, ks] = (dvv[WINDOW:] + dvc_ref[:, ks]).astype(BF16)
            dkc_ref[:, ks] = dkk[:WINDOW]
            dvc_ref[:, ks] = dvv[:WINDOW]
        ds_ref[...] += ds_acc

    cur = lambda i: (nb - 1 - i, 0)
    cur_t = lambda i: (0, nb - 1 - i)
    prev = lambda i: (jnp.maximum(nb - 2 - i, 0), 0)
    p_in, p_out, p_shape = _dproj_piece(t, WINDOW, KVW, lambda i: (nb - 1 - i, V_BLK))
    return pl.pallas_call(
        body, name="attn_bwd", grid=(nb,),
        in_specs=[pl.BlockSpec((QW, WINDOW), cur_t),
                  pl.BlockSpec((WINDOW, KVW), cur), pl.BlockSpec((WINDOW, KVW), prev),
                  pl.BlockSpec((WINDOW, KVW), lambda i: (nb - 1 - i, V_BLK)),
                  pl.BlockSpec((WINDOW, KVW), lambda i: (jnp.maximum(nb - 2 - i, 0), V_BLK)),
                  pl.BlockSpec((1, 128), lambda i: (0, 0)),
                  pl.BlockSpec((QW, WINDOW), cur_t), p_in],
        out_specs=[pl.BlockSpec((QW, WINDOW), cur_t), pl.BlockSpec((WINDOW, KVW), cur),
                   p_out, pl.BlockSpec((1, 128), lambda i: (0, 0))],
        out_shape=[jax.ShapeDtypeStruct((QW, t), F32), jax.ShapeDtypeStruct((t, KVW), F32),
                   p_shape, jax.ShapeDtypeStruct((1, 128), F32)],
        input_output_aliases={7: 2},
        scratch_shapes=[pltpu.VMEM((WINDOW, KVW), F32), pltpu.VMEM((WINDOW, KVW), F32)],
        compiler_params=_params(("arbitrary",), 32 << 20),
    )(qt, kr, kr, proj, proj, sinks, dot_, dproj)


GS_BLK = SEGS["gs"][2] // D
GA_BLK = SEGS["ga"][2] // D


def _merge_fwd(ys, ya, proj):
    t = ys.shape[0]
    tt = _divtile(t, 256)

    def body(ys_ref, ya_ref, gs_ref, ga_ref, o_ref):
        o_ref[...] = (_sigmoid(gs_ref[...]) * ys_ref[...] + _sigmoid(ga_ref[...]) * ya_ref[...]).astype(BF16)

    row = pl.BlockSpec((tt, D), lambda i: (i, 0))
    return pl.pallas_call(
        body, name="merge_fwd", grid=(t // tt,),
        in_specs=[row, row, pl.BlockSpec((tt, D), lambda i: (i, GS_BLK)), pl.BlockSpec((tt, D), lambda i: (i, GA_BLK))],
        out_specs=row, out_shape=jax.ShapeDtypeStruct((t, D), BF16),
        compiler_params=_params(("parallel",), 32 << 20),
    )(ys, ya, proj, proj)


def _dproj_piece(t, rows, width, index_map):
    return (pl.BlockSpec(memory_space=pl.ANY), pl.BlockSpec((rows, width), index_map),
            jax.ShapeDtypeStruct((t, PROJ_PAD), BF16))


def _merge_bwd(dm, ys, ya, proj, dproj):
    t = ys.shape[0]
    tt = _divtile(t, 256)

    def body(dm_ref, ys_ref, ya_ref, gs_ref, ga_ref, _, dys_ref, dya_ref, dg_ref):
        d = dm_ref[...]
        s = _sigmoid(gs_ref[...])
        a = _sigmoid(ga_ref[...])
        dys_ref[...] = (d * s).astype(BF16)
        dya_ref[...] = (d * a).astype(BF16)
        dg_ref[:, :D] = (d * ys_ref[...] * (s * (1.0 - s))).astype(BF16)
        dg_ref[:, D:] = (d * ya_ref[...] * (a * (1.0 - a))).astype(BF16)

    row = pl.BlockSpec((tt, D), lambda i: (i, 0))
    p_in, p_out, p_shape = _dproj_piece(t, tt, 2 * D, lambda i: (i, SEGS["gs"][2] // (2 * D)))
    return pl.pallas_call(
        body, name="merge_bwd", grid=(t // tt,),
        in_specs=[row, row, row, pl.BlockSpec((tt, D), lambda i: (i, GS_BLK)),
                  pl.BlockSpec((tt, D), lambda i: (i, GA_BLK)), p_in],
        out_specs=[row, row, p_out], out_shape=[jax.ShapeDtypeStruct((t, D), BF16)] * 2 + [p_shape],
        input_output_aliases={5: 2},
        compiler_params=_params(("parallel",), 40 << 20),
    )(dm, ys, ya, proj, proj, dproj)


def _pad128(v):
    return jnp.pad(v, ((0, 0), (0, 128 - v.shape[1])))


def _local_step(x, pos, target, w, small, fetch=None, early_grads=None):
    w = dict(w)
    xb = x.astype(BF16)
    if fetch is None:
        gu1, a1 = _mm_swiglu(xb, w["gu1"], "ffn1_gu")
    else:
        own = _mm_swiglu(xb, w["gu1_own"], "ffn1_gu_own", chip_idx=w["chip_idx"])
        w.update(fetch(0, own[1]))
        gu1, a1 = _mm_swiglu(xb, w["gu1"], "ffn1_gu_rest", chip_idx=w["chip_idx"], done=own)
        w.update(fetch(1, a1))
    f1 = _mm(a1, w["d1"], "nn", F32, "ffn1_down", caps=(512, 1024, FFN_H))
    h1, h1b, xh1, rs1 = _ln_fwd(x, f1, small["ln1_g"], small["ln1_b"], 0.5, "ln1_fwd")
    if fetch is not None:
        w.update(fetch(2, h1b))
    proj = _mm(h1b, w["win"], "nn", F32, "proj", caps=(1024, 896, 2048))
    if fetch is not None:
        w.update(fetch(3, proj))
    bias128 = _pad128(small["dt_bias"])
    alog128 = _pad128(small["a_log"])
    dt3, adt3 = _dt_prep(proj, bias128, alog128)
    xc = _conv_fwd(proj, small["conv_w"], small["conv_b"])
    y_ssd, hprev = _ssd_fwd(xc, dt3, adt3)
    dexp = jnp.repeat(small["d_skip"], SSD_P, axis=1)
    ysn = _gated_norm_fwd(y_ssd, xc, proj, dexp, small["ssd_norm_g"])
    ys = _mm(ysn, w["so"], "nn", F32, "ssd_out")
    invf = jnp.tile(ROPE_THETA ** (-jnp.arange(HD // 2, dtype=F32) * 2.0 / HD), 4)[None, :]
    qt, kr = _rope_fwd(proj, pos, invf)
    sinks128 = _pad128(small["attn_sinks"])
    ot = _attn_fwd(qt, kr, proj, sinks128)
    ya = _mm(ot, w["ao"], "tn", F32, "attn_out")
    mg = _merge_fwd(ys, ya, proj)
    mix = _mm(mg, w["out"], "nn", F32, "mix_out")
    h2, h2b, xh2, rs2 = _ln_fwd(h1, mix, small["ln2_g"], small["ln2_b"], 1.0, "ln2_fwd")
    gu2, a2 = _mm_swiglu(h2b, w["gu2"], "ffn2_gu")
    f2 = _mm(a2, w["d2"], "nn", F32, "ffn2_down", caps=(512, 1024, FFN_H))
    _, _, xh3, rs3, dh3, loss = _ln_fwd(h2, f2, small["ln3_g"], small["ln3_b"], 0.5, "ln3_fwd", target=target)

    gw, gs = {}, {}
    dr3, dr3h, gs["ln3_g"], gs["ln3_b"] = _ln_bwd(dh3, xh3, rs3, small["ln3_g"], 0.5, "ln3_bwd")
    gw["d2"] = _mm(a2, dr3h, "tn", F32, "ffn2_down_dw")
    dgu2 = _mm_swiglu_bwd(dr3h, w["d2"], gu2, "ffn2_down_dx")
    gw["gu2"] = _mm(h2b, dgu2, "tn", F32, "ffn2_gu_dw", caps=(1024, 1408, 2048), n_slabs=N_CHIPS)
    dh2 = _mm(dgu2, w["gu2"], "nt", F32, "ffn2_gu_dx", add=dr3, add_scale=ALPHA, caps=(1024, 1024, 2816))
    dr2, dr2b, gs["ln2_g"], gs["ln2_b"] = _ln_bwd(dh2, xh2, rs2, small["ln2_g"], 1.0, "ln2_bwd")
    gw["out"] = _mm(mg, dr2b, "tn", F32, "mix_out_dw")
    dmg = _mm(dr2b, w["out"], "nt", F32, "mix_out_dx")
    dproj = lax.empty((x.shape[0], PROJ_PAD), BF16)
    dys, dya, dproj = _merge_bwd(dmg, ys, ya, proj, dproj)
    gw["ao"] = _mm(ot, dya, "nn", F32, "attn_out_dw")
    dot_ = _mm(w["ao"], dya, "nt", BF16, "attn_out_dx")
    dqt, dkr, dproj, gs["attn_sinks"] = _attn_bwd(qt, kr, proj, sinks128, dot_, dproj)
    dproj, dk = _rope_bwd(dqt, dkr, pos, invf, dproj)
    dproj = _place_cols(dk, dproj, K_BLK, "place_dk")
    gw["so"] = _mm(ysn, dys, "tn", F32, "ssd_out_dw")
    dysn = _mm(dys, w["so"], "nt", F32, "ssd_out_dx")
    dproj, dy1, gs["ssd_norm_g"], dd_ch = _gated_norm_bwd(dysn, y_ssd, xc, proj, dexp, small["ssd_norm_g"], dproj)
    gs["d_skip"] = _fold_heads(dd_ch, "d_skip_fold")
    dxs, db, dc, dadt3, dxdx3 = _ssd_bwd(xc, dt3, adt3, hprev, dy1)
    ddt, gs["dt_bias"], gs["a_log"] = _dt_bwd(dadt3, dxdx3, proj, bias128, alog128)
    dproj = _place_cols(ddt, dproj, DT_BLK, "place_ddt")
    cw, cbias = small["conv_w"], small["conv_b"]
    dproj, dwx, dbx = _conv_bwd(proj, dxs, cw, cbias, 0, SSD_INNER, "conv_bwd_x", dproj, skip=(dy1, dexp))
    dproj, dwb, dbb = _conv_bwd(proj, db, cw, cbias, SSD_INNER, SSD_G * SSD_N, "conv_bwd_b", dproj)
    dproj, dwc, dbc = _conv_bwd(proj, dc, cw, cbias, SSD_INNER + SSD_G * SSD_N, SSD_G * SSD_N, "conv_bwd_c", dproj)
    gs["conv_w"] = jnp.concatenate([dwx[:4], dwb[:4], dwc[:4]], axis=1)
    gs["conv_b"] = jnp.concatenate([dbx, dbb, dbc], axis=1)
    gw["win"] = _mm(h1b, dproj, "tn", F32, "proj_dw", caps=(1024, 896, 2048))
    win = w["win"] if early_grads is None else early_grads[0](gw, w["win"])
    dh1 = _mm(dproj, win, "nt", F32, "proj_dx", add=dr2, add_scale=ALPHA, caps=(1024, 1024, 2432))
    ln1_g = small["ln1_g"]
    if early_grads is not None:
        ln1_g = ln1_g + early_grads[1](dh1)[0:1, 0:1]
    dr1, dr1h, gs["ln1_g"], gs["ln1_b"] = _ln_bwd(dh1, xh1, rs1, ln1_g, 0.5, "ln1_bwd")
    gw["d1"] = _mm(a1, dr1h, "tn", F32, "ffn1_down_dw")
    dgu1 = _mm_swiglu_bwd(dr1h, w["d1"], gu1, "ffn1_down_dx")
    gw["gu1"] = _mm(xb, dgu1, "tn", F32, "ffn1_gu_dw", caps=(1024, 1408, 2048), n_slabs=N_CHIPS)
    gu1w = w["gu1"] if early_grads is None else early_grads[2](gw, w["gu1"])
    grad_x = _mm(dgu1, gu1w, "nt", F32, "ffn1_gu_dx", add=dr1, add_scale=ALPHA, caps=(1024, 1024, 2816))
    return loss, grad_x, gw, gs


MESH = pl.DeviceIdType.MESH
ANY = pl.BlockSpec(memory_space=pl.ANY)


def _place():
    x, y, c = lax.axis_index("x"), lax.axis_index("y"), lax.axis_index("c")
    peers = [(1 - x, y), (x, 1 - y), (1 - x, 1 - y)]
    return x, y, c, peers


BIG = [
    ("ffn1_w_gate", D, SHARD_H, "gu1", "col", 0),
    ("ffn1_w_up", D, SHARD_H, "gu1", "col", SHARD_H),
    ("ffn1_w_down", SHARD_H, D, "d1", "row", 0),
    ("w_in", D, SHARD_IN, "win4", "lead", 0),
    ("w_ssd_o", SSD_INNER // N_CHIPS, D, "so", "row", 0),
    ("w_attn_o", D // N_CHIPS, D, "ao", "row", 0),
    ("w_out", D // N_CHIPS, D, "out", "row", 0),
    ("ffn2_w_gate", D, SHARD_H, "gu2", "col", 0),
    ("ffn2_w_up", D, SHARD_H, "gu2", "col", SHARD_H),
    ("ffn2_w_down", SHARD_H, D, "d2", "row", 0),
]
GATHERED = {"gu1": (D, 2 * FFN_H), "d1": (FFN_H, D), "win4": (N_CHIPS, D, SHARD_IN), "so": (SSD_INNER, D),
            "ao": (D, D), "out": (D, D), "gu2": (D, 2 * FFN_H), "d2": (FFN_H, D)}


def _cast_place(srcs, oname, chip_idx, also_alone=False):
    rows, cols = srcs[0].shape
    tr = _divtile(rows, 256, 16)
    kind = [b[4] for b in BIG if b[3] == oname][0]
    n_src = len(srcs)

    def body(chip_ref, *refs):
        for o_ref in refs[n_src:]:
            for k, s_ref in enumerate(refs[:n_src]):
                o_ref[:, k * cols:(k + 1) * cols] = s_ref[...].astype(BF16)

    nt = rows // tr
    if kind == "col":
        o_spec = pl.BlockSpec((tr, n_src * cols), lambda i, chip_ref: (i, chip_ref[0]))
    elif kind == "row":
        o_spec = pl.BlockSpec((tr, cols), lambda i, chip_ref: (chip_ref[0] * nt + i, 0))
    else:
        o_spec = pl.BlockSpec((None, tr, cols), lambda i, chip_ref: (chip_ref[0], i, 0))
    out_specs, out_shape = [o_spec], [jax.ShapeDtypeStruct(GATHERED[oname], BF16)]
    if also_alone:
        out_specs.append(pl.BlockSpec((tr, n_src * cols), lambda i, chip_ref: (i, 0)))
        out_shape.append(jax.ShapeDtypeStruct((rows, n_src * cols), BF16))
    res = pl.pallas_call(
        body, name="cast_place_" + oname,
        grid_spec=pltpu.PrefetchScalarGridSpec(
            num_scalar_prefetch=1, grid=(nt,),
            in_specs=[pl.BlockSpec((tr, cols), lambda i, chip_ref: (i, 0))] * n_src, out_specs=out_specs),
        out_shape=out_shape,
        compiler_params=_params(("parallel",), 32 << 20),
    )(chip_idx, *srcs)
    return res if also_alone else res[0]


def _slot(outs, entry, j, half):
    _, rows, cols, oname, kind, off = entry
    o = outs[oname]
    hr = rows // 2
    if kind == "col":
        cs = pl.ds(pl.multiple_of(j * (2 * SHARD_H) + off, 128), cols)
        return o.at[pl.ds(pl.multiple_of(half * hr, 16), hr), cs]
    if kind == "row":
        return o.at[pl.ds(pl.multiple_of(j * rows + half * hr, 16), hr), :]
    return o.at[j, pl.ds(pl.multiple_of(half * hr, 16), hr), :]


HBM = pl.BlockSpec(memory_space=pltpu.HBM)
SEM = pl.BlockSpec(memory_space=pltpu.SEMAPHORE)


def _ici_copy(outs, entry, j, c, to, send, recv, k):
    ref = _slot(outs, entry, j, c)
    return pltpu.make_async_remote_copy(src_ref=ref, dst_ref=ref, send_sem=send.at[k], recv_sem=recv.at[k],
                                        device_id=to, device_id_type=MESH)


GATHER_GROUPS = [["gu1"], ["d1"], ["win4"], ["so", "ao", "out", "gu2", "d2"]]


def _gather_ici_start(placed, groups, tag, carried):
    names = [k for grp in groups for k in grp]
    bigs = [[b for b in BIG if b[3] in grp] for grp in groups]
    ng = len(groups)
    n_in = len(names) + 1

    def body(*refs):
        sems = refs[n_in:n_in + 2 * ng]
        outs = dict(zip(names, refs[n_in + 2 * ng:n_in + 2 * ng + len(names)]))
        token = refs[-1]
        x, y, c, peers = _place()
        for gi, big in enumerate(bigs):
            for i, entry in enumerate(big):
                for k, (px, py) in enumerate(peers):
                    _ici_copy(outs, entry, 2 * x + y, c, (px, py, c), sems[2 * gi], sems[2 * gi + 1], 3 * i + k).start()
        token[...] = jnp.zeros_like(token)

    sem_shapes = [pltpu.SemaphoreType.DMA((3 * len(big),)) for big in bigs for _ in range(2)]
    res = pl.pallas_call(
        body, name="gather_ici_start_" + tag,
        in_specs=[HBM] * n_in,
        out_specs=[SEM] * (2 * ng) + [HBM] * n_in + [pl.BlockSpec(memory_space=pltpu.VMEM)],
        out_shape=sem_shapes + [pltpu.HBM(GATHERED[k], BF16) for k in names]
        + [pltpu.HBM(carried.shape, carried.dtype), jax.ShapeDtypeStruct((8, 128), F32)],
        input_output_aliases={i: i + 2 * ng for i in range(n_in)},
        compiler_params=pltpu.CompilerParams(has_side_effects=pltpu.SideEffectType.DATAFLOW_SIDE_EFFECTING),
    )(*[pltpu.with_memory_space_constraint(a, pltpu.HBM) for a in [placed[k] for k in names] + [carried]])
    sems = [(res[2 * gi], res[2 * gi + 1]) for gi in range(ng)]
    return sems, dict(zip(names, res[2 * ng:2 * ng + len(names)])), res[2 * ng + len(names)]


def _gather_ici_wait(send, recv, arrays, names, after, tag):
    big = [b for b in BIG if b[3] in names]

    def body(*refs):
        outs = dict(zip(names, refs[:len(names)]))
        send_ref, recv_ref = refs[len(names)], refs[len(names) + 1]
        x, y, c, peers = _place()
        for i, entry in enumerate(big):
            for k, (px, py) in enumerate(peers):
                mine = _ici_copy(outs, entry, 2 * x + y, c, (px, py, c), send_ref, recv_ref, 3 * i + k)
                mine.wait_send()
                theirs = _ici_copy(outs, entry, 2 * px + py, c, (px, py, c), send_ref, recv_ref, 3 * i + k)
                theirs.wait_recv()

    res = pl.pallas_call(
        body, name="gather_ici_wait_" + tag,
        in_specs=[HBM] * len(names) + [SEM, SEM, pl.BlockSpec(memory_space=pl.ANY)],
        out_specs=[HBM] * len(names),
        out_shape=[pltpu.HBM(GATHERED[k], BF16) for k in names],
        input_output_aliases={i: i for i in range(len(names))},
        compiler_params=pltpu.CompilerParams(has_side_effects=pltpu.SideEffectType.DATAFLOW_SIDE_EFFECTING),
    )(*[arrays[k] for k in names], send, recv, after)
    return dict(zip(names, res))


def _gather_d2d(arrays, names, tag):
    big = [b for b in BIG if b[3] in names]
    n = len(big)

    def body(*refs):
        outs = dict(zip(names, refs[len(names):2 * len(names)]))
        fsend, frecv = refs[2 * len(names):]
        x, y, c, peers = _place()
        cps = []
        for i, entry in enumerate(big):
            for k, (px, py) in enumerate(peers):
                cp = _ici_copy(outs, entry, 2 * px + py, c, (x, y, 1 - c), fsend, frecv, 3 * i + k)
                cp.start()
                cps.append(cp)
        for i, entry in enumerate(big):
            for k, (px, py) in enumerate(peers):
                _ici_copy(outs, entry, 2 * px + py, 1 - c, (x, y, 1 - c), fsend, frecv, 3 * i + k).wait_recv()
        for cp in cps:
            cp.wait_send()

    res = pl.pallas_call(
        body, name="gather_d2d_" + tag,
        in_specs=[ANY] * len(names), out_specs=[ANY] * len(names),
        out_shape=[jax.ShapeDtypeStruct(GATHERED[k], BF16) for k in names],
        input_output_aliases={i: i for i in range(len(names))},
        scratch_shapes=[pltpu.SemaphoreType.DMA((3 * n,))] * 2,
    )(*[arrays[k] for k in names])
    return dict(zip(names, res))


def _win_pieces():
    pieces = []
    for g0, wd, i0 in SEGS.values():
        for j in range(N_CHIPS):
            lo, hi = max(g0, j * SHARD_IN), min(g0 + wd, (j + 1) * SHARD_IN)
            if lo < hi:
                pieces.append((j, lo - j * SHARD_IN, hi - j * SHARD_IN, i0 + lo - g0))
    return pieces


def _win_to_internal(win4):
    tr = 128

    def body(i_ref, o_ref):
        for j, s0, s1, d0 in _win_pieces():
            o_ref[:, d0:d0 + s1 - s0] = i_ref[j, :, s0:s1]
        o_ref[:, PROJ_W:] = jnp.zeros((tr, PROJ_PAD - PROJ_W), o_ref.dtype)

    return pl.pallas_call(
        body, name="win_to_internal", grid=(D // tr,),
        in_specs=[pl.BlockSpec((N_CHIPS, tr, SHARD_IN), lambda i: (0, i, 0))],
        out_specs=pl.BlockSpec((tr, PROJ_PAD), lambda i: (i, 0)),
        out_shape=jax.ShapeDtypeStruct((D, PROJ_PAD), win4.dtype),
        compiler_params=_params(("parallel",), 40 << 20),
    )(win4)


def _win_from_internal(g):
    tr = 64

    def body(i_ref, o_ref):
        for j, s0, s1, d0 in _win_pieces():
            o_ref[j, :, s0:s1] = i_ref[:, d0:d0 + s1 - s0]

    return pl.pallas_call(
        body, name="win_from_internal", grid=(D // tr,),
        in_specs=[pl.BlockSpec((tr, PROJ_PAD), lambda i: (i, 0))],
        out_specs=pl.BlockSpec((N_CHIPS, tr, SHARD_IN), lambda i: (0, i, 0)),
        out_shape=jax.ShapeDtypeStruct((N_CHIPS, D, SHARD_IN), g.dtype),
        compiler_params=_params(("parallel",), 40 << 20),
    )(g)


def _pair_copy(src, dst, c, to, send, recv, k):
    hr = src.shape[1] // 2
    return pltpu.make_async_remote_copy(
        src_ref=src.at[:, pl.ds(pl.multiple_of((1 - c) * hr, 16), hr), :], dst_ref=dst,
        send_sem=send.at[k], recv_sem=recv.at[k], device_id=to, device_id_type=MESH)


def _rs_pair_start(grads, carried, tag):
    n = len(grads)

    def body(*refs):
        send, recv = refs[2 * n + 1], refs[2 * n + 2]
        srcs, dsts = refs[2 * n + 3:3 * n + 3], refs[3 * n + 3:4 * n + 3]
        x, y, c, _ = _place()
        for i in range(n):
            _pair_copy(srcs[i], dsts[i], c, (x, y, 1 - c), send, recv, i).start()

    lands = [lax.empty((g.shape[0], g.shape[1] // 2, g.shape[2]), F32) for g in grads]
    res = pl.pallas_call(
        body, name="rs_pair_start_" + tag,
        in_specs=[HBM] * (2 * n + 1), out_specs=[SEM, SEM] + [HBM] * (2 * n + 1),
        out_shape=[pltpu.SemaphoreType.DMA((n,)), pltpu.SemaphoreType.DMA((n,))]
        + [pltpu.HBM(g.shape, F32) for g in grads] + [pltpu.HBM(l.shape, F32) for l in lands]
        + [pltpu.HBM(carried.shape, carried.dtype)],
        input_output_aliases={i: i + 2 for i in range(2 * n + 1)},
        compiler_params=pltpu.CompilerParams(has_side_effects=pltpu.SideEffectType.DATAFLOW_SIDE_EFFECTING),
    )(*[pltpu.with_memory_space_constraint(a, pltpu.HBM) for a in list(grads) + lands + [carried]])
    return (res[0], res[1], list(res[2:2 + n]), list(res[2 + n:2 + 2 * n])), res[-1]


def _rs_pair_wait(send, recv, grads, lands, after, tag):
    n = len(grads)

    def body(*refs):
        srcs, dsts = refs[:n], refs[n:2 * n]
        send_ref, recv_ref = refs[2 * n], refs[2 * n + 1]
        x, y, c, _ = _place()
        for i in range(n):
            cp = _pair_copy(srcs[i], dsts[i], c, (x, y, 1 - c), send_ref, recv_ref, i)
            cp.wait_send()
            cp.wait_recv()

    res = pl.pallas_call(
        body, name="rs_pair_wait_" + tag,
        in_specs=[HBM] * (2 * n) + [SEM, SEM, pl.BlockSpec(memory_space=pl.ANY)],
        out_specs=[HBM] * (2 * n),
        out_shape=[pltpu.HBM(g.shape, F32) for g in grads] + [pltpu.HBM(l.shape, F32) for l in lands],
        input_output_aliases={i: i for i in range(2 * n)},
        compiler_params=pltpu.CompilerParams(has_side_effects=pltpu.SideEffectType.DATAFLOW_SIDE_EFFECTING),
    )(*grads, *lands, send, recv, after)
    return list(res[:n]), list(res[n:])


def _half_tile(hr):
    return _divtile(hr, 256, 16) if hr % 256 == 0 else _divtile(hr, 512, 16)


def _rs_pair_sum(g, r, c_idx, name):
    ns, rows, cols = g.shape
    hr = rows // 2
    tr = _half_tile(hr)
    nt = hr // tr

    def body(c_ref, g_ref, r_ref, ob_ref, of_ref):
        s = g_ref[...] + r_ref[...]
        ob_ref[...] = s.astype(BF16)
        of_ref[...] = s

    blk = pl.BlockSpec((None, tr, cols), lambda j, t, c_ref: (j, t, 0))
    return pl.pallas_call(
        body, name=name,
        grid_spec=pltpu.PrefetchScalarGridSpec(
            num_scalar_prefetch=1, grid=(ns, nt),
            in_specs=[pl.BlockSpec((None, tr, cols), lambda j, t, c_ref: (j, c_ref[0] * nt + t, 0)), blk],
            out_specs=[blk, blk]),
        out_shape=[jax.ShapeDtypeStruct((ns, hr, cols), BF16), jax.ShapeDtypeStruct((ns, hr, cols), F32)],
        compiler_params=_params(("parallel", "parallel"), 48 << 20),
    )(c_idx, g, r)


def _rs_chip_start(parts, tag):
    n = len(parts)

    def body(*refs):
        send, recv = refs[2 * n], refs[2 * n + 1]
        srcs, dsts = refs[2 * n + 2:3 * n + 2], refs[3 * n + 2:4 * n + 2]
        token = refs[-1]
        x, y, c, peers = _place()
        for i in range(n):
            for k, (px, py) in enumerate(peers):
                pltpu.make_async_remote_copy(
                    src_ref=srcs[i].at[2 * px + py], dst_ref=dsts[i].at[k],
                    send_sem=send.at[3 * i + k], recv_sem=recv.at[3 * i + k],
                    device_id=(px, py, c), device_id_type=MESH).start()
        token[...] = jnp.zeros_like(token)

    lands = [lax.empty((3,) + p.shape[1:], BF16) for p in parts]
    res = pl.pallas_call(
        body, name="rs_chip_start_" + tag,
        in_specs=[HBM] * (2 * n),
        out_specs=[SEM, SEM] + [HBM] * (2 * n) + [pl.BlockSpec(memory_space=pltpu.VMEM)],
        out_shape=[pltpu.SemaphoreType.DMA((3 * n,)), pltpu.SemaphoreType.DMA((3 * n,))]
        + [pltpu.HBM(p.shape, BF16) for p in parts] + [pltpu.HBM(l.shape, BF16) for l in lands]
        + [jax.ShapeDtypeStruct((8, 128), F32)],
        input_output_aliases={i: i + 2 for i in range(2 * n)},
        compiler_params=pltpu.CompilerParams(has_side_effects=pltpu.SideEffectType.DATAFLOW_SIDE_EFFECTING),
    )(*[pltpu.with_memory_space_constraint(a, pltpu.HBM) for a in list(parts) + lands])
    return res[0], res[1], list(res[2:2 + n]), list(res[2 + n:2 + 2 * n]), res[-1]


def _rs_chip_wait(send, recv, parts, lands, after, tag):
    n = len(parts)

    def body(*refs):
        srcs, dsts = refs[:n], refs[n:2 * n]
        send_ref, recv_ref = refs[2 * n], refs[2 * n + 1]
        x, y, c, peers = _place()
        for i in range(n):
            for k, (px, py) in enumerate(peers):
                cp = pltpu.make_async_remote_copy(
                    src_ref=srcs[i].at[2 * px + py], dst_ref=dsts[i].at[k],
                    send_sem=send_ref.at[3 * i + k], recv_sem=recv_ref.at[3 * i + k],
                    device_id=(px, py, c), device_id_type=MESH)
                cp.wait_send()
                cp.wait_recv()

    res = pl.pallas_call(
        body, name="rs_chip_wait_" + tag,
        in_specs=[HBM] * (2 * n) + [SEM, SEM, pl.BlockSpec(memory_space=pl.ANY)],
        out_specs=[HBM] * (2 * n),
        out_shape=[pltpu.HBM(p.shape, BF16) for p in parts] + [pltpu.HBM(l.shape, BF16) for l in lands],
        input_output_aliases={i: i for i in range(2 * n)},
        compiler_params=pltpu.CompilerParams(has_side_effects=pltpu.SideEffectType.DATAFLOW_SIDE_EFFECTING),
    )(*parts, *lands, send, recv, after)
    return list(res[n:])


def _rs_final_sum(own, got, chip_idx, c_idx, name):
    ns, hr, cols = own.shape
    tr = _half_tile(hr)
    nt = hr // tr

    def body(chip_ref, c_ref, o_ref, g_ref, out_ref):
        s = o_ref[...]
        for k in range(3):
            s = s + g_ref[k].astype(F32)
        out_ref[...] = s

    return pl.pallas_call(
        body, name=name,
        grid_spec=pltpu.PrefetchScalarGridSpec(
            num_scalar_prefetch=2, grid=(nt,),
            in_specs=[pl.BlockSpec((None, tr, cols), lambda t, chip_ref, c_ref: (chip_ref[0], t, 0)),
                      pl.BlockSpec((3, tr, cols), lambda t, chip_ref, c_ref: (0, t, 0))],
            out_specs=pl.BlockSpec((tr, cols), lambda t, chip_ref, c_ref: (c_ref[0] * nt + t, 0))),
        out_shape=jax.ShapeDtypeStruct((2 * hr, cols), F32),
        compiler_params=_params(("parallel",), 48 << 20),
    )(chip_idx, c_idx, own, got)


def _rs_share_halves(fulls, tag):
    n = len(fulls)

    def body(*refs):
        dsts = refs[n:2 * n]
        send, recv = refs[2 * n:]
        x, y, c, _ = _place()
        cps = []
        for i in range(n):
            hr = dsts[i].shape[0] // 2
            rows = dsts[i].at[pl.ds(pl.multiple_of(c * hr, 8), hr), :]
            cp = pltpu.make_async_remote_copy(src_ref=rows, dst_ref=rows, send_sem=send.at[i], recv_sem=recv.at[i],
                                              device_id=(x, y, 1 - c), device_id_type=MESH)
            cp.start()
            cps.append(cp)
        for i in range(n):
            hr = dsts[i].shape[0] // 2
            other = dsts[i].at[pl.ds(pl.multiple_of((1 - c) * hr, 8), hr), :]
            pltpu.make_async_remote_copy(src_ref=other, dst_ref=other, send_sem=send.at[i], recv_sem=recv.at[i],
                                         device_id=(x, y, 1 - c), device_id_type=MESH).wait_recv()
        for cp in cps:
            cp.wait_send()

    return pl.pallas_call(
        body, name="rs_share_halves_" + tag, in_specs=[ANY] * n, out_specs=[ANY] * n,
        out_shape=[jax.ShapeDtypeStruct(f.shape, F32) for f in fulls],
        input_output_aliases={i: i for i in range(n)},
        scratch_shapes=[pltpu.SemaphoreType.DMA((n,))] * 2,
    )(*fulls)


def _all_reduce_small(v):
    rows = v.shape[0]

    def body(v_ref, o_ref, buf, send, recv):
        x, y, c, _ = _place()
        me = 4 * x + 2 * y + c
        buf[me] = v_ref[...]
        cps = []
        for d in range(1, 8):
            px, py, pc = x ^ (d >> 2), y ^ ((d >> 1) & 1), c ^ (d & 1)
            cp = pltpu.make_async_remote_copy(src_ref=v_ref, dst_ref=buf.at[me], send_sem=send.at[d - 1],
                                              recv_sem=recv.at[d - 1], device_id=(px, py, pc), device_id_type=MESH)
            cp.start()
            cps.append(cp)
        for d in range(1, 8):
            px, py, pc = x ^ (d >> 2), y ^ ((d >> 1) & 1), c ^ (d & 1)
            pltpu.make_async_remote_copy(src_ref=v_ref, dst_ref=buf.at[4 * px + 2 * py + pc], send_sem=send.at[d - 1],
                                         recv_sem=recv.at[d - 1], device_id=(px, py, pc),
                                         device_id_type=MESH).wait_recv()
        for cp in cps:
            cp.wait_send()
        acc = buf[0]
        for d in range(1, 8):
            acc = acc + buf[d]
        o_ref[...] = acc

    vm = pl.BlockSpec(memory_space=pltpu.VMEM)
    return pl.pallas_call(
        body, name="all_reduce_small", in_specs=[vm], out_specs=vm,
        out_shape=jax.ShapeDtypeStruct((rows, 128), F32),
        scratch_shapes=[pltpu.VMEM((8, rows, 128), F32), pltpu.SemaphoreType.DMA((7,)), pltpu.SemaphoreType.DMA((7,))],
    )(v)


def _adamw(w, g, m, v, name, g_col_blk=0):
    rows, cols = w.shape
    tr = _divtile(rows, max(8, (2 << 20) // (4 * cols) // 8 * 8), 8)

    def body(w_ref, g_ref, m_ref, v_ref, go_ref, d_ref, mo_ref, vo_ref):
        gv = g_ref[...]
        mn = ADAM_B1 * m_ref[...] + (1.0 - ADAM_B1) * gv
        vn = ADAM_B2 * v_ref[...] + (1.0 - ADAM_B2) * (gv * gv)
        m_hat = mn / (1.0 - ADAM_B1 ** ADAM_STEP)
        v_hat = vn / (1.0 - ADAM_B2 ** ADAM_STEP)
        go_ref[...] = gv
        d_ref[...] = -ADAM_LR * (m_hat / (jnp.sqrt(v_hat) + ADAM_EPS) + ADAM_WD * w_ref[...])
        mo_ref[...] = mn
        vo_ref[...] = vn

    blk = pl.BlockSpec((tr, cols), lambda i: (i, 0))
    return pl.pallas_call(
        body, name=name, grid=(rows // tr,),
        in_specs=[blk, pl.BlockSpec((tr, cols), lambda i: (i, g_col_blk)), blk, blk],
        out_specs=[blk] * 4, out_shape=[jax.ShapeDtypeStruct((rows, cols), F32)] * 4,
        compiler_params=_params(("parallel",), 48 << 20),
    )(w, g, m, v)


SMALL = ["ln1_g", "ln1_b", "conv_w", "conv_b", "dt_bias", "a_log", "d_skip", "ssd_norm_g", "attn_sinks",
         "ln2_g", "ln2_b", "ln3_g", "ln3_b"]


def _pack_rows(vs):
    parts = []
    for v in vs:
        v = v.reshape(-1)
        parts.append(jnp.pad(v, (0, (-v.shape[0]) % 128)))
    flat = jnp.concatenate(parts)
    flat = jnp.pad(flat, (0, (-flat.shape[0]) % 1024))
    return flat.reshape(-1, 128)


def _unpack_rows(packed, shapes):
    flat = packed.reshape(-1)
    out, at = [], 0
    for s in shapes:
        nel = int(np.prod(s))
        out.append(flat[at:at + nel].reshape(s))
        at += nel + (-nel) % 128
    return out


def kernel(x, positions, ffn1_w_gate, ffn1_w_up, ffn1_w_down, ln1_g, ln1_b, w_in, conv_w, conv_b, dt_bias, a_log, d_skip, ssd_norm_g, w_ssd_o, attn_sinks, w_attn_o, w_out, ln2_g, ln2_b, ffn2_w_gate, ffn2_w_up, ffn2_w_down, ln3_g, ln3_b, loss_target, m_ffn1_w_gate, m_ffn1_w_up, m_ffn1_w_down, m_ln1_g, m_ln1_b, m_w_in, m_conv_w, m_conv_b, m_dt_bias, m_a_log, m_d_skip, m_ssd_norm_g, m_w_ssd_o, m_attn_sinks, m_w_attn_o, m_w_out, m_ln2_g, m_ln2_b, m_ffn2_w_gate, m_ffn2_w_up, m_ffn2_w_down, m_ln3_g, m_ln3_b, v_ffn1_w_gate, v_ffn1_w_up, v_ffn1_w_down, v_ln1_g, v_ln1_b, v_w_in, v_conv_w, v_conv_b, v_dt_bias, v_a_log, v_d_skip, v_ssd_norm_g, v_w_ssd_o, v_attn_sinks, v_w_attn_o, v_w_out, v_ln2_g, v_ln2_b, v_ffn2_w_gate, v_ffn2_w_up, v_ffn2_w_down, v_ln3_g, v_ln3_b):
    args = dict(locals())
    wts = {n: args[n][0] for n in [b[0] for b in BIG] + SMALL}
    mom_m = {n: args["m_" + n][0] for n in wts}
    mom_v = {n: args["v_" + n][0] for n in wts}
    t = x.shape[1]
    xi, yi, ci = lax.axis_index("x"), lax.axis_index("y"), lax.axis_index("c")
    chip = 2 * xi + yi

    c_idx = ci.astype(jnp.int32).reshape(1)
    chip_idx = chip.astype(jnp.int32).reshape(1)
    placed = {o: _cast_place([wts[b[0]] for b in BIG if b[3] == o], o, chip_idx, also_alone=(o == "gu1"))
              for o in GATHERED}
    placed["gu1"], gu1_own = placed["gu1"]
    g_sems, g_flight = {}, {}

    def fetch(group, after):
        names = GATHER_GROUPS[group]
        send, recv = g_sems[group]
        landed = _gather_ici_wait(send, recv, {k: g_flight[k] for k in names}, names, after, str(group))
        got = _gather_d2d(landed, names, str(group))
        if "win4" in got:
            got["win"] = _win_to_internal(got.pop("win4"))
        return got

    sems, arrays, gu1_own = _gather_ici_start(placed, GATHER_GROUPS, "all", gu1_own)
    g_sems.update(dict(enumerate(sems)))
    g_flight.update(arrays)
    w = {"gu1_own": gu1_own, "chip_idx": chip_idx}

    def slabs_of(gw, names):
        view = {"gu1": lambda: gw["gu1"], "gu2": lambda: gw["gu2"],
                "d1": lambda: gw["d1"].reshape(N_CHIPS, SHARD_H, D), "d2": lambda: gw["d2"].reshape(N_CHIPS, SHARD_H, D),
                "win": lambda: _win_from_internal(gw["win"]),
                "so": lambda: gw["so"].reshape(N_CHIPS, SSD_INNER // N_CHIPS, D),
                "ao": lambda: gw["ao"].reshape(N_CHIPS, D // N_CHIPS, D),
                "out": lambda: gw["out"].reshape(N_CHIPS, D // N_CHIPS, D)}
        return [view[nm]() for nm in names]

    early = ["win", "so", "ao", "out", "gu2", "d2"]
    late = ["gu1", "d1"]
    flight = {}

    def early_start(gw, win):
        flight["pair"], win = _rs_pair_start(slabs_of(gw, early), win, "early")
        return win

    def late_start(gw, gu1w):
        flight["pair_late"], gu1w = _rs_pair_start(slabs_of(gw, late), gu1w, "late")
        return gu1w

    def early_mid(dh1):
        slabs, from_sib = _rs_pair_wait(*flight["pair"], dh1, "early")
        pair = [_rs_pair_sum(g, r, c_idx, "rs_pair_sum_" + nm) for g, r, nm in zip(slabs, from_sib, early)]
        send, recv, parts, lands, token = _rs_chip_start([p[0] for p in pair], "early")
        flight.update(send=send, recv=recv, parts=parts, lands=lands, own=[p[1] for p in pair])
        return token

    early_grads = (early_start, early_mid, late_start)
    cw_rows = _pack_rows([lax.dynamic_update_slice(jnp.zeros((4, XBC), F32), wts["conv_w"], (0, chip * (XBC // N_CHIPS)))])
    cw_rows = jnp.where(ci == 0, cw_rows, 0.0)
    conv_w_full = _all_reduce_small(cw_rows)[:4 * XBC // 128].reshape(4, XBC)

    small = {n: (wts[n][None, :] if wts[n].ndim == 1 else wts[n]) for n in SMALL}
    small["conv_w"] = conv_w_full
    loss, grad_x, gw, gs = _local_step(x[0], positions[0].astype(F32)[:, None], loss_target[0], w, small,
                                       fetch=fetch, early_grads=early_grads)

    gvec = {n: gs[n] for n in SMALL}
    gvec["dt_bias"], gvec["a_log"], gvec["d_skip"] = gs["dt_bias"][:, :64], gs["a_log"][:, :64], gs["d_skip"][:, :64]
    gvec["attn_sinks"] = gs["attn_sinks"][:, :NQ]
    red = _all_reduce_small(_pack_rows([gvec[n] for n in SMALL] + [loss, grad_x[:1, :128]]))
    slabs, from_sib = _rs_pair_wait(*flight["pair_late"], red, "late")
    pair = [_rs_pair_sum(g, r, c_idx, "rs_pair_sum_" + nm) for g, r, nm in zip(slabs, from_sib, late)]
    l_send, l_recv, l_parts, l_lands, l_token = _rs_chip_start([p[0] for p in pair], "late")
    got_early = _rs_chip_wait(flight["send"], flight["recv"], flight["parts"], flight["lands"], l_token, "early")

    outs = {}
    big_src = {"ffn1_w_gate": ("gu1", 0), "ffn1_w_up": ("gu1", 1), "ffn1_w_down": ("d1", 0), "w_in": ("win", 0),
               "w_ssd_o": ("so", 0), "w_attn_o": ("ao", 0), "w_out": ("out", 0),
               "ffn2_w_gate": ("gu2", 0), "ffn2_w_up": ("gu2", 1), "ffn2_w_down": ("d2", 0)}

    def finish(names, own, got, tag):
        halves = [_rs_final_sum(o, gt, chip_idx, c_idx, "rs_final_sum_" + nm) for o, gt, nm in zip(own, got, names)]
        full = dict(zip(names, _rs_share_halves(halves, tag)))
        for nm, (src, blk) in big_src.items():
            if src in full:
                outs[nm] = _adamw(wts[nm], full[src], mom_m[nm], mom_v[nm], "adamw_" + nm, g_col_blk=blk)

    finish(early, flight["own"], got_early, "early")
    early_done = sum(o[1][:1, :1] for o in outs.values())
    got_late = _rs_chip_wait(l_send, l_recv, l_parts, l_lands, early_done, "late")
    finish(late, [p[1] for p in pair], got_late, "late")

    shapes = [(4, XBC) if n == "conv_w" else wts[n].shape for n in SMALL] + [(1,)]
    red_list = _unpack_rows(red, shapes)
    loss_out = red_list[-1].reshape(())
    gsm = dict(zip(SMALL, red_list[:-1]))
    gsm["conv_w"] = lax.dynamic_slice_in_dim(gsm["conv_w"], chip * (XBC // N_CHIPS), XBC // N_CHIPS, axis=1)
    sm_shapes = [wts[n].shape for n in SMALL]
    res = _adamw(_pack_rows([wts[n] for n in SMALL]), _pack_rows([gsm[n] for n in SMALL]),
                 _pack_rows([mom_m[n] for n in SMALL]), _pack_rows([mom_v[n] for n in SMALL]), "adamw_small")
    res = [_unpack_rows(r, sm_shapes) for r in res]
    for i, nm in enumerate(SMALL):
        outs[nm] = tuple(r[i] for r in res)

    order = ["ffn1_w_gate", "ffn1_w_up", "ffn1_w_down", "ln1_g", "ln1_b", "w_in", "conv_w", "conv_b", "dt_bias", "a_log",
             "d_skip", "ssd_norm_g", "w_ssd_o", "attn_sinks", "w_attn_o", "w_out", "ln2_g", "ln2_b",
             "ffn2_w_gate", "ffn2_w_up", "ffn2_w_down", "ln3_g", "ln3_b"]
    result = [loss_out, grad_x[None]]
    for kind in range(4):
        result += [outs[nm][kind][None] for nm in order]
    return tuple(result)
```

```python
import functools
import math

import numpy as np
import jax
import jax.numpy as jnp
from jax import lax
from jax.experimental import pallas as pl
from jax.experimental.pallas import tpu as pltpu

F32 = jnp.float32
BF16 = jnp.bfloat16
HI = lax.Precision.HIGHEST

D = 2048
FFN_H = 5632
SSD_INNER = 4096
SSD_HEADS = 64
SSD_P = 64
SSD_G = 8
SSD_R = 8
SSD_N = 128
CHUNK = 128
XBC = 6144
NQ = 32
NKV = 4
HD = 64
QW = 2048
KVW = 256
WINDOW = 128
ROPE_THETA = 10000.0
ALPHA = 2.0 ** 0.25
LN_EPS = 1e-5
RMS_EPS = 1e-5
PROJ_W = 16960
N_CHIPS = 4
SHARD_IN = PROJ_W // N_CHIPS
SHARD_H = FFN_H // N_CHIPS

SEGS = {
    "z": (0, 4096, 0),
    "xbc": (4096, 6144, 10240),
    "dt": (10240, 64, 16896),
    "q": (10304, 2048, 8192),
    "k": (12352, 256, 16384),
    "v": (12608, 256, 16640),
    "gs": (12864, 2048, 4096),
    "ga": (14912, 2048, 6144),
}
PROJ_PAD = 17024

ADAM_LR = 0.001
ADAM_B1 = 0.9
ADAM_B2 = 0.999
ADAM_EPS = 1e-08
ADAM_WD = 0.01
ADAM_STEP = 10

VMEM_CAP = 60 * 1024 * 1024


def _params(sem, vmem_bytes):
    return pltpu.CompilerParams(dimension_semantics=sem, vmem_limit_bytes=int(min(VMEM_CAP, vmem_bytes)))


def _divtile(n, cap, q=128):
    best = None
    for d in range(q, min(n, cap) + 1, q):
        if n % d == 0:
            best = d
    return n if best is None else best


def _sigmoid(x):
    return 0.5 * jnp.tanh(0.5 * x) + 0.5


def _mm(a, b, mode, out_dtype, name, add=None, add_scale=1.0, caps=(1024, 1024, 2048), n_slabs=1):
    if mode == "nn":
        (m, k), (k2, n) = a.shape, b.shape
    elif mode == "nt":
        (m, k), (n, k2) = a.shape, b.shape
    else:
        (k, m), (k2, n) = a.shape, b.shape
    assert k == k2, (a.shape, b.shape, mode)
    tm, tn, tk = _divtile(m, caps[0]), _divtile(n // n_slabs, caps[1]), _divtile(k, caps[2])
    nk = k // tk
    per_slab = n // n_slabs // tn
    dims = {"nn": ((1,), (0,)), "nt": ((1,), (1,)), "tn": ((0,), (0,))}[mode]
    has_add = add is not None

    def body(*refs):
        if has_add:
            a_ref, b_ref, add_ref, o_ref = refs[:4]
            scr = refs[4:]
        else:
            a_ref, b_ref, o_ref = refs[:3]
            add_ref = None
            scr = refs[3:]
        part = lax.dot_general(a_ref[...].astype(BF16), b_ref[...].astype(BF16), (dims, ((), ())),
                               preferred_element_type=F32)

        def finish(acc):
            if has_add:
                acc = acc + add_scale * add_ref[...].astype(F32)
            o_ref[...] = acc.astype(o_ref.dtype)

        if nk == 1:
            finish(part)
        else:
            acc_ref = scr[0]
            kk = pl.program_id(2)

            @pl.when(kk == 0)
            def _():
                acc_ref[...] = part

            @pl.when(kk > 0)
            def _():
                acc_ref[...] += part

            @pl.when(kk == nk - 1)
            def _():
                finish(acc_ref[...])

    if mode == "nn":
        a_spec = pl.BlockSpec((tm, tk), lambda i, j, kk: (i, kk))
        b_spec = pl.BlockSpec((tk, tn), lambda i, j, kk: (kk, j))
    elif mode == "nt":
        a_spec = pl.BlockSpec((tm, tk), lambda i, j, kk: (i, kk))
        b_spec = pl.BlockSpec((tn, tk), lambda i, j, kk: (j, kk))
    else:
        a_spec = pl.BlockSpec((tk, tm), lambda i, j, kk: (kk, i))
        b_spec = pl.BlockSpec((tk, tn), lambda i, j, kk: (kk, j))
    o_spec = pl.BlockSpec((tm, tn), lambda i, j, kk: (i, j))
    out_shape = jax.ShapeDtypeStruct((m, n), out_dtype)
    if n_slabs > 1:
        assert not has_add
        o_spec = pl.BlockSpec((None, tm, tn), lambda i, j, kk: (j // per_slab, i, j % per_slab))
        out_shape = jax.ShapeDtypeStruct((n_slabs, m, n // n_slabs), out_dtype)
    in_specs = [a_spec, b_spec] + ([o_spec] if has_add else [])
    args = (a, b) + ((add,) if has_add else ())
    osz = jnp.dtype(out_dtype).itemsize
    vmem = (2 * (tm * tk * a.dtype.itemsize + tk * tn * b.dtype.itemsize) + 2 * tm * tn * osz
            + (2 * tm * tn * add.dtype.itemsize if has_add else 0) + 2 * tm * tn * 4
            + 2 * (tm * tk + tk * tn) + (8 << 20))
    return pl.pallas_call(
        body, name=name, grid=(m // tm, n // tn, nk),
        in_specs=in_specs, out_specs=o_spec, out_shape=out_shape,
        scratch_shapes=[pltpu.VMEM((tm, tn), F32)] if nk > 1 else [],
        compiler_params=_params(("parallel", "parallel", "arbitrary"), vmem),
    )(*args)


def _mm_swiglu(a, b, name, chip_idx=None, done=None):
    m, k = a.shape
    w = SHARD_H
    tm = _divtile(m, 512)

    def body(*refs):
        a_ref, b_ref = refs[-4 if done is None else -6], refs[-3 if done is None else -5]
        gu_ref, act_ref = refs[-2:]
        gu = jnp.dot(a_ref[...], b_ref[...], preferred_element_type=F32)
        g = gu[:, :w]
        gu_ref[...] = gu.astype(BF16)
        act_ref[...] = (g * _sigmoid(g) * gu[:, w:]).astype(BF16)

    out_shape = [jax.ShapeDtypeStruct((m, 2 * FFN_H), BF16), jax.ShapeDtypeStruct((m, FFN_H), BF16)]
    cp = _params(("parallel", "parallel"), 56 << 20)
    if chip_idx is None:
        return pl.pallas_call(
            body, name=name, grid=(N_CHIPS, m // tm),
            in_specs=[pl.BlockSpec((tm, k), lambda j, i: (i, 0)), pl.BlockSpec((k, 2 * w), lambda j, i: (0, j))],
            out_specs=[pl.BlockSpec((tm, 2 * w), lambda j, i: (i, j)), pl.BlockSpec((tm, w), lambda j, i: (i, j))],
            out_shape=out_shape, compiler_params=cp,
        )(a, b)
    first = done is None
    tile = (lambda j, c: c[0]) if first else (lambda j, c: (c[0] + 1 + j) % N_CHIPS)
    in_specs = [pl.BlockSpec((tm, k), lambda j, i, c: (i, 0)),
                pl.BlockSpec((k, 2 * w), (lambda j, i, c: (0, 0)) if first else (lambda j, i, c: (0, tile(j, c))))]
    return pl.pallas_call(
        body, name=name,
        grid_spec=pltpu.PrefetchScalarGridSpec(
            num_scalar_prefetch=1, grid=(1 if first else N_CHIPS - 1, m // tm),
            in_specs=in_specs + ([] if first else [pl.BlockSpec(memory_space=pl.ANY)] * 2),
            out_specs=[pl.BlockSpec((tm, 2 * w), lambda j, i, c: (i, tile(j, c))),
                       pl.BlockSpec((tm, w), lambda j, i, c: (i, tile(j, c)))]),
        out_shape=out_shape, compiler_params=cp,
        input_output_aliases={} if first else {3: 0, 4: 1},
    )(chip_idx, a, b, *(() if first else done))


def _mm_swiglu_bwd(dr, wd, gu, name):
    m, k = dr.shape
    w = SHARD_H
    tm = _divtile(m, 512)

    def body(dr_ref, wd_ref, gu_ref, o_ref):
        d = lax.dot_general(dr_ref[...], wd_ref[...], NT_DIMS, preferred_element_type=F32)
        g = gu_ref[:, :w].astype(F32)
        u = gu_ref[:, w:].astype(F32)
        s = _sigmoid(g)
        o_ref[:, :w] = (d * u * (s * (1.0 + g * (1.0 - s)))).astype(BF16)
        o_ref[:, w:] = (d * (g * s)).astype(BF16)

    return pl.pallas_call(
        body, name=name, grid=(N_CHIPS, m // tm),
        in_specs=[pl.BlockSpec((tm, k), lambda j, i: (i, 0)), pl.BlockSpec((w, k), lambda j, i: (j, 0)),
                  pl.BlockSpec((tm, 2 * w), lambda j, i: (i, j))],
        out_specs=pl.BlockSpec((tm, 2 * w), lambda j, i: (i, j)),
        out_shape=jax.ShapeDtypeStruct((m, 2 * FFN_H), BF16),
        compiler_params=_params(("parallel", "parallel"), 48 << 20),
    )(dr, wd, gu)


def _ln_fwd(base, f, g, b, c, name, target=None):
    t = base.shape[0]
    tt = _divtile(t, 256)
    with_loss = target is not None

    def body(*refs):
        if with_loss:
            base_ref, f_ref, g_ref, b_ref, tg_ref, h_ref, hb_ref, xh_ref, rs_ref, dh_ref, loss_ref = refs
        else:
            base_ref, f_ref, g_ref, b_ref, h_ref, hb_ref, xh_ref, rs_ref = refs
        r = ALPHA * base_ref[...] + c * f_ref[...]
        mu = jnp.mean(r, axis=-1, keepdims=True)
        xc = r - mu
        var = jnp.mean(xc * xc, axis=-1, keepdims=True)
        rstd = lax.rsqrt(var + LN_EPS)
        xh = xc * rstd
        h = xh * g_ref[...] + b_ref[...]
        h_ref[...] = h
        hb_ref[...] = h.astype(BF16)
        xh_ref[...] = xh
        rs_ref[...] = rstd
        if with_loss:
            e = h - tg_ref[...]
            dh_ref[...] = e * (1.0 / D)
            part = 0.5 * jnp.sum(jnp.sum(e * e, axis=-1, keepdims=True) * (1.0 / D), axis=0, keepdims=True)

            @pl.when(pl.program_id(0) == 0)
            def _():
                loss_ref[...] = jnp.zeros_like(loss_ref)

            loss_ref[...] += part

    row = pl.BlockSpec((tt, D), lambda i: (i, 0))
    vec = pl.BlockSpec((1, D), lambda i: (0, 0))
    col = pl.BlockSpec((tt, 1), lambda i: (i, 0))
    in_specs = [row, row, vec, vec] + ([row] if with_loss else [])
    out_specs = [row, row, row, col] + ([row, pl.BlockSpec((1, 1), lambda i: (0, 0))] if with_loss else [])
    out_shape = [jax.ShapeDtypeStruct((t, D), F32), jax.ShapeDtypeStruct((t, D), BF16),
                 jax.ShapeDtypeStruct((t, D), F32), jax.ShapeDtypeStruct((t, 1), F32)]
    if with_loss:
        out_shape += [jax.ShapeDtypeStruct((t, D), F32), jax.ShapeDtypeStruct((1, 1), F32)]
    args = (base, f, g, b) + ((target,) if with_loss else ())
    return pl.pallas_call(
        body, name=name, grid=(t // tt,), in_specs=in_specs, out_specs=out_specs, out_shape=out_shape,
        compiler_params=_params(("arbitrary",) if with_loss else ("parallel",), 48 << 20),
    )(*args)


def _mm_ln(a, b, base, g, bvec, c, name):
    m, k = a.shape
    tm, tk = _divtile(m, 512), _divtile(k, 1408)
    nk = k // tk

    def body(a_ref, b_ref, base_ref, g_ref, bv_ref, h_ref, hb_ref, xh_ref, rs_ref, acc_ref):
        kk = pl.program_id(1)
        part = jnp.dot(a_ref[...], b_ref[...], preferred_element_type=F32)

        @pl.when(kk == 0)
        def _():
            acc_ref[...] = part

        @pl.when(kk > 0)
        def _():
            acc_ref[...] += part

        @pl.when(kk == nk - 1)
        def _():
            r = ALPHA * base_ref[...] + c * acc_ref[...]
            mu = jnp.mean(r, axis=-1, keepdims=True)
            xc = r - mu
            rstd = lax.rsqrt(jnp.mean(xc * xc, axis=-1, keepdims=True) + LN_EPS)
            xh = xc * rstd
            h = xh * g_ref[...] + bv_ref[...]
            h_ref[...] = h
            hb_ref[...] = h.astype(BF16)
            xh_ref[...] = xh
            rs_ref[...] = rstd

    row = pl.BlockSpec((tm, D), lambda i, kk: (i, 0))
    vec = pl.BlockSpec((1, D), lambda i, kk: (0, 0))
    return pl.pallas_call(
        body, name=name, grid=(m // tm, nk),
        in_specs=[pl.BlockSpec((tm, tk), lambda i, kk: (i, kk)), pl.BlockSpec((tk, D), lambda i, kk: (kk, 0)),
                  row, vec, vec],
        out_specs=[row, row, row, pl.BlockSpec((tm, 1), lambda i, kk: (i, 0))],
        out_shape=[jax.ShapeDtypeStruct((m, D), F32), jax.ShapeDtypeStruct((m, D), BF16),
                   jax.ShapeDtypeStruct((m, D), F32), jax.ShapeDtypeStruct((m, 1), F32)],
        scratch_shapes=[pltpu.VMEM((tm, D), F32)],
        compiler_params=_params(("parallel", "arbitrary"), 56 << 20),
    )(a, b, base, g, bvec)


def _ln_bwd(dy, xh, rstd, g, c, name):
    t = dy.shape[0]
    tt = _divtile(t, 256)

    def body(dy_ref, xh_ref, rs_ref, g_ref, dr_ref, drb_ref, dg_ref, db_ref):
        dyv = dy_ref[...]
        xhv = xh_ref[...]
        dxh = dyv * g_ref[...]
        m1 = jnp.mean(dxh, axis=-1, keepdims=True)
        m2 = jnp.mean(dxh * xhv, axis=-1, keepdims=True)
        dr = rs_ref[...] * (dxh - m1 - xhv * m2)
        dr_ref[...] = dr
        drb_ref[...] = (c * dr).astype(BF16)

        @pl.when(pl.program_id(0) == 0)
        def _():
            dg_ref[...] = jnp.zeros_like(dg_ref)
            db_ref[...] = jnp.zeros_like(db_ref)

        dg_ref[...] += jnp.sum(dyv * xhv, axis=0, keepdims=True)
        db_ref[...] += jnp.sum(dyv, axis=0, keepdims=True)

    row = pl.BlockSpec((tt, D), lambda i: (i, 0))
    vec = pl.BlockSpec((1, D), lambda i: (0, 0))
    col = pl.BlockSpec((tt, 1), lambda i: (i, 0))
    return pl.pallas_call(
        body, name=name, grid=(t // tt,), in_specs=[row, row, col, vec], out_specs=[row, row, vec, vec],
        out_shape=[jax.ShapeDtypeStruct((t, D), F32), jax.ShapeDtypeStruct((t, D), BF16),
                   jax.ShapeDtypeStruct((1, D), F32), jax.ShapeDtypeStruct((1, D), F32)],
        compiler_params=_params(("arbitrary",), 40 << 20),
    )(dy, xh, rstd, g)


DT_BLK = SEGS["dt"][2] // 128


def _dt_prep(proj, bias128, alog128):
    t = proj.shape[0]
    tt = _divtile(t, 256)

    def body(p_ref, bias_ref, alog_ref, dt_ref, adt_ref):
        dtv = jax.nn.softplus(p_ref[...] + bias_ref[...])
        adt = dtv * (-jnp.exp(alog_ref[...]))
        for g in range(SSD_G):
            dt_ref[g] = dtv[:, SSD_R * g:SSD_R * (g + 1)]
            adt_ref[g] = adt[:, SSD_R * g:SSD_R * (g + 1)]

    blk3 = pl.BlockSpec((SSD_G, tt, SSD_R), lambda i: (0, i, 0))
    vec = pl.BlockSpec((1, 128), lambda i: (0, 0))
    return pl.pallas_call(
        body, name="dt_prep", grid=(t // tt,),
        in_specs=[pl.BlockSpec((tt, 128), lambda i: (i, DT_BLK)), vec, vec], out_specs=[blk3, blk3],
        out_shape=[jax.ShapeDtypeStruct((SSD_G, t, SSD_R), F32)] * 2,
        compiler_params=_params(("parallel",), 16 << 20),
    )(proj, bias128, alog128)


def _dt_bwd(dadt3, dxdx3, proj, bias128, alog128):
    t = proj.shape[0]
    tt = _divtile(t, 256)

    def token_major(ref):
        parts = [ref[g] for g in range(SSD_G)] + [jnp.zeros((tt, 128 - SSD_HEADS), F32)]
        return jnp.concatenate(parts, axis=1)

    def body(dadt_ref, dxdx_ref, p_ref, bias_ref, alog_ref, o_ref, dbias_ref, dalog_ref):
        pre = p_ref[...] + bias_ref[...]
        dtv = jax.nn.softplus(pre)
        a = -jnp.exp(alog_ref[...])
        dadt = token_major(dadt_ref)
        ddt = a * dadt + token_major(dxdx_ref)
        draw = ddt * _sigmoid(pre)
        o_ref[...] = draw.astype(BF16)

        @pl.when(pl.program_id(0) == 0)
        def _():
            dbias_ref[...] = jnp.zeros_like(dbias_ref)
            dalog_ref[...] = jnp.zeros_like(dalog_ref)

        dbias_ref[...] += jnp.sum(draw, axis=0, keepdims=True)
        dalog_ref[...] += jnp.sum(dadt * dtv * a, axis=0, keepdims=True)

    blk = pl.BlockSpec((tt, 128), lambda i: (i, 0))
    blk3 = pl.BlockSpec((SSD_G, tt, SSD_R), lambda i: (0, i, 0))
    vec = pl.BlockSpec((1, 128), lambda i: (0, 0))
    return pl.pallas_call(
        body, name="dt_bwd", grid=(t // tt,),
        in_specs=[blk3, blk3, pl.BlockSpec((tt, 128), lambda i: (i, DT_BLK)), vec, vec],
        out_specs=[blk, vec, vec],
        out_shape=[jax.ShapeDtypeStruct((t, 128), BF16), jax.ShapeDtypeStruct((1, 128), F32),
                   jax.ShapeDtypeStruct((1, 128), F32)],
        compiler_params=_params(("arbitrary",), 16 << 20),
    )(dadt3, dxdx3, proj, bias128, alog128)


CONV_CB = 128
CONV_TT = 2048


def _shift_down(cur, prev8, s):
    if s == 0:
        return cur
    rolled = pltpu.roll(cur, s, 0)
    head = pltpu.roll(prev8, s, 0)
    r8 = lax.broadcasted_iota(jnp.int32, (8, 1), 0)
    top = jnp.where(r8 < s, head, rolled[:8])
    return jnp.concatenate([top, rolled[8:]], axis=0)


def _shift_up(cur, next8, s):
    if s == 0:
        return cur
    n = cur.shape[0]
    rolled = pltpu.roll(cur, n - s, 0)
    tail = pltpu.roll(next8, 8 - s, 0)
    r8 = lax.broadcasted_iota(jnp.int32, (8, 1), 0)
    bot = jnp.where(r8 >= 8 - s, tail, rolled[n - 8:])
    return jnp.concatenate([rolled[:n - 8], bot], axis=0)


def _conv_fwd(proj, conv_w, conv_b):
    t = proj.shape[0]
    tt = _divtile(t, CONV_TT)
    base = SEGS["xbc"][2] // CONV_CB
    r8 = tt // 8

    def body(u_ref, up_ref, w_ref, b_ref, o_ref):
        cur = u_ref[...]
        prev8 = jnp.where(pl.program_id(1) > 0, up_ref[...], 0.0)
        acc = b_ref[...] + w_ref[3:4, :] * cur
        for k in range(3):
            acc = acc + w_ref[k:k + 1, :] * _shift_down(cur, prev8, 3 - k)
        o_ref[...] = acc * _sigmoid(acc)

    return pl.pallas_call(
        body, name="conv_fwd", grid=(XBC // CONV_CB, t // tt),
        in_specs=[pl.BlockSpec((tt, CONV_CB), lambda c, i: (i, base + c)),
                  pl.BlockSpec((8, CONV_CB), lambda c, i: (jnp.maximum(i * r8 - 1, 0), base + c)),
                  pl.BlockSpec((4, CONV_CB), lambda c, i: (0, c)),
                  pl.BlockSpec((1, CONV_CB), lambda c, i: (0, c))],
        out_specs=pl.BlockSpec((tt, CONV_CB), lambda c, i: (i, c)),
        out_shape=jax.ShapeDtypeStruct((t, XBC), F32),
        compiler_params=_params(("parallel", "parallel"), 24 << 20),
    )(proj, proj, conv_w, conv_b)


def _conv_bwd(proj, dout, conv_w, conv_b, col0, width, name, dproj, skip=None):
    t = proj.shape[0]
    tt = _divtile(t, CONV_TT)
    nt = t // tt
    base = SEGS["xbc"][2] // CONV_CB + col0 // CONV_CB
    wb = col0 // CONV_CB
    r8 = tt // 8
    has_skip = skip is not None

    def body(*refs):
        if has_skip:
            u_ref, up_ref, d_ref, w_ref, b_ref, _, sk_ref, skw_ref, du_ref, dw_ref, db_ref, nx_ref = refs
        else:
            u_ref, up_ref, d_ref, w_ref, b_ref, _, du_ref, dw_ref, db_ref, nx_ref = refs
        i = pl.program_id(1)
        cur = u_ref[...]
        prev8 = jnp.where(i < nt - 1, up_ref[...], 0.0)
        sh = [_shift_down(cur, prev8, 3 - k) for k in range(3)] + [cur]
        pre = b_ref[...]
        for k in range(4):
            pre = pre + w_ref[k:k + 1, :] * sh[k]
        sg = _sigmoid(pre)
        dout_v = d_ref[...]
        if has_skip:
            dout_v = dout_v + sk_ref[...] * skw_ref[...]
        dpre = dout_v * (sg * (1.0 + pre * (1.0 - sg)))

        @pl.when(i == 0)
        def _():
            nx_ref[...] = jnp.zeros_like(nx_ref)
            dw_ref[...] = jnp.zeros_like(dw_ref)
            db_ref[...] = jnp.zeros_like(db_ref)

        next8 = nx_ref[...]
        du = w_ref[3:4, :] * dpre
        for s in range(1, 4):
            du = du + w_ref[3 - s:4 - s, :] * _shift_up(dpre, next8, s)
        du_ref[...] = du.astype(BF16)
        nx_ref[...] = dpre[:8]
        rows = [jnp.sum(dpre * sh[k], axis=0, keepdims=True) for k in range(4)]
        dw_ref[...] += jnp.concatenate(rows + [jnp.zeros((4, CONV_CB), F32)], axis=0)
        db_ref[...] += jnp.sum(dpre, axis=0, keepdims=True)

    rev = lambda c, i: (nt - 1 - i, c)
    p_in, p_out, p_shape = _dproj_piece(t, tt, CONV_CB, lambda c, i: (nt - 1 - i, base + c))
    in_specs = [pl.BlockSpec((tt, CONV_CB), lambda c, i: (nt - 1 - i, base + c)),
                pl.BlockSpec((8, CONV_CB), lambda c, i: (jnp.maximum((nt - 1 - i) * r8 - 1, 0), base + c)),
                pl.BlockSpec((tt, CONV_CB), rev),
                pl.BlockSpec((4, CONV_CB), lambda c, i: (0, wb + c)),
                pl.BlockSpec((1, CONV_CB), lambda c, i: (0, wb + c)), p_in]
    args = [proj, proj, dout, conv_w, conv_b, dproj]
    if has_skip:
        in_specs += [pl.BlockSpec((tt, CONV_CB), rev), pl.BlockSpec((1, CONV_CB), lambda c, i: (0, c))]
        args += [skip[0], skip[1]]
    return pl.pallas_call(
        body, name=name, grid=(width // CONV_CB, nt),
        in_specs=in_specs,
        out_specs=[p_out, pl.BlockSpec((8, CONV_CB), lambda c, i: (0, c)),
                   pl.BlockSpec((1, CONV_CB), lambda c, i: (0, c))],
        out_shape=[p_shape, jax.ShapeDtypeStruct((8, width), F32), jax.ShapeDtypeStruct((1, width), F32)],
        input_output_aliases={5: 0},
        scratch_shapes=[pltpu.VMEM((8, CONV_CB), F32)],
        compiler_params=_params(("parallel", "arbitrary"), 32 << 20),
    )(*args)


GW = SSD_R * SSD_P


def _expand8(v, passes=2):
    r = v.shape[0]
    if r < 8:
        v = jnp.broadcast_to(v, (8, SSD_R))
    ri = lax.broadcasted_iota(jnp.int32, (SSD_R, GW), 0)
    ci = lax.broadcasted_iota(jnp.int32, (SSD_R, GW), 1)
    spread = jnp.where((ci >= ri * SSD_P) & (ci < (ri + 1) * SSD_P), 1.0, 0.0)
    return _dot01(v, spread, passes)[:r]


def _head_pair_split(tile):
    first = lax.broadcasted_iota(jnp.int32, (1, 2 * SSD_P), 1) < SSD_P
    return jnp.where(first, tile, 0.0), jnp.where(first, 0.0, tile)


def _sel(rows, group):
    ri = lax.broadcasted_iota(jnp.int32, (rows, rows // group), 0)
    ci = lax.broadcasted_iota(jnp.int32, (rows, rows // group), 1)
    lo = ci * group
    return jnp.where((ri >= lo) & (ri < lo + group), 1.0, 0.0).astype(F32)


def _dot01(lhs, rhs, passes, split_lhs=True, dims=((1,), (0,))):
    val, m01 = (lhs, rhs) if split_lhs else (rhs, lhs)
    m01 = m01.astype(BF16)
    out = None
    for p in range(passes):
        piece = val.astype(BF16)
        ops = (piece, m01) if split_lhs else (m01, piece)
        d = lax.dot_general(ops[0], ops[1], (dims, ((), ())), preferred_element_type=F32)
        out = d if out is None else out + d
        if p + 1 < passes:
            val = val - piece.astype(F32)
    return out


def _ssd_chunk_terms(adt):
    li = lax.broadcasted_iota(jnp.int32, (CHUNK, CHUNK), 0)
    si = lax.broadcasted_iota(jnp.int32, (CHUNK, CHUNK), 1)
    causal = li >= si
    a_cs = _dot01(jnp.where(causal, 1.0, 0.0), adt, 3, split_lhs=False)
    a_cs_t = _dot01(adt, jnp.where(li <= si, 1.0, 0.0), 3, dims=((0,), (0,)))
    return a_cs, a_cs_t, causal


def _ssd_fwd(xc, dt3, adt3):
    t = xc.shape[0]
    nc = t // CHUNK

    gs = 2

    def body(xs_ref, b_ref, c_ref, dt_ref, adt_ref, y_ref, hp_ref, h_ref):
        @pl.when(pl.program_id(1) == 0)
        def _():
            h_ref[...] = jnp.zeros_like(h_ref)

        for gg in range(gs):
            a_cs, a_cs_t, causal = _ssd_chunk_terms(adt_ref[gg])
            a_last = a_cs[CHUNK - 1:CHUNK, :]
            h = h_ref[gg]
            hp_ref[gg, 0] = h
            xd = xs_ref[:, GW * gg:GW * (gg + 1)] * _expand8(dt_ref[gg])
            bb = b_ref[:, SSD_N * gg:SSD_N * (gg + 1)].astype(BF16)
            cbf = c_ref[:, SSD_N * gg:SSD_N * (gg + 1)].astype(BF16)
            cb = lax.dot_general(cbf, bb, (((1,), (1,)), ((), ())), preferred_element_type=F32)
            yoff = jnp.dot(cbf, h.astype(BF16), preferred_element_type=F32) * _expand8(jnp.exp(a_cs))
            for q in range(SSD_R // 2):
                lmats = []
                for r in (2 * q, 2 * q + 1):
                    seg = jnp.exp(jnp.where(causal, a_cs[:, r:r + 1] - a_cs_t[r:r + 1, :], -jnp.inf))
                    lmats.append((cb * seg).astype(BF16))
                tile = slice(2 * SSD_P * q, 2 * SSD_P * (q + 1))
                xa, xb = _head_pair_split(xd[:, tile])
                y_ref[:, GW * gg + 2 * SSD_P * q:GW * gg + 2 * SSD_P * (q + 1)] = (
                    jnp.dot(jnp.concatenate(lmats, axis=1), jnp.concatenate([xa, xb], axis=0).astype(BF16),
                            preferred_element_type=F32) + yoff[:, tile])
            xdd = (xd * _expand8(jnp.exp(a_last - a_cs))).astype(BF16)
            h_ref[gg] = _expand8(jnp.exp(a_last), 3) * h + lax.dot_general(
                bb, xdd, (((0,), (0,)), ((), ())), preferred_element_type=F32)

    nb = SSD_INNER // (gs * SSD_N)
    return pl.pallas_call(
        body, name="ssd_fwd", grid=(SSD_G // gs, nc),
        in_specs=[pl.BlockSpec((CHUNK, gs * GW), lambda g, c: (c, g)),
                  pl.BlockSpec((CHUNK, gs * SSD_N), lambda g, c: (c, nb + g)),
                  pl.BlockSpec((CHUNK, gs * SSD_N), lambda g, c: (c, nb + SSD_G // gs + g)),
                  pl.BlockSpec((gs, CHUNK, SSD_R), lambda g, c: (g, c, 0)),
                  pl.BlockSpec((gs, CHUNK, SSD_R), lambda g, c: (g, c, 0))],
        out_specs=[pl.BlockSpec((CHUNK, gs * GW), lambda g, c: (c, g)),
                   pl.BlockSpec((gs, 1, SSD_N, GW), lambda g, c: (g, c, 0, 0))],
        out_shape=[jax.ShapeDtypeStruct((t, SSD_INNER), F32), jax.ShapeDtypeStruct((SSD_G, nc, SSD_N, GW), F32)],
        scratch_shapes=[pltpu.VMEM((gs, SSD_N, GW), F32)],
        compiler_params=_params(("parallel", "arbitrary"), 32 << 20),
    )(xc, xc, xc, dt3, adt3)


def _ssd_bwd(xc, dt3, adt3, hprev, dy):
    t = xc.shape[0]
    nc = t // CHUNK

    gs = 4

    def body(xs_ref, b_ref, c_ref, dt_ref, adt_ref, hp_ref, dy_ref,
             dx_ref, db_ref, dc_ref, dadt_ref, dxdx_ref, dh_ref):
        @pl.when(pl.program_id(1) == 0)
        def _():
            dh_ref[...] = jnp.zeros_like(dh_ref)

        for gg in range(gs):
            wide = slice(GW * gg, GW * (gg + 1))
            narrow = slice(SSD_N * gg, SSD_N * (gg + 1))
            dx, db, dc, dadt, dxdx, dh_new = group_bwd(
                xs_ref[:, wide], b_ref[:, narrow], c_ref[:, narrow], dt_ref[gg], adt_ref[gg], hp_ref[gg, 0],
                dy_ref[:, wide], dh_ref[gg])
            dx_ref[:, wide] = dx
            db_ref[:, narrow] = db
            dc_ref[:, narrow] = dc
            dadt_ref[gg] = dadt
            dxdx_ref[gg] = dxdx
            dh_ref[gg] = dh_new

    def group_bwd(xs, b, c, dt, adt, hp, dyv, dh):
        a_cs, a_cs_t, causal = _ssd_chunk_terms(adt)
        a_last = a_cs[CHUNK - 1:CHUNK, :]
        e_last = jnp.exp(a_last)
        ex = _expand8(jnp.exp(a_cs))
        dtex = _expand8(jnp.exp(a_last - a_cs))
        dtx = _expand8(dt)
        sel = _sel(GW, SSD_P)
        seg8 = lambda v: _dot01(v, sel, 2)

        xd = xs * dtx
        xdd = xd * dtex
        bb = b.astype(BF16)
        cbf = c.astype(BF16)
        hpb = hp.astype(BF16)
        dhb = dh.astype(BF16)
        xdb = xd.astype(BF16)
        cb = lax.dot_general(cbf, bb, (((1,), (1,)), ((), ())), preferred_element_type=F32)
        dye = (dyv * ex).astype(BF16)
        yoff = jnp.dot(cbf, hpb, preferred_element_type=F32) * ex
        dc = lax.dot_general(dye, hpb, (((1,), (1,)), ((), ())), preferred_element_type=F32)
        bdh = jnp.dot(bb, dhb, preferred_element_type=F32)
        db = lax.dot_general(xdd.astype(BF16), dhb, (((1,), (1,)), ((), ())), preferred_element_type=F32)
        dxd_state = bdh * dtex
        q_terms = xdd * bdh
        d_a = seg8(dyv * yoff - q_terms)
        d_a_last = seg8(jnp.sum(q_terms, axis=0, keepdims=True)
                        + _expand8(e_last, 3) * jnp.sum(hp * dh, axis=0, keepdims=True))
        dh_new = (lax.dot_general(cbf, dye, (((0,), (0,)), ((), ())), preferred_element_type=F32)
                  + _expand8(e_last, 3) * dh)
        dcb = jnp.zeros((CHUNK, CHUNK), F32)
        w_all = []
        dxd_parts = []
        for q2 in range(SSD_R // 2):
            tile = slice(2 * SSD_P * q2, 2 * SSD_P * (q2 + 1))
            dy_pair = [part.astype(BF16) for part in _head_pair_split(dyv[:, tile])]
            lmats = []
            for k, r in enumerate((2 * q2, 2 * q2 + 1)):
                seg = jnp.exp(jnp.where(causal, a_cs[:, r:r + 1] - a_cs_t[r:r + 1, :], -jnp.inf))
                lmat = cb * seg
                dm = lax.dot_general(dy_pair[k], xdb[:, tile], (((1,), (1,)), ((), ())), preferred_element_type=F32)
                dcb = dcb + dm * seg
                w_all.append(dm * lmat)
                lmats.append(lmat.astype(BF16))
            dxd_parts.append(lax.dot_general(jnp.concatenate(lmats, axis=0), jnp.concatenate(dy_pair, axis=0),
                                             (((0,), (0,)), ((), ())), preferred_element_type=F32))
        row_sums = _dot01(jnp.concatenate(w_all, axis=1), _sel(SSD_R * CHUNK, CHUNK), 2)
        cs_rows = jnp.concatenate([jnp.sum(wr, axis=0, keepdims=True) for wr in w_all], axis=0)
        col_sums = _dot01(cs_rows, _sel(SSD_R, 1), 3, dims=((0,), (0,)))
        d_a = d_a + row_sums - col_sums
        li = lax.broadcasted_iota(jnp.int32, (CHUNK, SSD_R), 0)
        d_a = d_a + jnp.where(li == CHUNK - 1, d_a_last, 0.0)
        l2 = lax.broadcasted_iota(jnp.int32, (CHUNK, CHUNK), 0)
        s2 = lax.broadcasted_iota(jnp.int32, (CHUNK, CHUNK), 1)
        dadt = _dot01(jnp.where(s2 >= l2, 1.0, 0.0), d_a, 3, split_lhs=False)
        dxd = dxd_state + jnp.concatenate(dxd_parts, axis=1)
        dcbb = dcb.astype(BF16)
        db = db + lax.dot_general(dcbb, cbf, (((0,), (0,)), ((), ())), preferred_element_type=F32)
        dc = dc + jnp.dot(dcbb, bb, preferred_element_type=F32)
        return dxd * dtx, db, dc, dadt, seg8(dxd * xs), dh_new

    nb = SSD_INNER // (gs * SSD_N)
    rc = lambda g, c: (nc - 1 - c, g)
    r3 = lambda g, c: (g, nc - 1 - c, 0)
    return pl.pallas_call(
        body, name="ssd_bwd", grid=(SSD_G // gs, nc),
        in_specs=[pl.BlockSpec((CHUNK, gs * GW), rc),
                  pl.BlockSpec((CHUNK, gs * SSD_N), lambda g, c: (nc - 1 - c, nb + g)),
                  pl.BlockSpec((CHUNK, gs * SSD_N), lambda g, c: (nc - 1 - c, nb + SSD_G // gs + g)),
                  pl.BlockSpec((gs, CHUNK, SSD_R), r3),
                  pl.BlockSpec((gs, CHUNK, SSD_R), r3),
                  pl.BlockSpec((gs, 1, SSD_N, GW), lambda g, c: (g, nc - 1 - c, 0, 0)),
                  pl.BlockSpec((CHUNK, gs * GW), rc)],
        out_specs=[pl.BlockSpec((CHUNK, gs * GW), rc),
                   pl.BlockSpec((CHUNK, gs * SSD_N), rc),
                   pl.BlockSpec((CHUNK, gs * SSD_N), rc),
                   pl.BlockSpec((gs, CHUNK, SSD_R), r3),
                   pl.BlockSpec((gs, CHUNK, SSD_R), r3)],
        out_shape=[jax.ShapeDtypeStruct((t, SSD_INNER), F32),
                   jax.ShapeDtypeStruct((t, SSD_G * SSD_N), F32),
                   jax.ShapeDtypeStruct((t, SSD_G * SSD_N), F32),
                   jax.ShapeDtypeStruct((SSD_G, t, SSD_R), F32),
                   jax.ShapeDtypeStruct((SSD_G, t, SSD_R), F32)],
        scratch_shapes=[pltpu.VMEM((gs, SSD_N, GW), F32)],
        compiler_params=_params(("parallel", "arbitrary"), 48 << 20),
    )(xc, xc, xc, dt3, adt3, hprev, dy)


def _gated_norm_fwd(y, xc, proj, dexp, ng):
    t = y.shape[0]
    tt = _divtile(t, 256)

    def body(y_ref, x_ref, z_ref, d_ref, g_ref, o_ref):
        z = z_ref[...]
        y2 = (y_ref[...] + d_ref[...] * x_ref[...]) * (z * _sigmoid(z))
        for gi in range(SSD_G):
            sl = slice(GW * gi, GW * (gi + 1))
            seg = y2[:, sl]
            rinv = lax.rsqrt(jnp.mean(seg * seg, axis=-1, keepdims=True) + RMS_EPS)
            o_ref[:, sl] = (seg * rinv * g_ref[:, sl]).astype(BF16)

    row = pl.BlockSpec((tt, SSD_INNER), lambda i: (i, 0))
    vec = pl.BlockSpec((1, SSD_INNER), lambda i: (0, 0))
    return pl.pallas_call(
        body, name="gated_norm_fwd", grid=(t // tt,), in_specs=[row, row, row, vec, vec], out_specs=row,
        out_shape=jax.ShapeDtypeStruct((t, SSD_INNER), BF16),
        compiler_params=_params(("parallel",), 48 << 20),
    )(y, xc, proj, dexp, ng)


def _gated_norm_bwd(dout, y, xc, proj, dexp, ng, dproj):
    t = y.shape[0]
    tt = _divtile(t, 128)

    def body(do_ref, y_ref, x_ref, z_ref, d_ref, g_ref, _, dz_ref, dy_ref, dg_ref, dd_ref):
        z = z_ref[...]
        sg = _sigmoid(z)
        sz = z * sg
        xs = x_ref[...]
        y1 = y_ref[...] + d_ref[...] * xs
        y2 = y1 * sz
        dov = do_ref[...]

        @pl.when(pl.program_id(0) == 0)
        def _():
            dg_ref[...] = jnp.zeros_like(dg_ref)
            dd_ref[...] = jnp.zeros_like(dd_ref)

        for gi in range(SSD_G):
            sl = slice(GW * gi, GW * (gi + 1))
            seg = y2[:, sl]
            rinv = lax.rsqrt(jnp.mean(seg * seg, axis=-1, keepdims=True) + RMS_EPS)
            yn = seg * rinv
            dsl = dov[:, sl]
            dg_ref[:, sl] += jnp.sum(dsl * yn, axis=0, keepdims=True)
            dyn = dsl * g_ref[:, sl]
            dy2 = rinv * (dyn - yn * jnp.mean(dyn * yn, axis=-1, keepdims=True))
            dz_ref[:, sl] = (dy2 * y1[:, sl] * (sg[:, sl] * (1.0 + z[:, sl] * (1.0 - sg[:, sl])))).astype(BF16)
            dy1 = dy2 * sz[:, sl]
            dy_ref[:, sl] = dy1
            dd_ref[:, sl] += jnp.sum(dy1 * xs[:, sl], axis=0, keepdims=True)

    row = pl.BlockSpec((tt, SSD_INNER), lambda i: (i, 0))
    vec = pl.BlockSpec((1, SSD_INNER), lambda i: (0, 0))
    p_in, p_out, p_shape = _dproj_piece(t, tt, SSD_INNER, lambda i: (i, 0))
    return pl.pallas_call(
        body, name="gated_norm_bwd", grid=(t // tt,), in_specs=[row, row, row, row, vec, vec, p_in],
        out_specs=[p_out, row, vec, vec],
        out_shape=[p_shape, jax.ShapeDtypeStruct((t, SSD_INNER), F32),
                   jax.ShapeDtypeStruct((1, SSD_INNER), F32), jax.ShapeDtypeStruct((1, SSD_INNER), F32)],
        input_output_aliases={6: 0},
        compiler_params=_params(("arbitrary",), 48 << 20),
    )(dout, y, xc, proj, dexp, ng, dproj)


def _fold_heads(v, name):
    def body(v_ref, o_ref):
        ri = lax.broadcasted_iota(jnp.int32, (SSD_INNER, 128), 0)
        ci = lax.broadcasted_iota(jnp.int32, (SSD_INNER, 128), 1)
        fold = jnp.where((ri >= ci * SSD_P) & (ri < (ci + 1) * SSD_P), 1.0, 0.0).astype(F32)
        o_ref[...] = jnp.dot(v_ref[...], fold, preferred_element_type=F32, precision=HI)

    return pl.pallas_call(body, name=name, out_shape=jax.ShapeDtypeStruct((1, 128), F32))(v)


Q_BLK = SEGS["q"][2] // QW
K_BLK = SEGS["k"][2] // KVW
V_BLK = SEGS["v"][2] // KVW


def _rope_tables(pos_ref, invf_ref, width):
    ang = pos_ref[...] * invf_ref[...]
    lane = lax.broadcasted_iota(jnp.int32, (1, 128), 1)
    sign = jnp.where((lane % HD) < (HD // 2), -1.0, 1.0)
    cos = jnp.tile(jnp.cos(ang), (1, width // 128))
    sin = jnp.tile(sign * jnp.sin(ang), (1, width // 128))
    first = (lax.broadcasted_iota(jnp.int32, (1, width), 1) % HD) < (HD // 2)
    return cos, sin, first


def _rot_half(u, first):
    w = u.shape[1]
    return jnp.where(first, pltpu.roll(u, w - HD // 2, 1), pltpu.roll(u, HD // 2, 1))


def _rope_fwd(proj, pos, invf):
    t = proj.shape[0]
    tt = _divtile(t, 512)

    def body(q_ref, k_ref, pos_ref, invf_ref, qo_ref, ko_ref):
        cos, sin, first = _rope_tables(pos_ref, invf_ref, QW)
        q = q_ref[...]
        qr = q * cos + _rot_half(q, first) * sin
        for p in range(QW // 128):
            qo_ref[128 * p:128 * (p + 1), :] = qr[:, 128 * p:128 * (p + 1)].T.astype(BF16)
        k = k_ref[...]
        ko_ref[...] = (k * cos[:, :KVW] + _rot_half(k, first[:, :KVW]) * sin[:, :KVW]).astype(BF16)

    return pl.pallas_call(
        body, name="rope_fwd", grid=(t // tt,),
        in_specs=[pl.BlockSpec((tt, QW), lambda i: (i, Q_BLK)), pl.BlockSpec((tt, KVW), lambda i: (i, K_BLK)),
                  pl.BlockSpec((tt, 1), lambda i: (i, 0)), pl.BlockSpec((1, 128), lambda i: (0, 0))],
        out_specs=[pl.BlockSpec((QW, tt), lambda i: (0, i)), pl.BlockSpec((tt, KVW), lambda i: (i, 0))],
        out_shape=[jax.ShapeDtypeStruct((QW, t), BF16), jax.ShapeDtypeStruct((t, KVW), BF16)],
        compiler_params=_params(("parallel",), 40 << 20),
    )(proj, proj, pos, invf)


def _rope_bwd(dqt, dk, pos, invf, dproj):
    t = dk.shape[0]
    tt = _divtile(t, 512)

    def body(dq_ref, dk_ref, pos_ref, invf_ref, _, qo_ref, ko_ref):
        cos, sin, first = _rope_tables(pos_ref, invf_ref, QW)
        q = jnp.concatenate([dq_ref[128 * p:128 * (p + 1), :].T for p in range(QW // 128)], axis=1)
        qo_ref[...] = (q * cos + _rot_half(q * sin, first)).astype(BF16)
        k = dk_ref[...]
        ko_ref[...] = (k * cos[:, :KVW] + _rot_half(k * sin[:, :KVW], first[:, :KVW])).astype(BF16)

    p_in, p_out, p_shape = _dproj_piece(t, tt, QW, lambda i: (i, Q_BLK))
    return pl.pallas_call(
        body, name="rope_bwd", grid=(t // tt,),
        in_specs=[pl.BlockSpec((QW, tt), lambda i: (0, i)), pl.BlockSpec((tt, KVW), lambda i: (i, 0)),
                  pl.BlockSpec((tt, 1), lambda i: (i, 0)), pl.BlockSpec((1, 128), lambda i: (0, 0)), p_in],
        out_specs=[p_out, pl.BlockSpec((tt, KVW), lambda i: (i, 0))],
        out_shape=[p_shape, jax.ShapeDtypeStruct((t, KVW), BF16)],
        input_output_aliases={4: 0},
        compiler_params=_params(("parallel",), 40 << 20),
    )(dqt, dk, pos, invf, dproj)


def _place_cols(piece, dproj, col_blk, name):
    t, w = piece.shape
    tt = _divtile(t, 1024)

    def body(p_ref, _, o_ref):
        o_ref[...] = p_ref[...]

    p_in, p_out, p_shape = _dproj_piece(t, tt, w, lambda i: (i, col_blk))
    return pl.pallas_call(
        body, name=name, grid=(t // tt,),
        in_specs=[pl.BlockSpec((tt, w), lambda i: (i, 0)), p_in], out_specs=p_out, out_shape=p_shape,
        input_output_aliases={1: 0},
        compiler_params=_params(("parallel",), 16 << 20),
    )(piece, dproj)


GQ = NQ // NKV
NT_DIMS = (((1,), (1,)), ((), ()))
TN_DIMS = (((0,), (0,)), ((), ()))


def _attn_heads(ref, j, dtype=None):
    out = jnp.concatenate([ref[HD * h:HD * (h + 1), :] for h in range(j * GQ, (j + 1) * GQ)], axis=1)
    return out if dtype is None else out.astype(dtype)


def _attn_sink_row(s_ref, j):
    return jnp.concatenate([jnp.broadcast_to(s_ref[:, h:h + 1], (1, WINDOW)) for h in range(j * GQ, (j + 1) * GQ)],
                           axis=1)


def _attn_mask(n):
    kr = lax.broadcasted_iota(jnp.int32, (2 * WINDOW, GQ * WINDOW), 0)
    qi = lax.broadcasted_iota(jnp.int32, (2 * WINDOW, GQ * WINDOW), 1) % WINDOW
    return (kr > qi) & (kr <= qi + WINDOW) & ((n > 0) | (kr >= WINDOW))


def _attn_probs(qgt, kk, sink, mask):
    s = jnp.where(mask, jnp.dot(kk, qgt, preferred_element_type=F32) * (HD ** -0.5), -jnp.inf)
    m = jnp.maximum(jnp.max(s, axis=0, keepdims=True), sink)
    p = jnp.exp(s - m)
    ps = jnp.exp(sink - m)
    inv = 1.0 / (jnp.sum(p, axis=0, keepdims=True) + ps)
    return p * inv, ps * inv


def _attn_fwd(qt, kr, proj, sinks):
    t = kr.shape[0]
    nb = t // WINDOW

    def body(q_ref, kc_ref, kp_ref, vc_ref, vp_ref, s_ref, o_ref):
        mask = _attn_mask(pl.program_id(0))
        for j in range(NKV):
            ks = slice(HD * j, HD * (j + 1))
            kk = jnp.concatenate([kp_ref[:, ks], kc_ref[:, ks]], axis=0)
            vv = jnp.concatenate([vp_ref[:, ks], vc_ref[:, ks]], axis=0).astype(BF16)
            pn, _ = _attn_probs(_attn_heads(q_ref, j), kk, _attn_sink_row(s_ref, j), mask)
            ot = lax.dot_general(vv, pn.astype(BF16), TN_DIMS, preferred_element_type=F32).astype(BF16)
            for g in range(GQ):
                h = j * GQ + g
                o_ref[HD * h:HD * (h + 1), :] = ot[:, WINDOW * g:WINDOW * (g + 1)]

    prev = lambda n: (jnp.maximum(n - 1, 0), 0)
    return pl.pallas_call(
        body, name="attn_fwd", grid=(nb,),
        in_specs=[pl.BlockSpec((QW, WINDOW), lambda n: (0, n)),
                  pl.BlockSpec((WINDOW, KVW), lambda n: (n, 0)), pl.BlockSpec((WINDOW, KVW), prev),
                  pl.BlockSpec((WINDOW, KVW), lambda n: (n, V_BLK)),
                  pl.BlockSpec((WINDOW, KVW), lambda n: (jnp.maximum(n - 1, 0), V_BLK)),
                  pl.BlockSpec((1, 128), lambda n: (0, 0))],
        out_specs=pl.BlockSpec((QW, WINDOW), lambda n: (0, n)),
        out_shape=jax.ShapeDtypeStruct((QW, t), BF16),
        compiler_params=_params(("parallel",), 24 << 20),
    )(qt, kr, kr, proj, proj, sinks)


def _attn_bwd(qt, kr, proj, sinks, dot_, dproj):
    t = kr.shape[0]
    nb = t // WINDOW

    def body(q_ref, kc_ref, kp_ref, vc_ref, vp_ref, s_ref, do_ref, _,
             dq_ref, dk_ref, dv_ref, ds_ref, dkc_ref, dvc_ref):
        i = pl.program_id(0)
        mask = _attn_mask(nb - 1 - i)

        @pl.when(i == 0)
        def _():
            dkc_ref[...] = jnp.zeros_like(dkc_ref)
            dvc_ref[...] = jnp.zeros_like(dvc_ref)
            ds_ref[...] = jnp.zeros_like(ds_ref)

        lane = lax.broadcasted_iota(jnp.int32, (1, 128), 1)
        ds_acc = jnp.zeros((1, 128), F32)
        for j in range(NKV):
            ks = slice(HD * j, HD * (j + 1))
            kk = jnp.concatenate([kp_ref[:, ks], kc_ref[:, ks]], axis=0)
            vv = jnp.concatenate([vp_ref[:, ks], vc_ref[:, ks]], axis=0).astype(BF16)
            qgt = _attn_heads(q_ref, j)
            pn, psn = _attn_probs(qgt, kk, _attn_sink_row(s_ref, j), mask)
            dogt = _attn_heads(do_ref, j)
            dp = jnp.dot(vv, dogt, preferred_element_type=F32)
            delta = jnp.sum(dp * pn, axis=0, keepdims=True)
            dsb = (pn * (dp - delta) * (HD ** -0.5)).astype(BF16)
            dsink = -psn * delta
            dqt = lax.dot_general(kk, dsb, TN_DIMS, preferred_element_type=F32)
            for g in range(GQ):
                h = j * GQ + g
                cols = slice(WINDOW * g, WINDOW * (g + 1))
                dq_ref[HD * h:HD * (h + 1), :] = dqt[:, cols]
                ds_acc = ds_acc + jnp.where(lane == h, jnp.sum(dsink[:, cols], axis=1, keepdims=True), 0.0)
            dkk = lax.dot_general(dsb, qgt, NT_DIMS, preferred_element_type=F32)
            dvv = lax.dot_general(pn.astype(BF16), dogt, NT_DIMS, preferred_element_type=F32)
            dk_ref[:, ks] = dkk[WINDOW:] + dkc_ref[:, ks]
            dv_ref[:, ks] = (dvv[WINDOW:] + dvc_ref[:, ks]).astype(BF16)
            dkc_ref[:, ks] = dkk[:WINDOW]
            dvc_ref[:, ks] = dvv[:WINDOW]
        ds_ref[...] += ds_acc

    cur = lambda i: (nb - 1 - i, 0)
    cur_t = lambda i: (0, nb - 1 - i)
    prev = lambda i: (jnp.maximum(nb - 2 - i, 0), 0)
    p_in, p_out, p_shape = _dproj_piece(t, WINDOW, KVW, lambda i: (nb - 1 - i, V_BLK))
    return pl.pallas_call(
        body, name="attn_bwd", grid=(nb,),
        in_specs=[pl.BlockSpec((QW, WINDOW), cur_t),
                  pl.BlockSpec((WINDOW, KVW), cur), pl.BlockSpec((WINDOW, KVW), prev),
                  pl.BlockSpec((WINDOW, KVW), lambda i: (nb - 1 - i, V_BLK)),
                  pl.BlockSpec((WINDOW, KVW), lambda i: (jnp.maximum(nb - 2 - i, 0), V_BLK)),
                  pl.BlockSpec((1, 128), lambda i: (0, 0)),
                  pl.BlockSpec((QW, WINDOW), cur_t), p_in],
        out_specs=[pl.BlockSpec((QW, WINDOW), cur_t), pl.BlockSpec((WINDOW, KVW), cur),
                   p_out, pl.BlockSpec((1, 128), lambda i: (0, 0))],
        out_shape=[jax.ShapeDtypeStruct((QW, t), F32), jax.ShapeDtypeStruct((t, KVW), F32),
                   p_shape, jax.ShapeDtypeStruct((1, 128), F32)],
        input_output_aliases={7: 2},
        scratch_shapes=[pltpu.VMEM((WINDOW, KVW), F32), pltpu.VMEM((WINDOW, KVW), F32)],
        compiler_params=_params(("arbitrary",), 32 << 20),
    )(qt, kr, kr, proj, proj, sinks, dot_, dproj)


GS_BLK = SEGS["gs"][2] // D
GA_BLK = SEGS["ga"][2] // D


def _merge_fwd(ys, ya, proj):
    t = ys.shape[0]
    tt = _divtile(t, 256)

    def body(ys_ref, ya_ref, gs_ref, ga_ref, o_ref):
        o_ref[...] = (_sigmoid(gs_ref[...]) * ys_ref[...] + _sigmoid(ga_ref[...]) * ya_ref[...]).astype(BF16)

    row = pl.BlockSpec((tt, D), lambda i: (i, 0))
    return pl.pallas_call(
        body, name="merge_fwd", grid=(t // tt,),
        in_specs=[row, row, pl.BlockSpec((tt, D), lambda i: (i, GS_BLK)), pl.BlockSpec((tt, D), lambda i: (i, GA_BLK))],
        out_specs=row, out_shape=jax.ShapeDtypeStruct((t, D), BF16),
        compiler_params=_params(("parallel",), 32 << 20),
    )(ys, ya, proj, proj)


def _dproj_piece(t, rows, width, index_map):
    return (pl.BlockSpec(memory_space=pl.ANY), pl.BlockSpec((rows, width), index_map),
            jax.ShapeDtypeStruct((t, PROJ_PAD), BF16))


def _merge_bwd(dm, ys, ya, proj, dproj):
    t = ys.shape[0]
    tt = _divtile(t, 256)

    def body(dm_ref, ys_ref, ya_ref, gs_ref, ga_ref, _, dys_ref, dya_ref, dg_ref):
        d = dm_ref[...]
        s = _sigmoid(gs_ref[...])
        a = _sigmoid(ga_ref[...])
        dys_ref[...] = (d * s).astype(BF16)
        dya_ref[...] = (d * a).astype(BF16)
        dg_ref[:, :D] = (d * ys_ref[...] * (s * (1.0 - s))).astype(BF16)
        dg_ref[:, D:] = (d * ya_ref[...] * (a * (1.0 - a))).astype(BF16)

    row = pl.BlockSpec((tt, D), lambda i: (i, 0))
    p_in, p_out, p_shape = _dproj_piece(t, tt, 2 * D, lambda i: (i, SEGS["gs"][2] // (2 * D)))
    return pl.pallas_call(
        body, name="merge_bwd", grid=(t // tt,),
        in_specs=[row, row, row, pl.BlockSpec((tt, D), lambda i: (i, GS_BLK)),
                  pl.BlockSpec((tt, D), lambda i: (i, GA_BLK)), p_in],
        out_specs=[row, row, p_out], out_shape=[jax.ShapeDtypeStruct((t, D), BF16)] * 2 + [p_shape],
        input_output_aliases={5: 2},
        compiler_params=_params(("parallel",), 40 << 20),
    )(dm, ys, ya, proj, proj, dproj)


def _pad128(v):
    return jnp.pad(v, ((0, 0), (0, 128 - v.shape[1])))


def _local_step(x, pos, target, w, small, fetch=None, early_grads=None):
    w = dict(w)
    xb = x.astype(BF16)
    if fetch is None:
        gu1, a1 = _mm_swiglu(xb, w["gu1"], "ffn1_gu")
    else:
        own = _mm_swiglu(xb, w["gu1_own"], "ffn1_gu_own", chip_idx=w["chip_idx"])
        w.update(fetch(0, own[1]))
        gu1, a1 = _mm_swiglu(xb, w["gu1"], "ffn1_gu_rest", chip_idx=w["chip_idx"], done=own)
        w.update(fetch(1, a1))
    h1, h1b, xh1, rs1 = _mm_ln(a1, w["d1"], x, small["ln1_g"], small["ln1_b"], 0.5, "ffn1_down_ln1")
    if fetch is not None:
        w.update(fetch(2, h1b))
    proj = _mm(h1b, w["win"], "nn", F32, "proj", caps=(1024, 896, 2048))
    if fetch is not None:
        w.update(fetch(3, proj))
    bias128 = _pad128(small["dt_bias"])
    alog128 = _pad128(small["a_log"])
    dt3, adt3 = _dt_prep(proj, bias128, alog128)
    xc = _conv_fwd(proj, small["conv_w"], small["conv_b"])
    y_ssd, hprev = _ssd_fwd(xc, dt3, adt3)
    dexp = jnp.repeat(small["d_skip"], SSD_P, axis=1)
    ysn = _gated_norm_fwd(y_ssd, xc, proj, dexp, small["ssd_norm_g"])
    ys = _mm(ysn, w["so"], "nn", F32, "ssd_out")
    invf = jnp.tile(ROPE_THETA ** (-jnp.arange(HD // 2, dtype=F32) * 2.0 / HD), 4)[None, :]
    qt, kr = _rope_fwd(proj, pos, invf)
    sinks128 = _pad128(small["attn_sinks"])
    ot = _attn_fwd(qt, kr, proj, sinks128)
    ya = _mm(ot, w["ao"], "tn", F32, "attn_out")
    mg = _merge_fwd(ys, ya, proj)
    mix = _mm(mg, w["out"], "nn", F32, "mix_out")
    h2, h2b, xh2, rs2 = _ln_fwd(h1, mix, small["ln2_g"], small["ln2_b"], 1.0, "ln2_fwd")
    gu2, a2 = _mm_swiglu(h2b, w["gu2"], "ffn2_gu")
    f2 = _mm(a2, w["d2"], "nn", F32, "ffn2_down", caps=(512, 1024, FFN_H))
    _, _, xh3, rs3, dh3, loss = _ln_fwd(h2, f2, small["ln3_g"], small["ln3_b"], 0.5, "ln3_fwd", target=target)

    gw, gs = {}, {}
    dr3, dr3h, gs["ln3_g"], gs["ln3_b"] = _ln_bwd(dh3, xh3, rs3, small["ln3_g"], 0.5, "ln3_bwd")
    gw["d2"] = _mm(a2, dr3h, "tn", F32, "ffn2_down_dw")
    dgu2 = _mm_swiglu_bwd(dr3h, w["d2"], gu2, "ffn2_down_dx")
    gw["gu2"] = _mm(h2b, dgu2, "tn", F32, "ffn2_gu_dw", caps=(1024, 1408, 2048), n_slabs=N_CHIPS)
    dh2 = _mm(dgu2, w["gu2"], "nt", F32, "ffn2_gu_dx", add=dr3, add_scale=ALPHA, caps=(1024, 1024, 2816))
    dr2, dr2b, gs["ln2_g"], gs["ln2_b"] = _ln_bwd(dh2, xh2, rs2, small["ln2_g"], 1.0, "ln2_bwd")
    gw["out"] = _mm(mg, dr2b, "tn", F32, "mix_out_dw")
    dmg = _mm(dr2b, w["out"], "nt", F32, "mix_out_dx")
    dproj = lax.empty((x.shape[0], PROJ_PAD), BF16)
    dys, dya, dproj = _merge_bwd(dmg, ys, ya, proj, dproj)
    gw["ao"] = _mm(ot, dya, "nn", F32, "attn_out_dw")
    dot_ = _mm(w["ao"], dya, "nt", BF16, "attn_out_dx")
    dqt, dkr, dproj, gs["attn_sinks"] = _attn_bwd(qt, kr, proj, sinks128, dot_, dproj)
    dproj, dk = _rope_bwd(dqt, dkr, pos, invf, dproj)
    dproj = _place_cols(dk, dproj, K_BLK, "place_dk")
    gw["so"] = _mm(ysn, dys, "tn", F32, "ssd_out_dw")
    dysn = _mm(dys, w["so"], "nt", F32, "ssd_out_dx")
    dproj, dy1, gs["ssd_norm_g"], dd_ch = _gated_norm_bwd(dysn, y_ssd, xc, proj, dexp, small["ssd_norm_g"], dproj)
    gs["d_skip"] = _fold_heads(dd_ch, "d_skip_fold")
    dxs, db, dc, dadt3, dxdx3 = _ssd_bwd(xc, dt3, adt3, hprev, dy1)
    ddt, gs["dt_bias"], gs["a_log"] = _dt_bwd(dadt3, dxdx3, proj, bias128, alog128)
    dproj = _place_cols(ddt, dproj, DT_BLK, "place_ddt")
    cw, cbias = small["conv_w"], small["conv_b"]
    dproj, dwx, dbx = _conv_bwd(proj, dxs, cw, cbias, 0, SSD_INNER, "conv_bwd_x", dproj, skip=(dy1, dexp))
    dproj, dwb, dbb = _conv_bwd(proj, db, cw, cbias, SSD_INNER, SSD_G * SSD_N, "conv_bwd_b", dproj)
    dproj, dwc, dbc = _conv_bwd(proj, dc, cw, cbias, SSD_INNER + SSD_G * SSD_N, SSD_G * SSD_N, "conv_bwd_c", dproj)
    gs["conv_w"] = jnp.concatenate([dwx[:4], dwb[:4], dwc[:4]], axis=1)
    gs["conv_b"] = jnp.concatenate([dbx, dbb, dbc], axis=1)
    gw["win"] = _mm(h1b, dproj, "tn", F32, "proj_dw", caps=(1024, 896, 2048))
    win = w["win"] if early_grads is None else early_grads[0](gw, w["win"])
    dh1 = _mm(dproj, win, "nt", F32, "proj_dx", add=dr2, add_scale=ALPHA, caps=(1024, 1024, 2432))
    ln1_g = small["ln1_g"]
    if early_grads is not None:
        ln1_g = ln1_g + early_grads[1](dh1)[0:1, 0:1]
    dr1, dr1h, gs["ln1_g"], gs["ln1_b"] = _ln_bwd(dh1, xh1, rs1, ln1_g, 0.5, "ln1_bwd")
    gw["d1"] = _mm(a1, dr1h, "tn", F32, "ffn1_down_dw")
    dgu1 = _mm_swiglu_bwd(dr1h, w["d1"], gu1, "ffn1_down_dx")
    gw["gu1"] = _mm(xb, dgu1, "tn", F32, "ffn1_gu_dw", caps=(1024, 1408, 2048), n_slabs=N_CHIPS)
    gu1w = w["gu1"] if early_grads is None else early_grads[2](gw, w["gu1"])
    grad_x = _mm(dgu1, gu1w, "nt", F32, "ffn1_gu_dx", add=dr1, add_scale=ALPHA, caps=(1024, 1024, 2816))
    return loss, grad_x, gw, gs


MESH = pl.DeviceIdType.MESH
ANY = pl.BlockSpec(memory_space=pl.ANY)


def _place():
    x, y, c = lax.axis_index("x"), lax.axis_index("y"), lax.axis_index("c")
    peers = [(1 - x, y), (x, 1 - y), (1 - x, 1 - y)]
    return x, y, c, peers


BIG = [
    ("ffn1_w_gate", D, SHARD_H, "gu1", "col", 0),
    ("ffn1_w_up", D, SHARD_H, "gu1", "col", SHARD_H),
    ("ffn1_w_down", SHARD_H, D, "d1", "row", 0),
    ("w_in", D, SHARD_IN, "win4", "lead", 0),
    ("w_ssd_o", SSD_INNER // N_CHIPS, D, "so", "row", 0),
    ("w_attn_o", D // N_CHIPS, D, "ao", "row", 0),
    ("w_out", D // N_CHIPS, D, "out", "row", 0),
    ("ffn2_w_gate", D, SHARD_H, "gu2", "col", 0),
    ("ffn2_w_up", D, SHARD_H, "gu2", "col", SHARD_H),
    ("ffn2_w_down", SHARD_H, D, "d2", "row", 0),
]
GATHERED = {"gu1": (D, 2 * FFN_H), "d1": (FFN_H, D), "win4": (N_CHIPS, D, SHARD_IN), "so": (SSD_INNER, D),
            "ao": (D, D), "out": (D, D), "gu2": (D, 2 * FFN_H), "d2": (FFN_H, D)}


def _cast_place(srcs, oname, chip_idx, also_alone=False):
    rows, cols = srcs[0].shape
    tr = _divtile(rows, 256, 16)
    kind = [b[4] for b in BIG if b[3] == oname][0]
    n_src = len(srcs)

    def body(chip_ref, *refs):
        for o_ref in refs[n_src:]:
            for k, s_ref in enumerate(refs[:n_src]):
                o_ref[:, k * cols:(k + 1) * cols] = s_ref[...].astype(BF16)

    nt = rows // tr
    if kind == "col":
        o_spec = pl.BlockSpec((tr, n_src * cols), lambda i, chip_ref: (i, chip_ref[0]))
    elif kind == "row":
        o_spec = pl.BlockSpec((tr, cols), lambda i, chip_ref: (chip_ref[0] * nt + i, 0))
    else:
        o_spec = pl.BlockSpec((None, tr, cols), lambda i, chip_ref: (chip_ref[0], i, 0))
    out_specs, out_shape = [o_spec], [jax.ShapeDtypeStruct(GATHERED[oname], BF16)]
    if also_alone:
        out_specs.append(pl.BlockSpec((tr, n_src * cols), lambda i, chip_ref: (i, 0)))
        out_shape.append(jax.ShapeDtypeStruct((rows, n_src * cols), BF16))
    res = pl.pallas_call(
        body, name="cast_place_" + oname,
        grid_spec=pltpu.PrefetchScalarGridSpec(
            num_scalar_prefetch=1, grid=(nt,),
            in_specs=[pl.BlockSpec((tr, cols), lambda i, chip_ref: (i, 0))] * n_src, out_specs=out_specs),
        out_shape=out_shape,
        compiler_params=_params(("parallel",), 32 << 20),
    )(chip_idx, *srcs)
    return res if also_alone else res[0]


def _slot(outs, entry, j, half):
    _, rows, cols, oname, kind, off = entry
    o = outs[oname]
    hr = rows // 2
    if kind == "col":
        cs = pl.ds(pl.multiple_of(j * (2 * SHARD_H) + off, 128), cols)
        return o.at[pl.ds(pl.multiple_of(half * hr, 16), hr), cs]
    if kind == "row":
        return o.at[pl.ds(pl.multiple_of(j * rows + half * hr, 16), hr), :]
    return o.at[j, pl.ds(pl.multiple_of(half * hr, 16), hr), :]


HBM = pl.BlockSpec(memory_space=pltpu.HBM)
SEM = pl.BlockSpec(memory_space=pltpu.SEMAPHORE)


def _ici_copy(outs, entry, j, c, to, send, recv, k):
    ref = _slot(outs, entry, j, c)
    return pltpu.make_async_remote_copy(src_ref=ref, dst_ref=ref, send_sem=send.at[k], recv_sem=recv.at[k],
                                        device_id=to, device_id_type=MESH)


GATHER_GROUPS = [["gu1"], ["d1"], ["win4"], ["so", "ao", "out", "gu2", "d2"]]


def _gather_ici_start(placed, groups, tag, carried):
    names = [k for grp in groups for k in grp]
    bigs = [[b for b in BIG if b[3] in grp] for grp in groups]
    ng = len(groups)
    n_in = len(names) + 1

    def body(*refs):
        sems = refs[n_in:n_in + 2 * ng]
        outs = dict(zip(names, refs[n_in + 2 * ng:n_in + 2 * ng + len(names)]))
        token = refs[-1]
        x, y, c, peers = _place()
        for gi, big in enumerate(bigs):
            for i, entry in enumerate(big):
                for k, (px, py) in enumerate(peers):
                    _ici_copy(outs, entry, 2 * x + y, c, (px, py, c), sems[2 * gi], sems[2 * gi + 1], 3 * i + k).start()
        token[...] = jnp.zeros_like(token)

    sem_shapes = [pltpu.SemaphoreType.DMA((3 * len(big),)) for big in bigs for _ in range(2)]
    res = pl.pallas_call(
        body, name="gather_ici_start_" + tag,
        in_specs=[HBM] * n_in,
        out_specs=[SEM] * (2 * ng) + [HBM] * n_in + [pl.BlockSpec(memory_space=pltpu.VMEM)],
        out_shape=sem_shapes + [pltpu.HBM(GATHERED[k], BF16) for k in names]
        + [pltpu.HBM(carried.shape, carried.dtype), jax.ShapeDtypeStruct((8, 128), F32)],
        input_output_aliases={i: i + 2 * ng for i in range(n_in)},
        compiler_params=pltpu.CompilerParams(has_side_effects=pltpu.SideEffectType.DATAFLOW_SIDE_EFFECTING),
    )(*[pltpu.with_memory_space_constraint(a, pltpu.HBM) for a in [placed[k] for k in names] + [carried]])
    sems = [(res[2 * gi], res[2 * gi + 1]) for gi in range(ng)]
    return sems, dict(zip(names, res[2 * ng:2 * ng + len(names)])), res[2 * ng + len(names)]


def _gather_ici_wait(send, recv, arrays, names, after, tag):
    big = [b for b in BIG if b[3] in names]

    def body(*refs):
        outs = dict(zip(names, refs[:len(names)]))
        send_ref, recv_ref = refs[len(names)], refs[len(names) + 1]
        x, y, c, peers = _place()
        for i, entry in enumerate(big):
            for k, (px, py) in enumerate(peers):
                mine = _ici_copy(outs, entry, 2 * x + y, c, (px, py, c), send_ref, recv_ref, 3 * i + k)
                mine.wait_send()
                theirs = _ici_copy(outs, entry, 2 * px + py, c, (px, py, c), send_ref, recv_ref, 3 * i + k)
                theirs.wait_recv()

    res = pl.pallas_call(
        body, name="gather_ici_wait_" + tag,
        in_specs=[HBM] * len(names) + [SEM, SEM, pl.BlockSpec(memory_space=pl.ANY)],
        out_specs=[HBM] * len(names),
        out_shape=[pltpu.HBM(GATHERED[k], BF16) for k in names],
        input_output_aliases={i: i for i in range(len(names))},
        compiler_params=pltpu.CompilerParams(has_side_effects=pltpu.SideEffectType.DATAFLOW_SIDE_EFFECTING),
    )(*[arrays[k] for k in names], send, recv, after)
    return dict(zip(names, res))


def _gather_d2d(arrays, names, tag):
    big = [b for b in BIG if b[3] in names]
    n = len(big)

    def body(*refs):
        outs = dict(zip(names, refs[len(names):2 * len(names)]))
        fsend, frecv = refs[2 * len(names):]
        x, y, c, peers = _place()
        cps = []
        for i, entry in enumerate(big):
            for k, (px, py) in enumerate(peers):
                cp = _ici_copy(outs, entry, 2 * px + py, c, (x, y, 1 - c), fsend, frecv, 3 * i + k)
                cp.start()
                cps.append(cp)
        for i, entry in enumerate(big):
            for k, (px, py) in enumerate(peers):
                _ici_copy(outs, entry, 2 * px + py, 1 - c, (x, y, 1 - c), fsend, frecv, 3 * i + k).wait_recv()
        for cp in cps:
            cp.wait_send()

    res = pl.pallas_call(
        body, name="gather_d2d_" + tag,
        in_specs=[ANY] * len(names), out_specs=[ANY] * len(names),
        out_shape=[jax.ShapeDtypeStruct(GATHERED[k], BF16) for k in names],
        input_output_aliases={i: i for i in range(len(names))},
        scratch_shapes=[pltpu.SemaphoreType.DMA((3 * n,))] * 2,
    )(*[arrays[k] for k in names])
    return dict(zip(names, res))


def _win_pieces():
    pieces = []
    for g0, wd, i0 in SEGS.values():
        for j in range(N_CHIPS):
            lo, hi = max(g0, j * SHARD_IN), min(g0 + wd, (j + 1) * SHARD_IN)
            if lo < hi:
                pieces.append((j, lo - j * SHARD_IN, hi - j * SHARD_IN, i0 + lo - g0))
    return pieces


def _win_to_internal(win4):
    tr = 128

    def body(i_ref, o_ref):
        for j, s0, s1, d0 in _win_pieces():
            o_ref[:, d0:d0 + s1 - s0] = i_ref[j, :, s0:s1]
        o_ref[:, PROJ_W:] = jnp.zeros((tr, PROJ_PAD - PROJ_W), o_ref.dtype)

    return pl.pallas_call(
        body, name="win_to_internal", grid=(D // tr,),
        in_specs=[pl.BlockSpec((N_CHIPS, tr, SHARD_IN), lambda i: (0, i, 0))],
        out_specs=pl.BlockSpec((tr, PROJ_PAD), lambda i: (i, 0)),
        out_shape=jax.ShapeDtypeStruct((D, PROJ_PAD), win4.dtype),
        compiler_params=_params(("parallel",), 40 << 20),
    )(win4)


def _win_from_internal(g):
    tr = 64

    def body(i_ref, o_ref):
        for j, s0, s1, d0 in _win_pieces():
            o_ref[j, :, s0:s1] = i_ref[:, d0:d0 + s1 - s0]

    return pl.pallas_call(
        body, name="win_from_internal", grid=(D // tr,),
        in_specs=[pl.BlockSpec((tr, PROJ_PAD), lambda i: (i, 0))],
        out_specs=pl.BlockSpec((N_CHIPS, tr, SHARD_IN), lambda i: (0, i, 0)),
        out_shape=jax.ShapeDtypeStruct((N_CHIPS, D, SHARD_IN), g.dtype),
        compiler_params=_params(("parallel",), 40 << 20),
    )(g)


def _pair_copy(src, dst, c, to, send, recv, k):
    hr = src.shape[1] // 2
    return pltpu.make_async_remote_copy(
        src_ref=src.at[:, pl.ds(pl.multiple_of((1 - c) * hr, 16), hr), :], dst_ref=dst,
        send_sem=send.at[k], recv_sem=recv.at[k], device_id=to, device_id_type=MESH)


def _rs_pair_start(grads, carried, tag):
    n = len(grads)

    def body(*refs):
        send, recv = refs[2 * n + 1], refs[2 * n + 2]
        srcs, dsts = refs[2 * n + 3:3 * n + 3], refs[3 * n + 3:4 * n + 3]
        x, y, c, _ = _place()
        for i in range(n):
            _pair_copy(srcs[i], dsts[i], c, (x, y, 1 - c), send, recv, i).start()

    lands = [lax.empty((g.shape[0], g.shape[1] // 2, g.shape[2]), F32) for g in grads]
    res = pl.pallas_call(
        body, name="rs_pair_start_" + tag,
        in_specs=[HBM] * (2 * n + 1), out_specs=[SEM, SEM] + [HBM] * (2 * n + 1),
        out_shape=[pltpu.SemaphoreType.DMA((n,)), pltpu.SemaphoreType.DMA((n,))]
        + [pltpu.HBM(g.shape, F32) for g in grads] + [pltpu.HBM(l.shape, F32) for l in lands]
        + [pltpu.HBM(carried.shape, carried.dtype)],
        input_output_aliases={i: i + 2 for i in range(2 * n + 1)},
        compiler_params=pltpu.CompilerParams(has_side_effects=pltpu.SideEffectType.DATAFLOW_SIDE_EFFECTING),
    )(*[pltpu.with_memory_space_constraint(a, pltpu.HBM) for a in list(grads) + lands + [carried]])
    return (res[0], res[1], list(res[2:2 + n]), list(res[2 + n:2 + 2 * n])), res[-1]


def _rs_pair_wait(send, recv, grads, lands, after, tag):
    n = len(grads)

    def body(*refs):
        srcs, dsts = refs[:n], refs[n:2 * n]
        send_ref, recv_ref = refs[2 * n], refs[2 * n + 1]
        x, y, c, _ = _place()
        for i in range(n):
            cp = _pair_copy(srcs[i], dsts[i], c, (x, y, 1 - c), send_ref, recv_ref, i)
            cp.wait_send()
            cp.wait_recv()

    res = pl.pallas_call(
        body, name="rs_pair_wait_" + tag,
        in_specs=[HBM] * (2 * n) + [SEM, SEM, pl.BlockSpec(memory_space=pl.ANY)],
        out_specs=[HBM] * (2 * n),
        out_shape=[pltpu.HBM(g.shape, F32) for g in grads] + [pltpu.HBM(l.shape, F32) for l in lands],
        input_output_aliases={i: i for i in range(2 * n)},
        compiler_params=pltpu.CompilerParams(has_side_effects=pltpu.SideEffectType.DATAFLOW_SIDE_EFFECTING),
    )(*grads, *lands, send, recv, after)
    return list(res[:n]), list(res[n:])


def _half_tile(hr):
    return _divtile(hr, 256, 16) if hr % 256 == 0 else _divtile(hr, 512, 16)


def _rs_pair_sum(g, r, c_idx, name):
    ns, rows, cols = g.shape
    hr = rows // 2
    tr = _half_tile(hr)
    nt = hr // tr

    def body(c_ref, g_ref, r_ref, ob_ref, of_ref):
        s = g_ref[...] + r_ref[...]
        ob_ref[...] = s.astype(BF16)
        of_ref[...] = s

    blk = pl.BlockSpec((None, tr, cols), lambda j, t, c_ref: (j, t, 0))
    return pl.pallas_call(
        body, name=name,
        grid_spec=pltpu.PrefetchScalarGridSpec(
            num_scalar_prefetch=1, grid=(ns, nt),
            in_specs=[pl.BlockSpec((None, tr, cols), lambda j, t, c_ref: (j, c_ref[0] * nt + t, 0)), blk],
            out_specs=[blk, blk]),
        out_shape=[jax.ShapeDtypeStruct((ns, hr, cols), BF16), jax.ShapeDtypeStruct((ns, hr, cols), F32)],
        compiler_params=_params(("parallel", "parallel"), 48 << 20),
    )(c_idx, g, r)


def _rs_chip_start(parts, tag):
    n = len(parts)

    def body(*refs):
        send, recv = refs[2 * n], refs[2 * n + 1]
        srcs, dsts = refs[2 * n + 2:3 * n + 2], refs[3 * n + 2:4 * n + 2]
        token = refs[-1]
        x, y, c, peers = _place()
        for i in range(n):
            for k, (px, py) in enumerate(peers):
                pltpu.make_async_remote_copy(
                    src_ref=srcs[i].at[2 * px + py], dst_ref=dsts[i].at[k],
                    send_sem=send.at[3 * i + k], recv_sem=recv.at[3 * i + k],
                    device_id=(px, py, c), device_id_type=MESH).start()
        token[...] = jnp.zeros_like(token)

    lands = [lax.empty((3,) + p.shape[1:], BF16) for p in parts]
    res = pl.pallas_call(
        body, name="rs_chip_start_" + tag,
        in_specs=[HBM] * (2 * n),
        out_specs=[SEM, SEM] + [HBM] * (2 * n) + [pl.BlockSpec(memory_space=pltpu.VMEM)],
        out_shape=[pltpu.SemaphoreType.DMA((3 * n,)), pltpu.SemaphoreType.DMA((3 * n,))]
        + [pltpu.HBM(p.shape, BF16) for p in parts] + [pltpu.HBM(l.shape, BF16) for l in lands]
        + [jax.ShapeDtypeStruct((8, 128), F32)],
        input_output_aliases={i: i + 2 for i in range(2 * n)},
        compiler_params=pltpu.CompilerParams(has_side_effects=pltpu.SideEffectType.DATAFLOW_SIDE_EFFECTING),
    )(*[pltpu.with_memory_space_constraint(a, pltpu.HBM) for a in list(parts) + lands])
    return res[0], res[1], list(res[2:2 + n]), list(res[2 + n:2 + 2 * n]), res[-1]


def _rs_chip_wait(send, recv, parts, lands, after, tag):
    n = len(parts)

    def body(*refs):
        srcs, dsts = refs[:n], refs[n:2 * n]
        send_ref, recv_ref = refs[2 * n], refs[2 * n + 1]
        x, y, c, peers = _place()
        for i in range(n):
            for k, (px, py) in enumerate(peers):
                cp = pltpu.make_async_remote_copy(
                    src_ref=srcs[i].at[2 * px + py], dst_ref=dsts[i].at[k],
                    send_sem=send_ref.at[3 * i + k], recv_sem=recv_ref.at[3 * i + k],
                    device_id=(px, py, c), device_id_type=MESH)
                cp.wait_send()
                cp.wait_recv()

    res = pl.pallas_call(
        body, name="rs_chip_wait_" + tag,
        in_specs=[HBM] * (2 * n) + [SEM, SEM, pl.BlockSpec(memory_space=pl.ANY)],
        out_specs=[HBM] * (2 * n),
        out_shape=[pltpu.HBM(p.shape, BF16) for p in parts] + [pltpu.HBM(l.shape, BF16) for l in lands],
        input_output_aliases={i: i for i in range(2 * n)},
        compiler_params=pltpu.CompilerParams(has_side_effects=pltpu.SideEffectType.DATAFLOW_SIDE_EFFECTING),
    )(*parts, *lands, send, recv, after)
    return list(res[n:])


def _rs_final_sum(own, got, chip_idx, c_idx, name):
    ns, hr, cols = own.shape
    tr = _half_tile(hr)
    nt = hr // tr

    def body(chip_ref, c_ref, o_ref, g_ref, out_ref):
        s = o_ref[...]
        for k in range(3):
            s = s + g_ref[k].astype(F32)
        out_ref[...] = s

    return pl.pallas_call(
        body, name=name,
        grid_spec=pltpu.PrefetchScalarGridSpec(
            num_scalar_prefetch=2, grid=(nt,),
            in_specs=[pl.BlockSpec((None, tr, cols), lambda t, chip_ref, c_ref: (chip_ref[0], t, 0)),
                      pl.BlockSpec((3, tr, cols), lambda t, chip_ref, c_ref: (0, t, 0))],
            out_specs=pl.BlockSpec((tr, cols), lambda t, chip_ref, c_ref: (c_ref[0] * nt + t, 0))),
        out_shape=jax.ShapeDtypeStruct((2 * hr, cols), F32),
        compiler_params=_params(("parallel",), 48 << 20),
    )(chip_idx, c_idx, own, got)


def _rs_share_halves(fulls, tag):
    n = len(fulls)

    def body(*refs):
        dsts = refs[n:2 * n]
        send, recv = refs[2 * n:]
        x, y, c, _ = _place()
        cps = []
        for i in range(n):
            hr = dsts[i].shape[0] // 2
            rows = dsts[i].at[pl.ds(pl.multiple_of(c * hr, 8), hr), :]
            cp = pltpu.make_async_remote_copy(src_ref=rows, dst_ref=rows, send_sem=send.at[i], recv_sem=recv.at[i],
                                              device_id=(x, y, 1 - c), device_id_type=MESH)
            cp.start()
            cps.append(cp)
        for i in range(n):
            hr = dsts[i].shape[0] // 2
            other = dsts[i].at[pl.ds(pl.multiple_of((1 - c) * hr, 8), hr), :]
            pltpu.make_async_remote_copy(src_ref=other, dst_ref=other, send_sem=send.at[i], recv_sem=recv.at[i],
                                         device_id=(x, y, 1 - c), device_id_type=MESH).wait_recv()
        for cp in cps:
            cp.wait_send()

    return pl.pallas_call(
        body, name="rs_share_halves_" + tag, in_specs=[ANY] * n, out_specs=[ANY] * n,
        out_shape=[jax.ShapeDtypeStruct(f.shape, F32) for f in fulls],
        input_output_aliases={i: i for i in range(n)},
        scratch_shapes=[pltpu.SemaphoreType.DMA((n,))] * 2,
    )(*fulls)


def _all_reduce_small(v):
    rows = v.shape[0]

    def body(v_ref, o_ref, buf, send, recv):
        x, y, c, _ = _place()
        me = 4 * x + 2 * y + c
        buf[me] = v_ref[...]
        cps = []
        for d in range(1, 8):
            px, py, pc = x ^ (d >> 2), y ^ ((d >> 1) & 1), c ^ (d & 1)
            cp = pltpu.make_async_remote_copy(src_ref=v_ref, dst_ref=buf.at[me], send_sem=send.at[d - 1],
                                              recv_sem=recv.at[d - 1], device_id=(px, py, pc), device_id_type=MESH)
            cp.start()
            cps.append(cp)
        for d in range(1, 8):
            px, py, pc = x ^ (d >> 2), y ^ ((d >> 1) & 1), c ^ (d & 1)
            pltpu.make_async_remote_copy(src_ref=v_ref, dst_ref=buf.at[4 * px + 2 * py + pc], send_sem=send.at[d - 1],
                                         recv_sem=recv.at[d - 1], device_id=(px, py, pc),
                                         device_id_type=MESH).wait_recv()
        for cp in cps:
            cp.wait_send()
        acc = buf[0]
        for d in range(1, 8):
            acc = acc + buf[d]
        o_ref[...] = acc

    vm = pl.BlockSpec(memory_space=pltpu.VMEM)
    return pl.pallas_call(
        body, name="all_reduce_small", in_specs=[vm], out_specs=vm,
        out_shape=jax.ShapeDtypeStruct((rows, 128), F32),
        scratch_shapes=[pltpu.VMEM((8, rows, 128), F32), pltpu.SemaphoreType.DMA((7,)), pltpu.SemaphoreType.DMA((7,))],
    )(v)


def _adamw(w, g, m, v, name, g_col_blk=0):
    rows, cols = w.shape
    tr = _divtile(rows, max(8, (2 << 20) // (4 * cols) // 8 * 8), 8)

    def body(w_ref, g_ref, m_ref, v_ref, go_ref, d_ref, mo_ref, vo_ref):
        gv = g_ref[...]
        mn = ADAM_B1 * m_ref[...] + (1.0 - ADAM_B1) * gv
        vn = ADAM_B2 * v_ref[...] + (1.0 - ADAM_B2) * (gv * gv)
        m_hat = mn / (1.0 - ADAM_B1 ** ADAM_STEP)
        v_hat = vn / (1.0 - ADAM_B2 ** ADAM_STEP)
        go_ref[...] = gv
        d_ref[...] = -ADAM_LR * (m_hat / (jnp.sqrt(v_hat) + ADAM_EPS) + ADAM_WD * w_ref[...])
        mo_ref[...] = mn
        vo_ref[...] = vn

    blk = pl.BlockSpec((tr, cols), lambda i: (i, 0))
    return pl.pallas_call(
        body, name=name, grid=(rows // tr,),
        in_specs=[blk, pl.BlockSpec((tr, cols), lambda i: (i, g_col_blk)), blk, blk],
        out_specs=[blk] * 4, out_shape=[jax.ShapeDtypeStruct((rows, cols), F32)] * 4,
        compiler_params=_params(("parallel",), 48 << 20),
    )(w, g, m, v)


SMALL = ["ln1_g", "ln1_b", "conv_w", "conv_b", "dt_bias", "a_log", "d_skip", "ssd_norm_g", "attn_sinks",
         "ln2_g", "ln2_b", "ln3_g", "ln3_b"]


def _pack_rows(vs):
    parts = []
    for v in vs:
        v = v.reshape(-1)
        parts.append(jnp.pad(v, (0, (-v.shape[0]) % 128)))
    flat = jnp.concatenate(parts)
    flat = jnp.pad(flat, (0, (-flat.shape[0]) % 1024))
    return flat.reshape(-1, 128)


def _unpack_rows(packed, shapes):
    flat = packed.reshape(-1)
    out, at = [], 0
    for s in shapes:
        nel = int(np.prod(s))
        out.append(flat[at:at + nel].reshape(s))
        at += nel + (-nel) % 128
    return out


def kernel(x, positions, ffn1_w_gate, ffn1_w_up, ffn1_w_down, ln1_g, ln1_b, w_in, conv_w, conv_b, dt_bias, a_log, d_skip, ssd_norm_g, w_ssd_o, attn_sinks, w_attn_o, w_out, ln2_g, ln2_b, ffn2_w_gate, ffn2_w_up, ffn2_w_down, ln3_g, ln3_b, loss_target, m_ffn1_w_gate, m_ffn1_w_up, m_ffn1_w_down, m_ln1_g, m_ln1_b, m_w_in, m_conv_w, m_conv_b, m_dt_bias, m_a_log, m_d_skip, m_ssd_norm_g, m_w_ssd_o, m_attn_sinks, m_w_attn_o, m_w_out, m_ln2_g, m_ln2_b, m_ffn2_w_gate, m_ffn2_w_up, m_ffn2_w_down, m_ln3_g, m_ln3_b, v_ffn1_w_gate, v_ffn1_w_up, v_ffn1_w_down, v_ln1_g, v_ln1_b, v_w_in, v_conv_w, v_conv_b, v_dt_bias, v_a_log, v_d_skip, v_ssd_norm_g, v_w_ssd_o, v_attn_sinks, v_w_attn_o, v_w_out, v_ln2_g, v_ln2_b, v_ffn2_w_gate, v_ffn2_w_up, v_ffn2_w_down, v_ln3_g, v_ln3_b):
    args = dict(locals())
    wts = {n: args[n][0] for n in [b[0] for b in BIG] + SMALL}
    mom_m = {n: args["m_" + n][0] for n in wts}
    mom_v = {n: args["v_" + n][0] for n in wts}
    t = x.shape[1]
    xi, yi, ci = lax.axis_index("x"), lax.axis_index("y"), lax.axis_index("c")
    chip = 2 * xi + yi

    c_idx = ci.astype(jnp.int32).reshape(1)
    chip_idx = chip.astype(jnp.int32).reshape(1)
    placed = {o: _cast_place([wts[b[0]] for b in BIG if b[3] == o], o, chip_idx, also_alone=(o == "gu1"))
              for o in GATHERED}
    placed["gu1"], gu1_own = placed["gu1"]
    g_sems, g_flight = {}, {}

    def fetch(group, after):
        names = GATHER_GROUPS[group]
        send, recv = g_sems[group]
        landed = _gather_ici_wait(send, recv, {k: g_flight[k] for k in names}, names, after, str(group))
        got = _gather_d2d(landed, names, str(group))
        if "win4" in got:
            got["win"] = _win_to_internal(got.pop("win4"))
        return got

    sems, arrays, gu1_own = _gather_ici_start(placed, GATHER_GROUPS, "all", gu1_own)
    g_sems.update(dict(enumerate(sems)))
    g_flight.update(arrays)
    w = {"gu1_own": gu1_own, "chip_idx": chip_idx}

    def slabs_of(gw, names):
        view = {"gu1": lambda: gw["gu1"], "gu2": lambda: gw["gu2"],
                "d1": lambda: gw["d1"].reshape(N_CHIPS, SHARD_H, D), "d2": lambda: gw["d2"].reshape(N_CHIPS, SHARD_H, D),
                "win": lambda: _win_from_internal(gw["win"]),
                "so": lambda: gw["so"].reshape(N_CHIPS, SSD_INNER // N_CHIPS, D),
                "ao": lambda: gw["ao"].reshape(N_CHIPS, D // N_CHIPS, D),
                "out": lambda: gw["out"].reshape(N_CHIPS, D // N_CHIPS, D)}
        return [view[nm]() for nm in names]

    early = ["win", "so", "ao", "out", "gu2", "d2"]
    late = ["gu1", "d1"]
    flight = {}

    def early_start(gw, win):
        flight["pair"], win = _rs_pair_start(slabs_of(gw, early), win, "early")
        return win

    def late_start(gw, gu1w):
        flight["pair_late"], gu1w = _rs_pair_start(slabs_of(gw, late), gu1w, "late")
        return gu1w

    def early_mid(dh1):
        slabs, from_sib = _rs_pair_wait(*flight["pair"], dh1, "early")
        pair = [_rs_pair_sum(g, r, c_idx, "rs_pair_sum_" + nm) for g, r, nm in zip(slabs, from_sib, early)]
        send, recv, parts, lands, token = _rs_chip_start([p[0] for p in pair], "early")
        flight.update(send=send, recv=recv, parts=parts, lands=lands, own=[p[1] for p in pair])
        return token

    early_grads = (early_start, early_mid, late_start)
    cw_rows = _pack_rows([lax.dynamic_update_slice(jnp.zeros((4, XBC), F32), wts["conv_w"], (0, chip * (XBC // N_CHIPS)))])
    cw_rows = jnp.where(ci == 0, cw_rows, 0.0)
    conv_w_full = _all_reduce_small(cw_rows)[:4 * XBC // 128].reshape(4, XBC)

    small = {n: (wts[n][None, :] if wts[n].ndim == 1 else wts[n]) for n in SMALL}
    small["conv_w"] = conv_w_full
    loss, grad_x, gw, gs = _local_step(x[0], positions[0].astype(F32)[:, None], loss_target[0], w, small,
                                       fetch=fetch, early_grads=early_grads)

    gvec = {n: gs[n] for n in SMALL}
    gvec["dt_bias"], gvec["a_log"], gvec["d_skip"] = gs["dt_bias"][:, :64], gs["a_log"][:, :64], gs["d_skip"][:, :64]
    gvec["attn_sinks"] = gs["attn_sinks"][:, :NQ]
    red = _all_reduce_small(_pack_rows([gvec[n] for n in SMALL] + [loss, grad_x[:1, :128]]))
    slabs, from_sib = _rs_pair_wait(*flight["pair_late"], red, "late")
    pair = [_rs_pair_sum(g, r, c_idx, "rs_pair_sum_" + nm) for g, r, nm in zip(slabs, from_sib, late)]
    l_send, l_recv, l_parts, l_lands, l_token = _rs_chip_start([p[0] for p in pair], "late")
    got_early = _rs_chip_wait(flight["send"], flight["recv"], flight["parts"], flight["lands"], l_token, "early")

    outs = {}
    big_src = {"ffn1_w_gate": ("gu1", 0), "ffn1_w_up": ("gu1", 1), "ffn1_w_down": ("d1", 0), "w_in": ("win", 0),
               "w_ssd_o": ("so", 0), "w_attn_o": ("ao", 0), "w_out": ("out", 0),
               "ffn2_w_gate": ("gu2", 0), "ffn2_w_up": ("gu2", 1), "ffn2_w_down": ("d2", 0)}

    def finish(names, own, got, tag):
        halves = [_rs_final_sum(o, gt, chip_idx, c_idx, "rs_final_sum_" + nm) for o, gt, nm in zip(own, got, names)]
        full = dict(zip(names, _rs_share_halves(halves, tag)))
        for nm, (src, blk) in big_src.items():
            if src in full:
                outs[nm] = _adamw(wts[nm], full[src], mom_m[nm], mom_v[nm], "adamw_" + nm, g_col_blk=blk)

    finish(early, flight["own"], got_early, "early")
    early_done = sum(o[1][:1, :1] for o in outs.values())
    got_late = _rs_chip_wait(l_send, l_recv, l_parts, l_lands, early_done, "late")
    finish(late, [p[1] for p in pair], got_late, "late")

    shapes = [(4, XBC) if n == "conv_w" else wts[n].shape for n in SMALL] + [(1,)]
    red_list = _unpack_rows(red, shapes)
    loss_out = red_list[-1].reshape(())
    gsm = dict(zip(SMALL, red_list[:-1]))
    gsm["conv_w"] = lax.dynamic_slice_in_dim(gsm["conv_w"], chip * (XBC // N_CHIPS), XBC // N_CHIPS, axis=1)
    sm_shapes = [wts[n].shape for n in SMALL]
    res = _adamw(_pack_rows([wts[n] for n in SMALL]), _pack_rows([gsm[n] for n in SMALL]),
                 _pack_rows([mom_m[n] for n in SMALL]), _pack_rows([mom_v[n] for n in SMALL]), "adamw_small")
    res = [_unpack_rows(r, sm_shapes) for r in res]
    for i, nm in enumerate(SMALL):
        outs[nm] = tuple(r[i] for r in res)

    order = ["ffn1_w_gate", "ffn1_w_up", "ffn1_w_down", "ln1_g", "ln1_b", "w_in", "conv_w", "conv_b", "dt_bias", "a_log",
             "d_skip", "ssd_norm_g", "w_ssd_o", "attn_sinks", "w_attn_o", "w_out", "ln2_g", "ln2_b",
             "ffn2_w_gate", "ffn2_w_up", "ffn2_w_down", "ln3_g", "ln3_b"]
    result = [loss_out, grad_x[None]]
    for kind in range(4):
        result += [outs[nm][kind][None] for nm in order]
    return tuple(result)
```
